```python
import math
import jax, jax.numpy as jnp
from jax import lax
import numpy as np

D_MODEL = 1024
BATCH = 8
SEQ = 8192
DEPTH = 2

BRANCH_WIDTH = D_MODEL
N_BRANCHES = 3
CHUNK = 128
GMLP_GROUPS = 8
GMLP_GROUP_DIM = BRANCH_WIDTH // GMLP_GROUPS
CONV_WIDTH = 31
XATTN_HEADS = 4
XATTN_HEAD_DIM = BRANCH_WIDTH // XATTN_HEADS
MEM_LEN = 256
OFF_A_U = 0
OFF_A_V = OFF_A_U + BRANCH_WIDTH
OFF_A_G = OFF_A_V + BRANCH_WIDTH
OFF_B_A = OFF_A_G + BRANCH_WIDTH
OFF_B_B = OFF_B_A + BRANCH_WIDTH
OFF_B_G = OFF_B_B + BRANCH_WIDTH
OFF_C_Q = OFF_B_G + BRANCH_WIDTH
OFF_C_G = OFF_C_Q + BRANCH_WIDTH
OFF_MERGE = OFF_C_G + BRANCH_WIDTH
N_IN = OFF_MERGE + N_BRANCHES * D_MODEL
RMS_EPS = 1e-6
LN_EPS = 1e-5

kernel_name = "hybrid_gmlp_conformer_xattn_gated_merge"


def rms_norm(x, g):
    xf = x.astype(jnp.float32)
    y = xf * lax.rsqrt(jnp.mean(xf * xf, axis=-1, keepdims=True) + RMS_EPS)
    return (y * g.astype(jnp.float32)).astype(x.dtype)


def layer_norm(x, g, b):
    xf = x.astype(jnp.float32)
    mu = jnp.mean(xf, axis=-1, keepdims=True)
    xc = xf - mu
    var = jnp.mean(xc * xc, axis=-1, keepdims=True)
    y = xc * lax.rsqrt(var + LN_EPS)
    return (y * g.astype(jnp.float32) + b.astype(jnp.float32)).astype(x.dtype)


def gmlp_spatial_gate(u, v, w_s, b_s):
    bsz, seq, width = v.shape
    n_chunks = seq // CHUNK
    mask = jnp.tril(jnp.ones((CHUNK, CHUNK), dtype=bool))
    ws = jnp.where(mask[None], w_s, 0.0).astype(v.dtype)
    vr = v.reshape(bsz, n_chunks, CHUNK, GMLP_GROUPS, GMLP_GROUP_DIM)
    sv = jnp.einsum('gts,bcsgd->bctgd', ws, vr) + b_s.T.astype(v.dtype)[None, None, :, :, None]
    return u * sv.reshape(bsz, seq, width)


def causal_depthwise_conv(x, w, b):
    k = w.astype(x.dtype)[:, None, :]
    y = lax.conv_general_dilated(
        x, k, window_strides=(1,), padding=[(CONV_WIDTH - 1, 0)],
        dimension_numbers=('NWC', 'WIO', 'NWC'), feature_group_count=x.shape[-1])
    return y + b.astype(x.dtype)


def memory_cross_attention(q, mem_n, w_kv):
    bsz, seq, _ = q.shape
    kv = jnp.einsum('bmd,de->bme', mem_n, w_kv).reshape(bsz, MEM_LEN, 2, XATTN_HEADS, XATTN_HEAD_DIM)
    k, v = kv[:, :, 0], kv[:, :, 1]
    qh = q.reshape(bsz, seq, XATTN_HEADS, XATTN_HEAD_DIM)
    scores = jnp.einsum('bshd,bmhd->bhsm', qh.astype(jnp.float32), k.astype(jnp.float32))
    probs = jax.nn.softmax(scores * (1.0 / math.sqrt(XATTN_HEAD_DIM)), axis=-1).astype(v.dtype)
    out = jnp.einsum('bhsm,bmhd->bshd', probs, v)
    return out.reshape(bsz, seq, BRANCH_WIDTH)


def _fwd_setup_inputs(seed: int = 0) -> dict:
    key = jax.random.key(seed)
    ks = jax.random.split(key, 20)
    f32 = jnp.float32
    nrm = lambda k, shape, scale: jax.random.normal(k, shape, f32) * scale
    return {
        "x": nrm(ks[0], (BATCH, SEQ, D_MODEL), 1.0),
        "mem": nrm(ks[1], (BATCH, MEM_LEN, D_MODEL), 1.0),
        "norm_g": 1.0 + nrm(ks[2], (DEPTH, D_MODEL), 0.02),
        "mem_norm_g": 1.0 + nrm(ks[3], (DEPTH, D_MODEL), 0.02),
        "w_in": nrm(ks[4], (DEPTH, D_MODEL, N_IN), D_MODEL ** -0.5),
        "gmlp_ln_g": 1.0 + nrm(ks[5], (DEPTH, BRANCH_WIDTH), 0.02),
        "gmlp_ln_b": nrm(ks[6], (DEPTH, BRANCH_WIDTH), 0.02),
        "w_s": nrm(ks[7], (DEPTH, GMLP_GROUPS, CHUNK, CHUNK), CHUNK ** -0.5),
        "b_s": 1.0 + nrm(ks[8], (DEPTH, GMLP_GROUPS, CHUNK), 0.02),
        "conv_w": nrm(ks[9], (DEPTH, CONV_WIDTH, BRANCH_WIDTH), CONV_WIDTH ** -0.5),
        "conv_b": nrm(ks[10], (DEPTH, BRANCH_WIDTH), 0.02),
        "conv_ln_g": 1.0 + nrm(ks[11], (DEPTH, BRANCH_WIDTH), 0.02),
        "conv_ln_b": nrm(ks[12], (DEPTH, BRANCH_WIDTH), 0.02),
        "w_kv": nrm(ks[13], (DEPTH, D_MODEL, 2 * BRANCH_WIDTH), D_MODEL ** -0.5),
        "w_branch": nrm(ks[14], (DEPTH, N_BRANCHES, BRANCH_WIDTH, D_MODEL), BRANCH_WIDTH ** -0.5),
        "w_out": nrm(ks[15], (DEPTH, D_MODEL, D_MODEL), D_MODEL ** -0.5),
        "final_norm_g": 1.0 + nrm(ks[16], (D_MODEL,), 0.02),
    }


def _fwd_reference(x, mem, norm_g, mem_norm_g, w_in, gmlp_ln_g, gmlp_ln_b, w_s, b_s,
              conv_w, conv_b, conv_ln_g, conv_ln_b, w_kv, w_branch, w_out, final_norm_g):
    W = BRANCH_WIDTH
    for l in range(DEPTH):
        h = rms_norm(x, norm_g[l])
        z = jnp.einsum('bsd,de->bse', h, w_in[l])

        u = jax.nn.gelu(z[..., OFF_A_U:OFF_A_U + W])
        v = layer_norm(jax.nn.gelu(z[..., OFF_A_V:OFF_A_V + W]), gmlp_ln_g[l], gmlp_ln_b[l])
        br_a = gmlp_spatial_gate(u, v, w_s[l], b_s[l]) * jax.nn.silu(z[..., OFF_A_G:OFF_A_G + W])

        glu = z[..., OFF_B_A:OFF_B_A + W] * jax.nn.sigmoid(z[..., OFF_B_B:OFF_B_B + W])
        c = causal_depthwise_conv(glu, conv_w[l], conv_b[l])
        c = jax.nn.silu(layer_norm(c, conv_ln_g[l], conv_ln_b[l]))
        br_b = c * jax.nn.silu(z[..., OFF_B_G:OFF_B_G + W])

        mem_n = rms_norm(mem, mem_norm_g[l])
        att = memory_cross_attention(z[..., OFF_C_Q:OFF_C_Q + W], mem_n, w_kv[l])
        br_c = att * jax.nn.silu(z[..., OFF_C_G:OFF_C_G + W])

        branches = jnp.stack([br_a, br_b, br_c], axis=2)
        proj = jnp.einsum('bsnw,nwd->bsnd', branches, w_branch[l])
        gates = jax.nn.sigmoid(z[..., OFF_MERGE:OFF_MERGE + N_BRANCHES * D_MODEL]).reshape(
            x.shape[0], x.shape[1], N_BRANCHES, D_MODEL)
        merged = jnp.einsum('bsnd,bsnd->bsd', gates, proj)
        x = x + jnp.einsum('bsd,de->bse', merged, w_out[l])
    return rms_norm(x, final_norm_g)


import jax as _jax
import jax.numpy as _jnp

TWIN_FORMAT = 'train_step'
FWD_PARAMS = ['x', 'mem', 'norm_g', 'mem_norm_g', 'w_in', 'gmlp_ln_g', 'gmlp_ln_b', 'w_s', 'b_s', 'conv_w', 'conv_b', 'conv_ln_g', 'conv_ln_b', 'w_kv', 'w_branch', 'w_out', 'final_norm_g']
TWIN_WEIGHTS = ['norm_g', 'mem_norm_g', 'w_in', 'gmlp_ln_g', 'gmlp_ln_b', 'w_s', 'b_s', 'conv_w', 'conv_b', 'conv_ln_g', 'conv_ln_b', 'w_kv', 'w_branch', 'w_out', 'final_norm_g']
TWIN_DIFF_INPUT = 'x'
TWIN_INPUTS = ['x', 'mem', 'norm_g', 'mem_norm_g', 'w_in', 'gmlp_ln_g', 'gmlp_ln_b', 'w_s', 'b_s', 'conv_w', 'conv_b', 'conv_ln_g', 'conv_ln_b', 'w_kv', 'w_branch', 'w_out', 'final_norm_g', 'loss_target', 'm_norm_g', 'm_mem_norm_g', 'm_w_in', 'm_gmlp_ln_g', 'm_gmlp_ln_b', 'm_w_s', 'm_b_s', 'm_conv_w', 'm_conv_b', 'm_conv_ln_g', 'm_conv_ln_b', 'm_w_kv', 'm_w_branch', 'm_w_out', 'm_final_norm_g', 'v_norm_g', 'v_mem_norm_g', 'v_w_in', 'v_gmlp_ln_g', 'v_gmlp_ln_b', 'v_w_s', 'v_b_s', 'v_conv_w', 'v_conv_b', 'v_conv_ln_g', 'v_conv_ln_b', 'v_w_kv', 'v_w_branch', 'v_w_out', 'v_final_norm_g']
TWIN_OUTPUTS = ['loss', 'grad_x', 'grad_norm_g', 'grad_mem_norm_g', 'grad_w_in', 'grad_gmlp_ln_g', 'grad_gmlp_ln_b', 'grad_w_s', 'grad_b_s', 'grad_conv_w', 'grad_conv_b', 'grad_conv_ln_g', 'grad_conv_ln_b', 'grad_w_kv', 'grad_w_branch', 'grad_w_out', 'grad_final_norm_g', 'delta_norm_g', 'delta_mem_norm_g', 'delta_w_in', 'delta_gmlp_ln_g', 'delta_gmlp_ln_b', 'delta_w_s', 'delta_b_s', 'delta_conv_w', 'delta_conv_b', 'delta_conv_ln_g', 'delta_conv_ln_b', 'delta_w_kv', 'delta_w_branch', 'delta_w_out', 'delta_final_norm_g', 'new_m_norm_g', 'new_m_mem_norm_g', 'new_m_w_in', 'new_m_gmlp_ln_g', 'new_m_gmlp_ln_b', 'new_m_w_s', 'new_m_b_s', 'new_m_conv_w', 'new_m_conv_b', 'new_m_conv_ln_g', 'new_m_conv_ln_b', 'new_m_w_kv', 'new_m_w_branch', 'new_m_w_out', 'new_m_final_norm_g', 'new_v_norm_g', 'new_v_mem_norm_g', 'new_v_w_in', 'new_v_gmlp_ln_g', 'new_v_gmlp_ln_b', 'new_v_w_s', 'new_v_b_s', 'new_v_conv_w', 'new_v_conv_b', 'new_v_conv_ln_g', 'new_v_conv_ln_b', 'new_v_w_kv', 'new_v_w_branch', 'new_v_w_out', 'new_v_final_norm_g']
TWIN_LEAF_KINDS = {'loss': 'loss', 'grad_x': 'grad_x', 'grad_norm_g': 'grad_w', 'grad_mem_norm_g': 'grad_w', 'grad_w_in': 'grad_w', 'grad_gmlp_ln_g': 'grad_w', 'grad_gmlp_ln_b': 'grad_w', 'grad_w_s': 'grad_w', 'grad_b_s': 'grad_w', 'grad_conv_w': 'grad_w', 'grad_conv_b': 'grad_w', 'grad_conv_ln_g': 'grad_w', 'grad_conv_ln_b': 'grad_w', 'grad_w_kv': 'grad_w', 'grad_w_branch': 'grad_w', 'grad_w_out': 'grad_w', 'grad_final_norm_g': 'grad_w', 'delta_norm_g': 'delta_w', 'delta_mem_norm_g': 'delta_w', 'delta_w_in': 'delta_w', 'delta_gmlp_ln_g': 'delta_w', 'delta_gmlp_ln_b': 'delta_w', 'delta_w_s': 'delta_w', 'delta_b_s': 'delta_w', 'delta_conv_w': 'delta_w', 'delta_conv_b': 'delta_w', 'delta_conv_ln_g': 'delta_w', 'delta_conv_ln_b': 'delta_w', 'delta_w_kv': 'delta_w', 'delta_w_branch': 'delta_w', 'delta_w_out': 'delta_w', 'delta_final_norm_g': 'delta_w', 'new_m_norm_g': 'new_m', 'new_m_mem_norm_g': 'new_m', 'new_m_w_in': 'new_m', 'new_m_gmlp_ln_g': 'new_m', 'new_m_gmlp_ln_b': 'new_m', 'new_m_w_s': 'new_m', 'new_m_b_s': 'new_m', 'new_m_conv_w': 'new_m', 'new_m_conv_b': 'new_m', 'new_m_conv_ln_g': 'new_m', 'new_m_conv_ln_b': 'new_m', 'new_m_w_kv': 'new_m', 'new_m_w_branch': 'new_m', 'new_m_w_out': 'new_m', 'new_m_final_norm_g': 'new_m', 'new_v_norm_g': 'new_v', 'new_v_mem_norm_g': 'new_v', 'new_v_w_in': 'new_v', 'new_v_gmlp_ln_g': 'new_v', 'new_v_gmlp_ln_b': 'new_v', 'new_v_w_s': 'new_v', 'new_v_b_s': 'new_v', 'new_v_conv_w': 'new_v', 'new_v_conv_b': 'new_v', 'new_v_conv_ln_g': 'new_v', 'new_v_conv_ln_b': 'new_v', 'new_v_w_kv': 'new_v', 'new_v_w_branch': 'new_v', 'new_v_w_out': 'new_v', 'new_v_final_norm_g': 'new_v'}


def _forward(args):
    return _fwd_reference(*[args[k] for k in FWD_PARAMS])


def _output_shape():
    def fwd():
        inp = _fwd_setup_inputs(0)
        return _fwd_reference(*[inp[k] for k in FWD_PARAMS])
    out = _jax.eval_shape(fwd)
    return out.shape, out.dtype

N_MICROBATCH = 1
ADAM_LR = 0.001
ADAM_B1 = 0.9
ADAM_B2 = 0.999
ADAM_EPS = 1e-08
ADAM_WD = 0.01
ADAM_STEP = 10
PER_EXAMPLE_BATCH_AXIS = {'x': 0, 'mem': 0, 'loss_target': 0}
SHARED_INPUTS = []
_WEIGHT_DTYPES = {'norm_g': _jnp.float32, 'mem_norm_g': _jnp.float32, 'w_in': _jnp.float32, 'gmlp_ln_g': _jnp.float32, 'gmlp_ln_b': _jnp.float32, 'w_s': _jnp.float32, 'b_s': _jnp.float32, 'conv_w': _jnp.float32, 'conv_b': _jnp.float32, 'conv_ln_g': _jnp.float32, 'conv_ln_b': _jnp.float32, 'w_kv': _jnp.float32, 'w_branch': _jnp.float32, 'w_out': _jnp.float32, 'final_norm_g': _jnp.float32}
MOMENT_SCALE = {'norm_g': 1.295719e-01, 'mem_norm_g': 1.238331e-02, 'w_in': 3.840703e-02, 'gmlp_ln_g': 3.613268e-02, 'gmlp_ln_b': 3.781671e-02, 'w_s': 3.609885e-02, 'b_s': 5.228281e-02, 'conv_w': 4.878408e-02, 'conv_b': 1.033280e-01, 'conv_ln_g': 5.704407e-02, 'conv_ln_b': 4.823267e-02, 'w_kv': 8.258092e-03, 'w_branch': 4.558351e-02, 'w_out': 7.906708e-02, 'final_norm_g': 6.403376e+01}


def _to_microbatches(a, axis):
    t = _jnp.moveaxis(a, axis, 0)
    t = t.reshape((N_MICROBATCH, t.shape[0] // N_MICROBATCH) + t.shape[1:])
    return _jnp.moveaxis(t, 1, axis + 1)


def setup_inputs(seed: int = 0) -> dict:
    inp = _fwd_setup_inputs(seed)
    key = _jax.random.fold_in(_jax.random.key(seed), 7919)
    shape, _ = _output_shape()
    out = dict(inp)
    out["loss_target"] = _jax.random.normal(_jax.random.fold_in(key, 0), shape, _jnp.float32)
    for i, name in enumerate(TWIN_WEIGHTS):
        w = inp[name].astype(_jnp.float32)
        if MOMENT_SCALE is None:
            s = _jnp.sqrt(_jnp.mean(_jnp.square(w)) + 1e-30)
        else:
            s = MOMENT_SCALE[name]
        km, kv = _jax.random.split(_jax.random.fold_in(key, i + 1))
        out[name] = w
        out["m_" + name] = s * _jax.random.normal(km, w.shape, _jnp.float32)
        out["v_" + name] = (s * s) * _jax.random.uniform(kv, w.shape, _jnp.float32, 0.5, 1.5)
    if N_MICROBATCH > 1:
        for name, axis in PER_EXAMPLE_BATCH_AXIS.items():
            out[name] = _to_microbatches(out[name], axis)
    return {'x': out['x'], 'mem': out['mem'], 'norm_g': out['norm_g'], 'mem_norm_g': out['mem_norm_g'], 'w_in': out['w_in'], 'gmlp_ln_g': out['gmlp_ln_g'], 'gmlp_ln_b': out['gmlp_ln_b'], 'w_s': out['w_s'], 'b_s': out['b_s'], 'conv_w': out['conv_w'], 'conv_b': out['conv_b'], 'conv_ln_g': out['conv_ln_g'], 'conv_ln_b': out['conv_ln_b'], 'w_kv': out['w_kv'], 'w_branch': out['w_branch'], 'w_out': out['w_out'], 'final_norm_g': out['final_norm_g'], 'loss_target': out['loss_target'], 'm_norm_g': out['m_norm_g'], 'm_mem_norm_g': out['m_mem_norm_g'], 'm_w_in': out['m_w_in'], 'm_gmlp_ln_g': out['m_gmlp_ln_g'], 'm_gmlp_ln_b': out['m_gmlp_ln_b'], 'm_w_s': out['m_w_s'], 'm_b_s': out['m_b_s'], 'm_conv_w': out['m_conv_w'], 'm_conv_b': out['m_conv_b'], 'm_conv_ln_g': out['m_conv_ln_g'], 'm_conv_ln_b': out['m_conv_ln_b'], 'm_w_kv': out['m_w_kv'], 'm_w_branch': out['m_w_branch'], 'm_w_out': out['m_w_out'], 'm_final_norm_g': out['m_final_norm_g'], 'v_norm_g': out['v_norm_g'], 'v_mem_norm_g': out['v_mem_norm_g'], 'v_w_in': out['v_w_in'], 'v_gmlp_ln_g': out['v_gmlp_ln_g'], 'v_gmlp_ln_b': out['v_gmlp_ln_b'], 'v_w_s': out['v_w_s'], 'v_b_s': out['v_b_s'], 'v_conv_w': out['v_conv_w'], 'v_conv_b': out['v_conv_b'], 'v_conv_ln_g': out['v_conv_ln_g'], 'v_conv_ln_b': out['v_conv_ln_b'], 'v_w_kv': out['v_w_kv'], 'v_w_branch': out['v_w_branch'], 'v_w_out': out['v_w_out'], 'v_final_norm_g': out['v_final_norm_g']}


def _loss(weights, diff, rest, loss_target):
    with _jax.named_scope("forward"):
        args = {**rest, TWIN_DIFF_INPUT: diff, **{k: w.astype(_WEIGHT_DTYPES[k]) for k, w in weights.items()}}
        y = _forward(args)
    with _jax.named_scope("loss_head"):
        err = _jnp.square(y.astype(_jnp.float32) - loss_target)
        return 0.5 * _jnp.sum(_jnp.mean(err, axis=-1)) if err.ndim else 0.5 * err


def _adamw(w, g, m, v):
    m = ADAM_B1 * m + (1.0 - ADAM_B1) * g
    v = ADAM_B2 * v + (1.0 - ADAM_B2) * _jnp.square(g)
    m_hat = m / (1.0 - ADAM_B1 ** ADAM_STEP)
    v_hat = v / (1.0 - ADAM_B2 ** ADAM_STEP)
    delta = -ADAM_LR * (m_hat / (_jnp.sqrt(v_hat) + ADAM_EPS) + ADAM_WD * w)
    return delta, m, v


def reference(x, mem, norm_g, mem_norm_g, w_in, gmlp_ln_g, gmlp_ln_b, w_s, b_s, conv_w, conv_b, conv_ln_g, conv_ln_b, w_kv, w_branch, w_out, final_norm_g, loss_target, m_norm_g, m_mem_norm_g, m_w_in, m_gmlp_ln_g, m_gmlp_ln_b, m_w_s, m_b_s, m_conv_w, m_conv_b, m_conv_ln_g, m_conv_ln_b, m_w_kv, m_w_branch, m_w_out, m_final_norm_g, v_norm_g, v_mem_norm_g, v_w_in, v_gmlp_ln_g, v_gmlp_ln_b, v_w_s, v_b_s, v_conv_w, v_conv_b, v_conv_ln_g, v_conv_ln_b, v_w_kv, v_w_branch, v_w_out, v_final_norm_g):
    given = dict(x=x, mem=mem, norm_g=norm_g, mem_norm_g=mem_norm_g, w_in=w_in, gmlp_ln_g=gmlp_ln_g, gmlp_ln_b=gmlp_ln_b, w_s=w_s, b_s=b_s, conv_w=conv_w, conv_b=conv_b, conv_ln_g=conv_ln_g, conv_ln_b=conv_ln_b, w_kv=w_kv, w_branch=w_branch, w_out=w_out, final_norm_g=final_norm_g, loss_target=loss_target, m_norm_g=m_norm_g, m_mem_norm_g=m_mem_norm_g, m_w_in=m_w_in, m_gmlp_ln_g=m_gmlp_ln_g, m_gmlp_ln_b=m_gmlp_ln_b, m_w_s=m_w_s, m_b_s=m_b_s, m_conv_w=m_conv_w, m_conv_b=m_conv_b, m_conv_ln_g=m_conv_ln_g, m_conv_ln_b=m_conv_ln_b, m_w_kv=m_w_kv, m_w_branch=m_w_branch, m_w_out=m_w_out, m_final_norm_g=m_final_norm_g, v_norm_g=v_norm_g, v_mem_norm_g=v_mem_norm_g, v_w_in=v_w_in, v_gmlp_ln_g=v_gmlp_ln_g, v_gmlp_ln_b=v_gmlp_ln_b, v_w_s=v_w_s, v_b_s=v_b_s, v_conv_w=v_conv_w, v_conv_b=v_conv_b, v_conv_ln_g=v_conv_ln_g, v_conv_ln_b=v_conv_ln_b, v_w_kv=v_w_kv, v_w_branch=v_w_branch, v_w_out=v_w_out, v_final_norm_g=v_final_norm_g)
    weights = {n: given[n] for n in TWIN_WEIGHTS}
    shared = {n: given[n] for n in SHARED_INPUTS}
    per_example = {n: given[n] for n in ['x', 'mem']}
    grad_fn = _jax.value_and_grad(_loss, argnums=(0, 1))

    def one_microbatch(ex, loss_target):
        ex = dict(ex)
        diff = ex.pop(TWIN_DIFF_INPUT)
        return grad_fn(weights, diff, {**shared, **ex}, loss_target)

    if N_MICROBATCH == 1:
        loss, (grad_w, grad_x) = one_microbatch(per_example, given["loss_target"])
    else:
        def body(carry, xs):
            loss_sum, grad_sum = carry
            l_k, (gw_k, gx_k) = one_microbatch(xs[0], xs[1])
            with _jax.named_scope("update"):
                return (loss_sum + l_k, _jax.tree.map(_jnp.add, grad_sum, gw_k)), gx_k

        init = (_jnp.zeros((), _jnp.float32), _jax.tree.map(_jnp.zeros_like, weights))
        (loss, grad_w), grad_x = _jax.lax.scan(body, init, (per_example, given["loss_target"]))
    with _jax.named_scope("update"):
        delta_w, new_m, new_v = {}, {}, {}
        for n in TWIN_WEIGHTS:
            delta_w[n], new_m[n], new_v[n] = _adamw(weights[n], grad_w[n], given["m_" + n], given["v_" + n])
    return (loss, grad_x, *[grad_w[n] for n in TWIN_WEIGHTS], *[delta_w[n] for n in TWIN_WEIGHTS],
            *[new_m[n] for n in TWIN_WEIGHTS], *[new_v[n] for n in TWIN_WEIGHTS])
```

```python
import math

import jax
import jax.numpy as jnp
from jax import lax
from jax.experimental import pallas as pl
from jax.experimental.pallas import tpu as pltpu

F32 = jnp.float32
BF16 = jnp.bfloat16

N_LAYERS = 2
N_BRANCHES = 3
N_PLANES = 11
N_CHIPS = 4
CHUNK = 128
GROUPS = 8
HEADS = 4
CONV_K = 31
HALO = 32
LANES = 128
RMS_EPS = 1e-6
LN_EPS = 1e-5
ADAM_LR, ADAM_B1, ADAM_B2, ADAM_EPS, ADAM_WD, ADAM_STEP = 0.001, 0.9, 0.999, 1e-08, 0.01, 10
VMEM_LIMIT_BYTES = 48 * 1024 * 1024
MESH = pl.DeviceIdType.MESH
ANY = pl.BlockSpec(memory_space=pl.ANY)


def _params(*sem):
    return pltpu.CompilerParams(dimension_semantics=sem, vmem_limit_bytes=VMEM_LIMIT_BYTES)


def _tile(n, pref):
    t = min(n, pref)
    assert n % t == 0, (n, t)
    return t


def _dot(a, b):
    return jnp.dot(a, b, preferred_element_type=F32)


def _dot_tb(a, b):
    return lax.dot_general(a, b, (((1,), (1,)), ((), ())), preferred_element_type=F32)


def _dot_ta(a, b):
    return lax.dot_general(a, b, (((0,), (0,)), ((), ())), preferred_element_type=F32)


def _sigmoid(x):
    return 1.0 / (1.0 + jnp.exp(-x))


def _silu(x):
    return x * _sigmoid(x)


def _silu_and_grad(x):
    s = _sigmoid(x)
    return x * s, s * (1.0 + x * (1.0 - s))


_GELU_C = math.sqrt(2.0 / math.pi)
_GELU_A = 0.044715


def _gelu(x):
    return x * (0.5 * (1.0 + jnp.tanh(_GELU_C * (x + _GELU_A * (x * x * x)))))


def _gelu_and_grad(x):
    x2 = x * x
    t = jnp.tanh(_GELU_C * (x + _GELU_A * (x2 * x)))
    cdf = 0.5 * (1.0 + t)
    dcdf = 0.5 * (1.0 - t * t) * (_GELU_C * (1.0 + 3.0 * _GELU_A * x2))
    return x * cdf, cdf + x * dcdf


def _rows(i, n):
    return pl.ds(pl.multiple_of(i * n, n), n)


def _mean(x):
    return jnp.mean(x, axis=-1, keepdims=True)


def _colsum(x):
    return jnp.sum(x, axis=0, keepdims=True)


def _tok(ts, d):
    return pl.BlockSpec((ts, d), lambda i: (i, 0))


def _zcol(ts, d, n):
    return pl.BlockSpec((ts, d), lambda i: (i, n))


def _row1(d):
    return pl.BlockSpec((1, d), lambda i: (0, 0))


def _in_proj(x, g, w_all, l):
    S, D = x.shape
    N = w_all.shape[2]
    tm, tn = _tile(S, 1024), D

    def body(x_ref, g_ref, w_ref, z_ref, hb_ref, h_scr):
        @pl.when(pl.program_id(1) == 0)
        def _():
            def slab(i, c):
                r = _rows(i, 16)
                xs = x_ref[r, :]
                h = (xs * lax.rsqrt(_mean(xs * xs) + RMS_EPS)) * g_ref[...]
                hb = h.astype(BF16)
                h_scr[r, :] = hb
                hb_ref[r, :] = hb
                return c
            lax.fori_loop(0, tm // 16, slab, 0)
        z_ref[...] = _dot(h_scr[...], w_ref[...])

    return pl.pallas_call(
        body, name="in_proj", grid=(S // tm, N // tn),
        in_specs=[pl.BlockSpec((tm, D), lambda i, j: (i, 0)),
                  pl.BlockSpec((1, D), lambda i, j: (0, 0)),
                  pl.BlockSpec((None, D, tn), lambda i, j: (l, 0, j))],
        out_specs=[pl.BlockSpec((tm, tn), lambda i, j: (i, j)),
                   pl.BlockSpec((tm, D), lambda i, j: (i, 0))],
        out_shape=[jax.ShapeDtypeStruct((S, N), F32), jax.ShapeDtypeStruct((S, D), BF16)],
        scratch_shapes=[pltpu.VMEM((tm, D), BF16)],
        compiler_params=_params("parallel", "arbitrary"),
    )(x, g, w_all)


def _branch_a_fwd(z, lng, lnb, t_mix, bias_full):
    S = z.shape[0]
    D = lng.shape[1]
    GD = D // GROUPS
    ts = _tile(S, 256)

    def body(zu_ref, zv_ref, zg_ref, lng_ref, lnb_ref, t_ref, bias_ref, a_ref, v_scr):
        def chunk(ci, carry):
            c0 = pl.multiple_of(ci * CHUNK, CHUNK)

            def slab(si, c):
                r = pl.multiple_of(si * 16, 16)
                vg = _gelu(zv_ref[pl.ds(pl.multiple_of(c0 + r, 16), 16), :])
                xc = vg - _mean(vg)
                y = xc * lax.rsqrt(_mean(xc * xc) + LN_EPS)
                v_scr[pl.ds(r, 16), :] = (y * lng_ref[...] + lnb_ref[...]).astype(BF16)
                return c
            lax.fori_loop(0, CHUNK // 16, slab, 0)
            for g in range(GROUPS):
                cols = slice(g * GD, (g + 1) * GD)
                sv = _dot(t_ref[g], v_scr[:, cols]) + bias_ref[:, cols]
                u = _gelu(zu_ref[pl.ds(c0, CHUNK), cols])
                a_ref[pl.ds(c0, CHUNK), cols] = ((u * sv) * _silu(zg_ref[pl.ds(c0, CHUNK), cols])).astype(BF16)
            return carry
        lax.fori_loop(0, ts // CHUNK, chunk, 0)

    return pl.pallas_call(
        body, name="branch_a_fwd", grid=(S // ts,),
        in_specs=[_zcol(ts, D, 0), _zcol(ts, D, 1), _zcol(ts, D, 2), _row1(D), _row1(D),
                  pl.BlockSpec((GROUPS, CHUNK, CHUNK), lambda i: (0, 0, 0)),
                  pl.BlockSpec((CHUNK, D), lambda i: (0, 0))],
        out_specs=_tok(ts, D),
        out_shape=jax.ShapeDtypeStruct((S, D), BF16),
        scratch_shapes=[pltpu.VMEM((CHUNK, D), BF16)],
        compiler_params=_params("parallel"),
    )(z, z, z, lng, lnb, t_mix, bias_full)


def _conv_fwd(z, w_all, cb, l):
    S = z.shape[0]
    D = cb.shape[1]
    ts = _tile(S, 256)
    CB = 256
    hb = ts // HALO

    def body(za_ref, zb_ref, zap_ref, zbp_ref, w_ref, b_ref, c0_ref, glu_scr):
        i = pl.program_id(0)
        prev = zap_ref[...] * _sigmoid(zbp_ref[...])
        glu_scr[0:HALO, :] = jnp.where(i == 0, 0.0, prev)

        def fill(si, c):
            r = _rows(si, 32)
            glu_scr[pl.ds(pl.multiple_of(HALO + si * 32, 32), 32), :] = za_ref[r, :] * _sigmoid(zb_ref[r, :])
            return c
        lax.fori_loop(0, ts // 32, fill, 0)

        def colblock(ci, c):
            cols = _rows(ci, CB)
            for s in range(ts // 32):
                acc = jnp.zeros((32, CB), F32)
                for k in range(CONV_K):
                    o = s * 32 + k + (HALO - CONV_K + 1)
                    acc = acc + glu_scr[pl.ds(o, 32), cols] * w_ref[k:k + 1, cols]
                c0_ref[pl.ds(s * 32, 32), cols] = acc + b_ref[:, cols]
            return c
        lax.fori_loop(0, D // CB, colblock, 0)

    prev_map = lambda n: (lambda i: (jnp.maximum(i * hb - 1, 0), n))
    return pl.pallas_call(
        body, name="conv_fwd", grid=(S // ts,),
        in_specs=[_zcol(ts, D, 3), _zcol(ts, D, 4),
                  pl.BlockSpec((HALO, D), prev_map(3)), pl.BlockSpec((HALO, D), prev_map(4)),
                  pl.BlockSpec((None, CONV_K, D), lambda i: (l, 0, 0)), _row1(D)],
        out_specs=_tok(ts, D),
        out_shape=jax.ShapeDtypeStruct((S, D), F32),
        scratch_shapes=[pltpu.VMEM((ts + HALO, D), F32)],
        compiler_params=_params("parallel"),
    )(z, z, z, z, w_all, cb)


def _branch_b_fwd(c0, z, g, b):
    S, D = c0.shape
    ts = _tile(S, 512)

    def body(c0_ref, zg_ref, g_ref, b_ref, o_ref):
        def slab(si, c):
            r = _rows(si, 16)
            v = c0_ref[r, :]
            xc = v - _mean(v)
            c1 = (xc * lax.rsqrt(_mean(xc * xc) + LN_EPS)) * g_ref[...] + b_ref[...]
            o_ref[r, :] = (_silu(c1) * _silu(zg_ref[r, :])).astype(BF16)
            return c
        lax.fori_loop(0, ts // 16, slab, 0)

    return pl.pallas_call(
        body, name="branch_b_fwd", grid=(S // ts,),
        in_specs=[_tok(ts, D), _zcol(ts, D, 5), _row1(D), _row1(D)],
        out_specs=_tok(ts, D),
        out_shape=jax.ShapeDtypeStruct((S, D), BF16),
        compiler_params=_params("parallel"),
    )(c0, z, g, b)


def _kv_fwd(mem, mg, wkv_all, l):
    M, D = mem.shape

    def body(mem_ref, g_ref, w_ref, mn_ref, kv_ref, kvt_ref):
        m = mem_ref[...]
        mn = ((m * lax.rsqrt(_mean(m * m) + RMS_EPS)) * g_ref[...]).astype(BF16)
        mn_ref[...] = mn
        kv = _dot(mn, w_ref[...])
        kv_ref[...] = kv.astype(BF16)
        kvt_ref[...] = kv.T.astype(BF16)

    return pl.pallas_call(
        body, name="kv_fwd", grid=(1,),
        in_specs=[pl.BlockSpec((M, D), lambda i: (0, 0)), _row1(D),
                  pl.BlockSpec((None, D, 2 * D), lambda i: (l, 0, 0))],
        out_specs=[pl.BlockSpec((M, D), lambda i: (0, 0)), pl.BlockSpec((M, 2 * D), lambda i: (0, 0)),
                   pl.BlockSpec((2 * D, M), lambda i: (0, 0))],
        out_shape=[jax.ShapeDtypeStruct((M, D), BF16), jax.ShapeDtypeStruct((M, 2 * D), BF16),
                   jax.ShapeDtypeStruct((2 * D, M), BF16)],
        compiler_params=_params("arbitrary"),
    )(mem, mg, wkv_all)


def _softmax_rows(s):
    e = jnp.exp(s - jnp.max(s, axis=-1, keepdims=True))
    return e / jnp.sum(e, axis=-1, keepdims=True)


def _branch_c_fwd(z, kv, kvt):
    S = z.shape[0]
    M, D2 = kv.shape
    D = D2 // 2
    HD = D // HEADS
    scale = 1.0 / math.sqrt(HD)
    ts = _tile(S, 256)

    def body(q_ref, zg_ref, kt_ref, v_ref, o_ref):
        for h in range(HEADS):
            cols = slice(h * HD, (h + 1) * HD)
            p = _softmax_rows(_dot(q_ref[:, cols].astype(BF16), kt_ref[cols, :]) * scale)
            att = _dot(p.astype(BF16), v_ref[:, cols])
            o_ref[:, cols] = (att * _silu(zg_ref[:, cols])).astype(BF16)

    return pl.pallas_call(
        body, name="branch_c_fwd", grid=(S // ts,),
        in_specs=[_zcol(ts, D, 6), _zcol(ts, D, 7),
                  pl.BlockSpec((D, M), lambda i: (0, 0)), pl.BlockSpec((M, D), lambda i: (0, 1))],
        out_specs=_tok(ts, D),
        out_shape=jax.ShapeDtypeStruct((S, D), BF16),
        compiler_params=_params("parallel"),
    )(z, z, kvt, kv)


def _merge_fwd(bra, brb, brc, z, x, wb_all, wo_all, l):
    S, D = x.shape
    ts = _tile(S, 256)
    CB = 256

    def body(a_ref, b_ref, c_ref, zm0_ref, zm1_ref, zm2_ref, x_ref, wb_ref, wo_ref, m_ref, xo_ref):
        brs = (a_ref, b_ref, c_ref)
        zms = (zm0_ref, zm1_ref, zm2_ref)
        for cb in range(D // CB):
            cols = slice(cb * CB, (cb + 1) * CB)
            acc = None
            for n in range(N_BRANCHES):
                t = _sigmoid(zms[n][:, cols]) * _dot(brs[n][...], wb_ref[n, :, cols])
                acc = t if acc is None else acc + t
            m_ref[:, cols] = acc.astype(BF16)
        for cb in range(D // CB):
            cols = slice(cb * CB, (cb + 1) * CB)
            xo_ref[:, cols] = x_ref[:, cols] + _dot(m_ref[...], wo_ref[:, cols])

    return pl.pallas_call(
        body, name="merge_fwd", grid=(S // ts,),
        in_specs=[_tok(ts, D), _tok(ts, D), _tok(ts, D), _zcol(ts, D, 8), _zcol(ts, D, 9), _zcol(ts, D, 10),
                  _tok(ts, D),
                  pl.BlockSpec((None, N_BRANCHES, D, D), lambda i: (l, 0, 0, 0)),
                  pl.BlockSpec((None, D, D), lambda i: (l, 0, 0))],
        out_specs=[_tok(ts, D), _tok(ts, D)],
        out_shape=[jax.ShapeDtypeStruct((S, D), BF16), jax.ShapeDtypeStruct((S, D), F32)],
        compiler_params=_params("parallel"),
    )(bra, brb, brc, z, z, z, x, wb_all, wo_all)


def _loss_bwd(x, target, fg):
    S, D = x.shape
    ts = _tile(S, 512)

    def body(x_ref, t_ref, g_ref, dx_ref, ls_ref, dg_ref):
        @pl.when(pl.program_id(0) == 0)
        def _():
            ls_ref[...] = jnp.zeros_like(ls_ref)
            dg_ref[...] = jnp.zeros_like(dg_ref)

        def slab(si, c):
            r = _rows(si, 16)
            xs = x_ref[r, :]
            rs = lax.rsqrt(_mean(xs * xs) + RMS_EPS)
            n = xs * rs
            e = n * g_ref[...] - t_ref[r, :]
            ls_ref[...] += _colsum(e * e)
            dy = e * (1.0 / D)
            dg_ref[...] += _colsum(dy * n)
            dn = dy * g_ref[...]
            dx_ref[r, :] = rs * (dn - n * _mean(dn * n))
            return c
        lax.fori_loop(0, ts // 16, slab, 0)

    return pl.pallas_call(
        body, name="loss_bwd", grid=(S // ts,),
        in_specs=[_tok(ts, D), _tok(ts, D), _row1(D)],
        out_specs=[_tok(ts, D), _row1(D), _row1(D)],
        out_shape=[jax.ShapeDtypeStruct((S, D), F32), jax.ShapeDtypeStruct((1, D), F32),
                   jax.ShapeDtypeStruct((1, D), F32)],
        compiler_params=_params("arbitrary"),
    )(x, target, fg)


def _merge_bwd(dxp, bra, brb, brc, z, wb_all, wo_all, l):
    S, D = dxp.shape
    ts = _tile(S, 256)
    CB = 256

    def body(d_ref, a_ref, b_ref, c_ref, zm0_ref, zm1_ref, zm2_ref, wb_ref, wo_ref,
             dzm_ref, dproj_ref, dbr_ref, dxb_scr):
        brs = (a_ref, b_ref, c_ref)
        zms = (zm0_ref, zm1_ref, zm2_ref)
        dxb_scr[...] = d_ref[...].astype(BF16)
        for cb in range(D // CB):
            cols = slice(cb * CB, (cb + 1) * CB)
            dm = _dot_tb(dxb_scr[...], wo_ref[cols, :])
            for n in range(N_BRANCHES):
                proj = _dot(brs[n][...], wb_ref[n, :, cols])
                g = _sigmoid(zms[n][:, cols])
                dzm_ref[:, n * D + cb * CB:n * D + (cb + 1) * CB] = ((dm * proj) * (g * (1.0 - g))).astype(BF16)
                dproj_ref[n, :, cols] = (dm * g).astype(BF16)
        for n in range(N_BRANCHES):
            for cb in range(D // CB):
                cols = slice(cb * CB, (cb + 1) * CB)
                dbr_ref[n, :, cols] = _dot_tb(dproj_ref[n], wb_ref[n, cols, :])

    tok3 = lambda: pl.BlockSpec((N_BRANCHES, ts, D), lambda i: (0, i, 0))
    return pl.pallas_call(
        body, name="merge_bwd", grid=(S // ts,),
        in_specs=[_tok(ts, D), _tok(ts, D), _tok(ts, D), _tok(ts, D),
                  _zcol(ts, D, 8), _zcol(ts, D, 9), _zcol(ts, D, 10),
                  pl.BlockSpec((None, N_BRANCHES, D, D), lambda i: (l, 0, 0, 0)),
                  pl.BlockSpec((None, D, D), lambda i: (l, 0, 0))],
        out_specs=[_tok(ts, 3 * D), tok3(), tok3()],
        out_shape=[jax.ShapeDtypeStruct((S, 3 * D), BF16), jax.ShapeDtypeStruct((N_BRANCHES, S, D), BF16),
                   jax.ShapeDtypeStruct((N_BRANCHES, S, D), F32)],
        scratch_shapes=[pltpu.VMEM((ts, D), BF16)],
        compiler_params=_params("parallel"),
    )(dxp, bra, brb, brc, z, z, z, wb_all, wo_all)


def _branch_a_bwd(z, dbr, lng, lnb, t_mix, t_mix_t, bias_full, mask):
    S = z.shape[0]
    D = lng.shape[1]
    GD = D // GROUPS
    ts = _tile(S, 256)
    nt = S // ts

    def body(zu_ref, zv_ref, zg_ref, d_ref, lng_ref, lnb_ref, t_ref, tt_ref, bias_ref, mask_ref,
             dz_ref, dws_ref, dbs_ref, dlg_ref, dlb_ref,
             v_scr, vn_scr, rstd_scr, dv_scr, dsv_acc):
        i = pl.program_id(0)

        @pl.when(i == 0)
        def _():
            dws_ref[...] = jnp.zeros_like(dws_ref)
            dlg_ref[...] = jnp.zeros_like(dlg_ref)
            dlb_ref[...] = jnp.zeros_like(dlb_ref)
            dsv_acc[...] = jnp.zeros_like(dsv_acc)

        def chunk(ci, carry):
            c0 = pl.multiple_of(ci * CHUNK, CHUNK)

            def ln_fwd(si, c):
                r = pl.multiple_of(si * 16, 16)
                vg = _gelu(zv_ref[pl.ds(pl.multiple_of(c0 + r, 16), 16), :])
                xc = vg - _mean(vg)
                rstd = lax.rsqrt(_mean(xc * xc) + LN_EPS)
                vn = xc * rstd
                vn_scr[pl.ds(r, 16), :] = vn
                rstd_scr[pl.ds(r, 16), :] = rstd
                v_scr[pl.ds(r, 16), :] = (vn * lng_ref[...] + lnb_ref[...]).astype(BF16)
                return c
            lax.fori_loop(0, CHUNK // 16, ln_fwd, 0)

            for g in range(GROUPS):
                cols = slice(g * GD, (g + 1) * GD)
                rows = pl.ds(c0, CHUNK)
                vb = v_scr[:, cols]
                sv = _dot(t_ref[g], vb) + bias_ref[:, cols]
                u, du = _gelu_and_grad(zu_ref[rows, cols])
                sg, dsg = _silu_and_grad(zg_ref[rows, cols])
                d = d_ref[rows, cols]
                dsv = (d * u) * sg
                dz_ref[rows, g * GD:(g + 1) * GD] = (((d * sv) * sg) * du).astype(BF16)
                dz_ref[rows, 2 * D + g * GD:2 * D + (g + 1) * GD] = (((d * u) * sv) * dsg).astype(BF16)
                dsvb = dsv.astype(BF16)
                dws_ref[g] += _dot_tb(dsvb, vb)
                dv_scr[:, cols] = _dot(tt_ref[g], dsvb)
                dsv_acc[:, cols] += dsv

            def ln_bwd(si, c):
                r = pl.multiple_of(si * 16, 16)
                rr = pl.ds(r, 16)
                zrows = pl.ds(pl.multiple_of(c0 + r, 16), 16)
                dv = dv_scr[rr, :]
                vn = vn_scr[rr, :]
                dlg_ref[...] += _colsum(dv * vn)
                dlb_ref[...] += _colsum(dv)
                dvn = dv * lng_ref[...]
                dvg = rstd_scr[rr, :] * ((dvn - _mean(dvn)) - vn * _mean(dvn * vn))
                _, gg = _gelu_and_grad(zv_ref[zrows, :])
                dz_ref[zrows, D:2 * D] = (dvg * gg).astype(BF16)
                return c
            lax.fori_loop(0, CHUNK // 16, ln_bwd, 0)
            return carry
        lax.fori_loop(0, ts // CHUNK, chunk, 0)

        @pl.when(i == nt - 1)
        def _():
            for g in range(GROUPS):
                cols = slice(g * GD, (g + 1) * GD)
                dws_ref[g] = dws_ref[g] * mask_ref[...]
                dbs_ref[g:g + 1, :] = _colsum(dsv_acc[:, cols].T)

    c2 = lambda shape: pl.BlockSpec(shape, lambda i: (0, 0))
    c3 = lambda: pl.BlockSpec((GROUPS, CHUNK, CHUNK), lambda i: (0, 0, 0))
    return pl.pallas_call(
        body, name="branch_a_bwd", grid=(nt,),
        in_specs=[_zcol(ts, D, 0), _zcol(ts, D, 1), _zcol(ts, D, 2), _tok(ts, D), _row1(D), _row1(D),
                  c3(), c3(), c2((CHUNK, D)), c2((CHUNK, CHUNK))],
        out_specs=[_tok(ts, 3 * D), c3(), c2((GROUPS, CHUNK)), _row1(D), _row1(D)],
        out_shape=[jax.ShapeDtypeStruct((S, 3 * D), BF16), jax.ShapeDtypeStruct((GROUPS, CHUNK, CHUNK), F32),
                   jax.ShapeDtypeStruct((GROUPS, CHUNK), F32), jax.ShapeDtypeStruct((1, D), F32),
                   jax.ShapeDtypeStruct((1, D), F32)],
        scratch_shapes=[pltpu.VMEM((CHUNK, D), BF16), pltpu.VMEM((CHUNK, D), F32), pltpu.VMEM((CHUNK, 1), F32),
                        pltpu.VMEM((CHUNK, D), F32), pltpu.VMEM((CHUNK, D), F32)],
        compiler_params=_params("arbitrary"),
    )(z, z, z, dbr, lng, lnb, t_mix, t_mix_t, bias_full, mask)


def _branch_b_bwd(c0, z, dbr, g, b):
    S, D = c0.shape
    ts = _tile(S, 512)

    def body(c0_ref, zg_ref, d_ref, g_ref, b_ref, dc0_ref, dz_ref, dg_ref, db_ref, dcb_ref):
        @pl.when(pl.program_id(0) == 0)
        def _():
            dg_ref[...] = jnp.zeros_like(dg_ref)
            db_ref[...] = jnp.zeros_like(db_ref)
            dcb_ref[...] = jnp.zeros_like(dcb_ref)

        def slab(si, c):
            r = _rows(si, 16)
            v = c0_ref[r, :]
            xc = v - _mean(v)
            rstd = lax.rsqrt(_mean(xc * xc) + LN_EPS)
            cn = xc * rstd
            c1 = cn * g_ref[...] + b_ref[...]
            c2, dc2_dc1 = _silu_and_grad(c1)
            sg, dsg = _silu_and_grad(zg_ref[r, :])
            d = d_ref[r, :]
            dz_ref[r, :] = ((d * c2) * dsg).astype(BF16)
            dc1 = (d * sg) * dc2_dc1
            dg_ref[...] += _colsum(dc1 * cn)
            db_ref[...] += _colsum(dc1)
            dcn = dc1 * g_ref[...]
            dc0 = rstd * ((dcn - _mean(dcn)) - cn * _mean(dcn * cn))
            dcb_ref[...] += _colsum(dc0)
            dc0_ref[r, :] = dc0
            return c
        lax.fori_loop(0, ts // 16, slab, 0)

    return pl.pallas_call(
        body, name="branch_b_bwd", grid=(S // ts,),
        in_specs=[_tok(ts, D), _zcol(ts, D, 5), _tok(ts, D), _row1(D), _row1(D)],
        out_specs=[_tok(ts, D), _zcol(ts, D, 2), _row1(D), _row1(D), _row1(D)],
        out_shape=[jax.ShapeDtypeStruct((S, D), F32), jax.ShapeDtypeStruct((S, 3 * D), BF16),
                   jax.ShapeDtypeStruct((1, D), F32), jax.ShapeDtypeStruct((1, D), F32),
                   jax.ShapeDtypeStruct((1, D), F32)],
        compiler_params=_params("arbitrary"),
    )(c0, z, dbr, g, b)


def _conv_bwd(dc0, z, w_all, dz_b, l):
    S, D = dc0.shape
    ts = _tile(S, 256)
    nt = S // ts
    CB = 256
    hb = ts // HALO
    nh = S // HALO
    SH = HALO - CONV_K + 1

    def body(dc_ref, dcn_ref, za_ref, zb_ref, zap_ref, zbp_ref, w_ref, dzin_ref,
             dz_ref, dw_ref, glu_scr, dc_scr, dw_acc):
        del dzin_ref
        i = pl.program_id(0)

        @pl.when(i == 0)
        def _():
            dw_acc[...] = jnp.zeros_like(dw_acc)

        prev = zap_ref[...] * _sigmoid(zbp_ref[...])
        glu_scr[0:HALO, :] = jnp.where(i == 0, 0.0, prev)
        dc_scr[ts:ts + HALO, :] = jnp.where(i == nt - 1, 0.0, dcn_ref[...])

        def fill(si, c):
            r = _rows(si, 32)
            glu_scr[pl.ds(pl.multiple_of(HALO + si * 32, 32), 32), :] = za_ref[r, :] * _sigmoid(zb_ref[r, :])
            dc_scr[r, :] = dc_ref[r, :]
            return c
        lax.fori_loop(0, ts // 32, fill, 0)

        def colblock(ci, c):
            cols = _rows(ci, CB)
            cols_b = pl.ds(pl.multiple_of(D + ci * CB, CB), CB)
            for s in range(ts // 32):
                rows = pl.ds(s * 32, 32)
                dc = dc_scr[rows, cols]
                acc = jnp.zeros((32, CB), F32)
                for k in range(CONV_K):
                    prod = dc * glu_scr[pl.ds(s * 32 + k + SH, 32), cols]
                    dw_acc[k, :, cols] += (prod[0:8] + prod[8:16]) + (prod[16:24] + prod[24:32])
                    acc = acc + dc_scr[pl.ds(s * 32 + k, 32), cols] * w_ref[CONV_K - 1 - k:CONV_K - k, cols]
                sg = _sigmoid(zb_ref[rows, cols])
                dz_ref[rows, cols] = (acc * sg).astype(BF16)
                dz_ref[rows, cols_b] = ((acc * za_ref[rows, cols]) * (sg * (1.0 - sg))).astype(BF16)
            return c
        lax.fori_loop(0, D // CB, colblock, 0)

        @pl.when(i == nt - 1)
        def _():
            for k in range(CONV_K):
                dw_ref[k:k + 1, :] = _colsum(dw_acc[k])

    prev_map = lambda n: (lambda i: (jnp.maximum(i * hb - 1, 0), n))
    return pl.pallas_call(
        body, name="conv_bwd", grid=(nt,),
        in_specs=[_tok(ts, D), pl.BlockSpec((HALO, D), lambda i: (jnp.minimum((i + 1) * hb, nh - 1), 0)),
                  _zcol(ts, D, 3), _zcol(ts, D, 4),
                  pl.BlockSpec((HALO, D), prev_map(3)), pl.BlockSpec((HALO, D), prev_map(4)),
                  pl.BlockSpec((None, CONV_K, D), lambda i: (l, 0, 0)), ANY],
        out_specs=[_tok(ts, 2 * D), pl.BlockSpec((CONV_K, D), lambda i: (0, 0))],
        out_shape=[jax.ShapeDtypeStruct((S, 3 * D), BF16), jax.ShapeDtypeStruct((CONV_K, D), F32)],
        scratch_shapes=[pltpu.VMEM((ts + HALO, D), F32), pltpu.VMEM((ts + HALO, D), F32),
                        pltpu.VMEM((CONV_K, 8, D), F32)],
        input_output_aliases={7: 0},
        compiler_params=_params("arbitrary"),
    )(dc0, dc0, z, z, z, z, w_all, dz_b)


def _branch_c_bwd(z, dbr, kv, kvt):
    S = z.shape[0]
    M, D2 = kv.shape
    D = D2 // 2
    HD = D // HEADS
    scale = 1.0 / math.sqrt(HD)
    ts = _tile(S, 256)

    def body(q_ref, zg_ref, d_ref, kt_ref, vt_ref, k_ref, v_ref, dz_ref, dk_ref, dv_ref):
        @pl.when(pl.program_id(0) == 0)
        def _():
            dk_ref[...] = jnp.zeros_like(dk_ref)
            dv_ref[...] = jnp.zeros_like(dv_ref)

        for h in range(HEADS):
            cols = slice(h * HD, (h + 1) * HD)
            qb = q_ref[:, cols].astype(BF16)
            p = _softmax_rows(_dot(qb, kt_ref[cols, :]) * scale)
            pb = p.astype(BF16)
            att = _dot(pb, v_ref[:, cols])
            sg, dsg = _silu_and_grad(zg_ref[:, cols])
            d = d_ref[:, cols]
            dz_ref[:, D + h * HD:D + (h + 1) * HD] = ((d * att) * dsg).astype(BF16)
            datt = (d * sg).astype(BF16)
            dp = _dot(datt, vt_ref[cols, :])
            dv_ref[:, cols] += _dot_ta(pb, datt)
            ds = ((p * (dp - jnp.sum(dp * p, axis=-1, keepdims=True))) * scale).astype(BF16)
            dz_ref[:, cols] = _dot(ds, k_ref[:, cols]).astype(BF16)
            dk_ref[:, cols] += _dot_ta(ds, qb)

    return pl.pallas_call(
        body, name="branch_c_bwd", grid=(S // ts,),
        in_specs=[_zcol(ts, D, 6), _zcol(ts, D, 7), _tok(ts, D),
                  pl.BlockSpec((D, M), lambda i: (0, 0)), pl.BlockSpec((D, M), lambda i: (1, 0)),
                  pl.BlockSpec((M, D), lambda i: (0, 0)), pl.BlockSpec((M, D), lambda i: (0, 1))],
        out_specs=[_tok(ts, 2 * D), pl.BlockSpec((M, D), lambda i: (0, 0)), pl.BlockSpec((M, D), lambda i: (0, 0))],
        out_shape=[jax.ShapeDtypeStruct((S, 2 * D), BF16), jax.ShapeDtypeStruct((M, D), F32),
                   jax.ShapeDtypeStruct((M, D), F32)],
        compiler_params=_params("arbitrary"),
    )(z, z, dbr, kvt, kvt, kv, kv)


def _kv_bwd(dk, dv, mn, mem, wkv_all, l, prev):
    M, D = mem.shape

    def body(*refs):
        dk_ref, dv_ref, mn_ref, mem_ref, w_ref = refs[:5]
        dw_ref, dg_ref = refs[-2:]
        dkb = dk_ref[...].astype(BF16)
        dvb = dv_ref[...].astype(BF16)
        dw_ref[:, 0:D] = _dot_ta(mn_ref[...], dkb)
        dw_ref[:, D:2 * D] = _dot_ta(mn_ref[...], dvb)
        dmn = _dot_tb(dkb, w_ref[:, 0:D]) + _dot_tb(dvb, w_ref[:, D:2 * D])
        m = mem_ref[...]
        dg_ref[...] = _colsum(dmn * (m * lax.rsqrt(_mean(m * m) + RMS_EPS)))

    full = lambda shape: pl.BlockSpec(shape, lambda i: (0, 0))
    in_specs = [full((M, D)), full((M, D)), full((M, D)), full((M, D)),
                pl.BlockSpec((None, D, 2 * D), lambda i: (l, 0, 0))]
    args = [dk, dv, mn, mem, wkv_all]
    aliases = {}
    if prev is not None:
        in_specs.append(ANY)
        args.append(prev)
        aliases = {5: 0}
    return pl.pallas_call(
        body, name="kv_bwd", grid=(1,),
        in_specs=in_specs,
        out_specs=[pl.BlockSpec((None, D, 2 * D), lambda i: (l, 0, 0)), _row1(D)],
        out_shape=[jax.ShapeDtypeStruct((N_LAYERS, D, 2 * D), F32), jax.ShapeDtypeStruct((1, D), F32)],
        input_output_aliases=aliases,
        compiler_params=_params("arbitrary"),
    )(*args)


def _in_proj_bwd(pieces, w_all, x, g, dxp, l):
    S, D = x.shape
    tm = _tile(S, 512)
    n_p = len(pieces)

    def body(*refs):
        dz_refs = refs[:n_p]
        w_ref, x_ref, g_ref, dxp_ref, dx_ref, dg_ref, acc = refs[n_p:]
        i = pl.program_id(0)
        k = pl.program_id(1)

        @pl.when(k == 0)
        def _():
            acc[...] = jnp.zeros_like(acc)

        @pl.when((i == 0) & (k == 0))
        def _():
            dg_ref[...] = jnp.zeros_like(dg_ref)

        for (_, lo, n), r in zip(pieces, dz_refs):
            @pl.when((k >= lo) & (k < lo + n))
            def _(r=r):
                acc[...] += _dot_tb(r[...], w_ref[...])

        @pl.when(k == N_PLANES - 1)
        def _():
            def slab(si, c):
                rr = _rows(si, 16)
                xs = x_ref[rr, :]
                rs = lax.rsqrt(_mean(xs * xs) + RMS_EPS)
                n = xs * rs
                dh = acc[rr, :]
                dg_ref[...] += _colsum(dh * n)
                dn = dh * g_ref[...]
                dx_ref[rr, :] = rs * (dn - n * _mean(dn * n)) + dxp_ref[rr, :]
                return c
            lax.fori_loop(0, tm // 16, slab, 0)

    def piece_spec(lo, n):
        return pl.BlockSpec((tm, D), lambda i, k: (i, jnp.clip(k - lo, 0, n - 1)))

    tokk = lambda: pl.BlockSpec((tm, D), lambda i, k: (i, 0))
    rowk = lambda: pl.BlockSpec((1, D), lambda i, k: (0, 0))
    return pl.pallas_call(
        body, name="in_proj_bwd", grid=(S // tm, N_PLANES),
        in_specs=[piece_spec(lo, n) for _, lo, n in pieces]
        + [pl.BlockSpec((None, D, D), lambda i, k: (l, 0, k)), tokk(), rowk(), tokk()],
        out_specs=[tokk(), rowk()],
        out_shape=[jax.ShapeDtypeStruct((S, D), F32), jax.ShapeDtypeStruct((1, D), F32)],
        scratch_shapes=[pltpu.VMEM((tm, D), F32)],
        compiler_params=_params("arbitrary", "arbitrary"),
    )(*[p for p, _, _ in pieces], w_all, x, g, dxp)


def _wgrad(a, b, out_shape, plane, col_off, prev, a_plane=None, b_plane=None):
    S = a.shape[-2]
    K1 = a.shape[-1]
    nb_cols = b.shape[-1]
    ts = _tile(S, 512)
    tn = _tile(nb_cols, 1024)

    def body(*refs):
        a_ref, b_ref = refs[:2]
        o_ref = refs[-1]
        k = pl.program_id(1)
        prod = _dot_ta(a_ref[...].astype(BF16), b_ref[...].astype(BF16))

        @pl.when(k == 0)
        def _():
            o_ref[...] = prod

        @pl.when(k > 0)
        def _():
            o_ref[...] += prod

    if a_plane is None:
        a_spec = pl.BlockSpec((ts, K1), lambda j, k: (k, 0))
    else:
        a_spec = pl.BlockSpec((None, ts, K1), lambda j, k: (a_plane, k, 0))
    if b_plane is None:
        b_spec = pl.BlockSpec((ts, tn), lambda j, k: (k, j))
    else:
        b_spec = pl.BlockSpec((None, ts, tn), lambda j, k: (b_plane, k, j))
    in_specs, args, aliases = [a_spec, b_spec], [a, b], {}
    if prev is not None:
        in_specs.append(ANY)
        args.append(prev)
        aliases = {2: 0}
    return pl.pallas_call(
        body, name="wgrad", grid=(nb_cols // tn, S // ts),
        in_specs=in_specs,
        out_specs=pl.BlockSpec((None, K1, tn), lambda j, k: (plane, 0, col_off + j)),
        out_shape=jax.ShapeDtypeStruct(out_shape, F32),
        input_output_aliases=aliases,
        compiler_params=_params("parallel", "arbitrary"),
    )(*args)


def _position():
    x, y, c = lax.axis_index("x"), lax.axis_index("y"), lax.axis_index("c")
    chips = [(1 - x, y), (x, 1 - y), (1 - x, 1 - y)]
    return x, y, c, chips


def _block_view(ref, axis, blk, size):
    idx = [slice(None)] * len(ref.shape)
    idx[axis] = pl.ds(blk * size, size)
    return ref.at[tuple(idx)]


def _gather_weights(shards, axes):
    n = len(shards)
    full_shapes = []
    for s, ax in zip(shards, axes):
        shp = list(s.shape)
        shp[ax] *= N_CHIPS
        full_shapes.append(jax.ShapeDtypeStruct(tuple(shp), s.dtype))

    def body(*refs):
        ins, outs = refs[:n], refs[n:2 * n]
        lsem, ssem, rsem = refs[2 * n:]
        x, y, c, chips = _position()
        j = 2 * x + y
        for t in range(n):
            size = ins[t].shape[axes[t]]
            mine = _block_view(outs[t], axes[t], j, size)
            pltpu.make_async_copy(ins[t], mine, lsem.at[t]).start()
            for k, (px, py) in enumerate(chips):
                pltpu.make_async_remote_copy(
                    src_ref=ins[t], dst_ref=mine, send_sem=ssem.at[t * 3 + k], recv_sem=rsem.at[t * 3 + k],
                    device_id=(px, py, c), device_id_type=MESH).start()
        for t in range(n):
            size = ins[t].shape[axes[t]]
            for k, (px, py) in enumerate(chips):
                theirs = _block_view(outs[t], axes[t], 2 * px + py, size)
                pltpu.make_async_remote_copy(
                    src_ref=ins[t], dst_ref=theirs, send_sem=ssem.at[t * 3 + k], recv_sem=rsem.at[t * 3 + k],
                    device_id=(px, py, c), device_id_type=MESH).wait()
            pltpu.make_async_copy(ins[t], _block_view(outs[t], axes[t], j, size), lsem.at[t]).wait()

    return pl.pallas_call(
        body, name="gather_weights",
        in_specs=[ANY] * n, out_specs=[ANY] * n, out_shape=full_shapes,
        scratch_shapes=[pltpu.SemaphoreType.DMA((n,)), pltpu.SemaphoreType.DMA((n * 3,)),
                        pltpu.SemaphoreType.DMA((n * 3,))],
    )(*shards)


def _pair_exchange(grads):
    n = len(grads)

    def body(*refs):
        ins, outs = refs[:n], refs[n:2 * n]
        ssem, rsem = refs[2 * n:]
        x, y, c, _ = _position()
        cps = [pltpu.make_async_remote_copy(
            src_ref=ins[t].at[1 - c], dst_ref=outs[t], send_sem=ssem.at[t], recv_sem=rsem.at[t],
            device_id=(x, y, 1 - c), device_id_type=MESH) for t in range(n)]
        for cp in cps:
            cp.start()
        for cp in cps:
            cp.wait()

    return pl.pallas_call(
        body, name="pair_exchange",
        in_specs=[ANY] * n, out_specs=[ANY] * n,
        out_shape=[jax.ShapeDtypeStruct(g.shape[1:], g.dtype) for g in grads],
        scratch_shapes=[pltpu.SemaphoreType.DMA((n,)), pltpu.SemaphoreType.DMA((n,))],
    )(*grads)


def _block_exchange(sums, axes):
    n = len(sums)
    out_shapes = []
    for s, ax in zip(sums, axes):
        shp = list(s.shape)
        shp[ax] //= N_CHIPS
        out_shapes.append(jax.ShapeDtypeStruct((3,) + tuple(shp), s.dtype))

    def body(*refs):
        ins, outs = refs[:n], refs[n:2 * n]
        ssem, rsem = refs[2 * n:]
        x, y, c, chips = _position()
        cps = []
        for t in range(n):
            size = ins[t].shape[axes[t]] // N_CHIPS
            for k, (px, py) in enumerate(chips):
                cps.append(pltpu.make_async_remote_copy(
                    src_ref=_block_view(ins[t], axes[t], 2 * px + py, size), dst_ref=outs[t].at[k],
                    send_sem=ssem.at[t * 3 + k], recv_sem=rsem.at[t * 3 + k],
                    device_id=(px, py, c), device_id_type=MESH))
        for cp in cps:
            cp.start()
        for cp in cps:
            cp.wait()

    return pl.pallas_call(
        body, name="block_exchange",
        in_specs=[ANY] * n, out_specs=[ANY] * n, out_shape=out_shapes,
        scratch_shapes=[pltpu.SemaphoreType.DMA((n * 3,)), pltpu.SemaphoreType.DMA((n * 3,))],
    )(*sums)


def _sibling_share(reduced):
    n = len(reduced)

    def body(*refs):
        ins, outs = refs[:n], refs[n:2 * n]
        lsem, ssem, rsem = refs[2 * n:]
        x, y, c, _ = _position()
        cps = []
        for t in range(n):
            cps.append(pltpu.make_async_copy(ins[t], outs[t].at[c], lsem.at[t]))
            cps.append(pltpu.make_async_remote_copy(
                src_ref=ins[t], dst_ref=outs[t].at[c], send_sem=ssem.at[t], recv_sem=rsem.at[t],
                device_id=(x, y, 1 - c), device_id_type=MESH))
        for cp in cps:
            cp.start()
        for cp in cps:
            cp.wait()

    return pl.pallas_call(
        body, name="sibling_share",
        in_specs=[ANY] * n, out_specs=[ANY] * n,
        out_shape=[jax.ShapeDtypeStruct((N_LAYERS,) + r.shape, r.dtype) for r in reduced],
        scratch_shapes=[pltpu.SemaphoreType.DMA((n,)), pltpu.SemaphoreType.DMA((n,)),
                        pltpu.SemaphoreType.DMA((n,))],
    )(*reduced)


def _small_allreduce(p):
    R, C = p.shape

    def body(p_ref, o_ref, sib_buf, chip_sums, ssem, rsem):
        x, y, c, chips = _position()
        j = 2 * x + y
        sib = pltpu.make_async_remote_copy(
            src_ref=p_ref, dst_ref=sib_buf, send_sem=ssem.at[0], recv_sem=rsem.at[0],
            device_id=(x, y, 1 - c), device_id_type=MESH)
        sib.start()
        sib.wait()
        chip_sums[j] = p_ref[...] + sib_buf[...]
        cps = [pltpu.make_async_remote_copy(
            src_ref=chip_sums.at[j], dst_ref=chip_sums.at[j], send_sem=ssem.at[1 + k], recv_sem=rsem.at[1 + k],
            device_id=(px, py, c), device_id_type=MESH) for k, (px, py) in enumerate(chips)]
        for cp in cps:
            cp.start()
        for k, (px, py) in enumerate(chips):
            pltpu.make_async_remote_copy(
                src_ref=chip_sums.at[j], dst_ref=chip_sums.at[2 * px + py],
                send_sem=ssem.at[1 + k], recv_sem=rsem.at[1 + k],
                device_id=(px, py, c), device_id_type=MESH).wait()
        o_ref[...] = ((chip_sums[0] + chip_sums[1]) + chip_sums[2]) + chip_sums[3]

    vm = pl.BlockSpec(memory_space=pltpu.VMEM)
    return pl.pallas_call(
        body, name="small_allreduce",
        in_specs=[vm], out_specs=vm, out_shape=jax.ShapeDtypeStruct((R, C), F32),
        scratch_shapes=[pltpu.VMEM((R, C), F32), pltpu.VMEM((N_CHIPS, R, C), F32),
                        pltpu.SemaphoreType.DMA((4,)), pltpu.SemaphoreType.DMA((4,))],
        compiler_params=pltpu.CompilerParams(vmem_limit_bytes=VMEM_LIMIT_BYTES),
    )(p)


def _add_own_layer(g, recv, pos):
    _, B, R, C = g.shape
    tr = _tile(R, 256)
    tc = _tile(C, 2816)

    def body(pos_ref, g_ref, r_ref, o_ref):
        del pos_ref
        o_ref[...] = g_ref[...] + r_ref[...]

    spec = lambda: pl.BlockSpec((None, tr, tc), lambda b, i, j, pos_ref: (b, i, j))
    return pl.pallas_call(
        body, name="add_own_layer",
        grid_spec=pltpu.PrefetchScalarGridSpec(
            num_scalar_prefetch=1, grid=(B, R // tr, C // tc),
            in_specs=[pl.BlockSpec((None, None, tr, tc), lambda b, i, j, pos_ref: (pos_ref[0], b, i, j)), spec()],
            out_specs=spec()),
        out_shape=jax.ShapeDtypeStruct((B, R, C), F32),
        compiler_params=_params("parallel", "parallel", "parallel"),
    )(pos, g, recv)


def _sum_blocks(own, recv, axis, pos):
    _, B, rs, cs = recv.shape
    tr = _tile(rs, 128)
    nr = rs // tr

    def body(pos_ref, o_ref, r0_ref, r1_ref, r2_ref, out_ref):
        del pos_ref
        out_ref[...] = ((o_ref[...] + r0_ref[...]) + r1_ref[...]) + r2_ref[...]

    if axis == 2:
        own_spec = pl.BlockSpec((None, tr, cs), lambda b, i, pos_ref: (b, i, pos_ref[1]))
    else:
        own_spec = pl.BlockSpec((None, tr, cs), lambda b, i, pos_ref: (b, pos_ref[1] * nr + i, 0))
    rspec = lambda k: pl.BlockSpec((None, None, tr, cs), lambda b, i, pos_ref: (k, b, i, 0))
    return pl.pallas_call(
        body, name="sum_blocks",
        grid_spec=pltpu.PrefetchScalarGridSpec(
            num_scalar_prefetch=1, grid=(B, nr),
            in_specs=[own_spec, rspec(0), rspec(1), rspec(2)],
            out_specs=pl.BlockSpec((None, tr, cs), lambda b, i, pos_ref: (b, i, 0))),
        out_shape=jax.ShapeDtypeStruct((B, rs, cs), F32),
        compiler_params=_params("parallel", "parallel"),
    )(pos, own, recv, recv, recv)


def _adamw(w, g, m, v):
    shape = w.shape
    C = shape[-1]
    R = w.size // C
    tr = R
    for cand in (512, 256, 128, 64, 32, 16, 8):
        if R % cand == 0 and cand * C * 4 <= (1 << 20):
            tr = cand
            break
    bc1 = 1.0 - ADAM_B1 ** ADAM_STEP
    bc2 = 1.0 - ADAM_B2 ** ADAM_STEP

    def body(w_ref, g_ref, m_ref, v_ref, d_ref, mo_ref, vo_ref):
        gg = g_ref[...]
        mn = ADAM_B1 * m_ref[...] + (1.0 - ADAM_B1) * gg
        vn = ADAM_B2 * v_ref[...] + (1.0 - ADAM_B2) * (gg * gg)
        mo_ref[...] = mn
        vo_ref[...] = vn
        d_ref[...] = -ADAM_LR * ((mn / bc1) / (jnp.sqrt(vn / bc2) + ADAM_EPS) + ADAM_WD * w_ref[...])

    spec = lambda: pl.BlockSpec((tr, C), lambda i: (i, 0))
    outs = pl.pallas_call(
        body, name="adamw", grid=(R // tr,),
        in_specs=[spec()] * 4, out_specs=[spec()] * 3,
        out_shape=[jax.ShapeDtypeStruct((R, C), F32)] * 3,
        compiler_params=_params("parallel"),
    )(*[a.reshape(R, C) for a in (w, g, m, v)])
    return tuple(o.reshape(shape) for o in outs)


def kernel(x, mem, norm_g, mem_norm_g, w_in, gmlp_ln_g, gmlp_ln_b, w_s, b_s, conv_w, conv_b, conv_ln_g, conv_ln_b, w_kv, w_branch, w_out, final_norm_g, loss_target, m_norm_g, m_mem_norm_g, m_w_in, m_gmlp_ln_g, m_gmlp_ln_b, m_w_s, m_b_s, m_conv_w, m_conv_b, m_conv_ln_g, m_conv_ln_b, m_w_kv, m_w_branch, m_w_out, m_final_norm_g, v_norm_g, v_mem_norm_g, v_w_in, v_gmlp_ln_g, v_gmlp_ln_b, v_w_s, v_b_s, v_conv_w, v_conv_b, v_conv_ln_g, v_conv_ln_b, v_w_kv, v_w_branch, v_w_out, v_final_norm_g):
    xs, mems, tgt = x[0], mem[0], loss_target[0]
    S, D = xs.shape
    assert D // GROUPS == LANES and S % CHUNK == 0 and w_s.shape[-1] == CHUNK
    L = N_LAYERS

    win_f, wkv_f, wb_f, wo_f, cw_f = _gather_weights(
        [w_in.astype(BF16), w_kv.astype(BF16), w_branch.astype(BF16), w_out.astype(BF16), conv_w],
        [2, 2, 2, 1, 2])

    tri = jnp.tril(jnp.ones((CHUNK, CHUNK), F32))
    t_mix = (w_s * tri).astype(BF16)
    t_mix_t = jnp.swapaxes(t_mix, -1, -2)
    bias_full = jnp.repeat(jnp.swapaxes(b_s, -1, -2), D // GROUPS, axis=-1)
    row = lambda a, l: a[l][None, :]

    saved = []
    h = xs
    for l in range(L):
        z, hb = _in_proj(h, row(norm_g, l), win_f, l)
        bra = _branch_a_fwd(z, row(gmlp_ln_g, l), row(gmlp_ln_b, l), t_mix[l], bias_full[l])
        c0 = _conv_fwd(z, cw_f, row(conv_b, l), l)
        brb = _branch_b_fwd(c0, z, row(conv_ln_g, l), row(conv_ln_b, l))
        mn, kv, kvt = _kv_fwd(mems, row(mem_norm_g, l), wkv_f, l)
        brc = _branch_c_fwd(z, kv, kvt)
        merged, h_next = _merge_fwd(bra, brb, brc, z, h, wb_f, wo_f, l)
        saved.append((h, z, hb, bra, c0, brb, mn, kv, kvt, brc, merged))
        h = h_next

    dx, loss_cols, d_final_g = _loss_bwd(h, tgt, final_norm_g[None, :])
    loss = lax.psum(0.5 * jnp.sum(loss_cols) / D, ("x", "y", "c"))

    g_in = g_kv = g_b = g_o = None
    small = {k: [None] * L for k in ("norm_g", "mem_norm_g", "gmlp_ln_g", "gmlp_ln_b", "w_s", "b_s", "conv_w",
                                     "conv_b", "conv_ln_g", "conv_ln_b")}
    for l in reversed(range(L)):
        h_in, z, hb, bra, c0, brb, mn, kv, kvt, brc, merged = saved[l]
        dz_m, dproj, dbr = _merge_bwd(dx, bra, brb, brc, z, wb_f, wo_f, l)
        g_o = _wgrad(merged, dx, (L, D, D), l, 0, g_o)
        for n, br in enumerate((bra, brb, brc)):
            g_b = _wgrad(br, dproj, (L * N_BRANCHES, D, D), l * N_BRANCHES + n, 0, g_b, b_plane=n)
        dz_a, dws, dbs, dlg, dlb = _branch_a_bwd(z, dbr[0], row(gmlp_ln_g, l), row(gmlp_ln_b, l),
                                                t_mix[l], t_mix_t[l], bias_full[l], tri)
        dc0, dz_b, dcg, dcbeta, dcb = _branch_b_bwd(c0, z, dbr[1], row(conv_ln_g, l), row(conv_ln_b, l))
        dz_b, dcw = _conv_bwd(dc0, z, cw_f, dz_b, l)
        dz_c, dk, dv = _branch_c_bwd(z, dbr[2], kv, kvt)
        g_kv, dmg = _kv_bwd(dk, dv, mn, mems, wkv_f, l, g_kv)
        pieces = [(dz_a, 0, 3), (dz_b, 3, 3), (dz_c, 6, 2), (dz_m, 8, 3)]
        for dzp, lo, _ in pieces:
            g_in = _wgrad(hb, dzp, (L, D, N_PLANES * D), l, lo, g_in)
        dx, dng = _in_proj_bwd(pieces, win_f, h_in, row(norm_g, l), dx, l)
        for k, val in (("norm_g", dng), ("mem_norm_g", dmg), ("gmlp_ln_g", dlg), ("gmlp_ln_b", dlb), ("w_s", dws),
                       ("b_s", dbs), ("conv_w", dcw), ("conv_b", dcb), ("conv_ln_g", dcg), ("conv_ln_b", dcbeta)):
            small[k][l] = val
    grad_x = dx[None]

    xi, yi, ci = lax.axis_index("x"), lax.axis_index("y"), lax.axis_index("c")
    pos = jnp.stack([ci, 2 * xi + yi]).astype(jnp.int32)
    big = [g_in[:, None], g_kv[:, None], g_b.reshape(L, N_BRANCHES, D, D), g_o[:, None]]
    shard_axis = [2, 2, 1, 1]
    recv = _pair_exchange(big)
    sums = [_add_own_layer(g, r, pos) for g, r in zip(big, recv)]
    blocks = _block_exchange(sums, shard_axis)
    reduced = [_sum_blocks(s, b, ax, pos) for s, b, ax in zip(sums, blocks, shard_axis)]
    gs_in, gs_kv, gs_b, gs_o = _sibling_share(reduced)
    grad_w_in = gs_in.reshape(w_in.shape)
    grad_w_kv = gs_kv.reshape(w_kv.shape)
    grad_w_branch = gs_b.reshape(w_branch.shape)
    grad_w_out = gs_o.reshape(w_out.shape)

    order = [("norm_g", norm_g.shape), ("mem_norm_g", mem_norm_g.shape), ("gmlp_ln_g", gmlp_ln_g.shape),
             ("gmlp_ln_b", gmlp_ln_b.shape), ("w_s", w_s.shape), ("b_s", b_s.shape),
             ("conv_w", (L, CONV_K, D)), ("conv_b", conv_b.shape), ("conv_ln_g", conv_ln_g.shape),
             ("conv_ln_b", conv_ln_b.shape)]
    parts = [jnp.stack([v.reshape(shp[1:]) for v in small[k]]).reshape(-1, LANES) for k, shp in order]
    parts.append(d_final_g.reshape(-1, LANES))
    sizes = [p.shape[0] for p in parts]
    assert all(s % 8 == 0 for s in sizes)
    total = _small_allreduce(jnp.concatenate(parts, axis=0))
    red, off = {}, 0
    for (k, shp), n in zip(order + [("final_norm_g", final_norm_g.shape)], sizes):
        red[k] = total[off:off + n].reshape(shp)
        off += n
    cs = conv_w.shape[-1]
    red["conv_w"] = lax.dynamic_slice_in_dim(red["conv_w"], (2 * xi + yi) * cs, cs, axis=2)
    red.update(w_in=grad_w_in, w_kv=grad_w_kv, w_branch=grad_w_branch, w_out=grad_w_out)

    names = ["norm_g", "mem_norm_g", "w_in", "gmlp_ln_g", "gmlp_ln_b", "w_s", "b_s", "conv_w", "conv_b",
             "conv_ln_g", "conv_ln_b", "w_kv", "w_branch", "w_out", "final_norm_g"]
    weights = dict(norm_g=norm_g, mem_norm_g=mem_norm_g, w_in=w_in, gmlp_ln_g=gmlp_ln_g, gmlp_ln_b=gmlp_ln_b,
                   w_s=w_s, b_s=b_s, conv_w=conv_w, conv_b=conv_b, conv_ln_g=conv_ln_g, conv_ln_b=conv_ln_b,
                   w_kv=w_kv, w_branch=w_branch, w_out=w_out, final_norm_g=final_norm_g)
    ms = dict(norm_g=m_norm_g, mem_norm_g=m_mem_norm_g, w_in=m_w_in, gmlp_ln_g=m_gmlp_ln_g, gmlp_ln_b=m_gmlp_ln_b,
              w_s=m_w_s, b_s=m_b_s, conv_w=m_conv_w, conv_b=m_conv_b, conv_ln_g=m_conv_ln_g, conv_ln_b=m_conv_ln_b,
              w_kv=m_w_kv, w_branch=m_w_branch, w_out=m_w_out, final_norm_g=m_final_norm_g)
    vs = dict(norm_g=v_norm_g, mem_norm_g=v_mem_norm_g, w_in=v_w_in, gmlp_ln_g=v_gmlp_ln_g, gmlp_ln_b=v_gmlp_ln_b,
              w_s=v_w_s, b_s=v_b_s, conv_w=v_conv_w, conv_b=v_conv_b, conv_ln_g=v_conv_ln_g, conv_ln_b=v_conv_ln_b,
              w_kv=v_w_kv, w_branch=v_w_branch, w_out=v_w_out, final_norm_g=v_final_norm_g)
    deltas, new_m, new_v = [], [], []
    for k in names:
        w2 = weights[k] if weights[k].ndim > 1 else weights[k][None, :]
        d, mo, vo = _adamw(w2, red[k].reshape(w2.shape), ms[k].reshape(w2.shape), vs[k].reshape(w2.shape))
        deltas.append(d.reshape(weights[k].shape))
        new_m.append(mo.reshape(weights[k].shape))
        new_v.append(vo.reshape(weights[k].shape))
    grads = [red[k].reshape(weights[k].shape) for k in names]
    return (loss, grad_x, *grads, *deltas, *new_m, *new_v)
```

```python
import math

import jax
import jax.numpy as jnp
from jax import lax
from jax.experimental import pallas as pl
from jax.experimental.pallas import tpu as pltpu

F32 = jnp.float32
BF16 = jnp.bfloat16

N_LAYERS = 2
N_BRANCHES = 3
N_PLANES = 11
N_CHIPS = 4
CHUNK = 128
GROUPS = 8
HEADS = 4
CONV_K = 31
HALO = 32
LANES = 128
RMS_EPS = 1e-6
LN_EPS = 1e-5
ADAM_LR, ADAM_B1, ADAM_B2, ADAM_EPS, ADAM_WD, ADAM_STEP = 0.001, 0.9, 0.999, 1e-08, 0.01, 10
VMEM_LIMIT_BYTES = 48 * 1024 * 1024
MESH = pl.DeviceIdType.MESH
ANY = pl.BlockSpec(memory_space=pl.ANY)


def _params(*sem):
    return pltpu.CompilerParams(dimension_semantics=sem, vmem_limit_bytes=VMEM_LIMIT_BYTES)


def _tile(n, pref):
    t = min(n, pref)
    assert n % t == 0, (n, t)
    return t


def _dot(a, b):
    return jnp.dot(a, b, preferred_element_type=F32)


def _dot_tb(a, b):
    return lax.dot_general(a, b, (((1,), (1,)), ((), ())), preferred_element_type=F32)


def _dot_ta(a, b):
    return lax.dot_general(a, b, (((0,), (0,)), ((), ())), preferred_element_type=F32)


def _sigmoid(x):
    return 1.0 / (1.0 + jnp.exp(-x))


def _silu(x):
    return x * _sigmoid(x)


def _silu_and_grad(x):
    s = _sigmoid(x)
    return x * s, s * (1.0 + x * (1.0 - s))


_GELU_C = math.sqrt(2.0 / math.pi)
_GELU_A = 0.044715


def _gelu(x):
    return x * (0.5 * (1.0 + jnp.tanh(_GELU_C * (x + _GELU_A * (x * x * x)))))


def _gelu_and_grad(x):
    x2 = x * x
    t = jnp.tanh(_GELU_C * (x + _GELU_A * (x2 * x)))
    cdf = 0.5 * (1.0 + t)
    dcdf = 0.5 * (1.0 - t * t) * (_GELU_C * (1.0 + 3.0 * _GELU_A * x2))
    return x * cdf, cdf + x * dcdf


def _rows(i, n):
    return pl.ds(pl.multiple_of(i * n, n), n)


def _mean(x):
    return jnp.mean(x, axis=-1, keepdims=True)


def _colsum(x):
    return jnp.sum(x, axis=0, keepdims=True)


def _tok(ts, d):
    return pl.BlockSpec((ts, d), lambda i: (i, 0))


def _zcol(ts, d, n):
    return pl.BlockSpec((ts, d), lambda i: (i, n))


def _row1(d):
    return pl.BlockSpec((1, d), lambda i: (0, 0))


def _plane(ts, d, n):
    return pl.BlockSpec((None, ts, d), lambda i: (n, i, 0))


def _in_proj(x, g, w_all, l):
    S, D = x.shape
    N = w_all.shape[2]
    tm, tn = _tile(S, 1024), D

    def body(x_ref, g_ref, w_ref, z_ref, hb_ref, h_scr):
        @pl.when(pl.program_id(1) == 0)
        def _():
            def slab(i, c):
                r = _rows(i, 16)
                xs = x_ref[r, :]
                h = (xs * lax.rsqrt(_mean(xs * xs) + RMS_EPS)) * g_ref[...]
                hb = h.astype(BF16)
                h_scr[r, :] = hb
                hb_ref[r, :] = hb
                return c
            lax.fori_loop(0, tm // 16, slab, 0)
        z_ref[...] = _dot(h_scr[...], w_ref[...])

    return pl.pallas_call(
        body, name="in_proj", grid=(S // tm, N // tn),
        in_specs=[pl.BlockSpec((tm, D), lambda i, j: (i, 0)),
                  pl.BlockSpec((1, D), lambda i, j: (0, 0)),
                  pl.BlockSpec((None, D, tn), lambda i, j: (l, 0, j))],
        out_specs=[pl.BlockSpec((tm, tn), lambda i, j: (i, j)),
                   pl.BlockSpec((tm, D), lambda i, j: (i, 0))],
        out_shape=[jax.ShapeDtypeStruct((S, N), F32), jax.ShapeDtypeStruct((S, D), BF16)],
        scratch_shapes=[pltpu.VMEM((tm, D), BF16)],
        compiler_params=_params("parallel", "arbitrary"),
    )(x, g, w_all)


def _branch_a_fwd(z, lng, lnb, t_mix, bias_full):
    S = z.shape[0]
    D = lng.shape[1]
    GD = D // GROUPS
    ts = _tile(S, 256)

    def body(zu_ref, zv_ref, zg_ref, lng_ref, lnb_ref, t_ref, bias_ref, a_ref, v_scr):
        def chunk(ci, carry):
            c0 = pl.multiple_of(ci * CHUNK, CHUNK)

            def slab(si, c):
                r = pl.multiple_of(si * 16, 16)
                vg = _gelu(zv_ref[pl.ds(pl.multiple_of(c0 + r, 16), 16), :])
                xc = vg - _mean(vg)
                y = xc * lax.rsqrt(_mean(xc * xc) + LN_EPS)
                v_scr[pl.ds(r, 16), :] = (y * lng_ref[...] + lnb_ref[...]).astype(BF16)
                return c
            lax.fori_loop(0, CHUNK // 16, slab, 0)
            for g in range(GROUPS):
                cols = slice(g * GD, (g + 1) * GD)
                sv = _dot(t_ref[g], v_scr[:, cols]) + bias_ref[:, cols]
                u = _gelu(zu_ref[pl.ds(c0, CHUNK), cols])
                a_ref[pl.ds(c0, CHUNK), cols] = ((u * sv) * _silu(zg_ref[pl.ds(c0, CHUNK), cols])).astype(BF16)
            return carry
        lax.fori_loop(0, ts // CHUNK, chunk, 0)

    return pl.pallas_call(
        body, name="branch_a_fwd", grid=(S // ts,),
        in_specs=[_zcol(ts, D, 0), _zcol(ts, D, 1), _zcol(ts, D, 2), _row1(D), _row1(D),
                  pl.BlockSpec((GROUPS, CHUNK, CHUNK), lambda i: (0, 0, 0)),
                  pl.BlockSpec((CHUNK, D), lambda i: (0, 0))],
        out_specs=_tok(ts, D),
        out_shape=jax.ShapeDtypeStruct((S, D), BF16),
        scratch_shapes=[pltpu.VMEM((CHUNK, D), BF16)],
        compiler_params=_params("parallel"),
    )(z, z, z, lng, lnb, t_mix, bias_full)


def _conv_fwd(z, w_all, cb, l):
    S = z.shape[0]
    D = cb.shape[1]
    ts = _tile(S, 256)
    CB = 256
    hb = ts // HALO

    def body(za_ref, zb_ref, zap_ref, zbp_ref, w_ref, b_ref, c0_ref, glu_scr):
        i = pl.program_id(0)
        prev = zap_ref[...] * _sigmoid(zbp_ref[...])
        glu_scr[0:HALO, :] = jnp.where(i == 0, 0.0, prev)

        def fill(si, c):
            r = _rows(si, 32)
            glu_scr[pl.ds(pl.multiple_of(HALO + si * 32, 32), 32), :] = za_ref[r, :] * _sigmoid(zb_ref[r, :])
            return c
        lax.fori_loop(0, ts // 32, fill, 0)

        def colblock(ci, c):
            cols = _rows(ci, CB)
            for s in range(ts // 32):
                acc = jnp.zeros((32, CB), F32)
                for k in range(CONV_K):
                    o = s * 32 + k + (HALO - CONV_K + 1)
                    acc = acc + glu_scr[pl.ds(o, 32), cols] * w_ref[k:k + 1, cols]
                c0_ref[pl.ds(s * 32, 32), cols] = acc + b_ref[:, cols]
            return c
        lax.fori_loop(0, D // CB, colblock, 0)

    prev_map = lambda n: (lambda i: (jnp.maximum(i * hb - 1, 0), n))
    return pl.pallas_call(
        body, name="conv_fwd", grid=(S // ts,),
        in_specs=[_zcol(ts, D, 3), _zcol(ts, D, 4),
                  pl.BlockSpec((HALO, D), prev_map(3)), pl.BlockSpec((HALO, D), prev_map(4)),
                  pl.BlockSpec((None, CONV_K, D), lambda i: (l, 0, 0)), _row1(D)],
        out_specs=_tok(ts, D),
        out_shape=jax.ShapeDtypeStruct((S, D), F32),
        scratch_shapes=[pltpu.VMEM((ts + HALO, D), F32)],
        compiler_params=_params("parallel"),
    )(z, z, z, z, w_all, cb)


def _branch_b_fwd(c0, z, g, b):
    S, D = c0.shape
    ts = _tile(S, 512)

    def body(c0_ref, zg_ref, g_ref, b_ref, o_ref):
        def slab(si, c):
            r = _rows(si, 16)
            v = c0_ref[r, :]
            xc = v - _mean(v)
            c1 = (xc * lax.rsqrt(_mean(xc * xc) + LN_EPS)) * g_ref[...] + b_ref[...]
            o_ref[r, :] = (_silu(c1) * _silu(zg_ref[r, :])).astype(BF16)
            return c
        lax.fori_loop(0, ts // 16, slab, 0)

    return pl.pallas_call(
        body, name="branch_b_fwd", grid=(S // ts,),
        in_specs=[_tok(ts, D), _zcol(ts, D, 5), _row1(D), _row1(D)],
        out_specs=_tok(ts, D),
        out_shape=jax.ShapeDtypeStruct((S, D), BF16),
        compiler_params=_params("parallel"),
    )(c0, z, g, b)


def _kv_fwd(mem, mg, wkv_all, l):
    M, D = mem.shape

    def body(mem_ref, g_ref, w_ref, mn_ref, kv_ref, kvt_ref):
        m = mem_ref[...]
        mn = ((m * lax.rsqrt(_mean(m * m) + RMS_EPS)) * g_ref[...]).astype(BF16)
        mn_ref[...] = mn
        kv = _dot(mn, w_ref[...])
        kv_ref[...] = kv.astype(BF16)
        kvt_ref[...] = kv.T.astype(BF16)

    return pl.pallas_call(
        body, name="kv_fwd", grid=(1,),
        in_specs=[pl.BlockSpec((M, D), lambda i: (0, 0)), _row1(D),
                  pl.BlockSpec((None, D, 2 * D), lambda i: (l, 0, 0))],
        out_specs=[pl.BlockSpec((M, D), lambda i: (0, 0)), pl.BlockSpec((M, 2 * D), lambda i: (0, 0)),
                   pl.BlockSpec((2 * D, M), lambda i: (0, 0))],
        out_shape=[jax.ShapeDtypeStruct((M, D), BF16), jax.ShapeDtypeStruct((M, 2 * D), BF16),
                   jax.ShapeDtypeStruct((2 * D, M), BF16)],
        compiler_params=_params("arbitrary"),
    )(mem, mg, wkv_all)


def _softmax_rows(s):
    e = jnp.exp(s - jnp.max(s, axis=-1, keepdims=True))
    return e / jnp.sum(e, axis=-1, keepdims=True)


def _branch_c_fwd(z, kv, kvt):
    S = z.shape[0]
    M, D2 = kv.shape
    D = D2 // 2
    HD = D // HEADS
    scale = 1.0 / math.sqrt(HD)
    ts = _tile(S, 256)

    def body(q_ref, zg_ref, kt_ref, v_ref, o_ref):
        for h in range(HEADS):
            cols = slice(h * HD, (h + 1) * HD)
            p = _softmax_rows(_dot(q_ref[:, cols].astype(BF16), kt_ref[cols, :]) * scale)
            att = _dot(p.astype(BF16), v_ref[:, cols])
            o_ref[:, cols] = (att * _silu(zg_ref[:, cols])).astype(BF16)

    return pl.pallas_call(
        body, name="branch_c_fwd", grid=(S // ts,),
        in_specs=[_zcol(ts, D, 6), _zcol(ts, D, 7),
                  pl.BlockSpec((D, M), lambda i: (0, 0)), pl.BlockSpec((M, D), lambda i: (0, 1))],
        out_specs=_tok(ts, D),
        out_shape=jax.ShapeDtypeStruct((S, D), BF16),
        compiler_params=_params("parallel"),
    )(z, z, kvt, kv)


def _merge_fwd(bra, brb, brc, z, x, wb_all, wo_all, l):
    S, D = x.shape
    ts = _tile(S, 256)
    CB = 256

    def body(a_ref, b_ref, c_ref, zm0_ref, zm1_ref, zm2_ref, x_ref, wb_ref, wo_ref, m_ref, xo_ref):
        brs = (a_ref, b_ref, c_ref)
        zms = (zm0_ref, zm1_ref, zm2_ref)
        for cb in range(D // CB):
            cols = slice(cb * CB, (cb + 1) * CB)
            acc = None
            for n in range(N_BRANCHES):
                t = _sigmoid(zms[n][:, cols]) * _dot(brs[n][...], wb_ref[n, :, cols])
                acc = t if acc is None else acc + t
            m_ref[:, cols] = acc.astype(BF16)
        for cb in range(D // CB):
            cols = slice(cb * CB, (cb + 1) * CB)
            xo_ref[:, cols] = x_ref[:, cols] + _dot(m_ref[...], wo_ref[:, cols])

    return pl.pallas_call(
        body, name="merge_fwd", grid=(S // ts,),
        in_specs=[_tok(ts, D), _tok(ts, D), _tok(ts, D), _zcol(ts, D, 8), _zcol(ts, D, 9), _zcol(ts, D, 10),
                  _tok(ts, D),
                  pl.BlockSpec((None, N_BRANCHES, D, D), lambda i: (l, 0, 0, 0)),
                  pl.BlockSpec((None, D, D), lambda i: (l, 0, 0))],
        out_specs=[_tok(ts, D), _tok(ts, D)],
        out_shape=[jax.ShapeDtypeStruct((S, D), BF16), jax.ShapeDtypeStruct((S, D), F32)],
        compiler_params=_params("parallel"),
    )(bra, brb, brc, z, z, z, x, wb_all, wo_all)


def _loss_bwd(x, target, fg):
    S, D = x.shape
    ts = _tile(S, 512)

    def body(x_ref, t_ref, g_ref, dx_ref, ls_ref, dg_ref):
        @pl.when(pl.program_id(0) == 0)
        def _():
            ls_ref[...] = jnp.zeros_like(ls_ref)
            dg_ref[...] = jnp.zeros_like(dg_ref)

        def slab(si, c):
            r = _rows(si, 16)
            xs = x_ref[r, :]
            rs = lax.rsqrt(_mean(xs * xs) + RMS_EPS)
            n = xs * rs
            e = n * g_ref[...] - t_ref[r, :]
            ls_ref[...] += _colsum(e * e)
            dy = e * (1.0 / D)
            dg_ref[...] += _colsum(dy * n)
            dn = dy * g_ref[...]
            dx_ref[r, :] = rs * (dn - n * _mean(dn * n))
            return c
        lax.fori_loop(0, ts // 16, slab, 0)

    return pl.pallas_call(
        body, name="loss_bwd", grid=(S // ts,),
        in_specs=[_tok(ts, D), _tok(ts, D), _row1(D)],
        out_specs=[_tok(ts, D), _row1(D), _row1(D)],
        out_shape=[jax.ShapeDtypeStruct((S, D), F32), jax.ShapeDtypeStruct((1, D), F32),
                   jax.ShapeDtypeStruct((1, D), F32)],
        compiler_params=_params("arbitrary"),
    )(x, target, fg)


def _merge_bwd(dxp, bra, brb, brc, z, wb_all, wo_all, l):
    S, D = dxp.shape
    ts = _tile(S, 256)
    CB = 256

    def body(d_ref, a_ref, b_ref, c_ref, zm0_ref, zm1_ref, zm2_ref, wb_ref, wo_ref,
             dzm_ref, dproj_ref, dbr_ref, dxb_scr):
        brs = (a_ref, b_ref, c_ref)
        zms = (zm0_ref, zm1_ref, zm2_ref)
        dxb_scr[...] = d_ref[...].astype(BF16)
        for cb in range(D // CB):
            cols = slice(cb * CB, (cb + 1) * CB)
            dm = _dot_tb(dxb_scr[...], wo_ref[cols, :])
            for n in range(N_BRANCHES):
                proj = _dot(brs[n][...], wb_ref[n, :, cols])
                g = _sigmoid(zms[n][:, cols])
                dzm_ref[:, n * D + cb * CB:n * D + (cb + 1) * CB] = ((dm * proj) * (g * (1.0 - g))).astype(BF16)
                dproj_ref[n, :, cols] = (dm * g).astype(BF16)
        for n in range(N_BRANCHES):
            for cb in range(D // CB):
                cols = slice(cb * CB, (cb + 1) * CB)
                dbr_ref[n, :, cols] = _dot_tb(dproj_ref[n], wb_ref[n, cols, :])

    tok3 = lambda: pl.BlockSpec((N_BRANCHES, ts, D), lambda i: (0, i, 0))
    return pl.pallas_call(
        body, name="merge_bwd", grid=(S // ts,),
        in_specs=[_tok(ts, D), _tok(ts, D), _tok(ts, D), _tok(ts, D),
                  _zcol(ts, D, 8), _zcol(ts, D, 9), _zcol(ts, D, 10),
                  pl.BlockSpec((None, N_BRANCHES, D, D), lambda i: (l, 0, 0, 0)),
                  pl.BlockSpec((None, D, D), lambda i: (l, 0, 0))],
        out_specs=[_tok(ts, 3 * D), tok3(), tok3()],
        out_shape=[jax.ShapeDtypeStruct((S, 3 * D), BF16), jax.ShapeDtypeStruct((N_BRANCHES, S, D), BF16),
                   jax.ShapeDtypeStruct((N_BRANCHES, S, D), F32)],
        scratch_shapes=[pltpu.VMEM((ts, D), BF16)],
        compiler_params=_params("parallel"),
    )(dxp, bra, brb, brc, z, z, z, wb_all, wo_all)


def _branch_a_bwd(z, dbr, lng, lnb, t_mix, t_mix_t, bias_full, mask):
    S = z.shape[0]
    D = lng.shape[1]
    GD = D // GROUPS
    ts = _tile(S, 256)
    nt = S // ts

    def body(zu_ref, zv_ref, zg_ref, d_ref, lng_ref, lnb_ref, t_ref, tt_ref, bias_ref, mask_ref,
             dz_ref, dws_ref, dbs_ref, dlg_ref, dlb_ref,
             v_scr, vn_scr, rstd_scr, dv_scr, dsv_acc):
        i = pl.program_id(0)

        @pl.when(i == 0)
        def _():
            dws_ref[...] = jnp.zeros_like(dws_ref)
            dlg_ref[...] = jnp.zeros_like(dlg_ref)
            dlb_ref[...] = jnp.zeros_like(dlb_ref)
            dsv_acc[...] = jnp.zeros_like(dsv_acc)

        def chunk(ci, carry):
            c0 = pl.multiple_of(ci * CHUNK, CHUNK)

            def ln_fwd(si, c):
                r = pl.multiple_of(si * 16, 16)
                vg = _gelu(zv_ref[pl.ds(pl.multiple_of(c0 + r, 16), 16), :])
                xc = vg - _mean(vg)
                rstd = lax.rsqrt(_mean(xc * xc) + LN_EPS)
                vn = xc * rstd
                vn_scr[pl.ds(r, 16), :] = vn
                rstd_scr[pl.ds(r, 16), :] = rstd
                v_scr[pl.ds(r, 16), :] = (vn * lng_ref[...] + lnb_ref[...]).astype(BF16)
                return c
            lax.fori_loop(0, CHUNK // 16, ln_fwd, 0)

            for g in range(GROUPS):
                cols = slice(g * GD, (g + 1) * GD)
                rows = pl.ds(c0, CHUNK)
                vb = v_scr[:, cols]
                sv = _dot(t_ref[g], vb) + bias_ref[:, cols]
                u, du = _gelu_and_grad(zu_ref[rows, cols])
                sg, dsg = _silu_and_grad(zg_ref[rows, cols])
                d = d_ref[rows, cols]
                dsv = (d * u) * sg
                dz_ref[rows, g * GD:(g + 1) * GD] = (((d * sv) * sg) * du).astype(BF16)
                dz_ref[rows, 2 * D + g * GD:2 * D + (g + 1) * GD] = (((d * u) * sv) * dsg).astype(BF16)
                dsvb = dsv.astype(BF16)
                dws_ref[g] += _dot_tb(dsvb, vb)
                dv_scr[:, cols] = _dot(tt_ref[g], dsvb)
                dsv_acc[:, cols] += dsv

            def ln_bwd(si, c):
                r = pl.multiple_of(si * 16, 16)
                rr = pl.ds(r, 16)
                zrows = pl.ds(pl.multiple_of(c0 + r, 16), 16)
                dv = dv_scr[rr, :]
                vn = vn_scr[rr, :]
                dlg_ref[...] += _colsum(dv * vn)
                dlb_ref[...] += _colsum(dv)
                dvn = dv * lng_ref[...]
                dvg = rstd_scr[rr, :] * ((dvn - _mean(dvn)) - vn * _mean(dvn * vn))
                _, gg = _gelu_and_grad(zv_ref[zrows, :])
                dz_ref[zrows, D:2 * D] = (dvg * gg).astype(BF16)
                return c
            lax.fori_loop(0, CHUNK // 16, ln_bwd, 0)
            return carry
        lax.fori_loop(0, ts // CHUNK, chunk, 0)

        @pl.when(i == nt - 1)
        def _():
            for g in range(GROUPS):
                cols = slice(g * GD, (g + 1) * GD)
                dws_ref[g] = dws_ref[g] * mask_ref[...]
                dbs_ref[g:g + 1, :] = _colsum(dsv_acc[:, cols].T)

    c2 = lambda shape: pl.BlockSpec(shape, lambda i: (0, 0))
    c3 = lambda: pl.BlockSpec((GROUPS, CHUNK, CHUNK), lambda i: (0, 0, 0))
    return pl.pallas_call(
        body, name="branch_a_bwd", grid=(nt,),
        in_specs=[_zcol(ts, D, 0), _zcol(ts, D, 1), _zcol(ts, D, 2), _plane(ts, D, 0), _row1(D), _row1(D),
                  c3(), c3(), c2((CHUNK, D)), c2((CHUNK, CHUNK))],
        out_specs=[_tok(ts, 3 * D), c3(), c2((GROUPS, CHUNK)), _row1(D), _row1(D)],
        out_shape=[jax.ShapeDtypeStruct((S, 3 * D), BF16), jax.ShapeDtypeStruct((GROUPS, CHUNK, CHUNK), F32),
                   jax.ShapeDtypeStruct((GROUPS, CHUNK), F32), jax.ShapeDtypeStruct((1, D), F32),
                   jax.ShapeDtypeStruct((1, D), F32)],
        scratch_shapes=[pltpu.VMEM((CHUNK, D), BF16), pltpu.VMEM((CHUNK, D), F32), pltpu.VMEM((CHUNK, 1), F32),
                        pltpu.VMEM((CHUNK, D), F32), pltpu.VMEM((CHUNK, D), F32)],
        compiler_params=_params("arbitrary"),
    )(z, z, z, dbr, lng, lnb, t_mix, t_mix_t, bias_full, mask)


def _branch_b_bwd(c0, z, dbr, g, b):
    S, D = c0.shape
    ts = _tile(S, 512)

    def body(c0_ref, zg_ref, d_ref, g_ref, b_ref, dc0_ref, dz_ref, dg_ref, db_ref, dcb_ref):
        @pl.when(pl.program_id(0) == 0)
        def _():
            dg_ref[...] = jnp.zeros_like(dg_ref)
            db_ref[...] = jnp.zeros_like(db_ref)
            dcb_ref[...] = jnp.zeros_like(dcb_ref)

        def slab(si, c):
            r = _rows(si, 16)
            v = c0_ref[r, :]
            xc = v - _mean(v)
            rstd = lax.rsqrt(_mean(xc * xc) + LN_EPS)
            cn = xc * rstd
            c1 = cn * g_ref[...] + b_ref[...]
            c2, dc2_dc1 = _silu_and_grad(c1)
            sg, dsg = _silu_and_grad(zg_ref[r, :])
            d = d_ref[r, :]
            dz_ref[r, :] = ((d * c2) * dsg).astype(BF16)
            dc1 = (d * sg) * dc2_dc1
            dg_ref[...] += _colsum(dc1 * cn)
            db_ref[...] += _colsum(dc1)
            dcn = dc1 * g_ref[...]
            dc0 = rstd * ((dcn - _mean(dcn)) - cn * _mean(dcn * cn))
            dcb_ref[...] += _colsum(dc0)
            dc0_ref[r, :] = dc0
            return c
        lax.fori_loop(0, ts // 16, slab, 0)

    return pl.pallas_call(
        body, name="branch_b_bwd", grid=(S // ts,),
        in_specs=[_tok(ts, D), _zcol(ts, D, 5), _plane(ts, D, 1), _row1(D), _row1(D)],
        out_specs=[_tok(ts, D), _zcol(ts, D, 2), _row1(D), _row1(D), _row1(D)],
        out_shape=[jax.ShapeDtypeStruct((S, D), F32), jax.ShapeDtypeStruct((S, 3 * D), BF16),
                   jax.ShapeDtypeStruct((1, D), F32), jax.ShapeDtypeStruct((1, D), F32),
                   jax.ShapeDtypeStruct((1, D), F32)],
        compiler_params=_params("arbitrary"),
    )(c0, z, dbr, g, b)


def _conv_bwd(dc0, z, w_all, dz_b, l):
    S, D = dc0.shape
    ts = _tile(S, 256)
    nt = S // ts
    CB = 256
    hb = ts // HALO
    nh = S // HALO
    SH = HALO - CONV_K + 1

    def body(dc_ref, dcn_ref, za_ref, zb_ref, zap_ref, zbp_ref, w_ref, dzin_ref,
             dz_ref, dw_ref, glu_scr, dc_scr, dw_acc):
        del dzin_ref
        i = pl.program_id(0)

        @pl.when(i == 0)
        def _():
            dw_acc[...] = jnp.zeros_like(dw_acc)

        prev = zap_ref[...] * _sigmoid(zbp_ref[...])
        glu_scr[0:HALO, :] = jnp.where(i == 0, 0.0, prev)
        dc_scr[ts:ts + HALO, :] = jnp.where(i == nt - 1, 0.0, dcn_ref[...])

        def fill(si, c):
            r = _rows(si, 32)
            glu_scr[pl.ds(pl.multiple_of(HALO + si * 32, 32), 32), :] = za_ref[r, :] * _sigmoid(zb_ref[r, :])
            dc_scr[r, :] = dc_ref[r, :]
            return c
        lax.fori_loop(0, ts // 32, fill, 0)

        def colblock(ci, c):
            cols = _rows(ci, CB)
            cols_b = pl.ds(pl.multiple_of(D + ci * CB, CB), CB)
            for s in range(ts // 32):
                rows = pl.ds(s * 32, 32)
                dc = dc_scr[rows, cols]
                acc = jnp.zeros((32, CB), F32)
                for k in range(CONV_K):
                    prod = dc * glu_scr[pl.ds(s * 32 + k + SH, 32), cols]
                    dw_acc[k, :, cols] += (prod[0:8] + prod[8:16]) + (prod[16:24] + prod[24:32])
                    acc = acc + dc_scr[pl.ds(s * 32 + k, 32), cols] * w_ref[CONV_K - 1 - k:CONV_K - k, cols]
                sg = _sigmoid(zb_ref[rows, cols])
                dz_ref[rows, cols] = (acc * sg).astype(BF16)
                dz_ref[rows, cols_b] = ((acc * za_ref[rows, cols]) * (sg * (1.0 - sg))).astype(BF16)
            return c
        lax.fori_loop(0, D // CB, colblock, 0)

        @pl.when(i == nt - 1)
        def _():
            for k in range(CONV_K):
                dw_ref[k:k + 1, :] = _colsum(dw_acc[k])

    prev_map = lambda n: (lambda i: (jnp.maximum(i * hb - 1, 0), n))
    return pl.pallas_call(
        body, name="conv_bwd", grid=(nt,),
        in_specs=[_tok(ts, D), pl.BlockSpec((HALO, D), lambda i: (jnp.minimum((i + 1) * hb, nh - 1), 0)),
                  _zcol(ts, D, 3), _zcol(ts, D, 4),
                  pl.BlockSpec((HALO, D), prev_map(3)), pl.BlockSpec((HALO, D), prev_map(4)),
                  pl.BlockSpec((None, CONV_K, D), lambda i: (l, 0, 0)), ANY],
        out_specs=[_tok(ts, 2 * D), pl.BlockSpec((CONV_K, D), lambda i: (0, 0))],
        out_shape=[jax.ShapeDtypeStruct((S, 3 * D), BF16), jax.ShapeDtypeStruct((CONV_K, D), F32)],
        scratch_shapes=[pltpu.VMEM((ts + HALO, D), F32), pltpu.VMEM((ts + HALO, D), F32),
                        pltpu.VMEM((CONV_K, 8, D), F32)],
        input_output_aliases={7: 0},
        compiler_params=_params("arbitrary"),
    )(dc0, dc0, z, z, z, z, w_all, dz_b)


def _branch_c_bwd(z, dbr, kv, kvt):
    S = z.shape[0]
    M, D2 = kv.shape
    D = D2 // 2
    HD = D // HEADS
    scale = 1.0 / math.sqrt(HD)
    ts = _tile(S, 256)

    def body(q_ref, zg_ref, d_ref, kt_ref, vt_ref, k_ref, v_ref, dz_ref, dk_ref, dv_ref):
        @pl.when(pl.program_id(0) == 0)
        def _():
            dk_ref[...] = jnp.zeros_like(dk_ref)
            dv_ref[...] = jnp.zeros_like(dv_ref)

        for h in range(HEADS):
            cols = slice(h * HD, (h + 1) * HD)
            qb = q_ref[:, cols].astype(BF16)
            p = _softmax_rows(_dot(qb, kt_ref[cols, :]) * scale)
            pb = p.astype(BF16)
            att = _dot(pb, v_ref[:, cols])
            sg, dsg = _silu_and_grad(zg_ref[:, cols])
            d = d_ref[:, cols]
            dz_ref[:, D + h * HD:D + (h + 1) * HD] = ((d * att) * dsg).astype(BF16)
            datt = (d * sg).astype(BF16)
            dp = _dot(datt, vt_ref[cols, :])
            dv_ref[:, cols] += _dot_ta(pb, datt)
            ds = ((p * (dp - jnp.sum(dp * p, axis=-1, keepdims=True))) * scale).astype(BF16)
            dz_ref[:, cols] = _dot(ds, k_ref[:, cols]).astype(BF16)
            dk_ref[:, cols] += _dot_ta(ds, qb)

    return pl.pallas_call(
        body, name="branch_c_bwd", grid=(S // ts,),
        in_specs=[_zcol(ts, D, 6), _zcol(ts, D, 7), _plane(ts, D, 2),
                  pl.BlockSpec((D, M), lambda i: (0, 0)), pl.BlockSpec((D, M), lambda i: (1, 0)),
                  pl.BlockSpec((M, D), lambda i: (0, 0)), pl.BlockSpec((M, D), lambda i: (0, 1))],
        out_specs=[_tok(ts, 2 * D), pl.BlockSpec((M, D), lambda i: (0, 0)), pl.BlockSpec((M, D), lambda i: (0, 0))],
        out_shape=[jax.ShapeDtypeStruct((S, 2 * D), BF16), jax.ShapeDtypeStruct((M, D), F32),
                   jax.ShapeDtypeStruct((M, D), F32)],
        compiler_params=_params("arbitrary"),
    )(z, z, dbr, kvt, kvt, kv, kv)


def _kv_bwd(dk, dv, mn, mem, wkv_all, l, prev):
    M, D = mem.shape

    def body(*refs):
        dk_ref, dv_ref, mn_ref, mem_ref, w_ref = refs[:5]
        dw_ref, dg_ref = refs[-2:]
        dkb = dk_ref[...].astype(BF16)
        dvb = dv_ref[...].astype(BF16)
        dw_ref[:, 0:D] = _dot_ta(mn_ref[...], dkb)
        dw_ref[:, D:2 * D] = _dot_ta(mn_ref[...], dvb)
        dmn = _dot_tb(dkb, w_ref[:, 0:D]) + _dot_tb(dvb, w_ref[:, D:2 * D])
        m = mem_ref[...]
        dg_ref[...] = _colsum(dmn * (m * lax.rsqrt(_mean(m * m) + RMS_EPS)))

    full = lambda shape: pl.BlockSpec(shape, lambda i: (0, 0))
    in_specs = [full((M, D)), full((M, D)), full((M, D)), full((M, D)),
                pl.BlockSpec((None, D, 2 * D), lambda i: (l, 0, 0))]
    args = [dk, dv, mn, mem, wkv_all]
    aliases = {}
    if prev is not None:
        in_specs.append(ANY)
        args.append(prev)
        aliases = {5: 0}
    return pl.pallas_call(
        body, name="kv_bwd", grid=(1,),
        in_specs=in_specs,
        out_specs=[pl.BlockSpec((None, D, 2 * D), lambda i: (l, 0, 0)), _row1(D)],
        out_shape=[jax.ShapeDtypeStruct((N_LAYERS, D, 2 * D), F32), jax.ShapeDtypeStruct((1, D), F32)],
        input_output_aliases=aliases,
        compiler_params=_params("arbitrary"),
    )(*args)


def _in_proj_bwd(pieces, w_all, x, g, dxp, l):
    S, D = x.shape
    tm = _tile(S, 512)
    n_p = len(pieces)

    def body(*refs):
        dz_refs = refs[:n_p]
        w_ref, x_ref, g_ref, dxp_ref, dx_ref, dg_ref, acc = refs[n_p:]
        i = pl.program_id(0)
        k = pl.program_id(1)

        @pl.when(k == 0)
        def _():
            acc[...] = jnp.zeros_like(acc)

        @pl.when((i == 0) & (k == 0))
        def _():
            dg_ref[...] = jnp.zeros_like(dg_ref)

        for (_, lo, n), r in zip(pieces, dz_refs):
            @pl.when((k >= lo) & (k < lo + n))
            def _(r=r):
                acc[...] += _dot_tb(r[...], w_ref[...])

        @pl.when(k == N_PLANES - 1)
        def _():
            def slab(si, c):
                rr = _rows(si, 16)
                xs = x_ref[rr, :]
                rs = lax.rsqrt(_mean(xs * xs) + RMS_EPS)
                n = xs * rs
                dh = acc[rr, :]
                dg_ref[...] += _colsum(dh * n)
                dn = dh * g_ref[...]
                dx_ref[rr, :] = rs * (dn - n * _mean(dn * n)) + dxp_ref[rr, :]
                return c
            lax.fori_loop(0, tm // 16, slab, 0)

    def piece_spec(lo, n):
        return pl.BlockSpec((tm, D), lambda i, k: (i, jnp.clip(k - lo, 0, n - 1)))

    tokk = lambda: pl.BlockSpec((tm, D), lambda i, k: (i, 0))
    rowk = lambda: pl.BlockSpec((1, D), lambda i, k: (0, 0))
    return pl.pallas_call(
        body, name="in_proj_bwd", grid=(S // tm, N_PLANES),
        in_specs=[piece_spec(lo, n) for _, lo, n in pieces]
        + [pl.BlockSpec((None, D, D), lambda i, k: (l, 0, k)), tokk(), rowk(), tokk()],
        out_specs=[tokk(), rowk()],
        out_shape=[jax.ShapeDtypeStruct((S, D), F32), jax.ShapeDtypeStruct((1, D), F32)],
        scratch_shapes=[pltpu.VMEM((tm, D), F32)],
        compiler_params=_params("arbitrary", "arbitrary"),
    )(*[p for p, _, _ in pieces], w_all, x, g, dxp)


def _wgrad(a, b, out_shape, plane, col_off, prev, a_plane=None, b_plane=None):
    S = a.shape[-2]
    K1 = a.shape[-1]
    nb_cols = b.shape[-1]
    ts = _tile(S, 512)
    tn = _tile(nb_cols, 1024)

    def body(*refs):
        a_ref, b_ref = refs[:2]
        o_ref = refs[-1]
        k = pl.program_id(1)
        prod = _dot_ta(a_ref[...].astype(BF16), b_ref[...].astype(BF16))

        @pl.when(k == 0)
        def _():
            o_ref[...] = prod

        @pl.when(k > 0)
        def _():
            o_ref[...] += prod

    if a_plane is None:
        a_spec = pl.BlockSpec((ts, K1), lambda j, k: (k, 0))
    else:
        a_spec = pl.BlockSpec((None, ts, K1), lambda j, k: (a_plane, k, 0))
    if b_plane is None:
        b_spec = pl.BlockSpec((ts, tn), lambda j, k: (k, j))
    else:
        b_spec = pl.BlockSpec((None, ts, tn), lambda j, k: (b_plane, k, j))
    in_specs, args, aliases = [a_spec, b_spec], [a, b], {}
    if prev is not None:
        in_specs.append(ANY)
        args.append(prev)
        aliases = {2: 0}
    return pl.pallas_call(
        body, name="wgrad", grid=(nb_cols // tn, S // ts),
        in_specs=in_specs,
        out_specs=pl.BlockSpec((None, K1, tn), lambda j, k: (plane, 0, col_off + j)),
        out_shape=jax.ShapeDtypeStruct(out_shape, F32),
        input_output_aliases=aliases,
        compiler_params=_params("parallel", "arbitrary"),
    )(*args)


def _position():
    x, y, c = lax.axis_index("x"), lax.axis_index("y"), lax.axis_index("c")
    chips = [(1 - x, y), (x, 1 - y), (1 - x, 1 - y)]
    return x, y, c, chips


def _block_view(ref, axis, blk, size):
    idx = [slice(None)] * len(ref.shape)
    idx[axis] = pl.ds(blk * size, size)
    return ref.at[tuple(idx)]


def _gather_weights(shards, axes):
    n = len(shards)
    full_shapes = []
    for s, ax in zip(shards, axes):
        shp = list(s.shape)
        shp[ax] *= N_CHIPS
        full_shapes.append(jax.ShapeDtypeStruct(tuple(shp), s.dtype))

    def body(*refs):
        ins, outs = refs[:n], refs[n:2 * n]
        s_own, r_own, s_ici, r_ici, s_fwd, r_fwd = refs[2 * n:]
        x, y, c, chips = _position()
        j = 2 * x + y
        sib = (x, y, 1 - c)

        def block(t, blk, layer=None):
            size = ins[t].shape[axes[t]]
            idx = [slice(None)] * len(outs[t].shape)
            idx[axes[t]] = pl.ds(blk * size, size)
            if layer is not None:
                idx[0] = layer
            return outs[t].at[tuple(idx)]

        def copy(src, dst, ssem, rsem, to):
            return pltpu.make_async_remote_copy(src_ref=src, dst_ref=dst, send_sem=ssem, recv_sem=rsem,
                                                device_id=to, device_id_type=MESH)

        own = [copy(ins[t], block(t, j), s_own.at[t], r_own.at[t], sib) for t in range(n)]
        for cp in own:
            cp.start()
        out_ici = []
        for t in range(n):
            for k, (px, py) in enumerate(chips):
                cp = copy(ins[t].at[c], block(t, j, c), s_ici.at[t * 3 + k], r_ici.at[t * 3 + k], (px, py, c))
                cp.start()
                out_ici.append(cp)
        passed = []
        for t in range(n):
            for k, (px, py) in enumerate(chips):
                got = block(t, 2 * px + py, c)
                copy(ins[t].at[c], got, s_ici.at[t * 3 + k], r_ici.at[t * 3 + k], (px, py, c)).wait_recv()
                cp = copy(got, got, s_fwd.at[t * 3 + k], r_fwd.at[t * 3 + k], sib)
                cp.start()
                passed.append(cp)
        for t in range(n):
            for k, (px, py) in enumerate(chips):
                theirs = block(t, 2 * px + py, 1 - c)
                copy(theirs, theirs, s_fwd.at[t * 3 + k], r_fwd.at[t * 3 + k], sib).wait_recv()
        for cp in own:
            cp.wait()
        for cp in out_ici + passed:
            cp.wait_send()

    return pl.pallas_call(
        body, name="gather_weights",
        in_specs=[ANY] * n, out_specs=[ANY] * n, out_shape=full_shapes,
        scratch_shapes=[pltpu.SemaphoreType.DMA((n,)), pltpu.SemaphoreType.DMA((n,)),
                        pltpu.SemaphoreType.DMA((n * 3,)), pltpu.SemaphoreType.DMA((n * 3,)),
                        pltpu.SemaphoreType.DMA((n * 3,)), pltpu.SemaphoreType.DMA((n * 3,))],
    )(*shards)


def _pair_exchange(grads):
    n = len(grads)

    def body(*refs):
        ins, outs = refs[:n], refs[n:2 * n]
        ssem, rsem = refs[2 * n:]
        x, y, c, _ = _position()
        cps = [pltpu.make_async_remote_copy(
            src_ref=ins[t].at[1 - c], dst_ref=outs[t], send_sem=ssem.at[t], recv_sem=rsem.at[t],
            device_id=(x, y, 1 - c), device_id_type=MESH) for t in range(n)]
        for cp in cps:
            cp.start()
        for cp in cps:
            cp.wait()

    return pl.pallas_call(
        body, name="pair_exchange",
        in_specs=[ANY] * n, out_specs=[ANY] * n,
        out_shape=[jax.ShapeDtypeStruct(g.shape[1:], g.dtype) for g in grads],
        scratch_shapes=[pltpu.SemaphoreType.DMA((n,)), pltpu.SemaphoreType.DMA((n,))],
    )(*grads)


def _block_exchange(sums, axes):
    n = len(sums)
    out_shapes = []
    for s, ax in zip(sums, axes):
        shp = list(s.shape)
        shp[ax] //= N_CHIPS
        out_shapes.append(jax.ShapeDtypeStruct((3,) + tuple(shp), s.dtype))

    def body(*refs):
        ins, outs = refs[:n], refs[n:2 * n]
        ssem, rsem = refs[2 * n:]
        x, y, c, chips = _position()
        cps = []
        for t in range(n):
            size = ins[t].shape[axes[t]] // N_CHIPS
            for k, (px, py) in enumerate(chips):
                cps.append(pltpu.make_async_remote_copy(
                    src_ref=_block_view(ins[t], axes[t], 2 * px + py, size), dst_ref=outs[t].at[k],
                    send_sem=ssem.at[t * 3 + k], recv_sem=rsem.at[t * 3 + k],
                    device_id=(px, py, c), device_id_type=MESH))
        for cp in cps:
            cp.start()
        for cp in cps:
            cp.wait()

    return pl.pallas_call(
        body, name="block_exchange",
        in_specs=[ANY] * n, out_specs=[ANY] * n, out_shape=out_shapes,
        scratch_shapes=[pltpu.SemaphoreType.DMA((n * 3,)), pltpu.SemaphoreType.DMA((n * 3,))],
    )(*sums)


def _sibling_share(halves):
    n = len(halves)

    def body(*refs):
        outs = refs[n:2 * n]
        ssem, rsem = refs[2 * n:]
        x, y, c, _ = _position()
        cps = [pltpu.make_async_remote_copy(
            src_ref=outs[t].at[c], dst_ref=outs[t].at[c], send_sem=ssem.at[t], recv_sem=rsem.at[t],
            device_id=(x, y, 1 - c), device_id_type=MESH) for t in range(n)]
        for cp in cps:
            cp.start()
        for cp in cps:
            cp.wait()

    return pl.pallas_call(
        body, name="sibling_share",
        in_specs=[ANY] * n, out_specs=[ANY] * n,
        out_shape=[jax.ShapeDtypeStruct(h.shape, h.dtype) for h in halves],
        input_output_aliases={t: t for t in range(n)},
        scratch_shapes=[pltpu.SemaphoreType.DMA((n,)), pltpu.SemaphoreType.DMA((n,))],
    )(*halves)


def _small_allreduce(p):
    R, C = p.shape

    def body(p_ref, o_ref, sib_buf, chip_sums, ssem, rsem):
        x, y, c, chips = _position()
        j = 2 * x + y
        sib = pltpu.make_async_remote_copy(
            src_ref=p_ref, dst_ref=sib_buf, send_sem=ssem.at[0], recv_sem=rsem.at[0],
            device_id=(x, y, 1 - c), device_id_type=MESH)
        sib.start()
        sib.wait()
        chip_sums[j] = p_ref[...] + sib_buf[...]
        cps = [pltpu.make_async_remote_copy(
            src_ref=chip_sums.at[j], dst_ref=chip_sums.at[j], send_sem=ssem.at[1 + k], recv_sem=rsem.at[1 + k],
            device_id=(px, py, c), device_id_type=MESH) for k, (px, py) in enumerate(chips)]
        for cp in cps:
            cp.start()
        for k, (px, py) in enumerate(chips):
            pltpu.make_async_remote_copy(
                src_ref=chip_sums.at[j], dst_ref=chip_sums.at[2 * px + py],
                send_sem=ssem.at[1 + k], recv_sem=rsem.at[1 + k],
                device_id=(px, py, c), device_id_type=MESH).wait()
        o_ref[...] = ((chip_sums[0] + chip_sums[1]) + chip_sums[2]) + chip_sums[3]

    vm = pl.BlockSpec(memory_space=pltpu.VMEM)
    return pl.pallas_call(
        body, name="small_allreduce",
        in_specs=[vm], out_specs=vm, out_shape=jax.ShapeDtypeStruct((R, C), F32),
        scratch_shapes=[pltpu.VMEM((R, C), F32), pltpu.VMEM((N_CHIPS, R, C), F32),
                        pltpu.SemaphoreType.DMA((4,)), pltpu.SemaphoreType.DMA((4,))],
        compiler_params=pltpu.CompilerParams(vmem_limit_bytes=VMEM_LIMIT_BYTES),
    )(p)


def _add_own_layer(g, recv, pos):
    _, B, R, C = g.shape
    tr = _tile(R, 256)
    tc = _tile(C, 2816)

    def body(pos_ref, g_ref, r_ref, o_ref, ob_ref):
        del pos_ref
        s = g_ref[...] + r_ref[...]
        o_ref[...] = s
        ob_ref[...] = s.astype(BF16)

    spec = lambda: pl.BlockSpec((None, tr, tc), lambda b, i, j, pos_ref: (b, i, j))
    return pl.pallas_call(
        body, name="add_own_layer",
        grid_spec=pltpu.PrefetchScalarGridSpec(
            num_scalar_prefetch=1, grid=(B, R // tr, C // tc),
            in_specs=[pl.BlockSpec((None, None, tr, tc), lambda b, i, j, pos_ref: (pos_ref[0], b, i, j)), spec()],
            out_specs=[spec(), spec()]),
        out_shape=[jax.ShapeDtypeStruct((B, R, C), F32), jax.ShapeDtypeStruct((B, R, C), BF16)],
        compiler_params=_params("parallel", "parallel", "parallel"),
    )(pos, g, recv)


def _sum_blocks(own, recv, axis, pos):
    _, B, rs, cs = recv.shape
    tr = _tile(rs, 128)
    nr = rs // tr

    def body(pos_ref, o_ref, r0_ref, r1_ref, r2_ref, out_ref):
        del pos_ref
        out_ref[...] = (((o_ref[...] + r0_ref[...].astype(F32)) + r1_ref[...].astype(F32))
                        + r2_ref[...].astype(F32))

    if axis == 2:
        own_spec = pl.BlockSpec((None, tr, cs), lambda b, i, pos_ref: (b, i, pos_ref[1]))
    else:
        own_spec = pl.BlockSpec((None, tr, cs), lambda b, i, pos_ref: (b, pos_ref[1] * nr + i, 0))
    rspec = lambda k: pl.BlockSpec((None, None, tr, cs), lambda b, i, pos_ref: (k, b, i, 0))
    return pl.pallas_call(
        body, name="sum_blocks",
        grid_spec=pltpu.PrefetchScalarGridSpec(
            num_scalar_prefetch=1, grid=(B, nr),
            in_specs=[own_spec, rspec(0), rspec(1), rspec(2)],
            out_specs=pl.BlockSpec((None, None, tr, cs), lambda b, i, pos_ref: (pos_ref[0], b, i, 0))),
        out_shape=jax.ShapeDtypeStruct((N_LAYERS, B, rs, cs), F32),
        compiler_params=_params("parallel", "parallel"),
    )(pos, own, recv, recv, recv)


def _adamw(w, g, m, v):
    shape = w.shape
    C = shape[-1]
    R = w.size // C
    tr = R
    for cand in (512, 256, 128, 64, 32, 16, 8):
        if R % cand == 0 and cand * C * 4 <= (1 << 20):
            tr = cand
            break
    bc1 = 1.0 - ADAM_B1 ** ADAM_STEP
    bc2 = 1.0 - ADAM_B2 ** ADAM_STEP

    def body(w_ref, g_ref, m_ref, v_ref, d_ref, mo_ref, vo_ref):
        gg = g_ref[...]
        mn = ADAM_B1 * m_ref[...] + (1.0 - ADAM_B1) * gg
        vn = ADAM_B2 * v_ref[...] + (1.0 - ADAM_B2) * (gg * gg)
        mo_ref[...] = mn
        vo_ref[...] = vn
        d_ref[...] = -ADAM_LR * ((mn / bc1) / (jnp.sqrt(vn / bc2) + ADAM_EPS) + ADAM_WD * w_ref[...])

    spec = lambda: pl.BlockSpec((tr, C), lambda i: (i, 0))
    outs = pl.pallas_call(
        body, name="adamw", grid=(R // tr,),
        in_specs=[spec()] * 4, out_specs=[spec()] * 3,
        out_shape=[jax.ShapeDtypeStruct((R, C), F32)] * 3,
        compiler_params=_params("parallel"),
    )(*[a.reshape(R, C) for a in (w, g, m, v)])
    return tuple(o.reshape(shape) for o in outs)


def kernel(x, mem, norm_g, mem_norm_g, w_in, gmlp_ln_g, gmlp_ln_b, w_s, b_s, conv_w, conv_b, conv_ln_g, conv_ln_b, w_kv, w_branch, w_out, final_norm_g, loss_target, m_norm_g, m_mem_norm_g, m_w_in, m_gmlp_ln_g, m_gmlp_ln_b, m_w_s, m_b_s, m_conv_w, m_conv_b, m_conv_ln_g, m_conv_ln_b, m_w_kv, m_w_branch, m_w_out, m_final_norm_g, v_norm_g, v_mem_norm_g, v_w_in, v_gmlp_ln_g, v_gmlp_ln_b, v_w_s, v_b_s, v_conv_w, v_conv_b, v_conv_ln_g, v_conv_ln_b, v_w_kv, v_w_branch, v_w_out, v_final_norm_g):
    xs, mems, tgt = x[0], mem[0], loss_target[0]
    S, D = xs.shape
    assert D // GROUPS == LANES and S % CHUNK == 0 and w_s.shape[-1] == CHUNK
    L = N_LAYERS

    win_f, wkv_f, wb_f, wo_f, cw_f = _gather_weights(
        [w_in.astype(BF16), w_kv.astype(BF16), w_branch.astype(BF16), w_out.astype(BF16), conv_w],
        [2, 2, 2, 1, 2])

    tri = jnp.tril(jnp.ones((CHUNK, CHUNK), F32))
    t_mix = (w_s * tri).astype(BF16)
    t_mix_t = jnp.swapaxes(t_mix, -1, -2)
    bias_full = jnp.repeat(jnp.swapaxes(b_s, -1, -2), D // GROUPS, axis=-1)
    row = lambda a, l: a[l][None, :]

    saved = []
    h = xs
    for l in range(L):
        z, hb = _in_proj(h, row(norm_g, l), win_f, l)
        bra = _branch_a_fwd(z, row(gmlp_ln_g, l), row(gmlp_ln_b, l), t_mix[l], bias_full[l])
        c0 = _conv_fwd(z, cw_f, row(conv_b, l), l)
        brb = _branch_b_fwd(c0, z, row(conv_ln_g, l), row(conv_ln_b, l))
        mn, kv, kvt = _kv_fwd(mems, row(mem_norm_g, l), wkv_f, l)
        brc = _branch_c_fwd(z, kv, kvt)
        merged, h_next = _merge_fwd(bra, brb, brc, z, h, wb_f, wo_f, l)
        saved.append((h, z, hb, bra, c0, brb, mn, kv, kvt, brc, merged))
        h = h_next

    dx, loss_cols, d_final_g = _loss_bwd(h, tgt, final_norm_g[None, :])
    loss = lax.psum(0.5 * jnp.sum(loss_cols) / D, ("x", "y", "c"))

    g_in = g_kv = g_b = g_o = None
    small = {k: [None] * L for k in ("norm_g", "mem_norm_g", "gmlp_ln_g", "gmlp_ln_b", "w_s", "b_s", "conv_w",
                                     "conv_b", "conv_ln_g", "conv_ln_b")}
    for l in reversed(range(L)):
        h_in, z, hb, bra, c0, brb, mn, kv, kvt, brc, merged = saved[l]
        dz_m, dproj, dbr = _merge_bwd(dx, bra, brb, brc, z, wb_f, wo_f, l)
        g_o = _wgrad(merged, dx, (L, D, D), l, 0, g_o)
        for n, br in enumerate((bra, brb, brc)):
            g_b = _wgrad(br, dproj, (L * N_BRANCHES, D, D), l * N_BRANCHES + n, 0, g_b, b_plane=n)
        dz_a, dws, dbs, dlg, dlb = _branch_a_bwd(z, dbr, row(gmlp_ln_g, l), row(gmlp_ln_b, l),
                                                t_mix[l], t_mix_t[l], bias_full[l], tri)
        dc0, dz_b, dcg, dcbeta, dcb = _branch_b_bwd(c0, z, dbr, row(conv_ln_g, l), row(conv_ln_b, l))
        dz_b, dcw = _conv_bwd(dc0, z, cw_f, dz_b, l)
        dz_c, dk, dv = _branch_c_bwd(z, dbr, kv, kvt)
        g_kv, dmg = _kv_bwd(dk, dv, mn, mems, wkv_f, l, g_kv)
        pieces = [(dz_a, 0, 3), (dz_b, 3, 3), (dz_c, 6, 2), (dz_m, 8, 3)]
        for dzp, lo, _ in pieces:
            g_in = _wgrad(hb, dzp, (L, D, N_PLANES * D), l, lo, g_in)
        dx, dng = _in_proj_bwd(pieces, win_f, h_in, row(norm_g, l), dx, l)
        for k, val in (("norm_g", dng), ("mem_norm_g", dmg), ("gmlp_ln_g", dlg), ("gmlp_ln_b", dlb), ("w_s", dws),
                       ("b_s", dbs), ("conv_w", dcw), ("conv_b", dcb), ("conv_ln_g", dcg), ("conv_ln_b", dcbeta)):
            small[k][l] = val
    grad_x = dx[None]

    xi, yi, ci = lax.axis_index("x"), lax.axis_index("y"), lax.axis_index("c")
    pos = jnp.stack([ci, 2 * xi + yi]).astype(jnp.int32)
    big = [g_in[:, None], g_kv[:, None], g_b.reshape(L, N_BRANCHES, D, D), g_o[:, None]]
    shard_axis = [2, 2, 1, 1]
    recv = _pair_exchange(big)
    sums = [_add_own_layer(g, r, pos) for g, r in zip(big, recv)]
    blocks = _block_exchange([sb for _, sb in sums], shard_axis)
    reduced = [_sum_blocks(s, b, ax, pos) for (s, _), b, ax in zip(sums, blocks, shard_axis)]
    gs_in, gs_kv, gs_b, gs_o = _sibling_share(reduced)
    grad_w_in = gs_in.reshape(w_in.shape)
    grad_w_kv = gs_kv.reshape(w_kv.shape)
    grad_w_branch = gs_b.reshape(w_branch.shape)
    grad_w_out = gs_o.reshape(w_out.shape)

    order = [("norm_g", norm_g.shape), ("mem_norm_g", mem_norm_g.shape), ("gmlp_ln_g", gmlp_ln_g.shape),
             ("gmlp_ln_b", gmlp_ln_b.shape), ("w_s", w_s.shape), ("b_s", b_s.shape),
             ("conv_w", (L, CONV_K, D)), ("conv_b", conv_b.shape), ("conv_ln_g", conv_ln_g.shape),
             ("conv_ln_b", conv_ln_b.shape)]
    parts = [jnp.stack([v.reshape(shp[1:]) for v in small[k]]).reshape(-1, LANES) for k, shp in order]
    parts.append(d_final_g.reshape(-1, LANES))
    sizes = [p.shape[0] for p in parts]
    assert all(s % 8 == 0 for s in sizes)
    total = _small_allreduce(jnp.concatenate(parts, axis=0))
    red, off = {}, 0
    for (k, shp), n in zip(order + [("final_norm_g", final_norm_g.shape)], sizes):
        red[k] = total[off:off + n].reshape(shp)
        off += n
    cs = conv_w.shape[-1]
    red["conv_w"] = lax.dynamic_slice_in_dim(red["conv_w"], (2 * xi + yi) * cs, cs, axis=2)
    red.update(w_in=grad_w_in, w_kv=grad_w_kv, w_branch=grad_w_branch, w_out=grad_w_out)

    names = ["norm_g", "mem_norm_g", "w_in", "gmlp_ln_g", "gmlp_ln_b", "w_s", "b_s", "conv_w", "conv_b",
             "conv_ln_g", "conv_ln_b", "w_kv", "w_branch", "w_out", "final_norm_g"]
    weights = dict(norm_g=norm_g, mem_norm_g=mem_norm_g, w_in=w_in, gmlp_ln_g=gmlp_ln_g, gmlp_ln_b=gmlp_ln_b,
                   w_s=w_s, b_s=b_s, conv_w=conv_w, conv_b=conv_b, conv_ln_g=conv_ln_g, conv_ln_b=conv_ln_b,
                   w_kv=w_kv, w_branch=w_branch, w_out=w_out, final_norm_g=final_norm_g)
    ms = dict(norm_g=m_norm_g, mem_norm_g=m_mem_norm_g, w_in=m_w_in, gmlp_ln_g=m_gmlp_ln_g, gmlp_ln_b=m_gmlp_ln_b,
              w_s=m_w_s, b_s=m_b_s, conv_w=m_conv_w, conv_b=m_conv_b, conv_ln_g=m_conv_ln_g, conv_ln_b=m_conv_ln_b,
              w_kv=m_w_kv, w_branch=m_w_branch, w_out=m_w_out, final_norm_g=m_final_norm_g)
    vs = dict(norm_g=v_norm_g, mem_norm_g=v_mem_norm_g, w_in=v_w_in, gmlp_ln_g=v_gmlp_ln_g, gmlp_ln_b=v_gmlp_ln_b,
              w_s=v_w_s, b_s=v_b_s, conv_w=v_conv_w, conv_b=v_conv_b, conv_ln_g=v_conv_ln_g, conv_ln_b=v_conv_ln_b,
              w_kv=v_w_kv, w_branch=v_w_branch, w_out=v_w_out, final_norm_g=v_final_norm_g)
    deltas, new_m, new_v = [], [], []
    for k in names:
        w2 = weights[k] if weights[k].ndim > 1 else weights[k][None, :]
        d, mo, vo = _adamw(w2, red[k].reshape(w2.shape), ms[k].reshape(w2.shape), vs[k].reshape(w2.shape))
        deltas.append(d.reshape(weights[k].shape))
        new_m.append(mo.reshape(weights[k].shape))
        new_v.append(vo.reshape(weights[k].shape))
    grads = [red[k].reshape(weights[k].shape) for k in names]
    return (loss, grad_x, *grads, *deltas, *new_m, *new_v)
```

```python
import math

import jax
import jax.numpy as jnp
from jax import lax
from jax.experimental import pallas as pl
from jax.experimental.pallas import tpu as pltpu

F32 = jnp.float32
BF16 = jnp.bfloat16

N_LAYERS = 2
N_BRANCHES = 3
N_PLANES = 11
N_CHIPS = 4
CHUNK = 128
GROUPS = 8
HEADS = 4
CONV_K = 31
HALO = 32
LANES = 128
RMS_EPS = 1e-6
LN_EPS = 1e-5
ADAM_LR, ADAM_B1, ADAM_B2, ADAM_EPS, ADAM_WD, ADAM_STEP = 0.001, 0.9, 0.999, 1e-08, 0.01, 10
VMEM_LIMIT_BYTES = 48 * 1024 * 1024
MESH = pl.DeviceIdType.MESH
ANY = pl.BlockSpec(memory_space=pl.ANY)


def _params(*sem):
    return pltpu.CompilerParams(dimension_semantics=sem, vmem_limit_bytes=VMEM_LIMIT_BYTES)


def _tile(n, pref):
    t = min(n, pref)
    assert n % t == 0, (n, t)
    return t


def _dot(a, b):
    return jnp.dot(a, b, preferred_element_type=F32)


def _dot_tb(a, b):
    return lax.dot_general(a, b, (((1,), (1,)), ((), ())), preferred_element_type=F32)


def _dot_ta(a, b):
    return lax.dot_general(a, b, (((0,), (0,)), ((), ())), preferred_element_type=F32)


def _sigmoid(x):
    return 1.0 / (1.0 + jnp.exp(-x))


def _silu(x):
    return x * _sigmoid(x)


def _silu_and_grad(x):
    s = _sigmoid(x)
    return x * s, s * (1.0 + x * (1.0 - s))


_GELU_C = math.sqrt(2.0 / math.pi)
_GELU_A = 0.044715


def _gelu(x):
    return x * (0.5 * (1.0 + jnp.tanh(_GELU_C * (x + _GELU_A * (x * x * x)))))


def _gelu_and_grad(x):
    x2 = x * x
    t = jnp.tanh(_GELU_C * (x + _GELU_A * (x2 * x)))
    cdf = 0.5 * (1.0 + t)
    dcdf = 0.5 * (1.0 - t * t) * (_GELU_C * (1.0 + 3.0 * _GELU_A * x2))
    return x * cdf, cdf + x * dcdf


def _rows(i, n):
    return pl.ds(pl.multiple_of(i * n, n), n)


def _grouped_loop(n, group, body, init):
    assert n % group == 0

    def trip(i, c):
        for u in range(group):
            c = body(i * group + u, c)
        return c
    return lax.fori_loop(0, n // group, trip, init)


def _mean(x):
    return jnp.mean(x, axis=-1, keepdims=True)


def _colsum(x):
    return jnp.sum(x, axis=0, keepdims=True)


def _tok(ts, d):
    return pl.BlockSpec((ts, d), lambda i: (i, 0))


def _zcol(ts, d, n):
    return pl.BlockSpec((ts, d), lambda i: (i, n))


def _row1(d):
    return pl.BlockSpec((1, d), lambda i: (0, 0))


def _plane(ts, d, n):
    return pl.BlockSpec((None, ts, d), lambda i: (n, i, 0))


def _in_proj(x, g, w_all, l):
    S, D = x.shape
    N = w_all.shape[2]
    tm, tn = _tile(S, 1024), D

    def body(x_ref, g_ref, w_ref, z_ref, hb_ref, h_scr):
        @pl.when(pl.program_id(1) == 0)
        def _():
            def slab(i, c):
                r = _rows(i, 16)
                xs = x_ref[r, :]
                h = (xs * lax.rsqrt(_mean(xs * xs) + RMS_EPS)) * g_ref[...]
                hb = h.astype(BF16)
                h_scr[r, :] = hb
                hb_ref[r, :] = hb
                return c
            _grouped_loop(tm // 16, 4, slab, 0)
        z_ref[...] = _dot(h_scr[...], w_ref[...])

    return pl.pallas_call(
        body, name="in_proj", grid=(S // tm, N // tn),
        in_specs=[pl.BlockSpec((tm, D), lambda i, j: (i, 0)),
                  pl.BlockSpec((1, D), lambda i, j: (0, 0)),
                  pl.BlockSpec((None, D, tn), lambda i, j: (l, 0, j))],
        out_specs=[pl.BlockSpec((tm, tn), lambda i, j: (i, j)),
                   pl.BlockSpec((tm, D), lambda i, j: (i, 0))],
        out_shape=[jax.ShapeDtypeStruct((S, N), F32), jax.ShapeDtypeStruct((S, D), BF16)],
        scratch_shapes=[pltpu.VMEM((tm, D), BF16)],
        compiler_params=_params("parallel", "arbitrary"),
    )(x, g, w_all)


def _branch_a_fwd(z, lng, lnb, t_mix, bias_full):
    S = z.shape[0]
    D = lng.shape[1]
    GD = D // GROUPS
    ts = _tile(S, 256)

    def body(zu_ref, zv_ref, zg_ref, lng_ref, lnb_ref, t_ref, bias_ref, a_ref, v_scr):
        def chunk(ci, carry):
            c0 = pl.multiple_of(ci * CHUNK, CHUNK)

            def slab(si, c):
                r = pl.multiple_of(si * 16, 16)
                vg = _gelu(zv_ref[pl.ds(pl.multiple_of(c0 + r, 16), 16), :])
                xc = vg - _mean(vg)
                y = xc * lax.rsqrt(_mean(xc * xc) + LN_EPS)
                v_scr[pl.ds(r, 16), :] = (y * lng_ref[...] + lnb_ref[...]).astype(BF16)
                return c
            _grouped_loop(CHUNK // 16, 4, slab, 0)
            for g in range(GROUPS):
                cols = slice(g * GD, (g + 1) * GD)
                sv = _dot(t_ref[g], v_scr[:, cols]) + bias_ref[:, cols]
                u = _gelu(zu_ref[pl.ds(c0, CHUNK), cols])
                a_ref[pl.ds(c0, CHUNK), cols] = ((u * sv) * _silu(zg_ref[pl.ds(c0, CHUNK), cols])).astype(BF16)
            return carry
        lax.fori_loop(0, ts // CHUNK, chunk, 0)

    return pl.pallas_call(
        body, name="branch_a_fwd", grid=(S // ts,),
        in_specs=[_zcol(ts, D, 0), _zcol(ts, D, 1), _zcol(ts, D, 2), _row1(D), _row1(D),
                  pl.BlockSpec((GROUPS, CHUNK, CHUNK), lambda i: (0, 0, 0)),
                  pl.BlockSpec((CHUNK, D), lambda i: (0, 0))],
        out_specs=_tok(ts, D),
        out_shape=jax.ShapeDtypeStruct((S, D), BF16),
        scratch_shapes=[pltpu.VMEM((CHUNK, D), BF16)],
        compiler_params=_params("parallel"),
    )(z, z, z, lng, lnb, t_mix, bias_full)


SH = HALO - CONV_K + 1


def _fill_shifts(sh_scr, src_scr, cols, ts):
    for j in range(1, 8):
        for g in range(ts // 32):
            sh_scr[j - 1, pl.ds(g * 32, 32), :] = src_scr[pl.ds(g * 32 + j, 32), cols]
        sh_scr[j - 1, pl.ds(ts, HALO - 8), :] = src_scr[pl.ds(ts + j, HALO - 8), cols]


def _shifted(sh_scr, src_scr, cols, row0, n):
    q, j = divmod(row0, 8)
    if j == 0:
        return src_scr[pl.ds(row0, n), cols]
    return sh_scr[j - 1, pl.ds(8 * q, n), :]


def _conv_fwd(z, w_all, cb, l):
    S = z.shape[0]
    D = cb.shape[1]
    ts = _tile(S, 256)
    CB = 256
    hb = ts // HALO

    def body(za_ref, zb_ref, zap_ref, zbp_ref, w_ref, b_ref, c0_ref, glu_scr, sh_scr):
        i = pl.program_id(0)
        prev = zap_ref[...] * _sigmoid(zbp_ref[...])
        glu_scr[0:HALO, :] = jnp.where(i == 0, 0.0, prev)

        def fill(si, c):
            r = _rows(si, 32)
            glu_scr[pl.ds(pl.multiple_of(HALO + si * 32, 32), 32), :] = za_ref[r, :] * _sigmoid(zb_ref[r, :])
            return c
        _grouped_loop(ts // 32, 2, fill, 0)

        def colblock(ci, c):
            cols = _rows(ci, CB)
            _fill_shifts(sh_scr, glu_scr, cols, ts)
            for s in range(ts // 32):
                acc = jnp.zeros((32, CB), F32)
                for k in range(CONV_K):
                    acc = acc + _shifted(sh_scr, glu_scr, cols, s * 32 + k + SH, 32) * w_ref[k:k + 1, cols]
                c0_ref[pl.ds(s * 32, 32), cols] = acc + b_ref[:, cols]
            return c
        lax.fori_loop(0, D // CB, colblock, 0)

    prev_map = lambda n: (lambda i: (jnp.maximum(i * hb - 1, 0), n))
    return pl.pallas_call(
        body, name="conv_fwd", grid=(S // ts,),
        in_specs=[_zcol(ts, D, 3), _zcol(ts, D, 4),
                  pl.BlockSpec((HALO, D), prev_map(3)), pl.BlockSpec((HALO, D), prev_map(4)),
                  pl.BlockSpec((None, CONV_K, D), lambda i: (l, 0, 0)), _row1(D)],
        out_specs=_tok(ts, D),
        out_shape=jax.ShapeDtypeStruct((S, D), F32),
        scratch_shapes=[pltpu.VMEM((ts + HALO, D), F32), pltpu.VMEM((7, ts + HALO, CB), F32)],
        compiler_params=_params("parallel"),
    )(z, z, z, z, w_all, cb)


def _branch_b_fwd(c0, z, g, b):
    S, D = c0.shape
    ts = _tile(S, 512)

    def body(c0_ref, zg_ref, g_ref, b_ref, o_ref):
        def slab(si, c):
            r = _rows(si, 16)
            v = c0_ref[r, :]
            xc = v - _mean(v)
            c1 = (xc * lax.rsqrt(_mean(xc * xc) + LN_EPS)) * g_ref[...] + b_ref[...]
            o_ref[r, :] = (_silu(c1) * _silu(zg_ref[r, :])).astype(BF16)
            return c
        _grouped_loop(ts // 16, 4, slab, 0)

    return pl.pallas_call(
        body, name="branch_b_fwd", grid=(S // ts,),
        in_specs=[_tok(ts, D), _zcol(ts, D, 5), _row1(D), _row1(D)],
        out_specs=_tok(ts, D),
        out_shape=jax.ShapeDtypeStruct((S, D), BF16),
        compiler_params=_params("parallel"),
    )(c0, z, g, b)


def _kv_fwd(mem, mg, wkv_all, l):
    M, D = mem.shape

    def body(mem_ref, g_ref, w_ref, mn_ref, kv_ref, kvt_ref):
        m = mem_ref[...]
        mn = ((m * lax.rsqrt(_mean(m * m) + RMS_EPS)) * g_ref[...]).astype(BF16)
        mn_ref[...] = mn
        kv = _dot(mn, w_ref[...])
        kv_ref[...] = kv.astype(BF16)
        kvt_ref[...] = kv.T.astype(BF16)

    return pl.pallas_call(
        body, name="kv_fwd", grid=(1,),
        in_specs=[pl.BlockSpec((M, D), lambda i: (0, 0)), _row1(D),
                  pl.BlockSpec((None, D, 2 * D), lambda i: (l, 0, 0))],
        out_specs=[pl.BlockSpec((M, D), lambda i: (0, 0)), pl.BlockSpec((M, 2 * D), lambda i: (0, 0)),
                   pl.BlockSpec((2 * D, M), lambda i: (0, 0))],
        out_shape=[jax.ShapeDtypeStruct((M, D), BF16), jax.ShapeDtypeStruct((M, 2 * D), BF16),
                   jax.ShapeDtypeStruct((2 * D, M), BF16)],
        compiler_params=_params("arbitrary"),
    )(mem, mg, wkv_all)


def _softmax_rows(s):
    e = jnp.exp(s - jnp.max(s, axis=-1, keepdims=True))
    return e / jnp.sum(e, axis=-1, keepdims=True)


def _branch_c_fwd(z, kv, kvt):
    S = z.shape[0]
    M, D2 = kv.shape
    D = D2 // 2
    HD = D // HEADS
    scale = 1.0 / math.sqrt(HD)
    ts = _tile(S, 256)

    def body(q_ref, zg_ref, kt_ref, v_ref, o_ref):
        for h in range(HEADS):
            cols = slice(h * HD, (h + 1) * HD)
            p = _softmax_rows(_dot(q_ref[:, cols].astype(BF16), kt_ref[cols, :]) * scale)
            att = _dot(p.astype(BF16), v_ref[:, cols])
            o_ref[:, cols] = (att * _silu(zg_ref[:, cols])).astype(BF16)

    return pl.pallas_call(
        body, name="branch_c_fwd", grid=(S // ts,),
        in_specs=[_zcol(ts, D, 6), _zcol(ts, D, 7),
                  pl.BlockSpec((D, M), lambda i: (0, 0)), pl.BlockSpec((M, D), lambda i: (0, 1))],
        out_specs=_tok(ts, D),
        out_shape=jax.ShapeDtypeStruct((S, D), BF16),
        compiler_params=_params("parallel"),
    )(z, z, kvt, kv)


def _merge_fwd(bra, brb, brc, z, x, wb_all, wo_all, l):
    S, D = x.shape
    ts = _tile(S, 256)
    CB = 256

    def body(a_ref, b_ref, c_ref, zm0_ref, zm1_ref, zm2_ref, x_ref, wb_ref, wo_ref, m_ref, xo_ref):
        brs = (a_ref, b_ref, c_ref)
        zms = (zm0_ref, zm1_ref, zm2_ref)
        for cb in range(D // CB):
            cols = slice(cb * CB, (cb + 1) * CB)
            acc = None
            for n in range(N_BRANCHES):
                t = _sigmoid(zms[n][:, cols]) * _dot(brs[n][...], wb_ref[n, :, cols])
                acc = t if acc is None else acc + t
            m_ref[:, cols] = acc.astype(BF16)
        for cb in range(D // CB):
            cols = slice(cb * CB, (cb + 1) * CB)
            xo_ref[:, cols] = x_ref[:, cols] + _dot(m_ref[...], wo_ref[:, cols])

    return pl.pallas_call(
        body, name="merge_fwd", grid=(S // ts,),
        in_specs=[_tok(ts, D), _tok(ts, D), _tok(ts, D), _zcol(ts, D, 8), _zcol(ts, D, 9), _zcol(ts, D, 10),
                  _tok(ts, D),
                  pl.BlockSpec((None, N_BRANCHES, D, D), lambda i: (l, 0, 0, 0)),
                  pl.BlockSpec((None, D, D), lambda i: (l, 0, 0))],
        out_specs=[_tok(ts, D), _tok(ts, D)],
        out_shape=[jax.ShapeDtypeStruct((S, D), BF16), jax.ShapeDtypeStruct((S, D), F32)],
        compiler_params=_params("parallel"),
    )(bra, brb, brc, z, z, z, x, wb_all, wo_all)


def _loss_bwd(x, target, fg):
    S, D = x.shape
    ts = _tile(S, 512)

    def body(x_ref, t_ref, g_ref, dx_ref, ls_ref, dg_ref):
        @pl.when(pl.program_id(0) == 0)
        def _():
            ls_ref[...] = jnp.zeros_like(ls_ref)
            dg_ref[...] = jnp.zeros_like(dg_ref)

        def slab(si, c):
            ls, dg = c
            r = _rows(si, 16)
            xs = x_ref[r, :]
            rs = lax.rsqrt(_mean(xs * xs) + RMS_EPS)
            n = xs * rs
            e = n * g_ref[...] - t_ref[r, :]
            dy = e * (1.0 / D)
            dn = dy * g_ref[...]
            dx_ref[r, :] = rs * (dn - n * _mean(dn * n))
            return ls + _colsum(e * e), dg + _colsum(dy * n)
        zero = jnp.zeros((1, D), F32)
        ls, dg = _grouped_loop(ts // 16, 4, slab, (zero, zero))
        ls_ref[...] += ls
        dg_ref[...] += dg

    return pl.pallas_call(
        body, name="loss_bwd", grid=(S // ts,),
        in_specs=[_tok(ts, D), _tok(ts, D), _row1(D)],
        out_specs=[_tok(ts, D), _row1(D), _row1(D)],
        out_shape=[jax.ShapeDtypeStruct((S, D), F32), jax.ShapeDtypeStruct((1, D), F32),
                   jax.ShapeDtypeStruct((1, D), F32)],
        compiler_params=_params("arbitrary"),
    )(x, target, fg)


def _merge_bwd(dxp, bra, brb, brc, z, wb_all, wo_all, l):
    S, D = dxp.shape
    ts = _tile(S, 256)
    CB = 256

    def body(d_ref, a_ref, b_ref, c_ref, zm0_ref, zm1_ref, zm2_ref, wb_ref, wo_ref,
             dzm_ref, dproj_ref, dbr_ref, dxb_scr):
        brs = (a_ref, b_ref, c_ref)
        zms = (zm0_ref, zm1_ref, zm2_ref)
        dxb_scr[...] = d_ref[...].astype(BF16)
        for cb in range(D // CB):
            cols = slice(cb * CB, (cb + 1) * CB)
            dm = _dot_tb(dxb_scr[...], wo_ref[cols, :])
            for n in range(N_BRANCHES):
                proj = _dot(brs[n][...], wb_ref[n, :, cols])
                g = _sigmoid(zms[n][:, cols])
                dzm_ref[:, n * D + cb * CB:n * D + (cb + 1) * CB] = ((dm * proj) * (g * (1.0 - g))).astype(BF16)
                dproj_ref[n, :, cols] = (dm * g).astype(BF16)
        for n in range(N_BRANCHES):
            for cb in range(D // CB):
                cols = slice(cb * CB, (cb + 1) * CB)
                dbr_ref[n, :, cols] = _dot_tb(dproj_ref[n], wb_ref[n, cols, :])

    tok3 = lambda: pl.BlockSpec((N_BRANCHES, ts, D), lambda i: (0, i, 0))
    return pl.pallas_call(
        body, name="merge_bwd", grid=(S // ts,),
        in_specs=[_tok(ts, D), _tok(ts, D), _tok(ts, D), _tok(ts, D),
                  _zcol(ts, D, 8), _zcol(ts, D, 9), _zcol(ts, D, 10),
                  pl.BlockSpec((None, N_BRANCHES, D, D), lambda i: (l, 0, 0, 0)),
                  pl.BlockSpec((None, D, D), lambda i: (l, 0, 0))],
        out_specs=[_tok(ts, 3 * D), tok3(), tok3()],
        out_shape=[jax.ShapeDtypeStruct((S, 3 * D), BF16), jax.ShapeDtypeStruct((N_BRANCHES, S, D), BF16),
                   jax.ShapeDtypeStruct((N_BRANCHES, S, D), F32)],
        scratch_shapes=[pltpu.VMEM((ts, D), BF16)],
        compiler_params=_params("parallel"),
    )(dxp, bra, brb, brc, z, z, z, wb_all, wo_all)


def _branch_a_bwd(z, dbr, lng, lnb, t_mix, t_mix_t, bias_full, mask):
    S = z.shape[0]
    D = lng.shape[1]
    GD = D // GROUPS
    ts = _tile(S, 256)
    nt = S // ts

    def body(zu_ref, zv_ref, zg_ref, d_ref, lng_ref, lnb_ref, t_ref, tt_ref, bias_ref, mask_ref,
             dz_ref, dws_ref, dbs_ref, dlg_ref, dlb_ref,
             v_scr, vn_scr, rstd_scr, dv_scr, dsv_acc):
        i = pl.program_id(0)

        @pl.when(i == 0)
        def _():
            dws_ref[...] = jnp.zeros_like(dws_ref)
            dlg_ref[...] = jnp.zeros_like(dlg_ref)
            dlb_ref[...] = jnp.zeros_like(dlb_ref)
            dsv_acc[...] = jnp.zeros_like(dsv_acc)

        def chunk(ci, carry):
            c0 = pl.multiple_of(ci * CHUNK, CHUNK)

            def ln_fwd(si, c):
                r = pl.multiple_of(si * 16, 16)
                vg = _gelu(zv_ref[pl.ds(pl.multiple_of(c0 + r, 16), 16), :])
                xc = vg - _mean(vg)
                rstd = lax.rsqrt(_mean(xc * xc) + LN_EPS)
                vn = xc * rstd
                vn_scr[pl.ds(r, 16), :] = vn
                rstd_scr[pl.ds(r, 16), :] = rstd
                v_scr[pl.ds(r, 16), :] = (vn * lng_ref[...] + lnb_ref[...]).astype(BF16)
                return c
            _grouped_loop(CHUNK // 16, 4, ln_fwd, 0)

            for g in range(GROUPS):
                cols = slice(g * GD, (g + 1) * GD)
                rows = pl.ds(c0, CHUNK)
                vb = v_scr[:, cols]
                sv = _dot(t_ref[g], vb) + bias_ref[:, cols]
                u, du = _gelu_and_grad(zu_ref[rows, cols])
                sg, dsg = _silu_and_grad(zg_ref[rows, cols])
                d = d_ref[rows, cols]
                dsv = (d * u) * sg
                dz_ref[rows, g * GD:(g + 1) * GD] = (((d * sv) * sg) * du).astype(BF16)
                dz_ref[rows, 2 * D + g * GD:2 * D + (g + 1) * GD] = (((d * u) * sv) * dsg).astype(BF16)
                dsvb = dsv.astype(BF16)
                dws_ref[g] += _dot_tb(dsvb, vb)
                dv_scr[:, cols] = _dot(tt_ref[g], dsvb)
                dsv_acc[:, cols] += dsv

            def ln_bwd(si, c):
                dlg, dlb = c
                r = pl.multiple_of(si * 16, 16)
                rr = pl.ds(r, 16)
                zrows = pl.ds(pl.multiple_of(c0 + r, 16), 16)
                dv = dv_scr[rr, :]
                vn = vn_scr[rr, :]
                dvn = dv * lng_ref[...]
                dvg = rstd_scr[rr, :] * ((dvn - _mean(dvn)) - vn * _mean(dvn * vn))
                _, gg = _gelu_and_grad(zv_ref[zrows, :])
                dz_ref[zrows, D:2 * D] = (dvg * gg).astype(BF16)
                return dlg + _colsum(dv * vn), dlb + _colsum(dv)
            zero = jnp.zeros((1, D), F32)
            dlg, dlb = _grouped_loop(CHUNK // 16, 4, ln_bwd, (zero, zero))
            dlg_ref[...] += dlg
            dlb_ref[...] += dlb
            return carry
        lax.fori_loop(0, ts // CHUNK, chunk, 0)

        @pl.when(i == nt - 1)
        def _():
            for g in range(GROUPS):
                cols = slice(g * GD, (g + 1) * GD)
                dws_ref[g] = dws_ref[g] * mask_ref[...]
                dbs_ref[g:g + 1, :] = _colsum(dsv_acc[:, cols].T)

    c2 = lambda shape: pl.BlockSpec(shape, lambda i: (0, 0))
    c3 = lambda: pl.BlockSpec((GROUPS, CHUNK, CHUNK), lambda i: (0, 0, 0))
    return pl.pallas_call(
        body, name="branch_a_bwd", grid=(nt,),
        in_specs=[_zcol(ts, D, 0), _zcol(ts, D, 1), _zcol(ts, D, 2), _plane(ts, D, 0), _row1(D), _row1(D),
                  c3(), c3(), c2((CHUNK, D)), c2((CHUNK, CHUNK))],
        out_specs=[_tok(ts, 3 * D), c3(), c2((GROUPS, CHUNK)), _row1(D), _row1(D)],
        out_shape=[jax.ShapeDtypeStruct((S, 3 * D), BF16), jax.ShapeDtypeStruct((GROUPS, CHUNK, CHUNK), F32),
                   jax.ShapeDtypeStruct((GROUPS, CHUNK), F32), jax.ShapeDtypeStruct((1, D), F32),
                   jax.ShapeDtypeStruct((1, D), F32)],
        scratch_shapes=[pltpu.VMEM((CHUNK, D), BF16), pltpu.VMEM((CHUNK, D), F32), pltpu.VMEM((CHUNK, 1), F32),
                        pltpu.VMEM((CHUNK, D), F32), pltpu.VMEM((CHUNK, D), F32)],
        compiler_params=_params("arbitrary"),
    )(z, z, z, dbr, lng, lnb, t_mix, t_mix_t, bias_full, mask)


def _branch_b_bwd(c0, z, dbr, g, b):
    S, D = c0.shape
    ts = _tile(S, 512)

    def body(c0_ref, zg_ref, d_ref, g_ref, b_ref, dc0_ref, dz_ref, dg_ref, db_ref, dcb_ref):
        @pl.when(pl.program_id(0) == 0)
        def _():
            dg_ref[...] = jnp.zeros_like(dg_ref)
            db_ref[...] = jnp.zeros_like(db_ref)
            dcb_ref[...] = jnp.zeros_like(dcb_ref)

        def slab(si, c):
            dg, db, dcb = c
            r = _rows(si, 16)
            v = c0_ref[r, :]
            xc = v - _mean(v)
            rstd = lax.rsqrt(_mean(xc * xc) + LN_EPS)
            cn = xc * rstd
            c1 = cn * g_ref[...] + b_ref[...]
            c2, dc2_dc1 = _silu_and_grad(c1)
            sg, dsg = _silu_and_grad(zg_ref[r, :])
            d = d_ref[r, :]
            dz_ref[r, :] = ((d * c2) * dsg).astype(BF16)
            dc1 = (d * sg) * dc2_dc1
            dcn = dc1 * g_ref[...]
            dc0 = rstd * ((dcn - _mean(dcn)) - cn * _mean(dcn * cn))
            dc0_ref[r, :] = dc0
            return dg + _colsum(dc1 * cn), db + _colsum(dc1), dcb + _colsum(dc0)
        zero = jnp.zeros((1, D), F32)
        dg, db, dcb = _grouped_loop(ts // 16, 4, slab, (zero, zero, zero))
        dg_ref[...] += dg
        db_ref[...] += db
        dcb_ref[...] += dcb

    return pl.pallas_call(
        body, name="branch_b_bwd", grid=(S // ts,),
        in_specs=[_tok(ts, D), _zcol(ts, D, 5), _plane(ts, D, 1), _row1(D), _row1(D)],
        out_specs=[_tok(ts, D), _zcol(ts, D, 2), _row1(D), _row1(D), _row1(D)],
        out_shape=[jax.ShapeDtypeStruct((S, D), F32), jax.ShapeDtypeStruct((S, 3 * D), BF16),
                   jax.ShapeDtypeStruct((1, D), F32), jax.ShapeDtypeStruct((1, D), F32),
                   jax.ShapeDtypeStruct((1, D), F32)],
        compiler_params=_params("arbitrary"),
    )(c0, z, dbr, g, b)


def _conv_bwd(dc0, z, w_all, dz_b, l):
    S, D = dc0.shape
    ts = _tile(S, 256)
    nt = S // ts
    CB = 256
    hb = ts // HALO
    nh = S // HALO
    SH = HALO - CONV_K + 1

    def body(dc_ref, dcn_ref, za_ref, zb_ref, zap_ref, zbp_ref, w_ref, dzin_ref,
             dz_ref, dw_ref, glu_scr, dc_scr, dw_acc, gsh_scr, dsh_scr):
        del dzin_ref
        i = pl.program_id(0)

        @pl.when(i == 0)
        def _():
            dw_acc[...] = jnp.zeros_like(dw_acc)

        prev = zap_ref[...] * _sigmoid(zbp_ref[...])
        glu_scr[0:HALO, :] = jnp.where(i == 0, 0.0, prev)
        dc_scr[ts:ts + HALO, :] = jnp.where(i == nt - 1, 0.0, dcn_ref[...])

        def fill(si, c):
            r = _rows(si, 32)
            glu_scr[pl.ds(pl.multiple_of(HALO + si * 32, 32), 32), :] = za_ref[r, :] * _sigmoid(zb_ref[r, :])
            dc_scr[r, :] = dc_ref[r, :]
            return c
        _grouped_loop(ts // 32, 2, fill, 0)

        def colblock(ci, c):
            cols = _rows(ci, CB)
            cols_b = pl.ds(pl.multiple_of(D + ci * CB, CB), CB)
            _fill_shifts(gsh_scr, glu_scr, cols, ts)
            _fill_shifts(dsh_scr, dc_scr, cols, ts)
            for s in range(ts // 32):
                rows = pl.ds(s * 32, 32)
                dc = dc_scr[rows, cols]
                acc = jnp.zeros((32, CB), F32)
                for k in range(CONV_K):
                    prod = dc * _shifted(gsh_scr, glu_scr, cols, s * 32 + k + SH, 32)
                    dw_acc[k, :, cols] += (prod[0:8] + prod[8:16]) + (prod[16:24] + prod[24:32])
                    acc = acc + (_shifted(dsh_scr, dc_scr, cols, s * 32 + k, 32)
                                 * w_ref[CONV_K - 1 - k:CONV_K - k, cols])
                sg = _sigmoid(zb_ref[rows, cols])
                dz_ref[rows, cols] = (acc * sg).astype(BF16)
                dz_ref[rows, cols_b] = ((acc * za_ref[rows, cols]) * (sg * (1.0 - sg))).astype(BF16)
            return c
        lax.fori_loop(0, D // CB, colblock, 0)

        @pl.when(i == nt - 1)
        def _():
            for k in range(CONV_K):
                dw_ref[k:k + 1, :] = _colsum(dw_acc[k])

    prev_map = lambda n: (lambda i: (jnp.maximum(i * hb - 1, 0), n))
    return pl.pallas_call(
        body, name="conv_bwd", grid=(nt,),
        in_specs=[_tok(ts, D), pl.BlockSpec((HALO, D), lambda i: (jnp.minimum((i + 1) * hb, nh - 1), 0)),
                  _zcol(ts, D, 3), _zcol(ts, D, 4),
                  pl.BlockSpec((HALO, D), prev_map(3)), pl.BlockSpec((HALO, D), prev_map(4)),
                  pl.BlockSpec((None, CONV_K, D), lambda i: (l, 0, 0)), ANY],
        out_specs=[_tok(ts, 2 * D), pl.BlockSpec((CONV_K, D), lambda i: (0, 0))],
        out_shape=[jax.ShapeDtypeStruct((S, 3 * D), BF16), jax.ShapeDtypeStruct((CONV_K, D), F32)],
        scratch_shapes=[pltpu.VMEM((ts + HALO, D), F32), pltpu.VMEM((ts + HALO, D), F32),
                        pltpu.VMEM((CONV_K, 8, D), F32),
                        pltpu.VMEM((7, ts + HALO, CB), F32), pltpu.VMEM((7, ts + HALO, CB), F32)],
        input_output_aliases={7: 0},
        compiler_params=_params("arbitrary"),
    )(dc0, dc0, z, z, z, z, w_all, dz_b)


def _branch_c_bwd(z, dbr, kv, kvt):
    S = z.shape[0]
    M, D2 = kv.shape
    D = D2 // 2
    HD = D // HEADS
    scale = 1.0 / math.sqrt(HD)
    ts = _tile(S, 256)

    def body(q_ref, zg_ref, d_ref, kt_ref, vt_ref, k_ref, v_ref, dz_ref, dk_ref, dv_ref):
        @pl.when(pl.program_id(0) == 0)
        def _():
            dk_ref[...] = jnp.zeros_like(dk_ref)
            dv_ref[...] = jnp.zeros_like(dv_ref)

        for h in range(HEADS):
            cols = slice(h * HD, (h + 1) * HD)
            qb = q_ref[:, cols].astype(BF16)
            p = _softmax_rows(_dot(qb, kt_ref[cols, :]) * scale)
            pb = p.astype(BF16)
            att = _dot(pb, v_ref[:, cols])
            sg, dsg = _silu_and_grad(zg_ref[:, cols])
            d = d_ref[:, cols]
            dz_ref[:, D + h * HD:D + (h + 1) * HD] = ((d * att) * dsg).astype(BF16)
            datt = (d * sg).astype(BF16)
            dp = _dot(datt, vt_ref[cols, :])
            dv_ref[:, cols] += _dot_ta(pb, datt)
            ds = ((p * (dp - jnp.sum(dp * p, axis=-1, keepdims=True))) * scale).astype(BF16)
            dz_ref[:, cols] = _dot(ds, k_ref[:, cols]).astype(BF16)
            dk_ref[:, cols] += _dot_ta(ds, qb)

    return pl.pallas_call(
        body, name="branch_c_bwd", grid=(S // ts,),
        in_specs=[_zcol(ts, D, 6), _zcol(ts, D, 7), _plane(ts, D, 2),
                  pl.BlockSpec((D, M), lambda i: (0, 0)), pl.BlockSpec((D, M), lambda i: (1, 0)),
                  pl.BlockSpec((M, D), lambda i: (0, 0)), pl.BlockSpec((M, D), lambda i: (0, 1))],
        out_specs=[_tok(ts, 2 * D), pl.BlockSpec((M, D), lambda i: (0, 0)), pl.BlockSpec((M, D), lambda i: (0, 0))],
        out_shape=[jax.ShapeDtypeStruct((S, 2 * D), BF16), jax.ShapeDtypeStruct((M, D), F32),
                   jax.ShapeDtypeStruct((M, D), F32)],
        compiler_params=_params("arbitrary"),
    )(z, z, dbr, kvt, kvt, kv, kv)


def _kv_bwd(dk, dv, mn, mem, wkv_all, l, prev):
    M, D = mem.shape

    def body(*refs):
        dk_ref, dv_ref, mn_ref, mem_ref, w_ref = refs[:5]
        dw_ref, dg_ref = refs[-2:]
        dkb = dk_ref[...].astype(BF16)
        dvb = dv_ref[...].astype(BF16)
        dw_ref[:, 0:D] = _dot_ta(mn_ref[...], dkb)
        dw_ref[:, D:2 * D] = _dot_ta(mn_ref[...], dvb)
        dmn = _dot_tb(dkb, w_ref[:, 0:D]) + _dot_tb(dvb, w_ref[:, D:2 * D])
        m = mem_ref[...]
        dg_ref[...] = _colsum(dmn * (m * lax.rsqrt(_mean(m * m) + RMS_EPS)))

    full = lambda shape: pl.BlockSpec(shape, lambda i: (0, 0))
    in_specs = [full((M, D)), full((M, D)), full((M, D)), full((M, D)),
                pl.BlockSpec((None, D, 2 * D), lambda i: (l, 0, 0))]
    args = [dk, dv, mn, mem, wkv_all]
    aliases = {}
    if prev is not None:
        in_specs.append(ANY)
        args.append(prev)
        aliases = {5: 0}
    return pl.pallas_call(
        body, name="kv_bwd", grid=(1,),
        in_specs=in_specs,
        out_specs=[pl.BlockSpec((None, D, 2 * D), lambda i: (l, 0, 0)), _row1(D)],
        out_shape=[jax.ShapeDtypeStruct((N_LAYERS, D, 2 * D), F32), jax.ShapeDtypeStruct((1, D), F32)],
        input_output_aliases=aliases,
        compiler_params=_params("arbitrary"),
    )(*args)


def _in_proj_bwd(pieces, w_all, x, g, dxp, l):
    S, D = x.shape
    tm = _tile(S, 512)
    n_p = len(pieces)

    def body(*refs):
        dz_refs = refs[:n_p]
        w_ref, x_ref, g_ref, dxp_ref, dx_ref, dg_ref, acc = refs[n_p:]
        i = pl.program_id(0)
        k = pl.program_id(1)

        @pl.when(k == 0)
        def _():
            acc[...] = jnp.zeros_like(acc)

        @pl.when((i == 0) & (k == 0))
        def _():
            dg_ref[...] = jnp.zeros_like(dg_ref)

        for (_, lo, n), r in zip(pieces, dz_refs):
            @pl.when((k >= lo) & (k < lo + n))
            def _(r=r):
                acc[...] += _dot_tb(r[...], w_ref[...])

        @pl.when(k == N_PLANES - 1)
        def _():
            def slab(si, c):
                rr = _rows(si, 16)
                xs = x_ref[rr, :]
                rs = lax.rsqrt(_mean(xs * xs) + RMS_EPS)
                n = xs * rs
                dh = acc[rr, :]
                dn = dh * g_ref[...]
                dx_ref[rr, :] = rs * (dn - n * _mean(dn * n)) + dxp_ref[rr, :]
                return c + _colsum(dh * n)
            dg_ref[...] += _grouped_loop(tm // 16, 4, slab, jnp.zeros((1, D), F32))

    def piece_spec(lo, n):
        return pl.BlockSpec((tm, D), lambda i, k: (i, jnp.clip(k - lo, 0, n - 1)))

    tokk = lambda: pl.BlockSpec((tm, D), lambda i, k: (i, 0))
    rowk = lambda: pl.BlockSpec((1, D), lambda i, k: (0, 0))
    return pl.pallas_call(
        body, name="in_proj_bwd", grid=(S // tm, N_PLANES),
        in_specs=[piece_spec(lo, n) for _, lo, n in pieces]
        + [pl.BlockSpec((None, D, D), lambda i, k: (l, 0, k)), tokk(), rowk(), tokk()],
        out_specs=[tokk(), rowk()],
        out_shape=[jax.ShapeDtypeStruct((S, D), F32), jax.ShapeDtypeStruct((1, D), F32)],
        scratch_shapes=[pltpu.VMEM((tm, D), F32)],
        compiler_params=_params("arbitrary", "arbitrary"),
    )(*[p for p, _, _ in pieces], w_all, x, g, dxp)


def _wgrad(a, b, out_shape, plane, col_off, prev, a_plane=None, b_plane=None):
    S = a.shape[-2]
    K1 = a.shape[-1]
    nb_cols = b.shape[-1]
    ts = _tile(S, 1024)
    tn = _tile(nb_cols, 1024)

    def body(*refs):
        a_ref, b_ref = refs[:2]
        o_ref = refs[-1]
        k = pl.program_id(1)
        prod = _dot_ta(a_ref[...].astype(BF16), b_ref[...].astype(BF16))

        @pl.when(k == 0)
        def _():
            o_ref[...] = prod

        @pl.when(k > 0)
        def _():
            o_ref[...] += prod

    if a_plane is None:
        a_spec = pl.BlockSpec((ts, K1), lambda j, k: (k, 0))
    else:
        a_spec = pl.BlockSpec((None, ts, K1), lambda j, k: (a_plane, k, 0))
    if b_plane is None:
        b_spec = pl.BlockSpec((ts, tn), lambda j, k: (k, j))
    else:
        b_spec = pl.BlockSpec((None, ts, tn), lambda j, k: (b_plane, k, j))
    in_specs, args, aliases = [a_spec, b_spec], [a, b], {}
    if prev is not None:
        in_specs.append(ANY)
        args.append(prev)
        aliases = {2: 0}
    return pl.pallas_call(
        body, name="wgrad", grid=(nb_cols // tn, S // ts),
        in_specs=in_specs,
        out_specs=pl.BlockSpec((None, K1, tn), lambda j, k: (plane, 0, col_off + j)),
        out_shape=jax.ShapeDtypeStruct(out_shape, F32),
        input_output_aliases=aliases,
        compiler_params=_params("parallel", "arbitrary"),
    )(*args)


def _position():
    x, y, c = lax.axis_index("x"), lax.axis_index("y"), lax.axis_index("c")
    chips = [(1 - x, y), (x, 1 - y), (1 - x, 1 - y)]
    return x, y, c, chips


def _block_view(ref, axis, blk, size):
    idx = [slice(None)] * len(ref.shape)
    idx[axis] = pl.ds(blk * size, size)
    return ref.at[tuple(idx)]


def _gather_weights(shards, axes):
    n = len(shards)
    full_shapes = []
    for s, ax in zip(shards, axes):
        shp = list(s.shape)
        shp[ax] *= N_CHIPS
        full_shapes.append(jax.ShapeDtypeStruct(tuple(shp), s.dtype))

    def body(*refs):
        ins, outs = refs[:n], refs[n:2 * n]
        s_own, r_own, s_ici, r_ici, s_fwd, r_fwd = refs[2 * n:]
        x, y, c, chips = _position()
        j = 2 * x + y
        sib = (x, y, 1 - c)

        def block(t, blk, layer=None):
            size = ins[t].shape[axes[t]]
            idx = [slice(None)] * len(outs[t].shape)
            idx[axes[t]] = pl.ds(blk * size, size)
            if layer is not None:
                idx[0] = layer
            return outs[t].at[tuple(idx)]

        def copy(src, dst, ssem, rsem, to):
            return pltpu.make_async_remote_copy(src_ref=src, dst_ref=dst, send_sem=ssem, recv_sem=rsem,
                                                device_id=to, device_id_type=MESH)

        own = [copy(ins[t], block(t, j), s_own.at[t], r_own.at[t], sib) for t in range(n)]
        for cp in own:
            cp.start()
        out_ici = []
        for t in range(n):
            for k, (px, py) in enumerate(chips):
                cp = copy(ins[t].at[c], block(t, j, c), s_ici.at[t * 3 + k], r_ici.at[t * 3 + k], (px, py, c))
                cp.start()
                out_ici.append(cp)
        passed = []
        for t in range(n):
            for k, (px, py) in enumerate(chips):
                got = block(t, 2 * px + py, c)
                copy(ins[t].at[c], got, s_ici.at[t * 3 + k], r_ici.at[t * 3 + k], (px, py, c)).wait_recv()
                cp = copy(got, got, s_fwd.at[t * 3 + k], r_fwd.at[t * 3 + k], sib)
                cp.start()
                passed.append(cp)
        for t in range(n):
            for k, (px, py) in enumerate(chips):
                theirs = block(t, 2 * px + py, 1 - c)
                copy(theirs, theirs, s_fwd.at[t * 3 + k], r_fwd.at[t * 3 + k], sib).wait_recv()
        for cp in own:
            cp.wait()
        for cp in out_ici + passed:
            cp.wait_send()

    return pl.pallas_call(
        body, name="gather_weights",
        in_specs=[ANY] * n, out_specs=[ANY] * n, out_shape=full_shapes,
        scratch_shapes=[pltpu.SemaphoreType.DMA((n,)), pltpu.SemaphoreType.DMA((n,)),
                        pltpu.SemaphoreType.DMA((n * 3,)), pltpu.SemaphoreType.DMA((n * 3,)),
                        pltpu.SemaphoreType.DMA((n * 3,)), pltpu.SemaphoreType.DMA((n * 3,))],
    )(*shards)


def _pair_exchange(grads):
    n = len(grads)

    def body(*refs):
        ins, outs = refs[:n], refs[n:2 * n]
        ssem, rsem = refs[2 * n:]
        x, y, c, _ = _position()
        cps = [pltpu.make_async_remote_copy(
            src_ref=ins[t].at[1 - c], dst_ref=outs[t], send_sem=ssem.at[t], recv_sem=rsem.at[t],
            device_id=(x, y, 1 - c), device_id_type=MESH) for t in range(n)]
        for cp in cps:
            cp.start()
        for cp in cps:
            cp.wait()

    return pl.pallas_call(
        body, name="pair_exchange",
        in_specs=[ANY] * n, out_specs=[ANY] * n,
        out_shape=[jax.ShapeDtypeStruct(g.shape[1:], g.dtype) for g in grads],
        scratch_shapes=[pltpu.SemaphoreType.DMA((n,)), pltpu.SemaphoreType.DMA((n,))],
    )(*grads)


def _block_exchange(sums, axes):
    n = len(sums)
    out_shapes = []
    for s, ax in zip(sums, axes):
        shp = list(s.shape)
        shp[ax] //= N_CHIPS
        out_shapes.append(jax.ShapeDtypeStruct((3,) + tuple(shp), s.dtype))

    def body(*refs):
        ins, outs = refs[:n], refs[n:2 * n]
        ssem, rsem = refs[2 * n:]
        x, y, c, chips = _position()
        cps = []
        for t in range(n):
            size = ins[t].shape[axes[t]] // N_CHIPS
            for k, (px, py) in enumerate(chips):
                cps.append(pltpu.make_async_remote_copy(
                    src_ref=_block_view(ins[t], axes[t], 2 * px + py, size), dst_ref=outs[t].at[k],
                    send_sem=ssem.at[t * 3 + k], recv_sem=rsem.at[t * 3 + k],
                    device_id=(px, py, c), device_id_type=MESH))
        for cp in cps:
            cp.start()
        for cp in cps:
            cp.wait()

    return pl.pallas_call(
        body, name="block_exchange",
        in_specs=[ANY] * n, out_specs=[ANY] * n, out_shape=out_shapes,
        scratch_shapes=[pltpu.SemaphoreType.DMA((n * 3,)), pltpu.SemaphoreType.DMA((n * 3,))],
    )(*sums)


def _sibling_share(halves):
    n = len(halves)

    def body(*refs):
        outs = refs[n:2 * n]
        ssem, rsem = refs[2 * n:]
        x, y, c, _ = _position()
        cps = [pltpu.make_async_remote_copy(
            src_ref=outs[t].at[c], dst_ref=outs[t].at[c], send_sem=ssem.at[t], recv_sem=rsem.at[t],
            device_id=(x, y, 1 - c), device_id_type=MESH) for t in range(n)]
        for cp in cps:
            cp.start()
        for cp in cps:
            cp.wait()

    return pl.pallas_call(
        body, name="sibling_share",
        in_specs=[ANY] * n, out_specs=[ANY] * n,
        out_shape=[jax.ShapeDtypeStruct(h.shape, h.dtype) for h in halves],
        input_output_aliases={t: t for t in range(n)},
        scratch_shapes=[pltpu.SemaphoreType.DMA((n,)), pltpu.SemaphoreType.DMA((n,))],
    )(*halves)


def _small_allreduce(p):
    R, C = p.shape

    def body(p_ref, o_ref, sib_buf, chip_sums, ssem, rsem):
        x, y, c, chips = _position()
        j = 2 * x + y
        sib = pltpu.make_async_remote_copy(
            src_ref=p_ref, dst_ref=sib_buf, send_sem=ssem.at[0], recv_sem=rsem.at[0],
            device_id=(x, y, 1 - c), device_id_type=MESH)
        sib.start()
        sib.wait()
        chip_sums[j] = p_ref[...] + sib_buf[...]
        cps = [pltpu.make_async_remote_copy(
            src_ref=chip_sums.at[j], dst_ref=chip_sums.at[j], send_sem=ssem.at[1 + k], recv_sem=rsem.at[1 + k],
            device_id=(px, py, c), device_id_type=MESH) for k, (px, py) in enumerate(chips)]
        for cp in cps:
            cp.start()
        for k, (px, py) in enumerate(chips):
            pltpu.make_async_remote_copy(
                src_ref=chip_sums.at[j], dst_ref=chip_sums.at[2 * px + py],
                send_sem=ssem.at[1 + k], recv_sem=rsem.at[1 + k],
                device_id=(px, py, c), device_id_type=MESH).wait()
        o_ref[...] = ((chip_sums[0] + chip_sums[1]) + chip_sums[2]) + chip_sums[3]

    vm = pl.BlockSpec(memory_space=pltpu.VMEM)
    return pl.pallas_call(
        body, name="small_allreduce",
        in_specs=[vm], out_specs=vm, out_shape=jax.ShapeDtypeStruct((R, C), F32),
        scratch_shapes=[pltpu.VMEM((R, C), F32), pltpu.VMEM((N_CHIPS, R, C), F32),
                        pltpu.SemaphoreType.DMA((4,)), pltpu.SemaphoreType.DMA((4,))],
        compiler_params=pltpu.CompilerParams(vmem_limit_bytes=VMEM_LIMIT_BYTES),
    )(p)


def _add_own_layer(g, recv, pos):
    _, B, R, C = g.shape
    tr = _tile(R, 256)
    tc = _tile(C, 2816)

    def body(pos_ref, g_ref, r_ref, o_ref, ob_ref):
        del pos_ref
        s = g_ref[...] + r_ref[...]
        o_ref[...] = s
        ob_ref[...] = s.astype(BF16)

    spec = lambda: pl.BlockSpec((None, tr, tc), lambda b, i, j, pos_ref: (b, i, j))
    return pl.pallas_call(
        body, name="add_own_layer",
        grid_spec=pltpu.PrefetchScalarGridSpec(
            num_scalar_prefetch=1, grid=(B, R // tr, C // tc),
            in_specs=[pl.BlockSpec((None, None, tr, tc), lambda b, i, j, pos_ref: (pos_ref[0], b, i, j)), spec()],
            out_specs=[spec(), spec()]),
        out_shape=[jax.ShapeDtypeStruct((B, R, C), F32), jax.ShapeDtypeStruct((B, R, C), BF16)],
        compiler_params=_params("parallel", "parallel", "parallel"),
    )(pos, g, recv)


def _sum_blocks(own, recv, axis, pos):
    _, B, rs, cs = recv.shape
    tr = _tile(rs, 128)
    nr = rs // tr

    def body(pos_ref, o_ref, r0_ref, r1_ref, r2_ref, out_ref):
        del pos_ref
        out_ref[...] = (((o_ref[...] + r0_ref[...].astype(F32)) + r1_ref[...].astype(F32))
                        + r2_ref[...].astype(F32))

    if axis == 2:
        own_spec = pl.BlockSpec((None, tr, cs), lambda b, i, pos_ref: (b, i, pos_ref[1]))
    else:
        own_spec = pl.BlockSpec((None, tr, cs), lambda b, i, pos_ref: (b, pos_ref[1] * nr + i, 0))
    rspec = lambda k: pl.BlockSpec((None, None, tr, cs), lambda b, i, pos_ref: (k, b, i, 0))
    return pl.pallas_call(
        body, name="sum_blocks",
        grid_spec=pltpu.PrefetchScalarGridSpec(
            num_scalar_prefetch=1, grid=(B, nr),
            in_specs=[own_spec, rspec(0), rspec(1), rspec(2)],
            out_specs=pl.BlockSpec((None, None, tr, cs), lambda b, i, pos_ref: (pos_ref[0], b, i, 0))),
        out_shape=jax.ShapeDtypeStruct((N_LAYERS, B, rs, cs), F32),
        compiler_params=_params("parallel", "parallel"),
    )(pos, own, recv, recv, recv)


def _adamw(w, g, m, v):
    shape = w.shape
    C = shape[-1]
    R = w.size // C
    tr = R
    for cand in (512, 256, 128, 64, 32, 16, 8):
        if R % cand == 0 and cand * C * 4 <= (1 << 20):
            tr = cand
            break
    bc1 = 1.0 - ADAM_B1 ** ADAM_STEP
    bc2 = 1.0 - ADAM_B2 ** ADAM_STEP

    def body(w_ref, g_ref, m_ref, v_ref, d_ref, mo_ref, vo_ref):
        gg = g_ref[...]
        mn = ADAM_B1 * m_ref[...] + (1.0 - ADAM_B1) * gg
        vn = ADAM_B2 * v_ref[...] + (1.0 - ADAM_B2) * (gg * gg)
        mo_ref[...] = mn
        vo_ref[...] = vn
        d_ref[...] = -ADAM_LR * ((mn / bc1) / (jnp.sqrt(vn / bc2) + ADAM_EPS) + ADAM_WD * w_ref[...])

    spec = lambda: pl.BlockSpec((tr, C), lambda i: (i, 0))
    outs = pl.pallas_call(
        body, name="adamw", grid=(R // tr,),
        in_specs=[spec()] * 4, out_specs=[spec()] * 3,
        out_shape=[jax.ShapeDtypeStruct((R, C), F32)] * 3,
        compiler_params=_params("parallel"),
    )(*[a.reshape(R, C) for a in (w, g, m, v)])
    return tuple(o.reshape(shape) for o in outs)


def kernel(x, mem, norm_g, mem_norm_g, w_in, gmlp_ln_g, gmlp_ln_b, w_s, b_s, conv_w, conv_b, conv_ln_g, conv_ln_b, w_kv, w_branch, w_out, final_norm_g, loss_target, m_norm_g, m_mem_norm_g, m_w_in, m_gmlp_ln_g, m_gmlp_ln_b, m_w_s, m_b_s, m_conv_w, m_conv_b, m_conv_ln_g, m_conv_ln_b, m_w_kv, m_w_branch, m_w_out, m_final_norm_g, v_norm_g, v_mem_norm_g, v_w_in, v_gmlp_ln_g, v_gmlp_ln_b, v_w_s, v_b_s, v_conv_w, v_conv_b, v_conv_ln_g, v_conv_ln_b, v_w_kv, v_w_branch, v_w_out, v_final_norm_g):
    xs, mems, tgt = x[0], mem[0], loss_target[0]
    S, D = xs.shape
    assert D // GROUPS == LANES and S % CHUNK == 0 and w_s.shape[-1] == CHUNK
    L = N_LAYERS

    win_f, wkv_f, wb_f, wo_f, cw_f = _gather_weights(
        [w_in.astype(BF16), w_kv.astype(BF16), w_branch.astype(BF16), w_out.astype(BF16), conv_w],
        [2, 2, 2, 1, 2])

    tri = jnp.tril(jnp.ones((CHUNK, CHUNK), F32))
    t_mix = (w_s * tri).astype(BF16)
    t_mix_t = jnp.swapaxes(t_mix, -1, -2)
    bias_full = jnp.repeat(jnp.swapaxes(b_s, -1, -2), D // GROUPS, axis=-1)
    row = lambda a, l: a[l][None, :]

    saved = []
    h = xs
    for l in range(L):
        z, hb = _in_proj(h, row(norm_g, l), win_f, l)
        bra = _branch_a_fwd(z, row(gmlp_ln_g, l), row(gmlp_ln_b, l), t_mix[l], bias_full[l])
        c0 = _conv_fwd(z, cw_f, row(conv_b, l), l)
        brb = _branch_b_fwd(c0, z, row(conv_ln_g, l), row(conv_ln_b, l))
        mn, kv, kvt = _kv_fwd(mems, row(mem_norm_g, l), wkv_f, l)
        brc = _branch_c_fwd(z, kv, kvt)
        merged, h_next = _merge_fwd(bra, brb, brc, z, h, wb_f, wo_f, l)
        saved.append((h, z, hb, bra, c0, brb, mn, kv, kvt, brc, merged))
        h = h_next

    dx, loss_cols, d_final_g = _loss_bwd(h, tgt, final_norm_g[None, :])
    loss = lax.psum(0.5 * jnp.sum(loss_cols) / D, ("x", "y", "c"))

    g_in = g_kv = g_b = g_o = None
    small = {k: [None] * L for k in ("norm_g", "mem_norm_g", "gmlp_ln_g", "gmlp_ln_b", "w_s", "b_s", "conv_w",
                                     "conv_b", "conv_ln_g", "conv_ln_b")}
    for l in reversed(range(L)):
        h_in, z, hb, bra, c0, brb, mn, kv, kvt, brc, merged = saved[l]
        dz_m, dproj, dbr = _merge_bwd(dx, bra, brb, brc, z, wb_f, wo_f, l)
        g_o = _wgrad(merged, dx, (L, D, D), l, 0, g_o)
        for n, br in enumerate((bra, brb, brc)):
            g_b = _wgrad(br, dproj, (L * N_BRANCHES, D, D), l * N_BRANCHES + n, 0, g_b, b_plane=n)
        dz_a, dws, dbs, dlg, dlb = _branch_a_bwd(z, dbr, row(gmlp_ln_g, l), row(gmlp_ln_b, l),
                                                t_mix[l], t_mix_t[l], bias_full[l], tri)
        dc0, dz_b, dcg, dcbeta, dcb = _branch_b_bwd(c0, z, dbr, row(conv_ln_g, l), row(conv_ln_b, l))
        dz_b, dcw = _conv_bwd(dc0, z, cw_f, dz_b, l)
        dz_c, dk, dv = _branch_c_bwd(z, dbr, kv, kvt)
        g_kv, dmg = _kv_bwd(dk, dv, mn, mems, wkv_f, l, g_kv)
        pieces = [(dz_a, 0, 3), (dz_b, 3, 3), (dz_c, 6, 2), (dz_m, 8, 3)]
        for dzp, lo, _ in pieces:
            g_in = _wgrad(hb, dzp, (L, D, N_PLANES * D), l, lo, g_in)
        dx, dng = _in_proj_bwd(pieces, win_f, h_in, row(norm_g, l), dx, l)
        for k, val in (("norm_g", dng), ("mem_norm_g", dmg), ("gmlp_ln_g", dlg), ("gmlp_ln_b", dlb), ("w_s", dws),
                       ("b_s", dbs), ("conv_w", dcw), ("conv_b", dcb), ("conv_ln_g", dcg), ("conv_ln_b", dcbeta)):
            small[k][l] = val
    grad_x = dx[None]

    xi, yi, ci = lax.axis_index("x"), lax.axis_index("y"), lax.axis_index("c")
    pos = jnp.stack([ci, 2 * xi + yi]).astype(jnp.int32)
    big = [g_in[:, None], g_kv[:, None], g_b.reshape(L, N_BRANCHES, D, D), g_o[:, None]]
    shard_axis = [2, 2, 1, 1]
    recv = _pair_exchange(big)
    sums = [_add_own_layer(g, r, pos) for g, r in zip(big, recv)]
    blocks = _block_exchange([sb for _, sb in sums], shard_axis)
    reduced = [_sum_blocks(s, b, ax, pos) for (s, _), b, ax in zip(sums, blocks, shard_axis)]
    gs_in, gs_kv, gs_b, gs_o = _sibling_share(reduced)
    grad_w_in = gs_in.reshape(w_in.shape)
    grad_w_kv = gs_kv.reshape(w_kv.shape)
    grad_w_branch = gs_b.reshape(w_branch.shape)
    grad_w_out = gs_o.reshape(w_out.shape)

    order = [("norm_g", norm_g.shape), ("mem_norm_g", mem_norm_g.shape), ("gmlp_ln_g", gmlp_ln_g.shape),
             ("gmlp_ln_b", gmlp_ln_b.shape), ("w_s", w_s.shape), ("b_s", b_s.shape),
             ("conv_w", (L, CONV_K, D)), ("conv_b", conv_b.shape), ("conv_ln_g", conv_ln_g.shape),
             ("conv_ln_b", conv_ln_b.shape)]
    parts = [jnp.stack([v.reshape(shp[1:]) for v in small[k]]).reshape(-1, LANES) for k, shp in order]
    parts.append(d_final_g.reshape(-1, LANES))
    sizes = [p.shape[0] for p in parts]
    assert all(s % 8 == 0 for s in sizes)
    total = _small_allreduce(jnp.concatenate(parts, axis=0))
    red, off = {}, 0
    for (k, shp), n in zip(order + [("final_norm_g", final_norm_g.shape)], sizes):
        red[k] = total[off:off + n].reshape(shp)
        off += n
    cs = conv_w.shape[-1]
    red["conv_w"] = lax.dynamic_slice_in_dim(red["conv_w"], (2 * xi + yi) * cs, cs, axis=2)
    red.update(w_in=grad_w_in, w_kv=grad_w_kv, w_branch=grad_w_branch, w_out=grad_w_out)

    names = ["norm_g", "mem_norm_g", "w_in", "gmlp_ln_g", "gmlp_ln_b", "w_s", "b_s", "conv_w", "conv_b",
             "conv_ln_g", "conv_ln_b", "w_kv", "w_branch", "w_out", "final_norm_g"]
    weights = dict(norm_g=norm_g, mem_norm_g=mem_norm_g, w_in=w_in, gmlp_ln_g=gmlp_ln_g, gmlp_ln_b=gmlp_ln_b,
                   w_s=w_s, b_s=b_s, conv_w=conv_w, conv_b=conv_b, conv_ln_g=conv_ln_g, conv_ln_b=conv_ln_b,
                   w_kv=w_kv, w_branch=w_branch, w_out=w_out, final_norm_g=final_norm_g)
    ms = dict(norm_g=m_norm_g, mem_norm_g=m_mem_norm_g, w_in=m_w_in, gmlp_ln_g=m_gmlp_ln_g, gmlp_ln_b=m_gmlp_ln_b,
              w_s=m_w_s, b_s=m_b_s, conv_w=m_conv_w, conv_b=m_conv_b, conv_ln_g=m_conv_ln_g, conv_ln_b=m_conv_ln_b,
              w_kv=m_w_kv, w_branch=m_w_branch, w_out=m_w_out, final_norm_g=m_final_norm_g)
    vs = dict(norm_g=v_norm_g, mem_norm_g=v_mem_norm_g, w_in=v_w_in, gmlp_ln_g=v_gmlp_ln_g, gmlp_ln_b=v_gmlp_ln_b,
              w_s=v_w_s, b_s=v_b_s, conv_w=v_conv_w, conv_b=v_conv_b, conv_ln_g=v_conv_ln_g, conv_ln_b=v_conv_ln_b,
              w_kv=v_w_kv, w_branch=v_w_branch, w_out=v_w_out, final_norm_g=v_final_norm_g)
    deltas, new_m, new_v = [], [], []
    for k in names:
        w2 = weights[k] if weights[k].ndim > 1 else weights[k][None, :]
        d, mo, vo = _adamw(w2, red[k].reshape(w2.shape), ms[k].reshape(w2.shape), vs[k].reshape(w2.shape))
        deltas.append(d.reshape(weights[k].shape))
        new_m.append(mo.reshape(weights[k].shape))
        new_v.append(vo.reshape(weights[k].shape))
    grads = [red[k].reshape(weights[k].shape) for k in names]
    return (loss, grad_x, *grads, *deltas, *new_m, *new_v)
```

```python
import math

import jax
import jax.numpy as jnp
from jax import lax
from jax.experimental import pallas as pl
from jax.experimental.pallas import tpu as pltpu

F32 = jnp.float32
BF16 = jnp.bfloat16

N_LAYERS = 2
N_BRANCHES = 3
N_PLANES = 11
N_CHIPS = 4
CHUNK = 128
GROUPS = 8
HEADS = 4
CONV_K = 31
HALO = 32
LANES = 128
RMS_EPS = 1e-6
LN_EPS = 1e-5
ADAM_LR, ADAM_B1, ADAM_B2, ADAM_EPS, ADAM_WD, ADAM_STEP = 0.001, 0.9, 0.999, 1e-08, 0.01, 10
VMEM_LIMIT_BYTES = 48 * 1024 * 1024
MESH = pl.DeviceIdType.MESH
ANY = pl.BlockSpec(memory_space=pl.ANY)


def _params(*sem):
    return pltpu.CompilerParams(dimension_semantics=sem, vmem_limit_bytes=VMEM_LIMIT_BYTES)


def _tile(n, pref):
    t = min(n, pref)
    assert n % t == 0, (n, t)
    return t


def _dot(a, b):
    return jnp.dot(a, b, preferred_element_type=F32)


def _dot_tb(a, b):
    return lax.dot_general(a, b, (((1,), (1,)), ((), ())), preferred_element_type=F32)


def _dot_ta(a, b):
    return lax.dot_general(a, b, (((0,), (0,)), ((), ())), preferred_element_type=F32)


def _sigmoid(x):
    return 1.0 / (1.0 + jnp.exp(-x))


def _silu(x):
    return x * _sigmoid(x)


def _silu_and_grad(x):
    s = _sigmoid(x)
    return x * s, s * (1.0 + x * (1.0 - s))


_GELU_C = math.sqrt(2.0 / math.pi)
_GELU_A = 0.044715


def _gelu(x):
    return x * (0.5 * (1.0 + jnp.tanh(_GELU_C * (x + _GELU_A * (x * x * x)))))


def _gelu_and_grad(x):
    x2 = x * x
    t = jnp.tanh(_GELU_C * (x + _GELU_A * (x2 * x)))
    cdf = 0.5 * (1.0 + t)
    dcdf = 0.5 * (1.0 - t * t) * (_GELU_C * (1.0 + 3.0 * _GELU_A * x2))
    return x * cdf, cdf + x * dcdf


def _rows(i, n):
    return pl.ds(pl.multiple_of(i * n, n), n)


def _grouped_loop(n, group, body, init):
    assert n % group == 0

    def trip(i, c):
        for u in range(group):
            c = body(i * group + u, c)
        return c
    return lax.fori_loop(0, n // group, trip, init)


def _mean(x):
    return jnp.mean(x, axis=-1, keepdims=True)


def _colsum(x):
    return jnp.sum(x, axis=0, keepdims=True)


def _tok(ts, d):
    return pl.BlockSpec((ts, d), lambda i: (i, 0))


def _zcol(ts, d, n):
    return pl.BlockSpec((ts, d), lambda i: (i, n))


def _row1(d):
    return pl.BlockSpec((1, d), lambda i: (0, 0))


def _plane(ts, d, n):
    return pl.BlockSpec((None, ts, d), lambda i: (n, i, 0))


class _Hosted:
    def __init__(self, ins, out_shapes, sem_shapes, phases):
        self.ins, self.out_shapes, self.sem_shapes, self.phases = ins, out_shapes, sem_shapes, phases


def _host(comm, n_in, n_out, n_scr, grid, body):
    if comm is None:
        return body, [], [], [], [], None
    ci, co = len(comm.ins), len(comm.out_shapes)
    total = math.prod(grid)
    at = {"first": 0, "late": (3 * total) // 4, "last": total - 1}

    def wrapped(*refs):
        ins, cin = refs[:n_in], refs[n_in:n_in + ci]
        o0 = n_in + ci
        outs, cout = refs[o0:o0 + n_out], refs[o0 + n_out:o0 + n_out + co]
        s0 = o0 + n_out + co
        scr, csem = refs[s0:s0 + n_scr], refs[s0 + n_scr:]
        step = pl.program_id(0)
        for a in range(1, len(grid)):
            step = step * grid[a] + pl.program_id(a)

        def run(where):
            for w, fn in comm.phases:
                if w == where:
                    @pl.when(step == at[where])
                    def _(fn=fn):
                        fn(cin, cout, csem)
        run("first")
        run("late")
        body(*ins, *outs, *scr)
        run("last")

    return wrapped, [ANY] * ci, [ANY] * co, list(comm.out_shapes), list(comm.sem_shapes), ("arbitrary",) * len(grid)


def _in_proj(x, g, w_all, l, comm=None):
    S, D = x.shape
    N = w_all.shape[2]
    tm, tn = _tile(S, 1024), D
    grid = (S // tm, N // tn)

    def body(x_ref, g_ref, w_ref, z_ref, hb_ref, h_scr):
        @pl.when(pl.program_id(1) == 0)
        def _():
            def slab(i, c):
                r = _rows(i, 16)
                xs = x_ref[r, :]
                h = (xs * lax.rsqrt(_mean(xs * xs) + RMS_EPS)) * g_ref[...]
                hb = h.astype(BF16)
                h_scr[r, :] = hb
                hb_ref[r, :] = hb
                return c
            _grouped_loop(tm // 16, 4, slab, 0)
        z_ref[...] = _dot(h_scr[...], w_ref[...])

    body, c_in, c_out, c_shapes, c_scr, sem = _host(comm, 3, 2, 1, grid, body)
    outs = pl.pallas_call(
        body, name="in_proj", grid=grid,
        in_specs=[pl.BlockSpec((tm, D), lambda i, j: (i, 0)),
                  pl.BlockSpec((1, D), lambda i, j: (0, 0)),
                  pl.BlockSpec((None, D, tn), lambda i, j: (l, 0, j))] + c_in,
        out_specs=[pl.BlockSpec((tm, tn), lambda i, j: (i, j)),
                   pl.BlockSpec((tm, D), lambda i, j: (i, 0))] + c_out,
        out_shape=[jax.ShapeDtypeStruct((S, N), F32), jax.ShapeDtypeStruct((S, D), BF16)] + c_shapes,
        scratch_shapes=[pltpu.VMEM((tm, D), BF16)] + c_scr,
        compiler_params=_params(*(sem or ("parallel", "arbitrary"))),
    )(x, g, w_all, *(comm.ins if comm else []))
    return outs[0], outs[1], outs[2:]


def _branch_a_fwd(z, lng, lnb, t_mix, bias_full):
    S = z.shape[0]
    D = lng.shape[1]
    GD = D // GROUPS
    ts = _tile(S, 256)

    def body(zu_ref, zv_ref, zg_ref, lng_ref, lnb_ref, t_ref, bias_ref, a_ref, v_scr):
        def chunk(ci, carry):
            c0 = pl.multiple_of(ci * CHUNK, CHUNK)

            def slab(si, c):
                r = pl.multiple_of(si * 16, 16)
                vg = _gelu(zv_ref[pl.ds(pl.multiple_of(c0 + r, 16), 16), :])
                xc = vg - _mean(vg)
                y = xc * lax.rsqrt(_mean(xc * xc) + LN_EPS)
                v_scr[pl.ds(r, 16), :] = (y * lng_ref[...] + lnb_ref[...]).astype(BF16)
                return c
            _grouped_loop(CHUNK // 16, 4, slab, 0)
            for g in range(GROUPS):
                cols = slice(g * GD, (g + 1) * GD)
                sv = _dot(t_ref[g], v_scr[:, cols]) + bias_ref[:, cols]
                u = _gelu(zu_ref[pl.ds(c0, CHUNK), cols])
                a_ref[pl.ds(c0, CHUNK), cols] = ((u * sv) * _silu(zg_ref[pl.ds(c0, CHUNK), cols])).astype(BF16)
            return carry
        lax.fori_loop(0, ts // CHUNK, chunk, 0)

    return pl.pallas_call(
        body, name="branch_a_fwd", grid=(S // ts,),
        in_specs=[_zcol(ts, D, 0), _zcol(ts, D, 1), _zcol(ts, D, 2), _row1(D), _row1(D),
                  pl.BlockSpec((GROUPS, CHUNK, CHUNK), lambda i: (0, 0, 0)),
                  pl.BlockSpec((CHUNK, D), lambda i: (0, 0))],
        out_specs=_tok(ts, D),
        out_shape=jax.ShapeDtypeStruct((S, D), BF16),
        scratch_shapes=[pltpu.VMEM((CHUNK, D), BF16)],
        compiler_params=_params("parallel"),
    )(z, z, z, lng, lnb, t_mix, bias_full)


SH = HALO - CONV_K + 1


def _fill_shifts(sh_scr, src_scr, cols, ts):
    for j in range(1, 8):
        for g in range(ts // 32):
            sh_scr[j - 1, pl.ds(g * 32, 32), :] = src_scr[pl.ds(g * 32 + j, 32), cols]
        sh_scr[j - 1, pl.ds(ts, HALO - 8), :] = src_scr[pl.ds(ts + j, HALO - 8), cols]


def _shifted(sh_scr, src_scr, cols, row0, n):
    q, j = divmod(row0, 8)
    if j == 0:
        return src_scr[pl.ds(row0, n), cols]
    return sh_scr[j - 1, pl.ds(8 * q, n), :]


def _conv_fwd(z, w_all, cb, l):
    S = z.shape[0]
    D = cb.shape[1]
    ts = _tile(S, 256)
    CB = 256
    hb = ts // HALO

    def body(za_ref, zb_ref, zap_ref, zbp_ref, w_ref, b_ref, c0_ref, glu_scr, sh_scr):
        i = pl.program_id(0)
        prev = zap_ref[...] * _sigmoid(zbp_ref[...])
        glu_scr[0:HALO, :] = jnp.where(i == 0, 0.0, prev)

        def fill(si, c):
            r = _rows(si, 32)
            glu_scr[pl.ds(pl.multiple_of(HALO + si * 32, 32), 32), :] = za_ref[r, :] * _sigmoid(zb_ref[r, :])
            return c
        _grouped_loop(ts // 32, 2, fill, 0)

        def colblock(ci, c):
            cols = _rows(ci, CB)
            _fill_shifts(sh_scr, glu_scr, cols, ts)
            for s in range(ts // 32):
                acc = jnp.zeros((32, CB), F32)
                for k in range(CONV_K):
                    acc = acc + _shifted(sh_scr, glu_scr, cols, s * 32 + k + SH, 32) * w_ref[k:k + 1, cols]
                c0_ref[pl.ds(s * 32, 32), cols] = acc + b_ref[:, cols]
            return c
        lax.fori_loop(0, D // CB, colblock, 0)

    prev_map = lambda n: (lambda i: (jnp.maximum(i * hb - 1, 0), n))
    return pl.pallas_call(
        body, name="conv_fwd", grid=(S // ts,),
        in_specs=[_zcol(ts, D, 3), _zcol(ts, D, 4),
                  pl.BlockSpec((HALO, D), prev_map(3)), pl.BlockSpec((HALO, D), prev_map(4)),
                  pl.BlockSpec((None, CONV_K, D), lambda i: (l, 0, 0)), _row1(D)],
        out_specs=_tok(ts, D),
        out_shape=jax.ShapeDtypeStruct((S, D), F32),
        scratch_shapes=[pltpu.VMEM((ts + HALO, D), F32), pltpu.VMEM((7, ts + HALO, CB), F32)],
        compiler_params=_params("parallel"),
    )(z, z, z, z, w_all, cb)


def _branch_b_fwd(c0, z, g, b):
    S, D = c0.shape
    ts = _tile(S, 512)

    def body(c0_ref, zg_ref, g_ref, b_ref, o_ref):
        def slab(si, c):
            r = _rows(si, 16)
            v = c0_ref[r, :]
            xc = v - _mean(v)
            c1 = (xc * lax.rsqrt(_mean(xc * xc) + LN_EPS)) * g_ref[...] + b_ref[...]
            o_ref[r, :] = (_silu(c1) * _silu(zg_ref[r, :])).astype(BF16)
            return c
        _grouped_loop(ts // 16, 4, slab, 0)

    return pl.pallas_call(
        body, name="branch_b_fwd", grid=(S // ts,),
        in_specs=[_tok(ts, D), _zcol(ts, D, 5), _row1(D), _row1(D)],
        out_specs=_tok(ts, D),
        out_shape=jax.ShapeDtypeStruct((S, D), BF16),
        compiler_params=_params("parallel"),
    )(c0, z, g, b)


def _kv_fwd(mem, mg, wkv_all, l):
    M, D = mem.shape

    def body(mem_ref, g_ref, w_ref, mn_ref, kv_ref, kvt_ref):
        m = mem_ref[...]
        mn = ((m * lax.rsqrt(_mean(m * m) + RMS_EPS)) * g_ref[...]).astype(BF16)
        mn_ref[...] = mn
        kv = _dot(mn, w_ref[...])
        kv_ref[...] = kv.astype(BF16)
        kvt_ref[...] = kv.T.astype(BF16)

    return pl.pallas_call(
        body, name="kv_fwd", grid=(1,),
        in_specs=[pl.BlockSpec((M, D), lambda i: (0, 0)), _row1(D),
                  pl.BlockSpec((None, D, 2 * D), lambda i: (l, 0, 0))],
        out_specs=[pl.BlockSpec((M, D), lambda i: (0, 0)), pl.BlockSpec((M, 2 * D), lambda i: (0, 0)),
                   pl.BlockSpec((2 * D, M), lambda i: (0, 0))],
        out_shape=[jax.ShapeDtypeStruct((M, D), BF16), jax.ShapeDtypeStruct((M, 2 * D), BF16),
                   jax.ShapeDtypeStruct((2 * D, M), BF16)],
        compiler_params=_params("arbitrary"),
    )(mem, mg, wkv_all)


def _softmax_rows(s):
    e = jnp.exp(s - jnp.max(s, axis=-1, keepdims=True))
    return e / jnp.sum(e, axis=-1, keepdims=True)


def _branch_c_fwd(z, kv, kvt):
    S = z.shape[0]
    M, D2 = kv.shape
    D = D2 // 2
    HD = D // HEADS
    scale = 1.0 / math.sqrt(HD)
    ts = _tile(S, 256)

    def body(q_ref, zg_ref, kt_ref, v_ref, o_ref):
        for h in range(HEADS):
            cols = slice(h * HD, (h + 1) * HD)
            p = _softmax_rows(_dot(q_ref[:, cols].astype(BF16), kt_ref[cols, :]) * scale)
            att = _dot(p.astype(BF16), v_ref[:, cols])
            o_ref[:, cols] = (att * _silu(zg_ref[:, cols])).astype(BF16)

    return pl.pallas_call(
        body, name="branch_c_fwd", grid=(S // ts,),
        in_specs=[_zcol(ts, D, 6), _zcol(ts, D, 7),
                  pl.BlockSpec((D, M), lambda i: (0, 0)), pl.BlockSpec((M, D), lambda i: (0, 1))],
        out_specs=_tok(ts, D),
        out_shape=jax.ShapeDtypeStruct((S, D), BF16),
        compiler_params=_params("parallel"),
    )(z, z, kvt, kv)


def _merge_fwd(bra, brb, brc, z, x, wb_all, wo_all, l):
    S, D = x.shape
    ts = _tile(S, 256)
    CB = 256

    def body(a_ref, b_ref, c_ref, zm0_ref, zm1_ref, zm2_ref, x_ref, wb_ref, wo_ref, m_ref, xo_ref):
        brs = (a_ref, b_ref, c_ref)
        zms = (zm0_ref, zm1_ref, zm2_ref)
        for cb in range(D // CB):
            cols = slice(cb * CB, (cb + 1) * CB)
            acc = None
            for n in range(N_BRANCHES):
                t = _sigmoid(zms[n][:, cols]) * _dot(brs[n][...], wb_ref[n, :, cols])
                acc = t if acc is None else acc + t
            m_ref[:, cols] = acc.astype(BF16)
        for cb in range(D // CB):
            cols = slice(cb * CB, (cb + 1) * CB)
            xo_ref[:, cols] = x_ref[:, cols] + _dot(m_ref[...], wo_ref[:, cols])

    return pl.pallas_call(
        body, name="merge_fwd", grid=(S // ts,),
        in_specs=[_tok(ts, D), _tok(ts, D), _tok(ts, D), _zcol(ts, D, 8), _zcol(ts, D, 9), _zcol(ts, D, 10),
                  _tok(ts, D),
                  pl.BlockSpec((None, N_BRANCHES, D, D), lambda i: (l, 0, 0, 0)),
                  pl.BlockSpec((None, D, D), lambda i: (l, 0, 0))],
        out_specs=[_tok(ts, D), _tok(ts, D)],
        out_shape=[jax.ShapeDtypeStruct((S, D), BF16), jax.ShapeDtypeStruct((S, D), F32)],
        compiler_params=_params("parallel"),
    )(bra, brb, brc, z, z, z, x, wb_all, wo_all)


def _loss_bwd(x, target, fg):
    S, D = x.shape
    ts = _tile(S, 512)

    def body(x_ref, t_ref, g_ref, dx_ref, ls_ref, dg_ref):
        @pl.when(pl.program_id(0) == 0)
        def _():
            ls_ref[...] = jnp.zeros_like(ls_ref)
            dg_ref[...] = jnp.zeros_like(dg_ref)

        def slab(si, c):
            ls, dg = c
            r = _rows(si, 16)
            xs = x_ref[r, :]
            rs = lax.rsqrt(_mean(xs * xs) + RMS_EPS)
            n = xs * rs
            e = n * g_ref[...] - t_ref[r, :]
            dy = e * (1.0 / D)
            dn = dy * g_ref[...]
            dx_ref[r, :] = rs * (dn - n * _mean(dn * n))
            return ls + _colsum(e * e), dg + _colsum(dy * n)
        zero = jnp.zeros((1, D), F32)
        ls, dg = _grouped_loop(ts // 16, 4, slab, (zero, zero))
        ls_ref[...] += ls
        dg_ref[...] += dg

    return pl.pallas_call(
        body, name="loss_bwd", grid=(S // ts,),
        in_specs=[_tok(ts, D), _tok(ts, D), _row1(D)],
        out_specs=[_tok(ts, D), _row1(D), _row1(D)],
        out_shape=[jax.ShapeDtypeStruct((S, D), F32), jax.ShapeDtypeStruct((1, D), F32),
                   jax.ShapeDtypeStruct((1, D), F32)],
        compiler_params=_params("arbitrary"),
    )(x, target, fg)


def _merge_bwd(dxp, bra, brb, brc, z, wb_all, wo_all, l, comm=None):
    S, D = dxp.shape
    ts = _tile(S, 256)
    CB = 256

    def body(d_ref, a_ref, b_ref, c_ref, zm0_ref, zm1_ref, zm2_ref, wb_ref, wo_ref,
             dzm_ref, dproj_ref, dbr_ref, dxb_scr):
        brs = (a_ref, b_ref, c_ref)
        zms = (zm0_ref, zm1_ref, zm2_ref)
        dxb_scr[...] = d_ref[...].astype(BF16)
        for cb in range(D // CB):
            cols = slice(cb * CB, (cb + 1) * CB)
            dm = _dot_tb(dxb_scr[...], wo_ref[cols, :])
            for n in range(N_BRANCHES):
                proj = _dot(brs[n][...], wb_ref[n, :, cols])
                g = _sigmoid(zms[n][:, cols])
                dzm_ref[:, n * D + cb * CB:n * D + (cb + 1) * CB] = ((dm * proj) * (g * (1.0 - g))).astype(BF16)
                dproj_ref[n, :, cols] = (dm * g).astype(BF16)
        for n in range(N_BRANCHES):
            for cb in range(D // CB):
                cols = slice(cb * CB, (cb + 1) * CB)
                dbr_ref[n, :, cols] = _dot_tb(dproj_ref[n], wb_ref[n, cols, :])

    tok3 = lambda: pl.BlockSpec((N_BRANCHES, ts, D), lambda i: (0, i, 0))
    grid = (S // ts,)
    body, c_in, c_out, c_shapes, c_scr, sem = _host(comm, 9, 3, 1, grid, body)
    outs = pl.pallas_call(
        body, name="merge_bwd", grid=grid,
        in_specs=[_tok(ts, D), _tok(ts, D), _tok(ts, D), _tok(ts, D),
                  _zcol(ts, D, 8), _zcol(ts, D, 9), _zcol(ts, D, 10),
                  pl.BlockSpec((None, N_BRANCHES, D, D), lambda i: (l, 0, 0, 0)),
                  pl.BlockSpec((None, D, D), lambda i: (l, 0, 0))] + c_in,
        out_specs=[_tok(ts, 3 * D), tok3(), tok3()] + c_out,
        out_shape=[jax.ShapeDtypeStruct((S, 3 * D), BF16), jax.ShapeDtypeStruct((N_BRANCHES, S, D), BF16),
                   jax.ShapeDtypeStruct((N_BRANCHES, S, D), F32)] + c_shapes,
        scratch_shapes=[pltpu.VMEM((ts, D), BF16)] + c_scr,
        compiler_params=_params(*(sem or ("parallel",))),
    )(dxp, bra, brb, brc, z, z, z, wb_all, wo_all, *(comm.ins if comm else []))
    return outs[0], outs[1], outs[2], outs[3:]


def _branch_a_bwd(z, dbr, lng, lnb, t_mix, t_mix_t, bias_full, mask):
    S = z.shape[0]
    D = lng.shape[1]
    GD = D // GROUPS
    ts = _tile(S, 256)
    nt = S // ts

    def body(zu_ref, zv_ref, zg_ref, d_ref, lng_ref, lnb_ref, t_ref, tt_ref, bias_ref, mask_ref,
             dz_ref, dws_ref, dbs_ref, dlg_ref, dlb_ref,
             v_scr, vn_scr, rstd_scr, dv_scr, dsv_acc):
        i = pl.program_id(0)

        @pl.when(i == 0)
        def _():
            dws_ref[...] = jnp.zeros_like(dws_ref)
            dlg_ref[...] = jnp.zeros_like(dlg_ref)
            dlb_ref[...] = jnp.zeros_like(dlb_ref)
            dsv_acc[...] = jnp.zeros_like(dsv_acc)

        def chunk(ci, carry):
            c0 = pl.multiple_of(ci * CHUNK, CHUNK)

            def ln_fwd(si, c):
                r = pl.multiple_of(si * 16, 16)
                vg = _gelu(zv_ref[pl.ds(pl.multiple_of(c0 + r, 16), 16), :])
                xc = vg - _mean(vg)
                rstd = lax.rsqrt(_mean(xc * xc) + LN_EPS)
                vn = xc * rstd
                vn_scr[pl.ds(r, 16), :] = vn
                rstd_scr[pl.ds(r, 16), :] = rstd
                v_scr[pl.ds(r, 16), :] = (vn * lng_ref[...] + lnb_ref[...]).astype(BF16)
                return c
            _grouped_loop(CHUNK // 16, 4, ln_fwd, 0)

            for g in range(GROUPS):
                cols = slice(g * GD, (g + 1) * GD)
                rows = pl.ds(c0, CHUNK)
                vb = v_scr[:, cols]
                sv = _dot(t_ref[g], vb) + bias_ref[:, cols]
                u, du = _gelu_and_grad(zu_ref[rows, cols])
                sg, dsg = _silu_and_grad(zg_ref[rows, cols])
                d = d_ref[rows, cols]
                dsv = (d * u) * sg
                dz_ref[rows, g * GD:(g + 1) * GD] = (((d * sv) * sg) * du).astype(BF16)
                dz_ref[rows, 2 * D + g * GD:2 * D + (g + 1) * GD] = (((d * u) * sv) * dsg).astype(BF16)
                dsvb = dsv.astype(BF16)
                dws_ref[g] += _dot_tb(dsvb, vb)
                dv_scr[:, cols] = _dot(tt_ref[g], dsvb)
                dsv_acc[:, cols] += dsv

            def ln_bwd(si, c):
                dlg, dlb = c
                r = pl.multiple_of(si * 16, 16)
                rr = pl.ds(r, 16)
                zrows = pl.ds(pl.multiple_of(c0 + r, 16), 16)
                dv = dv_scr[rr, :]
                vn = vn_scr[rr, :]
                dvn = dv * lng_ref[...]
                dvg = rstd_scr[rr, :] * ((dvn - _mean(dvn)) - vn * _mean(dvn * vn))
                _, gg = _gelu_and_grad(zv_ref[zrows, :])
                dz_ref[zrows, D:2 * D] = (dvg * gg).astype(BF16)
                return dlg + _colsum(dv * vn), dlb + _colsum(dv)
            zero = jnp.zeros((1, D), F32)
            dlg, dlb = _grouped_loop(CHUNK // 16, 4, ln_bwd, (zero, zero))
            dlg_ref[...] += dlg
            dlb_ref[...] += dlb
            return carry
        lax.fori_loop(0, ts // CHUNK, chunk, 0)

        @pl.when(i == nt - 1)
        def _():
            for g in range(GROUPS):
                cols = slice(g * GD, (g + 1) * GD)
                dws_ref[g] = dws_ref[g] * mask_ref[...]
                dbs_ref[g:g + 1, :] = _colsum(dsv_acc[:, cols].T)

    c2 = lambda shape: pl.BlockSpec(shape, lambda i: (0, 0))
    c3 = lambda: pl.BlockSpec((GROUPS, CHUNK, CHUNK), lambda i: (0, 0, 0))
    return pl.pallas_call(
        body, name="branch_a_bwd", grid=(nt,),
        in_specs=[_zcol(ts, D, 0), _zcol(ts, D, 1), _zcol(ts, D, 2), _plane(ts, D, 0), _row1(D), _row1(D),
                  c3(), c3(), c2((CHUNK, D)), c2((CHUNK, CHUNK))],
        out_specs=[_tok(ts, 3 * D), c3(), c2((GROUPS, CHUNK)), _row1(D), _row1(D)],
        out_shape=[jax.ShapeDtypeStruct((S, 3 * D), BF16), jax.ShapeDtypeStruct((GROUPS, CHUNK, CHUNK), F32),
                   jax.ShapeDtypeStruct((GROUPS, CHUNK), F32), jax.ShapeDtypeStruct((1, D), F32),
                   jax.ShapeDtypeStruct((1, D), F32)],
        scratch_shapes=[pltpu.VMEM((CHUNK, D), BF16), pltpu.VMEM((CHUNK, D), F32), pltpu.VMEM((CHUNK, 1), F32),
                        pltpu.VMEM((CHUNK, D), F32), pltpu.VMEM((CHUNK, D), F32)],
        compiler_params=_params("arbitrary"),
    )(z, z, z, dbr, lng, lnb, t_mix, t_mix_t, bias_full, mask)


def _branch_b_bwd(c0, z, dbr, g, b):
    S, D = c0.shape
    ts = _tile(S, 512)

    def body(c0_ref, zg_ref, d_ref, g_ref, b_ref, dc0_ref, dz_ref, dg_ref, db_ref, dcb_ref):
        @pl.when(pl.program_id(0) == 0)
        def _():
            dg_ref[...] = jnp.zeros_like(dg_ref)
            db_ref[...] = jnp.zeros_like(db_ref)
            dcb_ref[...] = jnp.zeros_like(dcb_ref)

        def slab(si, c):
            dg, db, dcb = c
            r = _rows(si, 16)
            v = c0_ref[r, :]
            xc = v - _mean(v)
            rstd = lax.rsqrt(_mean(xc * xc) + LN_EPS)
            cn = xc * rstd
            c1 = cn * g_ref[...] + b_ref[...]
            c2, dc2_dc1 = _silu_and_grad(c1)
            sg, dsg = _silu_and_grad(zg_ref[r, :])
            d = d_ref[r, :]
            dz_ref[r, :] = ((d * c2) * dsg).astype(BF16)
            dc1 = (d * sg) * dc2_dc1
            dcn = dc1 * g_ref[...]
            dc0 = rstd * ((dcn - _mean(dcn)) - cn * _mean(dcn * cn))
            dc0_ref[r, :] = dc0
            return dg + _colsum(dc1 * cn), db + _colsum(dc1), dcb + _colsum(dc0)
        zero = jnp.zeros((1, D), F32)
        dg, db, dcb = _grouped_loop(ts // 16, 4, slab, (zero, zero, zero))
        dg_ref[...] += dg
        db_ref[...] += db
        dcb_ref[...] += dcb

    return pl.pallas_call(
        body, name="branch_b_bwd", grid=(S // ts,),
        in_specs=[_tok(ts, D), _zcol(ts, D, 5), _plane(ts, D, 1), _row1(D), _row1(D)],
        out_specs=[_tok(ts, D), _zcol(ts, D, 2), _row1(D), _row1(D), _row1(D)],
        out_shape=[jax.ShapeDtypeStruct((S, D), F32), jax.ShapeDtypeStruct((S, 3 * D), BF16),
                   jax.ShapeDtypeStruct((1, D), F32), jax.ShapeDtypeStruct((1, D), F32),
                   jax.ShapeDtypeStruct((1, D), F32)],
        compiler_params=_params("arbitrary"),
    )(c0, z, dbr, g, b)


def _conv_bwd(dc0, z, w_all, dz_b, l):
    S, D = dc0.shape
    ts = _tile(S, 256)
    nt = S // ts
    CB = 256
    hb = ts // HALO
    nh = S // HALO
    SH = HALO - CONV_K + 1

    def body(dc_ref, dcn_ref, za_ref, zb_ref, zap_ref, zbp_ref, w_ref, dzin_ref,
             dz_ref, dw_ref, glu_scr, dc_scr, dw_acc, gsh_scr, dsh_scr):
        del dzin_ref
        i = pl.program_id(0)

        @pl.when(i == 0)
        def _():
            dw_acc[...] = jnp.zeros_like(dw_acc)

        prev = zap_ref[...] * _sigmoid(zbp_ref[...])
        glu_scr[0:HALO, :] = jnp.where(i == 0, 0.0, prev)
        dc_scr[ts:ts + HALO, :] = jnp.where(i == nt - 1, 0.0, dcn_ref[...])

        def fill(si, c):
            r = _rows(si, 32)
            glu_scr[pl.ds(pl.multiple_of(HALO + si * 32, 32), 32), :] = za_ref[r, :] * _sigmoid(zb_ref[r, :])
            dc_scr[r, :] = dc_ref[r, :]
            return c
        _grouped_loop(ts // 32, 2, fill, 0)

        def colblock(ci, c):
            cols = _rows(ci, CB)
            cols_b = pl.ds(pl.multiple_of(D + ci * CB, CB), CB)
            _fill_shifts(gsh_scr, glu_scr, cols, ts)
            _fill_shifts(dsh_scr, dc_scr, cols, ts)
            for s in range(ts // 32):
                rows = pl.ds(s * 32, 32)
                dc = dc_scr[rows, cols]
                acc = jnp.zeros((32, CB), F32)
                for k in range(CONV_K):
                    prod = dc * _shifted(gsh_scr, glu_scr, cols, s * 32 + k + SH, 32)
                    dw_acc[k, :, cols] += (prod[0:8] + prod[8:16]) + (prod[16:24] + prod[24:32])
                    acc = acc + (_shifted(dsh_scr, dc_scr, cols, s * 32 + k, 32)
                                 * w_ref[CONV_K - 1 - k:CONV_K - k, cols])
                sg = _sigmoid(zb_ref[rows, cols])
                dz_ref[rows, cols] = (acc * sg).astype(BF16)
                dz_ref[rows, cols_b] = ((acc * za_ref[rows, cols]) * (sg * (1.0 - sg))).astype(BF16)
            return c
        lax.fori_loop(0, D // CB, colblock, 0)

        @pl.when(i == nt - 1)
        def _():
            for k in range(CONV_K):
                dw_ref[k:k + 1, :] = _colsum(dw_acc[k])

    prev_map = lambda n: (lambda i: (jnp.maximum(i * hb - 1, 0), n))
    return pl.pallas_call(
        body, name="conv_bwd", grid=(nt,),
        in_specs=[_tok(ts, D), pl.BlockSpec((HALO, D), lambda i: (jnp.minimum((i + 1) * hb, nh - 1), 0)),
                  _zcol(ts, D, 3), _zcol(ts, D, 4),
                  pl.BlockSpec((HALO, D), prev_map(3)), pl.BlockSpec((HALO, D), prev_map(4)),
                  pl.BlockSpec((None, CONV_K, D), lambda i: (l, 0, 0)), ANY],
        out_specs=[_tok(ts, 2 * D), pl.BlockSpec((CONV_K, D), lambda i: (0, 0))],
        out_shape=[jax.ShapeDtypeStruct((S, 3 * D), BF16), jax.ShapeDtypeStruct((CONV_K, D), F32)],
        scratch_shapes=[pltpu.VMEM((ts + HALO, D), F32), pltpu.VMEM((ts + HALO, D), F32),
                        pltpu.VMEM((CONV_K, 8, D), F32),
                        pltpu.VMEM((7, ts + HALO, CB), F32), pltpu.VMEM((7, ts + HALO, CB), F32)],
        input_output_aliases={7: 0},
        compiler_params=_params("arbitrary"),
    )(dc0, dc0, z, z, z, z, w_all, dz_b)


def _branch_c_bwd(z, dbr, kv, kvt):
    S = z.shape[0]
    M, D2 = kv.shape
    D = D2 // 2
    HD = D // HEADS
    scale = 1.0 / math.sqrt(HD)
    ts = _tile(S, 256)

    def body(q_ref, zg_ref, d_ref, kt_ref, vt_ref, k_ref, v_ref, dz_ref, dk_ref, dv_ref):
        @pl.when(pl.program_id(0) == 0)
        def _():
            dk_ref[...] = jnp.zeros_like(dk_ref)
            dv_ref[...] = jnp.zeros_like(dv_ref)

        for h in range(HEADS):
            cols = slice(h * HD, (h + 1) * HD)
            qb = q_ref[:, cols].astype(BF16)
            p = _softmax_rows(_dot(qb, kt_ref[cols, :]) * scale)
            pb = p.astype(BF16)
            att = _dot(pb, v_ref[:, cols])
            sg, dsg = _silu_and_grad(zg_ref[:, cols])
            d = d_ref[:, cols]
            dz_ref[:, D + h * HD:D + (h + 1) * HD] = ((d * att) * dsg).astype(BF16)
            datt = (d * sg).astype(BF16)
            dp = _dot(datt, vt_ref[cols, :])
            dv_ref[:, cols] += _dot_ta(pb, datt)
            ds = ((p * (dp - jnp.sum(dp * p, axis=-1, keepdims=True))) * scale).astype(BF16)
            dz_ref[:, cols] = _dot(ds, k_ref[:, cols]).astype(BF16)
            dk_ref[:, cols] += _dot_ta(ds, qb)

    return pl.pallas_call(
        body, name="branch_c_bwd", grid=(S // ts,),
        in_specs=[_zcol(ts, D, 6), _zcol(ts, D, 7), _plane(ts, D, 2),
                  pl.BlockSpec((D, M), lambda i: (0, 0)), pl.BlockSpec((D, M), lambda i: (1, 0)),
                  pl.BlockSpec((M, D), lambda i: (0, 0)), pl.BlockSpec((M, D), lambda i: (0, 1))],
        out_specs=[_tok(ts, 2 * D), pl.BlockSpec((M, D), lambda i: (0, 0)), pl.BlockSpec((M, D), lambda i: (0, 0))],
        out_shape=[jax.ShapeDtypeStruct((S, 2 * D), BF16), jax.ShapeDtypeStruct((M, D), F32),
                   jax.ShapeDtypeStruct((M, D), F32)],
        compiler_params=_params("arbitrary"),
    )(z, z, dbr, kvt, kvt, kv, kv)


def _kv_bwd(dk, dv, mn, mem, wkv_all, l, prev):
    M, D = mem.shape

    def body(*refs):
        dk_ref, dv_ref, mn_ref, mem_ref, w_ref = refs[:5]
        dw_ref, dg_ref = refs[-2:]
        dkb = dk_ref[...].astype(BF16)
        dvb = dv_ref[...].astype(BF16)
        dw_ref[:, 0:D] = _dot_ta(mn_ref[...], dkb)
        dw_ref[:, D:2 * D] = _dot_ta(mn_ref[...], dvb)
        dmn = _dot_tb(dkb, w_ref[:, 0:D]) + _dot_tb(dvb, w_ref[:, D:2 * D])
        m = mem_ref[...]
        dg_ref[...] = _colsum(dmn * (m * lax.rsqrt(_mean(m * m) + RMS_EPS)))

    full = lambda shape: pl.BlockSpec(shape, lambda i: (0, 0))
    in_specs = [full((M, D)), full((M, D)), full((M, D)), full((M, D)),
                pl.BlockSpec((None, D, 2 * D), lambda i: (l, 0, 0))]
    args = [dk, dv, mn, mem, wkv_all]
    aliases = {}
    if prev is not None:
        in_specs.append(ANY)
        args.append(prev)
        aliases = {5: 0}
    return pl.pallas_call(
        body, name="kv_bwd", grid=(1,),
        in_specs=in_specs,
        out_specs=[pl.BlockSpec((None, D, 2 * D), lambda i: (l, 0, 0)), _row1(D)],
        out_shape=[jax.ShapeDtypeStruct((wkv_all.shape[0], D, 2 * D), F32), jax.ShapeDtypeStruct((1, D), F32)],
        input_output_aliases=aliases,
        compiler_params=_params("arbitrary"),
    )(*args)


def _in_proj_bwd(pieces, w_all, x, g, dxp, l, comm=None):
    S, D = x.shape
    tm = _tile(S, 512)
    n_p = len(pieces)

    def body(*refs):
        dz_refs = refs[:n_p]
        w_ref, x_ref, g_ref, dxp_ref, dx_ref, dg_ref, acc = refs[n_p:]
        i = pl.program_id(0)
        k = pl.program_id(1)

        @pl.when(k == 0)
        def _():
            acc[...] = jnp.zeros_like(acc)

        @pl.when((i == 0) & (k == 0))
        def _():
            dg_ref[...] = jnp.zeros_like(dg_ref)

        for (_, lo, n), r in zip(pieces, dz_refs):
            @pl.when((k >= lo) & (k < lo + n))
            def _(r=r):
                acc[...] += _dot_tb(r[...], w_ref[...])

        @pl.when(k == N_PLANES - 1)
        def _():
            def slab(si, c):
                rr = _rows(si, 16)
                xs = x_ref[rr, :]
                rs = lax.rsqrt(_mean(xs * xs) + RMS_EPS)
                n = xs * rs
                dh = acc[rr, :]
                dn = dh * g_ref[...]
                dx_ref[rr, :] = rs * (dn - n * _mean(dn * n)) + dxp_ref[rr, :]
                return c + _colsum(dh * n)
            dg_ref[...] += _grouped_loop(tm // 16, 4, slab, jnp.zeros((1, D), F32))

    def piece_spec(lo, n):
        return pl.BlockSpec((tm, D), lambda i, k: (i, jnp.clip(k - lo, 0, n - 1)))

    tokk = lambda: pl.BlockSpec((tm, D), lambda i, k: (i, 0))
    rowk = lambda: pl.BlockSpec((1, D), lambda i, k: (0, 0))
    grid = (S // tm, N_PLANES)
    body, c_in, c_out, c_shapes, c_scr, _ = _host(comm, n_p + 4, 2, 1, grid, body)
    outs = pl.pallas_call(
        body, name="in_proj_bwd", grid=grid,
        in_specs=[piece_spec(lo, n) for _, lo, n in pieces]
        + [pl.BlockSpec((None, D, D), lambda i, k: (l, 0, k)), tokk(), rowk(), tokk()] + c_in,
        out_specs=[tokk(), rowk()] + c_out,
        out_shape=[jax.ShapeDtypeStruct((S, D), F32), jax.ShapeDtypeStruct((1, D), F32)] + c_shapes,
        scratch_shapes=[pltpu.VMEM((tm, D), F32)] + c_scr,
        compiler_params=_params("arbitrary", "arbitrary"),
    )(*[p for p, _, _ in pieces], w_all, x, g, dxp, *(comm.ins if comm else []))
    return outs[0], outs[1], outs[2:]


def _wgrad(a, b, out_shape, plane, col_off, prev, a_plane=None, b_plane=None):
    S = a.shape[-2]
    K1 = a.shape[-1]
    nb_cols = b.shape[-1]
    ts = _tile(S, 1024)
    tn = _tile(nb_cols, 1024)

    def body(*refs):
        a_ref, b_ref = refs[:2]
        o_ref = refs[-1]
        k = pl.program_id(1)
        prod = _dot_ta(a_ref[...].astype(BF16), b_ref[...].astype(BF16))

        @pl.when(k == 0)
        def _():
            o_ref[...] = prod

        @pl.when(k > 0)
        def _():
            o_ref[...] += prod

    if a_plane is None:
        a_spec = pl.BlockSpec((ts, K1), lambda j, k: (k, 0))
    else:
        a_spec = pl.BlockSpec((None, ts, K1), lambda j, k: (a_plane, k, 0))
    if b_plane is None:
        b_spec = pl.BlockSpec((ts, tn), lambda j, k: (k, j))
    else:
        b_spec = pl.BlockSpec((None, ts, tn), lambda j, k: (b_plane, k, j))
    in_specs, args, aliases = [a_spec, b_spec], [a, b], {}
    if prev is not None:
        in_specs.append(ANY)
        args.append(prev)
        aliases = {2: 0}
    return pl.pallas_call(
        body, name="wgrad", grid=(nb_cols // tn, S // ts),
        in_specs=in_specs,
        out_specs=pl.BlockSpec((None, K1, tn), lambda j, k: (plane, 0, col_off + j)),
        out_shape=jax.ShapeDtypeStruct(out_shape, F32),
        input_output_aliases=aliases,
        compiler_params=_params("parallel", "arbitrary"),
    )(*args)


def _position():
    x, y, c = lax.axis_index("x"), lax.axis_index("y"), lax.axis_index("c")
    chips = [(1 - x, y), (x, 1 - y), (1 - x, 1 - y)]
    return x, y, c, chips


def _sub(ref, axis, start, size):
    idx = [slice(None)] * len(ref.shape)
    idx[axis] = pl.ds(start, size)
    return ref.at[tuple(idx)]


def _dma(src, dst, ssem, rsem, to):
    return pltpu.make_async_remote_copy(src_ref=src, dst_ref=dst, send_sem=ssem, recv_sem=rsem,
                                        device_id=to, device_id_type=MESH)


def _dma_sems(*counts):
    return [pltpu.SemaphoreType.DMA((n,)) for n in counts]


def _run_alone(comm, name):
    ci, co = len(comm.ins), len(comm.out_shapes)

    def body(*refs):
        for _, fn in comm.phases:
            fn(refs[:ci], refs[ci:ci + co], refs[ci + co:])

    return pl.pallas_call(
        body, name=name, in_specs=[ANY] * ci, out_specs=[ANY] * co, out_shape=list(comm.out_shapes),
        scratch_shapes=list(comm.sem_shapes),
    )(*comm.ins)


def _gather_comm(shards, l, axes):
    n = len(shards)
    out_shapes = []
    for s, (ba, _) in zip(shards, axes):
        shp = list(s.shape[1:])
        shp[ba] *= N_CHIPS
        out_shapes.append(jax.ShapeDtypeStruct(tuple(shp), s.dtype))

    def half(ref, t, h):
        ha = axes[t][1]
        if ha is None:
            return ref
        hs = shards[t].shape[1 + ha] // 2
        return _sub(ref, ha, h * hs, hs)

    def block(outs, t, blk):
        ba = axes[t][0]
        bs = shards[t].shape[1 + ba]
        return _sub(outs[t], ba, blk * bs, bs)

    def start(ins, outs, sems):
        s_own, r_own, s_ici, r_ici, _, _ = sems
        x, y, c, chips = _position()
        j = 2 * x + y
        for t in range(n):
            _dma(ins[t].at[l], block(outs, t, j), s_own.at[t], r_own.at[t], (x, y, 1 - c)).start()
            for k, (px, py) in enumerate(chips):
                _dma(half(ins[t].at[l], t, c), half(block(outs, t, j), t, c),
                     s_ici.at[3 * t + k], r_ici.at[3 * t + k], (px, py, c)).start()

    def forward(ins, outs, sems):
        _, _, s_ici, r_ici, s_fwd, r_fwd = sems
        x, y, c, chips = _position()
        for t in range(n):
            if axes[t][1] is None:
                continue
            for k, (px, py) in enumerate(chips):
                got = half(block(outs, t, 2 * px + py), t, c)
                _dma(got, got, s_ici.at[3 * t + k], r_ici.at[3 * t + k], (px, py, c)).wait_recv()
                _dma(got, got, s_fwd.at[3 * t + k], r_fwd.at[3 * t + k], (x, y, 1 - c)).start()

    def finish(ins, outs, sems):
        s_own, r_own, s_ici, r_ici, s_fwd, r_fwd = sems
        x, y, c, chips = _position()
        j = 2 * x + y
        sib = (x, y, 1 - c)
        for t in range(n):
            for k, (px, py) in enumerate(chips):
                i = 3 * t + k
                sent = half(block(outs, t, j), t, c)
                _dma(sent, sent, s_ici.at[i], r_ici.at[i], (px, py, c)).wait_send()
                theirs = block(outs, t, 2 * px + py)
                if axes[t][1] is None:
                    _dma(theirs, theirs, s_ici.at[i], r_ici.at[i], (px, py, c)).wait_recv()
                else:
                    got, other = half(theirs, t, c), half(theirs, t, 1 - c)
                    _dma(got, got, s_fwd.at[i], r_fwd.at[i], sib).wait_send()
                    _dma(other, other, s_fwd.at[i], r_fwd.at[i], sib).wait_recv()
            _dma(ins[t].at[l], block(outs, t, j), s_own.at[t], r_own.at[t], sib).wait()

    return _Hosted(list(shards), out_shapes, _dma_sems(n, n, 3 * n, 3 * n, 3 * n, 3 * n),
                   [("first", start), ("late", forward), ("last", finish)])


def _halved(shape, axis):
    shp = list(shape)
    shp[axis] //= 2
    return tuple(shp)


def _pair_comm(grads, axes):
    n = len(grads)

    def copies(ins, outs, sems):
        x, y, c, _ = _position()
        cps = []
        for t in range(n):
            ha = axes[t][1]
            hs = grads[t].shape[ha] // 2
            cps.append(_dma(_sub(ins[t], ha, (1 - c) * hs, hs), outs[t], sems[0].at[t], sems[1].at[t], (x, y, 1 - c)))
        return cps

    def start(ins, outs, sems):
        for cp in copies(ins, outs, sems):
            cp.start()

    def finish(ins, outs, sems):
        for cp in copies(ins, outs, sems):
            cp.wait()

    return _Hosted(list(grads), [jax.ShapeDtypeStruct(_halved(g.shape, ax[1]), g.dtype) for g, ax in zip(grads, axes)],
                   _dma_sems(n, n), [("first", start), ("last", finish)])


def _block_comm(sums, axes):
    n = len(sums)
    out_shapes = []
    for s, (ba, _) in zip(sums, axes):
        shp = list(s.shape)
        shp[ba] //= N_CHIPS
        out_shapes.append(jax.ShapeDtypeStruct((3,) + tuple(shp), s.dtype))

    def copies(ins, outs, sems):
        x, y, c, chips = _position()
        cps = []
        for t in range(n):
            ba = axes[t][0]
            bs = sums[t].shape[ba] // N_CHIPS
            for k, (px, py) in enumerate(chips):
                cps.append(_dma(_sub(ins[t], ba, (2 * px + py) * bs, bs), outs[t].at[k],
                                sems[0].at[3 * t + k], sems[1].at[3 * t + k], (px, py, c)))
        return cps

    def start(ins, outs, sems):
        for cp in copies(ins, outs, sems):
            cp.start()

    def finish(ins, outs, sems):
        for cp in copies(ins, outs, sems):
            cp.wait()

    return _Hosted(list(sums), out_shapes, _dma_sems(3 * n, 3 * n), [("first", start), ("last", finish)])


def _sibling_share(shards, axes):
    n = len(shards)

    def body(*refs):
        outs = refs[n:2 * n]
        ssem, rsem = refs[2 * n:]
        x, y, c, _ = _position()
        cps = []
        for t in range(n):
            ha = axes[t][1]
            hs = shards[t].shape[ha] // 2
            mine = _sub(outs[t], ha, c * hs, hs)
            cps.append(_dma(mine, mine, ssem.at[t], rsem.at[t], (x, y, 1 - c)))
        for cp in cps:
            cp.start()
        for cp in cps:
            cp.wait()

    return pl.pallas_call(
        body, name="sibling_share",
        in_specs=[ANY] * n, out_specs=[ANY] * n,
        out_shape=[jax.ShapeDtypeStruct(s.shape, s.dtype) for s in shards],
        input_output_aliases={t: t for t in range(n)},
        scratch_shapes=_dma_sems(n, n),
    )(*shards)


def _small_allreduce(p):
    R, C = p.shape

    def body(p_ref, o_ref, sib_buf, chip_sums, ssem, rsem):
        x, y, c, chips = _position()
        j = 2 * x + y
        sib = pltpu.make_async_remote_copy(
            src_ref=p_ref, dst_ref=sib_buf, send_sem=ssem.at[0], recv_sem=rsem.at[0],
            device_id=(x, y, 1 - c), device_id_type=MESH)
        sib.start()
        sib.wait()
        chip_sums[j] = p_ref[...] + sib_buf[...]
        cps = [pltpu.make_async_remote_copy(
            src_ref=chip_sums.at[j], dst_ref=chip_sums.at[j], send_sem=ssem.at[1 + k], recv_sem=rsem.at[1 + k],
            device_id=(px, py, c), device_id_type=MESH) for k, (px, py) in enumerate(chips)]
        for cp in cps:
            cp.start()
        for k, (px, py) in enumerate(chips):
            pltpu.make_async_remote_copy(
                src_ref=chip_sums.at[j], dst_ref=chip_sums.at[2 * px + py],
                send_sem=ssem.at[1 + k], recv_sem=rsem.at[1 + k],
                device_id=(px, py, c), device_id_type=MESH).wait()
        o_ref[...] = ((chip_sums[0] + chip_sums[1]) + chip_sums[2]) + chip_sums[3]

    vm = pl.BlockSpec(memory_space=pltpu.VMEM)
    return pl.pallas_call(
        body, name="small_allreduce",
        in_specs=[vm], out_specs=vm, out_shape=jax.ShapeDtypeStruct((R, C), F32),
        scratch_shapes=[pltpu.VMEM((R, C), F32), pltpu.VMEM((N_CHIPS, R, C), F32),
                        pltpu.SemaphoreType.DMA((4,)), pltpu.SemaphoreType.DMA((4,))],
        compiler_params=pltpu.CompilerParams(vmem_limit_bytes=VMEM_LIMIT_BYTES),
    )(p)


def _add_half(g, recv, half_axis, pos):
    B, R, C = recv.shape
    tr = _tile(R, 256)
    tc = _tile(C, 2816)
    nr, nc = R // tr, C // tc
    if half_axis == 1:
        g_map = lambda b, i, j, pos_ref: (b, pos_ref[0] * nr + i, j)
    else:
        g_map = lambda b, i, j, pos_ref: (b, i, pos_ref[0] * nc + j)

    def body(pos_ref, g_ref, r_ref, o_ref, ob_ref):
        del pos_ref
        s = g_ref[...] + r_ref[...]
        o_ref[...] = s
        ob_ref[...] = s.astype(BF16)

    spec = lambda: pl.BlockSpec((None, tr, tc), lambda b, i, j, pos_ref: (b, i, j))
    return pl.pallas_call(
        body, name="add_half",
        grid_spec=pltpu.PrefetchScalarGridSpec(
            num_scalar_prefetch=1, grid=(B, nr, nc),
            in_specs=[pl.BlockSpec((None, tr, tc), g_map), spec()],
            out_specs=[spec(), spec()]),
        out_shape=[jax.ShapeDtypeStruct((B, R, C), F32), jax.ShapeDtypeStruct((B, R, C), BF16)],
        compiler_params=_params("parallel", "parallel", "parallel"),
    )(pos, g, recv)


def _sum_half(own, recv, axes, pos):
    _, B, rs, cs = recv.shape
    ba, ha = axes
    tr = _tile(rs, 128)
    nr = rs // tr

    def body(pos_ref, o_ref, r0_ref, r1_ref, r2_ref, out_ref):
        del pos_ref
        out_ref[...] = (((o_ref[...] + r0_ref[...].astype(F32)) + r1_ref[...].astype(F32))
                        + r2_ref[...].astype(F32))

    if ba == 2:
        own_spec = pl.BlockSpec((None, tr, cs), lambda b, i, pos_ref: (b, i, pos_ref[1]))
    else:
        own_spec = pl.BlockSpec((None, tr, cs), lambda b, i, pos_ref: (b, pos_ref[1] * nr + i, 0))
    if ha == 1:
        out_shape, out_map = (B, 2 * rs, cs), (lambda b, i, pos_ref: (b, pos_ref[0] * nr + i, 0))
    else:
        out_shape, out_map = (B, rs, 2 * cs), (lambda b, i, pos_ref: (b, i, pos_ref[0]))
    rspec = lambda k: pl.BlockSpec((None, None, tr, cs), lambda b, i, pos_ref: (k, b, i, 0))
    return pl.pallas_call(
        body, name="sum_half",
        grid_spec=pltpu.PrefetchScalarGridSpec(
            num_scalar_prefetch=1, grid=(B, nr),
            in_specs=[own_spec, rspec(0), rspec(1), rspec(2)],
            out_specs=pl.BlockSpec((None, tr, cs), out_map)),
        out_shape=jax.ShapeDtypeStruct(out_shape, F32),
        compiler_params=_params("parallel", "parallel"),
    )(pos, own, recv, recv, recv)


def _adamw(w, g, m, v):
    shape = w.shape
    C = shape[-1]
    R = w.size // C
    tr = R
    for cand in (512, 256, 128, 64, 32, 16, 8):
        if R % cand == 0 and cand * C * 4 <= (1 << 20):
            tr = cand
            break
    bc1 = 1.0 - ADAM_B1 ** ADAM_STEP
    bc2 = 1.0 - ADAM_B2 ** ADAM_STEP

    def body(w_ref, g_ref, m_ref, v_ref, d_ref, mo_ref, vo_ref):
        gg = g_ref[...]
        mn = ADAM_B1 * m_ref[...] + (1.0 - ADAM_B1) * gg
        vn = ADAM_B2 * v_ref[...] + (1.0 - ADAM_B2) * (gg * gg)
        mo_ref[...] = mn
        vo_ref[...] = vn
        d_ref[...] = -ADAM_LR * ((mn / bc1) / (jnp.sqrt(vn / bc2) + ADAM_EPS) + ADAM_WD * w_ref[...])

    spec = lambda: pl.BlockSpec((tr, C), lambda i: (i, 0))
    outs = pl.pallas_call(
        body, name="adamw", grid=(R // tr,),
        in_specs=[spec()] * 4, out_specs=[spec()] * 3,
        out_shape=[jax.ShapeDtypeStruct((R, C), F32)] * 3,
        compiler_params=_params("parallel"),
    )(*[a.reshape(R, C) for a in (w, g, m, v)])
    return tuple(o.reshape(shape) for o in outs)


def kernel(x, mem, norm_g, mem_norm_g, w_in, gmlp_ln_g, gmlp_ln_b, w_s, b_s, conv_w, conv_b, conv_ln_g, conv_ln_b, w_kv, w_branch, w_out, final_norm_g, loss_target, m_norm_g, m_mem_norm_g, m_w_in, m_gmlp_ln_g, m_gmlp_ln_b, m_w_s, m_b_s, m_conv_w, m_conv_b, m_conv_ln_g, m_conv_ln_b, m_w_kv, m_w_branch, m_w_out, m_final_norm_g, v_norm_g, v_mem_norm_g, v_w_in, v_gmlp_ln_g, v_gmlp_ln_b, v_w_s, v_b_s, v_conv_w, v_conv_b, v_conv_ln_g, v_conv_ln_b, v_w_kv, v_w_branch, v_w_out, v_final_norm_g):
    xs, mems, tgt = x[0], mem[0], loss_target[0]
    S, D = xs.shape
    assert D // GROUPS == LANES and S % CHUNK == 0 and w_s.shape[-1] == CHUNK
    L = N_LAYERS

    assert L == 2
    shards = [w_in.astype(BF16)[:, None], w_kv.astype(BF16)[:, None], w_branch.astype(BF16), w_out.astype(BF16)[:, None],
              conv_w[:, None]]
    gather_axes = [(2, 1), (2, 1), (1, 2), (1, 2), (2, None)]
    reduce_axes = gather_axes[:4]
    weights = [_run_alone(_gather_comm(shards, 0, gather_axes), "gather_layer0"), None]

    tri = jnp.tril(jnp.ones((CHUNK, CHUNK), F32))
    t_mix = (w_s * tri).astype(BF16)
    t_mix_t = jnp.swapaxes(t_mix, -1, -2)
    bias_full = jnp.repeat(jnp.swapaxes(b_s, -1, -2), D // GROUPS, axis=-1)
    row = lambda a, l: a[l][None, :]
    xi, yi, ci = lax.axis_index("x"), lax.axis_index("y"), lax.axis_index("c")
    pos = jnp.stack([ci, 2 * xi + yi]).astype(jnp.int32)

    saved = []
    h = xs
    for l in range(L):
        win, wkv, wb, wo, cw = weights[l]
        z, hb, fetched = _in_proj(h, row(norm_g, l), win, 0, _gather_comm(shards, 1, gather_axes) if l == 0 else None)
        if l == 0:
            weights[1] = fetched
        bra = _branch_a_fwd(z, row(gmlp_ln_g, l), row(gmlp_ln_b, l), t_mix[l], bias_full[l])
        c0 = _conv_fwd(z, cw, row(conv_b, l), 0)
        brb = _branch_b_fwd(c0, z, row(conv_ln_g, l), row(conv_ln_b, l))
        mn, kv, kvt = _kv_fwd(mems, row(mem_norm_g, l), wkv, 0)
        brc = _branch_c_fwd(z, kv, kvt)
        merged, h_next = _merge_fwd(bra, brb, brc, z, h, wb[None], wo, 0)
        saved.append((h, z, hb, bra, c0, brb, mn, kv, kvt, brc, merged))
        h = h_next

    dx, loss_cols, d_final_g = _loss_bwd(h, tgt, final_norm_g[None, :])
    loss = lax.psum(0.5 * jnp.sum(loss_cols) / D, ("x", "y", "c"))

    def add_halves(grads, recv):
        return [_add_half(g, r, ax[1], pos) for g, r, ax in zip(grads, recv, reduce_axes)]

    def finish_reduction(sums, blocks):
        halves = [_sum_half(s, b, ax, pos) for (s, _), b, ax in zip(sums, blocks, reduce_axes)]
        return _sibling_share(halves, reduce_axes)

    small = {k: [None] * L for k in ("norm_g", "mem_norm_g", "gmlp_ln_g", "gmlp_ln_b", "w_s", "b_s", "conv_w",
                                     "conv_b", "conv_ln_g", "conv_ln_b")}
    pending, reduced = None, [None] * L
    for l in reversed(range(L)):
        h_in, z, hb, bra, c0, brb, mn, kv, kvt, brc, merged = saved[l]
        win, wkv, wb, wo, cw = weights[l]
        dz_m, dproj, dbr, recv = _merge_bwd(dx, bra, brb, brc, z, wb[None], wo, 0,
                                            _pair_comm(pending, reduce_axes) if pending else None)
        sums = add_halves(pending, recv) if pending else None
        g_o = _wgrad(merged, dx, (1, D, D), 0, 0, None)
        g_b = None
        for n, br in enumerate((bra, brb, brc)):
            g_b = _wgrad(br, dproj, (N_BRANCHES, D, D), n, 0, g_b, b_plane=n)
        dz_a, dws, dbs, dlg, dlb = _branch_a_bwd(z, dbr, row(gmlp_ln_g, l), row(gmlp_ln_b, l),
                                                t_mix[l], t_mix_t[l], bias_full[l], tri)
        dc0, dz_b, dcg, dcbeta, dcb = _branch_b_bwd(c0, z, dbr, row(conv_ln_g, l), row(conv_ln_b, l))
        dz_b, dcw = _conv_bwd(dc0, z, cw, dz_b, 0)
        dz_c, dk, dv = _branch_c_bwd(z, dbr, kv, kvt)
        g_kv, dmg = _kv_bwd(dk, dv, mn, mems, wkv, 0, None)
        pieces = [(dz_a, 0, 3), (dz_b, 3, 3), (dz_c, 6, 2), (dz_m, 8, 3)]
        g_in = None
        for dzp, lo, _ in pieces:
            g_in = _wgrad(hb, dzp, (1, D, N_PLANES * D), 0, lo, g_in)
        dx, dng, blocks = _in_proj_bwd(pieces, win, h_in, row(norm_g, l), dx, 0,
                                       _block_comm([sb for _, sb in sums], reduce_axes) if pending else None)
        if pending:
            reduced[l + 1] = finish_reduction(sums, blocks)
        pending = [g_in, g_kv, g_b, g_o]
        for k, val in (("norm_g", dng), ("mem_norm_g", dmg), ("gmlp_ln_g", dlg), ("gmlp_ln_b", dlb), ("w_s", dws),
                       ("b_s", dbs), ("conv_w", dcw), ("conv_b", dcb), ("conv_ln_g", dcg), ("conv_ln_b", dcbeta)):
            small[k][l] = val
    grad_x = dx[None]

    sums = add_halves(pending, _run_alone(_pair_comm(pending, reduce_axes), "pair_exchange"))
    reduced[0] = finish_reduction(sums, _run_alone(_block_comm([sb for _, sb in sums], reduce_axes), "block_exchange"))
    grad_w_in, grad_w_kv, grad_w_branch, grad_w_out = [
        jnp.stack([reduced[0][t], reduced[1][t]]).reshape(w.shape) for t, w in enumerate((w_in, w_kv, w_branch, w_out))]

    order = [("norm_g", norm_g.shape), ("mem_norm_g", mem_norm_g.shape), ("gmlp_ln_g", gmlp_ln_g.shape),
             ("gmlp_ln_b", gmlp_ln_b.shape), ("w_s", w_s.shape), ("b_s", b_s.shape),
             ("conv_w", (L, CONV_K, D)), ("conv_b", conv_b.shape), ("conv_ln_g", conv_ln_g.shape),
             ("conv_ln_b", conv_ln_b.shape)]
    parts = [jnp.stack([v.reshape(shp[1:]) for v in small[k]]).reshape(-1, LANES) for k, shp in order]
    parts.append(d_final_g.reshape(-1, LANES))
    sizes = [p.shape[0] for p in parts]
    assert all(s % 8 == 0 for s in sizes)
    total = _small_allreduce(jnp.concatenate(parts, axis=0))
    red, off = {}, 0
    for (k, shp), n in zip(order + [("final_norm_g", final_norm_g.shape)], sizes):
        red[k] = total[off:off + n].reshape(shp)
        off += n
    cs = conv_w.shape[-1]
    red["conv_w"] = lax.dynamic_slice_in_dim(red["conv_w"], (2 * xi + yi) * cs, cs, axis=2)
    red.update(w_in=grad_w_in, w_kv=grad_w_kv, w_branch=grad_w_branch, w_out=grad_w_out)

    names = ["norm_g", "mem_norm_g", "w_in", "gmlp_ln_g", "gmlp_ln_b", "w_s", "b_s", "conv_w", "conv_b",
             "conv_ln_g", "conv_ln_b", "w_kv", "w_branch", "w_out", "final_norm_g"]
    weights = dict(norm_g=norm_g, mem_norm_g=mem_norm_g, w_in=w_in, gmlp_ln_g=gmlp_ln_g, gmlp_ln_b=gmlp_ln_b,
                   w_s=w_s, b_s=b_s, conv_w=conv_w, conv_b=conv_b, conv_ln_g=conv_ln_g, conv_ln_b=conv_ln_b,
                   w_kv=w_kv, w_branch=w_branch, w_out=w_out, final_norm_g=final_norm_g)
    ms = dict(norm_g=m_norm_g, mem_norm_g=m_mem_norm_g, w_in=m_w_in, gmlp_ln_g=m_gmlp_ln_g, gmlp_ln_b=m_gmlp_ln_b,
              w_s=m_w_s, b_s=m_b_s, conv_w=m_conv_w, conv_b=m_conv_b, conv_ln_g=m_conv_ln_g, conv_ln_b=m_conv_ln_b,
              w_kv=m_w_kv, w_branch=m_w_branch, w_out=m_w_out, final_norm_g=m_final_norm_g)
    vs = dict(norm_g=v_norm_g, mem_norm_g=v_mem_norm_g, w_in=v_w_in, gmlp_ln_g=v_gmlp_ln_g, gmlp_ln_b=v_gmlp_ln_b,
              w_s=v_w_s, b_s=v_b_s, conv_w=v_conv_w, conv_b=v_conv_b, conv_ln_g=v_conv_ln_g, conv_ln_b=v_conv_ln_b,
              w_kv=v_w_kv, w_branch=v_w_branch, w_out=v_w_out, final_norm_g=v_final_norm_g)
    deltas, new_m, new_v = [], [], []
    for k in names:
        w2 = weights[k] if weights[k].ndim > 1 else weights[k][None, :]
        d, mo, vo = _adamw(w2, red[k].reshape(w2.shape), ms[k].reshape(w2.shape), vs[k].reshape(w2.shape))
        deltas.append(d.reshape(weights[k].shape))
        new_m.append(mo.reshape(weights[k].shape))
        new_v.append(vo.reshape(weights[k].shape))
    grads = [red[k].reshape(weights[k].shape) for k in names]
    return (loss, grad_x, *grads, *deltas, *new_m, *new_v)
```

```python
import math

import jax
import jax.numpy as jnp
from jax import lax
from jax.experimental import pallas as pl
from jax.experimental.pallas import tpu as pltpu

F32 = jnp.float32
BF16 = jnp.bfloat16

N_LAYERS = 2
N_BRANCHES = 3
N_PLANES = 11
N_CHIPS = 4
CHUNK = 128
GROUPS = 8
HEADS = 4
CONV_K = 31
HALO = 32
LANES = 128
RMS_EPS = 1e-6
LN_EPS = 1e-5
ADAM_LR, ADAM_B1, ADAM_B2, ADAM_EPS, ADAM_WD, ADAM_STEP = 0.001, 0.9, 0.999, 1e-08, 0.01, 10
VMEM_LIMIT_BYTES = 48 * 1024 * 1024
MESH = pl.DeviceIdType.MESH
ANY = pl.BlockSpec(memory_space=pl.ANY)


def _params(*sem):
    return pltpu.CompilerParams(dimension_semantics=sem, vmem_limit_bytes=VMEM_LIMIT_BYTES)


def _tile(n, pref):
    t = min(n, pref)
    assert n % t == 0, (n, t)
    return t


def _dot(a, b):
    return jnp.dot(a, b, preferred_element_type=F32)


def _dot_tb(a, b):
    return lax.dot_general(a, b, (((1,), (1,)), ((), ())), preferred_element_type=F32)


def _dot_ta(a, b):
    return lax.dot_general(a, b, (((0,), (0,)), ((), ())), preferred_element_type=F32)


def _sigmoid(x):
    return 1.0 / (1.0 + jnp.exp(-x))


def _silu(x):
    return x * _sigmoid(x)


def _silu_and_grad(x):
    s = _sigmoid(x)
    return x * s, s * (1.0 + x * (1.0 - s))


_GELU_C = math.sqrt(2.0 / math.pi)
_GELU_A = 0.044715


def _gelu(x):
    return x * (0.5 * (1.0 + jnp.tanh(_GELU_C * (x + _GELU_A * (x * x * x)))))


def _gelu_and_grad(x):
    x2 = x * x
    t = jnp.tanh(_GELU_C * (x + _GELU_A * (x2 * x)))
    cdf = 0.5 * (1.0 + t)
    dcdf = 0.5 * (1.0 - t * t) * (_GELU_C * (1.0 + 3.0 * _GELU_A * x2))
    return x * cdf, cdf + x * dcdf


def _rows(i, n):
    return pl.ds(pl.multiple_of(i * n, n), n)


def _grouped_loop(n, group, body, init):
    assert n % group == 0

    def trip(i, c):
        for u in range(group):
            c = body(i * group + u, c)
        return c
    return lax.fori_loop(0, n // group, trip, init)


def _mean(x):
    return jnp.mean(x, axis=-1, keepdims=True)


def _colsum(x):
    return jnp.sum(x, axis=0, keepdims=True)


def _tok(ts, d):
    return pl.BlockSpec((ts, d), lambda i: (i, 0))


def _zcol(ts, d, n):
    return pl.BlockSpec((ts, d), lambda i: (i, n))


def _row1(d):
    return pl.BlockSpec((1, d), lambda i: (0, 0))


def _plane(ts, d, n):
    return pl.BlockSpec((None, ts, d), lambda i: (n, i, 0))


class _Hosted:
    def __init__(self, ins, out_shapes, sem_shapes, phases):
        self.ins, self.out_shapes, self.sem_shapes, self.phases = ins, out_shapes, sem_shapes, phases


def _host(comm, n_in, n_out, n_scr, grid, body):
    if comm is None:
        return body, [], [], [], [], None
    ci, co = len(comm.ins), len(comm.out_shapes)
    total = math.prod(grid)
    at = {"first": 0, "late": (3 * total) // 4, "last": total - 1}

    def wrapped(*refs):
        ins, cin = refs[:n_in], refs[n_in:n_in + ci]
        o0 = n_in + ci
        outs, cout = refs[o0:o0 + n_out], refs[o0 + n_out:o0 + n_out + co]
        s0 = o0 + n_out + co
        scr, csem = refs[s0:s0 + n_scr], refs[s0 + n_scr:]
        step = pl.program_id(0)
        for a in range(1, len(grid)):
            step = step * grid[a] + pl.program_id(a)

        def run(where):
            for w, fn in comm.phases:
                if w == where:
                    @pl.when(step == at[where])
                    def _(fn=fn):
                        fn(cin, cout, csem)
        run("first")
        run("late")
        body(*ins, *outs, *scr)
        run("last")

    return wrapped, [ANY] * ci, [ANY] * co, list(comm.out_shapes), list(comm.sem_shapes), ("arbitrary",) * len(grid)


def _in_proj(x, g, w_all, l, comm=None):
    S, D = x.shape
    N = w_all.shape[2]
    tm, tn = _tile(S, 1024), D
    grid = (S // tm, N // tn)

    def body(x_ref, g_ref, w_ref, z_ref, hb_ref, h_scr):
        @pl.when(pl.program_id(1) == 0)
        def _():
            def slab(i, c):
                r = _rows(i, 16)
                xs = x_ref[r, :]
                h = (xs * lax.rsqrt(_mean(xs * xs) + RMS_EPS)) * g_ref[...]
                hb = h.astype(BF16)
                h_scr[r, :] = hb
                hb_ref[r, :] = hb
                return c
            _grouped_loop(tm // 16, 4, slab, 0)
        z_ref[...] = _dot(h_scr[...], w_ref[...])

    body, c_in, c_out, c_shapes, c_scr, sem = _host(comm, 3, 2, 1, grid, body)
    outs = pl.pallas_call(
        body, name="in_proj", grid=grid,
        in_specs=[pl.BlockSpec((tm, D), lambda i, j: (i, 0)),
                  pl.BlockSpec((1, D), lambda i, j: (0, 0)),
                  pl.BlockSpec((None, D, tn), lambda i, j: (l, 0, j))] + c_in,
        out_specs=[pl.BlockSpec((tm, tn), lambda i, j: (i, j)),
                   pl.BlockSpec((tm, D), lambda i, j: (i, 0))] + c_out,
        out_shape=[jax.ShapeDtypeStruct((S, N), F32), jax.ShapeDtypeStruct((S, D), BF16)] + c_shapes,
        scratch_shapes=[pltpu.VMEM((tm, D), BF16)] + c_scr,
        compiler_params=_params(*(sem or ("parallel", "arbitrary"))),
    )(x, g, w_all, *(comm.ins if comm else []))
    return outs[0], outs[1], outs[2:]


def _branch_a_fwd(z, lng, lnb, t_mix, bias_full):
    S = z.shape[0]
    D = lng.shape[1]
    GD = D // GROUPS
    ts = _tile(S, 256)

    def body(zu_ref, zv_ref, zg_ref, lng_ref, lnb_ref, t_ref, bias_ref, a_ref, v_scr):
        def chunk(ci, carry):
            c0 = pl.multiple_of(ci * CHUNK, CHUNK)

            def slab(si, c):
                r = pl.multiple_of(si * 16, 16)
                vg = _gelu(zv_ref[pl.ds(pl.multiple_of(c0 + r, 16), 16), :])
                xc = vg - _mean(vg)
                y = xc * lax.rsqrt(_mean(xc * xc) + LN_EPS)
                v_scr[pl.ds(r, 16), :] = (y * lng_ref[...] + lnb_ref[...]).astype(BF16)
                return c
            _grouped_loop(CHUNK // 16, 4, slab, 0)
            for g in range(GROUPS):
                cols = slice(g * GD, (g + 1) * GD)
                sv = _dot(t_ref[g], v_scr[:, cols]) + bias_ref[:, cols]
                u = _gelu(zu_ref[pl.ds(c0, CHUNK), cols])
                a_ref[pl.ds(c0, CHUNK), cols] = ((u * sv) * _silu(zg_ref[pl.ds(c0, CHUNK), cols])).astype(BF16)
            return carry
        lax.fori_loop(0, ts // CHUNK, chunk, 0)

    return pl.pallas_call(
        body, name="branch_a_fwd", grid=(S // ts,),
        in_specs=[_zcol(ts, D, 0), _zcol(ts, D, 1), _zcol(ts, D, 2), _row1(D), _row1(D),
                  pl.BlockSpec((GROUPS, CHUNK, CHUNK), lambda i: (0, 0, 0)),
                  pl.BlockSpec((CHUNK, D), lambda i: (0, 0))],
        out_specs=_tok(ts, D),
        out_shape=jax.ShapeDtypeStruct((S, D), BF16),
        scratch_shapes=[pltpu.VMEM((CHUNK, D), BF16)],
        compiler_params=_params("parallel"),
    )(z, z, z, lng, lnb, t_mix, bias_full)


SH = HALO - CONV_K + 1


def _fill_shifts(sh_scr, src_scr, cols, ts):
    for j in range(1, 8):
        for g in range(ts // 32):
            sh_scr[j - 1, pl.ds(g * 32, 32), :] = src_scr[pl.ds(g * 32 + j, 32), cols]
        sh_scr[j - 1, pl.ds(ts, HALO - 8), :] = src_scr[pl.ds(ts + j, HALO - 8), cols]


def _shifted(sh_scr, src_scr, cols, row0, n):
    q, j = divmod(row0, 8)
    if j == 0:
        return src_scr[pl.ds(row0, n), cols]
    return sh_scr[j - 1, pl.ds(8 * q, n), :]


def _conv_fwd(z, w_all, cb, l):
    S = z.shape[0]
    D = cb.shape[1]
    ts = _tile(S, 256)
    CB = 256
    hb = ts // HALO

    def body(za_ref, zb_ref, zap_ref, zbp_ref, w_ref, b_ref, c0_ref, glu_scr, sh_scr):
        i = pl.program_id(0)
        prev = zap_ref[...] * _sigmoid(zbp_ref[...])
        glu_scr[0:HALO, :] = jnp.where(i == 0, 0.0, prev)

        def fill(si, c):
            r = _rows(si, 32)
            glu_scr[pl.ds(pl.multiple_of(HALO + si * 32, 32), 32), :] = za_ref[r, :] * _sigmoid(zb_ref[r, :])
            return c
        _grouped_loop(ts // 32, 2, fill, 0)

        def colblock(ci, c):
            cols = _rows(ci, CB)
            _fill_shifts(sh_scr, glu_scr, cols, ts)
            for s in range(ts // 32):
                acc = jnp.zeros((32, CB), F32)
                for k in range(CONV_K):
                    acc = acc + _shifted(sh_scr, glu_scr, cols, s * 32 + k + SH, 32) * w_ref[k:k + 1, cols]
                c0_ref[pl.ds(s * 32, 32), cols] = acc + b_ref[:, cols]
            return c
        lax.fori_loop(0, D // CB, colblock, 0)

    prev_map = lambda n: (lambda i: (jnp.maximum(i * hb - 1, 0), n))
    return pl.pallas_call(
        body, name="conv_fwd", grid=(S // ts,),
        in_specs=[_zcol(ts, D, 3), _zcol(ts, D, 4),
                  pl.BlockSpec((HALO, D), prev_map(3)), pl.BlockSpec((HALO, D), prev_map(4)),
                  pl.BlockSpec((None, CONV_K, D), lambda i: (l, 0, 0)), _row1(D)],
        out_specs=_tok(ts, D),
        out_shape=jax.ShapeDtypeStruct((S, D), F32),
        scratch_shapes=[pltpu.VMEM((ts + HALO, D), F32), pltpu.VMEM((7, ts + HALO, CB), F32)],
        compiler_params=_params("parallel"),
    )(z, z, z, z, w_all, cb)


def _branch_b_fwd(c0, z, g, b):
    S, D = c0.shape
    ts = _tile(S, 512)

    def body(c0_ref, zg_ref, g_ref, b_ref, o_ref):
        def slab(si, c):
            r = _rows(si, 16)
            v = c0_ref[r, :]
            xc = v - _mean(v)
            c1 = (xc * lax.rsqrt(_mean(xc * xc) + LN_EPS)) * g_ref[...] + b_ref[...]
            o_ref[r, :] = (_silu(c1) * _silu(zg_ref[r, :])).astype(BF16)
            return c
        _grouped_loop(ts // 16, 4, slab, 0)

    return pl.pallas_call(
        body, name="branch_b_fwd", grid=(S // ts,),
        in_specs=[_tok(ts, D), _zcol(ts, D, 5), _row1(D), _row1(D)],
        out_specs=_tok(ts, D),
        out_shape=jax.ShapeDtypeStruct((S, D), BF16),
        compiler_params=_params("parallel"),
    )(c0, z, g, b)


def _kv_fwd(mem, mg, wkv_all, l):
    M, D = mem.shape

    def body(mem_ref, g_ref, w_ref, mn_ref, kv_ref, kvt_ref):
        m = mem_ref[...]
        mn = ((m * lax.rsqrt(_mean(m * m) + RMS_EPS)) * g_ref[...]).astype(BF16)
        mn_ref[...] = mn
        kv = _dot(mn, w_ref[...])
        kv_ref[...] = kv.astype(BF16)
        kvt_ref[...] = kv.T.astype(BF16)

    return pl.pallas_call(
        body, name="kv_fwd", grid=(1,),
        in_specs=[pl.BlockSpec((M, D), lambda i: (0, 0)), _row1(D),
                  pl.BlockSpec((None, D, 2 * D), lambda i: (l, 0, 0))],
        out_specs=[pl.BlockSpec((M, D), lambda i: (0, 0)), pl.BlockSpec((M, 2 * D), lambda i: (0, 0)),
                   pl.BlockSpec((2 * D, M), lambda i: (0, 0))],
        out_shape=[jax.ShapeDtypeStruct((M, D), BF16), jax.ShapeDtypeStruct((M, 2 * D), BF16),
                   jax.ShapeDtypeStruct((2 * D, M), BF16)],
        compiler_params=_params("arbitrary"),
    )(mem, mg, wkv_all)


def _softmax_rows(s):
    e = jnp.exp(s - jnp.max(s, axis=-1, keepdims=True))
    return e / jnp.sum(e, axis=-1, keepdims=True)


def _branch_c_fwd(z, kv, kvt):
    S = z.shape[0]
    M, D2 = kv.shape
    D = D2 // 2
    HD = D // HEADS
    scale = 1.0 / math.sqrt(HD)
    ts = _tile(S, 256)

    def body(q_ref, zg_ref, kt_ref, v_ref, o_ref):
        for h in range(HEADS):
            cols = slice(h * HD, (h + 1) * HD)
            p = _softmax_rows(_dot(q_ref[:, cols].astype(BF16), kt_ref[cols, :]) * scale)
            att = _dot(p.astype(BF16), v_ref[:, cols])
            o_ref[:, cols] = (att * _silu(zg_ref[:, cols])).astype(BF16)

    return pl.pallas_call(
        body, name="branch_c_fwd", grid=(S // ts,),
        in_specs=[_zcol(ts, D, 6), _zcol(ts, D, 7),
                  pl.BlockSpec((D, M), lambda i: (0, 0)), pl.BlockSpec((M, D), lambda i: (0, 1))],
        out_specs=_tok(ts, D),
        out_shape=jax.ShapeDtypeStruct((S, D), BF16),
        compiler_params=_params("parallel"),
    )(z, z, kvt, kv)


def _merge_fwd(bra, brb, brc, z, x, wb_all, wo_all, l):
    S, D = x.shape
    ts = _tile(S, 256)
    CB = 256

    def body(a_ref, b_ref, c_ref, zm0_ref, zm1_ref, zm2_ref, x_ref, wb_ref, wo_ref, m_ref, xo_ref):
        brs = (a_ref, b_ref, c_ref)
        zms = (zm0_ref, zm1_ref, zm2_ref)
        for cb in range(D // CB):
            cols = slice(cb * CB, (cb + 1) * CB)
            acc = None
            for n in range(N_BRANCHES):
                t = _sigmoid(zms[n][:, cols]) * _dot(brs[n][...], wb_ref[n, :, cols])
                acc = t if acc is None else acc + t
            m_ref[:, cols] = acc.astype(BF16)
        for cb in range(D // CB):
            cols = slice(cb * CB, (cb + 1) * CB)
            xo_ref[:, cols] = x_ref[:, cols] + _dot(m_ref[...], wo_ref[:, cols])

    return pl.pallas_call(
        body, name="merge_fwd", grid=(S // ts,),
        in_specs=[_tok(ts, D), _tok(ts, D), _tok(ts, D), _zcol(ts, D, 8), _zcol(ts, D, 9), _zcol(ts, D, 10),
                  _tok(ts, D),
                  pl.BlockSpec((None, N_BRANCHES, D, D), lambda i: (l, 0, 0, 0)),
                  pl.BlockSpec((None, D, D), lambda i: (l, 0, 0))],
        out_specs=[_tok(ts, D), _tok(ts, D)],
        out_shape=[jax.ShapeDtypeStruct((S, D), BF16), jax.ShapeDtypeStruct((S, D), F32)],
        compiler_params=_params("parallel"),
    )(bra, brb, brc, z, z, z, x, wb_all, wo_all)


def _loss_bwd(x, target, fg):
    S, D = x.shape
    ts = _tile(S, 512)

    def body(x_ref, t_ref, g_ref, dx_ref, ls_ref, dg_ref):
        @pl.when(pl.program_id(0) == 0)
        def _():
            ls_ref[...] = jnp.zeros_like(ls_ref)
            dg_ref[...] = jnp.zeros_like(dg_ref)

        def slab(si, c):
            ls, dg = c
            r = _rows(si, 16)
            xs = x_ref[r, :]
            rs = lax.rsqrt(_mean(xs * xs) + RMS_EPS)
            n = xs * rs
            e = n * g_ref[...] - t_ref[r, :]
            dy = e * (1.0 / D)
            dn = dy * g_ref[...]
            dx_ref[r, :] = rs * (dn - n * _mean(dn * n))
            return ls + _colsum(e * e), dg + _colsum(dy * n)
        zero = jnp.zeros((1, D), F32)
        ls, dg = _grouped_loop(ts // 16, 4, slab, (zero, zero))
        ls_ref[...] += ls
        dg_ref[...] += dg

    return pl.pallas_call(
        body, name="loss_bwd", grid=(S // ts,),
        in_specs=[_tok(ts, D), _tok(ts, D), _row1(D)],
        out_specs=[_tok(ts, D), _row1(D), _row1(D)],
        out_shape=[jax.ShapeDtypeStruct((S, D), F32), jax.ShapeDtypeStruct((1, D), F32),
                   jax.ShapeDtypeStruct((1, D), F32)],
        compiler_params=_params("arbitrary"),
    )(x, target, fg)


def _merge_bwd(dxp, bra, brb, brc, z, wb_all, wo_all, l, comm=None):
    S, D = dxp.shape
    ts = _tile(S, 256)
    CB = 256

    def body(d_ref, a_ref, b_ref, c_ref, zm0_ref, zm1_ref, zm2_ref, wb_ref, wo_ref,
             dzm_ref, dproj_ref, dbr_ref, dxb_scr):
        brs = (a_ref, b_ref, c_ref)
        zms = (zm0_ref, zm1_ref, zm2_ref)
        dxb_scr[...] = d_ref[...].astype(BF16)
        for cb in range(D // CB):
            cols = slice(cb * CB, (cb + 1) * CB)
            dm = _dot_tb(dxb_scr[...], wo_ref[cols, :])
            for n in range(N_BRANCHES):
                proj = _dot(brs[n][...], wb_ref[n, :, cols])
                g = _sigmoid(zms[n][:, cols])
                dzm_ref[:, n * D + cb * CB:n * D + (cb + 1) * CB] = ((dm * proj) * (g * (1.0 - g))).astype(BF16)
                dproj_ref[n, :, cols] = (dm * g).astype(BF16)
        for n in range(N_BRANCHES):
            for cb in range(D // CB):
                cols = slice(cb * CB, (cb + 1) * CB)
                dbr_ref[n, :, cols] = _dot_tb(dproj_ref[n], wb_ref[n, cols, :])

    tok3 = lambda: pl.BlockSpec((N_BRANCHES, ts, D), lambda i: (0, i, 0))
    grid = (S // ts,)
    body, c_in, c_out, c_shapes, c_scr, sem = _host(comm, 9, 3, 1, grid, body)
    outs = pl.pallas_call(
        body, name="merge_bwd", grid=grid,
        in_specs=[_tok(ts, D), _tok(ts, D), _tok(ts, D), _tok(ts, D),
                  _zcol(ts, D, 8), _zcol(ts, D, 9), _zcol(ts, D, 10),
                  pl.BlockSpec((None, N_BRANCHES, D, D), lambda i: (l, 0, 0, 0)),
                  pl.BlockSpec((None, D, D), lambda i: (l, 0, 0))] + c_in,
        out_specs=[_tok(ts, 3 * D), tok3(), tok3()] + c_out,
        out_shape=[jax.ShapeDtypeStruct((S, 3 * D), BF16), jax.ShapeDtypeStruct((N_BRANCHES, S, D), BF16),
                   jax.ShapeDtypeStruct((N_BRANCHES, S, D), F32)] + c_shapes,
        scratch_shapes=[pltpu.VMEM((ts, D), BF16)] + c_scr,
        compiler_params=_params(*(sem or ("parallel",))),
    )(dxp, bra, brb, brc, z, z, z, wb_all, wo_all, *(comm.ins if comm else []))
    return outs[0], outs[1], outs[2], outs[3:]


def _branch_a_bwd(z, dbr, lng, lnb, t_mix, t_mix_t, bias_full, mask):
    S = z.shape[0]
    D = lng.shape[1]
    GD = D // GROUPS
    ts = _tile(S, 256)
    nt = S // ts

    def body(zu_ref, zv_ref, zg_ref, d_ref, lng_ref, lnb_ref, t_ref, tt_ref, bias_ref, mask_ref,
             dz_ref, dws_ref, dbs_ref, dlg_ref, dlb_ref,
             v_scr, vn_scr, rstd_scr, dv_scr, dsv_acc):
        i = pl.program_id(0)

        @pl.when(i == 0)
        def _():
            dws_ref[...] = jnp.zeros_like(dws_ref)
            dlg_ref[...] = jnp.zeros_like(dlg_ref)
            dlb_ref[...] = jnp.zeros_like(dlb_ref)
            dsv_acc[...] = jnp.zeros_like(dsv_acc)

        def chunk(ci, carry):
            c0 = pl.multiple_of(ci * CHUNK, CHUNK)

            def ln_fwd(si, c):
                r = pl.multiple_of(si * 16, 16)
                vg = _gelu(zv_ref[pl.ds(pl.multiple_of(c0 + r, 16), 16), :])
                xc = vg - _mean(vg)
                rstd = lax.rsqrt(_mean(xc * xc) + LN_EPS)
                vn = xc * rstd
                vn_scr[pl.ds(r, 16), :] = vn
                rstd_scr[pl.ds(r, 16), :] = rstd
                v_scr[pl.ds(r, 16), :] = (vn * lng_ref[...] + lnb_ref[...]).astype(BF16)
                return c
            _grouped_loop(CHUNK // 16, 4, ln_fwd, 0)

            for g in range(GROUPS):
                cols = slice(g * GD, (g + 1) * GD)
                rows = pl.ds(c0, CHUNK)
                vb = v_scr[:, cols]
                sv = _dot(t_ref[g], vb) + bias_ref[:, cols]
                u, du = _gelu_and_grad(zu_ref[rows, cols])
                sg, dsg = _silu_and_grad(zg_ref[rows, cols])
                d = d_ref[rows, cols]
                dsv = (d * u) * sg
                dz_ref[rows, g * GD:(g + 1) * GD] = (((d * sv) * sg) * du).astype(BF16)
                dz_ref[rows, 2 * D + g * GD:2 * D + (g + 1) * GD] = (((d * u) * sv) * dsg).astype(BF16)
                dsvb = dsv.astype(BF16)
                dws_ref[g] += _dot_tb(dsvb, vb)
                dv_scr[:, cols] = _dot(tt_ref[g], dsvb)
                dsv_acc[:, cols] += dsv

            def ln_bwd(si, c):
                dlg, dlb = c
                r = pl.multiple_of(si * 16, 16)
                rr = pl.ds(r, 16)
                zrows = pl.ds(pl.multiple_of(c0 + r, 16), 16)
                dv = dv_scr[rr, :]
                vn = vn_scr[rr, :]
                dvn = dv * lng_ref[...]
                dvg = rstd_scr[rr, :] * ((dvn - _mean(dvn)) - vn * _mean(dvn * vn))
                _, gg = _gelu_and_grad(zv_ref[zrows, :])
                dz_ref[zrows, D:2 * D] = (dvg * gg).astype(BF16)
                return dlg + _colsum(dv * vn), dlb + _colsum(dv)
            zero = jnp.zeros((1, D), F32)
            dlg, dlb = _grouped_loop(CHUNK // 16, 4, ln_bwd, (zero, zero))
            dlg_ref[...] += dlg
            dlb_ref[...] += dlb
            return carry
        lax.fori_loop(0, ts // CHUNK, chunk, 0)

        @pl.when(i == nt - 1)
        def _():
            for g in range(GROUPS):
                cols = slice(g * GD, (g + 1) * GD)
                dws_ref[g] = dws_ref[g] * mask_ref[...]
                dbs_ref[g:g + 1, :] = _colsum(dsv_acc[:, cols].T)

    c2 = lambda shape: pl.BlockSpec(shape, lambda i: (0, 0))
    c3 = lambda: pl.BlockSpec((GROUPS, CHUNK, CHUNK), lambda i: (0, 0, 0))
    return pl.pallas_call(
        body, name="branch_a_bwd", grid=(nt,),
        in_specs=[_zcol(ts, D, 0), _zcol(ts, D, 1), _zcol(ts, D, 2), _plane(ts, D, 0), _row1(D), _row1(D),
                  c3(), c3(), c2((CHUNK, D)), c2((CHUNK, CHUNK))],
        out_specs=[_tok(ts, 3 * D), c3(), c2((GROUPS, CHUNK)), _row1(D), _row1(D)],
        out_shape=[jax.ShapeDtypeStruct((S, 3 * D), BF16), jax.ShapeDtypeStruct((GROUPS, CHUNK, CHUNK), F32),
                   jax.ShapeDtypeStruct((GROUPS, CHUNK), F32), jax.ShapeDtypeStruct((1, D), F32),
                   jax.ShapeDtypeStruct((1, D), F32)],
        scratch_shapes=[pltpu.VMEM((CHUNK, D), BF16), pltpu.VMEM((CHUNK, D), F32), pltpu.VMEM((CHUNK, 1), F32),
                        pltpu.VMEM((CHUNK, D), F32), pltpu.VMEM((CHUNK, D), F32)],
        compiler_params=_params("arbitrary"),
    )(z, z, z, dbr, lng, lnb, t_mix, t_mix_t, bias_full, mask)


def _branch_b_bwd(c0, z, dbr, g, b):
    S, D = c0.shape
    ts = _tile(S, 512)

    def body(c0_ref, zg_ref, d_ref, g_ref, b_ref, dc0_ref, dz_ref, dg_ref, db_ref, dcb_ref):
        @pl.when(pl.program_id(0) == 0)
        def _():
            dg_ref[...] = jnp.zeros_like(dg_ref)
            db_ref[...] = jnp.zeros_like(db_ref)
            dcb_ref[...] = jnp.zeros_like(dcb_ref)

        def slab(si, c):
            dg, db, dcb = c
            r = _rows(si, 16)
            v = c0_ref[r, :]
            xc = v - _mean(v)
            rstd = lax.rsqrt(_mean(xc * xc) + LN_EPS)
            cn = xc * rstd
            c1 = cn * g_ref[...] + b_ref[...]
            c2, dc2_dc1 = _silu_and_grad(c1)
            sg, dsg = _silu_and_grad(zg_ref[r, :])
            d = d_ref[r, :]
            dz_ref[r, :] = ((d * c2) * dsg).astype(BF16)
            dc1 = (d * sg) * dc2_dc1
            dcn = dc1 * g_ref[...]
            dc0 = rstd * ((dcn - _mean(dcn)) - cn * _mean(dcn * cn))
            dc0_ref[r, :] = dc0
            return dg + _colsum(dc1 * cn), db + _colsum(dc1), dcb + _colsum(dc0)
        zero = jnp.zeros((1, D), F32)
        dg, db, dcb = _grouped_loop(ts // 16, 4, slab, (zero, zero, zero))
        dg_ref[...] += dg
        db_ref[...] += db
        dcb_ref[...] += dcb

    return pl.pallas_call(
        body, name="branch_b_bwd", grid=(S // ts,),
        in_specs=[_tok(ts, D), _zcol(ts, D, 5), _plane(ts, D, 1), _row1(D), _row1(D)],
        out_specs=[_tok(ts, D), _zcol(ts, D, 2), _row1(D), _row1(D), _row1(D)],
        out_shape=[jax.ShapeDtypeStruct((S, D), F32), jax.ShapeDtypeStruct((S, 3 * D), BF16),
                   jax.ShapeDtypeStruct((1, D), F32), jax.ShapeDtypeStruct((1, D), F32),
                   jax.ShapeDtypeStruct((1, D), F32)],
        compiler_params=_params("arbitrary"),
    )(c0, z, dbr, g, b)


def _conv_bwd(dc0, z, w_all, dz_b, l):
    S, D = dc0.shape
    ts = _tile(S, 256)
    nt = S // ts
    CB = 256
    hb = ts // HALO
    nh = S // HALO
    SH = HALO - CONV_K + 1

    def body(dc_ref, dcn_ref, za_ref, zb_ref, zap_ref, zbp_ref, w_ref, dzin_ref,
             dz_ref, dw_ref, glu_scr, dc_scr, dw_acc, gsh_scr, dsh_scr):
        del dzin_ref
        i = pl.program_id(0)

        @pl.when(i == 0)
        def _():
            dw_acc[...] = jnp.zeros_like(dw_acc)

        prev = zap_ref[...] * _sigmoid(zbp_ref[...])
        glu_scr[0:HALO, :] = jnp.where(i == 0, 0.0, prev)
        dc_scr[ts:ts + HALO, :] = jnp.where(i == nt - 1, 0.0, dcn_ref[...])

        def fill(si, c):
            r = _rows(si, 32)
            glu_scr[pl.ds(pl.multiple_of(HALO + si * 32, 32), 32), :] = za_ref[r, :] * _sigmoid(zb_ref[r, :])
            dc_scr[r, :] = dc_ref[r, :]
            return c
        _grouped_loop(ts // 32, 2, fill, 0)

        def colblock(ci, c):
            cols = _rows(ci, CB)
            cols_b = pl.ds(pl.multiple_of(D + ci * CB, CB), CB)
            _fill_shifts(gsh_scr, glu_scr, cols, ts)
            _fill_shifts(dsh_scr, dc_scr, cols, ts)
            for s in range(ts // 32):
                rows = pl.ds(s * 32, 32)
                dc = dc_scr[rows, cols]
                acc = jnp.zeros((32, CB), F32)
                for k in range(CONV_K):
                    prod = dc * _shifted(gsh_scr, glu_scr, cols, s * 32 + k + SH, 32)
                    dw_acc[k, :, cols] += (prod[0:8] + prod[8:16]) + (prod[16:24] + prod[24:32])
                    acc = acc + (_shifted(dsh_scr, dc_scr, cols, s * 32 + k, 32)
                                 * w_ref[CONV_K - 1 - k:CONV_K - k, cols])
                sg = _sigmoid(zb_ref[rows, cols])
                dz_ref[rows, cols] = (acc * sg).astype(BF16)
                dz_ref[rows, cols_b] = ((acc * za_ref[rows, cols]) * (sg * (1.0 - sg))).astype(BF16)
            return c
        lax.fori_loop(0, D // CB, colblock, 0)

        @pl.when(i == nt - 1)
        def _():
            for k in range(CONV_K):
                dw_ref[k:k + 1, :] = _colsum(dw_acc[k])

    prev_map = lambda n: (lambda i: (jnp.maximum(i * hb - 1, 0), n))
    return pl.pallas_call(
        body, name="conv_bwd", grid=(nt,),
        in_specs=[_tok(ts, D), pl.BlockSpec((HALO, D), lambda i: (jnp.minimum((i + 1) * hb, nh - 1), 0)),
                  _zcol(ts, D, 3), _zcol(ts, D, 4),
                  pl.BlockSpec((HALO, D), prev_map(3)), pl.BlockSpec((HALO, D), prev_map(4)),
                  pl.BlockSpec((None, CONV_K, D), lambda i: (l, 0, 0)), ANY],
        out_specs=[_tok(ts, 2 * D), pl.BlockSpec((CONV_K, D), lambda i: (0, 0))],
        out_shape=[jax.ShapeDtypeStruct((S, 3 * D), BF16), jax.ShapeDtypeStruct((CONV_K, D), F32)],
        scratch_shapes=[pltpu.VMEM((ts + HALO, D), F32), pltpu.VMEM((ts + HALO, D), F32),
                        pltpu.VMEM((CONV_K, 8, D), F32),
                        pltpu.VMEM((7, ts + HALO, CB), F32), pltpu.VMEM((7, ts + HALO, CB), F32)],
        input_output_aliases={7: 0},
        compiler_params=_params("arbitrary"),
    )(dc0, dc0, z, z, z, z, w_all, dz_b)


def _branch_c_bwd(z, dbr, kv, kvt):
    S = z.shape[0]
    M, D2 = kv.shape
    D = D2 // 2
    HD = D // HEADS
    scale = 1.0 / math.sqrt(HD)
    ts = _tile(S, 256)

    def body(q_ref, zg_ref, d_ref, kt_ref, vt_ref, k_ref, v_ref, dz_ref, dk_ref, dv_ref):
        @pl.when(pl.program_id(0) == 0)
        def _():
            dk_ref[...] = jnp.zeros_like(dk_ref)
            dv_ref[...] = jnp.zeros_like(dv_ref)

        for h in range(HEADS):
            cols = slice(h * HD, (h + 1) * HD)
            qb = q_ref[:, cols].astype(BF16)
            p = _softmax_rows(_dot(qb, kt_ref[cols, :]) * scale)
            pb = p.astype(BF16)
            att = _dot(pb, v_ref[:, cols])
            sg, dsg = _silu_and_grad(zg_ref[:, cols])
            d = d_ref[:, cols]
            dz_ref[:, D + h * HD:D + (h + 1) * HD] = ((d * att) * dsg).astype(BF16)
            datt = (d * sg).astype(BF16)
            dp = _dot(datt, vt_ref[cols, :])
            dv_ref[:, cols] += _dot_ta(pb, datt)
            ds = ((p * (dp - jnp.sum(dp * p, axis=-1, keepdims=True))) * scale).astype(BF16)
            dz_ref[:, cols] = _dot(ds, k_ref[:, cols]).astype(BF16)
            dk_ref[:, cols] += _dot_ta(ds, qb)

    return pl.pallas_call(
        body, name="branch_c_bwd", grid=(S // ts,),
        in_specs=[_zcol(ts, D, 6), _zcol(ts, D, 7), _plane(ts, D, 2),
                  pl.BlockSpec((D, M), lambda i: (0, 0)), pl.BlockSpec((D, M), lambda i: (1, 0)),
                  pl.BlockSpec((M, D), lambda i: (0, 0)), pl.BlockSpec((M, D), lambda i: (0, 1))],
        out_specs=[_tok(ts, 2 * D), pl.BlockSpec((M, D), lambda i: (0, 0)), pl.BlockSpec((M, D), lambda i: (0, 0))],
        out_shape=[jax.ShapeDtypeStruct((S, 2 * D), BF16), jax.ShapeDtypeStruct((M, D), F32),
                   jax.ShapeDtypeStruct((M, D), F32)],
        compiler_params=_params("arbitrary"),
    )(z, z, dbr, kvt, kvt, kv, kv)


def _kv_bwd(dk, dv, mn, mem, wkv_all, l, prev):
    M, D = mem.shape

    def body(*refs):
        dk_ref, dv_ref, mn_ref, mem_ref, w_ref = refs[:5]
        dw_ref, dg_ref = refs[-2:]
        dkb = dk_ref[...].astype(BF16)
        dvb = dv_ref[...].astype(BF16)
        dw_ref[:, 0:D] = _dot_ta(mn_ref[...], dkb)
        dw_ref[:, D:2 * D] = _dot_ta(mn_ref[...], dvb)
        dmn = _dot_tb(dkb, w_ref[:, 0:D]) + _dot_tb(dvb, w_ref[:, D:2 * D])
        m = mem_ref[...]
        dg_ref[...] = _colsum(dmn * (m * lax.rsqrt(_mean(m * m) + RMS_EPS)))

    full = lambda shape: pl.BlockSpec(shape, lambda i: (0, 0))
    in_specs = [full((M, D)), full((M, D)), full((M, D)), full((M, D)),
                pl.BlockSpec((None, D, 2 * D), lambda i: (l, 0, 0))]
    args = [dk, dv, mn, mem, wkv_all]
    aliases = {}
    if prev is not None:
        in_specs.append(ANY)
        args.append(prev)
        aliases = {5: 0}
    return pl.pallas_call(
        body, name="kv_bwd", grid=(1,),
        in_specs=in_specs,
        out_specs=[pl.BlockSpec((None, D, 2 * D), lambda i: (l, 0, 0)), _row1(D)],
        out_shape=[jax.ShapeDtypeStruct((wkv_all.shape[0], D, 2 * D), F32), jax.ShapeDtypeStruct((1, D), F32)],
        input_output_aliases=aliases,
        compiler_params=_params("arbitrary"),
    )(*args)


def _in_proj_bwd(pieces, w_all, x, g, dxp, l, comm=None):
    S, D = x.shape
    tm = _tile(S, 512)
    n_p = len(pieces)

    def body(*refs):
        dz_refs = refs[:n_p]
        w_ref, x_ref, g_ref, dxp_ref, dx_ref, dg_ref, acc = refs[n_p:]
        i = pl.program_id(0)
        k = pl.program_id(1)

        @pl.when(k == 0)
        def _():
            acc[...] = jnp.zeros_like(acc)

        @pl.when((i == 0) & (k == 0))
        def _():
            dg_ref[...] = jnp.zeros_like(dg_ref)

        for (_, lo, n), r in zip(pieces, dz_refs):
            @pl.when((k >= lo) & (k < lo + n))
            def _(r=r):
                acc[...] += _dot(r[...], w_ref[...])

        @pl.when(k == N_PLANES - 1)
        def _():
            def slab(si, c):
                rr = _rows(si, 16)
                xs = x_ref[rr, :]
                rs = lax.rsqrt(_mean(xs * xs) + RMS_EPS)
                n = xs * rs
                dh = acc[rr, :]
                dn = dh * g_ref[...]
                dx_ref[rr, :] = rs * (dn - n * _mean(dn * n)) + dxp_ref[rr, :]
                return c + _colsum(dh * n)
            dg_ref[...] += _grouped_loop(tm // 16, 4, slab, jnp.zeros((1, D), F32))

    def piece_spec(lo, n):
        return pl.BlockSpec((tm, D), lambda i, k: (i, jnp.clip(k - lo, 0, n - 1)))

    tokk = lambda: pl.BlockSpec((tm, D), lambda i, k: (i, 0))
    rowk = lambda: pl.BlockSpec((1, D), lambda i, k: (0, 0))
    grid = (S // tm, N_PLANES)
    body, c_in, c_out, c_shapes, c_scr, _ = _host(comm, n_p + 4, 2, 1, grid, body)
    outs = pl.pallas_call(
        body, name="in_proj_bwd", grid=grid,
        in_specs=[piece_spec(lo, n) for _, lo, n in pieces]
        + [pl.BlockSpec((None, D, D), lambda i, k: (l, k, 0)), tokk(), rowk(), tokk()] + c_in,
        out_specs=[tokk(), rowk()] + c_out,
        out_shape=[jax.ShapeDtypeStruct((S, D), F32), jax.ShapeDtypeStruct((1, D), F32)] + c_shapes,
        scratch_shapes=[pltpu.VMEM((tm, D), F32)] + c_scr,
        compiler_params=_params("arbitrary", "arbitrary"),
    )(*[p for p, _, _ in pieces], w_all, x, g, dxp, *(comm.ins if comm else []))
    return outs[0], outs[1], outs[2:]


def _wgrad(a, b, out_shape, plane, col_off, prev, a_plane=None, b_plane=None):
    S = a.shape[-2]
    K1 = a.shape[-1]
    nb_cols = b.shape[-1]
    ts = _tile(S, 1024)
    tn = _tile(nb_cols, 1024)

    def body(*refs):
        a_ref, b_ref = refs[:2]
        o_ref = refs[-1]
        k = pl.program_id(1)
        prod = _dot_ta(a_ref[...].astype(BF16), b_ref[...].astype(BF16))

        @pl.when(k == 0)
        def _():
            o_ref[...] = prod

        @pl.when(k > 0)
        def _():
            o_ref[...] += prod

    if a_plane is None:
        a_spec = pl.BlockSpec((ts, K1), lambda j, k: (k, 0))
    else:
        a_spec = pl.BlockSpec((None, ts, K1), lambda j, k: (a_plane, k, 0))
    if b_plane is None:
        b_spec = pl.BlockSpec((ts, tn), lambda j, k: (k, j))
    else:
        b_spec = pl.BlockSpec((None, ts, tn), lambda j, k: (b_plane, k, j))
    in_specs, args, aliases = [a_spec, b_spec], [a, b], {}
    if prev is not None:
        in_specs.append(ANY)
        args.append(prev)
        aliases = {2: 0}
    return pl.pallas_call(
        body, name="wgrad", grid=(nb_cols // tn, S // ts),
        in_specs=in_specs,
        out_specs=pl.BlockSpec((None, K1, tn), lambda j, k: (plane, 0, col_off + j)),
        out_shape=jax.ShapeDtypeStruct(out_shape, F32),
        input_output_aliases=aliases,
        compiler_params=_params("parallel", "arbitrary"),
    )(*args)


def _position():
    x, y, c = lax.axis_index("x"), lax.axis_index("y"), lax.axis_index("c")
    chips = [(1 - x, y), (x, 1 - y), (1 - x, 1 - y)]
    return x, y, c, chips


def _sub(ref, axis, start, size):
    idx = [slice(None)] * len(ref.shape)
    idx[axis] = pl.ds(start, size)
    return ref.at[tuple(idx)]


def _dma(src, dst, ssem, rsem, to):
    return pltpu.make_async_remote_copy(src_ref=src, dst_ref=dst, send_sem=ssem, recv_sem=rsem,
                                        device_id=to, device_id_type=MESH)


def _dma_sems(*counts):
    return [pltpu.SemaphoreType.DMA((n,)) for n in counts]


def _run_alone(comm, name):
    ci, co = len(comm.ins), len(comm.out_shapes)

    def body(*refs):
        for _, fn in comm.phases:
            fn(refs[:ci], refs[ci:ci + co], refs[ci + co:])

    return pl.pallas_call(
        body, name=name, in_specs=[ANY] * ci, out_specs=[ANY] * co, out_shape=list(comm.out_shapes),
        scratch_shapes=list(comm.sem_shapes),
    )(*comm.ins)


def _gather_comm(shards, layers, axes):
    n = len(shards)
    out_shapes = []
    for s, (ba, _) in zip(shards, axes):
        shp = list(s.shape[1:])
        shp[ba] *= N_CHIPS
        out_shapes.append(jax.ShapeDtypeStruct(tuple(shp), s.dtype))

    def half(ref, t, h):
        ha = axes[t][1]
        if ha is None:
            return ref
        hs = shards[t].shape[1 + ha] // 2
        return _sub(ref, ha, h * hs, hs)

    def block(outs, t, blk):
        ba = axes[t][0]
        bs = shards[t].shape[1 + ba]
        return _sub(outs[t], ba, blk * bs, bs)

    def start(ins, outs, sems):
        s_own, r_own, s_ici, r_ici, _, _ = sems
        x, y, c, chips = _position()
        j = 2 * x + y
        for t in range(n):
            _dma(ins[t].at[layers[t]], block(outs, t, j), s_own.at[t], r_own.at[t], (x, y, 1 - c)).start()
            for k, (px, py) in enumerate(chips):
                _dma(half(ins[t].at[layers[t]], t, c), half(block(outs, t, j), t, c),
                     s_ici.at[3 * t + k], r_ici.at[3 * t + k], (px, py, c)).start()

    def forward(ins, outs, sems):
        _, _, s_ici, r_ici, s_fwd, r_fwd = sems
        x, y, c, chips = _position()
        for t in range(n):
            if axes[t][1] is None:
                continue
            for k, (px, py) in enumerate(chips):
                got = half(block(outs, t, 2 * px + py), t, c)
                _dma(got, got, s_ici.at[3 * t + k], r_ici.at[3 * t + k], (px, py, c)).wait_recv()
                _dma(got, got, s_fwd.at[3 * t + k], r_fwd.at[3 * t + k], (x, y, 1 - c)).start()

    def finish(ins, outs, sems):
        s_own, r_own, s_ici, r_ici, s_fwd, r_fwd = sems
        x, y, c, chips = _position()
        j = 2 * x + y
        sib = (x, y, 1 - c)
        for t in range(n):
            for k, (px, py) in enumerate(chips):
                i = 3 * t + k
                sent = half(block(outs, t, j), t, c)
                _dma(sent, sent, s_ici.at[i], r_ici.at[i], (px, py, c)).wait_send()
                theirs = block(outs, t, 2 * px + py)
                if axes[t][1] is None:
                    _dma(theirs, theirs, s_ici.at[i], r_ici.at[i], (px, py, c)).wait_recv()
                else:
                    got, other = half(theirs, t, c), half(theirs, t, 1 - c)
                    _dma(got, got, s_fwd.at[i], r_fwd.at[i], sib).wait_send()
                    _dma(other, other, s_fwd.at[i], r_fwd.at[i], sib).wait_recv()
            _dma(ins[t].at[layers[t]], block(outs, t, j), s_own.at[t], r_own.at[t], sib).wait()

    return _Hosted(list(shards), out_shapes, _dma_sems(n, n, 3 * n, 3 * n, 3 * n, 3 * n),
                   [("first", start), ("late", forward), ("last", finish)])


def _halved(shape, axis):
    shp = list(shape)
    shp[axis] //= 2
    return tuple(shp)


def _pair_comm(grads, axes):
    n = len(grads)

    def copies(ins, outs, sems):
        x, y, c, _ = _position()
        cps = []
        for t in range(n):
            ha = axes[t][1]
            hs = grads[t].shape[ha] // 2
            cps.append(_dma(_sub(ins[t], ha, (1 - c) * hs, hs), outs[t], sems[0].at[t], sems[1].at[t], (x, y, 1 - c)))
        return cps

    def start(ins, outs, sems):
        for cp in copies(ins, outs, sems):
            cp.start()

    def finish(ins, outs, sems):
        for cp in copies(ins, outs, sems):
            cp.wait()

    return _Hosted(list(grads), [jax.ShapeDtypeStruct(_halved(g.shape, ax[1]), g.dtype) for g, ax in zip(grads, axes)],
                   _dma_sems(n, n), [("first", start), ("last", finish)])


def _block_comm(sums, axes):
    n = len(sums)
    out_shapes = []
    for s, (ba, _) in zip(sums, axes):
        shp = list(s.shape)
        shp[ba] //= N_CHIPS
        out_shapes.append(jax.ShapeDtypeStruct((3,) + tuple(shp), s.dtype))

    def copies(ins, outs, sems):
        x, y, c, chips = _position()
        cps = []
        for t in range(n):
            ba = axes[t][0]
            bs = sums[t].shape[ba] // N_CHIPS
            for k, (px, py) in enumerate(chips):
                cps.append(_dma(_sub(ins[t], ba, (2 * px + py) * bs, bs), outs[t].at[k],
                                sems[0].at[3 * t + k], sems[1].at[3 * t + k], (px, py, c)))
        return cps

    def start(ins, outs, sems):
        for cp in copies(ins, outs, sems):
            cp.start()

    def finish(ins, outs, sems):
        for cp in copies(ins, outs, sems):
            cp.wait()

    return _Hosted(list(sums), out_shapes, _dma_sems(3 * n, 3 * n), [("first", start), ("last", finish)])


def _sibling_share(shards, axes):
    n = len(shards)

    def body(*refs):
        outs = refs[n:2 * n]
        ssem, rsem = refs[2 * n:]
        x, y, c, _ = _position()
        cps = []
        for t in range(n):
            ha = axes[t][1]
            hs = shards[t].shape[ha] // 2
            mine = _sub(outs[t], ha, c * hs, hs)
            cps.append(_dma(mine, mine, ssem.at[t], rsem.at[t], (x, y, 1 - c)))
        for cp in cps:
            cp.start()
        for cp in cps:
            cp.wait()

    return pl.pallas_call(
        body, name="sibling_share",
        in_specs=[ANY] * n, out_specs=[ANY] * n,
        out_shape=[jax.ShapeDtypeStruct(s.shape, s.dtype) for s in shards],
        input_output_aliases={t: t for t in range(n)},
        scratch_shapes=_dma_sems(n, n),
    )(*shards)


def _small_allreduce(p):
    R, C = p.shape

    def body(p_ref, o_ref, sib_buf, chip_sums, ssem, rsem):
        x, y, c, chips = _position()
        j = 2 * x + y
        sib = pltpu.make_async_remote_copy(
            src_ref=p_ref, dst_ref=sib_buf, send_sem=ssem.at[0], recv_sem=rsem.at[0],
            device_id=(x, y, 1 - c), device_id_type=MESH)
        sib.start()
        sib.wait()
        chip_sums[j] = p_ref[...] + sib_buf[...]
        cps = [pltpu.make_async_remote_copy(
            src_ref=chip_sums.at[j], dst_ref=chip_sums.at[j], send_sem=ssem.at[1 + k], recv_sem=rsem.at[1 + k],
            device_id=(px, py, c), device_id_type=MESH) for k, (px, py) in enumerate(chips)]
        for cp in cps:
            cp.start()
        for k, (px, py) in enumerate(chips):
            pltpu.make_async_remote_copy(
                src_ref=chip_sums.at[j], dst_ref=chip_sums.at[2 * px + py],
                send_sem=ssem.at[1 + k], recv_sem=rsem.at[1 + k],
                device_id=(px, py, c), device_id_type=MESH).wait()
        o_ref[...] = ((chip_sums[0] + chip_sums[1]) + chip_sums[2]) + chip_sums[3]

    vm = pl.BlockSpec(memory_space=pltpu.VMEM)
    return pl.pallas_call(
        body, name="small_allreduce",
        in_specs=[vm], out_specs=vm, out_shape=jax.ShapeDtypeStruct((R, C), F32),
        scratch_shapes=[pltpu.VMEM((R, C), F32), pltpu.VMEM((N_CHIPS, R, C), F32),
                        pltpu.SemaphoreType.DMA((4,)), pltpu.SemaphoreType.DMA((4,))],
        compiler_params=pltpu.CompilerParams(vmem_limit_bytes=VMEM_LIMIT_BYTES),
    )(p)


def _add_half(g, recv, half_axis, pos):
    B, R, C = recv.shape
    tr = _tile(R, 256)
    tc = _tile(C, 2816)
    nr, nc = R // tr, C // tc
    if half_axis == 1:
        g_map = lambda b, i, j, pos_ref: (b, pos_ref[0] * nr + i, j)
    else:
        g_map = lambda b, i, j, pos_ref: (b, i, pos_ref[0] * nc + j)

    def body(pos_ref, g_ref, r_ref, o_ref, ob_ref):
        del pos_ref
        s = g_ref[...] + r_ref[...]
        o_ref[...] = s
        ob_ref[...] = s.astype(BF16)

    spec = lambda: pl.BlockSpec((None, tr, tc), lambda b, i, j, pos_ref: (b, i, j))
    return pl.pallas_call(
        body, name="add_half",
        grid_spec=pltpu.PrefetchScalarGridSpec(
            num_scalar_prefetch=1, grid=(B, nr, nc),
            in_specs=[pl.BlockSpec((None, tr, tc), g_map), spec()],
            out_specs=[spec(), spec()]),
        out_shape=[jax.ShapeDtypeStruct((B, R, C), F32), jax.ShapeDtypeStruct((B, R, C), BF16)],
        compiler_params=_params("parallel", "parallel", "parallel"),
    )(pos, g, recv)


def _sum_half(own, recv, axes, pos):
    _, B, rs, cs = recv.shape
    ba, ha = axes
    tr = _tile(rs, 128)
    nr = rs // tr

    def body(pos_ref, o_ref, r0_ref, r1_ref, r2_ref, out_ref):
        del pos_ref
        out_ref[...] = (((o_ref[...] + r0_ref[...].astype(F32)) + r1_ref[...].astype(F32))
                        + r2_ref[...].astype(F32))

    if ba == 2:
        own_spec = pl.BlockSpec((None, tr, cs), lambda b, i, pos_ref: (b, i, pos_ref[1]))
    else:
        own_spec = pl.BlockSpec((None, tr, cs), lambda b, i, pos_ref: (b, pos_ref[1] * nr + i, 0))
    if ha == 1:
        out_shape, out_map = (B, 2 * rs, cs), (lambda b, i, pos_ref: (b, pos_ref[0] * nr + i, 0))
    else:
        out_shape, out_map = (B, rs, 2 * cs), (lambda b, i, pos_ref: (b, i, pos_ref[0]))
    rspec = lambda k: pl.BlockSpec((None, None, tr, cs), lambda b, i, pos_ref: (k, b, i, 0))
    return pl.pallas_call(
        body, name="sum_half",
        grid_spec=pltpu.PrefetchScalarGridSpec(
            num_scalar_prefetch=1, grid=(B, nr),
            in_specs=[own_spec, rspec(0), rspec(1), rspec(2)],
            out_specs=pl.BlockSpec((None, tr, cs), out_map)),
        out_shape=jax.ShapeDtypeStruct(out_shape, F32),
        compiler_params=_params("parallel", "parallel"),
    )(pos, own, recv, recv, recv)


def _adamw(w, g, m, v):
    shape = w.shape
    C = shape[-1]
    R = w.size // C
    tr = R
    for cand in (512, 256, 128, 64, 32, 16, 8):
        if R % cand == 0 and cand * C * 4 <= (1 << 20):
            tr = cand
            break
    bc1 = 1.0 - ADAM_B1 ** ADAM_STEP
    bc2 = 1.0 - ADAM_B2 ** ADAM_STEP

    def body(w_ref, g_ref, m_ref, v_ref, d_ref, mo_ref, vo_ref):
        gg = g_ref[...]
        mn = ADAM_B1 * m_ref[...] + (1.0 - ADAM_B1) * gg
        vn = ADAM_B2 * v_ref[...] + (1.0 - ADAM_B2) * (gg * gg)
        mo_ref[...] = mn
        vo_ref[...] = vn
        d_ref[...] = -ADAM_LR * ((mn / bc1) / (jnp.sqrt(vn / bc2) + ADAM_EPS) + ADAM_WD * w_ref[...])

    spec = lambda: pl.BlockSpec((tr, C), lambda i: (i, 0))
    outs = pl.pallas_call(
        body, name="adamw", grid=(R // tr,),
        in_specs=[spec()] * 4, out_specs=[spec()] * 3,
        out_shape=[jax.ShapeDtypeStruct((R, C), F32)] * 3,
        compiler_params=_params("parallel"),
    )(*[a.reshape(R, C) for a in (w, g, m, v)])
    return tuple(o.reshape(shape) for o in outs)


def kernel(x, mem, norm_g, mem_norm_g, w_in, gmlp_ln_g, gmlp_ln_b, w_s, b_s, conv_w, conv_b, conv_ln_g, conv_ln_b, w_kv, w_branch, w_out, final_norm_g, loss_target, m_norm_g, m_mem_norm_g, m_w_in, m_gmlp_ln_g, m_gmlp_ln_b, m_w_s, m_b_s, m_conv_w, m_conv_b, m_conv_ln_g, m_conv_ln_b, m_w_kv, m_w_branch, m_w_out, m_final_norm_g, v_norm_g, v_mem_norm_g, v_w_in, v_gmlp_ln_g, v_gmlp_ln_b, v_w_s, v_b_s, v_conv_w, v_conv_b, v_conv_ln_g, v_conv_ln_b, v_w_kv, v_w_branch, v_w_out, v_final_norm_g):
    xs, mems, tgt = x[0], mem[0], loss_target[0]
    S, D = xs.shape
    assert D // GROUPS == LANES and S % CHUNK == 0 and w_s.shape[-1] == CHUNK
    L = N_LAYERS

    assert L == 2
    shards = [w_in.astype(BF16)[:, None], w_kv.astype(BF16)[:, None], w_branch.astype(BF16), w_out.astype(BF16)[:, None],
              conv_w[:, None]]
    gather_axes = [(2, 1), (2, 1), (1, 2), (1, 2), (2, None)]
    reduce_axes = gather_axes[:4]
    win0 = _run_alone(_gather_comm(shards[:1], [0], gather_axes[:1]), "gather_w_in0")[0]
    gather_rest = _gather_comm(shards[1:] + shards, [0] * 4 + [1] * 5, gather_axes[1:] + gather_axes)

    tri = jnp.tril(jnp.ones((CHUNK, CHUNK), F32))
    t_mix = (w_s * tri).astype(BF16)
    t_mix_t = jnp.swapaxes(t_mix, -1, -2)
    bias_full = jnp.repeat(jnp.swapaxes(b_s, -1, -2), D // GROUPS, axis=-1)
    row = lambda a, l: a[l][None, :]
    xi, yi, ci = lax.axis_index("x"), lax.axis_index("y"), lax.axis_index("c")
    pos = jnp.stack([ci, 2 * xi + yi]).astype(jnp.int32)

    saved = []
    h = xs
    weights = [None] * L
    for l in range(L):
        if l == 0:
            z, hb, fetched = _in_proj(h, row(norm_g, l), win0, 0, gather_rest)
            weights = [[win0] + list(fetched[:4]), list(fetched[4:])]
        else:
            z, hb, _ = _in_proj(h, row(norm_g, l), weights[l][0], 0)
        win, wkv, wb, wo, cw = weights[l]
        bra = _branch_a_fwd(z, row(gmlp_ln_g, l), row(gmlp_ln_b, l), t_mix[l], bias_full[l])
        c0 = _conv_fwd(z, cw, row(conv_b, l), 0)
        brb = _branch_b_fwd(c0, z, row(conv_ln_g, l), row(conv_ln_b, l))
        mn, kv, kvt = _kv_fwd(mems, row(mem_norm_g, l), wkv, 0)
        brc = _branch_c_fwd(z, kv, kvt)
        merged, h_next = _merge_fwd(bra, brb, brc, z, h, wb[None], wo, 0)
        saved.append((h, z, hb, bra, c0, brb, mn, kv, kvt, brc, merged))
        h = h_next

    dx, loss_cols, d_final_g = _loss_bwd(h, tgt, final_norm_g[None, :])
    loss = lax.psum(0.5 * jnp.sum(loss_cols) / D, ("x", "y", "c"))

    def add_halves(grads, recv):
        return [_add_half(g, r, ax[1], pos) for g, r, ax in zip(grads, recv, reduce_axes)]

    def finish_reduction(sums, blocks):
        halves = [_sum_half(s, b, ax, pos) for (s, _), b, ax in zip(sums, blocks, reduce_axes)]
        return _sibling_share(halves, reduce_axes)

    small = {k: [None] * L for k in ("norm_g", "mem_norm_g", "gmlp_ln_g", "gmlp_ln_b", "w_s", "b_s", "conv_w",
                                     "conv_b", "conv_ln_g", "conv_ln_b")}
    pending, reduced = None, [None] * L
    for l in reversed(range(L)):
        h_in, z, hb, bra, c0, brb, mn, kv, kvt, brc, merged = saved[l]
        win, wkv, wb, wo, cw = weights[l]
        dz_m, dproj, dbr, recv = _merge_bwd(dx, bra, brb, brc, z, wb[None], wo, 0,
                                            _pair_comm(pending, reduce_axes) if pending else None)
        sums = add_halves(pending, recv) if pending else None
        g_o = _wgrad(merged, dx, (1, D, D), 0, 0, None)
        g_b = None
        for n, br in enumerate((bra, brb, brc)):
            g_b = _wgrad(br, dproj, (N_BRANCHES, D, D), n, 0, g_b, b_plane=n)
        dz_a, dws, dbs, dlg, dlb = _branch_a_bwd(z, dbr, row(gmlp_ln_g, l), row(gmlp_ln_b, l),
                                                t_mix[l], t_mix_t[l], bias_full[l], tri)
        dc0, dz_b, dcg, dcbeta, dcb = _branch_b_bwd(c0, z, dbr, row(conv_ln_g, l), row(conv_ln_b, l))
        dz_b, dcw = _conv_bwd(dc0, z, cw, dz_b, 0)
        dz_c, dk, dv = _branch_c_bwd(z, dbr, kv, kvt)
        g_kv, dmg = _kv_bwd(dk, dv, mn, mems, wkv, 0, None)
        pieces = [(dz_a, 0, 3), (dz_b, 3, 3), (dz_c, 6, 2), (dz_m, 8, 3)]
        g_in = None
        for dzp, lo, _ in pieces:
            g_in = _wgrad(hb, dzp, (1, D, N_PLANES * D), 0, lo, g_in)
        dx, dng, blocks = _in_proj_bwd(pieces, jnp.swapaxes(win, 1, 2), h_in, row(norm_g, l), dx, 0,
                                       _block_comm([sb for _, sb in sums], reduce_axes) if pending else None)
        if pending:
            reduced[l + 1] = finish_reduction(sums, blocks)
        pending = [g_in, g_kv, g_b, g_o]
        for k, val in (("norm_g", dng), ("mem_norm_g", dmg), ("gmlp_ln_g", dlg), ("gmlp_ln_b", dlb), ("w_s", dws),
                       ("b_s", dbs), ("conv_w", dcw), ("conv_b", dcb), ("conv_ln_g", dcg), ("conv_ln_b", dcbeta)):
            small[k][l] = val
    grad_x = dx[None]

    sums = add_halves(pending, _run_alone(_pair_comm(pending, reduce_axes), "pair_exchange"))
    reduced[0] = finish_reduction(sums, _run_alone(_block_comm([sb for _, sb in sums], reduce_axes), "block_exchange"))
    grad_w_in, grad_w_kv, grad_w_branch, grad_w_out = [
        jnp.stack([reduced[0][t], reduced[1][t]]).reshape(w.shape) for t, w in enumerate((w_in, w_kv, w_branch, w_out))]

    order = [("norm_g", norm_g.shape), ("mem_norm_g", mem_norm_g.shape), ("gmlp_ln_g", gmlp_ln_g.shape),
             ("gmlp_ln_b", gmlp_ln_b.shape), ("w_s", w_s.shape), ("b_s", b_s.shape),
             ("conv_w", (L, CONV_K, D)), ("conv_b", conv_b.shape), ("conv_ln_g", conv_ln_g.shape),
             ("conv_ln_b", conv_ln_b.shape)]
    parts = [jnp.stack([v.reshape(shp[1:]) for v in small[k]]).reshape(-1, LANES) for k, shp in order]
    parts.append(d_final_g.reshape(-1, LANES))
    sizes = [p.shape[0] for p in parts]
    assert all(s % 8 == 0 for s in sizes)
    total = _small_allreduce(jnp.concatenate(parts, axis=0))
    red, off = {}, 0
    for (k, shp), n in zip(order + [("final_norm_g", final_norm_g.shape)], sizes):
        red[k] = total[off:off + n].reshape(shp)
        off += n
    cs = conv_w.shape[-1]
    red["conv_w"] = lax.dynamic_slice_in_dim(red["conv_w"], (2 * xi + yi) * cs, cs, axis=2)
    red.update(w_in=grad_w_in, w_kv=grad_w_kv, w_branch=grad_w_branch, w_out=grad_w_out)

    names = ["norm_g", "mem_norm_g", "w_in", "gmlp_ln_g", "gmlp_ln_b", "w_s", "b_s", "conv_w", "conv_b",
             "conv_ln_g", "conv_ln_b", "w_kv", "w_branch", "w_out", "final_norm_g"]
    weights = dict(norm_g=norm_g, mem_norm_g=mem_norm_g, w_in=w_in, gmlp_ln_g=gmlp_ln_g, gmlp_ln_b=gmlp_ln_b,
                   w_s=w_s, b_s=b_s, conv_w=conv_w, conv_b=conv_b, conv_ln_g=conv_ln_g, conv_ln_b=conv_ln_b,
                   w_kv=w_kv, w_branch=w_branch, w_out=w_out, final_norm_g=final_norm_g)
    ms = dict(norm_g=m_norm_g, mem_norm_g=m_mem_norm_g, w_in=m_w_in, gmlp_ln_g=m_gmlp_ln_g, gmlp_ln_b=m_gmlp_ln_b,
              w_s=m_w_s, b_s=m_b_s, conv_w=m_conv_w, conv_b=m_conv_b, conv_ln_g=m_conv_ln_g, conv_ln_b=m_conv_ln_b,
              w_kv=m_w_kv, w_branch=m_w_branch, w_out=m_w_out, final_norm_g=m_final_norm_g)
    vs = dict(norm_g=v_norm_g, mem_norm_g=v_mem_norm_g, w_in=v_w_in, gmlp_ln_g=v_gmlp_ln_g, gmlp_ln_b=v_gmlp_ln_b,
              w_s=v_w_s, b_s=v_b_s, conv_w=v_conv_w, conv_b=v_conv_b, conv_ln_g=v_conv_ln_g, conv_ln_b=v_conv_ln_b,
              w_kv=v_w_kv, w_branch=v_w_branch, w_out=v_w_out, final_norm_g=v_final_norm_g)
    deltas, new_m, new_v = [], [], []
    for k in names:
        w2 = weights[k] if weights[k].ndim > 1 else weights[k][None, :]
        d, mo, vo = _adamw(w2, red[k].reshape(w2.shape), ms[k].reshape(w2.shape), vs[k].reshape(w2.shape))
        deltas.append(d.reshape(weights[k].shape))
        new_m.append(mo.reshape(weights[k].shape))
        new_v.append(vo.reshape(weights[k].shape))
    grads = [red[k].reshape(weights[k].shape) for k in names]
    return (loss, grad_x, *grads, *deltas, *new_m, *new_v)
```

```python
import math

import jax
import jax.numpy as jnp
from jax import lax
from jax.experimental import pallas as pl
from jax.experimental.pallas import tpu as pltpu

F32 = jnp.float32
BF16 = jnp.bfloat16

N_LAYERS = 2
N_BRANCHES = 3
N_PLANES = 11
N_CHIPS = 4
CHUNK = 128
GROUPS = 8
HEADS = 4
CONV_K = 31
HALO = 32
LANES = 128
RMS_EPS = 1e-6
LN_EPS = 1e-5
ADAM_LR, ADAM_B1, ADAM_B2, ADAM_EPS, ADAM_WD, ADAM_STEP = 0.001, 0.9, 0.999, 1e-08, 0.01, 10
VMEM_LIMIT_BYTES = 48 * 1024 * 1024
MESH = pl.DeviceIdType.MESH
ANY = pl.BlockSpec(memory_space=pl.ANY)


def _params(*sem):
    return pltpu.CompilerParams(dimension_semantics=sem, vmem_limit_bytes=VMEM_LIMIT_BYTES)


def _tile(n, pref):
    t = min(n, pref)
    assert n % t == 0, (n, t)
    return t


def _dot(a, b):
    return jnp.dot(a, b, preferred_element_type=F32)


def _dot_tb(a, b):
    return lax.dot_general(a, b, (((1,), (1,)), ((), ())), preferred_element_type=F32)


def _dot_ta(a, b):
    return lax.dot_general(a, b, (((0,), (0,)), ((), ())), preferred_element_type=F32)


def _sigmoid(x):
    return 1.0 / (1.0 + jnp.exp(-x))


def _silu(x):
    return x * _sigmoid(x)


def _silu_and_grad(x):
    s = _sigmoid(x)
    return x * s, s * (1.0 + x * (1.0 - s))


_GELU_C = math.sqrt(2.0 / math.pi)
_GELU_A = 0.044715


def _gelu(x):
    return x * (0.5 * (1.0 + jnp.tanh(_GELU_C * (x + _GELU_A * (x * x * x)))))


def _gelu_and_grad(x):
    x2 = x * x
    t = jnp.tanh(_GELU_C * (x + _GELU_A * (x2 * x)))
    cdf = 0.5 * (1.0 + t)
    dcdf = 0.5 * (1.0 - t * t) * (_GELU_C * (1.0 + 3.0 * _GELU_A * x2))
    return x * cdf, cdf + x * dcdf


def _rows(i, n):
    return pl.ds(pl.multiple_of(i * n, n), n)


def _grouped_loop(n, group, body, init):
    assert n % group == 0

    def trip(i, c):
        for u in range(group):
            c = body(i * group + u, c)
        return c
    return lax.fori_loop(0, n // group, trip, init)


def _mean(x):
    return jnp.mean(x, axis=-1, keepdims=True)


def _colsum(x):
    return jnp.sum(x, axis=0, keepdims=True)


def _tok(ts, d):
    return pl.BlockSpec((ts, d), lambda i: (i, 0))


def _zcol(ts, d, n):
    return pl.BlockSpec((ts, d), lambda i: (i, n))


def _row1(d):
    return pl.BlockSpec((1, d), lambda i: (0, 0))


def _plane(ts, d, n):
    return pl.BlockSpec((None, ts, d), lambda i: (n, i, 0))


class _Hosted:
    def __init__(self, ins, out_shapes, sem_shapes, phases):
        self.ins, self.out_shapes, self.sem_shapes, self.phases = ins, out_shapes, sem_shapes, phases


def _host(comm, n_in, n_out, n_scr, grid, body):
    if comm is None:
        return body, [], [], [], [], None
    ci, co = len(comm.ins), len(comm.out_shapes)
    total = math.prod(grid)
    at = {"first": 0, "late": (7 * total) // 8, "last": total - 1}

    def wrapped(*refs):
        ins, cin = refs[:n_in], refs[n_in:n_in + ci]
        o0 = n_in + ci
        outs, cout = refs[o0:o0 + n_out], refs[o0 + n_out:o0 + n_out + co]
        s0 = o0 + n_out + co
        scr, csem = refs[s0:s0 + n_scr], refs[s0 + n_scr:]
        step = pl.program_id(0)
        for a in range(1, len(grid)):
            step = step * grid[a] + pl.program_id(a)

        def run(where):
            for w, fn in comm.phases:
                if w == where:
                    @pl.when(step == at[where])
                    def _(fn=fn):
                        fn(cin, cout, csem)
        run("first")
        run("late")
        body(*ins, *outs, *scr)
        run("last")

    return wrapped, [ANY] * ci, [ANY] * co, list(comm.out_shapes), list(comm.sem_shapes), ("arbitrary",) * len(grid)


def _in_proj(x, g, w_all, l, comm=None):
    S, D = x.shape
    N = w_all.shape[2]
    tm, tn = _tile(S, 1024), _tile(N, 1408)
    grid = (S // tm, N // tn)

    def body(x_ref, g_ref, w_ref, z_ref, hb_ref, h_scr):
        @pl.when(pl.program_id(1) == 0)
        def _():
            def slab(i, c):
                r = _rows(i, 16)
                xs = x_ref[r, :]
                h = (xs * lax.rsqrt(_mean(xs * xs) + RMS_EPS)) * g_ref[...]
                hb = h.astype(BF16)
                h_scr[r, :] = hb
                hb_ref[r, :] = hb
                return c
            _grouped_loop(tm // 16, 4, slab, 0)
        z_ref[...] = _dot(h_scr[...], w_ref[...])

    body, c_in, c_out, c_shapes, c_scr, sem = _host(comm, 3, 2, 1, grid, body)
    outs = pl.pallas_call(
        body, name="in_proj", grid=grid,
        in_specs=[pl.BlockSpec((tm, D), lambda i, j: (i, 0)),
                  pl.BlockSpec((1, D), lambda i, j: (0, 0)),
                  pl.BlockSpec((None, D, tn), lambda i, j: (l, 0, j))] + c_in,
        out_specs=[pl.BlockSpec((tm, tn), lambda i, j: (i, j)),
                   pl.BlockSpec((tm, D), lambda i, j: (i, 0))] + c_out,
        out_shape=[jax.ShapeDtypeStruct((S, N), F32), jax.ShapeDtypeStruct((S, D), BF16)] + c_shapes,
        scratch_shapes=[pltpu.VMEM((tm, D), BF16)] + c_scr,
        compiler_params=_params(*(sem or ("parallel", "arbitrary"))),
    )(x, g, w_all, *(comm.ins if comm else []))
    return outs[0], outs[1], outs[2:]


def _branch_a_fwd(z, lng, lnb, t_mix, bias_full):
    S = z.shape[0]
    D = lng.shape[1]
    GD = D // GROUPS
    ts = _tile(S, 256)

    def body(zu_ref, zv_ref, zg_ref, lng_ref, lnb_ref, t_ref, bias_ref, a_ref, v_scr):
        def chunk(ci, carry):
            c0 = pl.multiple_of(ci * CHUNK, CHUNK)

            def slab(si, c):
                r = pl.multiple_of(si * 16, 16)
                vg = _gelu(zv_ref[pl.ds(pl.multiple_of(c0 + r, 16), 16), :])
                xc = vg - _mean(vg)
                y = xc * lax.rsqrt(_mean(xc * xc) + LN_EPS)
                v_scr[pl.ds(r, 16), :] = (y * lng_ref[...] + lnb_ref[...]).astype(BF16)
                return c
            _grouped_loop(CHUNK // 16, 4, slab, 0)
            for g in range(GROUPS):
                cols = slice(g * GD, (g + 1) * GD)
                sv = _dot(t_ref[g], v_scr[:, cols]) + bias_ref[:, cols]
                u = _gelu(zu_ref[pl.ds(c0, CHUNK), cols])
                a_ref[pl.ds(c0, CHUNK), cols] = ((u * sv) * _silu(zg_ref[pl.ds(c0, CHUNK), cols])).astype(BF16)
            return carry
        lax.fori_loop(0, ts // CHUNK, chunk, 0)

    return pl.pallas_call(
        body, name="branch_a_fwd", grid=(S // ts,),
        in_specs=[_zcol(ts, D, 0), _zcol(ts, D, 1), _zcol(ts, D, 2), _row1(D), _row1(D),
                  pl.BlockSpec((GROUPS, CHUNK, CHUNK), lambda i: (0, 0, 0)),
                  pl.BlockSpec((CHUNK, D), lambda i: (0, 0))],
        out_specs=_tok(ts, D),
        out_shape=jax.ShapeDtypeStruct((S, D), BF16),
        scratch_shapes=[pltpu.VMEM((CHUNK, D), BF16)],
        compiler_params=_params("parallel"),
    )(z, z, z, lng, lnb, t_mix, bias_full)


SH = HALO - CONV_K + 1


def _fill_shifts(sh_scr, src_scr, cols, ts):
    for j in range(1, 8):
        for g in range(ts // 32):
            sh_scr[j - 1, pl.ds(g * 32, 32), :] = src_scr[pl.ds(g * 32 + j, 32), cols]
        sh_scr[j - 1, pl.ds(ts, HALO - 8), :] = src_scr[pl.ds(ts + j, HALO - 8), cols]


def _shifted(sh_scr, src_scr, cols, row0, n):
    q, j = divmod(row0, 8)
    if j == 0:
        return src_scr[pl.ds(row0, n), cols]
    return sh_scr[j - 1, pl.ds(8 * q, n), :]


def _conv_fwd(z, w_all, cb, l):
    S = z.shape[0]
    D = cb.shape[1]
    ts = _tile(S, 256)
    CB = 256
    hb = ts // HALO

    def body(za_ref, zb_ref, zap_ref, zbp_ref, w_ref, b_ref, c0_ref, glu_scr, sh_scr):
        i = pl.program_id(0)
        prev = zap_ref[...] * _sigmoid(zbp_ref[...])
        glu_scr[0:HALO, :] = jnp.where(i == 0, 0.0, prev)

        def fill(si, c):
            r = _rows(si, 32)
            glu_scr[pl.ds(pl.multiple_of(HALO + si * 32, 32), 32), :] = za_ref[r, :] * _sigmoid(zb_ref[r, :])
            return c
        _grouped_loop(ts // 32, 2, fill, 0)

        def colblock(ci, c):
            cols = _rows(ci, CB)
            _fill_shifts(sh_scr, glu_scr, cols, ts)
            for s in range(ts // 32):
                acc = jnp.zeros((32, CB), F32)
                for k in range(CONV_K):
                    acc = acc + _shifted(sh_scr, glu_scr, cols, s * 32 + k + SH, 32) * w_ref[k:k + 1, cols]
                c0_ref[pl.ds(s * 32, 32), cols] = acc + b_ref[:, cols]
            return c
        lax.fori_loop(0, D // CB, colblock, 0)

    prev_map = lambda n: (lambda i: (jnp.maximum(i * hb - 1, 0), n))
    return pl.pallas_call(
        body, name="conv_fwd", grid=(S // ts,),
        in_specs=[_zcol(ts, D, 3), _zcol(ts, D, 4),
                  pl.BlockSpec((HALO, D), prev_map(3)), pl.BlockSpec((HALO, D), prev_map(4)),
                  pl.BlockSpec((None, CONV_K, D), lambda i: (l, 0, 0)), _row1(D)],
        out_specs=_tok(ts, D),
        out_shape=jax.ShapeDtypeStruct((S, D), F32),
        scratch_shapes=[pltpu.VMEM((ts + HALO, D), F32), pltpu.VMEM((7, ts + HALO, CB), F32)],
        compiler_params=_params("parallel"),
    )(z, z, z, z, w_all, cb)


def _branch_b_fwd(c0, z, g, b):
    S, D = c0.shape
    ts = _tile(S, 512)

    def body(c0_ref, zg_ref, g_ref, b_ref, o_ref):
        def slab(si, c):
            r = _rows(si, 16)
            v = c0_ref[r, :]
            xc = v - _mean(v)
            c1 = (xc * lax.rsqrt(_mean(xc * xc) + LN_EPS)) * g_ref[...] + b_ref[...]
            o_ref[r, :] = (_silu(c1) * _silu(zg_ref[r, :])).astype(BF16)
            return c
        _grouped_loop(ts // 16, 4, slab, 0)

    return pl.pallas_call(
        body, name="branch_b_fwd", grid=(S // ts,),
        in_specs=[_tok(ts, D), _zcol(ts, D, 5), _row1(D), _row1(D)],
        out_specs=_tok(ts, D),
        out_shape=jax.ShapeDtypeStruct((S, D), BF16),
        compiler_params=_params("parallel"),
    )(c0, z, g, b)


def _kv_fwd(mem, mg, wkv_all, l):
    M, D = mem.shape

    def body(mem_ref, g_ref, w_ref, mn_ref, kv_ref, kvt_ref):
        m = mem_ref[...]
        mn = ((m * lax.rsqrt(_mean(m * m) + RMS_EPS)) * g_ref[...]).astype(BF16)
        mn_ref[...] = mn
        kv = _dot(mn, w_ref[...])
        kv_ref[...] = kv.astype(BF16)
        kvt_ref[...] = kv.T.astype(BF16)

    return pl.pallas_call(
        body, name="kv_fwd", grid=(1,),
        in_specs=[pl.BlockSpec((M, D), lambda i: (0, 0)), _row1(D),
                  pl.BlockSpec((None, D, 2 * D), lambda i: (l, 0, 0))],
        out_specs=[pl.BlockSpec((M, D), lambda i: (0, 0)), pl.BlockSpec((M, 2 * D), lambda i: (0, 0)),
                   pl.BlockSpec((2 * D, M), lambda i: (0, 0))],
        out_shape=[jax.ShapeDtypeStruct((M, D), BF16), jax.ShapeDtypeStruct((M, 2 * D), BF16),
                   jax.ShapeDtypeStruct((2 * D, M), BF16)],
        compiler_params=_params("arbitrary"),
    )(mem, mg, wkv_all)


def _softmax_rows(s):
    e = jnp.exp(s - jnp.max(s, axis=-1, keepdims=True))
    return e / jnp.sum(e, axis=-1, keepdims=True)


def _branch_c_fwd(z, kv, kvt):
    S = z.shape[0]
    M, D2 = kv.shape
    D = D2 // 2
    HD = D // HEADS
    scale = 1.0 / math.sqrt(HD)
    ts = _tile(S, 256)

    def body(q_ref, zg_ref, kt_ref, v_ref, o_ref):
        for h in range(HEADS):
            cols = slice(h * HD, (h + 1) * HD)
            p = _softmax_rows(_dot(q_ref[:, cols].astype(BF16), kt_ref[cols, :]) * scale)
            att = _dot(p.astype(BF16), v_ref[:, cols])
            o_ref[:, cols] = (att * _silu(zg_ref[:, cols])).astype(BF16)

    return pl.pallas_call(
        body, name="branch_c_fwd", grid=(S // ts,),
        in_specs=[_zcol(ts, D, 6), _zcol(ts, D, 7),
                  pl.BlockSpec((D, M), lambda i: (0, 0)), pl.BlockSpec((M, D), lambda i: (0, 1))],
        out_specs=_tok(ts, D),
        out_shape=jax.ShapeDtypeStruct((S, D), BF16),
        compiler_params=_params("parallel"),
    )(z, z, kvt, kv)


def _merge_fwd(bra, brb, brc, z, x, wb_all, wo_all, l):
    S, D = x.shape
    ts = _tile(S, 512)
    CB = 256

    def body(a_ref, b_ref, c_ref, zm0_ref, zm1_ref, zm2_ref, x_ref, wb_ref, wo_ref, m_ref, xo_ref):
        brs = (a_ref, b_ref, c_ref)
        zms = (zm0_ref, zm1_ref, zm2_ref)
        for cb in range(D // CB):
            cols = slice(cb * CB, (cb + 1) * CB)
            acc = None
            for n in range(N_BRANCHES):
                t = _sigmoid(zms[n][:, cols]) * _dot(brs[n][...], wb_ref[n, :, cols])
                acc = t if acc is None else acc + t
            m_ref[:, cols] = acc.astype(BF16)
        for cb in range(D // CB):
            cols = slice(cb * CB, (cb + 1) * CB)
            xo_ref[:, cols] = x_ref[:, cols] + _dot(m_ref[...], wo_ref[:, cols])

    return pl.pallas_call(
        body, name="merge_fwd", grid=(S // ts,),
        in_specs=[_tok(ts, D), _tok(ts, D), _tok(ts, D), _zcol(ts, D, 8), _zcol(ts, D, 9), _zcol(ts, D, 10),
                  _tok(ts, D),
                  pl.BlockSpec((None, N_BRANCHES, D, D), lambda i: (l, 0, 0, 0)),
                  pl.BlockSpec((None, D, D), lambda i: (l, 0, 0))],
        out_specs=[_tok(ts, D), _tok(ts, D)],
        out_shape=[jax.ShapeDtypeStruct((S, D), BF16), jax.ShapeDtypeStruct((S, D), F32)],
        compiler_params=_params("parallel"),
    )(bra, brb, brc, z, z, z, x, wb_all, wo_all)


def _loss_bwd(x, target, fg):
    S, D = x.shape
    ts = _tile(S, 512)

    def body(x_ref, t_ref, g_ref, dx_ref, ls_ref, dg_ref):
        @pl.when(pl.program_id(0) == 0)
        def _():
            ls_ref[...] = jnp.zeros_like(ls_ref)
            dg_ref[...] = jnp.zeros_like(dg_ref)

        def slab(si, c):
            ls, dg = c
            r = _rows(si, 16)
            xs = x_ref[r, :]
            rs = lax.rsqrt(_mean(xs * xs) + RMS_EPS)
            n = xs * rs
            e = n * g_ref[...] - t_ref[r, :]
            dy = e * (1.0 / D)
            dn = dy * g_ref[...]
            dx_ref[r, :] = rs * (dn - n * _mean(dn * n))
            return ls + _colsum(e * e), dg + _colsum(dy * n)
        zero = jnp.zeros((1, D), F32)
        ls, dg = _grouped_loop(ts // 16, 4, slab, (zero, zero))
        ls_ref[...] += ls
        dg_ref[...] += dg

    return pl.pallas_call(
        body, name="loss_bwd", grid=(S // ts,),
        in_specs=[_tok(ts, D), _tok(ts, D), _row1(D)],
        out_specs=[_tok(ts, D), _row1(D), _row1(D)],
        out_shape=[jax.ShapeDtypeStruct((S, D), F32), jax.ShapeDtypeStruct((1, D), F32),
                   jax.ShapeDtypeStruct((1, D), F32)],
        compiler_params=_params("arbitrary"),
    )(x, target, fg)


def _merge_bwd(dxp, bra, brb, brc, z, wb_all, wo_all, l, comm=None):
    S, D = dxp.shape
    ts = _tile(S, 256)
    CB = 256

    def body(d_ref, a_ref, b_ref, c_ref, zm0_ref, zm1_ref, zm2_ref, wb_ref, wo_ref,
             dzm_ref, dproj_ref, dbr_ref, dxb_scr):
        brs = (a_ref, b_ref, c_ref)
        zms = (zm0_ref, zm1_ref, zm2_ref)
        dxb_scr[...] = d_ref[...].astype(BF16)
        for cb in range(D // CB):
            cols = slice(cb * CB, (cb + 1) * CB)
            dm = _dot_tb(dxb_scr[...], wo_ref[cols, :])
            for n in range(N_BRANCHES):
                proj = _dot(brs[n][...], wb_ref[n, :, cols])
                g = _sigmoid(zms[n][:, cols])
                dzm_ref[:, n * D + cb * CB:n * D + (cb + 1) * CB] = ((dm * proj) * (g * (1.0 - g))).astype(BF16)
                dproj_ref[n, :, cols] = (dm * g).astype(BF16)
        for n in range(N_BRANCHES):
            for cb in range(D // CB):
                cols = slice(cb * CB, (cb + 1) * CB)
                dbr_ref[n, :, cols] = _dot_tb(dproj_ref[n], wb_ref[n, cols, :])

    tok3 = lambda: pl.BlockSpec((N_BRANCHES, ts, D), lambda i: (0, i, 0))
    grid = (S // ts,)
    body, c_in, c_out, c_shapes, c_scr, sem = _host(comm, 9, 3, 1, grid, body)
    outs = pl.pallas_call(
        body, name="merge_bwd", grid=grid,
        in_specs=[_tok(ts, D), _tok(ts, D), _tok(ts, D), _tok(ts, D),
                  _zcol(ts, D, 8), _zcol(ts, D, 9), _zcol(ts, D, 10),
                  pl.BlockSpec((None, N_BRANCHES, D, D), lambda i: (l, 0, 0, 0)),
                  pl.BlockSpec((None, D, D), lambda i: (l, 0, 0))] + c_in,
        out_specs=[_tok(ts, 3 * D), tok3(), tok3()] + c_out,
        out_shape=[jax.ShapeDtypeStruct((S, 3 * D), BF16), jax.ShapeDtypeStruct((N_BRANCHES, S, D), BF16),
                   jax.ShapeDtypeStruct((N_BRANCHES, S, D), F32)] + c_shapes,
        scratch_shapes=[pltpu.VMEM((ts, D), BF16)] + c_scr,
        compiler_params=_params(*(sem or ("parallel",))),
    )(dxp, bra, brb, brc, z, z, z, wb_all, wo_all, *(comm.ins if comm else []))
    return outs[0], outs[1], outs[2], outs[3:]


def _branch_a_bwd(z, dbr, lng, lnb, t_mix, t_mix_t, bias_full, mask):
    S = z.shape[0]
    D = lng.shape[1]
    GD = D // GROUPS
    ts = _tile(S, 256)
    nt = S // ts

    def body(zu_ref, zv_ref, zg_ref, d_ref, lng_ref, lnb_ref, t_ref, tt_ref, bias_ref, mask_ref,
             dz_ref, dws_ref, dbs_ref, dlg_ref, dlb_ref,
             v_scr, vn_scr, rstd_scr, dv_scr, dsv_acc):
        i = pl.program_id(0)

        @pl.when(i == 0)
        def _():
            dws_ref[...] = jnp.zeros_like(dws_ref)
            dlg_ref[...] = jnp.zeros_like(dlg_ref)
            dlb_ref[...] = jnp.zeros_like(dlb_ref)
            dsv_acc[...] = jnp.zeros_like(dsv_acc)

        def chunk(ci, carry):
            c0 = pl.multiple_of(ci * CHUNK, CHUNK)

            def ln_fwd(si, c):
                r = pl.multiple_of(si * 16, 16)
                vg = _gelu(zv_ref[pl.ds(pl.multiple_of(c0 + r, 16), 16), :])
                xc = vg - _mean(vg)
                rstd = lax.rsqrt(_mean(xc * xc) + LN_EPS)
                vn = xc * rstd
                vn_scr[pl.ds(r, 16), :] = vn
                rstd_scr[pl.ds(r, 16), :] = rstd
                v_scr[pl.ds(r, 16), :] = (vn * lng_ref[...] + lnb_ref[...]).astype(BF16)
                return c
            _grouped_loop(CHUNK // 16, 4, ln_fwd, 0)

            for g in range(GROUPS):
                cols = slice(g * GD, (g + 1) * GD)
                rows = pl.ds(c0, CHUNK)
                vb = v_scr[:, cols]
                sv = _dot(t_ref[g], vb) + bias_ref[:, cols]
                u, du = _gelu_and_grad(zu_ref[rows, cols])
                sg, dsg = _silu_and_grad(zg_ref[rows, cols])
                d = d_ref[rows, cols]
                dsv = (d * u) * sg
                dz_ref[rows, g * GD:(g + 1) * GD] = (((d * sv) * sg) * du).astype(BF16)
                dz_ref[rows, 2 * D + g * GD:2 * D + (g + 1) * GD] = (((d * u) * sv) * dsg).astype(BF16)
                dsvb = dsv.astype(BF16)
                dws_ref[g] += _dot_tb(dsvb, vb)
                dv_scr[:, cols] = _dot(tt_ref[g], dsvb)
                dsv_acc[:, cols] += dsv

            def ln_bwd(si, c):
                dlg, dlb = c
                r = pl.multiple_of(si * 16, 16)
                rr = pl.ds(r, 16)
                zrows = pl.ds(pl.multiple_of(c0 + r, 16), 16)
                dv = dv_scr[rr, :]
                vn = vn_scr[rr, :]
                dvn = dv * lng_ref[...]
                dvg = rstd_scr[rr, :] * ((dvn - _mean(dvn)) - vn * _mean(dvn * vn))
                _, gg = _gelu_and_grad(zv_ref[zrows, :])
                dz_ref[zrows, D:2 * D] = (dvg * gg).astype(BF16)
                return dlg + _colsum(dv * vn), dlb + _colsum(dv)
            zero = jnp.zeros((1, D), F32)
            dlg, dlb = _grouped_loop(CHUNK // 16, 4, ln_bwd, (zero, zero))
            dlg_ref[...] += dlg
            dlb_ref[...] += dlb
            return carry
        lax.fori_loop(0, ts // CHUNK, chunk, 0)

        @pl.when(i == nt - 1)
        def _():
            for g in range(GROUPS):
                cols = slice(g * GD, (g + 1) * GD)
                dws_ref[g] = dws_ref[g] * mask_ref[...]
                dbs_ref[g:g + 1, :] = _colsum(dsv_acc[:, cols].T)

    c2 = lambda shape: pl.BlockSpec(shape, lambda i: (0, 0))
    c3 = lambda: pl.BlockSpec((GROUPS, CHUNK, CHUNK), lambda i: (0, 0, 0))
    return pl.pallas_call(
        body, name="branch_a_bwd", grid=(nt,),
        in_specs=[_zcol(ts, D, 0), _zcol(ts, D, 1), _zcol(ts, D, 2), _plane(ts, D, 0), _row1(D), _row1(D),
                  c3(), c3(), c2((CHUNK, D)), c2((CHUNK, CHUNK))],
        out_specs=[_tok(ts, 3 * D), c3(), c2((GROUPS, CHUNK)), _row1(D), _row1(D)],
        out_shape=[jax.ShapeDtypeStruct((S, 3 * D), BF16), jax.ShapeDtypeStruct((GROUPS, CHUNK, CHUNK), F32),
                   jax.ShapeDtypeStruct((GROUPS, CHUNK), F32), jax.ShapeDtypeStruct((1, D), F32),
                   jax.ShapeDtypeStruct((1, D), F32)],
        scratch_shapes=[pltpu.VMEM((CHUNK, D), BF16), pltpu.VMEM((CHUNK, D), F32), pltpu.VMEM((CHUNK, 1), F32),
                        pltpu.VMEM((CHUNK, D), F32), pltpu.VMEM((CHUNK, D), F32)],
        compiler_params=_params("arbitrary"),
    )(z, z, z, dbr, lng, lnb, t_mix, t_mix_t, bias_full, mask)


def _branch_b_bwd(c0, z, dbr, g, b):
    S, D = c0.shape
    ts = _tile(S, 512)

    def body(c0_ref, zg_ref, d_ref, g_ref, b_ref, dc0_ref, dz_ref, dg_ref, db_ref, dcb_ref):
        @pl.when(pl.program_id(0) == 0)
        def _():
            dg_ref[...] = jnp.zeros_like(dg_ref)
            db_ref[...] = jnp.zeros_like(db_ref)
            dcb_ref[...] = jnp.zeros_like(dcb_ref)

        def slab(si, c):
            dg, db, dcb = c
            r = _rows(si, 16)
            v = c0_ref[r, :]
            xc = v - _mean(v)
            rstd = lax.rsqrt(_mean(xc * xc) + LN_EPS)
            cn = xc * rstd
            c1 = cn * g_ref[...] + b_ref[...]
            c2, dc2_dc1 = _silu_and_grad(c1)
            sg, dsg = _silu_and_grad(zg_ref[r, :])
            d = d_ref[r, :]
            dz_ref[r, :] = ((d * c2) * dsg).astype(BF16)
            dc1 = (d * sg) * dc2_dc1
            dcn = dc1 * g_ref[...]
            dc0 = rstd * ((dcn - _mean(dcn)) - cn * _mean(dcn * cn))
            dc0_ref[r, :] = dc0
            return dg + _colsum(dc1 * cn), db + _colsum(dc1), dcb + _colsum(dc0)
        zero = jnp.zeros((1, D), F32)
        dg, db, dcb = _grouped_loop(ts // 16, 4, slab, (zero, zero, zero))
        dg_ref[...] += dg
        db_ref[...] += db
        dcb_ref[...] += dcb

    return pl.pallas_call(
        body, name="branch_b_bwd", grid=(S // ts,),
        in_specs=[_tok(ts, D), _zcol(ts, D, 5), _plane(ts, D, 1), _row1(D), _row1(D)],
        out_specs=[_tok(ts, D), _zcol(ts, D, 2), _row1(D), _row1(D), _row1(D)],
        out_shape=[jax.ShapeDtypeStruct((S, D), F32), jax.ShapeDtypeStruct((S, 3 * D), BF16),
                   jax.ShapeDtypeStruct((1, D), F32), jax.ShapeDtypeStruct((1, D), F32),
                   jax.ShapeDtypeStruct((1, D), F32)],
        compiler_params=_params("arbitrary"),
    )(c0, z, dbr, g, b)


def _conv_bwd(dc0, z, w_all, dz_b, l):
    S, D = dc0.shape
    ts = _tile(S, 256)
    nt = S // ts
    CB = 256
    hb = ts // HALO
    nh = S // HALO
    SH = HALO - CONV_K + 1

    def body(dc_ref, dcn_ref, za_ref, zb_ref, zap_ref, zbp_ref, w_ref, dzin_ref,
             dz_ref, dw_ref, glu_scr, dc_scr, dw_acc, gsh_scr, dsh_scr):
        del dzin_ref
        i = pl.program_id(0)

        @pl.when(i == 0)
        def _():
            dw_acc[...] = jnp.zeros_like(dw_acc)

        prev = zap_ref[...] * _sigmoid(zbp_ref[...])
        glu_scr[0:HALO, :] = jnp.where(i == 0, 0.0, prev)
        dc_scr[ts:ts + HALO, :] = jnp.where(i == nt - 1, 0.0, dcn_ref[...])

        def fill(si, c):
            r = _rows(si, 32)
            glu_scr[pl.ds(pl.multiple_of(HALO + si * 32, 32), 32), :] = za_ref[r, :] * _sigmoid(zb_ref[r, :])
            dc_scr[r, :] = dc_ref[r, :]
            return c
        _grouped_loop(ts // 32, 2, fill, 0)

        def colblock(ci, c):
            cols = _rows(ci, CB)
            cols_b = pl.ds(pl.multiple_of(D + ci * CB, CB), CB)
            _fill_shifts(gsh_scr, glu_scr, cols, ts)
            _fill_shifts(dsh_scr, dc_scr, cols, ts)
            for s in range(ts // 32):
                rows = pl.ds(s * 32, 32)
                dc = dc_scr[rows, cols]
                acc = jnp.zeros((32, CB), F32)
                for k in range(CONV_K):
                    prod = dc * _shifted(gsh_scr, glu_scr, cols, s * 32 + k + SH, 32)
                    dw_acc[k, :, cols] += (prod[0:8] + prod[8:16]) + (prod[16:24] + prod[24:32])
                    acc = acc + (_shifted(dsh_scr, dc_scr, cols, s * 32 + k, 32)
                                 * w_ref[CONV_K - 1 - k:CONV_K - k, cols])
                sg = _sigmoid(zb_ref[rows, cols])
                dz_ref[rows, cols] = (acc * sg).astype(BF16)
                dz_ref[rows, cols_b] = ((acc * za_ref[rows, cols]) * (sg * (1.0 - sg))).astype(BF16)
            return c
        lax.fori_loop(0, D // CB, colblock, 0)

        @pl.when(i == nt - 1)
        def _():
            for k in range(CONV_K):
                dw_ref[k:k + 1, :] = _colsum(dw_acc[k])

    prev_map = lambda n: (lambda i: (jnp.maximum(i * hb - 1, 0), n))
    return pl.pallas_call(
        body, name="conv_bwd", grid=(nt,),
        in_specs=[_tok(ts, D), pl.BlockSpec((HALO, D), lambda i: (jnp.minimum((i + 1) * hb, nh - 1), 0)),
                  _zcol(ts, D, 3), _zcol(ts, D, 4),
                  pl.BlockSpec((HALO, D), prev_map(3)), pl.BlockSpec((HALO, D), prev_map(4)),
                  pl.BlockSpec((None, CONV_K, D), lambda i: (l, 0, 0)), ANY],
        out_specs=[_tok(ts, 2 * D), pl.BlockSpec((CONV_K, D), lambda i: (0, 0))],
        out_shape=[jax.ShapeDtypeStruct((S, 3 * D), BF16), jax.ShapeDtypeStruct((CONV_K, D), F32)],
        scratch_shapes=[pltpu.VMEM((ts + HALO, D), F32), pltpu.VMEM((ts + HALO, D), F32),
                        pltpu.VMEM((CONV_K, 8, D), F32),
                        pltpu.VMEM((7, ts + HALO, CB), F32), pltpu.VMEM((7, ts + HALO, CB), F32)],
        input_output_aliases={7: 0},
        compiler_params=_params("arbitrary"),
    )(dc0, dc0, z, z, z, z, w_all, dz_b)


def _branch_c_bwd(z, dbr, kv, kvt):
    S = z.shape[0]
    M, D2 = kv.shape
    D = D2 // 2
    HD = D // HEADS
    scale = 1.0 / math.sqrt(HD)
    ts = _tile(S, 256)

    def body(q_ref, zg_ref, d_ref, kt_ref, vt_ref, k_ref, v_ref, dz_ref, dk_ref, dv_ref):
        @pl.when(pl.program_id(0) == 0)
        def _():
            dk_ref[...] = jnp.zeros_like(dk_ref)
            dv_ref[...] = jnp.zeros_like(dv_ref)

        for h in range(HEADS):
            cols = slice(h * HD, (h + 1) * HD)
            qb = q_ref[:, cols].astype(BF16)
            p = _softmax_rows(_dot(qb, kt_ref[cols, :]) * scale)
            pb = p.astype(BF16)
            att = _dot(pb, v_ref[:, cols])
            sg, dsg = _silu_and_grad(zg_ref[:, cols])
            d = d_ref[:, cols]
            dz_ref[:, D + h * HD:D + (h + 1) * HD] = ((d * att) * dsg).astype(BF16)
            datt = (d * sg).astype(BF16)
            dp = _dot(datt, vt_ref[cols, :])
            dv_ref[:, cols] += _dot_ta(pb, datt)
            ds = ((p * (dp - jnp.sum(dp * p, axis=-1, keepdims=True))) * scale).astype(BF16)
            dz_ref[:, cols] = _dot(ds, k_ref[:, cols]).astype(BF16)
            dk_ref[:, cols] += _dot_ta(ds, qb)

    return pl.pallas_call(
        body, name="branch_c_bwd", grid=(S // ts,),
        in_specs=[_zcol(ts, D, 6), _zcol(ts, D, 7), _plane(ts, D, 2),
                  pl.BlockSpec((D, M), lambda i: (0, 0)), pl.BlockSpec((D, M), lambda i: (1, 0)),
                  pl.BlockSpec((M, D), lambda i: (0, 0)), pl.BlockSpec((M, D), lambda i: (0, 1))],
        out_specs=[_tok(ts, 2 * D), pl.BlockSpec((M, D), lambda i: (0, 0)), pl.BlockSpec((M, D), lambda i: (0, 0))],
        out_shape=[jax.ShapeDtypeStruct((S, 2 * D), BF16), jax.ShapeDtypeStruct((M, D), F32),
                   jax.ShapeDtypeStruct((M, D), F32)],
        compiler_params=_params("arbitrary"),
    )(z, z, dbr, kvt, kvt, kv, kv)


def _kv_bwd(dk, dv, mn, mem, wkv_all, l, prev):
    M, D = mem.shape

    def body(*refs):
        dk_ref, dv_ref, mn_ref, mem_ref, w_ref = refs[:5]
        dw_ref, dg_ref = refs[-2:]
        dkb = dk_ref[...].astype(BF16)
        dvb = dv_ref[...].astype(BF16)
        dw_ref[:, 0:D] = _dot_ta(mn_ref[...], dkb)
        dw_ref[:, D:2 * D] = _dot_ta(mn_ref[...], dvb)
        dmn = _dot_tb(dkb, w_ref[:, 0:D]) + _dot_tb(dvb, w_ref[:, D:2 * D])
        m = mem_ref[...]
        dg_ref[...] = _colsum(dmn * (m * lax.rsqrt(_mean(m * m) + RMS_EPS)))

    full = lambda shape: pl.BlockSpec(shape, lambda i: (0, 0))
    in_specs = [full((M, D)), full((M, D)), full((M, D)), full((M, D)),
                pl.BlockSpec((None, D, 2 * D), lambda i: (l, 0, 0))]
    args = [dk, dv, mn, mem, wkv_all]
    aliases = {}
    if prev is not None:
        in_specs.append(ANY)
        args.append(prev)
        aliases = {5: 0}
    return pl.pallas_call(
        body, name="kv_bwd", grid=(1,),
        in_specs=in_specs,
        out_specs=[pl.BlockSpec((None, D, 2 * D), lambda i: (l, 0, 0)), _row1(D)],
        out_shape=[jax.ShapeDtypeStruct((wkv_all.shape[0], D, 2 * D), F32), jax.ShapeDtypeStruct((1, D), F32)],
        input_output_aliases=aliases,
        compiler_params=_params("arbitrary"),
    )(*args)


def _in_proj_bwd(pieces, w_all, x, g, dxp, l, comm=None):
    S, D = x.shape
    tm = _tile(S, 1024)
    n_p = len(pieces)

    def body(*refs):
        dz_refs = refs[:n_p]
        w_ref, x_hbm, g_ref, dxp_hbm, dx_ref, dg_ref, acc, x_ref, dxp_ref, sem = refs[n_p:]
        i = pl.program_id(0)
        k = pl.program_id(1)
        rows = pl.ds(pl.multiple_of(i * tm, tm), tm)
        fetch = [pltpu.make_async_copy(x_hbm.at[rows, :], x_ref, sem.at[0]),
                 pltpu.make_async_copy(dxp_hbm.at[rows, :], dxp_ref, sem.at[1])]

        @pl.when(k == N_PLANES - 2)
        def _():
            for cp in fetch:
                cp.start()

        @pl.when(k == 0)
        def _():
            acc[...] = jnp.zeros_like(acc)

        @pl.when((i == 0) & (k == 0))
        def _():
            dg_ref[...] = jnp.zeros_like(dg_ref)

        for (_, lo, n), r in zip(pieces, dz_refs):
            @pl.when((k >= lo) & (k < lo + n))
            def _(r=r):
                acc[...] += _dot_tb(r[...], w_ref[...])

        @pl.when(k == N_PLANES - 1)
        def _():
            for cp in fetch:
                cp.wait()

            def slab(si, c):
                rr = _rows(si, 16)
                xs = x_ref[rr, :]
                rs = lax.rsqrt(_mean(xs * xs) + RMS_EPS)
                n = xs * rs
                dh = acc[rr, :]
                dn = dh * g_ref[...]
                dx_ref[rr, :] = rs * (dn - n * _mean(dn * n)) + dxp_ref[rr, :]
                return c + _colsum(dh * n)
            dg_ref[...] += _grouped_loop(tm // 16, 4, slab, jnp.zeros((1, D), F32))

    def piece_spec(lo, n):
        return pl.BlockSpec((tm, D), lambda i, k: (i, jnp.clip(k - lo, 0, n - 1)))

    tokk = lambda: pl.BlockSpec((tm, D), lambda i, k: (i, 0))
    rowk = lambda: pl.BlockSpec((1, D), lambda i, k: (0, 0))
    grid = (S // tm, N_PLANES)
    body, c_in, c_out, c_shapes, c_scr, _ = _host(comm, n_p + 4, 2, 4, grid, body)
    outs = pl.pallas_call(
        body, name="in_proj_bwd", grid=grid,
        in_specs=[piece_spec(lo, n) for _, lo, n in pieces]
        + [pl.BlockSpec((None, D, D), lambda i, k: (l, 0, k)), ANY, rowk(), ANY] + c_in,
        out_specs=[tokk(), rowk()] + c_out,
        out_shape=[jax.ShapeDtypeStruct((S, D), F32), jax.ShapeDtypeStruct((1, D), F32)] + c_shapes,
        scratch_shapes=[pltpu.VMEM((tm, D), F32), pltpu.VMEM((tm, D), F32), pltpu.VMEM((tm, D), F32),
                        pltpu.SemaphoreType.DMA((2,))] + c_scr,
        compiler_params=_params("arbitrary", "arbitrary"),
    )(*[p for p, _, _ in pieces], w_all, x, g, dxp, *(comm.ins if comm else []))
    return outs[0], outs[1], outs[2:]


def _wgrad(a, b, out_shape, plane, col_off, prev, a_plane=None, b_plane=None):
    S = a.shape[-2]
    K1 = a.shape[-1]
    nb_cols = b.shape[-1]
    ts = _tile(S, 2048)
    tn = _tile(nb_cols, 1024)
    col_off = col_off // tn

    def body(*refs):
        a_ref, b_ref = refs[:2]
        o_ref = refs[-1]
        k = pl.program_id(1)
        prod = _dot_ta(a_ref[...].astype(BF16), b_ref[...].astype(BF16))

        @pl.when(k == 0)
        def _():
            o_ref[...] = prod

        @pl.when(k > 0)
        def _():
            o_ref[...] += prod

    if a_plane is None:
        a_spec = pl.BlockSpec((ts, K1), lambda j, k: (k, 0))
    else:
        a_spec = pl.BlockSpec((None, ts, K1), lambda j, k: (a_plane, k, 0))
    if b_plane is None:
        b_spec = pl.BlockSpec((ts, tn), lambda j, k: (k, j))
    else:
        b_spec = pl.BlockSpec((None, ts, tn), lambda j, k: (b_plane, k, j))
    in_specs, args, aliases = [a_spec, b_spec], [a, b], {}
    if prev is not None:
        in_specs.append(ANY)
        args.append(prev)
        aliases = {2: 0}
    return pl.pallas_call(
        body, name="wgrad", grid=(nb_cols // tn, S // ts),
        in_specs=in_specs,
        out_specs=pl.BlockSpec((None, K1, tn), lambda j, k: (plane, 0, col_off + j)),
        out_shape=jax.ShapeDtypeStruct(out_shape, F32),
        input_output_aliases=aliases,
        compiler_params=_params("parallel", "arbitrary"),
    )(*args)


def _position():
    x, y, c = lax.axis_index("x"), lax.axis_index("y"), lax.axis_index("c")
    chips = [(1 - x, y), (x, 1 - y), (1 - x, 1 - y)]
    return x, y, c, chips


def _sub(ref, axis, start, size):
    idx = [slice(None)] * len(ref.shape)
    idx[axis] = pl.ds(start, size)
    return ref.at[tuple(idx)]


def _dma(src, dst, ssem, rsem, to):
    return pltpu.make_async_remote_copy(src_ref=src, dst_ref=dst, send_sem=ssem, recv_sem=rsem,
                                        device_id=to, device_id_type=MESH)


def _dma_sems(*counts):
    return [pltpu.SemaphoreType.DMA((n,)) for n in counts]


def _run_alone(comm, name):
    ci, co = len(comm.ins), len(comm.out_shapes)

    def body(*refs):
        for _, fn in comm.phases:
            fn(refs[:ci], refs[ci:ci + co], refs[ci + co:])

    return pl.pallas_call(
        body, name=name, in_specs=[ANY] * ci, out_specs=[ANY] * co, out_shape=list(comm.out_shapes),
        scratch_shapes=list(comm.sem_shapes),
    )(*comm.ins)


def _gather_comm(shards, layers, axes):
    n = len(shards)
    out_shapes = []
    for s, (ba, _) in zip(shards, axes):
        shp = list(s.shape[1:])
        shp[ba] *= N_CHIPS
        out_shapes.append(jax.ShapeDtypeStruct(tuple(shp), s.dtype))

    def half(ref, t, h):
        ha = axes[t][1]
        if ha is None:
            return ref
        hs = shards[t].shape[1 + ha] // 2
        return _sub(ref, ha, h * hs, hs)

    def block(outs, t, blk):
        ba = axes[t][0]
        bs = shards[t].shape[1 + ba]
        return _sub(outs[t], ba, blk * bs, bs)

    def start(ins, outs, sems):
        s_own, r_own, s_ici, r_ici, _, _ = sems
        x, y, c, chips = _position()
        j = 2 * x + y
        for t in range(n):
            _dma(ins[t].at[layers[t]], block(outs, t, j), s_own.at[t], r_own.at[t], (x, y, 1 - c)).start()
            for k, (px, py) in enumerate(chips):
                _dma(half(ins[t].at[layers[t]], t, c), half(block(outs, t, j), t, c),
                     s_ici.at[3 * t + k], r_ici.at[3 * t + k], (px, py, c)).start()

    def forward(ins, outs, sems):
        _, _, s_ici, r_ici, s_fwd, r_fwd = sems
        x, y, c, chips = _position()
        for t in range(n):
            if axes[t][1] is None:
                continue
            for k, (px, py) in enumerate(chips):
                got = half(block(outs, t, 2 * px + py), t, c)
                _dma(got, got, s_ici.at[3 * t + k], r_ici.at[3 * t + k], (px, py, c)).wait_recv()
                _dma(got, got, s_fwd.at[3 * t + k], r_fwd.at[3 * t + k], (x, y, 1 - c)).start()

    def finish(ins, outs, sems):
        s_own, r_own, s_ici, r_ici, s_fwd, r_fwd = sems
        x, y, c, chips = _position()
        j = 2 * x + y
        sib = (x, y, 1 - c)
        for t in range(n):
            for k, (px, py) in enumerate(chips):
                i = 3 * t + k
                sent = half(block(outs, t, j), t, c)
                _dma(sent, sent, s_ici.at[i], r_ici.at[i], (px, py, c)).wait_send()
                theirs = block(outs, t, 2 * px + py)
                if axes[t][1] is None:
                    _dma(theirs, theirs, s_ici.at[i], r_ici.at[i], (px, py, c)).wait_recv()
                else:
                    got, other = half(theirs, t, c), half(theirs, t, 1 - c)
                    _dma(got, got, s_fwd.at[i], r_fwd.at[i], sib).wait_send()
                    _dma(other, other, s_fwd.at[i], r_fwd.at[i], sib).wait_recv()
            _dma(ins[t].at[layers[t]], block(outs, t, j), s_own.at[t], r_own.at[t], sib).wait()

    return _Hosted(list(shards), out_shapes, _dma_sems(n, n, 3 * n, 3 * n, 3 * n, 3 * n),
                   [("first", start), ("late", forward), ("last", finish)])


def _halved(shape, axis):
    shp = list(shape)
    shp[axis] //= 2
    return tuple(shp)


def _pair_comm(grads, axes):
    n = len(grads)

    def copies(ins, outs, sems):
        x, y, c, _ = _position()
        cps = []
        for t in range(n):
            ha = axes[t][1]
            hs = grads[t].shape[ha] // 2
            cps.append(_dma(_sub(ins[t], ha, (1 - c) * hs, hs), outs[t], sems[0].at[t], sems[1].at[t], (x, y, 1 - c)))
        return cps

    def start(ins, outs, sems):
        for cp in copies(ins, outs, sems):
            cp.start()

    def finish(ins, outs, sems):
        for cp in copies(ins, outs, sems):
            cp.wait()

    return _Hosted(list(grads), [jax.ShapeDtypeStruct(_halved(g.shape, ax[1]), g.dtype) for g, ax in zip(grads, axes)],
                   _dma_sems(n, n), [("first", start), ("last", finish)])


def _block_comm(sums, axes):
    n = len(sums)
    out_shapes = []
    for s, (ba, _) in zip(sums, axes):
        shp = list(s.shape)
        shp[ba] //= N_CHIPS
        out_shapes.append(jax.ShapeDtypeStruct((3,) + tuple(shp), s.dtype))

    def copies(ins, outs, sems):
        x, y, c, chips = _position()
        cps = []
        for t in range(n):
            ba = axes[t][0]
            bs = sums[t].shape[ba] // N_CHIPS
            for k, (px, py) in enumerate(chips):
                cps.append(_dma(_sub(ins[t], ba, (2 * px + py) * bs, bs), outs[t].at[k],
                                sems[0].at[3 * t + k], sems[1].at[3 * t + k], (px, py, c)))
        return cps

    def start(ins, outs, sems):
        for cp in copies(ins, outs, sems):
            cp.start()

    def finish(ins, outs, sems):
        for cp in copies(ins, outs, sems):
            cp.wait()

    return _Hosted(list(sums), out_shapes, _dma_sems(3 * n, 3 * n), [("first", start), ("last", finish)])


def _sibling_share(shards, axes):
    n = len(shards)

    def body(*refs):
        outs = refs[n:2 * n]
        ssem, rsem = refs[2 * n:]
        x, y, c, _ = _position()
        cps = []
        for t in range(n):
            ha = axes[t][1]
            hs = shards[t].shape[ha] // 2
            mine = _sub(outs[t], ha, c * hs, hs)
            cps.append(_dma(mine, mine, ssem.at[t], rsem.at[t], (x, y, 1 - c)))
        for cp in cps:
            cp.start()
        for cp in cps:
            cp.wait()

    return pl.pallas_call(
        body, name="sibling_share",
        in_specs=[ANY] * n, out_specs=[ANY] * n,
        out_shape=[jax.ShapeDtypeStruct(s.shape, s.dtype) for s in shards],
        input_output_aliases={t: t for t in range(n)},
        scratch_shapes=_dma_sems(n, n),
    )(*shards)


def _small_allreduce(p):
    R, C = p.shape

    def body(p_ref, o_ref, sib_buf, chip_sums, ssem, rsem):
        x, y, c, chips = _position()
        j = 2 * x + y
        sib = pltpu.make_async_remote_copy(
            src_ref=p_ref, dst_ref=sib_buf, send_sem=ssem.at[0], recv_sem=rsem.at[0],
            device_id=(x, y, 1 - c), device_id_type=MESH)
        sib.start()
        sib.wait()
        chip_sums[j] = p_ref[...] + sib_buf[...]
        cps = [pltpu.make_async_remote_copy(
            src_ref=chip_sums.at[j], dst_ref=chip_sums.at[j], send_sem=ssem.at[1 + k], recv_sem=rsem.at[1 + k],
            device_id=(px, py, c), device_id_type=MESH) for k, (px, py) in enumerate(chips)]
        for cp in cps:
            cp.start()
        for k, (px, py) in enumerate(chips):
            pltpu.make_async_remote_copy(
                src_ref=chip_sums.at[j], dst_ref=chip_sums.at[2 * px + py],
                send_sem=ssem.at[1 + k], recv_sem=rsem.at[1 + k],
                device_id=(px, py, c), device_id_type=MESH).wait()
        o_ref[...] = ((chip_sums[0] + chip_sums[1]) + chip_sums[2]) + chip_sums[3]

    vm = pl.BlockSpec(memory_space=pltpu.VMEM)
    return pl.pallas_call(
        body, name="small_allreduce",
        in_specs=[vm], out_specs=vm, out_shape=jax.ShapeDtypeStruct((R, C), F32),
        scratch_shapes=[pltpu.VMEM((R, C), F32), pltpu.VMEM((N_CHIPS, R, C), F32),
                        pltpu.SemaphoreType.DMA((4,)), pltpu.SemaphoreType.DMA((4,))],
        compiler_params=pltpu.CompilerParams(vmem_limit_bytes=VMEM_LIMIT_BYTES),
    )(p)


def _add_half(g, recv, half_axis, pos):
    B, R, C = recv.shape
    tr = _tile(R, 256)
    tc = _tile(C, 2816)
    nr, nc = R // tr, C // tc
    if half_axis == 1:
        g_map = lambda b, i, j, pos_ref: (b, pos_ref[0] * nr + i, j)
    else:
        g_map = lambda b, i, j, pos_ref: (b, i, pos_ref[0] * nc + j)

    def body(pos_ref, g_ref, r_ref, o_ref, ob_ref):
        del pos_ref
        s = g_ref[...] + r_ref[...]
        o_ref[...] = s
        ob_ref[...] = s.astype(BF16)

    spec = lambda: pl.BlockSpec((None, tr, tc), lambda b, i, j, pos_ref: (b, i, j))
    return pl.pallas_call(
        body, name="add_half",
        grid_spec=pltpu.PrefetchScalarGridSpec(
            num_scalar_prefetch=1, grid=(B, nr, nc),
            in_specs=[pl.BlockSpec((None, tr, tc), g_map), spec()],
            out_specs=[spec(), spec()]),
        out_shape=[jax.ShapeDtypeStruct((B, R, C), F32), jax.ShapeDtypeStruct((B, R, C), BF16)],
        compiler_params=_params("parallel", "parallel", "parallel"),
    )(pos, g, recv)


def _sum_half(own, recv, axes, pos):
    _, B, rs, cs = recv.shape
    ba, ha = axes
    tr = _tile(rs, 128)
    nr = rs // tr

    def body(pos_ref, o_ref, r0_ref, r1_ref, r2_ref, out_ref):
        del pos_ref
        out_ref[...] = (((o_ref[...] + r0_ref[...].astype(F32)) + r1_ref[...].astype(F32))
                        + r2_ref[...].astype(F32))

    if ba == 2:
        own_spec = pl.BlockSpec((None, tr, cs), lambda b, i, pos_ref: (b, i, pos_ref[1]))
    else:
        own_spec = pl.BlockSpec((None, tr, cs), lambda b, i, pos_ref: (b, pos_ref[1] * nr + i, 0))
    if ha == 1:
        out_shape, out_map = (B, 2 * rs, cs), (lambda b, i, pos_ref: (b, pos_ref[0] * nr + i, 0))
    else:
        out_shape, out_map = (B, rs, 2 * cs), (lambda b, i, pos_ref: (b, i, pos_ref[0]))
    rspec = lambda k: pl.BlockSpec((None, None, tr, cs), lambda b, i, pos_ref: (k, b, i, 0))
    return pl.pallas_call(
        body, name="sum_half",
        grid_spec=pltpu.PrefetchScalarGridSpec(
            num_scalar_prefetch=1, grid=(B, nr),
            in_specs=[own_spec, rspec(0), rspec(1), rspec(2)],
            out_specs=pl.BlockSpec((None, tr, cs), out_map)),
        out_shape=jax.ShapeDtypeStruct(out_shape, F32),
        compiler_params=_params("parallel", "parallel"),
    )(pos, own, recv, recv, recv)


def _adamw(w, g, m, v):
    shape = w.shape
    C = shape[-1]
    R = w.size // C
    tr = R
    for cand in (512, 256, 128, 64, 32, 16, 8):
        if R % cand == 0 and cand * C * 4 <= (1 << 20):
            tr = cand
            break
    bc1 = 1.0 - ADAM_B1 ** ADAM_STEP
    bc2 = 1.0 - ADAM_B2 ** ADAM_STEP

    def body(w_ref, g_ref, m_ref, v_ref, d_ref, mo_ref, vo_ref):
        gg = g_ref[...]
        mn = ADAM_B1 * m_ref[...] + (1.0 - ADAM_B1) * gg
        vn = ADAM_B2 * v_ref[...] + (1.0 - ADAM_B2) * (gg * gg)
        mo_ref[...] = mn
        vo_ref[...] = vn
        d_ref[...] = -ADAM_LR * ((mn / bc1) / (jnp.sqrt(vn / bc2) + ADAM_EPS) + ADAM_WD * w_ref[...])

    spec = lambda: pl.BlockSpec((tr, C), lambda i: (i, 0))
    outs = pl.pallas_call(
        body, name="adamw", grid=(R // tr,),
        in_specs=[spec()] * 4, out_specs=[spec()] * 3,
        out_shape=[jax.ShapeDtypeStruct((R, C), F32)] * 3,
        compiler_params=_params("parallel"),
    )(*[a.reshape(R, C) for a in (w, g, m, v)])
    return tuple(o.reshape(shape) for o in outs)


def kernel(x, mem, norm_g, mem_norm_g, w_in, gmlp_ln_g, gmlp_ln_b, w_s, b_s, conv_w, conv_b, conv_ln_g, conv_ln_b, w_kv, w_branch, w_out, final_norm_g, loss_target, m_norm_g, m_mem_norm_g, m_w_in, m_gmlp_ln_g, m_gmlp_ln_b, m_w_s, m_b_s, m_conv_w, m_conv_b, m_conv_ln_g, m_conv_ln_b, m_w_kv, m_w_branch, m_w_out, m_final_norm_g, v_norm_g, v_mem_norm_g, v_w_in, v_gmlp_ln_g, v_gmlp_ln_b, v_w_s, v_b_s, v_conv_w, v_conv_b, v_conv_ln_g, v_conv_ln_b, v_w_kv, v_w_branch, v_w_out, v_final_norm_g):
    xs, mems, tgt = x[0], mem[0], loss_target[0]
    S, D = xs.shape
    assert D // GROUPS == LANES and S % CHUNK == 0 and w_s.shape[-1] == CHUNK
    L = N_LAYERS

    assert L == 2
    shards = [w_in.astype(BF16)[:, None], w_kv.astype(BF16)[:, None], w_branch.astype(BF16), w_out.astype(BF16)[:, None],
              conv_w[:, None]]
    gather_axes = [(2, 1), (2, 1), (1, 2), (1, 2), (2, None)]
    reduce_axes = gather_axes[:4]
    win0 = _run_alone(_gather_comm(shards[:1], [0], gather_axes[:1]), "gather_w_in0")[0]
    gather_rest = _gather_comm(shards[1:] + shards, [0] * 4 + [1] * 5, gather_axes[1:] + gather_axes)

    tri = jnp.tril(jnp.ones((CHUNK, CHUNK), F32))
    t_mix = (w_s * tri).astype(BF16)
    t_mix_t = jnp.swapaxes(t_mix, -1, -2)
    bias_full = jnp.repeat(jnp.swapaxes(b_s, -1, -2), D // GROUPS, axis=-1)
    row = lambda a, l: a[l][None, :]
    xi, yi, ci = lax.axis_index("x"), lax.axis_index("y"), lax.axis_index("c")
    pos = jnp.stack([ci, 2 * xi + yi]).astype(jnp.int32)

    saved = []
    h = xs
    weights = [None] * L
    for l in range(L):
        if l == 0:
            z, hb, fetched = _in_proj(h, row(norm_g, l), win0, 0, gather_rest)
            weights = [[win0] + list(fetched[:4]), list(fetched[4:])]
        else:
            z, hb, _ = _in_proj(h, row(norm_g, l), weights[l][0], 0)
        win, wkv, wb, wo, cw = weights[l]
        bra = _branch_a_fwd(z, row(gmlp_ln_g, l), row(gmlp_ln_b, l), t_mix[l], bias_full[l])
        c0 = _conv_fwd(z, cw, row(conv_b, l), 0)
        brb = _branch_b_fwd(c0, z, row(conv_ln_g, l), row(conv_ln_b, l))
        mn, kv, kvt = _kv_fwd(mems, row(mem_norm_g, l), wkv, 0)
        brc = _branch_c_fwd(z, kv, kvt)
        merged, h_next = _merge_fwd(bra, brb, brc, z, h, wb[None], wo, 0)
        saved.append((h, z, hb, bra, c0, brb, mn, kv, kvt, brc, merged))
        h = h_next

    dx, loss_cols, d_final_g = _loss_bwd(h, tgt, final_norm_g[None, :])
    loss = lax.psum(0.5 * jnp.sum(loss_cols) / D, ("x", "y", "c"))

    def add_halves(grads, recv):
        return [_add_half(g, r, ax[1], pos) for g, r, ax in zip(grads, recv, reduce_axes)]

    def finish_reduction(sums, blocks):
        halves = [_sum_half(s, b, ax, pos) for (s, _), b, ax in zip(sums, blocks, reduce_axes)]
        return _sibling_share(halves, reduce_axes)

    small = {k: [None] * L for k in ("norm_g", "mem_norm_g", "gmlp_ln_g", "gmlp_ln_b", "w_s", "b_s", "conv_w",
                                     "conv_b", "conv_ln_g", "conv_ln_b")}
    pending, reduced = None, [None] * L
    for l in reversed(range(L)):
        h_in, z, hb, bra, c0, brb, mn, kv, kvt, brc, merged = saved[l]
        win, wkv, wb, wo, cw = weights[l]
        dz_m, dproj, dbr, recv = _merge_bwd(dx, bra, brb, brc, z, wb[None], wo, 0,
                                            _pair_comm(pending, reduce_axes) if pending else None)
        sums = add_halves(pending, recv) if pending else None
        g_o = _wgrad(merged, dx, (1, D, D), 0, 0, None)
        g_b = None
        for n, br in enumerate((bra, brb, brc)):
            g_b = _wgrad(br, dproj, (N_BRANCHES, D, D), n, 0, g_b, b_plane=n)
        dz_a, dws, dbs, dlg, dlb = _branch_a_bwd(z, dbr, row(gmlp_ln_g, l), row(gmlp_ln_b, l),
                                                t_mix[l], t_mix_t[l], bias_full[l], tri)
        dc0, dz_b, dcg, dcbeta, dcb = _branch_b_bwd(c0, z, dbr, row(conv_ln_g, l), row(conv_ln_b, l))
        dz_b, dcw = _conv_bwd(dc0, z, cw, dz_b, 0)
        dz_c, dk, dv = _branch_c_bwd(z, dbr, kv, kvt)
        g_kv, dmg = _kv_bwd(dk, dv, mn, mems, wkv, 0, None)
        pieces = [(dz_a, 0, 3), (dz_b, 3, 3), (dz_c, 6, 2), (dz_m, 8, 3)]
        g_in = None
        for dzp, lo, _ in pieces:
            g_in = _wgrad(hb, dzp, (1, D, N_PLANES * D), 0, lo * D, g_in)
        dx, dng, blocks = _in_proj_bwd(pieces, win, h_in, row(norm_g, l), dx, 0,
                                       _block_comm([sb for _, sb in sums], reduce_axes) if pending else None)
        if pending:
            reduced[l + 1] = finish_reduction(sums, blocks)
        pending = [g_in, g_kv, g_b, g_o]
        for k, val in (("norm_g", dng), ("mem_norm_g", dmg), ("gmlp_ln_g", dlg), ("gmlp_ln_b", dlb), ("w_s", dws),
                       ("b_s", dbs), ("conv_w", dcw), ("conv_b", dcb), ("conv_ln_g", dcg), ("conv_ln_b", dcbeta)):
            small[k][l] = val
    grad_x = dx[None]

    sums = add_halves(pending, _run_alone(_pair_comm(pending, reduce_axes), "pair_exchange"))
    reduced[0] = finish_reduction(sums, _run_alone(_block_comm([sb for _, sb in sums], reduce_axes), "block_exchange"))
    grad_w_in, grad_w_kv, grad_w_branch, grad_w_out = [
        jnp.stack([reduced[0][t], reduced[1][t]]).reshape(w.shape) for t, w in enumerate((w_in, w_kv, w_branch, w_out))]

    order = [("norm_g", norm_g.shape), ("mem_norm_g", mem_norm_g.shape), ("gmlp_ln_g", gmlp_ln_g.shape),
             ("gmlp_ln_b", gmlp_ln_b.shape), ("w_s", w_s.shape), ("b_s", b_s.shape),
             ("conv_w", (L, CONV_K, D)), ("conv_b", conv_b.shape), ("conv_ln_g", conv_ln_g.shape),
             ("conv_ln_b", conv_ln_b.shape)]
    parts = [jnp.stack([v.reshape(shp[1:]) for v in small[k]]).reshape(-1, LANES) for k, shp in order]
    parts.append(d_final_g.reshape(-1, LANES))
    sizes = [p.shape[0] for p in parts]
    assert all(s % 8 == 0 for s in sizes)
    total = _small_allreduce(jnp.concatenate(parts, axis=0))
    red, off = {}, 0
    for (k, shp), n in zip(order + [("final_norm_g", final_norm_g.shape)], sizes):
        red[k] = total[off:off + n].reshape(shp)
        off += n
    cs = conv_w.shape[-1]
    red["conv_w"] = lax.dynamic_slice_in_dim(red["conv_w"], (2 * xi + yi) * cs, cs, axis=2)
    red.update(w_in=grad_w_in, w_kv=grad_w_kv, w_branch=grad_w_branch, w_out=grad_w_out)

    names = ["norm_g", "mem_norm_g", "w_in", "gmlp_ln_g", "gmlp_ln_b", "w_s", "b_s", "conv_w", "conv_b",
             "conv_ln_g", "conv_ln_b", "w_kv", "w_branch", "w_out", "final_norm_g"]
    weights = dict(norm_g=norm_g, mem_norm_g=mem_norm_g, w_in=w_in, gmlp_ln_g=gmlp_ln_g, gmlp_ln_b=gmlp_ln_b,
                   w_s=w_s, b_s=b_s, conv_w=conv_w, conv_b=conv_b, conv_ln_g=conv_ln_g, conv_ln_b=conv_ln_b,
                   w_kv=w_kv, w_branch=w_branch, w_out=w_out, final_norm_g=final_norm_g)
    ms = dict(norm_g=m_norm_g, mem_norm_g=m_mem_norm_g, w_in=m_w_in, gmlp_ln_g=m_gmlp_ln_g, gmlp_ln_b=m_gmlp_ln_b,
              w_s=m_w_s, b_s=m_b_s, conv_w=m_conv_w, conv_b=m_conv_b, conv_ln_g=m_conv_ln_g, conv_ln_b=m_conv_ln_b,
              w_kv=m_w_kv, w_branch=m_w_branch, w_out=m_w_out, final_norm_g=m_final_norm_g)
    vs = dict(norm_g=v_norm_g, mem_norm_g=v_mem_norm_g, w_in=v_w_in, gmlp_ln_g=v_gmlp_ln_g, gmlp_ln_b=v_gmlp_ln_b,
              w_s=v_w_s, b_s=v_b_s, conv_w=v_conv_w, conv_b=v_conv_b, conv_ln_g=v_conv_ln_g, conv_ln_b=v_conv_ln_b,
              w_kv=v_w_kv, w_branch=v_w_branch, w_out=v_w_out, final_norm_g=v_final_norm_g)
    deltas, new_m, new_v = [], [], []
    for k in names:
        w2 = weights[k] if weights[k].ndim > 1 else weights[k][None, :]
        d, mo, vo = _adamw(w2, red[k].reshape(w2.shape), ms[k].reshape(w2.shape), vs[k].reshape(w2.shape))
        deltas.append(d.reshape(weights[k].shape))
        new_m.append(mo.reshape(weights[k].shape))
        new_v.append(vo.reshape(weights[k].shape))
    grads = [red[k].reshape(weights[k].shape) for k in names]
    return (loss, grad_x, *grads, *deltas, *new_m, *new_v)
```

```python
import math

import jax
import jax.numpy as jnp
from jax import lax
from jax.experimental import pallas as pl
from jax.experimental.pallas import tpu as pltpu

F32 = jnp.float32
BF16 = jnp.bfloat16

N_LAYERS = 2
N_BRANCHES = 3
N_PLANES = 11
N_CHIPS = 4
CHUNK = 128
GROUPS = 8
HEADS = 4
CONV_K = 31
HALO = 32
LANES = 128
RMS_EPS = 1e-6
LN_EPS = 1e-5
ADAM_LR, ADAM_B1, ADAM_B2, ADAM_EPS, ADAM_WD, ADAM_STEP = 0.001, 0.9, 0.999, 1e-08, 0.01, 10
VMEM_LIMIT_BYTES = 48 * 1024 * 1024
MESH = pl.DeviceIdType.MESH
ANY = pl.BlockSpec(memory_space=pl.ANY)


def _params(*sem):
    return pltpu.CompilerParams(dimension_semantics=sem, vmem_limit_bytes=VMEM_LIMIT_BYTES)


def _tile(n, pref):
    t = min(n, pref)
    assert n % t == 0, (n, t)
    return t


def _dot(a, b):
    return jnp.dot(a, b, preferred_element_type=F32)


def _dot_tb(a, b):
    return lax.dot_general(a, b, (((1,), (1,)), ((), ())), preferred_element_type=F32)


def _dot_ta(a, b):
    return lax.dot_general(a, b, (((0,), (0,)), ((), ())), preferred_element_type=F32)


def _sigmoid(x):
    return 1.0 / (1.0 + jnp.exp(-x))


def _silu(x):
    return x * _sigmoid(x)


def _silu_and_grad(x):
    s = _sigmoid(x)
    return x * s, s * (1.0 + x * (1.0 - s))


_GELU_C = math.sqrt(2.0 / math.pi)
_GELU_A = 0.044715


def _gelu(x):
    return x * (0.5 * (1.0 + jnp.tanh(_GELU_C * (x + _GELU_A * (x * x * x)))))


def _gelu_and_grad(x):
    x2 = x * x
    t = jnp.tanh(_GELU_C * (x + _GELU_A * (x2 * x)))
    cdf = 0.5 * (1.0 + t)
    dcdf = 0.5 * (1.0 - t * t) * (_GELU_C * (1.0 + 3.0 * _GELU_A * x2))
    return x * cdf, cdf + x * dcdf


def _rows(i, n):
    return pl.ds(pl.multiple_of(i * n, n), n)


def _grouped_loop(n, group, body, init):
    assert n % group == 0

    def trip(i, c):
        for u in range(group):
            c = body(i * group + u, c)
        return c
    return lax.fori_loop(0, n // group, trip, init)


def _mean(x):
    return jnp.mean(x, axis=-1, keepdims=True)


def _colsum(x):
    return jnp.sum(x, axis=0, keepdims=True)


def _tok(ts, d):
    return pl.BlockSpec((ts, d), lambda i: (i, 0))


def _zcol(ts, d, n):
    return pl.BlockSpec((ts, d), lambda i: (i, n))


def _row1(d):
    return pl.BlockSpec((1, d), lambda i: (0, 0))


def _plane(ts, d, n):
    return pl.BlockSpec((None, ts, d), lambda i: (n, i, 0))


class _Hosted:
    def __init__(self, ins, out_shapes, sem_shapes, phases):
        self.ins, self.out_shapes, self.sem_shapes, self.phases = ins, out_shapes, sem_shapes, phases


def _host(comm, n_in, n_out, n_scr, grid, body):
    if comm is None:
        return body, [], [], [], [], None
    ci, co = len(comm.ins), len(comm.out_shapes)
    total = math.prod(grid)
    at = {"first": 0, "late": (7 * total) // 8, "last": total - 1}

    def wrapped(*refs):
        ins, cin = refs[:n_in], refs[n_in:n_in + ci]
        o0 = n_in + ci
        outs, cout = refs[o0:o0 + n_out], refs[o0 + n_out:o0 + n_out + co]
        s0 = o0 + n_out + co
        scr, csem = refs[s0:s0 + n_scr], refs[s0 + n_scr:]
        step = pl.program_id(0)
        for a in range(1, len(grid)):
            step = step * grid[a] + pl.program_id(a)

        def run(where):
            for w, fn in comm.phases:
                if w == where:
                    @pl.when(step == at[where])
                    def _(fn=fn):
                        fn(cin, cout, csem)
        run("first")
        run("late")
        body(*ins, *outs, *scr)
        run("last")

    return wrapped, [ANY] * ci, [ANY] * co, list(comm.out_shapes), list(comm.sem_shapes), ("arbitrary",) * len(grid)


def _in_proj(x, g, w_all, l, comm=None):
    S, D = x.shape
    N = w_all.shape[2]
    tm, tn = _tile(S, 1024), _tile(N, 1408)
    grid = (S // tm, N // tn)

    def body(x_ref, g_ref, w_ref, z_ref, hb_ref, h_scr):
        @pl.when(pl.program_id(1) == 0)
        def _():
            def slab(i, c):
                r = _rows(i, 16)
                xs = x_ref[r, :]
                h = (xs * lax.rsqrt(_mean(xs * xs) + RMS_EPS)) * g_ref[...]
                hb = h.astype(BF16)
                h_scr[r, :] = hb
                hb_ref[r, :] = hb
                return c
            _grouped_loop(tm // 16, 4, slab, 0)
        z_ref[...] = _dot(h_scr[...], w_ref[...])

    body, c_in, c_out, c_shapes, c_scr, sem = _host(comm, 3, 2, 1, grid, body)
    outs = pl.pallas_call(
        body, name="in_proj", grid=grid,
        in_specs=[pl.BlockSpec((tm, D), lambda i, j: (i, 0)),
                  pl.BlockSpec((1, D), lambda i, j: (0, 0)),
                  pl.BlockSpec((None, D, tn), lambda i, j: (l, 0, j))] + c_in,
        out_specs=[pl.BlockSpec((tm, tn), lambda i, j: (i, j)),
                   pl.BlockSpec((tm, D), lambda i, j: (i, 0))] + c_out,
        out_shape=[jax.ShapeDtypeStruct((S, N), F32), jax.ShapeDtypeStruct((S, D), BF16)] + c_shapes,
        scratch_shapes=[pltpu.VMEM((tm, D), BF16)] + c_scr,
        compiler_params=_params(*(sem or ("parallel", "arbitrary"))),
    )(x, g, w_all, *(comm.ins if comm else []))
    return outs[0], outs[1], outs[2:]


def _branch_a_fwd(z, lng, lnb, t_mix, bias_full):
    S = z.shape[0]
    D = lng.shape[1]
    GD = D // GROUPS
    ts = _tile(S, 256)

    def body(zu_ref, zv_ref, zg_ref, lng_ref, lnb_ref, t_ref, bias_ref, a_ref, v_scr):
        def chunk(ci, carry):
            c0 = pl.multiple_of(ci * CHUNK, CHUNK)

            def slab(si, c):
                r = pl.multiple_of(si * 16, 16)
                vg = _gelu(zv_ref[pl.ds(pl.multiple_of(c0 + r, 16), 16), :])
                xc = vg - _mean(vg)
                y = xc * lax.rsqrt(_mean(xc * xc) + LN_EPS)
                v_scr[pl.ds(r, 16), :] = (y * lng_ref[...] + lnb_ref[...]).astype(BF16)
                return c
            _grouped_loop(CHUNK // 16, 4, slab, 0)
            for g in range(GROUPS):
                cols = slice(g * GD, (g + 1) * GD)
                sv = _dot(t_ref[g], v_scr[:, cols]) + bias_ref[:, cols]
                u = _gelu(zu_ref[pl.ds(c0, CHUNK), cols])
                a_ref[pl.ds(c0, CHUNK), cols] = ((u * sv) * _silu(zg_ref[pl.ds(c0, CHUNK), cols])).astype(BF16)
            return carry
        lax.fori_loop(0, ts // CHUNK, chunk, 0)

    return pl.pallas_call(
        body, name="branch_a_fwd", grid=(S // ts,),
        in_specs=[_zcol(ts, D, 0), _zcol(ts, D, 1), _zcol(ts, D, 2), _row1(D), _row1(D),
                  pl.BlockSpec((GROUPS, CHUNK, CHUNK), lambda i: (0, 0, 0)),
                  pl.BlockSpec((CHUNK, D), lambda i: (0, 0))],
        out_specs=_tok(ts, D),
        out_shape=jax.ShapeDtypeStruct((S, D), BF16),
        scratch_shapes=[pltpu.VMEM((CHUNK, D), BF16)],
        compiler_params=_params("parallel"),
    )(z, z, z, lng, lnb, t_mix, bias_full)


SH = HALO - CONV_K + 1


def _fill_shifts(sh_scr, src_scr, cols, ts):
    for j in range(1, 8):
        for g in range(ts // 32):
            sh_scr[j - 1, pl.ds(g * 32, 32), :] = src_scr[pl.ds(g * 32 + j, 32), cols]
        sh_scr[j - 1, pl.ds(ts, HALO - 8), :] = src_scr[pl.ds(ts + j, HALO - 8), cols]


def _shifted(sh_scr, src_scr, cols, row0, n):
    q, j = divmod(row0, 8)
    if j == 0:
        return src_scr[pl.ds(row0, n), cols]
    return sh_scr[j - 1, pl.ds(8 * q, n), :]


def _conv_fwd(z, w_all, cb, l, comm=None):
    S = z.shape[0]
    D = cb.shape[1]
    ts = _tile(S, 256)
    CB = 256
    hb = ts // HALO

    def body(za_ref, zb_ref, zap_ref, zbp_ref, w_ref, b_ref, c0_ref, glu_scr, sh_scr):
        i = pl.program_id(0)
        prev = zap_ref[...] * _sigmoid(zbp_ref[...])
        glu_scr[0:HALO, :] = jnp.where(i == 0, 0.0, prev)

        def fill(si, c):
            r = _rows(si, 32)
            glu_scr[pl.ds(pl.multiple_of(HALO + si * 32, 32), 32), :] = za_ref[r, :] * _sigmoid(zb_ref[r, :])
            return c
        _grouped_loop(ts // 32, 2, fill, 0)

        def colblock(ci, c):
            cols = _rows(ci, CB)
            _fill_shifts(sh_scr, glu_scr, cols, ts)
            for s in range(ts // 32):
                acc = jnp.zeros((32, CB), F32)
                for k in range(CONV_K):
                    acc = acc + _shifted(sh_scr, glu_scr, cols, s * 32 + k + SH, 32) * w_ref[k:k + 1, cols]
                c0_ref[pl.ds(s * 32, 32), cols] = acc + b_ref[:, cols]
            return c
        lax.fori_loop(0, D // CB, colblock, 0)

    prev_map = lambda n: (lambda i: (jnp.maximum(i * hb - 1, 0), n))
    grid = (S // ts,)
    body, c_in, c_out, c_shapes, c_scr, sem = _host(comm, 6, 1, 2, grid, body)
    outs = pl.pallas_call(
        body, name="conv_fwd", grid=grid,
        in_specs=[_zcol(ts, D, 3), _zcol(ts, D, 4),
                  pl.BlockSpec((HALO, D), prev_map(3)), pl.BlockSpec((HALO, D), prev_map(4)),
                  pl.BlockSpec((None, CONV_K, D), lambda i: (l, 0, 0)), _row1(D)] + c_in,
        out_specs=[_tok(ts, D)] + c_out,
        out_shape=[jax.ShapeDtypeStruct((S, D), F32)] + c_shapes,
        scratch_shapes=[pltpu.VMEM((ts + HALO, D), F32), pltpu.VMEM((7, ts + HALO, CB), F32)] + c_scr,
        compiler_params=_params(*(sem or ("parallel",))),
    )(z, z, z, z, w_all, cb, *(comm.ins if comm else []))
    return outs[0], outs[1:]


def _branch_b_fwd(c0, z, g, b):
    S, D = c0.shape
    ts = _tile(S, 512)

    def body(c0_ref, zg_ref, g_ref, b_ref, o_ref):
        def slab(si, c):
            r = _rows(si, 16)
            v = c0_ref[r, :]
            xc = v - _mean(v)
            c1 = (xc * lax.rsqrt(_mean(xc * xc) + LN_EPS)) * g_ref[...] + b_ref[...]
            o_ref[r, :] = (_silu(c1) * _silu(zg_ref[r, :])).astype(BF16)
            return c
        _grouped_loop(ts // 16, 4, slab, 0)

    return pl.pallas_call(
        body, name="branch_b_fwd", grid=(S // ts,),
        in_specs=[_tok(ts, D), _zcol(ts, D, 5), _row1(D), _row1(D)],
        out_specs=_tok(ts, D),
        out_shape=jax.ShapeDtypeStruct((S, D), BF16),
        compiler_params=_params("parallel"),
    )(c0, z, g, b)


def _kv_fwd(mem, mg, wkv_all, l):
    M, D = mem.shape

    def body(mem_ref, g_ref, w_ref, mn_ref, kv_ref, kvt_ref):
        m = mem_ref[...]
        mn = ((m * lax.rsqrt(_mean(m * m) + RMS_EPS)) * g_ref[...]).astype(BF16)
        mn_ref[...] = mn
        kv = _dot(mn, w_ref[...])
        kv_ref[...] = kv.astype(BF16)
        kvt_ref[...] = kv.T.astype(BF16)

    return pl.pallas_call(
        body, name="kv_fwd", grid=(1,),
        in_specs=[pl.BlockSpec((M, D), lambda i: (0, 0)), _row1(D),
                  pl.BlockSpec((None, D, 2 * D), lambda i: (l, 0, 0))],
        out_specs=[pl.BlockSpec((M, D), lambda i: (0, 0)), pl.BlockSpec((M, 2 * D), lambda i: (0, 0)),
                   pl.BlockSpec((2 * D, M), lambda i: (0, 0))],
        out_shape=[jax.ShapeDtypeStruct((M, D), BF16), jax.ShapeDtypeStruct((M, 2 * D), BF16),
                   jax.ShapeDtypeStruct((2 * D, M), BF16)],
        compiler_params=_params("arbitrary"),
    )(mem, mg, wkv_all)


def _softmax_rows(s):
    e = jnp.exp(s - jnp.max(s, axis=-1, keepdims=True))
    return e / jnp.sum(e, axis=-1, keepdims=True)


def _branch_c_fwd(z, kv, kvt):
    S = z.shape[0]
    M, D2 = kv.shape
    D = D2 // 2
    HD = D // HEADS
    scale = 1.0 / math.sqrt(HD)
    ts = _tile(S, 256)

    def body(q_ref, zg_ref, kt_ref, v_ref, o_ref):
        for h in range(HEADS):
            cols = slice(h * HD, (h + 1) * HD)
            p = _softmax_rows(_dot(q_ref[:, cols].astype(BF16), kt_ref[cols, :]) * scale)
            att = _dot(p.astype(BF16), v_ref[:, cols])
            o_ref[:, cols] = (att * _silu(zg_ref[:, cols])).astype(BF16)

    return pl.pallas_call(
        body, name="branch_c_fwd", grid=(S // ts,),
        in_specs=[_zcol(ts, D, 6), _zcol(ts, D, 7),
                  pl.BlockSpec((D, M), lambda i: (0, 0)), pl.BlockSpec((M, D), lambda i: (0, 1))],
        out_specs=_tok(ts, D),
        out_shape=jax.ShapeDtypeStruct((S, D), BF16),
        compiler_params=_params("parallel"),
    )(z, z, kvt, kv)


def _merge_fwd(bra, brb, brc, z, x, wb_all, wo_all, l):
    S, D = x.shape
    ts = _tile(S, 512)
    CB = 256

    def body(a_ref, b_ref, c_ref, zm0_ref, zm1_ref, zm2_ref, x_ref, wb_ref, wo_ref, m_ref, xo_ref):
        brs = (a_ref, b_ref, c_ref)
        zms = (zm0_ref, zm1_ref, zm2_ref)
        for cb in range(D // CB):
            cols = slice(cb * CB, (cb + 1) * CB)
            acc = None
            for n in range(N_BRANCHES):
                t = _sigmoid(zms[n][:, cols]) * _dot(brs[n][...], wb_ref[n, :, cols])
                acc = t if acc is None else acc + t
            m_ref[:, cols] = acc.astype(BF16)
        for cb in range(D // CB):
            cols = slice(cb * CB, (cb + 1) * CB)
            xo_ref[:, cols] = x_ref[:, cols] + _dot(m_ref[...], wo_ref[:, cols])

    return pl.pallas_call(
        body, name="merge_fwd", grid=(S // ts,),
        in_specs=[_tok(ts, D), _tok(ts, D), _tok(ts, D), _zcol(ts, D, 8), _zcol(ts, D, 9), _zcol(ts, D, 10),
                  _tok(ts, D),
                  pl.BlockSpec((None, N_BRANCHES, D, D), lambda i: (l, 0, 0, 0)),
                  pl.BlockSpec((None, D, D), lambda i: (l, 0, 0))],
        out_specs=[_tok(ts, D), _tok(ts, D)],
        out_shape=[jax.ShapeDtypeStruct((S, D), BF16), jax.ShapeDtypeStruct((S, D), F32)],
        compiler_params=_params("parallel"),
    )(bra, brb, brc, z, z, z, x, wb_all, wo_all)


def _loss_bwd(x, target, fg):
    S, D = x.shape
    ts = _tile(S, 512)

    def body(x_ref, t_ref, g_ref, dx_ref, ls_ref, dg_ref):
        @pl.when(pl.program_id(0) == 0)
        def _():
            ls_ref[...] = jnp.zeros_like(ls_ref)
            dg_ref[...] = jnp.zeros_like(dg_ref)

        def slab(si, c):
            ls, dg = c
            r = _rows(si, 16)
            xs = x_ref[r, :]
            rs = lax.rsqrt(_mean(xs * xs) + RMS_EPS)
            n = xs * rs
            e = n * g_ref[...] - t_ref[r, :]
            dy = e * (1.0 / D)
            dn = dy * g_ref[...]
            dx_ref[r, :] = rs * (dn - n * _mean(dn * n))
            return ls + _colsum(e * e), dg + _colsum(dy * n)
        zero = jnp.zeros((1, D), F32)
        ls, dg = _grouped_loop(ts // 16, 4, slab, (zero, zero))
        ls_ref[...] += ls
        dg_ref[...] += dg

    return pl.pallas_call(
        body, name="loss_bwd", grid=(S // ts,),
        in_specs=[_tok(ts, D), _tok(ts, D), _row1(D)],
        out_specs=[_tok(ts, D), _row1(D), _row1(D)],
        out_shape=[jax.ShapeDtypeStruct((S, D), F32), jax.ShapeDtypeStruct((1, D), F32),
                   jax.ShapeDtypeStruct((1, D), F32)],
        compiler_params=_params("arbitrary"),
    )(x, target, fg)


def _merge_bwd(dxp, bra, brb, brc, z, wb_all, wo_all, l, comm=None):
    S, D = dxp.shape
    ts = _tile(S, 256)
    CB = 256

    def body(d_ref, a_ref, b_ref, c_ref, zm0_ref, zm1_ref, zm2_ref, wb_ref, wo_ref,
             dzm_ref, dproj_ref, dbr_ref, dxb_scr):
        brs = (a_ref, b_ref, c_ref)
        zms = (zm0_ref, zm1_ref, zm2_ref)
        dxb_scr[...] = d_ref[...].astype(BF16)
        for cb in range(D // CB):
            cols = slice(cb * CB, (cb + 1) * CB)
            dm = _dot_tb(dxb_scr[...], wo_ref[cols, :])
            for n in range(N_BRANCHES):
                proj = _dot(brs[n][...], wb_ref[n, :, cols])
                g = _sigmoid(zms[n][:, cols])
                dzm_ref[:, n * D + cb * CB:n * D + (cb + 1) * CB] = ((dm * proj) * (g * (1.0 - g))).astype(BF16)
                dproj_ref[n, :, cols] = (dm * g).astype(BF16)
        for n in range(N_BRANCHES):
            for cb in range(D // CB):
                cols = slice(cb * CB, (cb + 1) * CB)
                dbr_ref[n, :, cols] = _dot_tb(dproj_ref[n], wb_ref[n, cols, :])

    tok3 = lambda: pl.BlockSpec((N_BRANCHES, ts, D), lambda i: (0, i, 0))
    grid = (S // ts,)
    body, c_in, c_out, c_shapes, c_scr, sem = _host(comm, 9, 3, 1, grid, body)
    outs = pl.pallas_call(
        body, name="merge_bwd", grid=grid,
        in_specs=[_tok(ts, D), _tok(ts, D), _tok(ts, D), _tok(ts, D),
                  _zcol(ts, D, 8), _zcol(ts, D, 9), _zcol(ts, D, 10),
                  pl.BlockSpec((None, N_BRANCHES, D, D), lambda i: (l, 0, 0, 0)),
                  pl.BlockSpec((None, D, D), lambda i: (l, 0, 0))] + c_in,
        out_specs=[_tok(ts, 3 * D), tok3(), tok3()] + c_out,
        out_shape=[jax.ShapeDtypeStruct((S, 3 * D), BF16), jax.ShapeDtypeStruct((N_BRANCHES, S, D), BF16),
                   jax.ShapeDtypeStruct((N_BRANCHES, S, D), F32)] + c_shapes,
        scratch_shapes=[pltpu.VMEM((ts, D), BF16)] + c_scr,
        compiler_params=_params(*(sem or ("parallel",))),
    )(dxp, bra, brb, brc, z, z, z, wb_all, wo_all, *(comm.ins if comm else []))
    return outs[0], outs[1], outs[2], outs[3:]


def _branch_a_bwd(z, dbr, lng, lnb, t_mix, t_mix_t, bias_full, mask):
    S = z.shape[0]
    D = lng.shape[1]
    GD = D // GROUPS
    ts = _tile(S, 256)
    nt = S // ts

    def body(zu_ref, zv_ref, zg_ref, d_ref, lng_ref, lnb_ref, t_ref, tt_ref, bias_ref, mask_ref,
             dz_ref, dws_ref, dbs_ref, dlg_ref, dlb_ref,
             v_scr, vn_scr, rstd_scr, dv_scr, dsv_acc):
        i = pl.program_id(0)

        @pl.when(i == 0)
        def _():
            dws_ref[...] = jnp.zeros_like(dws_ref)
            dlg_ref[...] = jnp.zeros_like(dlg_ref)
            dlb_ref[...] = jnp.zeros_like(dlb_ref)
            dsv_acc[...] = jnp.zeros_like(dsv_acc)

        def chunk(ci, carry):
            c0 = pl.multiple_of(ci * CHUNK, CHUNK)

            def ln_fwd(si, c):
                r = pl.multiple_of(si * 16, 16)
                vg = _gelu(zv_ref[pl.ds(pl.multiple_of(c0 + r, 16), 16), :])
                xc = vg - _mean(vg)
                rstd = lax.rsqrt(_mean(xc * xc) + LN_EPS)
                vn = xc * rstd
                vn_scr[pl.ds(r, 16), :] = vn
                rstd_scr[pl.ds(r, 16), :] = rstd
                v_scr[pl.ds(r, 16), :] = (vn * lng_ref[...] + lnb_ref[...]).astype(BF16)
                return c
            _grouped_loop(CHUNK // 16, 4, ln_fwd, 0)

            for g in range(GROUPS):
                cols = slice(g * GD, (g + 1) * GD)
                rows = pl.ds(c0, CHUNK)
                vb = v_scr[:, cols]
                sv = _dot(t_ref[g], vb) + bias_ref[:, cols]
                u, du = _gelu_and_grad(zu_ref[rows, cols])
                sg, dsg = _silu_and_grad(zg_ref[rows, cols])
                d = d_ref[rows, cols]
                dsv = (d * u) * sg
                dz_ref[rows, g * GD:(g + 1) * GD] = (((d * sv) * sg) * du).astype(BF16)
                dz_ref[rows, 2 * D + g * GD:2 * D + (g + 1) * GD] = (((d * u) * sv) * dsg).astype(BF16)
                dsvb = dsv.astype(BF16)
                dws_ref[g] += _dot_tb(dsvb, vb)
                dv_scr[:, cols] = _dot(tt_ref[g], dsvb)
                dsv_acc[:, cols] += dsv

            def ln_bwd(si, c):
                dlg, dlb = c
                r = pl.multiple_of(si * 16, 16)
                rr = pl.ds(r, 16)
                zrows = pl.ds(pl.multiple_of(c0 + r, 16), 16)
                dv = dv_scr[rr, :]
                vn = vn_scr[rr, :]
                dvn = dv * lng_ref[...]
                dvg = rstd_scr[rr, :] * ((dvn - _mean(dvn)) - vn * _mean(dvn * vn))
                _, gg = _gelu_and_grad(zv_ref[zrows, :])
                dz_ref[zrows, D:2 * D] = (dvg * gg).astype(BF16)
                return dlg + _colsum(dv * vn), dlb + _colsum(dv)
            zero = jnp.zeros((1, D), F32)
            dlg, dlb = _grouped_loop(CHUNK // 16, 4, ln_bwd, (zero, zero))
            dlg_ref[...] += dlg
            dlb_ref[...] += dlb
            return carry
        lax.fori_loop(0, ts // CHUNK, chunk, 0)

        @pl.when(i == nt - 1)
        def _():
            for g in range(GROUPS):
                cols = slice(g * GD, (g + 1) * GD)
                dws_ref[g] = dws_ref[g] * mask_ref[...]
                dbs_ref[g:g + 1, :] = _colsum(dsv_acc[:, cols].T)

    c2 = lambda shape: pl.BlockSpec(shape, lambda i: (0, 0))
    c3 = lambda: pl.BlockSpec((GROUPS, CHUNK, CHUNK), lambda i: (0, 0, 0))
    return pl.pallas_call(
        body, name="branch_a_bwd", grid=(nt,),
        in_specs=[_zcol(ts, D, 0), _zcol(ts, D, 1), _zcol(ts, D, 2), _plane(ts, D, 0), _row1(D), _row1(D),
                  c3(), c3(), c2((CHUNK, D)), c2((CHUNK, CHUNK))],
        out_specs=[_tok(ts, 3 * D), c3(), c2((GROUPS, CHUNK)), _row1(D), _row1(D)],
        out_shape=[jax.ShapeDtypeStruct((S, 3 * D), BF16), jax.ShapeDtypeStruct((GROUPS, CHUNK, CHUNK), F32),
                   jax.ShapeDtypeStruct((GROUPS, CHUNK), F32), jax.ShapeDtypeStruct((1, D), F32),
                   jax.ShapeDtypeStruct((1, D), F32)],
        scratch_shapes=[pltpu.VMEM((CHUNK, D), BF16), pltpu.VMEM((CHUNK, D), F32), pltpu.VMEM((CHUNK, 1), F32),
                        pltpu.VMEM((CHUNK, D), F32), pltpu.VMEM((CHUNK, D), F32)],
        compiler_params=_params("arbitrary"),
    )(z, z, z, dbr, lng, lnb, t_mix, t_mix_t, bias_full, mask)


def _branch_b_bwd(c0, z, dbr, g, b):
    S, D = c0.shape
    ts = _tile(S, 512)

    def body(c0_ref, zg_ref, d_ref, g_ref, b_ref, dc0_ref, dz_ref, dg_ref, db_ref, dcb_ref):
        @pl.when(pl.program_id(0) == 0)
        def _():
            dg_ref[...] = jnp.zeros_like(dg_ref)
            db_ref[...] = jnp.zeros_like(db_ref)
            dcb_ref[...] = jnp.zeros_like(dcb_ref)

        def slab(si, c):
            dg, db, dcb = c
            r = _rows(si, 16)
            v = c0_ref[r, :]
            xc = v - _mean(v)
            rstd = lax.rsqrt(_mean(xc * xc) + LN_EPS)
            cn = xc * rstd
            c1 = cn * g_ref[...] + b_ref[...]
            c2, dc2_dc1 = _silu_and_grad(c1)
            sg, dsg = _silu_and_grad(zg_ref[r, :])
            d = d_ref[r, :]
            dz_ref[r, :] = ((d * c2) * dsg).astype(BF16)
            dc1 = (d * sg) * dc2_dc1
            dcn = dc1 * g_ref[...]
            dc0 = rstd * ((dcn - _mean(dcn)) - cn * _mean(dcn * cn))
            dc0_ref[r, :] = dc0
            return dg + _colsum(dc1 * cn), db + _colsum(dc1), dcb + _colsum(dc0)
        zero = jnp.zeros((1, D), F32)
        dg, db, dcb = _grouped_loop(ts // 16, 4, slab, (zero, zero, zero))
        dg_ref[...] += dg
        db_ref[...] += db
        dcb_ref[...] += dcb

    return pl.pallas_call(
        body, name="branch_b_bwd", grid=(S // ts,),
        in_specs=[_tok(ts, D), _zcol(ts, D, 5), _plane(ts, D, 1), _row1(D), _row1(D)],
        out_specs=[_tok(ts, D), _zcol(ts, D, 2), _row1(D), _row1(D), _row1(D)],
        out_shape=[jax.ShapeDtypeStruct((S, D), F32), jax.ShapeDtypeStruct((S, 3 * D), BF16),
                   jax.ShapeDtypeStruct((1, D), F32), jax.ShapeDtypeStruct((1, D), F32),
                   jax.ShapeDtypeStruct((1, D), F32)],
        compiler_params=_params("arbitrary"),
    )(c0, z, dbr, g, b)


def _conv_bwd(dc0, z, w_all, dz_b, l, comm=None):
    S, D = dc0.shape
    ts = _tile(S, 256)
    nt = S // ts
    CB = 256
    hb = ts // HALO
    nh = S // HALO

    def body(dc_ref, dcn_ref, za_ref, zb_ref, w_ref, dzin_ref, dz_ref, dw_ref, dc_scr, dw_acc, dsh_scr):
        del dzin_ref
        i = pl.program_id(0)

        @pl.when(i == 0)
        def _():
            dw_acc[...] = jnp.zeros_like(dw_acc)

        dc_scr[ts:ts + HALO, :] = jnp.where(i == nt - 1, 0.0, dcn_ref[...])

        def fill(si, c):
            r = _rows(si, 32)
            dc_scr[r, :] = dc_ref[r, :]
            return c
        _grouped_loop(ts // 32, 2, fill, 0)

        def colblock(ci, c):
            cols = _rows(ci, CB)
            cols_b = pl.ds(pl.multiple_of(D + ci * CB, CB), CB)
            _fill_shifts(dsh_scr, dc_scr, cols, ts)
            for s in range(ts // 32):
                rows = pl.ds(s * 32, 32)
                za = za_ref[rows, cols]
                sg = _sigmoid(zb_ref[rows, cols])
                glu = za * sg
                acc = jnp.zeros((32, CB), F32)
                for k in range(CONV_K):
                    dcs = _shifted(dsh_scr, dc_scr, cols, s * 32 + CONV_K - 1 - k, 32)
                    prod = glu * dcs
                    dw_acc[k, :, cols] += (prod[0:8] + prod[8:16]) + (prod[16:24] + prod[24:32])
                    acc = acc + dcs * w_ref[k:k + 1, cols]
                dz_ref[rows, cols] = (acc * sg).astype(BF16)
                dz_ref[rows, cols_b] = ((acc * za) * (sg * (1.0 - sg))).astype(BF16)
            return c
        lax.fori_loop(0, D // CB, colblock, 0)

        @pl.when(i == nt - 1)
        def _():
            for k in range(CONV_K):
                dw_ref[k:k + 1, :] = _colsum(dw_acc[k])

    grid = (nt,)
    body, c_in, c_out, c_shapes, c_scr, _ = _host(comm, 6, 2, 3, grid, body)
    outs = pl.pallas_call(
        body, name="conv_bwd", grid=grid,
        in_specs=[_tok(ts, D), pl.BlockSpec((HALO, D), lambda i: (jnp.minimum((i + 1) * hb, nh - 1), 0)),
                  _zcol(ts, D, 3), _zcol(ts, D, 4),
                  pl.BlockSpec((None, CONV_K, D), lambda i: (l, 0, 0)), ANY] + c_in,
        out_specs=[_tok(ts, 2 * D), pl.BlockSpec((CONV_K, D), lambda i: (0, 0))] + c_out,
        out_shape=[jax.ShapeDtypeStruct((S, 3 * D), BF16), jax.ShapeDtypeStruct((CONV_K, D), F32)] + c_shapes,
        scratch_shapes=[pltpu.VMEM((ts + HALO, D), F32), pltpu.VMEM((CONV_K, 8, D), F32),
                        pltpu.VMEM((7, ts + HALO, CB), F32)] + c_scr,
        input_output_aliases={5: 0},
        compiler_params=_params("arbitrary"),
    )(dc0, dc0, z, z, w_all, dz_b, *(comm.ins if comm else []))
    return outs[0], outs[1], outs[2:]


def _branch_c_bwd(z, dbr, kv, kvt):
    S = z.shape[0]
    M, D2 = kv.shape
    D = D2 // 2
    HD = D // HEADS
    scale = 1.0 / math.sqrt(HD)
    ts = _tile(S, 256)

    def body(q_ref, zg_ref, d_ref, kt_ref, vt_ref, k_ref, v_ref, dz_ref, dk_ref, dv_ref):
        @pl.when(pl.program_id(0) == 0)
        def _():
            dk_ref[...] = jnp.zeros_like(dk_ref)
            dv_ref[...] = jnp.zeros_like(dv_ref)

        for h in range(HEADS):
            cols = slice(h * HD, (h + 1) * HD)
            qb = q_ref[:, cols].astype(BF16)
            p = _softmax_rows(_dot(qb, kt_ref[cols, :]) * scale)
            pb = p.astype(BF16)
            att = _dot(pb, v_ref[:, cols])
            sg, dsg = _silu_and_grad(zg_ref[:, cols])
            d = d_ref[:, cols]
            dz_ref[:, D + h * HD:D + (h + 1) * HD] = ((d * att) * dsg).astype(BF16)
            datt = (d * sg).astype(BF16)
            dp = _dot(datt, vt_ref[cols, :])
            dv_ref[:, cols] += _dot_ta(pb, datt)
            ds = ((p * (dp - jnp.sum(dp * p, axis=-1, keepdims=True))) * scale).astype(BF16)
            dz_ref[:, cols] = _dot(ds, k_ref[:, cols]).astype(BF16)
            dk_ref[:, cols] += _dot_ta(ds, qb)

    return pl.pallas_call(
        body, name="branch_c_bwd", grid=(S // ts,),
        in_specs=[_zcol(ts, D, 6), _zcol(ts, D, 7), _plane(ts, D, 2),
                  pl.BlockSpec((D, M), lambda i: (0, 0)), pl.BlockSpec((D, M), lambda i: (1, 0)),
                  pl.BlockSpec((M, D), lambda i: (0, 0)), pl.BlockSpec((M, D), lambda i: (0, 1))],
        out_specs=[_tok(ts, 2 * D), pl.BlockSpec((M, D), lambda i: (0, 0)), pl.BlockSpec((M, D), lambda i: (0, 0))],
        out_shape=[jax.ShapeDtypeStruct((S, 2 * D), BF16), jax.ShapeDtypeStruct((M, D), F32),
                   jax.ShapeDtypeStruct((M, D), F32)],
        compiler_params=_params("arbitrary"),
    )(z, z, dbr, kvt, kvt, kv, kv)


def _kv_bwd(dk, dv, mn, mem, wkv_all, l, prev):
    M, D = mem.shape

    def body(*refs):
        dk_ref, dv_ref, mn_ref, mem_ref, w_ref = refs[:5]
        dw_ref, dg_ref = refs[-2:]
        dkb = dk_ref[...].astype(BF16)
        dvb = dv_ref[...].astype(BF16)
        dw_ref[:, 0:D] = _dot_ta(mn_ref[...], dkb)
        dw_ref[:, D:2 * D] = _dot_ta(mn_ref[...], dvb)
        dmn = _dot_tb(dkb, w_ref[:, 0:D]) + _dot_tb(dvb, w_ref[:, D:2 * D])
        m = mem_ref[...]
        dg_ref[...] = _colsum(dmn * (m * lax.rsqrt(_mean(m * m) + RMS_EPS)))

    full = lambda shape: pl.BlockSpec(shape, lambda i: (0, 0))
    in_specs = [full((M, D)), full((M, D)), full((M, D)), full((M, D)),
                pl.BlockSpec((None, D, 2 * D), lambda i: (l, 0, 0))]
    args = [dk, dv, mn, mem, wkv_all]
    aliases = {}
    if prev is not None:
        in_specs.append(ANY)
        args.append(prev)
        aliases = {5: 0}
    return pl.pallas_call(
        body, name="kv_bwd", grid=(1,),
        in_specs=in_specs,
        out_specs=[pl.BlockSpec((None, D, 2 * D), lambda i: (l, 0, 0)), _row1(D)],
        out_shape=[jax.ShapeDtypeStruct((wkv_all.shape[0], D, 2 * D), F32), jax.ShapeDtypeStruct((1, D), F32)],
        input_output_aliases=aliases,
        compiler_params=_params("arbitrary"),
    )(*args)


def _in_proj_bwd(pieces, w_all, x, g, dxp, l, comm=None):
    S, D = x.shape
    tm = _tile(S, 1024)
    n_p = len(pieces)

    def body(*refs):
        dz_refs = refs[:n_p]
        w_ref, x_hbm, g_ref, dxp_hbm, dx_ref, dg_ref, acc, x_ref, dxp_ref, sem = refs[n_p:]
        i = pl.program_id(0)
        k = pl.program_id(1)
        rows = pl.ds(pl.multiple_of(i * tm, tm), tm)
        fetch = [pltpu.make_async_copy(x_hbm.at[rows, :], x_ref, sem.at[0]),
                 pltpu.make_async_copy(dxp_hbm.at[rows, :], dxp_ref, sem.at[1])]

        @pl.when(k == N_PLANES - 2)
        def _():
            for cp in fetch:
                cp.start()

        @pl.when(k == 0)
        def _():
            acc[...] = jnp.zeros_like(acc)

        @pl.when((i == 0) & (k == 0))
        def _():
            dg_ref[...] = jnp.zeros_like(dg_ref)

        for (_, lo, n), r in zip(pieces, dz_refs):
            @pl.when((k >= lo) & (k < lo + n))
            def _(r=r):
                acc[...] += _dot_tb(r[...], w_ref[...])

        @pl.when(k == N_PLANES - 1)
        def _():
            for cp in fetch:
                cp.wait()

            def slab(si, c):
                rr = _rows(si, 16)
                xs = x_ref[rr, :]
                rs = lax.rsqrt(_mean(xs * xs) + RMS_EPS)
                n = xs * rs
                dh = acc[rr, :]
                dn = dh * g_ref[...]
                dx_ref[rr, :] = rs * (dn - n * _mean(dn * n)) + dxp_ref[rr, :]
                return c + _colsum(dh * n)
            dg_ref[...] += _grouped_loop(tm // 16, 4, slab, jnp.zeros((1, D), F32))

    def piece_spec(lo, n):
        return pl.BlockSpec((tm, D), lambda i, k: (i, jnp.clip(k - lo, 0, n - 1)))

    tokk = lambda: pl.BlockSpec((tm, D), lambda i, k: (i, 0))
    rowk = lambda: pl.BlockSpec((1, D), lambda i, k: (0, 0))
    grid = (S // tm, N_PLANES)
    body, c_in, c_out, c_shapes, c_scr, _ = _host(comm, n_p + 4, 2, 4, grid, body)
    outs = pl.pallas_call(
        body, name="in_proj_bwd", grid=grid,
        in_specs=[piece_spec(lo, n) for _, lo, n in pieces]
        + [pl.BlockSpec((None, D, D), lambda i, k: (l, 0, k)), ANY, rowk(), ANY] + c_in,
        out_specs=[tokk(), rowk()] + c_out,
        out_shape=[jax.ShapeDtypeStruct((S, D), F32), jax.ShapeDtypeStruct((1, D), F32)] + c_shapes,
        scratch_shapes=[pltpu.VMEM((tm, D), F32), pltpu.VMEM((tm, D), F32), pltpu.VMEM((tm, D), F32),
                        pltpu.SemaphoreType.DMA((2,))] + c_scr,
        compiler_params=_params("arbitrary", "arbitrary"),
    )(*[p for p, _, _ in pieces], w_all, x, g, dxp, *(comm.ins if comm else []))
    return outs[0], outs[1], outs[2:]


def _wgrad(a, b, out_shape, plane, col_off, prev, a_plane=None, b_plane=None):
    S = a.shape[-2]
    K1 = a.shape[-1]
    nb_cols = b.shape[-1]
    ts = _tile(S, 2048)
    tn = _tile(nb_cols, 1024)
    col_off = col_off // tn

    def body(*refs):
        a_ref, b_ref = refs[:2]
        o_ref = refs[-1]
        k = pl.program_id(1)
        prod = _dot_ta(a_ref[...].astype(BF16), b_ref[...].astype(BF16))

        @pl.when(k == 0)
        def _():
            o_ref[...] = prod

        @pl.when(k > 0)
        def _():
            o_ref[...] += prod

    if a_plane is None:
        a_spec = pl.BlockSpec((ts, K1), lambda j, k: (k, 0))
    else:
        a_spec = pl.BlockSpec((None, ts, K1), lambda j, k: (a_plane, k, 0))
    if b_plane is None:
        b_spec = pl.BlockSpec((ts, tn), lambda j, k: (k, j))
    else:
        b_spec = pl.BlockSpec((None, ts, tn), lambda j, k: (b_plane, k, j))
    in_specs, args, aliases = [a_spec, b_spec], [a, b], {}
    if prev is not None:
        in_specs.append(ANY)
        args.append(prev)
        aliases = {2: 0}
    return pl.pallas_call(
        body, name="wgrad", grid=(nb_cols // tn, S // ts),
        in_specs=in_specs,
        out_specs=pl.BlockSpec((None, K1, tn), lambda j, k: (plane, 0, col_off + j)),
        out_shape=jax.ShapeDtypeStruct(out_shape, F32),
        input_output_aliases=aliases,
        compiler_params=_params("parallel", "arbitrary"),
    )(*args)


def _position():
    x, y, c = lax.axis_index("x"), lax.axis_index("y"), lax.axis_index("c")
    chips = [(1 - x, y), (x, 1 - y), (1 - x, 1 - y)]
    return x, y, c, chips


def _sub(ref, axis, start, size):
    idx = [slice(None)] * len(ref.shape)
    idx[axis] = pl.ds(start, size)
    return ref.at[tuple(idx)]


def _dma(src, dst, ssem, rsem, to):
    return pltpu.make_async_remote_copy(src_ref=src, dst_ref=dst, send_sem=ssem, recv_sem=rsem,
                                        device_id=to, device_id_type=MESH)


def _dma_sems(*counts):
    return [pltpu.SemaphoreType.DMA((n,)) for n in counts]


def _run_alone(comm, name):
    ci, co = len(comm.ins), len(comm.out_shapes)

    def body(*refs):
        for _, fn in comm.phases:
            fn(refs[:ci], refs[ci:ci + co], refs[ci + co:])

    return pl.pallas_call(
        body, name=name, in_specs=[ANY] * ci, out_specs=[ANY] * co, out_shape=list(comm.out_shapes),
        scratch_shapes=list(comm.sem_shapes),
    )(*comm.ins)


def _gather_comm(shards, layers, axes):
    n = len(shards)
    out_shapes = []
    for s, (ba, _) in zip(shards, axes):
        shp = list(s.shape[1:])
        shp[ba] *= N_CHIPS
        out_shapes.append(jax.ShapeDtypeStruct(tuple(shp), s.dtype))

    def half(ref, t, h):
        ha = axes[t][1]
        if ha is None:
            return ref
        hs = shards[t].shape[1 + ha] // 2
        return _sub(ref, ha, h * hs, hs)

    def block(outs, t, blk):
        ba = axes[t][0]
        bs = shards[t].shape[1 + ba]
        return _sub(outs[t], ba, blk * bs, bs)

    def start(ins, outs, sems):
        s_own, r_own, s_ici, r_ici, _, _ = sems
        x, y, c, chips = _position()
        j = 2 * x + y
        for t in range(n):
            _dma(ins[t].at[layers[t]], block(outs, t, j), s_own.at[t], r_own.at[t], (x, y, 1 - c)).start()
            for k, (px, py) in enumerate(chips):
                _dma(half(ins[t].at[layers[t]], t, c), half(block(outs, t, j), t, c),
                     s_ici.at[3 * t + k], r_ici.at[3 * t + k], (px, py, c)).start()

    def forward(ins, outs, sems):
        _, _, s_ici, r_ici, s_fwd, r_fwd = sems
        x, y, c, chips = _position()
        for t in range(n):
            if axes[t][1] is None:
                continue
            for k, (px, py) in enumerate(chips):
                got = half(block(outs, t, 2 * px + py), t, c)
                _dma(got, got, s_ici.at[3 * t + k], r_ici.at[3 * t + k], (px, py, c)).wait_recv()
                _dma(got, got, s_fwd.at[3 * t + k], r_fwd.at[3 * t + k], (x, y, 1 - c)).start()

    def finish(ins, outs, sems):
        s_own, r_own, s_ici, r_ici, s_fwd, r_fwd = sems
        x, y, c, chips = _position()
        j = 2 * x + y
        sib = (x, y, 1 - c)
        for t in range(n):
            for k, (px, py) in enumerate(chips):
                i = 3 * t + k
                sent = half(block(outs, t, j), t, c)
                _dma(sent, sent, s_ici.at[i], r_ici.at[i], (px, py, c)).wait_send()
                theirs = block(outs, t, 2 * px + py)
                if axes[t][1] is None:
                    _dma(theirs, theirs, s_ici.at[i], r_ici.at[i], (px, py, c)).wait_recv()
                else:
                    got, other = half(theirs, t, c), half(theirs, t, 1 - c)
                    _dma(got, got, s_fwd.at[i], r_fwd.at[i], sib).wait_send()
                    _dma(other, other, s_fwd.at[i], r_fwd.at[i], sib).wait_recv()
            _dma(ins[t].at[layers[t]], block(outs, t, j), s_own.at[t], r_own.at[t], sib).wait()

    return _Hosted(list(shards), out_shapes, _dma_sems(n, n, 3 * n, 3 * n, 3 * n, 3 * n),
                   [("first", start), ("late", forward), ("last", finish)])


def _halved(shape, axis):
    shp = list(shape)
    shp[axis] //= 2
    return tuple(shp)


def _pair_comm(grads, axes):
    n = len(grads)

    def copies(ins, outs, sems):
        x, y, c, _ = _position()
        cps = []
        for t in range(n):
            ha = axes[t][1]
            hs = grads[t].shape[ha] // 2
            cps.append(_dma(_sub(ins[t], ha, (1 - c) * hs, hs), outs[t], sems[0].at[t], sems[1].at[t], (x, y, 1 - c)))
        return cps

    def start(ins, outs, sems):
        for cp in copies(ins, outs, sems):
            cp.start()

    def finish(ins, outs, sems):
        for cp in copies(ins, outs, sems):
            cp.wait()

    return _Hosted(list(grads), [jax.ShapeDtypeStruct(_halved(g.shape, ax[1]), g.dtype) for g, ax in zip(grads, axes)],
                   _dma_sems(n, n), [("first", start), ("last", finish)])


def _block_comm(sums, axes):
    n = len(sums)
    out_shapes = []
    for s, (ba, _) in zip(sums, axes):
        shp = list(s.shape)
        shp[ba] //= N_CHIPS
        out_shapes.append(jax.ShapeDtypeStruct((3,) + tuple(shp), s.dtype))

    def copies(ins, outs, sems):
        x, y, c, chips = _position()
        cps = []
        for t in range(n):
            ba = axes[t][0]
            bs = sums[t].shape[ba] // N_CHIPS
            for k, (px, py) in enumerate(chips):
                cps.append(_dma(_sub(ins[t], ba, (2 * px + py) * bs, bs), outs[t].at[k],
                                sems[0].at[3 * t + k], sems[1].at[3 * t + k], (px, py, c)))
        return cps

    def start(ins, outs, sems):
        for cp in copies(ins, outs, sems):
            cp.start()

    def finish(ins, outs, sems):
        for cp in copies(ins, outs, sems):
            cp.wait()

    return _Hosted(list(sums), out_shapes, _dma_sems(3 * n, 3 * n), [("first", start), ("last", finish)])


def _sibling_share(shards, axes):
    n = len(shards)

    def body(*refs):
        outs = refs[n:2 * n]
        ssem, rsem = refs[2 * n:]
        x, y, c, _ = _position()
        cps = []
        for t in range(n):
            ha = axes[t][1]
            hs = shards[t].shape[ha] // 2
            mine = _sub(outs[t], ha, c * hs, hs)
            cps.append(_dma(mine, mine, ssem.at[t], rsem.at[t], (x, y, 1 - c)))
        for cp in cps:
            cp.start()
        for cp in cps:
            cp.wait()

    return pl.pallas_call(
        body, name="sibling_share",
        in_specs=[ANY] * n, out_specs=[ANY] * n,
        out_shape=[jax.ShapeDtypeStruct(s.shape, s.dtype) for s in shards],
        input_output_aliases={t: t for t in range(n)},
        scratch_shapes=_dma_sems(n, n),
    )(*shards)


def _small_allreduce(p):
    R, C = p.shape

    def body(p_ref, o_ref, sib_buf, chip_sums, ssem, rsem):
        x, y, c, chips = _position()
        j = 2 * x + y
        sib = pltpu.make_async_remote_copy(
            src_ref=p_ref, dst_ref=sib_buf, send_sem=ssem.at[0], recv_sem=rsem.at[0],
            device_id=(x, y, 1 - c), device_id_type=MESH)
        sib.start()
        sib.wait()
        chip_sums[j] = p_ref[...] + sib_buf[...]
        cps = [pltpu.make_async_remote_copy(
            src_ref=chip_sums.at[j], dst_ref=chip_sums.at[j], send_sem=ssem.at[1 + k], recv_sem=rsem.at[1 + k],
            device_id=(px, py, c), device_id_type=MESH) for k, (px, py) in enumerate(chips)]
        for cp in cps:
            cp.start()
        for k, (px, py) in enumerate(chips):
            pltpu.make_async_remote_copy(
                src_ref=chip_sums.at[j], dst_ref=chip_sums.at[2 * px + py],
                send_sem=ssem.at[1 + k], recv_sem=rsem.at[1 + k],
                device_id=(px, py, c), device_id_type=MESH).wait()
        o_ref[...] = ((chip_sums[0] + chip_sums[1]) + chip_sums[2]) + chip_sums[3]

    vm = pl.BlockSpec(memory_space=pltpu.VMEM)
    return pl.pallas_call(
        body, name="small_allreduce",
        in_specs=[vm], out_specs=vm, out_shape=jax.ShapeDtypeStruct((R, C), F32),
        scratch_shapes=[pltpu.VMEM((R, C), F32), pltpu.VMEM((N_CHIPS, R, C), F32),
                        pltpu.SemaphoreType.DMA((4,)), pltpu.SemaphoreType.DMA((4,))],
        compiler_params=pltpu.CompilerParams(vmem_limit_bytes=VMEM_LIMIT_BYTES),
    )(p)


def _add_half(g, recv, half_axis, pos):
    B, R, C = recv.shape
    tr = _tile(R, 256)
    tc = _tile(C, 2816)
    nr, nc = R // tr, C // tc
    if half_axis == 1:
        g_map = lambda b, i, j, pos_ref: (b, pos_ref[0] * nr + i, j)
    else:
        g_map = lambda b, i, j, pos_ref: (b, i, pos_ref[0] * nc + j)

    def body(pos_ref, g_ref, r_ref, o_ref, ob_ref):
        del pos_ref
        s = g_ref[...] + r_ref[...]
        o_ref[...] = s
        ob_ref[...] = s.astype(BF16)

    spec = lambda: pl.BlockSpec((None, tr, tc), lambda b, i, j, pos_ref: (b, i, j))
    return pl.pallas_call(
        body, name="add_half",
        grid_spec=pltpu.PrefetchScalarGridSpec(
            num_scalar_prefetch=1, grid=(B, nr, nc),
            in_specs=[pl.BlockSpec((None, tr, tc), g_map), spec()],
            out_specs=[spec(), spec()]),
        out_shape=[jax.ShapeDtypeStruct((B, R, C), F32), jax.ShapeDtypeStruct((B, R, C), BF16)],
        compiler_params=_params("parallel", "parallel", "parallel"),
    )(pos, g, recv)


def _sum_half(own, recv, axes, pos):
    _, B, rs, cs = recv.shape
    ba, ha = axes
    tr = _tile(rs, 128)
    nr = rs // tr

    def body(pos_ref, o_ref, r0_ref, r1_ref, r2_ref, out_ref):
        del pos_ref
        out_ref[...] = (((o_ref[...] + r0_ref[...].astype(F32)) + r1_ref[...].astype(F32))
                        + r2_ref[...].astype(F32))

    if ba == 2:
        own_spec = pl.BlockSpec((None, tr, cs), lambda b, i, pos_ref: (b, i, pos_ref[1]))
    else:
        own_spec = pl.BlockSpec((None, tr, cs), lambda b, i, pos_ref: (b, pos_ref[1] * nr + i, 0))
    if ha == 1:
        out_shape, out_map = (B, 2 * rs, cs), (lambda b, i, pos_ref: (b, pos_ref[0] * nr + i, 0))
    else:
        out_shape, out_map = (B, rs, 2 * cs), (lambda b, i, pos_ref: (b, i, pos_ref[0]))
    rspec = lambda k: pl.BlockSpec((None, None, tr, cs), lambda b, i, pos_ref: (k, b, i, 0))
    return pl.pallas_call(
        body, name="sum_half",
        grid_spec=pltpu.PrefetchScalarGridSpec(
            num_scalar_prefetch=1, grid=(B, nr),
            in_specs=[own_spec, rspec(0), rspec(1), rspec(2)],
            out_specs=pl.BlockSpec((None, tr, cs), out_map)),
        out_shape=jax.ShapeDtypeStruct(out_shape, F32),
        compiler_params=_params("parallel", "parallel"),
    )(pos, own, recv, recv, recv)


def _adamw(w, g, m, v):
    shape = w.shape
    C = shape[-1]
    R = w.size // C
    tr = R
    for cand in (512, 256, 128, 64, 32, 16, 8):
        if R % cand == 0 and cand * C * 4 <= (1 << 20):
            tr = cand
            break
    bc1 = 1.0 - ADAM_B1 ** ADAM_STEP
    bc2 = 1.0 - ADAM_B2 ** ADAM_STEP

    def body(w_ref, g_ref, m_ref, v_ref, d_ref, mo_ref, vo_ref):
        gg = g_ref[...]
        mn = ADAM_B1 * m_ref[...] + (1.0 - ADAM_B1) * gg
        vn = ADAM_B2 * v_ref[...] + (1.0 - ADAM_B2) * (gg * gg)
        mo_ref[...] = mn
        vo_ref[...] = vn
        d_ref[...] = -ADAM_LR * ((mn / bc1) / (jnp.sqrt(vn / bc2) + ADAM_EPS) + ADAM_WD * w_ref[...])

    spec = lambda: pl.BlockSpec((tr, C), lambda i: (i, 0))
    outs = pl.pallas_call(
        body, name="adamw", grid=(R // tr,),
        in_specs=[spec()] * 4, out_specs=[spec()] * 3,
        out_shape=[jax.ShapeDtypeStruct((R, C), F32)] * 3,
        compiler_params=_params("parallel"),
    )(*[a.reshape(R, C) for a in (w, g, m, v)])
    return tuple(o.reshape(shape) for o in outs)


def kernel(x, mem, norm_g, mem_norm_g, w_in, gmlp_ln_g, gmlp_ln_b, w_s, b_s, conv_w, conv_b, conv_ln_g, conv_ln_b, w_kv, w_branch, w_out, final_norm_g, loss_target, m_norm_g, m_mem_norm_g, m_w_in, m_gmlp_ln_g, m_gmlp_ln_b, m_w_s, m_b_s, m_conv_w, m_conv_b, m_conv_ln_g, m_conv_ln_b, m_w_kv, m_w_branch, m_w_out, m_final_norm_g, v_norm_g, v_mem_norm_g, v_w_in, v_gmlp_ln_g, v_gmlp_ln_b, v_w_s, v_b_s, v_conv_w, v_conv_b, v_conv_ln_g, v_conv_ln_b, v_w_kv, v_w_branch, v_w_out, v_final_norm_g):
    xs, mems, tgt = x[0], mem[0], loss_target[0]
    S, D = xs.shape
    assert D // GROUPS == LANES and S % CHUNK == 0 and w_s.shape[-1] == CHUNK
    L = N_LAYERS

    assert L == 2
    shards = [w_in.astype(BF16)[:, None], w_kv.astype(BF16)[:, None], w_branch.astype(BF16), w_out.astype(BF16)[:, None],
              conv_w[:, None]]
    gather_axes = [(2, 1), (2, 1), (1, 2), (1, 2), (2, None)]
    reduce_axes = gather_axes[:4]
    win0 = _run_alone(_gather_comm(shards[:1], [0], gather_axes[:1]), "gather_w_in0")[0]
    gather_early = _gather_comm(shards[1:] + shards[:1], [0] * 4 + [1], gather_axes[1:] + gather_axes[:1])
    gather_late = _gather_comm(shards[1:], [1] * 4, gather_axes[1:])

    tri = jnp.tril(jnp.ones((CHUNK, CHUNK), F32))
    t_mix = (w_s * tri).astype(BF16)
    t_mix_t = jnp.swapaxes(t_mix, -1, -2)
    bias_full = jnp.repeat(jnp.swapaxes(b_s, -1, -2), D // GROUPS, axis=-1)
    row = lambda a, l: a[l][None, :]
    xi, yi, ci = lax.axis_index("x"), lax.axis_index("y"), lax.axis_index("c")
    pos = jnp.stack([ci, 2 * xi + yi]).astype(jnp.int32)

    saved = []
    h = xs
    weights = [None] * L
    for l in range(L):
        if l == 0:
            z, hb, fetched = _in_proj(h, row(norm_g, l), win0, 0, gather_early)
            weights[0] = [win0] + list(fetched[:4])
        else:
            z, hb, _ = _in_proj(h, row(norm_g, l), weights[l][0], 0)
        win, wkv, wb, wo, cw = weights[l]
        bra = _branch_a_fwd(z, row(gmlp_ln_g, l), row(gmlp_ln_b, l), t_mix[l], bias_full[l])
        c0, late = _conv_fwd(z, cw, row(conv_b, l), 0, gather_late if l == 0 else None)
        if l == 0:
            weights[1] = [fetched[4]] + list(late)
        brb = _branch_b_fwd(c0, z, row(conv_ln_g, l), row(conv_ln_b, l))
        mn, kv, kvt = _kv_fwd(mems, row(mem_norm_g, l), wkv, 0)
        brc = _branch_c_fwd(z, kv, kvt)
        merged, h_next = _merge_fwd(bra, brb, brc, z, h, wb[None], wo, 0)
        saved.append((h, z, hb, bra, c0, brb, mn, kv, kvt, brc, merged))
        h = h_next

    dx, loss_cols, d_final_g = _loss_bwd(h, tgt, final_norm_g[None, :])
    loss = lax.psum(0.5 * jnp.sum(loss_cols) / D, ("x", "y", "c"))

    def add_halves(grads, recv, axes):
        return [_add_half(g, r, ax[1], pos) for g, r, ax in zip(grads, recv, axes)]

    def finish_reduction(sums, blocks, axes):
        halves = [_sum_half(s, b, ax, pos) for (s, _), b, ax in zip(sums, blocks, axes)]
        return list(_sibling_share(halves, axes))

    small = {k: [None] * L for k in ("norm_g", "mem_norm_g", "gmlp_ln_g", "gmlp_ln_b", "w_s", "b_s", "conv_w",
                                     "conv_b", "conv_ln_g", "conv_ln_b")}
    early_axes = reduce_axes[1:]
    pending, reduced, early = None, [None] * L, None
    for l in reversed(range(L)):
        h_in, z, hb, bra, c0, brb, mn, kv, kvt, brc, merged = saved[l]
        win, wkv, wb, wo, cw = weights[l]
        dz_m, dproj, dbr, recv = _merge_bwd(dx, bra, brb, brc, z, wb[None], wo, 0,
                                            _pair_comm(pending, reduce_axes) if pending else None)
        sums = add_halves(pending, recv, reduce_axes) if pending else []
        g_o = _wgrad(merged, dx, (1, D, D), 0, 0, None)
        g_b = None
        for n, br in enumerate((bra, brb, brc)):
            g_b = _wgrad(br, dproj, (N_BRANCHES, D, D), n, 0, g_b, b_plane=n)
        dz_a, dws, dbs, dlg, dlb = _branch_a_bwd(z, dbr, row(gmlp_ln_g, l), row(gmlp_ln_b, l),
                                                t_mix[l], t_mix_t[l], bias_full[l], tri)
        dz_c, dk, dv = _branch_c_bwd(z, dbr, kv, kvt)
        g_kv, dmg = _kv_bwd(dk, dv, mn, mems, wkv, 0, None)
        dc0, dz_b, dcg, dcbeta, dcb = _branch_b_bwd(c0, z, dbr, row(conv_ln_g, l), row(conv_ln_b, l))
        ready = [g_kv, g_b, g_o]
        dz_b, dcw, recv_early = _conv_bwd(dc0, z, cw, dz_b, 0, _pair_comm(ready, early_axes) if pending else None)
        sums_early = add_halves(ready, recv_early, early_axes) if pending else []
        pieces = [(dz_a, 0, 3), (dz_b, 3, 3), (dz_c, 6, 2), (dz_m, 8, 3)]
        g_in = None
        for dzp, lo, _ in pieces:
            g_in = _wgrad(hb, dzp, (1, D, N_PLANES * D), 0, lo * D, g_in)
        riding = [sb for _, sb in sums + sums_early]
        dx, dng, blocks = _in_proj_bwd(pieces, win, h_in, row(norm_g, l), dx, 0,
                                       _block_comm(riding, reduce_axes + early_axes) if pending else None)
        if pending:
            reduced[l + 1] = finish_reduction(sums, blocks[:4], reduce_axes)
            early = finish_reduction(sums_early, blocks[4:], early_axes)
        pending = [g_in, g_kv, g_b, g_o]
        for k, val in (("norm_g", dng), ("mem_norm_g", dmg), ("gmlp_ln_g", dlg), ("gmlp_ln_b", dlb), ("w_s", dws),
                       ("b_s", dbs), ("conv_w", dcw), ("conv_b", dcb), ("conv_ln_g", dcg), ("conv_ln_b", dcbeta)):
            small[k][l] = val
    grad_x = dx[None]

    last, last_axes = pending[:1], reduce_axes[:1]
    sums = add_halves(last, _run_alone(_pair_comm(last, last_axes), "pair_exchange"), last_axes)
    blocks = _run_alone(_block_comm([sb for _, sb in sums], last_axes), "block_exchange")
    reduced[0] = finish_reduction(sums, blocks, last_axes) + early
    grad_w_in, grad_w_kv, grad_w_branch, grad_w_out = [
        jnp.stack([reduced[0][t], reduced[1][t]]).reshape(w.shape) for t, w in enumerate((w_in, w_kv, w_branch, w_out))]

    order = [("norm_g", norm_g.shape), ("mem_norm_g", mem_norm_g.shape), ("gmlp_ln_g", gmlp_ln_g.shape),
             ("gmlp_ln_b", gmlp_ln_b.shape), ("w_s", w_s.shape), ("b_s", b_s.shape),
             ("conv_w", (L, CONV_K, D)), ("conv_b", conv_b.shape), ("conv_ln_g", conv_ln_g.shape),
             ("conv_ln_b", conv_ln_b.shape)]
    parts = [jnp.stack([v.reshape(shp[1:]) for v in small[k]]).reshape(-1, LANES) for k, shp in order]
    parts.append(d_final_g.reshape(-1, LANES))
    sizes = [p.shape[0] for p in parts]
    assert all(s % 8 == 0 for s in sizes)
    total = _small_allreduce(jnp.concatenate(parts, axis=0))
    red, off = {}, 0
    for (k, shp), n in zip(order + [("final_norm_g", final_norm_g.shape)], sizes):
        red[k] = total[off:off + n].reshape(shp)
        off += n
    cs = conv_w.shape[-1]
    red["conv_w"] = lax.dynamic_slice_in_dim(red["conv_w"], (2 * xi + yi) * cs, cs, axis=2)
    red.update(w_in=grad_w_in, w_kv=grad_w_kv, w_branch=grad_w_branch, w_out=grad_w_out)

    names = ["norm_g", "mem_norm_g", "w_in", "gmlp_ln_g", "gmlp_ln_b", "w_s", "b_s", "conv_w", "conv_b",
             "conv_ln_g", "conv_ln_b", "w_kv", "w_branch", "w_out", "final_norm_g"]
    weights = dict(norm_g=norm_g, mem_norm_g=mem_norm_g, w_in=w_in, gmlp_ln_g=gmlp_ln_g, gmlp_ln_b=gmlp_ln_b,
                   w_s=w_s, b_s=b_s, conv_w=conv_w, conv_b=conv_b, conv_ln_g=conv_ln_g, conv_ln_b=conv_ln_b,
                   w_kv=w_kv, w_branch=w_branch, w_out=w_out, final_norm_g=final_norm_g)
    ms = dict(norm_g=m_norm_g, mem_norm_g=m_mem_norm_g, w_in=m_w_in, gmlp_ln_g=m_gmlp_ln_g, gmlp_ln_b=m_gmlp_ln_b,
              w_s=m_w_s, b_s=m_b_s, conv_w=m_conv_w, conv_b=m_conv_b, conv_ln_g=m_conv_ln_g, conv_ln_b=m_conv_ln_b,
              w_kv=m_w_kv, w_branch=m_w_branch, w_out=m_w_out, final_norm_g=m_final_norm_g)
    vs = dict(norm_g=v_norm_g, mem_norm_g=v_mem_norm_g, w_in=v_w_in, gmlp_ln_g=v_gmlp_ln_g, gmlp_ln_b=v_gmlp_ln_b,
              w_s=v_w_s, b_s=v_b_s, conv_w=v_conv_w, conv_b=v_conv_b, conv_ln_g=v_conv_ln_g, conv_ln_b=v_conv_ln_b,
              w_kv=v_w_kv, w_branch=v_w_branch, w_out=v_w_out, final_norm_g=v_final_norm_g)
    deltas, new_m, new_v = [], [], []
    for k in names:
        w2 = weights[k] if weights[k].ndim > 1 else weights[k][None, :]
        d, mo, vo = _adamw(w2, red[k].reshape(w2.shape), ms[k].reshape(w2.shape), vs[k].reshape(w2.shape))
        deltas.append(d.reshape(weights[k].shape))
        new_m.append(mo.reshape(weights[k].shape))
        new_v.append(vo.reshape(weights[k].shape))
    grads = [red[k].reshape(weights[k].shape) for k in names]
    return (loss, grad_x, *grads, *deltas, *new_m, *new_v)
```

```python
import math

import jax
import jax.numpy as jnp
from jax import lax
from jax.experimental import pallas as pl
from jax.experimental.pallas import tpu as pltpu

F32 = jnp.float32
BF16 = jnp.bfloat16

N_LAYERS = 2
N_BRANCHES = 3
N_PLANES = 11
N_CHIPS = 4
CHUNK = 128
GROUPS = 8
HEADS = 4
CONV_K = 31
HALO = 32
LANES = 128
RMS_EPS = 1e-6
LN_EPS = 1e-5
ADAM_LR, ADAM_B1, ADAM_B2, ADAM_EPS, ADAM_WD, ADAM_STEP = 0.001, 0.9, 0.999, 1e-08, 0.01, 10
VMEM_LIMIT_BYTES = 48 * 1024 * 1024
MESH = pl.DeviceIdType.MESH
ANY = pl.BlockSpec(memory_space=pl.ANY)


def _params(*sem):
    return pltpu.CompilerParams(dimension_semantics=sem, vmem_limit_bytes=VMEM_LIMIT_BYTES)


def _tile(n, pref):
    t = min(n, pref)
    assert n % t == 0, (n, t)
    return t


def _dot(a, b):
    return jnp.dot(a, b, preferred_element_type=F32)


def _dot_tb(a, b):
    return lax.dot_general(a, b, (((1,), (1,)), ((), ())), preferred_element_type=F32)


def _dot_ta(a, b):
    return lax.dot_general(a, b, (((0,), (0,)), ((), ())), preferred_element_type=F32)


def _sigmoid(x):
    return 1.0 / (1.0 + jnp.exp(-x))


def _silu(x):
    return x * _sigmoid(x)


def _silu_and_grad(x):
    s = _sigmoid(x)
    return x * s, s * (1.0 + x * (1.0 - s))


_GELU_C = math.sqrt(2.0 / math.pi)
_GELU_A = 0.044715


def _gelu(x):
    return x * (0.5 * (1.0 + jnp.tanh(_GELU_C * (x + _GELU_A * (x * x * x)))))


def _gelu_and_grad(x):
    x2 = x * x
    t = jnp.tanh(_GELU_C * (x + _GELU_A * (x2 * x)))
    cdf = 0.5 * (1.0 + t)
    dcdf = 0.5 * (1.0 - t * t) * (_GELU_C * (1.0 + 3.0 * _GELU_A * x2))
    return x * cdf, cdf + x * dcdf


def _rows(i, n):
    return pl.ds(pl.multiple_of(i * n, n), n)


def _grouped_loop(n, group, body, init):
    assert n % group == 0

    def trip(i, c):
        for u in range(group):
            c = body(i * group + u, c)
        return c
    return lax.fori_loop(0, n // group, trip, init)


def _mean(x):
    return jnp.mean(x, axis=-1, keepdims=True)


def _colsum(x):
    return jnp.sum(x, axis=0, keepdims=True)


def _tok(ts, d):
    return pl.BlockSpec((ts, d), lambda i: (i, 0))


def _zcol(ts, d, n):
    return pl.BlockSpec((ts, d), lambda i: (i, n))


def _row1(d):
    return pl.BlockSpec((1, d), lambda i: (0, 0))


def _plane(ts, d, n):
    return pl.BlockSpec((None, ts, d), lambda i: (n, i, 0))


class _Hosted:
    def __init__(self, ins, out_shapes, sem_shapes, phases):
        self.ins, self.out_shapes, self.sem_shapes, self.phases = ins, out_shapes, sem_shapes, phases


def _host(comm, n_in, n_out, n_scr, grid, body):
    if comm is None:
        return body, [], [], [], [], None
    ci, co = len(comm.ins), len(comm.out_shapes)
    total = math.prod(grid)
    at = {"first": 0, "late": (7 * total) // 8, "last": total - 1}

    def wrapped(*refs):
        ins, cin = refs[:n_in], refs[n_in:n_in + ci]
        o0 = n_in + ci
        outs, cout = refs[o0:o0 + n_out], refs[o0 + n_out:o0 + n_out + co]
        s0 = o0 + n_out + co
        scr, csem = refs[s0:s0 + n_scr], refs[s0 + n_scr:]
        step = pl.program_id(0)
        for a in range(1, len(grid)):
            step = step * grid[a] + pl.program_id(a)

        def run(where):
            for w, fn in comm.phases:
                if w == where:
                    @pl.when(step == at[where])
                    def _(fn=fn):
                        fn(cin, cout, csem)
        run("first")
        run("late")
        body(*ins, *outs, *scr)
        run("last")

    return wrapped, [ANY] * ci, [ANY] * co, list(comm.out_shapes), list(comm.sem_shapes), ("arbitrary",) * len(grid)


def _in_proj(x, g, w_all, l, comm=None):
    S, D = x.shape
    N = w_all.shape[2]
    tm, tn = _tile(S, 1024), _tile(N, 1408)
    grid = (S // tm, N // tn)

    def body(x_ref, g_ref, w_ref, z_ref, hb_ref, h_scr):
        @pl.when(pl.program_id(1) == 0)
        def _():
            def slab(i, c):
                r = _rows(i, 16)
                xs = x_ref[r, :]
                h = (xs * lax.rsqrt(_mean(xs * xs) + RMS_EPS)) * g_ref[...]
                hb = h.astype(BF16)
                h_scr[r, :] = hb
                hb_ref[r, :] = hb
                return c
            _grouped_loop(tm // 16, 4, slab, 0)
        z_ref[...] = _dot(h_scr[...], w_ref[...])

    body, c_in, c_out, c_shapes, c_scr, sem = _host(comm, 3, 2, 1, grid, body)
    outs = pl.pallas_call(
        body, name="in_proj", grid=grid,
        in_specs=[pl.BlockSpec((tm, D), lambda i, j: (i, 0)),
                  pl.BlockSpec((1, D), lambda i, j: (0, 0)),
                  pl.BlockSpec((None, D, tn), lambda i, j: (l, 0, j))] + c_in,
        out_specs=[pl.BlockSpec((tm, tn), lambda i, j: (i, j)),
                   pl.BlockSpec((tm, D), lambda i, j: (i, 0))] + c_out,
        out_shape=[jax.ShapeDtypeStruct((S, N), F32), jax.ShapeDtypeStruct((S, D), BF16)] + c_shapes,
        scratch_shapes=[pltpu.VMEM((tm, D), BF16)] + c_scr,
        compiler_params=_params(*(sem or ("parallel", "arbitrary"))),
    )(x, g, w_all, *(comm.ins if comm else []))
    return outs[0], outs[1], outs[2:]


def _branch_a_fwd(z, lng, lnb, t_mix, bias_full):
    S = z.shape[0]
    D = lng.shape[1]
    GD = D // GROUPS
    ts = _tile(S, 256)

    def body(zu_ref, zv_ref, zg_ref, lng_ref, lnb_ref, t_ref, bias_ref, a_ref, v_scr):
        def chunk(ci, carry):
            c0 = pl.multiple_of(ci * CHUNK, CHUNK)

            def slab(si, c):
                r = pl.multiple_of(si * 16, 16)
                vg = _gelu(zv_ref[pl.ds(pl.multiple_of(c0 + r, 16), 16), :])
                xc = vg - _mean(vg)
                y = xc * lax.rsqrt(_mean(xc * xc) + LN_EPS)
                v_scr[pl.ds(r, 16), :] = (y * lng_ref[...] + lnb_ref[...]).astype(BF16)
                return c
            _grouped_loop(CHUNK // 16, 4, slab, 0)
            for g in range(GROUPS):
                cols = slice(g * GD, (g + 1) * GD)
                sv = _dot(t_ref[g], v_scr[:, cols]) + bias_ref[:, cols]
                u = _gelu(zu_ref[pl.ds(c0, CHUNK), cols])
                a_ref[pl.ds(c0, CHUNK), cols] = ((u * sv) * _silu(zg_ref[pl.ds(c0, CHUNK), cols])).astype(BF16)
            return carry
        lax.fori_loop(0, ts // CHUNK, chunk, 0)

    return pl.pallas_call(
        body, name="branch_a_fwd", grid=(S // ts,),
        in_specs=[_zcol(ts, D, 0), _zcol(ts, D, 1), _zcol(ts, D, 2), _row1(D), _row1(D),
                  pl.BlockSpec((GROUPS, CHUNK, CHUNK), lambda i: (0, 0, 0)),
                  pl.BlockSpec((CHUNK, D), lambda i: (0, 0))],
        out_specs=_tok(ts, D),
        out_shape=jax.ShapeDtypeStruct((S, D), BF16),
        scratch_shapes=[pltpu.VMEM((CHUNK, D), BF16)],
        compiler_params=_params("parallel"),
    )(z, z, z, lng, lnb, t_mix, bias_full)


SH = HALO - CONV_K + 1


def _fill_shifts(sh_scr, src_scr, cols, ts):
    for j in range(1, 8):
        for g in range(ts // 32):
            sh_scr[j - 1, pl.ds(g * 32, 32), :] = src_scr[pl.ds(g * 32 + j, 32), cols]
        sh_scr[j - 1, pl.ds(ts, HALO - 8), :] = src_scr[pl.ds(ts + j, HALO - 8), cols]


def _shifted(sh_scr, src_scr, cols, row0, n):
    q, j = divmod(row0, 8)
    if j == 0:
        return src_scr[pl.ds(row0, n), cols]
    return sh_scr[j - 1, pl.ds(8 * q, n), :]


def _conv_fwd(z, w_all, cb, l, comm=None):
    S = z.shape[0]
    D = cb.shape[1]
    ts = _tile(S, 256)
    CB = 256
    hb = ts // HALO

    def body(za_ref, zb_ref, zap_ref, zbp_ref, w_ref, b_ref, c0_ref, glu_scr, sh_scr):
        i = pl.program_id(0)
        prev = zap_ref[...] * _sigmoid(zbp_ref[...])
        glu_scr[0:HALO, :] = jnp.where(i == 0, 0.0, prev)

        def fill(si, c):
            r = _rows(si, 32)
            glu_scr[pl.ds(pl.multiple_of(HALO + si * 32, 32), 32), :] = za_ref[r, :] * _sigmoid(zb_ref[r, :])
            return c
        _grouped_loop(ts // 32, 2, fill, 0)

        def colblock(ci, c):
            cols = _rows(ci, CB)
            _fill_shifts(sh_scr, glu_scr, cols, ts)
            for s in range(ts // 32):
                acc = jnp.zeros((32, CB), F32)
                for k in range(CONV_K):
                    acc = acc + _shifted(sh_scr, glu_scr, cols, s * 32 + k + SH, 32) * w_ref[k:k + 1, cols]
                c0_ref[pl.ds(s * 32, 32), cols] = acc + b_ref[:, cols]
            return c
        lax.fori_loop(0, D // CB, colblock, 0)

    prev_map = lambda n: (lambda i: (jnp.maximum(i * hb - 1, 0), n))
    grid = (S // ts,)
    body, c_in, c_out, c_shapes, c_scr, sem = _host(comm, 6, 1, 2, grid, body)
    outs = pl.pallas_call(
        body, name="conv_fwd", grid=grid,
        in_specs=[_zcol(ts, D, 3), _zcol(ts, D, 4),
                  pl.BlockSpec((HALO, D), prev_map(3)), pl.BlockSpec((HALO, D), prev_map(4)),
                  pl.BlockSpec((None, CONV_K, D), lambda i: (l, 0, 0)), _row1(D)] + c_in,
        out_specs=[_tok(ts, D)] + c_out,
        out_shape=[jax.ShapeDtypeStruct((S, D), F32)] + c_shapes,
        scratch_shapes=[pltpu.VMEM((ts + HALO, D), F32), pltpu.VMEM((7, ts + HALO, CB), F32)] + c_scr,
        compiler_params=_params(*(sem or ("parallel",))),
    )(z, z, z, z, w_all, cb, *(comm.ins if comm else []))
    return outs[0], outs[1:]


def _branch_b_fwd(c0, z, g, b):
    S, D = c0.shape
    ts = _tile(S, 512)

    def body(c0_ref, zg_ref, g_ref, b_ref, o_ref):
        def slab(si, c):
            r = _rows(si, 16)
            v = c0_ref[r, :]
            xc = v - _mean(v)
            c1 = (xc * lax.rsqrt(_mean(xc * xc) + LN_EPS)) * g_ref[...] + b_ref[...]
            o_ref[r, :] = (_silu(c1) * _silu(zg_ref[r, :])).astype(BF16)
            return c
        _grouped_loop(ts // 16, 4, slab, 0)

    return pl.pallas_call(
        body, name="branch_b_fwd", grid=(S // ts,),
        in_specs=[_tok(ts, D), _zcol(ts, D, 5), _row1(D), _row1(D)],
        out_specs=_tok(ts, D),
        out_shape=jax.ShapeDtypeStruct((S, D), BF16),
        compiler_params=_params("parallel"),
    )(c0, z, g, b)


def _kv_fwd(mem, mg, wkv_all, l):
    M, D = mem.shape

    def body(mem_ref, g_ref, w_ref, mn_ref, kv_ref, kvt_ref):
        m = mem_ref[...]
        mn = ((m * lax.rsqrt(_mean(m * m) + RMS_EPS)) * g_ref[...]).astype(BF16)
        mn_ref[...] = mn
        kv = _dot(mn, w_ref[...])
        kv_ref[...] = kv.astype(BF16)
        kvt_ref[...] = kv.T.astype(BF16)

    return pl.pallas_call(
        body, name="kv_fwd", grid=(1,),
        in_specs=[pl.BlockSpec((M, D), lambda i: (0, 0)), _row1(D),
                  pl.BlockSpec((None, D, 2 * D), lambda i: (l, 0, 0))],
        out_specs=[pl.BlockSpec((M, D), lambda i: (0, 0)), pl.BlockSpec((M, 2 * D), lambda i: (0, 0)),
                   pl.BlockSpec((2 * D, M), lambda i: (0, 0))],
        out_shape=[jax.ShapeDtypeStruct((M, D), BF16), jax.ShapeDtypeStruct((M, 2 * D), BF16),
                   jax.ShapeDtypeStruct((2 * D, M), BF16)],
        compiler_params=_params("arbitrary"),
    )(mem, mg, wkv_all)


def _softmax_rows(s):
    e = jnp.exp(s - jnp.max(s, axis=-1, keepdims=True))
    return e / jnp.sum(e, axis=-1, keepdims=True)


def _branch_c_fwd(z, kv, kvt):
    S = z.shape[0]
    M, D2 = kv.shape
    D = D2 // 2
    HD = D // HEADS
    scale = 1.0 / math.sqrt(HD)
    ts = _tile(S, 256)

    def body(q_ref, zg_ref, kt_ref, v_ref, o_ref):
        for h in range(HEADS):
            cols = slice(h * HD, (h + 1) * HD)
            p = _softmax_rows(_dot(q_ref[:, cols].astype(BF16), kt_ref[cols, :]) * scale)
            att = _dot(p.astype(BF16), v_ref[:, cols])
            o_ref[:, cols] = (att * _silu(zg_ref[:, cols])).astype(BF16)

    return pl.pallas_call(
        body, name="branch_c_fwd", grid=(S // ts,),
        in_specs=[_zcol(ts, D, 6), _zcol(ts, D, 7),
                  pl.BlockSpec((D, M), lambda i: (0, 0)), pl.BlockSpec((M, D), lambda i: (0, 1))],
        out_specs=_tok(ts, D),
        out_shape=jax.ShapeDtypeStruct((S, D), BF16),
        compiler_params=_params("parallel"),
    )(z, z, kvt, kv)


def _merge_fwd(bra, brb, brc, z, x, wb_all, wo_all, l):
    S, D = x.shape
    ts = _tile(S, 512)
    CB = 256

    def body(a_ref, b_ref, c_ref, zm0_ref, zm1_ref, zm2_ref, x_ref, wb_ref, wo_ref, m_ref, xo_ref):
        brs = (a_ref, b_ref, c_ref)
        zms = (zm0_ref, zm1_ref, zm2_ref)
        for cb in range(D // CB):
            cols = slice(cb * CB, (cb + 1) * CB)
            acc = None
            for n in range(N_BRANCHES):
                t = _sigmoid(zms[n][:, cols]) * _dot(brs[n][...], wb_ref[n, :, cols])
                acc = t if acc is None else acc + t
            m_ref[:, cols] = acc.astype(BF16)
        for cb in range(D // CB):
            cols = slice(cb * CB, (cb + 1) * CB)
            xo_ref[:, cols] = x_ref[:, cols] + _dot(m_ref[...], wo_ref[:, cols])

    return pl.pallas_call(
        body, name="merge_fwd", grid=(S // ts,),
        in_specs=[_tok(ts, D), _tok(ts, D), _tok(ts, D), _zcol(ts, D, 8), _zcol(ts, D, 9), _zcol(ts, D, 10),
                  _tok(ts, D),
                  pl.BlockSpec((None, N_BRANCHES, D, D), lambda i: (l, 0, 0, 0)),
                  pl.BlockSpec((None, D, D), lambda i: (l, 0, 0))],
        out_specs=[_tok(ts, D), _tok(ts, D)],
        out_shape=[jax.ShapeDtypeStruct((S, D), BF16), jax.ShapeDtypeStruct((S, D), F32)],
        compiler_params=_params("parallel"),
    )(bra, brb, brc, z, z, z, x, wb_all, wo_all)


def _loss_bwd(x, target, fg):
    S, D = x.shape
    ts = _tile(S, 512)

    def body(x_ref, t_ref, g_ref, dx_ref, ls_ref, dg_ref):
        @pl.when(pl.program_id(0) == 0)
        def _():
            ls_ref[...] = jnp.zeros_like(ls_ref)
            dg_ref[...] = jnp.zeros_like(dg_ref)

        def slab(si, c):
            ls, dg = c
            r = _rows(si, 16)
            xs = x_ref[r, :]
            rs = lax.rsqrt(_mean(xs * xs) + RMS_EPS)
            n = xs * rs
            e = n * g_ref[...] - t_ref[r, :]
            dy = e * (1.0 / D)
            dn = dy * g_ref[...]
            dx_ref[r, :] = rs * (dn - n * _mean(dn * n))
            return ls + _colsum(e * e), dg + _colsum(dy * n)
        zero = jnp.zeros((1, D), F32)
        ls, dg = _grouped_loop(ts // 16, 4, slab, (zero, zero))
        ls_ref[...] += ls
        dg_ref[...] += dg

    return pl.pallas_call(
        body, name="loss_bwd", grid=(S // ts,),
        in_specs=[_tok(ts, D), _tok(ts, D), _row1(D)],
        out_specs=[_tok(ts, D), _row1(D), _row1(D)],
        out_shape=[jax.ShapeDtypeStruct((S, D), F32), jax.ShapeDtypeStruct((1, D), F32),
                   jax.ShapeDtypeStruct((1, D), F32)],
        compiler_params=_params("arbitrary"),
    )(x, target, fg)


def _merge_bwd(dxp, bra, brb, brc, z, wb_all, wo_all, l, comm=None):
    S, D = dxp.shape
    ts = _tile(S, 256)
    CB = 256

    def body(d_ref, a_ref, b_ref, c_ref, zm0_ref, zm1_ref, zm2_ref, wb_ref, wo_ref,
             dzm_ref, dproj_ref, dbr_ref, dxb_scr):
        brs = (a_ref, b_ref, c_ref)
        zms = (zm0_ref, zm1_ref, zm2_ref)
        dxb_scr[...] = d_ref[...].astype(BF16)
        for cb in range(D // CB):
            cols = slice(cb * CB, (cb + 1) * CB)
            dm = _dot_tb(dxb_scr[...], wo_ref[cols, :])
            for n in range(N_BRANCHES):
                proj = _dot(brs[n][...], wb_ref[n, :, cols])
                g = _sigmoid(zms[n][:, cols])
                dzm_ref[:, n * D + cb * CB:n * D + (cb + 1) * CB] = ((dm * proj) * (g * (1.0 - g))).astype(BF16)
                dproj_ref[n, :, cols] = (dm * g).astype(BF16)
        for n in range(N_BRANCHES):
            for cb in range(D // CB):
                cols = slice(cb * CB, (cb + 1) * CB)
                dbr_ref[n, :, cols] = _dot_tb(dproj_ref[n], wb_ref[n, cols, :])

    tok3 = lambda: pl.BlockSpec((N_BRANCHES, ts, D), lambda i: (0, i, 0))
    grid = (S // ts,)
    body, c_in, c_out, c_shapes, c_scr, sem = _host(comm, 9, 3, 1, grid, body)
    outs = pl.pallas_call(
        body, name="merge_bwd", grid=grid,
        in_specs=[_tok(ts, D), _tok(ts, D), _tok(ts, D), _tok(ts, D),
                  _zcol(ts, D, 8), _zcol(ts, D, 9), _zcol(ts, D, 10),
                  pl.BlockSpec((None, N_BRANCHES, D, D), lambda i: (l, 0, 0, 0)),
                  pl.BlockSpec((None, D, D), lambda i: (l, 0, 0))] + c_in,
        out_specs=[_tok(ts, 3 * D), tok3(), tok3()] + c_out,
        out_shape=[jax.ShapeDtypeStruct((S, 3 * D), BF16), jax.ShapeDtypeStruct((N_BRANCHES, S, D), BF16),
                   jax.ShapeDtypeStruct((N_BRANCHES, S, D), F32)] + c_shapes,
        scratch_shapes=[pltpu.VMEM((ts, D), BF16)] + c_scr,
        compiler_params=_params(*(sem or ("parallel",))),
    )(dxp, bra, brb, brc, z, z, z, wb_all, wo_all, *(comm.ins if comm else []))
    return outs[0], outs[1], outs[2], outs[3:]


def _branch_a_bwd(z, dbr, lng, lnb, t_mix, t_mix_t, bias_full, mask):
    S = z.shape[0]
    D = lng.shape[1]
    GD = D // GROUPS
    ts = _tile(S, 256)
    nt = S // ts

    def body(zu_ref, zv_ref, zg_ref, d_ref, lng_ref, lnb_ref, t_ref, tt_ref, bias_ref, mask_ref,
             dz_ref, dws_ref, dbs_ref, dlg_ref, dlb_ref,
             v_scr, vn_scr, rstd_scr, dv_scr, dsv_acc):
        i = pl.program_id(0)

        @pl.when(i == 0)
        def _():
            dws_ref[...] = jnp.zeros_like(dws_ref)
            dlg_ref[...] = jnp.zeros_like(dlg_ref)
            dlb_ref[...] = jnp.zeros_like(dlb_ref)
            dsv_acc[...] = jnp.zeros_like(dsv_acc)

        def chunk(ci, carry):
            c0 = pl.multiple_of(ci * CHUNK, CHUNK)

            def ln_fwd(si, c):
                r = pl.multiple_of(si * 16, 16)
                vg = _gelu(zv_ref[pl.ds(pl.multiple_of(c0 + r, 16), 16), :])
                xc = vg - _mean(vg)
                rstd = lax.rsqrt(_mean(xc * xc) + LN_EPS)
                vn = xc * rstd
                vn_scr[pl.ds(r, 16), :] = vn
                rstd_scr[pl.ds(r, 16), :] = rstd
                v_scr[pl.ds(r, 16), :] = (vn * lng_ref[...] + lnb_ref[...]).astype(BF16)
                return c
            _grouped_loop(CHUNK // 16, 4, ln_fwd, 0)

            for g in range(GROUPS):
                cols = slice(g * GD, (g + 1) * GD)
                rows = pl.ds(c0, CHUNK)
                vb = v_scr[:, cols]
                sv = _dot(t_ref[g], vb) + bias_ref[:, cols]
                u, du = _gelu_and_grad(zu_ref[rows, cols])
                sg, dsg = _silu_and_grad(zg_ref[rows, cols])
                d = d_ref[rows, cols]
                dsv = (d * u) * sg
                dz_ref[rows, g * GD:(g + 1) * GD] = (((d * sv) * sg) * du).astype(BF16)
                dz_ref[rows, 2 * D + g * GD:2 * D + (g + 1) * GD] = (((d * u) * sv) * dsg).astype(BF16)
                dsvb = dsv.astype(BF16)
                dws_ref[g] += _dot_tb(dsvb, vb)
                dv_scr[:, cols] = _dot(tt_ref[g], dsvb)
                dsv_acc[:, cols] += dsv

            def ln_bwd(si, c):
                dlg, dlb = c
                r = pl.multiple_of(si * 16, 16)
                rr = pl.ds(r, 16)
                zrows = pl.ds(pl.multiple_of(c0 + r, 16), 16)
                dv = dv_scr[rr, :]
                vn = vn_scr[rr, :]
                dvn = dv * lng_ref[...]
                dvg = rstd_scr[rr, :] * ((dvn - _mean(dvn)) - vn * _mean(dvn * vn))
                _, gg = _gelu_and_grad(zv_ref[zrows, :])
                dz_ref[zrows, D:2 * D] = (dvg * gg).astype(BF16)
                return dlg + _colsum(dv * vn), dlb + _colsum(dv)
            zero = jnp.zeros((1, D), F32)
            dlg, dlb = _grouped_loop(CHUNK // 16, 4, ln_bwd, (zero, zero))
            dlg_ref[...] += dlg
            dlb_ref[...] += dlb
            return carry
        lax.fori_loop(0, ts // CHUNK, chunk, 0)

        @pl.when(i == nt - 1)
        def _():
            for g in range(GROUPS):
                cols = slice(g * GD, (g + 1) * GD)
                dws_ref[g] = dws_ref[g] * mask_ref[...]
                dbs_ref[g:g + 1, :] = _colsum(dsv_acc[:, cols].T)

    c2 = lambda shape: pl.BlockSpec(shape, lambda i: (0, 0))
    c3 = lambda: pl.BlockSpec((GROUPS, CHUNK, CHUNK), lambda i: (0, 0, 0))
    return pl.pallas_call(
        body, name="branch_a_bwd", grid=(nt,),
        in_specs=[_zcol(ts, D, 0), _zcol(ts, D, 1), _zcol(ts, D, 2), _plane(ts, D, 0), _row1(D), _row1(D),
                  c3(), c3(), c2((CHUNK, D)), c2((CHUNK, CHUNK))],
        out_specs=[_tok(ts, 3 * D), c3(), c2((GROUPS, CHUNK)), _row1(D), _row1(D)],
        out_shape=[jax.ShapeDtypeStruct((S, 3 * D), BF16), jax.ShapeDtypeStruct((GROUPS, CHUNK, CHUNK), F32),
                   jax.ShapeDtypeStruct((GROUPS, CHUNK), F32), jax.ShapeDtypeStruct((1, D), F32),
                   jax.ShapeDtypeStruct((1, D), F32)],
        scratch_shapes=[pltpu.VMEM((CHUNK, D), BF16), pltpu.VMEM((CHUNK, D), F32), pltpu.VMEM((CHUNK, 1), F32),
                        pltpu.VMEM((CHUNK, D), F32), pltpu.VMEM((CHUNK, D), F32)],
        compiler_params=_params("arbitrary"),
    )(z, z, z, dbr, lng, lnb, t_mix, t_mix_t, bias_full, mask)


def _branch_b_bwd(c0, z, dbr, g, b):
    S, D = c0.shape
    ts = _tile(S, 512)

    def body(c0_ref, zg_ref, d_ref, g_ref, b_ref, dc0_ref, dz_ref, dg_ref, db_ref, dcb_ref):
        @pl.when(pl.program_id(0) == 0)
        def _():
            dg_ref[...] = jnp.zeros_like(dg_ref)
            db_ref[...] = jnp.zeros_like(db_ref)
            dcb_ref[...] = jnp.zeros_like(dcb_ref)

        def slab(si, c):
            dg, db, dcb = c
            r = _rows(si, 16)
            v = c0_ref[r, :]
            xc = v - _mean(v)
            rstd = lax.rsqrt(_mean(xc * xc) + LN_EPS)
            cn = xc * rstd
            c1 = cn * g_ref[...] + b_ref[...]
            c2, dc2_dc1 = _silu_and_grad(c1)
            sg, dsg = _silu_and_grad(zg_ref[r, :])
            d = d_ref[r, :]
            dz_ref[r, :] = ((d * c2) * dsg).astype(BF16)
            dc1 = (d * sg) * dc2_dc1
            dcn = dc1 * g_ref[...]
            dc0 = rstd * ((dcn - _mean(dcn)) - cn * _mean(dcn * cn))
            dc0_ref[r, :] = dc0
            return dg + _colsum(dc1 * cn), db + _colsum(dc1), dcb + _colsum(dc0)
        zero = jnp.zeros((1, D), F32)
        dg, db, dcb = _grouped_loop(ts // 16, 4, slab, (zero, zero, zero))
        dg_ref[...] += dg
        db_ref[...] += db
        dcb_ref[...] += dcb

    return pl.pallas_call(
        body, name="branch_b_bwd", grid=(S // ts,),
        in_specs=[_tok(ts, D), _zcol(ts, D, 5), _plane(ts, D, 1), _row1(D), _row1(D)],
        out_specs=[_tok(ts, D), _zcol(ts, D, 2), _row1(D), _row1(D), _row1(D)],
        out_shape=[jax.ShapeDtypeStruct((S, D), F32), jax.ShapeDtypeStruct((S, 3 * D), BF16),
                   jax.ShapeDtypeStruct((1, D), F32), jax.ShapeDtypeStruct((1, D), F32),
                   jax.ShapeDtypeStruct((1, D), F32)],
        compiler_params=_params("arbitrary"),
    )(c0, z, dbr, g, b)


def _conv_bwd(dc0, z, w_all, dz_b, l, comm=None):
    S, D = dc0.shape
    ts = _tile(S, 256)
    nt = S // ts
    CB = 256
    hb = ts // HALO
    nh = S // HALO

    def body(dc_ref, dcn_ref, za_ref, zb_ref, w_ref, dzin_ref, dz_ref, dw_ref, dc_scr, dw_acc, dsh_scr):
        del dzin_ref
        i = pl.program_id(0)

        @pl.when(i == 0)
        def _():
            dw_acc[...] = jnp.zeros_like(dw_acc)

        dc_scr[ts:ts + HALO, :] = jnp.where(i == nt - 1, 0.0, dcn_ref[...])

        def fill(si, c):
            r = _rows(si, 32)
            dc_scr[r, :] = dc_ref[r, :]
            return c
        _grouped_loop(ts // 32, 2, fill, 0)

        def colblock(ci, c):
            cols = _rows(ci, CB)
            cols_b = pl.ds(pl.multiple_of(D + ci * CB, CB), CB)
            _fill_shifts(dsh_scr, dc_scr, cols, ts)
            for s in range(ts // 32):
                rows = pl.ds(s * 32, 32)
                za = za_ref[rows, cols]
                sg = _sigmoid(zb_ref[rows, cols])
                glu = za * sg
                acc = jnp.zeros((32, CB), F32)
                for k in range(CONV_K):
                    dcs = _shifted(dsh_scr, dc_scr, cols, s * 32 + CONV_K - 1 - k, 32)
                    prod = glu * dcs
                    dw_acc[k, :, cols] += (prod[0:8] + prod[8:16]) + (prod[16:24] + prod[24:32])
                    acc = acc + dcs * w_ref[k:k + 1, cols]
                dz_ref[rows, cols] = (acc * sg).astype(BF16)
                dz_ref[rows, cols_b] = ((acc * za) * (sg * (1.0 - sg))).astype(BF16)
            return c
        lax.fori_loop(0, D // CB, colblock, 0)

        @pl.when(i == nt - 1)
        def _():
            for k in range(CONV_K):
                dw_ref[k:k + 1, :] = _colsum(dw_acc[k])

    grid = (nt,)
    body, c_in, c_out, c_shapes, c_scr, _ = _host(comm, 6, 2, 3, grid, body)
    outs = pl.pallas_call(
        body, name="conv_bwd", grid=grid,
        in_specs=[_tok(ts, D), pl.BlockSpec((HALO, D), lambda i: (jnp.minimum((i + 1) * hb, nh - 1), 0)),
                  _zcol(ts, D, 3), _zcol(ts, D, 4),
                  pl.BlockSpec((None, CONV_K, D), lambda i: (l, 0, 0)), ANY] + c_in,
        out_specs=[_tok(ts, 2 * D), pl.BlockSpec((CONV_K, D), lambda i: (0, 0))] + c_out,
        out_shape=[jax.ShapeDtypeStruct((S, 3 * D), BF16), jax.ShapeDtypeStruct((CONV_K, D), F32)] + c_shapes,
        scratch_shapes=[pltpu.VMEM((ts + HALO, D), F32), pltpu.VMEM((CONV_K, 8, D), F32),
                        pltpu.VMEM((7, ts + HALO, CB), F32)] + c_scr,
        input_output_aliases={5: 0},
        compiler_params=_params("arbitrary"),
    )(dc0, dc0, z, z, w_all, dz_b, *(comm.ins if comm else []))
    return outs[0], outs[1], outs[2:]


def _branch_c_bwd(z, dbr, kv, kvt):
    S = z.shape[0]
    M, D2 = kv.shape
    D = D2 // 2
    HD = D // HEADS
    scale = 1.0 / math.sqrt(HD)
    ts = _tile(S, 256)

    def body(q_ref, zg_ref, d_ref, kt_ref, vt_ref, k_ref, v_ref, dz_ref, dk_ref, dv_ref):
        @pl.when(pl.program_id(0) == 0)
        def _():
            dk_ref[...] = jnp.zeros_like(dk_ref)
            dv_ref[...] = jnp.zeros_like(dv_ref)

        for h in range(HEADS):
            cols = slice(h * HD, (h + 1) * HD)
            qb = q_ref[:, cols].astype(BF16)
            p = _softmax_rows(_dot(qb, kt_ref[cols, :]) * scale)
            pb = p.astype(BF16)
            att = _dot(pb, v_ref[:, cols])
            sg, dsg = _silu_and_grad(zg_ref[:, cols])
            d = d_ref[:, cols]
            dz_ref[:, D + h * HD:D + (h + 1) * HD] = ((d * att) * dsg).astype(BF16)
            datt = (d * sg).astype(BF16)
            dp = _dot(datt, vt_ref[cols, :])
            dv_ref[:, cols] += _dot_ta(pb, datt)
            ds = ((p * (dp - jnp.sum(dp * p, axis=-1, keepdims=True))) * scale).astype(BF16)
            dz_ref[:, cols] = _dot(ds, k_ref[:, cols]).astype(BF16)
            dk_ref[:, cols] += _dot_ta(ds, qb)

    return pl.pallas_call(
        body, name="branch_c_bwd", grid=(S // ts,),
        in_specs=[_zcol(ts, D, 6), _zcol(ts, D, 7), _plane(ts, D, 2),
                  pl.BlockSpec((D, M), lambda i: (0, 0)), pl.BlockSpec((D, M), lambda i: (1, 0)),
                  pl.BlockSpec((M, D), lambda i: (0, 0)), pl.BlockSpec((M, D), lambda i: (0, 1))],
        out_specs=[_tok(ts, 2 * D), pl.BlockSpec((M, D), lambda i: (0, 0)), pl.BlockSpec((M, D), lambda i: (0, 0))],
        out_shape=[jax.ShapeDtypeStruct((S, 2 * D), BF16), jax.ShapeDtypeStruct((M, D), F32),
                   jax.ShapeDtypeStruct((M, D), F32)],
        compiler_params=_params("arbitrary"),
    )(z, z, dbr, kvt, kvt, kv, kv)


def _kv_bwd(dk, dv, mn, mem, wkv_all, l, prev):
    M, D = mem.shape

    def body(*refs):
        dk_ref, dv_ref, mn_ref, mem_ref, w_ref = refs[:5]
        dw_ref, dg_ref = refs[-2:]
        dkb = dk_ref[...].astype(BF16)
        dvb = dv_ref[...].astype(BF16)
        dw_ref[:, 0:D] = _dot_ta(mn_ref[...], dkb)
        dw_ref[:, D:2 * D] = _dot_ta(mn_ref[...], dvb)
        dmn = _dot_tb(dkb, w_ref[:, 0:D]) + _dot_tb(dvb, w_ref[:, D:2 * D])
        m = mem_ref[...]
        dg_ref[...] = _colsum(dmn * (m * lax.rsqrt(_mean(m * m) + RMS_EPS)))

    full = lambda shape: pl.BlockSpec(shape, lambda i: (0, 0))
    in_specs = [full((M, D)), full((M, D)), full((M, D)), full((M, D)),
                pl.BlockSpec((None, D, 2 * D), lambda i: (l, 0, 0))]
    args = [dk, dv, mn, mem, wkv_all]
    aliases = {}
    if prev is not None:
        in_specs.append(ANY)
        args.append(prev)
        aliases = {5: 0}
    return pl.pallas_call(
        body, name="kv_bwd", grid=(1,),
        in_specs=in_specs,
        out_specs=[pl.BlockSpec((None, D, 2 * D), lambda i: (l, 0, 0)), _row1(D)],
        out_shape=[jax.ShapeDtypeStruct((wkv_all.shape[0], D, 2 * D), F32), jax.ShapeDtypeStruct((1, D), F32)],
        input_output_aliases=aliases,
        compiler_params=_params("arbitrary"),
    )(*args)


def _in_proj_bwd(pieces, w_all, x, g, dxp, l, comm=None, part=(0, 1), prev=None):
    S, D = x.shape
    tm = _tile(S, 1024)
    n_p = len(pieces)
    n_i = (S // tm) // part[1]
    i0 = part[0] * n_i
    n_in = n_p + 4 + (prev is not None)

    def body(*refs):
        dz_refs = refs[:n_p]
        w_ref, x_hbm, g_ref, dxp_hbm = refs[n_p:n_p + 4]
        dx_ref, dg_ref, acc, x_ref, dxp_ref, sem = refs[n_in:]
        i = pl.program_id(0)
        k = pl.program_id(1)
        rows = pl.ds(pl.multiple_of((i + i0) * tm, tm), tm)
        fetch = [pltpu.make_async_copy(x_hbm.at[rows, :], x_ref, sem.at[0]),
                 pltpu.make_async_copy(dxp_hbm.at[rows, :], dxp_ref, sem.at[1])]

        @pl.when(k == N_PLANES - 2)
        def _():
            for cp in fetch:
                cp.start()

        @pl.when(k == 0)
        def _():
            acc[...] = jnp.zeros_like(acc)

        @pl.when((i == 0) & (k == 0))
        def _():
            dg_ref[...] = jnp.zeros_like(dg_ref)

        for (_, lo, n), r in zip(pieces, dz_refs):
            @pl.when((k >= lo) & (k < lo + n))
            def _(r=r):
                acc[...] += _dot_tb(r[...], w_ref[...])

        @pl.when(k == N_PLANES - 1)
        def _():
            for cp in fetch:
                cp.wait()

            def slab(si, c):
                rr = _rows(si, 16)
                xs = x_ref[rr, :]
                rs = lax.rsqrt(_mean(xs * xs) + RMS_EPS)
                n = xs * rs
                dh = acc[rr, :]
                dn = dh * g_ref[...]
                dx_ref[rr, :] = rs * (dn - n * _mean(dn * n)) + dxp_ref[rr, :]
                return c + _colsum(dh * n)
            dg_ref[...] += _grouped_loop(tm // 16, 4, slab, jnp.zeros((1, D), F32))

    def piece_spec(lo, n):
        return pl.BlockSpec((tm, D), lambda i, k: (i + i0, jnp.clip(k - lo, 0, n - 1)))

    tokk = lambda: pl.BlockSpec((tm, D), lambda i, k: (i + i0, 0))
    rowk = lambda: pl.BlockSpec((1, D), lambda i, k: (0, 0))
    grid = (n_i, N_PLANES)
    body, c_in, c_out, c_shapes, c_scr, _ = _host(comm, n_in, 2, 4, grid, body)
    more_in, more_args, aliases = ([ANY], [prev], {n_in - 1: 0}) if prev is not None else ([], [], {})
    outs = pl.pallas_call(
        body, name="in_proj_bwd", grid=grid,
        in_specs=[piece_spec(lo, n) for _, lo, n in pieces]
        + [pl.BlockSpec((None, D, D), lambda i, k: (l, 0, k)), ANY, rowk(), ANY] + more_in + c_in,
        out_specs=[tokk(), rowk()] + c_out,
        out_shape=[jax.ShapeDtypeStruct((S, D), F32), jax.ShapeDtypeStruct((1, D), F32)] + c_shapes,
        scratch_shapes=[pltpu.VMEM((tm, D), F32), pltpu.VMEM((tm, D), F32), pltpu.VMEM((tm, D), F32),
                        pltpu.SemaphoreType.DMA((2,))] + c_scr,
        input_output_aliases=aliases,
        compiler_params=_params("arbitrary", "arbitrary"),
    )(*[p for p, _, _ in pieces], w_all, x, g, dxp, *more_args, *(comm.ins if comm else []))
    return outs[0], outs[1], outs[2:]


def _wgrad(a, b, out_shape, plane, col_off, prev, a_plane=None, b_plane=None):
    S = a.shape[-2]
    K1 = a.shape[-1]
    nb_cols = b.shape[-1]
    ts = _tile(S, 2048)
    tn = _tile(nb_cols, 1024)
    col_off = col_off // tn

    def body(*refs):
        a_ref, b_ref = refs[:2]
        o_ref = refs[-1]
        k = pl.program_id(1)
        prod = _dot_ta(a_ref[...].astype(BF16), b_ref[...].astype(BF16))

        @pl.when(k == 0)
        def _():
            o_ref[...] = prod

        @pl.when(k > 0)
        def _():
            o_ref[...] += prod

    if a_plane is None:
        a_spec = pl.BlockSpec((ts, K1), lambda j, k: (k, 0))
    else:
        a_spec = pl.BlockSpec((None, ts, K1), lambda j, k: (a_plane, k, 0))
    if b_plane is None:
        b_spec = pl.BlockSpec((ts, tn), lambda j, k: (k, j))
    else:
        b_spec = pl.BlockSpec((None, ts, tn), lambda j, k: (b_plane, k, j))
    in_specs, args, aliases = [a_spec, b_spec], [a, b], {}
    if prev is not None:
        in_specs.append(ANY)
        args.append(prev)
        aliases = {2: 0}
    return pl.pallas_call(
        body, name="wgrad", grid=(nb_cols // tn, S // ts),
        in_specs=in_specs,
        out_specs=pl.BlockSpec((None, K1, tn), lambda j, k: (plane, 0, col_off + j)),
        out_shape=jax.ShapeDtypeStruct(out_shape, F32),
        input_output_aliases=aliases,
        compiler_params=_params("parallel", "arbitrary"),
    )(*args)


def _position():
    x, y, c = lax.axis_index("x"), lax.axis_index("y"), lax.axis_index("c")
    chips = [(1 - x, y), (x, 1 - y), (1 - x, 1 - y)]
    return x, y, c, chips


def _sub(ref, axis, start, size):
    idx = [slice(None)] * len(ref.shape)
    idx[axis] = pl.ds(start, size)
    return ref.at[tuple(idx)]


def _dma(src, dst, ssem, rsem, to):
    return pltpu.make_async_remote_copy(src_ref=src, dst_ref=dst, send_sem=ssem, recv_sem=rsem,
                                        device_id=to, device_id_type=MESH)


def _dma_sems(*counts):
    return [pltpu.SemaphoreType.DMA((n,)) for n in counts]


def _both(a, b):
    na, oa, sa = len(a.ins), len(a.out_shapes), len(a.sem_shapes)
    phases = [(w, lambda i, o, s, fn=fn: fn(i[:na], o[:oa], s[:sa])) for w, fn in a.phases]
    phases += [(w, lambda i, o, s, fn=fn: fn(i[na:], o[oa:], s[sa:])) for w, fn in b.phases]
    return _Hosted(a.ins + b.ins, list(a.out_shapes) + list(b.out_shapes), list(a.sem_shapes) + list(b.sem_shapes), phases)


def _run_alone(comm, name):
    ci, co = len(comm.ins), len(comm.out_shapes)

    def body(*refs):
        for _, fn in comm.phases:
            fn(refs[:ci], refs[ci:ci + co], refs[ci + co:])

    return pl.pallas_call(
        body, name=name, in_specs=[ANY] * ci, out_specs=[ANY] * co, out_shape=list(comm.out_shapes),
        scratch_shapes=list(comm.sem_shapes),
    )(*comm.ins)


def _gather_comm(shards, layers, axes):
    n = len(shards)
    out_shapes = []
    for s, (ba, _) in zip(shards, axes):
        shp = list(s.shape[1:])
        shp[ba] *= N_CHIPS
        out_shapes.append(jax.ShapeDtypeStruct(tuple(shp), s.dtype))

    def half(ref, t, h):
        ha = axes[t][1]
        if ha is None:
            return ref
        hs = shards[t].shape[1 + ha] // 2
        return _sub(ref, ha, h * hs, hs)

    def block(outs, t, blk):
        ba = axes[t][0]
        bs = shards[t].shape[1 + ba]
        return _sub(outs[t], ba, blk * bs, bs)

    def start(ins, outs, sems):
        s_own, r_own, s_ici, r_ici, _, _ = sems
        x, y, c, chips = _position()
        j = 2 * x + y
        for t in range(n):
            _dma(ins[t].at[layers[t]], block(outs, t, j), s_own.at[t], r_own.at[t], (x, y, 1 - c)).start()
            for k, (px, py) in enumerate(chips):
                _dma(half(ins[t].at[layers[t]], t, c), half(block(outs, t, j), t, c),
                     s_ici.at[3 * t + k], r_ici.at[3 * t + k], (px, py, c)).start()

    def forward(ins, outs, sems):
        _, _, s_ici, r_ici, s_fwd, r_fwd = sems
        x, y, c, chips = _position()
        for t in range(n):
            if axes[t][1] is None:
                continue
            for k, (px, py) in enumerate(chips):
                got = half(block(outs, t, 2 * px + py), t, c)
                _dma(got, got, s_ici.at[3 * t + k], r_ici.at[3 * t + k], (px, py, c)).wait_recv()
                _dma(got, got, s_fwd.at[3 * t + k], r_fwd.at[3 * t + k], (x, y, 1 - c)).start()

    def finish(ins, outs, sems):
        s_own, r_own, s_ici, r_ici, s_fwd, r_fwd = sems
        x, y, c, chips = _position()
        j = 2 * x + y
        sib = (x, y, 1 - c)
        for t in range(n):
            for k, (px, py) in enumerate(chips):
                i = 3 * t + k
                sent = half(block(outs, t, j), t, c)
                _dma(sent, sent, s_ici.at[i], r_ici.at[i], (px, py, c)).wait_send()
                theirs = block(outs, t, 2 * px + py)
                if axes[t][1] is None:
                    _dma(theirs, theirs, s_ici.at[i], r_ici.at[i], (px, py, c)).wait_recv()
                else:
                    got, other = half(theirs, t, c), half(theirs, t, 1 - c)
                    _dma(got, got, s_fwd.at[i], r_fwd.at[i], sib).wait_send()
                    _dma(other, other, s_fwd.at[i], r_fwd.at[i], sib).wait_recv()
            _dma(ins[t].at[layers[t]], block(outs, t, j), s_own.at[t], r_own.at[t], sib).wait()

    return _Hosted(list(shards), out_shapes, _dma_sems(n, n, 3 * n, 3 * n, 3 * n, 3 * n),
                   [("first", start), ("late", forward), ("last", finish)])


def _halved(shape, axis):
    shp = list(shape)
    shp[axis] //= 2
    return tuple(shp)


def _pair_comm(grads, axes):
    n = len(grads)

    def copies(ins, outs, sems):
        x, y, c, _ = _position()
        cps = []
        for t in range(n):
            ha = axes[t][1]
            hs = grads[t].shape[ha] // 2
            cps.append(_dma(_sub(ins[t], ha, (1 - c) * hs, hs), outs[t], sems[0].at[t], sems[1].at[t], (x, y, 1 - c)))
        return cps

    def start(ins, outs, sems):
        for cp in copies(ins, outs, sems):
            cp.start()

    def finish(ins, outs, sems):
        for cp in copies(ins, outs, sems):
            cp.wait()

    return _Hosted(list(grads), [jax.ShapeDtypeStruct(_halved(g.shape, ax[1]), g.dtype) for g, ax in zip(grads, axes)],
                   _dma_sems(n, n), [("first", start), ("last", finish)])


def _block_comm(sums, axes):
    n = len(sums)
    out_shapes = []
    for s, (ba, _) in zip(sums, axes):
        shp = list(s.shape)
        shp[ba] //= N_CHIPS
        out_shapes.append(jax.ShapeDtypeStruct((3,) + tuple(shp), s.dtype))

    def copies(ins, outs, sems):
        x, y, c, chips = _position()
        cps = []
        for t in range(n):
            ba = axes[t][0]
            bs = sums[t].shape[ba] // N_CHIPS
            for k, (px, py) in enumerate(chips):
                cps.append(_dma(_sub(ins[t], ba, (2 * px + py) * bs, bs), outs[t].at[k],
                                sems[0].at[3 * t + k], sems[1].at[3 * t + k], (px, py, c)))
        return cps

    def start(ins, outs, sems):
        for cp in copies(ins, outs, sems):
            cp.start()

    def finish(ins, outs, sems):
        for cp in copies(ins, outs, sems):
            cp.wait()

    return _Hosted(list(sums), out_shapes, _dma_sems(3 * n, 3 * n), [("first", start), ("last", finish)])


def _sibling_share(shards, axes):
    n = len(shards)

    def body(*refs):
        outs = refs[n:2 * n]
        ssem, rsem = refs[2 * n:]
        x, y, c, _ = _position()
        cps = []
        for t in range(n):
            ha = axes[t][1]
            hs = shards[t].shape[ha] // 2
            mine = _sub(outs[t], ha, c * hs, hs)
            cps.append(_dma(mine, mine, ssem.at[t], rsem.at[t], (x, y, 1 - c)))
        for cp in cps:
            cp.start()
        for cp in cps:
            cp.wait()

    return pl.pallas_call(
        body, name="sibling_share",
        in_specs=[ANY] * n, out_specs=[ANY] * n,
        out_shape=[jax.ShapeDtypeStruct(s.shape, s.dtype) for s in shards],
        input_output_aliases={t: t for t in range(n)},
        scratch_shapes=_dma_sems(n, n),
    )(*shards)


def _small_allreduce(p):
    R, C = p.shape

    def body(p_ref, o_ref, sib_buf, chip_sums, ssem, rsem):
        x, y, c, chips = _position()
        j = 2 * x + y
        sib = pltpu.make_async_remote_copy(
            src_ref=p_ref, dst_ref=sib_buf, send_sem=ssem.at[0], recv_sem=rsem.at[0],
            device_id=(x, y, 1 - c), device_id_type=MESH)
        sib.start()
        sib.wait()
        chip_sums[j] = p_ref[...] + sib_buf[...]
        cps = [pltpu.make_async_remote_copy(
            src_ref=chip_sums.at[j], dst_ref=chip_sums.at[j], send_sem=ssem.at[1 + k], recv_sem=rsem.at[1 + k],
            device_id=(px, py, c), device_id_type=MESH) for k, (px, py) in enumerate(chips)]
        for cp in cps:
            cp.start()
        for k, (px, py) in enumerate(chips):
            pltpu.make_async_remote_copy(
                src_ref=chip_sums.at[j], dst_ref=chip_sums.at[2 * px + py],
                send_sem=ssem.at[1 + k], recv_sem=rsem.at[1 + k],
                device_id=(px, py, c), device_id_type=MESH).wait()
        o_ref[...] = ((chip_sums[0] + chip_sums[1]) + chip_sums[2]) + chip_sums[3]

    vm = pl.BlockSpec(memory_space=pltpu.VMEM)
    return pl.pallas_call(
        body, name="small_allreduce",
        in_specs=[vm], out_specs=vm, out_shape=jax.ShapeDtypeStruct((R, C), F32),
        scratch_shapes=[pltpu.VMEM((R, C), F32), pltpu.VMEM((N_CHIPS, R, C), F32),
                        pltpu.SemaphoreType.DMA((4,)), pltpu.SemaphoreType.DMA((4,))],
        compiler_params=pltpu.CompilerParams(vmem_limit_bytes=VMEM_LIMIT_BYTES),
    )(p)


def _add_half(g, recv, half_axis, pos):
    B, R, C = recv.shape
    tr = _tile(R, 256)
    tc = _tile(C, 2816)
    nr, nc = R // tr, C // tc
    if half_axis == 1:
        g_map = lambda b, i, j, pos_ref: (b, pos_ref[0] * nr + i, j)
    else:
        g_map = lambda b, i, j, pos_ref: (b, i, pos_ref[0] * nc + j)

    def body(pos_ref, g_ref, r_ref, o_ref, ob_ref):
        del pos_ref
        s = g_ref[...] + r_ref[...]
        o_ref[...] = s
        ob_ref[...] = s.astype(BF16)

    spec = lambda: pl.BlockSpec((None, tr, tc), lambda b, i, j, pos_ref: (b, i, j))
    return pl.pallas_call(
        body, name="add_half",
        grid_spec=pltpu.PrefetchScalarGridSpec(
            num_scalar_prefetch=1, grid=(B, nr, nc),
            in_specs=[pl.BlockSpec((None, tr, tc), g_map), spec()],
            out_specs=[spec(), spec()]),
        out_shape=[jax.ShapeDtypeStruct((B, R, C), F32), jax.ShapeDtypeStruct((B, R, C), BF16)],
        compiler_params=_params("parallel", "parallel", "parallel"),
    )(pos, g, recv)


def _sum_half(own, recv, axes, pos):
    _, B, rs, cs = recv.shape
    ba, ha = axes
    tr = _tile(rs, 128)
    nr = rs // tr

    def body(pos_ref, o_ref, r0_ref, r1_ref, r2_ref, out_ref):
        del pos_ref
        out_ref[...] = (((o_ref[...] + r0_ref[...].astype(F32)) + r1_ref[...].astype(F32))
                        + r2_ref[...].astype(F32))

    if ba == 2:
        own_spec = pl.BlockSpec((None, tr, cs), lambda b, i, pos_ref: (b, i, pos_ref[1]))
    else:
        own_spec = pl.BlockSpec((None, tr, cs), lambda b, i, pos_ref: (b, pos_ref[1] * nr + i, 0))
    if ha == 1:
        out_shape, out_map = (B, 2 * rs, cs), (lambda b, i, pos_ref: (b, pos_ref[0] * nr + i, 0))
    else:
        out_shape, out_map = (B, rs, 2 * cs), (lambda b, i, pos_ref: (b, i, pos_ref[0]))
    rspec = lambda k: pl.BlockSpec((None, None, tr, cs), lambda b, i, pos_ref: (k, b, i, 0))
    return pl.pallas_call(
        body, name="sum_half",
        grid_spec=pltpu.PrefetchScalarGridSpec(
            num_scalar_prefetch=1, grid=(B, nr),
            in_specs=[own_spec, rspec(0), rspec(1), rspec(2)],
            out_specs=pl.BlockSpec((None, tr, cs), out_map)),
        out_shape=jax.ShapeDtypeStruct(out_shape, F32),
        compiler_params=_params("parallel", "parallel"),
    )(pos, own, recv, recv, recv)


def _adamw(w, g, m, v):
    shape = w.shape
    C = shape[-1]
    R = w.size // C
    tr = R
    for cand in (512, 256, 128, 64, 32, 16, 8):
        if R % cand == 0 and cand * C * 4 <= (1 << 20):
            tr = cand
            break
    bc1 = 1.0 - ADAM_B1 ** ADAM_STEP
    bc2 = 1.0 - ADAM_B2 ** ADAM_STEP

    def body(w_ref, g_ref, m_ref, v_ref, d_ref, mo_ref, vo_ref):
        gg = g_ref[...]
        mn = ADAM_B1 * m_ref[...] + (1.0 - ADAM_B1) * gg
        vn = ADAM_B2 * v_ref[...] + (1.0 - ADAM_B2) * (gg * gg)
        mo_ref[...] = mn
        vo_ref[...] = vn
        d_ref[...] = -ADAM_LR * ((mn / bc1) / (jnp.sqrt(vn / bc2) + ADAM_EPS) + ADAM_WD * w_ref[...])

    spec = lambda: pl.BlockSpec((tr, C), lambda i: (i, 0))
    outs = pl.pallas_call(
        body, name="adamw", grid=(R // tr,),
        in_specs=[spec()] * 4, out_specs=[spec()] * 3,
        out_shape=[jax.ShapeDtypeStruct((R, C), F32)] * 3,
        compiler_params=_params("parallel"),
    )(*[a.reshape(R, C) for a in (w, g, m, v)])
    return tuple(o.reshape(shape) for o in outs)


def kernel(x, mem, norm_g, mem_norm_g, w_in, gmlp_ln_g, gmlp_ln_b, w_s, b_s, conv_w, conv_b, conv_ln_g, conv_ln_b, w_kv, w_branch, w_out, final_norm_g, loss_target, m_norm_g, m_mem_norm_g, m_w_in, m_gmlp_ln_g, m_gmlp_ln_b, m_w_s, m_b_s, m_conv_w, m_conv_b, m_conv_ln_g, m_conv_ln_b, m_w_kv, m_w_branch, m_w_out, m_final_norm_g, v_norm_g, v_mem_norm_g, v_w_in, v_gmlp_ln_g, v_gmlp_ln_b, v_w_s, v_b_s, v_conv_w, v_conv_b, v_conv_ln_g, v_conv_ln_b, v_w_kv, v_w_branch, v_w_out, v_final_norm_g):
    xs, mems, tgt = x[0], mem[0], loss_target[0]
    S, D = xs.shape
    assert D // GROUPS == LANES and S % CHUNK == 0 and w_s.shape[-1] == CHUNK
    L = N_LAYERS

    assert L == 2
    shards = [w_in.astype(BF16)[:, None], w_kv.astype(BF16)[:, None], w_branch.astype(BF16), w_out.astype(BF16)[:, None],
              conv_w[:, None]]
    gather_axes = [(2, 1), (2, 1), (1, 2), (1, 2), (2, None)]
    reduce_axes = gather_axes[:4]
    win0 = _run_alone(_gather_comm(shards[:1], [0], gather_axes[:1]), "gather_w_in0")[0]
    gather_early = _gather_comm(shards[1:] + shards[:1], [0] * 4 + [1], gather_axes[1:] + gather_axes[:1])
    gather_late = _gather_comm(shards[1:], [1] * 4, gather_axes[1:])

    tri = jnp.tril(jnp.ones((CHUNK, CHUNK), F32))
    t_mix = (w_s * tri).astype(BF16)
    t_mix_t = jnp.swapaxes(t_mix, -1, -2)
    bias_full = jnp.repeat(jnp.swapaxes(b_s, -1, -2), D // GROUPS, axis=-1)
    row = lambda a, l: a[l][None, :]
    xi, yi, ci = lax.axis_index("x"), lax.axis_index("y"), lax.axis_index("c")
    pos = jnp.stack([ci, 2 * xi + yi]).astype(jnp.int32)

    saved = []
    h = xs
    weights = [None] * L
    for l in range(L):
        if l == 0:
            z, hb, fetched = _in_proj(h, row(norm_g, l), win0, 0, gather_early)
            weights[0] = [win0] + list(fetched[:4])
        else:
            z, hb, _ = _in_proj(h, row(norm_g, l), weights[l][0], 0)
        win, wkv, wb, wo, cw = weights[l]
        bra = _branch_a_fwd(z, row(gmlp_ln_g, l), row(gmlp_ln_b, l), t_mix[l], bias_full[l])
        c0, late = _conv_fwd(z, cw, row(conv_b, l), 0, gather_late if l == 0 else None)
        if l == 0:
            weights[1] = [fetched[4]] + list(late)
        brb = _branch_b_fwd(c0, z, row(conv_ln_g, l), row(conv_ln_b, l))
        mn, kv, kvt = _kv_fwd(mems, row(mem_norm_g, l), wkv, 0)
        brc = _branch_c_fwd(z, kv, kvt)
        merged, h_next = _merge_fwd(bra, brb, brc, z, h, wb[None], wo, 0)
        saved.append((h, z, hb, bra, c0, brb, mn, kv, kvt, brc, merged))
        h = h_next

    dx, loss_cols, d_final_g = _loss_bwd(h, tgt, final_norm_g[None, :])
    loss = lax.psum(0.5 * jnp.sum(loss_cols) / D, ("x", "y", "c"))

    def add_halves(grads, recv, axes):
        return [_add_half(g, r, ax[1], pos) for g, r, ax in zip(grads, recv, axes)]

    def finish_reduction(sums, blocks, axes):
        halves = [_sum_half(s, b, ax, pos) for (s, _), b, ax in zip(sums, blocks, axes)]
        return list(_sibling_share(halves, axes))

    small = {k: [None] * L for k in ("norm_g", "mem_norm_g", "gmlp_ln_g", "gmlp_ln_b", "w_s", "b_s", "conv_w",
                                     "conv_b", "conv_ln_g", "conv_ln_b")}
    early_axes = reduce_axes[1:]
    pending, reduced, early = None, [None] * L, None
    for l in reversed(range(L)):
        h_in, z, hb, bra, c0, brb, mn, kv, kvt, brc, merged = saved[l]
        win, wkv, wb, wo, cw = weights[l]
        dz_m, dproj, dbr, recv = _merge_bwd(dx, bra, brb, brc, z, wb[None], wo, 0,
                                            _pair_comm(pending, reduce_axes) if pending else None)
        sums = add_halves(pending, recv, reduce_axes) if pending else []
        g_o = _wgrad(merged, dx, (1, D, D), 0, 0, None)
        g_b = None
        for n, br in enumerate((bra, brb, brc)):
            g_b = _wgrad(br, dproj, (N_BRANCHES, D, D), n, 0, g_b, b_plane=n)
        dz_a, dws, dbs, dlg, dlb = _branch_a_bwd(z, dbr, row(gmlp_ln_g, l), row(gmlp_ln_b, l),
                                                t_mix[l], t_mix_t[l], bias_full[l], tri)
        dz_c, dk, dv = _branch_c_bwd(z, dbr, kv, kvt)
        g_kv, dmg = _kv_bwd(dk, dv, mn, mems, wkv, 0, None)
        dc0, dz_b, dcg, dcbeta, dcb = _branch_b_bwd(c0, z, dbr, row(conv_ln_g, l), row(conv_ln_b, l))
        ready = [g_kv, g_b, g_o]
        dz_b, dcw, recv_early = _conv_bwd(dc0, z, cw, dz_b, 0, _pair_comm(ready, early_axes) if pending else None)
        sums_early = add_halves(ready, recv_early, early_axes) if pending else []
        pieces = [(dz_a, 0, 3), (dz_b, 3, 3), (dz_c, 6, 2), (dz_m, 8, 3)]
        g_in = None
        for dzp, lo, _ in pieces:
            g_in = _wgrad(hb, dzp, (1, D, N_PLANES * D), 0, lo * D, g_in)
        if pending:
            riding = _block_comm([sb for _, sb in sums + sums_early], reduce_axes + early_axes)
            dx_half, dng_a, got = _in_proj_bwd(pieces, win, h_in, row(norm_g, l), dx, 0,
                                               _both(riding, _pair_comm([g_in], reduce_axes[:1])), (0, 2))
            reduced[l + 1] = finish_reduction(sums, got[:4], reduce_axes)
            early = finish_reduction(sums_early, got[4:7], early_axes)
            sums_in = add_halves([g_in], got[7:], reduce_axes[:1])
            dx, dng_b, blocks_in = _in_proj_bwd(pieces, win, h_in, row(norm_g, l), dx, 0,
                                                _block_comm([sb for _, sb in sums_in], reduce_axes[:1]), (1, 2), dx_half)
            dng = dng_a + dng_b
            reduced[l] = finish_reduction(sums_in, blocks_in, reduce_axes[:1]) + early
        else:
            dx, dng, _ = _in_proj_bwd(pieces, win, h_in, row(norm_g, l), dx, 0)
        pending = [g_in, g_kv, g_b, g_o]
        for k, val in (("norm_g", dng), ("mem_norm_g", dmg), ("gmlp_ln_g", dlg), ("gmlp_ln_b", dlb), ("w_s", dws),
                       ("b_s", dbs), ("conv_w", dcw), ("conv_b", dcb), ("conv_ln_g", dcg), ("conv_ln_b", dcbeta)):
            small[k][l] = val
    grad_x = dx[None]

    grad_w_in, grad_w_kv, grad_w_branch, grad_w_out = [
        jnp.stack([reduced[0][t], reduced[1][t]]).reshape(w.shape) for t, w in enumerate((w_in, w_kv, w_branch, w_out))]

    order = [("norm_g", norm_g.shape), ("mem_norm_g", mem_norm_g.shape), ("gmlp_ln_g", gmlp_ln_g.shape),
             ("gmlp_ln_b", gmlp_ln_b.shape), ("w_s", w_s.shape), ("b_s", b_s.shape),
             ("conv_w", (L, CONV_K, D)), ("conv_b", conv_b.shape), ("conv_ln_g", conv_ln_g.shape),
             ("conv_ln_b", conv_ln_b.shape)]
    parts = [jnp.stack([v.reshape(shp[1:]) for v in small[k]]).reshape(-1, LANES) for k, shp in order]
    parts.append(d_final_g.reshape(-1, LANES))
    sizes = [p.shape[0] for p in parts]
    assert all(s % 8 == 0 for s in sizes)
    total = _small_allreduce(jnp.concatenate(parts, axis=0))
    red, off = {}, 0
    for (k, shp), n in zip(order + [("final_norm_g", final_norm_g.shape)], sizes):
        red[k] = total[off:off + n].reshape(shp)
        off += n
    cs = conv_w.shape[-1]
    red["conv_w"] = lax.dynamic_slice_in_dim(red["conv_w"], (2 * xi + yi) * cs, cs, axis=2)
    red.update(w_in=grad_w_in, w_kv=grad_w_kv, w_branch=grad_w_branch, w_out=grad_w_out)

    names = ["norm_g", "mem_norm_g", "w_in", "gmlp_ln_g", "gmlp_ln_b", "w_s", "b_s", "conv_w", "conv_b",
             "conv_ln_g", "conv_ln_b", "w_kv", "w_branch", "w_out", "final_norm_g"]
    weights = dict(norm_g=norm_g, mem_norm_g=mem_norm_g, w_in=w_in, gmlp_ln_g=gmlp_ln_g, gmlp_ln_b=gmlp_ln_b,
                   w_s=w_s, b_s=b_s, conv_w=conv_w, conv_b=conv_b, conv_ln_g=conv_ln_g, conv_ln_b=conv_ln_b,
                   w_kv=w_kv, w_branch=w_branch, w_out=w_out, final_norm_g=final_norm_g)
    ms = dict(norm_g=m_norm_g, mem_norm_g=m_mem_norm_g, w_in=m_w_in, gmlp_ln_g=m_gmlp_ln_g, gmlp_ln_b=m_gmlp_ln_b,
              w_s=m_w_s, b_s=m_b_s, conv_w=m_conv_w, conv_b=m_conv_b, conv_ln_g=m_conv_ln_g, conv_ln_b=m_conv_ln_b,
              w_kv=m_w_kv, w_branch=m_w_branch, w_out=m_w_out, final_norm_g=m_final_norm_g)
    vs = dict(norm_g=v_norm_g, mem_norm_g=v_mem_norm_g, w_in=v_w_in, gmlp_ln_g=v_gmlp_ln_g, gmlp_ln_b=v_gmlp_ln_b,
              w_s=v_w_s, b_s=v_b_s, conv_w=v_conv_w, conv_b=v_conv_b, conv_ln_g=v_conv_ln_g, conv_ln_b=v_conv_ln_b,
              w_kv=v_w_kv, w_branch=v_w_branch, w_out=v_w_out, final_norm_g=v_final_norm_g)
    deltas, new_m, new_v = [], [], []
    for k in names:
        w2 = weights[k] if weights[k].ndim > 1 else weights[k][None, :]
        d, mo, vo = _adamw(w2, red[k].reshape(w2.shape), ms[k].reshape(w2.shape), vs[k].reshape(w2.shape))
        deltas.append(d.reshape(weights[k].shape))
        new_m.append(mo.reshape(weights[k].shape))
        new_v.append(vo.reshape(weights[k].shape))
    grads = [red[k].reshape(weights[k].shape) for k in names]
    return (loss, grad_x, *grads, *deltas, *new_m, *new_v)
```

```python
import math

import jax
import jax.numpy as jnp
from jax import lax
from jax.experimental import pallas as pl
from jax.experimental.pallas import tpu as pltpu

F32 = jnp.float32
BF16 = jnp.bfloat16

N_LAYERS = 2
N_BRANCHES = 3
N_PLANES = 11
N_CHIPS = 4
CHUNK = 128
GROUPS = 8
HEADS = 4
CONV_K = 31
HALO = 32
LANES = 128
RMS_EPS = 1e-6
LN_EPS = 1e-5
ADAM_LR, ADAM_B1, ADAM_B2, ADAM_EPS, ADAM_WD, ADAM_STEP = 0.001, 0.9, 0.999, 1e-08, 0.01, 10
VMEM_LIMIT_BYTES = 48 * 1024 * 1024
MESH = pl.DeviceIdType.MESH
ANY = pl.BlockSpec(memory_space=pl.ANY)


def _params(*sem):
    return pltpu.CompilerParams(dimension_semantics=sem, vmem_limit_bytes=VMEM_LIMIT_BYTES)


def _tile(n, pref):
    t = min(n, pref)
    assert n % t == 0, (n, t)
    return t


def _dot(a, b):
    return jnp.dot(a, b, preferred_element_type=F32)


def _dot_tb(a, b):
    return lax.dot_general(a, b, (((1,), (1,)), ((), ())), preferred_element_type=F32)


def _dot_ta(a, b):
    return lax.dot_general(a, b, (((0,), (0,)), ((), ())), preferred_element_type=F32)


def _sigmoid(x):
    return 1.0 / (1.0 + jnp.exp(-x))


def _silu(x):
    return x * _sigmoid(x)


def _silu_and_grad(x):
    s = _sigmoid(x)
    return x * s, s * (1.0 + x * (1.0 - s))


_GELU_C = math.sqrt(2.0 / math.pi)
_GELU_A = 0.044715


def _gelu(x):
    return x * (0.5 * (1.0 + jnp.tanh(_GELU_C * (x + _GELU_A * (x * x * x)))))


def _gelu_and_grad(x):
    x2 = x * x
    t = jnp.tanh(_GELU_C * (x + _GELU_A * (x2 * x)))
    cdf = 0.5 * (1.0 + t)
    dcdf = 0.5 * (1.0 - t * t) * (_GELU_C * (1.0 + 3.0 * _GELU_A * x2))
    return x * cdf, cdf + x * dcdf


def _rows(i, n):
    return pl.ds(pl.multiple_of(i * n, n), n)


def _grouped_loop(n, group, body, init):
    assert n % group == 0

    def trip(i, c):
        for u in range(group):
            c = body(i * group + u, c)
        return c
    return lax.fori_loop(0, n // group, trip, init)


def _mean(x):
    return jnp.mean(x, axis=-1, keepdims=True)


def _colsum(x):
    return jnp.sum(x, axis=0, keepdims=True)


def _tok(ts, d):
    return pl.BlockSpec((ts, d), lambda i: (i, 0))


def _zcol(ts, d, n):
    return pl.BlockSpec((ts, d), lambda i: (i, n))


def _row1(d):
    return pl.BlockSpec((1, d), lambda i: (0, 0))


def _plane(ts, d, n):
    return pl.BlockSpec((None, ts, d), lambda i: (n, i, 0))


class _Hosted:
    def __init__(self, ins, out_shapes, sem_shapes, phases):
        self.ins, self.out_shapes, self.sem_shapes, self.phases = ins, out_shapes, sem_shapes, phases


def _host(comm, n_in, n_out, n_scr, grid, body):
    if comm is None:
        return body, [], [], [], [], None
    ci, co = len(comm.ins), len(comm.out_shapes)
    total = math.prod(grid)
    at = {"first": 0, "late": (7 * total) // 8, "last": total - 1}

    def wrapped(*refs):
        ins, cin = refs[:n_in], refs[n_in:n_in + ci]
        o0 = n_in + ci
        outs, cout = refs[o0:o0 + n_out], refs[o0 + n_out:o0 + n_out + co]
        s0 = o0 + n_out + co
        scr, csem = refs[s0:s0 + n_scr], refs[s0 + n_scr:]
        step = pl.program_id(0)
        for a in range(1, len(grid)):
            step = step * grid[a] + pl.program_id(a)

        def run(where):
            for w, fn in comm.phases:
                if w == where:
                    @pl.when(step == at[where])
                    def _(fn=fn):
                        fn(cin, cout, csem)
        run("first")
        run("late")
        body(*ins, *outs, *scr)
        run("last")

    return wrapped, [ANY] * ci, [ANY] * co, list(comm.out_shapes), list(comm.sem_shapes), ("arbitrary",) * len(grid)


def _in_proj(x, g, w_all, l, comm=None):
    S, D = x.shape
    N = w_all.shape[2]
    tm, tn = _tile(S, 1024), _tile(N, 1408)
    grid = (S // tm, N // tn)

    def body(x_ref, g_ref, w_ref, z_ref, hb_ref, h_scr):
        @pl.when(pl.program_id(1) == 0)
        def _():
            def slab(i, c):
                r = _rows(i, 16)
                xs = x_ref[r, :]
                h = (xs * lax.rsqrt(_mean(xs * xs) + RMS_EPS)) * g_ref[...]
                hb = h.astype(BF16)
                h_scr[r, :] = hb
                hb_ref[r, :] = hb
                return c
            _grouped_loop(tm // 16, 4, slab, 0)
        z_ref[...] = _dot(h_scr[...], w_ref[...])

    body, c_in, c_out, c_shapes, c_scr, sem = _host(comm, 3, 2, 1, grid, body)
    outs = pl.pallas_call(
        body, name="in_proj", grid=grid,
        in_specs=[pl.BlockSpec((tm, D), lambda i, j: (i, 0)),
                  pl.BlockSpec((1, D), lambda i, j: (0, 0)),
                  pl.BlockSpec((None, D, tn), lambda i, j: (l, 0, j))] + c_in,
        out_specs=[pl.BlockSpec((tm, tn), lambda i, j: (i, j)),
                   pl.BlockSpec((tm, D), lambda i, j: (i, 0))] + c_out,
        out_shape=[jax.ShapeDtypeStruct((S, N), F32), jax.ShapeDtypeStruct((S, D), BF16)] + c_shapes,
        scratch_shapes=[pltpu.VMEM((tm, D), BF16)] + c_scr,
        compiler_params=_params(*(sem or ("parallel", "arbitrary"))),
    )(x, g, w_all, *(comm.ins if comm else []))
    return outs[0], outs[1], outs[2:]


def _branch_a_fwd(z, lng, lnb, t_mix, bias_full):
    S = z.shape[0]
    D = lng.shape[1]
    GD = D // GROUPS
    ts = _tile(S, 256)

    def body(zu_ref, zv_ref, zg_ref, lng_ref, lnb_ref, t_ref, bias_ref, a_ref, v_scr):
        def chunk(ci, carry):
            c0 = pl.multiple_of(ci * CHUNK, CHUNK)

            def slab(si, c):
                r = pl.multiple_of(si * 16, 16)
                vg = _gelu(zv_ref[pl.ds(pl.multiple_of(c0 + r, 16), 16), :])
                xc = vg - _mean(vg)
                y = xc * lax.rsqrt(_mean(xc * xc) + LN_EPS)
                v_scr[pl.ds(r, 16), :] = (y * lng_ref[...] + lnb_ref[...]).astype(BF16)
                return c
            _grouped_loop(CHUNK // 16, 4, slab, 0)
            for g in range(GROUPS):
                cols = slice(g * GD, (g + 1) * GD)
                sv = _dot(t_ref[g], v_scr[:, cols]) + bias_ref[:, cols]
                u = _gelu(zu_ref[pl.ds(c0, CHUNK), cols])
                a_ref[pl.ds(c0, CHUNK), cols] = ((u * sv) * _silu(zg_ref[pl.ds(c0, CHUNK), cols])).astype(BF16)
            return carry
        lax.fori_loop(0, ts // CHUNK, chunk, 0)

    return pl.pallas_call(
        body, name="branch_a_fwd", grid=(S // ts,),
        in_specs=[_zcol(ts, D, 0), _zcol(ts, D, 1), _zcol(ts, D, 2), _row1(D), _row1(D),
                  pl.BlockSpec((GROUPS, CHUNK, CHUNK), lambda i: (0, 0, 0)),
                  pl.BlockSpec((CHUNK, D), lambda i: (0, 0))],
        out_specs=_tok(ts, D),
        out_shape=jax.ShapeDtypeStruct((S, D), BF16),
        scratch_shapes=[pltpu.VMEM((CHUNK, D), BF16)],
        compiler_params=_params("parallel"),
    )(z, z, z, lng, lnb, t_mix, bias_full)


SH = HALO - CONV_K + 1


def _fill_shifts(sh_scr, src_scr, cols, ts):
    for j in range(1, 8):
        for g in range(ts // 32):
            sh_scr[j - 1, pl.ds(g * 32, 32), :] = src_scr[pl.ds(g * 32 + j, 32), cols]
        sh_scr[j - 1, pl.ds(ts, HALO - 8), :] = src_scr[pl.ds(ts + j, HALO - 8), cols]


def _shifted(sh_scr, src_scr, cols, row0, n):
    q, j = divmod(row0, 8)
    if j == 0:
        return src_scr[pl.ds(row0, n), cols]
    return sh_scr[j - 1, pl.ds(8 * q, n), :]


def _conv_fwd(z, w_all, cb, l, comm=None):
    S = z.shape[0]
    D = cb.shape[1]
    ts = _tile(S, 256)
    CB = 256
    hb = ts // HALO

    def body(za_ref, zb_ref, zap_ref, zbp_ref, w_ref, b_ref, c0_ref, glu_scr, sh_scr):
        i = pl.program_id(0)
        prev = zap_ref[...] * _sigmoid(zbp_ref[...])
        glu_scr[0:HALO, :] = jnp.where(i == 0, 0.0, prev)

        def fill(si, c):
            r = _rows(si, 32)
            glu_scr[pl.ds(pl.multiple_of(HALO + si * 32, 32), 32), :] = za_ref[r, :] * _sigmoid(zb_ref[r, :])
            return c
        _grouped_loop(ts // 32, 2, fill, 0)

        def colblock(ci, c):
            cols = _rows(ci, CB)
            _fill_shifts(sh_scr, glu_scr, cols, ts)
            for s in range(ts // 32):
                acc = jnp.zeros((32, CB), F32)
                for k in range(CONV_K):
                    acc = acc + _shifted(sh_scr, glu_scr, cols, s * 32 + k + SH, 32) * w_ref[k:k + 1, cols]
                c0_ref[pl.ds(s * 32, 32), cols] = acc + b_ref[:, cols]
            return c
        lax.fori_loop(0, D // CB, colblock, 0)

    prev_map = lambda n: (lambda i: (jnp.maximum(i * hb - 1, 0), n))
    grid = (S // ts,)
    body, c_in, c_out, c_shapes, c_scr, sem = _host(comm, 6, 1, 2, grid, body)
    outs = pl.pallas_call(
        body, name="conv_fwd", grid=grid,
        in_specs=[_zcol(ts, D, 3), _zcol(ts, D, 4),
                  pl.BlockSpec((HALO, D), prev_map(3)), pl.BlockSpec((HALO, D), prev_map(4)),
                  pl.BlockSpec((None, CONV_K, D), lambda i: (l, 0, 0)), _row1(D)] + c_in,
        out_specs=[_tok(ts, D)] + c_out,
        out_shape=[jax.ShapeDtypeStruct((S, D), F32)] + c_shapes,
        scratch_shapes=[pltpu.VMEM((ts + HALO, D), F32), pltpu.VMEM((7, ts + HALO, CB), F32)] + c_scr,
        compiler_params=_params(*(sem or ("parallel",))),
    )(z, z, z, z, w_all, cb, *(comm.ins if comm else []))
    return outs[0], outs[1:]


def _branch_b_fwd(c0, z, g, b):
    S, D = c0.shape
    ts = _tile(S, 512)

    def body(c0_ref, zg_ref, g_ref, b_ref, o_ref):
        def slab(si, c):
            r = _rows(si, 16)
            v = c0_ref[r, :]
            xc = v - _mean(v)
            c1 = (xc * lax.rsqrt(_mean(xc * xc) + LN_EPS)) * g_ref[...] + b_ref[...]
            o_ref[r, :] = (_silu(c1) * _silu(zg_ref[r, :])).astype(BF16)
            return c
        _grouped_loop(ts // 16, 4, slab, 0)

    return pl.pallas_call(
        body, name="branch_b_fwd", grid=(S // ts,),
        in_specs=[_tok(ts, D), _zcol(ts, D, 5), _row1(D), _row1(D)],
        out_specs=_tok(ts, D),
        out_shape=jax.ShapeDtypeStruct((S, D), BF16),
        compiler_params=_params("parallel"),
    )(c0, z, g, b)


def _kv_fwd(mem, mg, wkv_all, l):
    M, D = mem.shape

    def body(mem_ref, g_ref, w_ref, mn_ref, kv_ref, kvt_ref):
        m = mem_ref[...]
        mn = ((m * lax.rsqrt(_mean(m * m) + RMS_EPS)) * g_ref[...]).astype(BF16)
        mn_ref[...] = mn
        kv = _dot(mn, w_ref[...])
        kv_ref[...] = kv.astype(BF16)
        kvt_ref[...] = kv.T.astype(BF16)

    return pl.pallas_call(
        body, name="kv_fwd", grid=(1,),
        in_specs=[pl.BlockSpec((M, D), lambda i: (0, 0)), _row1(D),
                  pl.BlockSpec((None, D, 2 * D), lambda i: (l, 0, 0))],
        out_specs=[pl.BlockSpec((M, D), lambda i: (0, 0)), pl.BlockSpec((M, 2 * D), lambda i: (0, 0)),
                   pl.BlockSpec((2 * D, M), lambda i: (0, 0))],
        out_shape=[jax.ShapeDtypeStruct((M, D), BF16), jax.ShapeDtypeStruct((M, 2 * D), BF16),
                   jax.ShapeDtypeStruct((2 * D, M), BF16)],
        compiler_params=_params("arbitrary"),
    )(mem, mg, wkv_all)


def _softmax_rows(s):
    e = jnp.exp(s - jnp.max(s, axis=-1, keepdims=True))
    return e / jnp.sum(e, axis=-1, keepdims=True)


def _branch_c_fwd(z, kv, kvt):
    S = z.shape[0]
    M, D2 = kv.shape
    D = D2 // 2
    HD = D // HEADS
    scale = 1.0 / math.sqrt(HD)
    ts = _tile(S, 256)

    def body(q_ref, zg_ref, kt_ref, v_ref, o_ref):
        for h in range(HEADS):
            cols = slice(h * HD, (h + 1) * HD)
            p = _softmax_rows(_dot(q_ref[:, cols].astype(BF16), kt_ref[cols, :]) * scale)
            att = _dot(p.astype(BF16), v_ref[:, cols])
            o_ref[:, cols] = (att * _silu(zg_ref[:, cols])).astype(BF16)

    return pl.pallas_call(
        body, name="branch_c_fwd", grid=(S // ts,),
        in_specs=[_zcol(ts, D, 6), _zcol(ts, D, 7),
                  pl.BlockSpec((D, M), lambda i: (0, 0)), pl.BlockSpec((M, D), lambda i: (0, 1))],
        out_specs=_tok(ts, D),
        out_shape=jax.ShapeDtypeStruct((S, D), BF16),
        compiler_params=_params("parallel"),
    )(z, z, kvt, kv)


def _merge_fwd(bra, brb, brc, z, x, wb_all, wo_all, l):
    S, D = x.shape
    ts = _tile(S, 512)
    CB = 256

    def body(a_ref, b_ref, c_ref, zm0_ref, zm1_ref, zm2_ref, x_ref, wb_ref, wo_ref, m_ref, xo_ref):
        brs = (a_ref, b_ref, c_ref)
        zms = (zm0_ref, zm1_ref, zm2_ref)
        for cb in range(D // CB):
            cols = slice(cb * CB, (cb + 1) * CB)
            acc = None
            for n in range(N_BRANCHES):
                t = _sigmoid(zms[n][:, cols]) * _dot(brs[n][...], wb_ref[n, :, cols])
                acc = t if acc is None else acc + t
            m_ref[:, cols] = acc.astype(BF16)
        for cb in range(D // CB):
            cols = slice(cb * CB, (cb + 1) * CB)
            xo_ref[:, cols] = x_ref[:, cols] + _dot(m_ref[...], wo_ref[:, cols])

    return pl.pallas_call(
        body, name="merge_fwd", grid=(S // ts,),
        in_specs=[_tok(ts, D), _tok(ts, D), _tok(ts, D), _zcol(ts, D, 8), _zcol(ts, D, 9), _zcol(ts, D, 10),
                  _tok(ts, D),
                  pl.BlockSpec((None, N_BRANCHES, D, D), lambda i: (l, 0, 0, 0)),
                  pl.BlockSpec((None, D, D), lambda i: (l, 0, 0))],
        out_specs=[_tok(ts, D), _tok(ts, D)],
        out_shape=[jax.ShapeDtypeStruct((S, D), BF16), jax.ShapeDtypeStruct((S, D), F32)],
        compiler_params=_params("parallel"),
    )(bra, brb, brc, z, z, z, x, wb_all, wo_all)


def _loss_bwd(x, target, fg):
    S, D = x.shape
    ts = _tile(S, 512)

    def body(x_ref, t_ref, g_ref, dx_ref, ls_ref, dg_ref):
        @pl.when(pl.program_id(0) == 0)
        def _():
            ls_ref[...] = jnp.zeros_like(ls_ref)
            dg_ref[...] = jnp.zeros_like(dg_ref)

        def slab(si, c):
            ls, dg = c
            r = _rows(si, 16)
            xs = x_ref[r, :]
            rs = lax.rsqrt(_mean(xs * xs) + RMS_EPS)
            n = xs * rs
            e = n * g_ref[...] - t_ref[r, :]
            dy = e * (1.0 / D)
            dn = dy * g_ref[...]
            dx_ref[r, :] = rs * (dn - n * _mean(dn * n))
            return ls + _colsum(e * e), dg + _colsum(dy * n)
        zero = jnp.zeros((1, D), F32)
        ls, dg = _grouped_loop(ts // 16, 4, slab, (zero, zero))
        ls_ref[...] += ls
        dg_ref[...] += dg

    return pl.pallas_call(
        body, name="loss_bwd", grid=(S // ts,),
        in_specs=[_tok(ts, D), _tok(ts, D), _row1(D)],
        out_specs=[_tok(ts, D), _row1(D), _row1(D)],
        out_shape=[jax.ShapeDtypeStruct((S, D), F32), jax.ShapeDtypeStruct((1, D), F32),
                   jax.ShapeDtypeStruct((1, D), F32)],
        compiler_params=_params("arbitrary"),
    )(x, target, fg)


def _merge_bwd(dxp, bra, brb, brc, z, wb_all, wo_all, l, comm=None):
    S, D = dxp.shape
    ts = _tile(S, 256)
    CB = 256

    def body(d_ref, a_ref, b_ref, c_ref, zm0_ref, zm1_ref, zm2_ref, wb_ref, wo_ref,
             dzm_ref, dproj_ref, dbr_ref, dxb_scr):
        brs = (a_ref, b_ref, c_ref)
        zms = (zm0_ref, zm1_ref, zm2_ref)
        dxb_scr[...] = d_ref[...].astype(BF16)
        for cb in range(D // CB):
            cols = slice(cb * CB, (cb + 1) * CB)
            dm = _dot_tb(dxb_scr[...], wo_ref[cols, :])
            for n in range(N_BRANCHES):
                proj = _dot(brs[n][...], wb_ref[n, :, cols])
                g = _sigmoid(zms[n][:, cols])
                dzm_ref[:, n * D + cb * CB:n * D + (cb + 1) * CB] = ((dm * proj) * (g * (1.0 - g))).astype(BF16)
                dproj_ref[n, :, cols] = (dm * g).astype(BF16)
        for n in range(N_BRANCHES):
            for cb in range(D // CB):
                cols = slice(cb * CB, (cb + 1) * CB)
                dbr_ref[n, :, cols] = _dot_tb(dproj_ref[n], wb_ref[n, cols, :])

    tok3 = lambda: pl.BlockSpec((N_BRANCHES, ts, D), lambda i: (0, i, 0))
    grid = (S // ts,)
    body, c_in, c_out, c_shapes, c_scr, sem = _host(comm, 9, 3, 1, grid, body)
    outs = pl.pallas_call(
        body, name="merge_bwd", grid=grid,
        in_specs=[_tok(ts, D), _tok(ts, D), _tok(ts, D), _tok(ts, D),
                  _zcol(ts, D, 8), _zcol(ts, D, 9), _zcol(ts, D, 10),
                  pl.BlockSpec((None, N_BRANCHES, D, D), lambda i: (l, 0, 0, 0)),
                  pl.BlockSpec((None, D, D), lambda i: (l, 0, 0))] + c_in,
        out_specs=[_tok(ts, 3 * D), tok3(), tok3()] + c_out,
        out_shape=[jax.ShapeDtypeStruct((S, 3 * D), BF16), jax.ShapeDtypeStruct((N_BRANCHES, S, D), BF16),
                   jax.ShapeDtypeStruct((N_BRANCHES, S, D), F32)] + c_shapes,
        scratch_shapes=[pltpu.VMEM((ts, D), BF16)] + c_scr,
        compiler_params=_params(*(sem or ("parallel",))),
    )(dxp, bra, brb, brc, z, z, z, wb_all, wo_all, *(comm.ins if comm else []))
    return outs[0], outs[1], outs[2], outs[3:]


def _branch_a_bwd(z, dbr, lng, lnb, t_mix, t_mix_t, bias_full, mask):
    S = z.shape[0]
    D = lng.shape[1]
    GD = D // GROUPS
    ts = _tile(S, 256)
    nt = S // ts

    def body(zu_ref, zv_ref, zg_ref, d_ref, lng_ref, lnb_ref, t_ref, tt_ref, bias_ref, mask_ref,
             dz_ref, dws_ref, dbs_ref, dlg_ref, dlb_ref,
             v_scr, vn_scr, rstd_scr, dv_scr, dsv_acc):
        i = pl.program_id(0)

        @pl.when(i == 0)
        def _():
            dws_ref[...] = jnp.zeros_like(dws_ref)
            dlg_ref[...] = jnp.zeros_like(dlg_ref)
            dlb_ref[...] = jnp.zeros_like(dlb_ref)
            dsv_acc[...] = jnp.zeros_like(dsv_acc)

        def chunk(ci, carry):
            c0 = pl.multiple_of(ci * CHUNK, CHUNK)

            def ln_fwd(si, c):
                r = pl.multiple_of(si * 16, 16)
                vg = _gelu(zv_ref[pl.ds(pl.multiple_of(c0 + r, 16), 16), :])
                xc = vg - _mean(vg)
                rstd = lax.rsqrt(_mean(xc * xc) + LN_EPS)
                vn = xc * rstd
                vn_scr[pl.ds(r, 16), :] = vn
                rstd_scr[pl.ds(r, 16), :] = rstd
                v_scr[pl.ds(r, 16), :] = (vn * lng_ref[...] + lnb_ref[...]).astype(BF16)
                return c
            _grouped_loop(CHUNK // 16, 4, ln_fwd, 0)

            for g in range(GROUPS):
                cols = slice(g * GD, (g + 1) * GD)
                rows = pl.ds(c0, CHUNK)
                vb = v_scr[:, cols]
                sv = _dot(t_ref[g], vb) + bias_ref[:, cols]
                u, du = _gelu_and_grad(zu_ref[rows, cols])
                sg, dsg = _silu_and_grad(zg_ref[rows, cols])
                d = d_ref[rows, cols]
                dsv = (d * u) * sg
                dz_ref[rows, g * GD:(g + 1) * GD] = (((d * sv) * sg) * du).astype(BF16)
                dz_ref[rows, 2 * D + g * GD:2 * D + (g + 1) * GD] = (((d * u) * sv) * dsg).astype(BF16)
                dsvb = dsv.astype(BF16)
                dws_ref[g] += _dot_tb(dsvb, vb)
                dv_scr[:, cols] = _dot(tt_ref[g], dsvb)
                dsv_acc[:, cols] += dsv

            def ln_bwd(si, c):
                dlg, dlb = c
                r = pl.multiple_of(si * 16, 16)
                rr = pl.ds(r, 16)
                zrows = pl.ds(pl.multiple_of(c0 + r, 16), 16)
                dv = dv_scr[rr, :]
                vn = vn_scr[rr, :]
                dvn = dv * lng_ref[...]
                dvg = rstd_scr[rr, :] * ((dvn - _mean(dvn)) - vn * _mean(dvn * vn))
                _, gg = _gelu_and_grad(zv_ref[zrows, :])
                dz_ref[zrows, D:2 * D] = (dvg * gg).astype(BF16)
                return dlg + _colsum(dv * vn), dlb + _colsum(dv)
            zero = jnp.zeros((1, D), F32)
            dlg, dlb = _grouped_loop(CHUNK // 16, 4, ln_bwd, (zero, zero))
            dlg_ref[...] += dlg
            dlb_ref[...] += dlb
            return carry
        lax.fori_loop(0, ts // CHUNK, chunk, 0)

        @pl.when(i == nt - 1)
        def _():
            for g in range(GROUPS):
                cols = slice(g * GD, (g + 1) * GD)
                dws_ref[g] = dws_ref[g] * mask_ref[...]
                dbs_ref[g:g + 1, :] = _colsum(dsv_acc[:, cols].T)

    c2 = lambda shape: pl.BlockSpec(shape, lambda i: (0, 0))
    c3 = lambda: pl.BlockSpec((GROUPS, CHUNK, CHUNK), lambda i: (0, 0, 0))
    return pl.pallas_call(
        body, name="branch_a_bwd", grid=(nt,),
        in_specs=[_zcol(ts, D, 0), _zcol(ts, D, 1), _zcol(ts, D, 2), _plane(ts, D, 0), _row1(D), _row1(D),
                  c3(), c3(), c2((CHUNK, D)), c2((CHUNK, CHUNK))],
        out_specs=[_tok(ts, 3 * D), c3(), c2((GROUPS, CHUNK)), _row1(D), _row1(D)],
        out_shape=[jax.ShapeDtypeStruct((S, 3 * D), BF16), jax.ShapeDtypeStruct((GROUPS, CHUNK, CHUNK), F32),
                   jax.ShapeDtypeStruct((GROUPS, CHUNK), F32), jax.ShapeDtypeStruct((1, D), F32),
                   jax.ShapeDtypeStruct((1, D), F32)],
        scratch_shapes=[pltpu.VMEM((CHUNK, D), BF16), pltpu.VMEM((CHUNK, D), F32), pltpu.VMEM((CHUNK, 1), F32),
                        pltpu.VMEM((CHUNK, D), F32), pltpu.VMEM((CHUNK, D), F32)],
        compiler_params=_params("arbitrary"),
    )(z, z, z, dbr, lng, lnb, t_mix, t_mix_t, bias_full, mask)


def _branch_b_bwd(c0, z, dbr, g, b, comm=None):
    S, D = c0.shape
    ts = _tile(S, 512)

    def body(c0_ref, zg_ref, d_ref, g_ref, b_ref, dc0_ref, dz_ref, dg_ref, db_ref, dcb_ref):
        @pl.when(pl.program_id(0) == 0)
        def _():
            dg_ref[...] = jnp.zeros_like(dg_ref)
            db_ref[...] = jnp.zeros_like(db_ref)
            dcb_ref[...] = jnp.zeros_like(dcb_ref)

        def slab(si, c):
            dg, db, dcb = c
            r = _rows(si, 16)
            v = c0_ref[r, :]
            xc = v - _mean(v)
            rstd = lax.rsqrt(_mean(xc * xc) + LN_EPS)
            cn = xc * rstd
            c1 = cn * g_ref[...] + b_ref[...]
            c2, dc2_dc1 = _silu_and_grad(c1)
            sg, dsg = _silu_and_grad(zg_ref[r, :])
            d = d_ref[r, :]
            dz_ref[r, :] = ((d * c2) * dsg).astype(BF16)
            dc1 = (d * sg) * dc2_dc1
            dcn = dc1 * g_ref[...]
            dc0 = rstd * ((dcn - _mean(dcn)) - cn * _mean(dcn * cn))
            dc0_ref[r, :] = dc0
            return dg + _colsum(dc1 * cn), db + _colsum(dc1), dcb + _colsum(dc0)
        zero = jnp.zeros((1, D), F32)
        dg, db, dcb = _grouped_loop(ts // 16, 4, slab, (zero, zero, zero))
        dg_ref[...] += dg
        db_ref[...] += db
        dcb_ref[...] += dcb

    grid = (S // ts,)
    body, c_in, c_out, c_shapes, c_scr, _ = _host(comm, 5, 5, 0, grid, body)
    outs = pl.pallas_call(
        body, name="branch_b_bwd", grid=grid,
        in_specs=[_tok(ts, D), _zcol(ts, D, 5), _plane(ts, D, 1), _row1(D), _row1(D)] + c_in,
        out_specs=[_tok(ts, D), _zcol(ts, D, 2), _row1(D), _row1(D), _row1(D)] + c_out,
        out_shape=[jax.ShapeDtypeStruct((S, D), F32), jax.ShapeDtypeStruct((S, 3 * D), BF16),
                   jax.ShapeDtypeStruct((1, D), F32), jax.ShapeDtypeStruct((1, D), F32),
                   jax.ShapeDtypeStruct((1, D), F32)] + c_shapes,
        scratch_shapes=c_scr,
        compiler_params=_params("arbitrary"),
    )(c0, z, dbr, g, b, *(comm.ins if comm else []))
    return (*outs[:5], outs[5:])


def _conv_bwd(dc0, z, w_all, dz_b, l, comm=None):
    S, D = dc0.shape
    ts = _tile(S, 256)
    nt = S // ts
    CB = 256
    hb = ts // HALO
    nh = S // HALO

    def body(dc_ref, dcn_ref, za_ref, zb_ref, w_ref, dzin_ref, dz_ref, dw_ref, dc_scr, dw_acc, dsh_scr):
        del dzin_ref
        i = pl.program_id(0)

        @pl.when(i == 0)
        def _():
            dw_acc[...] = jnp.zeros_like(dw_acc)

        dc_scr[ts:ts + HALO, :] = jnp.where(i == nt - 1, 0.0, dcn_ref[...])

        def fill(si, c):
            r = _rows(si, 32)
            dc_scr[r, :] = dc_ref[r, :]
            return c
        _grouped_loop(ts // 32, 2, fill, 0)

        def colblock(ci, c):
            cols = _rows(ci, CB)
            cols_b = pl.ds(pl.multiple_of(D + ci * CB, CB), CB)
            _fill_shifts(dsh_scr, dc_scr, cols, ts)
            for s in range(ts // 32):
                rows = pl.ds(s * 32, 32)
                za = za_ref[rows, cols]
                sg = _sigmoid(zb_ref[rows, cols])
                glu = za * sg
                acc = jnp.zeros((32, CB), F32)
                for k in range(CONV_K):
                    dcs = _shifted(dsh_scr, dc_scr, cols, s * 32 + CONV_K - 1 - k, 32)
                    prod = glu * dcs
                    dw_acc[k, :, cols] += (prod[0:8] + prod[8:16]) + (prod[16:24] + prod[24:32])
                    acc = acc + dcs * w_ref[k:k + 1, cols]
                dz_ref[rows, cols] = (acc * sg).astype(BF16)
                dz_ref[rows, cols_b] = ((acc * za) * (sg * (1.0 - sg))).astype(BF16)
            return c
        lax.fori_loop(0, D // CB, colblock, 0)

        @pl.when(i == nt - 1)
        def _():
            for k in range(CONV_K):
                dw_ref[k:k + 1, :] = _colsum(dw_acc[k])

    grid = (nt,)
    body, c_in, c_out, c_shapes, c_scr, _ = _host(comm, 6, 2, 3, grid, body)
    outs = pl.pallas_call(
        body, name="conv_bwd", grid=grid,
        in_specs=[_tok(ts, D), pl.BlockSpec((HALO, D), lambda i: (jnp.minimum((i + 1) * hb, nh - 1), 0)),
                  _zcol(ts, D, 3), _zcol(ts, D, 4),
                  pl.BlockSpec((None, CONV_K, D), lambda i: (l, 0, 0)), ANY] + c_in,
        out_specs=[_tok(ts, 2 * D), pl.BlockSpec((CONV_K, D), lambda i: (0, 0))] + c_out,
        out_shape=[jax.ShapeDtypeStruct((S, 3 * D), BF16), jax.ShapeDtypeStruct((CONV_K, D), F32)] + c_shapes,
        scratch_shapes=[pltpu.VMEM((ts + HALO, D), F32), pltpu.VMEM((CONV_K, 8, D), F32),
                        pltpu.VMEM((7, ts + HALO, CB), F32)] + c_scr,
        input_output_aliases={5: 0},
        compiler_params=_params("arbitrary"),
    )(dc0, dc0, z, z, w_all, dz_b, *(comm.ins if comm else []))
    return outs[0], outs[1], outs[2:]


def _branch_c_bwd(z, dbr, kv, kvt):
    S = z.shape[0]
    M, D2 = kv.shape
    D = D2 // 2
    HD = D // HEADS
    scale = 1.0 / math.sqrt(HD)
    ts = _tile(S, 256)

    def body(q_ref, zg_ref, d_ref, kt_ref, vt_ref, k_ref, v_ref, dz_ref, dk_ref, dv_ref):
        @pl.when(pl.program_id(0) == 0)
        def _():
            dk_ref[...] = jnp.zeros_like(dk_ref)
            dv_ref[...] = jnp.zeros_like(dv_ref)

        for h in range(HEADS):
            cols = slice(h * HD, (h + 1) * HD)
            qb = q_ref[:, cols].astype(BF16)
            p = _softmax_rows(_dot(qb, kt_ref[cols, :]) * scale)
            pb = p.astype(BF16)
            att = _dot(pb, v_ref[:, cols])
            sg, dsg = _silu_and_grad(zg_ref[:, cols])
            d = d_ref[:, cols]
            dz_ref[:, D + h * HD:D + (h + 1) * HD] = ((d * att) * dsg).astype(BF16)
            datt = (d * sg).astype(BF16)
            dp = _dot(datt, vt_ref[cols, :])
            dv_ref[:, cols] += _dot_ta(pb, datt)
            ds = ((p * (dp - jnp.sum(dp * p, axis=-1, keepdims=True))) * scale).astype(BF16)
            dz_ref[:, cols] = _dot(ds, k_ref[:, cols]).astype(BF16)
            dk_ref[:, cols] += _dot_ta(ds, qb)

    return pl.pallas_call(
        body, name="branch_c_bwd", grid=(S // ts,),
        in_specs=[_zcol(ts, D, 6), _zcol(ts, D, 7), _plane(ts, D, 2),
                  pl.BlockSpec((D, M), lambda i: (0, 0)), pl.BlockSpec((D, M), lambda i: (1, 0)),
                  pl.BlockSpec((M, D), lambda i: (0, 0)), pl.BlockSpec((M, D), lambda i: (0, 1))],
        out_specs=[_tok(ts, 2 * D), pl.BlockSpec((M, D), lambda i: (0, 0)), pl.BlockSpec((M, D), lambda i: (0, 0))],
        out_shape=[jax.ShapeDtypeStruct((S, 2 * D), BF16), jax.ShapeDtypeStruct((M, D), F32),
                   jax.ShapeDtypeStruct((M, D), F32)],
        compiler_params=_params("arbitrary"),
    )(z, z, dbr, kvt, kvt, kv, kv)


def _kv_bwd(dk, dv, mn, mem, wkv_all, l, prev):
    M, D = mem.shape

    def body(*refs):
        dk_ref, dv_ref, mn_ref, mem_ref, w_ref = refs[:5]
        dw_ref, dg_ref = refs[-2:]
        dkb = dk_ref[...].astype(BF16)
        dvb = dv_ref[...].astype(BF16)
        dw_ref[:, 0:D] = _dot_ta(mn_ref[...], dkb)
        dw_ref[:, D:2 * D] = _dot_ta(mn_ref[...], dvb)
        dmn = _dot_tb(dkb, w_ref[:, 0:D]) + _dot_tb(dvb, w_ref[:, D:2 * D])
        m = mem_ref[...]
        dg_ref[...] = _colsum(dmn * (m * lax.rsqrt(_mean(m * m) + RMS_EPS)))

    full = lambda shape: pl.BlockSpec(shape, lambda i: (0, 0))
    in_specs = [full((M, D)), full((M, D)), full((M, D)), full((M, D)),
                pl.BlockSpec((None, D, 2 * D), lambda i: (l, 0, 0))]
    args = [dk, dv, mn, mem, wkv_all]
    aliases = {}
    if prev is not None:
        in_specs.append(ANY)
        args.append(prev)
        aliases = {5: 0}
    return pl.pallas_call(
        body, name="kv_bwd", grid=(1,),
        in_specs=in_specs,
        out_specs=[pl.BlockSpec((None, D, 2 * D), lambda i: (l, 0, 0)), _row1(D)],
        out_shape=[jax.ShapeDtypeStruct((wkv_all.shape[0], D, 2 * D), F32), jax.ShapeDtypeStruct((1, D), F32)],
        input_output_aliases=aliases,
        compiler_params=_params("arbitrary"),
    )(*args)


def _in_proj_bwd(pieces, w_all, x, g, dxp, l, comm=None, part=(0, 1), prev=None):
    S, D = x.shape
    tm = _tile(S, 1024)
    n_p = len(pieces)
    n_i = (S // tm) // part[1]
    i0 = part[0] * n_i
    n_in = n_p + 4 + (prev is not None)

    def body(*refs):
        dz_refs = refs[:n_p]
        w_ref, x_hbm, g_ref, dxp_hbm = refs[n_p:n_p + 4]
        dx_ref, dg_ref, acc, x_ref, dxp_ref, sem = refs[n_in:]
        i = pl.program_id(0)
        k = pl.program_id(1)
        rows = pl.ds(pl.multiple_of((i + i0) * tm, tm), tm)
        fetch = [pltpu.make_async_copy(x_hbm.at[rows, :], x_ref, sem.at[0]),
                 pltpu.make_async_copy(dxp_hbm.at[rows, :], dxp_ref, sem.at[1])]

        @pl.when(k == N_PLANES - 2)
        def _():
            for cp in fetch:
                cp.start()

        @pl.when(k == 0)
        def _():
            acc[...] = jnp.zeros_like(acc)

        @pl.when((i == 0) & (k == 0))
        def _():
            dg_ref[...] = jnp.zeros_like(dg_ref)

        for (_, lo, n), r in zip(pieces, dz_refs):
            @pl.when((k >= lo) & (k < lo + n))
            def _(r=r):
                acc[...] += _dot_tb(r[...], w_ref[...])

        @pl.when(k == N_PLANES - 1)
        def _():
            for cp in fetch:
                cp.wait()

            def slab(si, c):
                rr = _rows(si, 16)
                xs = x_ref[rr, :]
                rs = lax.rsqrt(_mean(xs * xs) + RMS_EPS)
                n = xs * rs
                dh = acc[rr, :]
                dn = dh * g_ref[...]
                dx_ref[rr, :] = rs * (dn - n * _mean(dn * n)) + dxp_ref[rr, :]
                return c + _colsum(dh * n)
            dg_ref[...] += _grouped_loop(tm // 16, 4, slab, jnp.zeros((1, D), F32))

    def piece_spec(lo, n):
        return pl.BlockSpec((tm, D), lambda i, k: (i + i0, jnp.clip(k - lo, 0, n - 1)))

    tokk = lambda: pl.BlockSpec((tm, D), lambda i, k: (i + i0, 0))
    rowk = lambda: pl.BlockSpec((1, D), lambda i, k: (0, 0))
    grid = (n_i, N_PLANES)
    body, c_in, c_out, c_shapes, c_scr, _ = _host(comm, n_in, 2, 4, grid, body)
    more_in, more_args, aliases = ([ANY], [prev], {n_in - 1: 0}) if prev is not None else ([], [], {})
    outs = pl.pallas_call(
        body, name="in_proj_bwd", grid=grid,
        in_specs=[piece_spec(lo, n) for _, lo, n in pieces]
        + [pl.BlockSpec((None, D, D), lambda i, k: (l, 0, k)), ANY, rowk(), ANY] + more_in + c_in,
        out_specs=[tokk(), rowk()] + c_out,
        out_shape=[jax.ShapeDtypeStruct((S, D), F32), jax.ShapeDtypeStruct((1, D), F32)] + c_shapes,
        scratch_shapes=[pltpu.VMEM((tm, D), F32), pltpu.VMEM((tm, D), F32), pltpu.VMEM((tm, D), F32),
                        pltpu.SemaphoreType.DMA((2,))] + c_scr,
        input_output_aliases=aliases,
        compiler_params=_params("arbitrary", "arbitrary"),
    )(*[p for p, _, _ in pieces], w_all, x, g, dxp, *more_args, *(comm.ins if comm else []))
    return outs[0], outs[1], outs[2:]


def _wgrad(a, b, out_shape, plane, col_off, prev, a_plane=None, b_plane=None):
    S = a.shape[-2]
    K1 = a.shape[-1]
    nb_cols = b.shape[-1]
    ts = _tile(S, 2048)
    tn = _tile(nb_cols, 1024)
    col_off = col_off // tn

    def body(*refs):
        a_ref, b_ref = refs[:2]
        o_ref = refs[-1]
        k = pl.program_id(1)
        prod = _dot_ta(a_ref[...].astype(BF16), b_ref[...].astype(BF16))

        @pl.when(k == 0)
        def _():
            o_ref[...] = prod

        @pl.when(k > 0)
        def _():
            o_ref[...] += prod

    if a_plane is None:
        a_spec = pl.BlockSpec((ts, K1), lambda j, k: (k, 0))
    else:
        a_spec = pl.BlockSpec((None, ts, K1), lambda j, k: (a_plane, k, 0))
    if b_plane is None:
        b_spec = pl.BlockSpec((ts, tn), lambda j, k: (k, j))
    else:
        b_spec = pl.BlockSpec((None, ts, tn), lambda j, k: (b_plane, k, j))
    in_specs, args, aliases = [a_spec, b_spec], [a, b], {}
    if prev is not None:
        in_specs.append(ANY)
        args.append(prev)
        aliases = {2: 0}
    return pl.pallas_call(
        body, name="wgrad", grid=(nb_cols // tn, S // ts),
        in_specs=in_specs,
        out_specs=pl.BlockSpec((None, K1, tn), lambda j, k: (plane, 0, col_off + j)),
        out_shape=jax.ShapeDtypeStruct(out_shape, F32),
        input_output_aliases=aliases,
        compiler_params=_params("parallel", "arbitrary"),
    )(*args)


def _position():
    x, y, c = lax.axis_index("x"), lax.axis_index("y"), lax.axis_index("c")
    chips = [(1 - x, y), (x, 1 - y), (1 - x, 1 - y)]
    return x, y, c, chips


def _sub(ref, axis, start, size):
    idx = [slice(None)] * len(ref.shape)
    idx[axis] = pl.ds(start, size)
    return ref.at[tuple(idx)]


def _dma(src, dst, ssem, rsem, to):
    return pltpu.make_async_remote_copy(src_ref=src, dst_ref=dst, send_sem=ssem, recv_sem=rsem,
                                        device_id=to, device_id_type=MESH)


def _dma_sems(*counts):
    return [pltpu.SemaphoreType.DMA((n,)) for n in counts]


def _both(a, b):
    na, oa, sa = len(a.ins), len(a.out_shapes), len(a.sem_shapes)
    phases = [(w, lambda i, o, s, fn=fn: fn(i[:na], o[:oa], s[:sa])) for w, fn in a.phases]
    phases += [(w, lambda i, o, s, fn=fn: fn(i[na:], o[oa:], s[sa:])) for w, fn in b.phases]
    return _Hosted(a.ins + b.ins, list(a.out_shapes) + list(b.out_shapes), list(a.sem_shapes) + list(b.sem_shapes), phases)


def _run_alone(comm, name):
    ci, co = len(comm.ins), len(comm.out_shapes)

    def body(*refs):
        for _, fn in comm.phases:
            fn(refs[:ci], refs[ci:ci + co], refs[ci + co:])

    return pl.pallas_call(
        body, name=name, in_specs=[ANY] * ci, out_specs=[ANY] * co, out_shape=list(comm.out_shapes),
        scratch_shapes=list(comm.sem_shapes),
    )(*comm.ins)


def _gather_comm(shards, layers, axes):
    n = len(shards)
    out_shapes = []
    for s, (ba, _) in zip(shards, axes):
        shp = list(s.shape[1:])
        shp[ba] *= N_CHIPS
        out_shapes.append(jax.ShapeDtypeStruct(tuple(shp), s.dtype))

    def half(ref, t, h):
        ha = axes[t][1]
        if ha is None:
            return ref
        hs = shards[t].shape[1 + ha] // 2
        return _sub(ref, ha, h * hs, hs)

    def block(outs, t, blk):
        ba = axes[t][0]
        bs = shards[t].shape[1 + ba]
        return _sub(outs[t], ba, blk * bs, bs)

    def start(ins, outs, sems):
        s_own, r_own, s_ici, r_ici, _, _ = sems
        x, y, c, chips = _position()
        j = 2 * x + y
        for t in range(n):
            _dma(ins[t].at[layers[t]], block(outs, t, j), s_own.at[t], r_own.at[t], (x, y, 1 - c)).start()
            for k, (px, py) in enumerate(chips):
                _dma(half(ins[t].at[layers[t]], t, c), half(block(outs, t, j), t, c),
                     s_ici.at[3 * t + k], r_ici.at[3 * t + k], (px, py, c)).start()

    def forward(ins, outs, sems):
        _, _, s_ici, r_ici, s_fwd, r_fwd = sems
        x, y, c, chips = _position()
        for t in range(n):
            if axes[t][1] is None:
                continue
            for k, (px, py) in enumerate(chips):
                got = half(block(outs, t, 2 * px + py), t, c)
                _dma(got, got, s_ici.at[3 * t + k], r_ici.at[3 * t + k], (px, py, c)).wait_recv()
                _dma(got, got, s_fwd.at[3 * t + k], r_fwd.at[3 * t + k], (x, y, 1 - c)).start()

    def finish(ins, outs, sems):
        s_own, r_own, s_ici, r_ici, s_fwd, r_fwd = sems
        x, y, c, chips = _position()
        j = 2 * x + y
        sib = (x, y, 1 - c)
        for t in range(n):
            for k, (px, py) in enumerate(chips):
                i = 3 * t + k
                sent = half(block(outs, t, j), t, c)
                _dma(sent, sent, s_ici.at[i], r_ici.at[i], (px, py, c)).wait_send()
                theirs = block(outs, t, 2 * px + py)
                if axes[t][1] is None:
                    _dma(theirs, theirs, s_ici.at[i], r_ici.at[i], (px, py, c)).wait_recv()
                else:
                    got, other = half(theirs, t, c), half(theirs, t, 1 - c)
                    _dma(got, got, s_fwd.at[i], r_fwd.at[i], sib).wait_send()
                    _dma(other, other, s_fwd.at[i], r_fwd.at[i], sib).wait_recv()
            _dma(ins[t].at[layers[t]], block(outs, t, j), s_own.at[t], r_own.at[t], sib).wait()

    return _Hosted(list(shards), out_shapes, _dma_sems(n, n, 3 * n, 3 * n, 3 * n, 3 * n),
                   [("first", start), ("late", forward), ("last", finish)])


def _halved(shape, axis):
    shp = list(shape)
    shp[axis] //= 2
    return tuple(shp)


def _pair_comm(grads, axes):
    n = len(grads)

    def copies(ins, outs, sems):
        x, y, c, _ = _position()
        cps = []
        for t in range(n):
            ha = axes[t][1]
            hs = grads[t].shape[ha] // 2
            cps.append(_dma(_sub(ins[t], ha, (1 - c) * hs, hs), outs[t], sems[0].at[t], sems[1].at[t], (x, y, 1 - c)))
        return cps

    def start(ins, outs, sems):
        for cp in copies(ins, outs, sems):
            cp.start()

    def finish(ins, outs, sems):
        for cp in copies(ins, outs, sems):
            cp.wait()

    return _Hosted(list(grads), [jax.ShapeDtypeStruct(_halved(g.shape, ax[1]), g.dtype) for g, ax in zip(grads, axes)],
                   _dma_sems(n, n), [("first", start), ("last", finish)])


def _block_comm(sums, axes):
    n = len(sums)
    out_shapes = []
    for s, (ba, _) in zip(sums, axes):
        shp = list(s.shape)
        shp[ba] //= N_CHIPS
        out_shapes.append(jax.ShapeDtypeStruct((3,) + tuple(shp), s.dtype))

    def copies(ins, outs, sems):
        x, y, c, chips = _position()
        cps = []
        for t in range(n):
            ba = axes[t][0]
            bs = sums[t].shape[ba] // N_CHIPS
            for k, (px, py) in enumerate(chips):
                cps.append(_dma(_sub(ins[t], ba, (2 * px + py) * bs, bs), outs[t].at[k],
                                sems[0].at[3 * t + k], sems[1].at[3 * t + k], (px, py, c)))
        return cps

    def start(ins, outs, sems):
        for cp in copies(ins, outs, sems):
            cp.start()

    def finish(ins, outs, sems):
        for cp in copies(ins, outs, sems):
            cp.wait()

    return _Hosted(list(sums), out_shapes, _dma_sems(3 * n, 3 * n), [("first", start), ("last", finish)])


def _sibling_share(shards, axes):
    n = len(shards)

    def body(*refs):
        outs = refs[n:2 * n]
        ssem, rsem = refs[2 * n:]
        x, y, c, _ = _position()
        cps = []
        for t in range(n):
            ha = axes[t][1]
            hs = shards[t].shape[ha] // 2
            mine = _sub(outs[t], ha, c * hs, hs)
            cps.append(_dma(mine, mine, ssem.at[t], rsem.at[t], (x, y, 1 - c)))
        for cp in cps:
            cp.start()
        for cp in cps:
            cp.wait()

    return pl.pallas_call(
        body, name="sibling_share",
        in_specs=[ANY] * n, out_specs=[ANY] * n,
        out_shape=[jax.ShapeDtypeStruct(s.shape, s.dtype) for s in shards],
        input_output_aliases={t: t for t in range(n)},
        scratch_shapes=_dma_sems(n, n),
    )(*shards)


def _small_allreduce(p):
    R, C = p.shape

    def body(p_ref, o_ref, sib_buf, chip_sums, ssem, rsem):
        x, y, c, chips = _position()
        j = 2 * x + y
        sib = pltpu.make_async_remote_copy(
            src_ref=p_ref, dst_ref=sib_buf, send_sem=ssem.at[0], recv_sem=rsem.at[0],
            device_id=(x, y, 1 - c), device_id_type=MESH)
        sib.start()
        sib.wait()
        chip_sums[j] = p_ref[...] + sib_buf[...]
        cps = [pltpu.make_async_remote_copy(
            src_ref=chip_sums.at[j], dst_ref=chip_sums.at[j], send_sem=ssem.at[1 + k], recv_sem=rsem.at[1 + k],
            device_id=(px, py, c), device_id_type=MESH) for k, (px, py) in enumerate(chips)]
        for cp in cps:
            cp.start()
        for k, (px, py) in enumerate(chips):
            pltpu.make_async_remote_copy(
                src_ref=chip_sums.at[j], dst_ref=chip_sums.at[2 * px + py],
                send_sem=ssem.at[1 + k], recv_sem=rsem.at[1 + k],
                device_id=(px, py, c), device_id_type=MESH).wait()
        o_ref[...] = ((chip_sums[0] + chip_sums[1]) + chip_sums[2]) + chip_sums[3]

    vm = pl.BlockSpec(memory_space=pltpu.VMEM)
    return pl.pallas_call(
        body, name="small_allreduce",
        in_specs=[vm], out_specs=vm, out_shape=jax.ShapeDtypeStruct((R, C), F32),
        scratch_shapes=[pltpu.VMEM((R, C), F32), pltpu.VMEM((N_CHIPS, R, C), F32),
                        pltpu.SemaphoreType.DMA((4,)), pltpu.SemaphoreType.DMA((4,))],
        compiler_params=pltpu.CompilerParams(vmem_limit_bytes=VMEM_LIMIT_BYTES),
    )(p)


def _add_half(g, recv, half_axis, pos):
    B, R, C = recv.shape
    tr = _tile(R, 256)
    tc = _tile(C, 2816)
    nr, nc = R // tr, C // tc
    if half_axis == 1:
        g_map = lambda b, i, j, pos_ref: (b, pos_ref[0] * nr + i, j)
    else:
        g_map = lambda b, i, j, pos_ref: (b, i, pos_ref[0] * nc + j)

    def body(pos_ref, g_ref, r_ref, o_ref, ob_ref):
        del pos_ref
        s = g_ref[...] + r_ref[...]
        o_ref[...] = s
        ob_ref[...] = s.astype(BF16)

    spec = lambda: pl.BlockSpec((None, tr, tc), lambda b, i, j, pos_ref: (b, i, j))
    return pl.pallas_call(
        body, name="add_half",
        grid_spec=pltpu.PrefetchScalarGridSpec(
            num_scalar_prefetch=1, grid=(B, nr, nc),
            in_specs=[pl.BlockSpec((None, tr, tc), g_map), spec()],
            out_specs=[spec(), spec()]),
        out_shape=[jax.ShapeDtypeStruct((B, R, C), F32), jax.ShapeDtypeStruct((B, R, C), BF16)],
        compiler_params=_params("parallel", "parallel", "parallel"),
    )(pos, g, recv)


def _sum_half(own, recv, axes, pos):
    _, B, rs, cs = recv.shape
    ba, ha = axes
    tr = _tile(rs, 128)
    nr = rs // tr

    def body(pos_ref, o_ref, r0_ref, r1_ref, r2_ref, out_ref):
        del pos_ref
        out_ref[...] = (((o_ref[...] + r0_ref[...].astype(F32)) + r1_ref[...].astype(F32))
                        + r2_ref[...].astype(F32))

    if ba == 2:
        own_spec = pl.BlockSpec((None, tr, cs), lambda b, i, pos_ref: (b, i, pos_ref[1]))
    else:
        own_spec = pl.BlockSpec((None, tr, cs), lambda b, i, pos_ref: (b, pos_ref[1] * nr + i, 0))
    if ha == 1:
        out_shape, out_map = (B, 2 * rs, cs), (lambda b, i, pos_ref: (b, pos_ref[0] * nr + i, 0))
    else:
        out_shape, out_map = (B, rs, 2 * cs), (lambda b, i, pos_ref: (b, i, pos_ref[0]))
    rspec = lambda k: pl.BlockSpec((None, None, tr, cs), lambda b, i, pos_ref: (k, b, i, 0))
    return pl.pallas_call(
        body, name="sum_half",
        grid_spec=pltpu.PrefetchScalarGridSpec(
            num_scalar_prefetch=1, grid=(B, nr),
            in_specs=[own_spec, rspec(0), rspec(1), rspec(2)],
            out_specs=pl.BlockSpec((None, tr, cs), out_map)),
        out_shape=jax.ShapeDtypeStruct(out_shape, F32),
        compiler_params=_params("parallel", "parallel"),
    )(pos, own, recv, recv, recv)


def _adamw(w, g, m, v):
    shape = w.shape
    C = shape[-1]
    R = w.size // C
    tr = R
    for cand in (512, 256, 128, 64, 32, 16, 8):
        if R % cand == 0 and cand * C * 4 <= (1 << 20):
            tr = cand
            break
    bc1 = 1.0 - ADAM_B1 ** ADAM_STEP
    bc2 = 1.0 - ADAM_B2 ** ADAM_STEP

    def body(w_ref, g_ref, m_ref, v_ref, d_ref, mo_ref, vo_ref):
        gg = g_ref[...]
        mn = ADAM_B1 * m_ref[...] + (1.0 - ADAM_B1) * gg
        vn = ADAM_B2 * v_ref[...] + (1.0 - ADAM_B2) * (gg * gg)
        mo_ref[...] = mn
        vo_ref[...] = vn
        d_ref[...] = -ADAM_LR * ((mn / bc1) / (jnp.sqrt(vn / bc2) + ADAM_EPS) + ADAM_WD * w_ref[...])

    spec = lambda: pl.BlockSpec((tr, C), lambda i: (i, 0))
    outs = pl.pallas_call(
        body, name="adamw", grid=(R // tr,),
        in_specs=[spec()] * 4, out_specs=[spec()] * 3,
        out_shape=[jax.ShapeDtypeStruct((R, C), F32)] * 3,
        compiler_params=_params("parallel"),
    )(*[a.reshape(R, C) for a in (w, g, m, v)])
    return tuple(o.reshape(shape) for o in outs)


def kernel(x, mem, norm_g, mem_norm_g, w_in, gmlp_ln_g, gmlp_ln_b, w_s, b_s, conv_w, conv_b, conv_ln_g, conv_ln_b, w_kv, w_branch, w_out, final_norm_g, loss_target, m_norm_g, m_mem_norm_g, m_w_in, m_gmlp_ln_g, m_gmlp_ln_b, m_w_s, m_b_s, m_conv_w, m_conv_b, m_conv_ln_g, m_conv_ln_b, m_w_kv, m_w_branch, m_w_out, m_final_norm_g, v_norm_g, v_mem_norm_g, v_w_in, v_gmlp_ln_g, v_gmlp_ln_b, v_w_s, v_b_s, v_conv_w, v_conv_b, v_conv_ln_g, v_conv_ln_b, v_w_kv, v_w_branch, v_w_out, v_final_norm_g):
    xs, mems, tgt = x[0], mem[0], loss_target[0]
    S, D = xs.shape
    assert D // GROUPS == LANES and S % CHUNK == 0 and w_s.shape[-1] == CHUNK
    L = N_LAYERS

    assert L == 2
    shards = [w_in.astype(BF16)[:, None], w_kv.astype(BF16)[:, None], w_branch.astype(BF16), w_out.astype(BF16)[:, None],
              conv_w[:, None]]
    gather_axes = [(2, 1), (2, 1), (1, 2), (1, 2), (2, None)]
    reduce_axes = gather_axes[:4]
    win0 = _run_alone(_gather_comm(shards[:1], [0], gather_axes[:1]), "gather_w_in0")[0]
    gather_early = _gather_comm(shards[1:] + shards[:1], [0] * 4 + [1], gather_axes[1:] + gather_axes[:1])
    gather_late = _gather_comm(shards[1:], [1] * 4, gather_axes[1:])

    tri = jnp.tril(jnp.ones((CHUNK, CHUNK), F32))
    t_mix = (w_s * tri).astype(BF16)
    t_mix_t = jnp.swapaxes(t_mix, -1, -2)
    bias_full = jnp.repeat(jnp.swapaxes(b_s, -1, -2), D // GROUPS, axis=-1)
    row = lambda a, l: a[l][None, :]
    xi, yi, ci = lax.axis_index("x"), lax.axis_index("y"), lax.axis_index("c")
    pos = jnp.stack([ci, 2 * xi + yi]).astype(jnp.int32)

    saved = []
    h = xs
    weights = [None] * L
    for l in range(L):
        if l == 0:
            z, hb, fetched = _in_proj(h, row(norm_g, l), win0, 0, gather_early)
            weights[0] = [win0] + list(fetched[:4])
        else:
            z, hb, _ = _in_proj(h, row(norm_g, l), weights[l][0], 0)
        win, wkv, wb, wo, cw = weights[l]
        bra = _branch_a_fwd(z, row(gmlp_ln_g, l), row(gmlp_ln_b, l), t_mix[l], bias_full[l])
        c0, late = _conv_fwd(z, cw, row(conv_b, l), 0, gather_late if l == 0 else None)
        if l == 0:
            weights[1] = [fetched[4]] + list(late)
        brb = _branch_b_fwd(c0, z, row(conv_ln_g, l), row(conv_ln_b, l))
        mn, kv, kvt = _kv_fwd(mems, row(mem_norm_g, l), wkv, 0)
        brc = _branch_c_fwd(z, kv, kvt)
        merged, h_next = _merge_fwd(bra, brb, brc, z, h, wb[None], wo, 0)
        saved.append((h, z, hb, bra, c0, brb, mn, kv, kvt, brc, merged))
        h = h_next

    dx, loss_cols, d_final_g = _loss_bwd(h, tgt, final_norm_g[None, :])
    loss = lax.psum(0.5 * jnp.sum(loss_cols) / D, ("x", "y", "c"))

    def add_halves(grads, recv, axes):
        return [_add_half(g, r, ax[1], pos) for g, r, ax in zip(grads, recv, axes)]

    def finish_reduction(sums, blocks, axes):
        halves = [_sum_half(s, b, ax, pos) for (s, _), b, ax in zip(sums, blocks, axes)]
        return list(_sibling_share(halves, axes))

    small = {k: [None] * L for k in ("norm_g", "mem_norm_g", "gmlp_ln_g", "gmlp_ln_b", "w_s", "b_s", "conv_w",
                                     "conv_b", "conv_ln_g", "conv_ln_b")}
    early_axes = reduce_axes[1:]
    pending, reduced, early = None, [None] * L, None
    for l in reversed(range(L)):
        h_in, z, hb, bra, c0, brb, mn, kv, kvt, brc, merged = saved[l]
        win, wkv, wb, wo, cw = weights[l]
        dz_m, dproj, dbr, recv = _merge_bwd(dx, bra, brb, brc, z, wb[None], wo, 0,
                                            _pair_comm(pending, reduce_axes) if pending else None)
        sums = add_halves(pending, recv, reduce_axes) if pending else []
        g_o = _wgrad(merged, dx, (1, D, D), 0, 0, None)
        g_b = None
        for n, br in enumerate((bra, brb, brc)):
            g_b = _wgrad(br, dproj, (N_BRANCHES, D, D), n, 0, g_b, b_plane=n)
        dz_a, dws, dbs, dlg, dlb = _branch_a_bwd(z, dbr, row(gmlp_ln_g, l), row(gmlp_ln_b, l),
                                                t_mix[l], t_mix_t[l], bias_full[l], tri)
        dz_c, dk, dv = _branch_c_bwd(z, dbr, kv, kvt)
        g_kv, dmg = _kv_bwd(dk, dv, mn, mems, wkv, 0, None)
        ready = [g_kv, g_b, g_o]
        dc0, dz_b, dcg, dcbeta, dcb, recv_early = _branch_b_bwd(c0, z, dbr, row(conv_ln_g, l), row(conv_ln_b, l),
                                                                _pair_comm(ready, early_axes) if pending else None)
        sums_early = add_halves(ready, recv_early, early_axes) if pending else []
        dz_b, dcw, blocks = _conv_bwd(dc0, z, cw, dz_b, 0,
                                      _block_comm([sb for _, sb in sums], reduce_axes) if pending else None)
        if pending:
            reduced[l + 1] = finish_reduction(sums, blocks, reduce_axes)
        pieces =[(dz_a, 0, 3), (dz_b, 3, 3), (dz_c, 6, 2), (dz_m, 8, 3)]
        g_in = None
        for dzp, lo, _ in pieces:
            g_in = _wgrad(hb, dzp, (1, D, N_PLANES * D), 0, lo * D, g_in)
        if pending:
            riding = _block_comm([sb for _, sb in sums_early], early_axes)
            dx_half, dng_a, got = _in_proj_bwd(pieces, win, h_in, row(norm_g, l), dx, 0,
                                               _both(riding, _pair_comm([g_in], reduce_axes[:1])), (0, 2))
            early = finish_reduction(sums_early, got[:3], early_axes)
            sums_in = add_halves([g_in], got[3:], reduce_axes[:1])
            dx, dng_b, blocks_in = _in_proj_bwd(pieces, win, h_in, row(norm_g, l), dx, 0,
                                                _block_comm([sb for _, sb in sums_in], reduce_axes[:1]), (1, 2), dx_half)
            dng = dng_a + dng_b
            reduced[l] = finish_reduction(sums_in, blocks_in, reduce_axes[:1]) + early
        else:
            dx, dng, _ = _in_proj_bwd(pieces, win, h_in, row(norm_g, l), dx, 0)
        pending = [g_in, g_kv, g_b, g_o]
        for k, val in (("norm_g", dng), ("mem_norm_g", dmg), ("gmlp_ln_g", dlg), ("gmlp_ln_b", dlb), ("w_s", dws),
                       ("b_s", dbs), ("conv_w", dcw), ("conv_b", dcb), ("conv_ln_g", dcg), ("conv_ln_b", dcbeta)):
            small[k][l] = val
    grad_x = dx[None]

    grad_w_in, grad_w_kv, grad_w_branch, grad_w_out = [
        jnp.stack([reduced[0][t], reduced[1][t]]).reshape(w.shape) for t, w in enumerate((w_in, w_kv, w_branch, w_out))]

    order = [("norm_g", norm_g.shape), ("mem_norm_g", mem_norm_g.shape), ("gmlp_ln_g", gmlp_ln_g.shape),
             ("gmlp_ln_b", gmlp_ln_b.shape), ("w_s", w_s.shape), ("b_s", b_s.shape),
             ("conv_w", (L, CONV_K, D)), ("conv_b", conv_b.shape), ("conv_ln_g", conv_ln_g.shape),
             ("conv_ln_b", conv_ln_b.shape)]
    parts = [jnp.stack([v.reshape(shp[1:]) for v in small[k]]).reshape(-1, LANES) for k, shp in order]
    parts.append(d_final_g.reshape(-1, LANES))
    sizes = [p.shape[0] for p in parts]
    assert all(s % 8 == 0 for s in sizes)
    total = _small_allreduce(jnp.concatenate(parts, axis=0))
    red, off = {}, 0
    for (k, shp), n in zip(order + [("final_norm_g", final_norm_g.shape)], sizes):
        red[k] = total[off:off + n].reshape(shp)
        off += n
    cs = conv_w.shape[-1]
    red["conv_w"] = lax.dynamic_slice_in_dim(red["conv_w"], (2 * xi + yi) * cs, cs, axis=2)
    red.update(w_in=grad_w_in, w_kv=grad_w_kv, w_branch=grad_w_branch, w_out=grad_w_out)

    names = ["norm_g", "mem_norm_g", "w_in", "gmlp_ln_g", "gmlp_ln_b", "w_s", "b_s", "conv_w", "conv_b",
             "conv_ln_g", "conv_ln_b", "w_kv", "w_branch", "w_out", "final_norm_g"]
    weights = dict(norm_g=norm_g, mem_norm_g=mem_norm_g, w_in=w_in, gmlp_ln_g=gmlp_ln_g, gmlp_ln_b=gmlp_ln_b,
                   w_s=w_s, b_s=b_s, conv_w=conv_w, conv_b=conv_b, conv_ln_g=conv_ln_g, conv_ln_b=conv_ln_b,
                   w_kv=w_kv, w_branch=w_branch, w_out=w_out, final_norm_g=final_norm_g)
    ms = dict(norm_g=m_norm_g, mem_norm_g=m_mem_norm_g, w_in=m_w_in, gmlp_ln_g=m_gmlp_ln_g, gmlp_ln_b=m_gmlp_ln_b,
              w_s=m_w_s, b_s=m_b_s, conv_w=m_conv_w, conv_b=m_conv_b, conv_ln_g=m_conv_ln_g, conv_ln_b=m_conv_ln_b,
              w_kv=m_w_kv, w_branch=m_w_branch, w_out=m_w_out, final_norm_g=m_final_norm_g)
    vs = dict(norm_g=v_norm_g, mem_norm_g=v_mem_norm_g, w_in=v_w_in, gmlp_ln_g=v_gmlp_ln_g, gmlp_ln_b=v_gmlp_ln_b,
              w_s=v_w_s, b_s=v_b_s, conv_w=v_conv_w, conv_b=v_conv_b, conv_ln_g=v_conv_ln_g, conv_ln_b=v_conv_ln_b,
              w_kv=v_w_kv, w_branch=v_w_branch, w_out=v_w_out, final_norm_g=v_final_norm_g)
    deltas, new_m, new_v = [], [], []
    for k in names:
        w2 = weights[k] if weights[k].ndim > 1 else weights[k][None, :]
        d, mo, vo = _adamw(w2, red[k].reshape(w2.shape), ms[k].reshape(w2.shape), vs[k].reshape(w2.shape))
        deltas.append(d.reshape(weights[k].shape))
        new_m.append(mo.reshape(weights[k].shape))
        new_v.append(vo.reshape(weights[k].shape))
    grads = [red[k].reshape(weights[k].shape) for k in names]
    return (loss, grad_x, *grads, *deltas, *new_m, *new_v)
```

```python
import math

import jax
import jax.numpy as jnp
from jax import lax
from jax.experimental import pallas as pl
from jax.experimental.pallas import tpu as pltpu

F32 = jnp.float32
BF16 = jnp.bfloat16

N_LAYERS = 2
N_BRANCHES = 3
N_PLANES = 11
N_CHIPS = 4
CHUNK = 128
GROUPS = 8
HEADS = 4
CONV_K = 31
HALO = 32
LANES = 128
RMS_EPS = 1e-6
LN_EPS = 1e-5
ADAM_LR, ADAM_B1, ADAM_B2, ADAM_EPS, ADAM_WD, ADAM_STEP = 0.001, 0.9, 0.999, 1e-08, 0.01, 10
VMEM_LIMIT_BYTES = 48 * 1024 * 1024
MESH = pl.DeviceIdType.MESH
ANY = pl.BlockSpec(memory_space=pl.ANY)


def _params(*sem):
    return pltpu.CompilerParams(dimension_semantics=sem, vmem_limit_bytes=VMEM_LIMIT_BYTES)


def _tile(n, pref):
    t = min(n, pref)
    assert n % t == 0, (n, t)
    return t


def _dot(a, b):
    return jnp.dot(a, b, preferred_element_type=F32)


def _dot_tb(a, b):
    return lax.dot_general(a, b, (((1,), (1,)), ((), ())), preferred_element_type=F32)


def _dot_ta(a, b):
    return lax.dot_general(a, b, (((0,), (0,)), ((), ())), preferred_element_type=F32)


def _sigmoid(x):
    return 1.0 / (1.0 + jnp.exp(-x))


def _silu(x):
    return x * _sigmoid(x)


def _silu_and_grad(x):
    s = _sigmoid(x)
    return x * s, s * (1.0 + x * (1.0 - s))


_GELU_C = math.sqrt(2.0 / math.pi)
_GELU_A = 0.044715


def _gelu(x):
    return x * (0.5 * (1.0 + jnp.tanh(_GELU_C * (x + _GELU_A * (x * x * x)))))


def _gelu_and_grad(x):
    x2 = x * x
    t = jnp.tanh(_GELU_C * (x + _GELU_A * (x2 * x)))
    cdf = 0.5 * (1.0 + t)
    dcdf = 0.5 * (1.0 - t * t) * (_GELU_C * (1.0 + 3.0 * _GELU_A * x2))
    return x * cdf, cdf + x * dcdf


def _rows(i, n):
    return pl.ds(pl.multiple_of(i * n, n), n)


def _grouped_loop(n, group, body, init):
    assert n % group == 0

    def trip(i, c):
        for u in range(group):
            c = body(i * group + u, c)
        return c
    return lax.fori_loop(0, n // group, trip, init)


def _mean(x):
    return jnp.mean(x, axis=-1, keepdims=True)


def _colsum(x):
    return jnp.sum(x, axis=0, keepdims=True)


def _tok(ts, d):
    return pl.BlockSpec((ts, d), lambda i: (i, 0))


def _zcol(ts, d, n):
    return pl.BlockSpec((ts, d), lambda i: (i, n))


def _row1(d):
    return pl.BlockSpec((1, d), lambda i: (0, 0))


def _plane(ts, d, n):
    return pl.BlockSpec((None, ts, d), lambda i: (n, i, 0))


class _Hosted:
    def __init__(self, ins, out_shapes, sem_shapes, phases):
        self.ins, self.out_shapes, self.sem_shapes, self.phases = ins, out_shapes, sem_shapes, phases


def _host(comm, n_in, n_out, n_scr, grid, body, n_prefetch=0, wants_results=False):
    if comm is None:
        return body, [], [], [], [], None
    ci, co = len(comm.ins), len(comm.out_shapes)
    total = math.prod(grid)
    at = {"first": 0, "late": (7 * total) // 8, "last": total - 1}

    def wrapped(*refs):
        pre, refs = refs[:n_prefetch], refs[n_prefetch:]
        ins, cin = refs[:n_in], refs[n_in:n_in + ci]
        o0 = n_in + ci
        outs, cout = refs[o0:o0 + n_out], refs[o0 + n_out:o0 + n_out + co]
        s0 = o0 + n_out + co
        scr, csem = refs[s0:s0 + n_scr], refs[s0 + n_scr:]
        step = pl.program_id(0)
        for a in range(1, len(grid)):
            step = step * grid[a] + pl.program_id(a)

        def run(last):
            for w, fn in comm.phases:
                if (w == "last") == last:
                    @pl.when(step == at.get(w, w))
                    def _(fn=fn):
                        fn(cin, cout, csem)
        run(False)
        body(*pre, *ins, *outs, *scr, **({"cout": cout} if wants_results else {}))
        run(True)

    return wrapped, [ANY] * ci, [ANY] * co, list(comm.out_shapes), list(comm.sem_shapes), ("arbitrary",) * len(grid)


def _in_proj(x, g, w_all, l, comm=None):
    S, D = x.shape
    N = w_all.shape[2]
    tm, tn = _tile(S, 1024), _tile(N, 1408)
    grid = (S // tm, N // tn)

    def body(x_ref, g_ref, w_ref, z_ref, hb_ref, h_scr):
        @pl.when(pl.program_id(1) == 0)
        def _():
            def slab(i, c):
                r = _rows(i, 16)
                xs = x_ref[r, :]
                h = (xs * lax.rsqrt(_mean(xs * xs) + RMS_EPS)) * g_ref[...]
                hb = h.astype(BF16)
                h_scr[r, :] = hb
                hb_ref[r, :] = hb
                return c
            _grouped_loop(tm // 16, 4, slab, 0)
        z_ref[...] = _dot(h_scr[...], w_ref[...])

    body, c_in, c_out, c_shapes, c_scr, sem = _host(comm, 3, 2, 1, grid, body)
    outs = pl.pallas_call(
        body, name="in_proj", grid=grid,
        in_specs=[pl.BlockSpec((tm, D), lambda i, j: (i, 0)),
                  pl.BlockSpec((1, D), lambda i, j: (0, 0)),
                  pl.BlockSpec((None, D, tn), lambda i, j: (l, 0, j))] + c_in,
        out_specs=[pl.BlockSpec((tm, tn), lambda i, j: (i, j)),
                   pl.BlockSpec((tm, D), lambda i, j: (i, 0))] + c_out,
        out_shape=[jax.ShapeDtypeStruct((S, N), F32), jax.ShapeDtypeStruct((S, D), BF16)] + c_shapes,
        scratch_shapes=[pltpu.VMEM((tm, D), BF16)] + c_scr,
        compiler_params=_params(*(sem or ("parallel", "arbitrary"))),
    )(x, g, w_all, *(comm.ins if comm else []))
    return outs[0], outs[1], outs[2:]


def _rms_fwd(x, g):
    S, D = x.shape
    ts = _tile(S, 512)

    def body(x_ref, g_ref, o_ref):
        def slab(i, c):
            r = _rows(i, 16)
            xs = x_ref[r, :]
            o_ref[r, :] = ((xs * lax.rsqrt(_mean(xs * xs) + RMS_EPS)) * g_ref[...]).astype(BF16)
            return c
        _grouped_loop(ts // 16, 4, slab, 0)

    return pl.pallas_call(
        body, name="rms_fwd", grid=(S // ts,), in_specs=[_tok(ts, D), _row1(D)], out_specs=_tok(ts, D),
        out_shape=jax.ShapeDtypeStruct((S, D), BF16), compiler_params=_params("parallel"),
    )(x, g)


def _in_proj_own_first(hb, shard, l, order, comm):
    S, D = hb.shape
    bs = shard.shape[-1]
    tn = bs // 2
    N = bs * N_CHIPS
    tm = _tile(S, 1024)
    n_i = S // tm
    grid = (N // tn, n_i)

    def body(order_ref, hb_ref, shard_hbm, z_ref, w_scr, wsem, cout):
        jo, i = pl.program_id(0), pl.program_id(1)

        @pl.when((i == 0) & (jo < bs // tn))
        def _():
            cp = pltpu.make_async_copy(shard_hbm.at[l, 0, :, pl.ds(pl.multiple_of(jo * tn, LANES), tn)], w_scr, wsem)
            cp.start()
            cp.wait()

        @pl.when((i == 0) & (jo >= bs // tn))
        def _():
            cols = pl.ds(pl.multiple_of(order_ref[jo] * tn, LANES), tn)
            cp = pltpu.make_async_copy(cout[0].at[0, :, cols], w_scr, wsem)
            cp.start()
            cp.wait()

        z_ref[...] = _dot(hb_ref[...], w_scr[...])

    body, c_in, c_out, c_shapes, c_scr, _ = _host(comm, 2, 1, 2, grid, body, n_prefetch=1, wants_results=True)
    outs = pl.pallas_call(
        body, name="in_proj_own_first",
        grid_spec=pltpu.PrefetchScalarGridSpec(
            num_scalar_prefetch=1, grid=grid,
            in_specs=[pl.BlockSpec((tm, D), lambda jo, i, order_ref: (i, 0)), ANY] + c_in,
            out_specs=[pl.BlockSpec((tm, tn), lambda jo, i, order_ref: (i, order_ref[jo]))] + c_out,
            scratch_shapes=[pltpu.VMEM((D, tn), BF16), pltpu.SemaphoreType.DMA] + c_scr),
        out_shape=[jax.ShapeDtypeStruct((S, N), F32)] + c_shapes,
        compiler_params=_params("arbitrary", "arbitrary"),
    )(order, hb, shard, *comm.ins)
    return outs[0], outs[1:]


def _branch_a_fwd(z, lng, lnb, t_mix, bias_full):
    S = z.shape[0]
    D = lng.shape[1]
    GD = D // GROUPS
    ts = _tile(S, 256)

    def body(zu_ref, zv_ref, zg_ref, lng_ref, lnb_ref, t_ref, bias_ref, a_ref, v_scr):
        def chunk(ci, carry):
            c0 = pl.multiple_of(ci * CHUNK, CHUNK)

            def slab(si, c):
                r = pl.multiple_of(si * 16, 16)
                vg = _gelu(zv_ref[pl.ds(pl.multiple_of(c0 + r, 16), 16), :])
                xc = vg - _mean(vg)
                y = xc * lax.rsqrt(_mean(xc * xc) + LN_EPS)
                v_scr[pl.ds(r, 16), :] = (y * lng_ref[...] + lnb_ref[...]).astype(BF16)
                return c
            _grouped_loop(CHUNK // 16, 4, slab, 0)
            for g in range(GROUPS):
                cols = slice(g * GD, (g + 1) * GD)
                sv = _dot(t_ref[g], v_scr[:, cols]) + bias_ref[:, cols]
                u = _gelu(zu_ref[pl.ds(c0, CHUNK), cols])
                a_ref[pl.ds(c0, CHUNK), cols] = ((u * sv) * _silu(zg_ref[pl.ds(c0, CHUNK), cols])).astype(BF16)
            return carry
        lax.fori_loop(0, ts // CHUNK, chunk, 0)

    return pl.pallas_call(
        body, name="branch_a_fwd", grid=(S // ts,),
        in_specs=[_zcol(ts, D, 0), _zcol(ts, D, 1), _zcol(ts, D, 2), _row1(D), _row1(D),
                  pl.BlockSpec((GROUPS, CHUNK, CHUNK), lambda i: (0, 0, 0)),
                  pl.BlockSpec((CHUNK, D), lambda i: (0, 0))],
        out_specs=_tok(ts, D),
        out_shape=jax.ShapeDtypeStruct((S, D), BF16),
        scratch_shapes=[pltpu.VMEM((CHUNK, D), BF16)],
        compiler_params=_params("parallel"),
    )(z, z, z, lng, lnb, t_mix, bias_full)


SH = HALO - CONV_K + 1


def _fill_shifts(sh_scr, src_scr, cols, ts):
    for j in range(1, 8):
        for g in range(ts // 32):
            sh_scr[j - 1, pl.ds(g * 32, 32), :] = src_scr[pl.ds(g * 32 + j, 32), cols]
        sh_scr[j - 1, pl.ds(ts, HALO - 8), :] = src_scr[pl.ds(ts + j, HALO - 8), cols]


def _shifted(sh_scr, src_scr, cols, row0, n):
    q, j = divmod(row0, 8)
    if j == 0:
        return src_scr[pl.ds(row0, n), cols]
    return sh_scr[j - 1, pl.ds(8 * q, n), :]


def _conv_fwd(z, w_all, cb, l, comm=None):
    S = z.shape[0]
    D = cb.shape[1]
    ts = _tile(S, 256)
    CB = 256
    hb = ts // HALO

    def body(za_ref, zb_ref, zap_ref, zbp_ref, w_ref, b_ref, c0_ref, glu_scr, sh_scr):
        i = pl.program_id(0)
        prev = zap_ref[...] * _sigmoid(zbp_ref[...])
        glu_scr[0:HALO, :] = jnp.where(i == 0, 0.0, prev)

        def fill(si, c):
            r = _rows(si, 32)
            glu_scr[pl.ds(pl.multiple_of(HALO + si * 32, 32), 32), :] = za_ref[r, :] * _sigmoid(zb_ref[r, :])
            return c
        _grouped_loop(ts // 32, 2, fill, 0)

        def colblock(ci, c):
            cols = _rows(ci, CB)
            _fill_shifts(sh_scr, glu_scr, cols, ts)
            for s in range(ts // 32):
                acc = jnp.zeros((32, CB), F32)
                for k in range(CONV_K):
                    acc = acc + _shifted(sh_scr, glu_scr, cols, s * 32 + k + SH, 32) * w_ref[k:k + 1, cols]
                c0_ref[pl.ds(s * 32, 32), cols] = acc + b_ref[:, cols]
            return c
        lax.fori_loop(0, D // CB, colblock, 0)

    prev_map = lambda n: (lambda i: (jnp.maximum(i * hb - 1, 0), n))
    grid = (S // ts,)
    body, c_in, c_out, c_shapes, c_scr, sem = _host(comm, 6, 1, 2, grid, body)
    outs = pl.pallas_call(
        body, name="conv_fwd", grid=grid,
        in_specs=[_zcol(ts, D, 3), _zcol(ts, D, 4),
                  pl.BlockSpec((HALO, D), prev_map(3)), pl.BlockSpec((HALO, D), prev_map(4)),
                  pl.BlockSpec((None, CONV_K, D), lambda i: (l, 0, 0)), _row1(D)] + c_in,
        out_specs=[_tok(ts, D)] + c_out,
        out_shape=[jax.ShapeDtypeStruct((S, D), F32)] + c_shapes,
        scratch_shapes=[pltpu.VMEM((ts + HALO, D), F32), pltpu.VMEM((7, ts + HALO, CB), F32)] + c_scr,
        compiler_params=_params(*(sem or ("parallel",))),
    )(z, z, z, z, w_all, cb, *(comm.ins if comm else []))
    return outs[0], outs[1:]


def _branch_b_fwd(c0, z, g, b):
    S, D = c0.shape
    ts = _tile(S, 512)

    def body(c0_ref, zg_ref, g_ref, b_ref, o_ref):
        def slab(si, c):
            r = _rows(si, 16)
            v = c0_ref[r, :]
            xc = v - _mean(v)
            c1 = (xc * lax.rsqrt(_mean(xc * xc) + LN_EPS)) * g_ref[...] + b_ref[...]
            o_ref[r, :] = (_silu(c1) * _silu(zg_ref[r, :])).astype(BF16)
            return c
        _grouped_loop(ts // 16, 4, slab, 0)

    return pl.pallas_call(
        body, name="branch_b_fwd", grid=(S // ts,),
        in_specs=[_tok(ts, D), _zcol(ts, D, 5), _row1(D), _row1(D)],
        out_specs=_tok(ts, D),
        out_shape=jax.ShapeDtypeStruct((S, D), BF16),
        compiler_params=_params("parallel"),
    )(c0, z, g, b)


def _kv_fwd(mem, mg, wkv_all, l):
    M, D = mem.shape

    def body(mem_ref, g_ref, w_ref, mn_ref, kv_ref, kvt_ref):
        m = mem_ref[...]
        mn = ((m * lax.rsqrt(_mean(m * m) + RMS_EPS)) * g_ref[...]).astype(BF16)
        mn_ref[...] = mn
        kv = _dot(mn, w_ref[...])
        kv_ref[...] = kv.astype(BF16)
        kvt_ref[...] = kv.T.astype(BF16)

    return pl.pallas_call(
        body, name="kv_fwd", grid=(1,),
        in_specs=[pl.BlockSpec((M, D), lambda i: (0, 0)), _row1(D),
                  pl.BlockSpec((None, D, 2 * D), lambda i: (l, 0, 0))],
        out_specs=[pl.BlockSpec((M, D), lambda i: (0, 0)), pl.BlockSpec((M, 2 * D), lambda i: (0, 0)),
                   pl.BlockSpec((2 * D, M), lambda i: (0, 0))],
        out_shape=[jax.ShapeDtypeStruct((M, D), BF16), jax.ShapeDtypeStruct((M, 2 * D), BF16),
                   jax.ShapeDtypeStruct((2 * D, M), BF16)],
        compiler_params=_params("arbitrary"),
    )(mem, mg, wkv_all)


def _softmax_rows(s):
    e = jnp.exp(s - jnp.max(s, axis=-1, keepdims=True))
    return e / jnp.sum(e, axis=-1, keepdims=True)


def _branch_c_fwd(z, kv, kvt):
    S = z.shape[0]
    M, D2 = kv.shape
    D = D2 // 2
    HD = D // HEADS
    scale = 1.0 / math.sqrt(HD)
    ts = _tile(S, 256)

    def body(q_ref, zg_ref, kt_ref, v_ref, o_ref):
        for h in range(HEADS):
            cols = slice(h * HD, (h + 1) * HD)
            p = _softmax_rows(_dot(q_ref[:, cols].astype(BF16), kt_ref[cols, :]) * scale)
            att = _dot(p.astype(BF16), v_ref[:, cols])
            o_ref[:, cols] = (att * _silu(zg_ref[:, cols])).astype(BF16)

    return pl.pallas_call(
        body, name="branch_c_fwd", grid=(S // ts,),
        in_specs=[_zcol(ts, D, 6), _zcol(ts, D, 7),
                  pl.BlockSpec((D, M), lambda i: (0, 0)), pl.BlockSpec((M, D), lambda i: (0, 1))],
        out_specs=_tok(ts, D),
        out_shape=jax.ShapeDtypeStruct((S, D), BF16),
        compiler_params=_params("parallel"),
    )(z, z, kvt, kv)


def _merge_fwd(bra, brb, brc, z, x, wb_all, wo_all, l):
    S, D = x.shape
    ts = _tile(S, 512)
    CB = 256

    def body(a_ref, b_ref, c_ref, zm0_ref, zm1_ref, zm2_ref, x_ref, wb_ref, wo_ref, m_ref, xo_ref):
        brs = (a_ref, b_ref, c_ref)
        zms = (zm0_ref, zm1_ref, zm2_ref)
        for cb in range(D // CB):
            cols = slice(cb * CB, (cb + 1) * CB)
            acc = None
            for n in range(N_BRANCHES):
                t = _sigmoid(zms[n][:, cols]) * _dot(brs[n][...], wb_ref[n, :, cols])
                acc = t if acc is None else acc + t
            m_ref[:, cols] = acc.astype(BF16)
        for cb in range(D // CB):
            cols = slice(cb * CB, (cb + 1) * CB)
            xo_ref[:, cols] = x_ref[:, cols] + _dot(m_ref[...], wo_ref[:, cols])

    return pl.pallas_call(
        body, name="merge_fwd", grid=(S // ts,),
        in_specs=[_tok(ts, D), _tok(ts, D), _tok(ts, D), _zcol(ts, D, 8), _zcol(ts, D, 9), _zcol(ts, D, 10),
                  _tok(ts, D),
                  pl.BlockSpec((None, N_BRANCHES, D, D), lambda i: (l, 0, 0, 0)),
                  pl.BlockSpec((None, D, D), lambda i: (l, 0, 0))],
        out_specs=[_tok(ts, D), _tok(ts, D)],
        out_shape=[jax.ShapeDtypeStruct((S, D), BF16), jax.ShapeDtypeStruct((S, D), F32)],
        compiler_params=_params("parallel"),
    )(bra, brb, brc, z, z, z, x, wb_all, wo_all)


def _loss_bwd(x, target, fg):
    S, D = x.shape
    ts = _tile(S, 512)

    def body(x_ref, t_ref, g_ref, dx_ref, ls_ref, dg_ref):
        @pl.when(pl.program_id(0) == 0)
        def _():
            ls_ref[...] = jnp.zeros_like(ls_ref)
            dg_ref[...] = jnp.zeros_like(dg_ref)

        def slab(si, c):
            ls, dg = c
            r = _rows(si, 16)
            xs = x_ref[r, :]
            rs = lax.rsqrt(_mean(xs * xs) + RMS_EPS)
            n = xs * rs
            e = n * g_ref[...] - t_ref[r, :]
            dy = e * (1.0 / D)
            dn = dy * g_ref[...]
            dx_ref[r, :] = rs * (dn - n * _mean(dn * n))
            return ls + _colsum(e * e), dg + _colsum(dy * n)
        zero = jnp.zeros((1, D), F32)
        ls, dg = _grouped_loop(ts // 16, 4, slab, (zero, zero))
        ls_ref[...] += ls
        dg_ref[...] += dg

    return pl.pallas_call(
        body, name="loss_bwd", grid=(S // ts,),
        in_specs=[_tok(ts, D), _tok(ts, D), _row1(D)],
        out_specs=[_tok(ts, D), _row1(D), _row1(D)],
        out_shape=[jax.ShapeDtypeStruct((S, D), F32), jax.ShapeDtypeStruct((1, D), F32),
                   jax.ShapeDtypeStruct((1, D), F32)],
        compiler_params=_params("arbitrary"),
    )(x, target, fg)


def _merge_bwd(dxp, bra, brb, brc, z, wb_all, wo_all, l, comm=None):
    S, D = dxp.shape
    ts = _tile(S, 256)
    CB = 256

    def body(d_ref, a_ref, b_ref, c_ref, zm0_ref, zm1_ref, zm2_ref, wb_ref, wo_ref,
             dzm_ref, dproj_ref, dbr_ref, dxb_scr):
        brs = (a_ref, b_ref, c_ref)
        zms = (zm0_ref, zm1_ref, zm2_ref)
        dxb_scr[...] = d_ref[...].astype(BF16)
        for cb in range(D // CB):
            cols = slice(cb * CB, (cb + 1) * CB)
            dm = _dot_tb(dxb_scr[...], wo_ref[cols, :])
            for n in range(N_BRANCHES):
                proj = _dot(brs[n][...], wb_ref[n, :, cols])
                g = _sigmoid(zms[n][:, cols])
                dzm_ref[:, n * D + cb * CB:n * D + (cb + 1) * CB] = ((dm * proj) * (g * (1.0 - g))).astype(BF16)
                dproj_ref[n, :, cols] = (dm * g).astype(BF16)
        for n in range(N_BRANCHES):
            for cb in range(D // CB):
                cols = slice(cb * CB, (cb + 1) * CB)
                dbr_ref[n, :, cols] = _dot_tb(dproj_ref[n], wb_ref[n, cols, :])

    tok3 = lambda: pl.BlockSpec((N_BRANCHES, ts, D), lambda i: (0, i, 0))
    grid = (S // ts,)
    body, c_in, c_out, c_shapes, c_scr, sem = _host(comm, 9, 3, 1, grid, body)
    outs = pl.pallas_call(
        body, name="merge_bwd", grid=grid,
        in_specs=[_tok(ts, D), _tok(ts, D), _tok(ts, D), _tok(ts, D),
                  _zcol(ts, D, 8), _zcol(ts, D, 9), _zcol(ts, D, 10),
                  pl.BlockSpec((None, N_BRANCHES, D, D), lambda i: (l, 0, 0, 0)),
                  pl.BlockSpec((None, D, D), lambda i: (l, 0, 0))] + c_in,
        out_specs=[_tok(ts, 3 * D), tok3(), tok3()] + c_out,
        out_shape=[jax.ShapeDtypeStruct((S, 3 * D), BF16), jax.ShapeDtypeStruct((N_BRANCHES, S, D), BF16),
                   jax.ShapeDtypeStruct((N_BRANCHES, S, D), F32)] + c_shapes,
        scratch_shapes=[pltpu.VMEM((ts, D), BF16)] + c_scr,
        compiler_params=_params(*(sem or ("parallel",))),
    )(dxp, bra, brb, brc, z, z, z, wb_all, wo_all, *(comm.ins if comm else []))
    return outs[0], outs[1], outs[2], outs[3:]


def _branch_a_bwd(z, dbr, lng, lnb, t_mix, t_mix_t, bias_full, mask):
    S = z.shape[0]
    D = lng.shape[1]
    GD = D // GROUPS
    ts = _tile(S, 256)
    nt = S // ts

    def body(zu_ref, zv_ref, zg_ref, d_ref, lng_ref, lnb_ref, t_ref, tt_ref, bias_ref, mask_ref,
             dz_ref, dws_ref, dbs_ref, dlg_ref, dlb_ref,
             v_scr, vn_scr, rstd_scr, dv_scr, dsv_acc):
        i = pl.program_id(0)

        @pl.when(i == 0)
        def _():
            dws_ref[...] = jnp.zeros_like(dws_ref)
            dlg_ref[...] = jnp.zeros_like(dlg_ref)
            dlb_ref[...] = jnp.zeros_like(dlb_ref)
            dsv_acc[...] = jnp.zeros_like(dsv_acc)

        def chunk(ci, carry):
            c0 = pl.multiple_of(ci * CHUNK, CHUNK)

            def ln_fwd(si, c):
                r = pl.multiple_of(si * 16, 16)
                vg = _gelu(zv_ref[pl.ds(pl.multiple_of(c0 + r, 16), 16), :])
                xc = vg - _mean(vg)
                rstd = lax.rsqrt(_mean(xc * xc) + LN_EPS)
                vn = xc * rstd
                vn_scr[pl.ds(r, 16), :] = vn
                rstd_scr[pl.ds(r, 16), :] = rstd
                v_scr[pl.ds(r, 16), :] = (vn * lng_ref[...] + lnb_ref[...]).astype(BF16)
                return c
            _grouped_loop(CHUNK // 16, 4, ln_fwd, 0)

            for g in range(GROUPS):
                cols = slice(g * GD, (g + 1) * GD)
                rows = pl.ds(c0, CHUNK)
                vb = v_scr[:, cols]
                sv = _dot(t_ref[g], vb) + bias_ref[:, cols]
                u, du = _gelu_and_grad(zu_ref[rows, cols])
                sg, dsg = _silu_and_grad(zg_ref[rows, cols])
                d = d_ref[rows, cols]
                dsv = (d * u) * sg
                dz_ref[rows, g * GD:(g + 1) * GD] = (((d * sv) * sg) * du).astype(BF16)
                dz_ref[rows, 2 * D + g * GD:2 * D + (g + 1) * GD] = (((d * u) * sv) * dsg).astype(BF16)
                dsvb = dsv.astype(BF16)
                dws_ref[g] += _dot_tb(dsvb, vb)
                dv_scr[:, cols] = _dot(tt_ref[g], dsvb)
                dsv_acc[:, cols] += dsv

            def ln_bwd(si, c):
                dlg, dlb = c
                r = pl.multiple_of(si * 16, 16)
                rr = pl.ds(r, 16)
                zrows = pl.ds(pl.multiple_of(c0 + r, 16), 16)
                dv = dv_scr[rr, :]
                vn = vn_scr[rr, :]
                dvn = dv * lng_ref[...]
                dvg = rstd_scr[rr, :] * ((dvn - _mean(dvn)) - vn * _mean(dvn * vn))
                _, gg = _gelu_and_grad(zv_ref[zrows, :])
                dz_ref[zrows, D:2 * D] = (dvg * gg).astype(BF16)
                return dlg + _colsum(dv * vn), dlb + _colsum(dv)
            zero = jnp.zeros((1, D), F32)
            dlg, dlb = _grouped_loop(CHUNK // 16, 4, ln_bwd, (zero, zero))
            dlg_ref[...] += dlg
            dlb_ref[...] += dlb
            return carry
        lax.fori_loop(0, ts // CHUNK, chunk, 0)

        @pl.when(i == nt - 1)
        def _():
            for g in range(GROUPS):
                cols = slice(g * GD, (g + 1) * GD)
                dws_ref[g] = dws_ref[g] * mask_ref[...]
                dbs_ref[g:g + 1, :] = _colsum(dsv_acc[:, cols].T)

    c2 = lambda shape: pl.BlockSpec(shape, lambda i: (0, 0))
    c3 = lambda: pl.BlockSpec((GROUPS, CHUNK, CHUNK), lambda i: (0, 0, 0))
    return pl.pallas_call(
        body, name="branch_a_bwd", grid=(nt,),
        in_specs=[_zcol(ts, D, 0), _zcol(ts, D, 1), _zcol(ts, D, 2), _plane(ts, D, 0), _row1(D), _row1(D),
                  c3(), c3(), c2((CHUNK, D)), c2((CHUNK, CHUNK))],
        out_specs=[_tok(ts, 3 * D), c3(), c2((GROUPS, CHUNK)), _row1(D), _row1(D)],
        out_shape=[jax.ShapeDtypeStruct((S, 3 * D), BF16), jax.ShapeDtypeStruct((GROUPS, CHUNK, CHUNK), F32),
                   jax.ShapeDtypeStruct((GROUPS, CHUNK), F32), jax.ShapeDtypeStruct((1, D), F32),
                   jax.ShapeDtypeStruct((1, D), F32)],
        scratch_shapes=[pltpu.VMEM((CHUNK, D), BF16), pltpu.VMEM((CHUNK, D), F32), pltpu.VMEM((CHUNK, 1), F32),
                        pltpu.VMEM((CHUNK, D), F32), pltpu.VMEM((CHUNK, D), F32)],
        compiler_params=_params("arbitrary"),
    )(z, z, z, dbr, lng, lnb, t_mix, t_mix_t, bias_full, mask)


def _branch_b_bwd(c0, z, dbr, g, b, comm=None):
    S, D = c0.shape
    ts = _tile(S, 512)

    def body(c0_ref, zg_ref, d_ref, g_ref, b_ref, dc0_ref, dz_ref, dg_ref, db_ref, dcb_ref):
        @pl.when(pl.program_id(0) == 0)
        def _():
            dg_ref[...] = jnp.zeros_like(dg_ref)
            db_ref[...] = jnp.zeros_like(db_ref)
            dcb_ref[...] = jnp.zeros_like(dcb_ref)

        def slab(si, c):
            dg, db, dcb = c
            r = _rows(si, 16)
            v = c0_ref[r, :]
            xc = v - _mean(v)
            rstd = lax.rsqrt(_mean(xc * xc) + LN_EPS)
            cn = xc * rstd
            c1 = cn * g_ref[...] + b_ref[...]
            c2, dc2_dc1 = _silu_and_grad(c1)
            sg, dsg = _silu_and_grad(zg_ref[r, :])
            d = d_ref[r, :]
            dz_ref[r, :] = ((d * c2) * dsg).astype(BF16)
            dc1 = (d * sg) * dc2_dc1
            dcn = dc1 * g_ref[...]
            dc0 = rstd * ((dcn - _mean(dcn)) - cn * _mean(dcn * cn))
            dc0_ref[r, :] = dc0
            return dg + _colsum(dc1 * cn), db + _colsum(dc1), dcb + _colsum(dc0)
        zero = jnp.zeros((1, D), F32)
        dg, db, dcb = _grouped_loop(ts // 16, 4, slab, (zero, zero, zero))
        dg_ref[...] += dg
        db_ref[...] += db
        dcb_ref[...] += dcb

    grid = (S // ts,)
    body, c_in, c_out, c_shapes, c_scr, _ = _host(comm, 5, 5, 0, grid, body)
    outs = pl.pallas_call(
        body, name="branch_b_bwd", grid=grid,
        in_specs=[_tok(ts, D), _zcol(ts, D, 5), _plane(ts, D, 1), _row1(D), _row1(D)] + c_in,
        out_specs=[_tok(ts, D), _zcol(ts, D, 2), _row1(D), _row1(D), _row1(D)] + c_out,
        out_shape=[jax.ShapeDtypeStruct((S, D), F32), jax.ShapeDtypeStruct((S, 3 * D), BF16),
                   jax.ShapeDtypeStruct((1, D), F32), jax.ShapeDtypeStruct((1, D), F32),
                   jax.ShapeDtypeStruct((1, D), F32)] + c_shapes,
        scratch_shapes=c_scr,
        compiler_params=_params("arbitrary"),
    )(c0, z, dbr, g, b, *(comm.ins if comm else []))
    return (*outs[:5], outs[5:])


def _conv_bwd(dc0, z, w_all, dz_b, l, comm=None):
    S, D = dc0.shape
    ts = _tile(S, 256)
    nt = S // ts
    CB = 256
    hb = ts // HALO
    nh = S // HALO

    def body(dc_ref, dcn_ref, za_ref, zb_ref, w_ref, dzin_ref, dz_ref, dw_ref, dc_scr, dw_acc, dsh_scr):
        del dzin_ref
        i = pl.program_id(0)

        @pl.when(i == 0)
        def _():
            dw_acc[...] = jnp.zeros_like(dw_acc)

        dc_scr[ts:ts + HALO, :] = jnp.where(i == nt - 1, 0.0, dcn_ref[...])

        def fill(si, c):
            r = _rows(si, 32)
            dc_scr[r, :] = dc_ref[r, :]
            return c
        _grouped_loop(ts // 32, 2, fill, 0)

        def colblock(ci, c):
            cols = _rows(ci, CB)
            cols_b = pl.ds(pl.multiple_of(D + ci * CB, CB), CB)
            _fill_shifts(dsh_scr, dc_scr, cols, ts)
            for s in range(ts // 32):
                rows = pl.ds(s * 32, 32)
                za = za_ref[rows, cols]
                sg = _sigmoid(zb_ref[rows, cols])
                glu = za * sg
                acc = jnp.zeros((32, CB), F32)
                for k in range(CONV_K):
                    dcs = _shifted(dsh_scr, dc_scr, cols, s * 32 + CONV_K - 1 - k, 32)
                    prod = glu * dcs
                    dw_acc[k, :, cols] += (prod[0:8] + prod[8:16]) + (prod[16:24] + prod[24:32])
                    acc = acc + dcs * w_ref[k:k + 1, cols]
                dz_ref[rows, cols] = (acc * sg).astype(BF16)
                dz_ref[rows, cols_b] = ((acc * za) * (sg * (1.0 - sg))).astype(BF16)
            return c
        lax.fori_loop(0, D // CB, colblock, 0)

        @pl.when(i == nt - 1)
        def _():
            for k in range(CONV_K):
                dw_ref[k:k + 1, :] = _colsum(dw_acc[k])

    grid = (nt,)
    body, c_in, c_out, c_shapes, c_scr, _ = _host(comm, 6, 2, 3, grid, body)
    outs = pl.pallas_call(
        body, name="conv_bwd", grid=grid,
        in_specs=[_tok(ts, D), pl.BlockSpec((HALO, D), lambda i: (jnp.minimum((i + 1) * hb, nh - 1), 0)),
                  _zcol(ts, D, 3), _zcol(ts, D, 4),
                  pl.BlockSpec((None, CONV_K, D), lambda i: (l, 0, 0)), ANY] + c_in,
        out_specs=[_tok(ts, 2 * D), pl.BlockSpec((CONV_K, D), lambda i: (0, 0))] + c_out,
        out_shape=[jax.ShapeDtypeStruct((S, 3 * D), BF16), jax.ShapeDtypeStruct((CONV_K, D), F32)] + c_shapes,
        scratch_shapes=[pltpu.VMEM((ts + HALO, D), F32), pltpu.VMEM((CONV_K, 8, D), F32),
                        pltpu.VMEM((7, ts + HALO, CB), F32)] + c_scr,
        input_output_aliases={5: 0},
        compiler_params=_params("arbitrary"),
    )(dc0, dc0, z, z, w_all, dz_b, *(comm.ins if comm else []))
    return outs[0], outs[1], outs[2:]


def _branch_c_bwd(z, dbr, kv, kvt):
    S = z.shape[0]
    M, D2 = kv.shape
    D = D2 // 2
    HD = D // HEADS
    scale = 1.0 / math.sqrt(HD)
    ts = _tile(S, 256)

    def body(q_ref, zg_ref, d_ref, kt_ref, vt_ref, k_ref, v_ref, dz_ref, dk_ref, dv_ref):
        @pl.when(pl.program_id(0) == 0)
        def _():
            dk_ref[...] = jnp.zeros_like(dk_ref)
            dv_ref[...] = jnp.zeros_like(dv_ref)

        for h in range(HEADS):
            cols = slice(h * HD, (h + 1) * HD)
            qb = q_ref[:, cols].astype(BF16)
            p = _softmax_rows(_dot(qb, kt_ref[cols, :]) * scale)
            pb = p.astype(BF16)
            att = _dot(pb, v_ref[:, cols])
            sg, dsg = _silu_and_grad(zg_ref[:, cols])
            d = d_ref[:, cols]
            dz_ref[:, D + h * HD:D + (h + 1) * HD] = ((d * att) * dsg).astype(BF16)
            datt = (d * sg).astype(BF16)
            dp = _dot(datt, vt_ref[cols, :])
            dv_ref[:, cols] += _dot_ta(pb, datt)
            ds = ((p * (dp - jnp.sum(dp * p, axis=-1, keepdims=True))) * scale).astype(BF16)
            dz_ref[:, cols] = _dot(ds, k_ref[:, cols]).astype(BF16)
            dk_ref[:, cols] += _dot_ta(ds, qb)

    return pl.pallas_call(
        body, name="branch_c_bwd", grid=(S // ts,),
        in_specs=[_zcol(ts, D, 6), _zcol(ts, D, 7), _plane(ts, D, 2),
                  pl.BlockSpec((D, M), lambda i: (0, 0)), pl.BlockSpec((D, M), lambda i: (1, 0)),
                  pl.BlockSpec((M, D), lambda i: (0, 0)), pl.BlockSpec((M, D), lambda i: (0, 1))],
        out_specs=[_tok(ts, 2 * D), pl.BlockSpec((M, D), lambda i: (0, 0)), pl.BlockSpec((M, D), lambda i: (0, 0))],
        out_shape=[jax.ShapeDtypeStruct((S, 2 * D), BF16), jax.ShapeDtypeStruct((M, D), F32),
                   jax.ShapeDtypeStruct((M, D), F32)],
        compiler_params=_params("arbitrary"),
    )(z, z, dbr, kvt, kvt, kv, kv)


def _kv_bwd(dk, dv, mn, mem, wkv_all, l, prev):
    M, D = mem.shape

    def body(*refs):
        dk_ref, dv_ref, mn_ref, mem_ref, w_ref = refs[:5]
        dw_ref, dg_ref = refs[-2:]
        dkb = dk_ref[...].astype(BF16)
        dvb = dv_ref[...].astype(BF16)
        dw_ref[:, 0:D] = _dot_ta(mn_ref[...], dkb)
        dw_ref[:, D:2 * D] = _dot_ta(mn_ref[...], dvb)
        dmn = _dot_tb(dkb, w_ref[:, 0:D]) + _dot_tb(dvb, w_ref[:, D:2 * D])
        m = mem_ref[...]
        dg_ref[...] = _colsum(dmn * (m * lax.rsqrt(_mean(m * m) + RMS_EPS)))

    full = lambda shape: pl.BlockSpec(shape, lambda i: (0, 0))
    in_specs = [full((M, D)), full((M, D)), full((M, D)), full((M, D)),
                pl.BlockSpec((None, D, 2 * D), lambda i: (l, 0, 0))]
    args = [dk, dv, mn, mem, wkv_all]
    aliases = {}
    if prev is not None:
        in_specs.append(ANY)
        args.append(prev)
        aliases = {5: 0}
    return pl.pallas_call(
        body, name="kv_bwd", grid=(1,),
        in_specs=in_specs,
        out_specs=[pl.BlockSpec((None, D, 2 * D), lambda i: (l, 0, 0)), _row1(D)],
        out_shape=[jax.ShapeDtypeStruct((wkv_all.shape[0], D, 2 * D), F32), jax.ShapeDtypeStruct((1, D), F32)],
        input_output_aliases=aliases,
        compiler_params=_params("arbitrary"),
    )(*args)


def _in_proj_bwd(pieces, w_all, x, g, dxp, l, comm=None, part=(0, 1), prev=None):
    S, D = x.shape
    tm = _tile(S, 1024)
    n_p = len(pieces)
    n_i = (S // tm) // part[1]
    i0 = part[0] * n_i
    n_in = n_p + 4 + (prev is not None)

    def body(*refs):
        dz_refs = refs[:n_p]
        w_ref, x_hbm, g_ref, dxp_hbm = refs[n_p:n_p + 4]
        dx_ref, dg_ref, acc, x_ref, dxp_ref, sem = refs[n_in:]
        i = pl.program_id(0)
        k = pl.program_id(1)
        rows = pl.ds(pl.multiple_of((i + i0) * tm, tm), tm)
        fetch = [pltpu.make_async_copy(x_hbm.at[rows, :], x_ref, sem.at[0]),
                 pltpu.make_async_copy(dxp_hbm.at[rows, :], dxp_ref, sem.at[1])]

        @pl.when(k == N_PLANES - 2)
        def _():
            for cp in fetch:
                cp.start()

        @pl.when(k == 0)
        def _():
            acc[...] = jnp.zeros_like(acc)

        @pl.when((i == 0) & (k == 0))
        def _():
            dg_ref[...] = jnp.zeros_like(dg_ref)

        for (_, lo, n), r in zip(pieces, dz_refs):
            @pl.when((k >= lo) & (k < lo + n))
            def _(r=r):
                acc[...] += _dot_tb(r[...], w_ref[...])

        @pl.when(k == N_PLANES - 1)
        def _():
            for cp in fetch:
                cp.wait()

            def slab(si, c):
                rr = _rows(si, 16)
                xs = x_ref[rr, :]
                rs = lax.rsqrt(_mean(xs * xs) + RMS_EPS)
                n = xs * rs
                dh = acc[rr, :]
                dn = dh * g_ref[...]
                dx_ref[rr, :] = rs * (dn - n * _mean(dn * n)) + dxp_ref[rr, :]
                return c + _colsum(dh * n)
            dg_ref[...] += _grouped_loop(tm // 16, 4, slab, jnp.zeros((1, D), F32))

    def piece_spec(lo, n):
        return pl.BlockSpec((tm, D), lambda i, k: (i + i0, jnp.clip(k - lo, 0, n - 1)))

    tokk = lambda: pl.BlockSpec((tm, D), lambda i, k: (i + i0, 0))
    rowk = lambda: pl.BlockSpec((1, D), lambda i, k: (0, 0))
    grid = (n_i, N_PLANES)
    body, c_in, c_out, c_shapes, c_scr, _ = _host(comm, n_in, 2, 4, grid, body)
    more_in, more_args, aliases = ([ANY], [prev], {n_in - 1: 0}) if prev is not None else ([], [], {})
    outs = pl.pallas_call(
        body, name="in_proj_bwd", grid=grid,
        in_specs=[piece_spec(lo, n) for _, lo, n in pieces]
        + [pl.BlockSpec((None, D, D), lambda i, k: (l, 0, k)), ANY, rowk(), ANY] + more_in + c_in,
        out_specs=[tokk(), rowk()] + c_out,
        out_shape=[jax.ShapeDtypeStruct((S, D), F32), jax.ShapeDtypeStruct((1, D), F32)] + c_shapes,
        scratch_shapes=[pltpu.VMEM((tm, D), F32), pltpu.VMEM((tm, D), F32), pltpu.VMEM((tm, D), F32),
                        pltpu.SemaphoreType.DMA((2,))] + c_scr,
        input_output_aliases=aliases,
        compiler_params=_params("arbitrary", "arbitrary"),
    )(*[p for p, _, _ in pieces], w_all, x, g, dxp, *more_args, *(comm.ins if comm else []))
    return outs[0], outs[1], outs[2:]


def _wgrad(a, b, out_shape, plane, col_off, prev, a_plane=None, b_plane=None):
    S = a.shape[-2]
    K1 = a.shape[-1]
    nb_cols = b.shape[-1]
    ts = _tile(S, 2048)
    tn = _tile(nb_cols, 1024)
    col_off = col_off // tn

    def body(*refs):
        a_ref, b_ref = refs[:2]
        o_ref = refs[-1]
        k = pl.program_id(1)
        prod = _dot_ta(a_ref[...].astype(BF16), b_ref[...].astype(BF16))

        @pl.when(k == 0)
        def _():
            o_ref[...] = prod

        @pl.when(k > 0)
        def _():
            o_ref[...] += prod

    if a_plane is None:
        a_spec = pl.BlockSpec((ts, K1), lambda j, k: (k, 0))
    else:
        a_spec = pl.BlockSpec((None, ts, K1), lambda j, k: (a_plane, k, 0))
    if b_plane is None:
        b_spec = pl.BlockSpec((ts, tn), lambda j, k: (k, j))
    else:
        b_spec = pl.BlockSpec((None, ts, tn), lambda j, k: (b_plane, k, j))
    in_specs, args, aliases = [a_spec, b_spec], [a, b], {}
    if prev is not None:
        in_specs.append(ANY)
        args.append(prev)
        aliases = {2: 0}
    return pl.pallas_call(
        body, name="wgrad", grid=(nb_cols // tn, S // ts),
        in_specs=in_specs,
        out_specs=pl.BlockSpec((None, K1, tn), lambda j, k: (plane, 0, col_off + j)),
        out_shape=jax.ShapeDtypeStruct(out_shape, F32),
        input_output_aliases=aliases,
        compiler_params=_params("parallel", "arbitrary"),
    )(*args)


def _position():
    x, y, c = lax.axis_index("x"), lax.axis_index("y"), lax.axis_index("c")
    chips = [(1 - x, y), (x, 1 - y), (1 - x, 1 - y)]
    return x, y, c, chips


def _sub(ref, axis, start, size):
    idx = [slice(None)] * len(ref.shape)
    idx[axis] = pl.ds(start, size)
    return ref.at[tuple(idx)]


def _dma(src, dst, ssem, rsem, to):
    return pltpu.make_async_remote_copy(src_ref=src, dst_ref=dst, send_sem=ssem, recv_sem=rsem,
                                        device_id=to, device_id_type=MESH)


def _dma_sems(*counts):
    return [pltpu.SemaphoreType.DMA((n,)) for n in counts]


def _both(a, b):
    na, oa, sa = len(a.ins), len(a.out_shapes), len(a.sem_shapes)
    phases = [(w, lambda i, o, s, fn=fn: fn(i[:na], o[:oa], s[:sa])) for w, fn in a.phases]
    phases += [(w, lambda i, o, s, fn=fn: fn(i[na:], o[oa:], s[sa:])) for w, fn in b.phases]
    return _Hosted(a.ins + b.ins, list(a.out_shapes) + list(b.out_shapes), list(a.sem_shapes) + list(b.sem_shapes), phases)


def _run_alone(comm, name):
    ci, co = len(comm.ins), len(comm.out_shapes)

    def body(*refs):
        for _, fn in comm.phases:
            fn(refs[:ci], refs[ci:ci + co], refs[ci + co:])

    return pl.pallas_call(
        body, name=name, in_specs=[ANY] * ci, out_specs=[ANY] * co, out_shape=list(comm.out_shapes),
        scratch_shapes=list(comm.sem_shapes),
    )(*comm.ins)


def _gather_comm(shards, layers, axes, start_at="first"):
    n = len(shards)
    out_shapes = []
    for s, (ba, _) in zip(shards, axes):
        shp = list(s.shape[1:])
        shp[ba] *= N_CHIPS
        out_shapes.append(jax.ShapeDtypeStruct(tuple(shp), s.dtype))

    def half(ref, t, h):
        ha = axes[t][1]
        if ha is None:
            return ref
        hs = shards[t].shape[1 + ha] // 2
        return _sub(ref, ha, h * hs, hs)

    def block(outs, t, blk):
        ba = axes[t][0]
        bs = shards[t].shape[1 + ba]
        return _sub(outs[t], ba, blk * bs, bs)

    def start(ins, outs, sems):
        s_own, r_own, s_ici, r_ici, _, _ = sems
        x, y, c, chips = _position()
        j = 2 * x + y
        for t in range(n):
            _dma(ins[t].at[layers[t]], block(outs, t, j), s_own.at[t], r_own.at[t], (x, y, 1 - c)).start()
            for k, (px, py) in enumerate(chips):
                _dma(half(ins[t].at[layers[t]], t, c), half(block(outs, t, j), t, c),
                     s_ici.at[3 * t + k], r_ici.at[3 * t + k], (px, py, c)).start()

    def forward(ins, outs, sems):
        _, _, s_ici, r_ici, s_fwd, r_fwd = sems
        x, y, c, chips = _position()
        for t in range(n):
            if axes[t][1] is None:
                continue
            for k, (px, py) in enumerate(chips):
                got = half(block(outs, t, 2 * px + py), t, c)
                _dma(got, got, s_ici.at[3 * t + k], r_ici.at[3 * t + k], (px, py, c)).wait_recv()
                _dma(got, got, s_fwd.at[3 * t + k], r_fwd.at[3 * t + k], (x, y, 1 - c)).start()

    def finish(ins, outs, sems):
        s_own, r_own, s_ici, r_ici, s_fwd, r_fwd = sems
        x, y, c, chips = _position()
        j = 2 * x + y
        sib = (x, y, 1 - c)
        for t in range(n):
            for k, (px, py) in enumerate(chips):
                i = 3 * t + k
                sent = half(block(outs, t, j), t, c)
                _dma(sent, sent, s_ici.at[i], r_ici.at[i], (px, py, c)).wait_send()
                theirs = block(outs, t, 2 * px + py)
                if axes[t][1] is None:
                    _dma(theirs, theirs, s_ici.at[i], r_ici.at[i], (px, py, c)).wait_recv()
                else:
                    got, other = half(theirs, t, c), half(theirs, t, 1 - c)
                    _dma(got, got, s_fwd.at[i], r_fwd.at[i], sib).wait_send()
                    _dma(other, other, s_fwd.at[i], r_fwd.at[i], sib).wait_recv()
            _dma(ins[t].at[layers[t]], block(outs, t, j), s_own.at[t], r_own.at[t], sib).wait()

    return _Hosted(list(shards), out_shapes, _dma_sems(n, n, 3 * n, 3 * n, 3 * n, 3 * n),
                   [(start_at, start), ("late", forward), ("last", finish)])


def _gather_ordered_comm(shard, l, axes, n_i, tiles_per_block):
    ba, ha = axes
    shp = list(shard.shape[1:])
    hs = shp[ha] // 2
    bs = shp[ba]
    shp[ba] *= N_CHIPS

    def half(ref, h):
        return _sub(ref, ha, h * hs, hs)

    def block(outs, blk):
        return _sub(outs[0], ba, blk * bs, bs)

    def start(ks, with_own):
        def fn(ins, outs, sems):
            s_own, r_own, s_ici, r_ici, _, _ = sems
            x, y, c, chips = _position()
            j = 2 * x + y
            if with_own:
                _dma(ins[0].at[l], block(outs, j), s_own.at[0], r_own.at[0], (x, y, 1 - c)).start()
            for k in ks:
                px, py = chips[k]
                _dma(half(ins[0].at[l], c), half(block(outs, j), c), s_ici.at[k], r_ici.at[k], (px, py, c)).start()
        return fn

    def forward(k):
        def fn(ins, outs, sems):
            _, _, s_ici, r_ici, s_fwd, r_fwd = sems
            x, y, c, chips = _position()
            px, py = chips[k]
            got = half(block(outs, 2 * px + py), c)
            _dma(got, got, s_ici.at[k], r_ici.at[k], (px, py, c)).wait_recv()
            _dma(got, got, s_fwd.at[k], r_fwd.at[k], (x, y, 1 - c)).start()
        return fn

    def complete(k):
        def fn(ins, outs, sems):
            _, _, _, _, s_fwd, r_fwd = sems
            x, y, c, chips = _position()
            px, py = chips[k]
            other = half(block(outs, 2 * px + py), 1 - c)
            _dma(other, other, s_fwd.at[k], r_fwd.at[k], (x, y, 1 - c)).wait_recv()
        return fn

    def finish(ins, outs, sems):
        s_own, r_own, s_ici, r_ici, s_fwd, r_fwd = sems
        x, y, c, chips = _position()
        j = 2 * x + y
        sent = half(block(outs, j), c)
        for k, (px, py) in enumerate(chips):
            _dma(sent, sent, s_ici.at[k], r_ici.at[k], (px, py, c)).wait_send()
            got = half(block(outs, 2 * px + py), c)
            _dma(got, got, s_fwd.at[k], r_fwd.at[k], (x, y, 1 - c)).wait_send()
        _dma(ins[0].at[l], block(outs, j), s_own.at[0], r_own.at[0], (x, y, 1 - c)).wait()

    use = lambda k: (k + 1) * tiles_per_block * n_i
    phases = [(0, start([0, 1], True)), (min(n_i // 2, use(2) - 2), start([2], False))]
    for k in range(3):
        phases += [(use(k) - 2, forward(k)), (use(k) - 1, complete(k))]
    phases.append(("last", finish))
    return _Hosted([shard], [jax.ShapeDtypeStruct(tuple(shp), shard.dtype)], _dma_sems(1, 1, 3, 3, 3, 3), phases)


def _halved(shape, axis):
    shp = list(shape)
    shp[axis] //= 2
    return tuple(shp)


def _pair_comm(grads, axes):
    n = len(grads)

    def copies(ins, outs, sems):
        x, y, c, _ = _position()
        cps = []
        for t in range(n):
            ha = axes[t][1]
            hs = grads[t].shape[ha] // 2
            cps.append(_dma(_sub(ins[t], ha, (1 - c) * hs, hs), outs[t], sems[0].at[t], sems[1].at[t], (x, y, 1 - c)))
        return cps

    def start(ins, outs, sems):
        for cp in copies(ins, outs, sems):
            cp.start()

    def finish(ins, outs, sems):
        for cp in copies(ins, outs, sems):
            cp.wait()

    return _Hosted(list(grads), [jax.ShapeDtypeStruct(_halved(g.shape, ax[1]), g.dtype) for g, ax in zip(grads, axes)],
                   _dma_sems(n, n), [("first", start), ("last", finish)])


def _block_comm(sums, axes):
    n = len(sums)
    out_shapes = []
    for s, (ba, _) in zip(sums, axes):
        shp = list(s.shape)
        shp[ba] //= N_CHIPS
        out_shapes.append(jax.ShapeDtypeStruct((3,) + tuple(shp), s.dtype))

    def copies(ins, outs, sems):
        x, y, c, chips = _position()
        cps = []
        for t in range(n):
            ba = axes[t][0]
            bs = sums[t].shape[ba] // N_CHIPS
            for k, (px, py) in enumerate(chips):
                cps.append(_dma(_sub(ins[t], ba, (2 * px + py) * bs, bs), outs[t].at[k],
                                sems[0].at[3 * t + k], sems[1].at[3 * t + k], (px, py, c)))
        return cps

    def start(ins, outs, sems):
        for cp in copies(ins, outs, sems):
            cp.start()

    def finish(ins, outs, sems):
        for cp in copies(ins, outs, sems):
            cp.wait()

    return _Hosted(list(sums), out_shapes, _dma_sems(3 * n, 3 * n), [("first", start), ("last", finish)])


def _sibling_share(shards, axes):
    n = len(shards)

    def body(*refs):
        outs = refs[n:2 * n]
        ssem, rsem = refs[2 * n:]
        x, y, c, _ = _position()
        cps = []
        for t in range(n):
            ha = axes[t][1]
            hs = shards[t].shape[ha] // 2
            mine = _sub(outs[t], ha, c * hs, hs)
            cps.append(_dma(mine, mine, ssem.at[t], rsem.at[t], (x, y, 1 - c)))
        for cp in cps:
            cp.start()
        for cp in cps:
            cp.wait()

    return pl.pallas_call(
        body, name="sibling_share",
        in_specs=[ANY] * n, out_specs=[ANY] * n,
        out_shape=[jax.ShapeDtypeStruct(s.shape, s.dtype) for s in shards],
        input_output_aliases={t: t for t in range(n)},
        scratch_shapes=_dma_sems(n, n),
    )(*shards)


def _small_allreduce(p):
    R, C = p.shape

    def body(p_ref, o_ref, sib_buf, chip_sums, ssem, rsem):
        x, y, c, chips = _position()
        j = 2 * x + y
        sib = pltpu.make_async_remote_copy(
            src_ref=p_ref, dst_ref=sib_buf, send_sem=ssem.at[0], recv_sem=rsem.at[0],
            device_id=(x, y, 1 - c), device_id_type=MESH)
        sib.start()
        sib.wait()
        chip_sums[j] = p_ref[...] + sib_buf[...]
        cps = [pltpu.make_async_remote_copy(
            src_ref=chip_sums.at[j], dst_ref=chip_sums.at[j], send_sem=ssem.at[1 + k], recv_sem=rsem.at[1 + k],
            device_id=(px, py, c), device_id_type=MESH) for k, (px, py) in enumerate(chips)]
        for cp in cps:
            cp.start()
        for k, (px, py) in enumerate(chips):
            pltpu.make_async_remote_copy(
                src_ref=chip_sums.at[j], dst_ref=chip_sums.at[2 * px + py],
                send_sem=ssem.at[1 + k], recv_sem=rsem.at[1 + k],
                device_id=(px, py, c), device_id_type=MESH).wait()
        o_ref[...] = ((chip_sums[0] + chip_sums[1]) + chip_sums[2]) + chip_sums[3]

    vm = pl.BlockSpec(memory_space=pltpu.VMEM)
    return pl.pallas_call(
        body, name="small_allreduce",
        in_specs=[vm], out_specs=vm, out_shape=jax.ShapeDtypeStruct((R, C), F32),
        scratch_shapes=[pltpu.VMEM((R, C), F32), pltpu.VMEM((N_CHIPS, R, C), F32),
                        pltpu.SemaphoreType.DMA((4,)), pltpu.SemaphoreType.DMA((4,))],
        compiler_params=pltpu.CompilerParams(vmem_limit_bytes=VMEM_LIMIT_BYTES),
    )(p)


def _add_half(g, recv, half_axis, pos):
    B, R, C = recv.shape
    tr = _tile(R, 256)
    tc = _tile(C, 2816)
    nr, nc = R // tr, C // tc
    if half_axis == 1:
        g_map = lambda b, i, j, pos_ref: (b, pos_ref[0] * nr + i, j)
    else:
        g_map = lambda b, i, j, pos_ref: (b, i, pos_ref[0] * nc + j)

    def body(pos_ref, g_ref, r_ref, o_ref, ob_ref):
        del pos_ref
        s = g_ref[...] + r_ref[...]
        o_ref[...] = s
        ob_ref[...] = s.astype(BF16)

    spec = lambda: pl.BlockSpec((None, tr, tc), lambda b, i, j, pos_ref: (b, i, j))
    return pl.pallas_call(
        body, name="add_half",
        grid_spec=pltpu.PrefetchScalarGridSpec(
            num_scalar_prefetch=1, grid=(B, nr, nc),
            in_specs=[pl.BlockSpec((None, tr, tc), g_map), spec()],
            out_specs=[spec(), spec()]),
        out_shape=[jax.ShapeDtypeStruct((B, R, C), F32), jax.ShapeDtypeStruct((B, R, C), BF16)],
        compiler_params=_params("parallel", "parallel", "parallel"),
    )(pos, g, recv)


def _sum_half(own, recv, axes, pos):
    _, B, rs, cs = recv.shape
    ba, ha = axes
    tr = _tile(rs, 128)
    nr = rs // tr

    def body(pos_ref, o_ref, r0_ref, r1_ref, r2_ref, out_ref):
        del pos_ref
        out_ref[...] = (((o_ref[...] + r0_ref[...].astype(F32)) + r1_ref[...].astype(F32))
                        + r2_ref[...].astype(F32))

    if ba == 2:
        own_spec = pl.BlockSpec((None, tr, cs), lambda b, i, pos_ref: (b, i, pos_ref[1]))
    else:
        own_spec = pl.BlockSpec((None, tr, cs), lambda b, i, pos_ref: (b, pos_ref[1] * nr + i, 0))
    if ha == 1:
        out_shape, out_map = (B, 2 * rs, cs), (lambda b, i, pos_ref: (b, pos_ref[0] * nr + i, 0))
    else:
        out_shape, out_map = (B, rs, 2 * cs), (lambda b, i, pos_ref: (b, i, pos_ref[0]))
    rspec = lambda k: pl.BlockSpec((None, None, tr, cs), lambda b, i, pos_ref: (k, b, i, 0))
    return pl.pallas_call(
        body, name="sum_half",
        grid_spec=pltpu.PrefetchScalarGridSpec(
            num_scalar_prefetch=1, grid=(B, nr),
            in_specs=[own_spec, rspec(0), rspec(1), rspec(2)],
            out_specs=pl.BlockSpec((None, tr, cs), out_map)),
        out_shape=jax.ShapeDtypeStruct(out_shape, F32),
        compiler_params=_params("parallel", "parallel"),
    )(pos, own, recv, recv, recv)


def _adamw(w, g, m, v):
    shape = w.shape
    C = shape[-1]
    R = w.size // C
    tr = R
    for cand in (512, 256, 128, 64, 32, 16, 8):
        if R % cand == 0 and cand * C * 4 <= (1 << 20):
            tr = cand
            break
    bc1 = 1.0 - ADAM_B1 ** ADAM_STEP
    bc2 = 1.0 - ADAM_B2 ** ADAM_STEP

    def body(w_ref, g_ref, m_ref, v_ref, d_ref, mo_ref, vo_ref):
        gg = g_ref[...]
        mn = ADAM_B1 * m_ref[...] + (1.0 - ADAM_B1) * gg
        vn = ADAM_B2 * v_ref[...] + (1.0 - ADAM_B2) * (gg * gg)
        mo_ref[...] = mn
        vo_ref[...] = vn
        d_ref[...] = -ADAM_LR * ((mn / bc1) / (jnp.sqrt(vn / bc2) + ADAM_EPS) + ADAM_WD * w_ref[...])

    spec = lambda: pl.BlockSpec((tr, C), lambda i: (i, 0))
    outs = pl.pallas_call(
        body, name="adamw", grid=(R // tr,),
        in_specs=[spec()] * 4, out_specs=[spec()] * 3,
        out_shape=[jax.ShapeDtypeStruct((R, C), F32)] * 3,
        compiler_params=_params("parallel"),
    )(*[a.reshape(R, C) for a in (w, g, m, v)])
    return tuple(o.reshape(shape) for o in outs)


def kernel(x, mem, norm_g, mem_norm_g, w_in, gmlp_ln_g, gmlp_ln_b, w_s, b_s, conv_w, conv_b, conv_ln_g, conv_ln_b, w_kv, w_branch, w_out, final_norm_g, loss_target, m_norm_g, m_mem_norm_g, m_w_in, m_gmlp_ln_g, m_gmlp_ln_b, m_w_s, m_b_s, m_conv_w, m_conv_b, m_conv_ln_g, m_conv_ln_b, m_w_kv, m_w_branch, m_w_out, m_final_norm_g, v_norm_g, v_mem_norm_g, v_w_in, v_gmlp_ln_g, v_gmlp_ln_b, v_w_s, v_b_s, v_conv_w, v_conv_b, v_conv_ln_g, v_conv_ln_b, v_w_kv, v_w_branch, v_w_out, v_final_norm_g):
    xs, mems, tgt = x[0], mem[0], loss_target[0]
    S, D = xs.shape
    assert D // GROUPS == LANES and S % CHUNK == 0 and w_s.shape[-1] == CHUNK
    L = N_LAYERS

    assert L == 2
    shards = [w_in.astype(BF16)[:, None], w_kv.astype(BF16)[:, None], w_branch.astype(BF16), w_out.astype(BF16)[:, None],
              conv_w[:, None]]
    gather_axes = [(2, 1), (2, 1), (1, 2), (1, 2), (2, None)]
    reduce_axes = gather_axes[:4]
    gather_late = _gather_comm(shards[1:], [1] * 4, gather_axes[1:])

    tri = jnp.tril(jnp.ones((CHUNK, CHUNK), F32))
    t_mix = (w_s * tri).astype(BF16)
    t_mix_t = jnp.swapaxes(t_mix, -1, -2)
    bias_full = jnp.repeat(jnp.swapaxes(b_s, -1, -2), D // GROUPS, axis=-1)
    row = lambda a, l: a[l][None, :]
    xi, yi, ci = lax.axis_index("x"), lax.axis_index("y"), lax.axis_index("c")
    pos = jnp.stack([ci, 2 * xi + yi]).astype(jnp.int32)

    saved = []
    h = xs
    weights = [None] * L
    for l in range(L):
        if l == 0:
            hb = _rms_fwd(h, row(norm_g, l))
            n_i = S // _tile(S, 1024)
            chip = 2 * xi + yi
            blocks = jnp.stack([chip, 2 * (1 - xi) + yi, 2 * xi + (1 - yi), 2 * (1 - xi) + (1 - yi)])
            order = (2 * blocks[:, None] + jnp.arange(2)[None, :]).reshape(-1).astype(jnp.int32)
            gather_first = _both(
                _gather_ordered_comm(shards[0], 0, gather_axes[0], n_i, 2),
                _gather_comm(shards[1:] + shards[:1], [0] * 4 + [1], gather_axes[1:] + gather_axes[:1], 5 * n_i))
            z, fetched = _in_proj_own_first(hb, shards[0], 0, order, gather_first)
            weights[0] = list(fetched[:5])
            fetched = fetched[1:]
        else:
            z, hb, _ = _in_proj(h, row(norm_g, l), weights[l][0], 0)
        win, wkv, wb, wo, cw = weights[l]
        bra = _branch_a_fwd(z, row(gmlp_ln_g, l), row(gmlp_ln_b, l), t_mix[l], bias_full[l])
        c0, late = _conv_fwd(z, cw, row(conv_b, l), 0, gather_late if l == 0 else None)
        if l == 0:
            weights[1] = [fetched[4]] + list(late)
        brb = _branch_b_fwd(c0, z, row(conv_ln_g, l), row(conv_ln_b, l))
        mn, kv, kvt = _kv_fwd(mems, row(mem_norm_g, l), wkv, 0)
        brc = _branch_c_fwd(z, kv, kvt)
        merged, h_next = _merge_fwd(bra, brb, brc, z, h, wb[None], wo, 0)
        saved.append((h, z, hb, bra, c0, brb, mn, kv, kvt, brc, merged))
        h = h_next

    dx, loss_cols, d_final_g = _loss_bwd(h, tgt, final_norm_g[None, :])
    loss = lax.psum(0.5 * jnp.sum(loss_cols) / D, ("x", "y", "c"))

    def add_halves(grads, recv, axes):
        return [_add_half(g, r, ax[1], pos) for g, r, ax in zip(grads, recv, axes)]

    def finish_reduction(sums, blocks, axes):
        halves = [_sum_half(s, b, ax, pos) for (s, _), b, ax in zip(sums, blocks, axes)]
        return list(_sibling_share(halves, axes))

    small = {k: [None] * L for k in ("norm_g", "mem_norm_g", "gmlp_ln_g", "gmlp_ln_b", "w_s", "b_s", "conv_w",
                                     "conv_b", "conv_ln_g", "conv_ln_b")}
    early_axes = reduce_axes[1:]
    pending, reduced, early = None, [None] * L, None
    for l in reversed(range(L)):
        h_in, z, hb, bra, c0, brb, mn, kv, kvt, brc, merged = saved[l]
        win, wkv, wb, wo, cw = weights[l]
        dz_m, dproj, dbr, recv = _merge_bwd(dx, bra, brb, brc, z, wb[None], wo, 0,
                                            _pair_comm(pending, reduce_axes) if pending else None)
        sums = add_halves(pending, recv, reduce_axes) if pending else []
        g_o = _wgrad(merged, dx, (1, D, D), 0, 0, None)
        g_b = None
        for n, br in enumerate((bra, brb, brc)):
            g_b = _wgrad(br, dproj, (N_BRANCHES, D, D), n, 0, g_b, b_plane=n)
        dz_a, dws, dbs, dlg, dlb = _branch_a_bwd(z, dbr, row(gmlp_ln_g, l), row(gmlp_ln_b, l),
                                                t_mix[l], t_mix_t[l], bias_full[l], tri)
        dz_c, dk, dv = _branch_c_bwd(z, dbr, kv, kvt)
        g_kv, dmg = _kv_bwd(dk, dv, mn, mems, wkv, 0, None)
        ready = [g_kv, g_b, g_o]
        dc0, dz_b, dcg, dcbeta, dcb, recv_early = _branch_b_bwd(c0, z, dbr, row(conv_ln_g, l), row(conv_ln_b, l),
                                                                _pair_comm(ready, early_axes) if pending else None)
        sums_early = add_halves(ready, recv_early, early_axes) if pending else []
        dz_b, dcw, blocks = _conv_bwd(dc0, z, cw, dz_b, 0,
                                      _block_comm([sb for _, sb in sums], reduce_axes) if pending else None)
        if pending:
            reduced[l + 1] = finish_reduction(sums, blocks, reduce_axes)
        pieces =[(dz_a, 0, 3), (dz_b, 3, 3), (dz_c, 6, 2), (dz_m, 8, 3)]
        g_in = None
        for dzp, lo, _ in pieces:
            g_in = _wgrad(hb, dzp, (1, D, N_PLANES * D), 0, lo * D, g_in)
        if pending:
            riding = _block_comm([sb for _, sb in sums_early], early_axes)
            dx_half, dng_a, got = _in_proj_bwd(pieces, win, h_in, row(norm_g, l), dx, 0,
                                               _both(riding, _pair_comm([g_in], reduce_axes[:1])), (0, 2))
            early = finish_reduction(sums_early, got[:3], early_axes)
            sums_in = add_halves([g_in], got[3:], reduce_axes[:1])
            dx, dng_b, blocks_in = _in_proj_bwd(pieces, win, h_in, row(norm_g, l), dx, 0,
                                                _block_comm([sb for _, sb in sums_in], reduce_axes[:1]), (1, 2), dx_half)
            dng = dng_a + dng_b
            reduced[l] = finish_reduction(sums_in, blocks_in, reduce_axes[:1]) + early
        else:
            dx, dng, _ = _in_proj_bwd(pieces, win, h_in, row(norm_g, l), dx, 0)
        pending = [g_in, g_kv, g_b, g_o]
        for k, val in (("norm_g", dng), ("mem_norm_g", dmg), ("gmlp_ln_g", dlg), ("gmlp_ln_b", dlb), ("w_s", dws),
                       ("b_s", dbs), ("conv_w", dcw), ("conv_b", dcb), ("conv_ln_g", dcg), ("conv_ln_b", dcbeta)):
            small[k][l] = val
    grad_x = dx[None]

    grad_w_in, grad_w_kv, grad_w_branch, grad_w_out = [
        jnp.stack([reduced[0][t], reduced[1][t]]).reshape(w.shape) for t, w in enumerate((w_in, w_kv, w_branch, w_out))]

    order = [("norm_g", norm_g.shape), ("mem_norm_g", mem_norm_g.shape), ("gmlp_ln_g", gmlp_ln_g.shape),
             ("gmlp_ln_b", gmlp_ln_b.shape), ("w_s", w_s.shape), ("b_s", b_s.shape),
             ("conv_w", (L, CONV_K, D)), ("conv_b", conv_b.shape), ("conv_ln_g", conv_ln_g.shape),
             ("conv_ln_b", conv_ln_b.shape)]
    parts = [jnp.stack([v.reshape(shp[1:]) for v in small[k]]).reshape(-1, LANES) for k, shp in order]
    parts.append(d_final_g.reshape(-1, LANES))
    sizes = [p.shape[0] for p in parts]
    assert all(s % 8 == 0 for s in sizes)
    total = _small_allreduce(jnp.concatenate(parts, axis=0))
    red, off = {}, 0
    for (k, shp), n in zip(order + [("final_norm_g", final_norm_g.shape)], sizes):
        red[k] = total[off:off + n].reshape(shp)
        off += n
    cs = conv_w.shape[-1]
    red["conv_w"] = lax.dynamic_slice_in_dim(red["conv_w"], (2 * xi + yi) * cs, cs, axis=2)
    red.update(w_in=grad_w_in, w_kv=grad_w_kv, w_branch=grad_w_branch, w_out=grad_w_out)

    names = ["norm_g", "mem_norm_g", "w_in", "gmlp_ln_g", "gmlp_ln_b", "w_s", "b_s", "conv_w", "conv_b",
             "conv_ln_g", "conv_ln_b", "w_kv", "w_branch", "w_out", "final_norm_g"]
    weights = dict(norm_g=norm_g, mem_norm_g=mem_norm_g, w_in=w_in, gmlp_ln_g=gmlp_ln_g, gmlp_ln_b=gmlp_ln_b,
                   w_s=w_s, b_s=b_s, conv_w=conv_w, conv_b=conv_b, conv_ln_g=conv_ln_g, conv_ln_b=conv_ln_b,
                   w_kv=w_kv, w_branch=w_branch, w_out=w_out, final_norm_g=final_norm_g)
    ms = dict(norm_g=m_norm_g, mem_norm_g=m_mem_norm_g, w_in=m_w_in, gmlp_ln_g=m_gmlp_ln_g, gmlp_ln_b=m_gmlp_ln_b,
              w_s=m_w_s, b_s=m_b_s, conv_w=m_conv_w, conv_b=m_conv_b, conv_ln_g=m_conv_ln_g, conv_ln_b=m_conv_ln_b,
              w_kv=m_w_kv, w_branch=m_w_branch, w_out=m_w_out, final_norm_g=m_final_norm_g)
    vs = dict(norm_g=v_norm_g, mem_norm_g=v_mem_norm_g, w_in=v_w_in, gmlp_ln_g=v_gmlp_ln_g, gmlp_ln_b=v_gmlp_ln_b,
              w_s=v_w_s, b_s=v_b_s, conv_w=v_conv_w, conv_b=v_conv_b, conv_ln_g=v_conv_ln_g, conv_ln_b=v_conv_ln_b,
              w_kv=v_w_kv, w_branch=v_w_branch, w_out=v_w_out, final_norm_g=v_final_norm_g)
    deltas, new_m, new_v = [], [], []
    for k in names:
        w2 = weights[k] if weights[k].ndim > 1 else weights[k][None, :]
        d, mo, vo = _adamw(w2, red[k].reshape(w2.shape), ms[k].reshape(w2.shape), vs[k].reshape(w2.shape))
        deltas.append(d.reshape(weights[k].shape))
        new_m.append(mo.reshape(weights[k].shape))
        new_v.append(vo.reshape(weights[k].shape))
    grads = [red[k].reshape(weights[k].shape) for k in names]
    return (loss, grad_x, *grads, *deltas, *new_m, *new_v)
```

```python
import math

import jax
import jax.numpy as jnp
from jax import lax
from jax.experimental import pallas as pl
from jax.experimental.pallas import tpu as pltpu

F32 = jnp.float32
BF16 = jnp.bfloat16

N_LAYERS = 2
N_BRANCHES = 3
N_PLANES = 11
N_CHIPS = 4
CHUNK = 128
GROUPS = 8
HEADS = 4
CONV_K = 31
HALO = 32
LANES = 128
RMS_EPS = 1e-6
LN_EPS = 1e-5
ADAM_LR, ADAM_B1, ADAM_B2, ADAM_EPS, ADAM_WD, ADAM_STEP = 0.001, 0.9, 0.999, 1e-08, 0.01, 10
VMEM_LIMIT_BYTES = 48 * 1024 * 1024
MESH = pl.DeviceIdType.MESH
ANY = pl.BlockSpec(memory_space=pl.ANY)


def _params(*sem):
    return pltpu.CompilerParams(dimension_semantics=sem, vmem_limit_bytes=VMEM_LIMIT_BYTES)


def _tile(n, pref):
    t = min(n, pref)
    assert n % t == 0, (n, t)
    return t


def _dot(a, b):
    return jnp.dot(a, b, preferred_element_type=F32)


def _dot_tb(a, b):
    return lax.dot_general(a, b, (((1,), (1,)), ((), ())), preferred_element_type=F32)


def _dot_ta(a, b):
    return lax.dot_general(a, b, (((0,), (0,)), ((), ())), preferred_element_type=F32)


def _sigmoid(x):
    return 1.0 / (1.0 + jnp.exp(-x))


def _silu(x):
    return x * _sigmoid(x)


def _silu_and_grad(x):
    s = _sigmoid(x)
    return x * s, s * (1.0 + x * (1.0 - s))


_GELU_C = math.sqrt(2.0 / math.pi)
_GELU_A = 0.044715


def _gelu(x):
    return x * (0.5 * (1.0 + jnp.tanh(_GELU_C * (x + _GELU_A * (x * x * x)))))


def _gelu_and_grad(x):
    x2 = x * x
    t = jnp.tanh(_GELU_C * (x + _GELU_A * (x2 * x)))
    cdf = 0.5 * (1.0 + t)
    dcdf = 0.5 * (1.0 - t * t) * (_GELU_C * (1.0 + 3.0 * _GELU_A * x2))
    return x * cdf, cdf + x * dcdf


def _rows(i, n):
    return pl.ds(pl.multiple_of(i * n, n), n)


def _grouped_loop(n, group, body, init):
    assert n % group == 0

    def trip(i, c):
        for u in range(group):
            c = body(i * group + u, c)
        return c
    return lax.fori_loop(0, n // group, trip, init)


def _mean(x):
    return jnp.mean(x, axis=-1, keepdims=True)


def _colsum(x):
    return jnp.sum(x, axis=0, keepdims=True)


def _tok(ts, d):
    return pl.BlockSpec((ts, d), lambda i: (i, 0))


def _zcol(ts, d, n):
    return pl.BlockSpec((ts, d), lambda i: (i, n))


def _row1(d):
    return pl.BlockSpec((1, d), lambda i: (0, 0))


def _plane(ts, d, n):
    return pl.BlockSpec((None, ts, d), lambda i: (n, i, 0))


class _Hosted:
    def __init__(self, ins, out_shapes, sem_shapes, phases):
        self.ins, self.out_shapes, self.sem_shapes, self.phases = ins, out_shapes, sem_shapes, phases


def _host(comm, n_in, n_out, n_scr, grid, body, n_prefetch=0, wants_results=False):
    if comm is None:
        return body, [], [], [], [], None
    ci, co = len(comm.ins), len(comm.out_shapes)
    total = math.prod(grid)
    at = {"first": 0, "late": (7 * total) // 8, "last": total - 1}

    def wrapped(*refs):
        pre, refs = refs[:n_prefetch], refs[n_prefetch:]
        ins, cin = refs[:n_in], refs[n_in:n_in + ci]
        o0 = n_in + ci
        outs, cout = refs[o0:o0 + n_out], refs[o0 + n_out:o0 + n_out + co]
        s0 = o0 + n_out + co
        scr, csem = refs[s0:s0 + n_scr], refs[s0 + n_scr:]
        step = pl.program_id(0)
        for a in range(1, len(grid)):
            step = step * grid[a] + pl.program_id(a)

        def run(last):
            for w, fn in comm.phases:
                if (w == "last") == last:
                    @pl.when(step == at.get(w, w))
                    def _(fn=fn):
                        fn(cin, cout, csem)
        run(False)
        body(*pre, *ins, *outs, *scr, **({"cout": cout} if wants_results else {}))
        run(True)

    return wrapped, [ANY] * ci, [ANY] * co, list(comm.out_shapes), list(comm.sem_shapes), ("arbitrary",) * len(grid)


def _in_proj(x, g, w_all, l, comm=None):
    S, D = x.shape
    N = w_all.shape[2]
    tm, tn = _tile(S, 1024), _tile(N, 1408)
    grid = (S // tm, N // tn)

    def body(x_ref, g_ref, w_ref, z_ref, hb_ref, h_scr):
        @pl.when(pl.program_id(1) == 0)
        def _():
            def slab(i, c):
                r = _rows(i, 16)
                xs = x_ref[r, :]
                h = (xs * lax.rsqrt(_mean(xs * xs) + RMS_EPS)) * g_ref[...]
                hb = h.astype(BF16)
                h_scr[r, :] = hb
                hb_ref[r, :] = hb
                return c
            _grouped_loop(tm // 16, 4, slab, 0)
        z_ref[...] = _dot(h_scr[...], w_ref[...])

    body, c_in, c_out, c_shapes, c_scr, sem = _host(comm, 3, 2, 1, grid, body)
    outs = pl.pallas_call(
        body, name="in_proj", grid=grid,
        in_specs=[pl.BlockSpec((tm, D), lambda i, j: (i, 0)),
                  pl.BlockSpec((1, D), lambda i, j: (0, 0)),
                  pl.BlockSpec((None, D, tn), lambda i, j: (l, 0, j))] + c_in,
        out_specs=[pl.BlockSpec((tm, tn), lambda i, j: (i, j)),
                   pl.BlockSpec((tm, D), lambda i, j: (i, 0))] + c_out,
        out_shape=[jax.ShapeDtypeStruct((S, N), F32), jax.ShapeDtypeStruct((S, D), BF16)] + c_shapes,
        scratch_shapes=[pltpu.VMEM((tm, D), BF16)] + c_scr,
        compiler_params=_params(*(sem or ("parallel", "arbitrary"))),
    )(x, g, w_all, *(comm.ins if comm else []))
    return outs[0], outs[1], outs[2:]


def _branch_a_fwd(z, lng, lnb, t_mix, bias_full):
    S = z.shape[0]
    D = lng.shape[1]
    GD = D // GROUPS
    ts = _tile(S, 256)

    def body(zu_ref, zv_ref, zg_ref, lng_ref, lnb_ref, t_ref, bias_ref, a_ref, v_scr):
        def chunk(ci, carry):
            c0 = pl.multiple_of(ci * CHUNK, CHUNK)

            def slab(si, c):
                r = pl.multiple_of(si * 16, 16)
                vg = _gelu(zv_ref[pl.ds(pl.multiple_of(c0 + r, 16), 16), :])
                xc = vg - _mean(vg)
                y = xc * lax.rsqrt(_mean(xc * xc) + LN_EPS)
                v_scr[pl.ds(r, 16), :] = (y * lng_ref[...] + lnb_ref[...]).astype(BF16)
                return c
            _grouped_loop(CHUNK // 16, 4, slab, 0)
            for g in range(GROUPS):
                cols = slice(g * GD, (g + 1) * GD)
                sv = _dot(t_ref[g], v_scr[:, cols]) + bias_ref[:, cols]
                u = _gelu(zu_ref[pl.ds(c0, CHUNK), cols])
                a_ref[pl.ds(c0, CHUNK), cols] = ((u * sv) * _silu(zg_ref[pl.ds(c0, CHUNK), cols])).astype(BF16)
            return carry
        lax.fori_loop(0, ts // CHUNK, chunk, 0)

    return pl.pallas_call(
        body, name="branch_a_fwd", grid=(S // ts,),
        in_specs=[_zcol(ts, D, 0), _zcol(ts, D, 1), _zcol(ts, D, 2), _row1(D), _row1(D),
                  pl.BlockSpec((GROUPS, CHUNK, CHUNK), lambda i: (0, 0, 0)),
                  pl.BlockSpec((CHUNK, D), lambda i: (0, 0))],
        out_specs=_tok(ts, D),
        out_shape=jax.ShapeDtypeStruct((S, D), BF16),
        scratch_shapes=[pltpu.VMEM((CHUNK, D), BF16)],
        compiler_params=_params("parallel"),
    )(z, z, z, lng, lnb, t_mix, bias_full)


SH = HALO - CONV_K + 1


def _fill_shifts(sh_scr, src_scr, cols, ts):
    for j in range(1, 8):
        for g in range(ts // 32):
            sh_scr[j - 1, pl.ds(g * 32, 32), :] = src_scr[pl.ds(g * 32 + j, 32), cols]
        sh_scr[j - 1, pl.ds(ts, HALO - 8), :] = src_scr[pl.ds(ts + j, HALO - 8), cols]


def _shifted(sh_scr, src_scr, cols, row0, n):
    q, j = divmod(row0, 8)
    if j == 0:
        return src_scr[pl.ds(row0, n), cols]
    return sh_scr[j - 1, pl.ds(8 * q, n), :]


def _conv_fwd(z, w_all, cb, l, comm=None):
    S = z.shape[0]
    D = cb.shape[1]
    ts = _tile(S, 256)
    CB = 256
    hb = ts // HALO

    def body(za_ref, zb_ref, zap_ref, zbp_ref, w_ref, b_ref, c0_ref, glu_scr, sh_scr):
        i = pl.program_id(0)
        prev = zap_ref[...] * _sigmoid(zbp_ref[...])
        glu_scr[0:HALO, :] = jnp.where(i == 0, 0.0, prev)

        def fill(si, c):
            r = _rows(si, 32)
            glu_scr[pl.ds(pl.multiple_of(HALO + si * 32, 32), 32), :] = za_ref[r, :] * _sigmoid(zb_ref[r, :])
            return c
        _grouped_loop(ts // 32, 2, fill, 0)

        def colblock(ci, c):
            cols = _rows(ci, CB)
            _fill_shifts(sh_scr, glu_scr, cols, ts)
            for s in range(ts // 32):
                acc = jnp.zeros((32, CB), F32)
                for k in range(CONV_K):
                    acc = acc + _shifted(sh_scr, glu_scr, cols, s * 32 + k + SH, 32) * w_ref[k:k + 1, cols]
                c0_ref[pl.ds(s * 32, 32), cols] = acc + b_ref[:, cols]
            return c
        lax.fori_loop(0, D // CB, colblock, 0)

    prev_map = lambda n: (lambda i: (jnp.maximum(i * hb - 1, 0), n))
    grid = (S // ts,)
    body, c_in, c_out, c_shapes, c_scr, sem = _host(comm, 6, 1, 2, grid, body)
    outs = pl.pallas_call(
        body, name="conv_fwd", grid=grid,
        in_specs=[_zcol(ts, D, 3), _zcol(ts, D, 4),
                  pl.BlockSpec((HALO, D), prev_map(3)), pl.BlockSpec((HALO, D), prev_map(4)),
                  pl.BlockSpec((None, CONV_K, D), lambda i: (l, 0, 0)), _row1(D)] + c_in,
        out_specs=[_tok(ts, D)] + c_out,
        out_shape=[jax.ShapeDtypeStruct((S, D), F32)] + c_shapes,
        scratch_shapes=[pltpu.VMEM((ts + HALO, D), F32), pltpu.VMEM((7, ts + HALO, CB), F32)] + c_scr,
        compiler_params=_params(*(sem or ("parallel",))),
    )(z, z, z, z, w_all, cb, *(comm.ins if comm else []))
    return outs[0], outs[1:]


def _branch_b_fwd(c0, z, g, b):
    S, D = c0.shape
    ts = _tile(S, 512)

    def body(c0_ref, zg_ref, g_ref, b_ref, o_ref):
        def slab(si, c):
            r = _rows(si, 16)
            v = c0_ref[r, :]
            xc = v - _mean(v)
            c1 = (xc * lax.rsqrt(_mean(xc * xc) + LN_EPS)) * g_ref[...] + b_ref[...]
            o_ref[r, :] = (_silu(c1) * _silu(zg_ref[r, :])).astype(BF16)
            return c
        _grouped_loop(ts // 16, 4, slab, 0)

    return pl.pallas_call(
        body, name="branch_b_fwd", grid=(S // ts,),
        in_specs=[_tok(ts, D), _zcol(ts, D, 5), _row1(D), _row1(D)],
        out_specs=_tok(ts, D),
        out_shape=jax.ShapeDtypeStruct((S, D), BF16),
        compiler_params=_params("parallel"),
    )(c0, z, g, b)


def _kv_fwd(mem, mg, wkv_all, l):
    M, D = mem.shape

    def body(mem_ref, g_ref, w_ref, mn_ref, kv_ref, kvt_ref):
        m = mem_ref[...]
        mn = ((m * lax.rsqrt(_mean(m * m) + RMS_EPS)) * g_ref[...]).astype(BF16)
        mn_ref[...] = mn
        kv = _dot(mn, w_ref[...])
        kv_ref[...] = kv.astype(BF16)
        kvt_ref[...] = kv.T.astype(BF16)

    return pl.pallas_call(
        body, name="kv_fwd", grid=(1,),
        in_specs=[pl.BlockSpec((M, D), lambda i: (0, 0)), _row1(D),
                  pl.BlockSpec((None, D, 2 * D), lambda i: (l, 0, 0))],
        out_specs=[pl.BlockSpec((M, D), lambda i: (0, 0)), pl.BlockSpec((M, 2 * D), lambda i: (0, 0)),
                   pl.BlockSpec((2 * D, M), lambda i: (0, 0))],
        out_shape=[jax.ShapeDtypeStruct((M, D), BF16), jax.ShapeDtypeStruct((M, 2 * D), BF16),
                   jax.ShapeDtypeStruct((2 * D, M), BF16)],
        compiler_params=_params("arbitrary"),
    )(mem, mg, wkv_all)


def _softmax_rows(s):
    e = jnp.exp(s - jnp.max(s, axis=-1, keepdims=True))
    return e / jnp.sum(e, axis=-1, keepdims=True)


def _branch_c_fwd(z, kv, kvt):
    S = z.shape[0]
    M, D2 = kv.shape
    D = D2 // 2
    HD = D // HEADS
    scale = 1.0 / math.sqrt(HD)
    ts = _tile(S, 256)

    def body(q_ref, zg_ref, kt_ref, v_ref, o_ref):
        for h in range(HEADS):
            cols = slice(h * HD, (h + 1) * HD)
            p = _softmax_rows(_dot(q_ref[:, cols].astype(BF16), kt_ref[cols, :]) * scale)
            att = _dot(p.astype(BF16), v_ref[:, cols])
            o_ref[:, cols] = (att * _silu(zg_ref[:, cols])).astype(BF16)

    return pl.pallas_call(
        body, name="branch_c_fwd", grid=(S // ts,),
        in_specs=[_zcol(ts, D, 6), _zcol(ts, D, 7),
                  pl.BlockSpec((D, M), lambda i: (0, 0)), pl.BlockSpec((M, D), lambda i: (0, 1))],
        out_specs=_tok(ts, D),
        out_shape=jax.ShapeDtypeStruct((S, D), BF16),
        compiler_params=_params("parallel"),
    )(z, z, kvt, kv)


def _merge_fwd(bra, brb, brc, z, x, wb_all, wo_all, l):
    S, D = x.shape
    ts = _tile(S, 512)
    CB = 256

    def body(a_ref, b_ref, c_ref, zm0_ref, zm1_ref, zm2_ref, x_ref, wb_ref, wo_ref, m_ref, xo_ref):
        brs = (a_ref, b_ref, c_ref)
        zms = (zm0_ref, zm1_ref, zm2_ref)
        for cb in range(D // CB):
            cols = slice(cb * CB, (cb + 1) * CB)
            acc = None
            for n in range(N_BRANCHES):
                t = _sigmoid(zms[n][:, cols]) * _dot(brs[n][...], wb_ref[n, :, cols])
                acc = t if acc is None else acc + t
            m_ref[:, cols] = acc.astype(BF16)
        for cb in range(D // CB):
            cols = slice(cb * CB, (cb + 1) * CB)
            xo_ref[:, cols] = x_ref[:, cols] + _dot(m_ref[...], wo_ref[:, cols])

    return pl.pallas_call(
        body, name="merge_fwd", grid=(S // ts,),
        in_specs=[_tok(ts, D), _tok(ts, D), _tok(ts, D), _zcol(ts, D, 8), _zcol(ts, D, 9), _zcol(ts, D, 10),
                  _tok(ts, D),
                  pl.BlockSpec((None, N_BRANCHES, D, D), lambda i: (l, 0, 0, 0)),
                  pl.BlockSpec((None, D, D), lambda i: (l, 0, 0))],
        out_specs=[_tok(ts, D), _tok(ts, D)],
        out_shape=[jax.ShapeDtypeStruct((S, D), BF16), jax.ShapeDtypeStruct((S, D), F32)],
        compiler_params=_params("parallel"),
    )(bra, brb, brc, z, z, z, x, wb_all, wo_all)


def _loss_bwd(x, target, fg):
    S, D = x.shape
    ts = _tile(S, 512)

    def body(x_ref, t_ref, g_ref, dx_ref, ls_ref, dg_ref):
        @pl.when(pl.program_id(0) == 0)
        def _():
            ls_ref[...] = jnp.zeros_like(ls_ref)
            dg_ref[...] = jnp.zeros_like(dg_ref)

        def slab(si, c):
            ls, dg = c
            r = _rows(si, 16)
            xs = x_ref[r, :]
            rs = lax.rsqrt(_mean(xs * xs) + RMS_EPS)
            n = xs * rs
            e = n * g_ref[...] - t_ref[r, :]
            dy = e * (1.0 / D)
            dn = dy * g_ref[...]
            dx_ref[r, :] = rs * (dn - n * _mean(dn * n))
            return ls + _colsum(e * e), dg + _colsum(dy * n)
        zero = jnp.zeros((1, D), F32)
        ls, dg = _grouped_loop(ts // 16, 4, slab, (zero, zero))
        ls_ref[...] += ls
        dg_ref[...] += dg

    return pl.pallas_call(
        body, name="loss_bwd", grid=(S // ts,),
        in_specs=[_tok(ts, D), _tok(ts, D), _row1(D)],
        out_specs=[_tok(ts, D), _row1(D), _row1(D)],
        out_shape=[jax.ShapeDtypeStruct((S, D), F32), jax.ShapeDtypeStruct((1, D), F32),
                   jax.ShapeDtypeStruct((1, D), F32)],
        compiler_params=_params("arbitrary"),
    )(x, target, fg)


def _merge_bwd(dxp, bra, brb, brc, z, c0, lng, lnb, wb_all, wo_all, l, comm=None):
    S, D = dxp.shape
    ts = _tile(S, 256)
    CB = 256

    def body(d_ref, a_ref, b_ref, c_ref, zm0_ref, zm1_ref, zm2_ref, c0_ref, zg_ref, g_ref, be_ref, wb_ref, wo_ref,
             dzm_ref, dproj_ref, dbr_ref, dc0_ref, dzb_ref, dg_ref, db_ref, dcb_ref, dxb_scr, dbb_scr):
        @pl.when(pl.program_id(0) == 0)
        def _():
            dg_ref[...] = jnp.zeros_like(dg_ref)
            db_ref[...] = jnp.zeros_like(db_ref)
            dcb_ref[...] = jnp.zeros_like(dcb_ref)

        brs = (a_ref, b_ref, c_ref)
        zms = (zm0_ref, zm1_ref, zm2_ref)
        dxb_scr[...] = d_ref[...].astype(BF16)
        for cb in range(D // CB):
            cols = slice(cb * CB, (cb + 1) * CB)
            dm = _dot_tb(dxb_scr[...], wo_ref[cols, :])
            for n in range(N_BRANCHES):
                proj = _dot(brs[n][...], wb_ref[n, :, cols])
                g = _sigmoid(zms[n][:, cols])
                dzm_ref[:, n * D + cb * CB:n * D + (cb + 1) * CB] = ((dm * proj) * (g * (1.0 - g))).astype(BF16)
                dproj_ref[n, :, cols] = (dm * g).astype(BF16)
        for cb in range(D // CB):
            cols = slice(cb * CB, (cb + 1) * CB)
            dbb_scr[:, cols] = _dot_tb(dproj_ref[1], wb_ref[1, cols, :])

        dg = db = dcb = jnp.zeros((1, D), F32)
        for s in range(ts // 16):
            r = pl.ds(s * 16, 16)
            v = c0_ref[r, :]
            xc = v - _mean(v)
            rstd = lax.rsqrt(_mean(xc * xc) + LN_EPS)
            cn = xc * rstd
            c1 = cn * g_ref[...] + be_ref[...]
            c2, dc2_dc1 = _silu_and_grad(c1)
            sg, dsg = _silu_and_grad(zg_ref[r, :])
            d = dbb_scr[r, :]
            dzb_ref[r, :] = ((d * c2) * dsg).astype(BF16)
            dc1 = (d * sg) * dc2_dc1
            dcn = dc1 * g_ref[...]
            dc0 = rstd * ((dcn - _mean(dcn)) - cn * _mean(dcn * cn))
            dc0_ref[r, :] = dc0
            dg, db, dcb = dg + _colsum(dc1 * cn), db + _colsum(dc1), dcb + _colsum(dc0)
        dg_ref[...] += dg
        db_ref[...] += db
        dcb_ref[...] += dcb

        for plane, n in enumerate((0, 2)):
            for cb in range(D // CB):
                cols = slice(cb * CB, (cb + 1) * CB)
                dbr_ref[plane, :, cols] = _dot_tb(dproj_ref[n], wb_ref[n, cols, :])

    tokn = lambda n: pl.BlockSpec((n, ts, D), lambda i: (0, i, 0))
    grid = (S // ts,)
    body, c_in, c_out, c_shapes, c_scr, _ = _host(comm, 13, 8, 2, grid, body)
    outs = pl.pallas_call(
        body, name="merge_bwd", grid=grid,
        in_specs=[_tok(ts, D), _tok(ts, D), _tok(ts, D), _tok(ts, D),
                  _zcol(ts, D, 8), _zcol(ts, D, 9), _zcol(ts, D, 10), _tok(ts, D), _zcol(ts, D, 5), _row1(D), _row1(D),
                  pl.BlockSpec((None, N_BRANCHES, D, D), lambda i: (l, 0, 0, 0)),
                  pl.BlockSpec((None, D, D), lambda i: (l, 0, 0))] + c_in,
        out_specs=[_tok(ts, 3 * D), tokn(N_BRANCHES), tokn(2), _tok(ts, D), _zcol(ts, D, 2),
                   _row1(D), _row1(D), _row1(D)] + c_out,
        out_shape=[jax.ShapeDtypeStruct((S, 3 * D), BF16), jax.ShapeDtypeStruct((N_BRANCHES, S, D), BF16),
                   jax.ShapeDtypeStruct((2, S, D), F32), jax.ShapeDtypeStruct((S, D), F32),
                   jax.ShapeDtypeStruct((S, 3 * D), BF16), jax.ShapeDtypeStruct((1, D), F32),
                   jax.ShapeDtypeStruct((1, D), F32), jax.ShapeDtypeStruct((1, D), F32)] + c_shapes,
        scratch_shapes=[pltpu.VMEM((ts, D), BF16), pltpu.VMEM((ts, D), F32)] + c_scr,
        compiler_params=_params("arbitrary"),
    )(dxp, bra, brb, brc, z, z, z, c0, z, lng, lnb, wb_all, wo_all, *(comm.ins if comm else []))
    return (*outs[:8], outs[8:])


def _branch_a_bwd(z, dbr, lng, lnb, t_mix, t_mix_t, bias_full, mask):
    S = z.shape[0]
    D = lng.shape[1]
    GD = D // GROUPS
    ts = _tile(S, 256)
    nt = S // ts

    def body(zu_ref, zv_ref, zg_ref, d_ref, lng_ref, lnb_ref, t_ref, tt_ref, bias_ref, mask_ref,
             dz_ref, dws_ref, dbs_ref, dlg_ref, dlb_ref,
             v_scr, vn_scr, rstd_scr, dv_scr, dsv_acc):
        i = pl.program_id(0)

        @pl.when(i == 0)
        def _():
            dws_ref[...] = jnp.zeros_like(dws_ref)
            dlg_ref[...] = jnp.zeros_like(dlg_ref)
            dlb_ref[...] = jnp.zeros_like(dlb_ref)
            dsv_acc[...] = jnp.zeros_like(dsv_acc)

        def chunk(ci, carry):
            c0 = pl.multiple_of(ci * CHUNK, CHUNK)

            def ln_fwd(si, c):
                r = pl.multiple_of(si * 16, 16)
                vg = _gelu(zv_ref[pl.ds(pl.multiple_of(c0 + r, 16), 16), :])
                xc = vg - _mean(vg)
                rstd = lax.rsqrt(_mean(xc * xc) + LN_EPS)
                vn = xc * rstd
                vn_scr[pl.ds(r, 16), :] = vn
                rstd_scr[pl.ds(r, 16), :] = rstd
                v_scr[pl.ds(r, 16), :] = (vn * lng_ref[...] + lnb_ref[...]).astype(BF16)
                return c
            _grouped_loop(CHUNK // 16, 4, ln_fwd, 0)

            for g in range(GROUPS):
                cols = slice(g * GD, (g + 1) * GD)
                rows = pl.ds(c0, CHUNK)
                vb = v_scr[:, cols]
                sv = _dot(t_ref[g], vb) + bias_ref[:, cols]
                u, du = _gelu_and_grad(zu_ref[rows, cols])
                sg, dsg = _silu_and_grad(zg_ref[rows, cols])
                d = d_ref[rows, cols]
                dsv = (d * u) * sg
                dz_ref[rows, g * GD:(g + 1) * GD] = (((d * sv) * sg) * du).astype(BF16)
                dz_ref[rows, 2 * D + g * GD:2 * D + (g + 1) * GD] = (((d * u) * sv) * dsg).astype(BF16)
                dsvb = dsv.astype(BF16)
                dws_ref[g] += _dot_tb(dsvb, vb)
                dv_scr[:, cols] = _dot(tt_ref[g], dsvb)
                dsv_acc[:, cols] += dsv

            def ln_bwd(si, c):
                dlg, dlb = c
                r = pl.multiple_of(si * 16, 16)
                rr = pl.ds(r, 16)
                zrows = pl.ds(pl.multiple_of(c0 + r, 16), 16)
                dv = dv_scr[rr, :]
                vn = vn_scr[rr, :]
                dvn = dv * lng_ref[...]
                dvg = rstd_scr[rr, :] * ((dvn - _mean(dvn)) - vn * _mean(dvn * vn))
                _, gg = _gelu_and_grad(zv_ref[zrows, :])
                dz_ref[zrows, D:2 * D] = (dvg * gg).astype(BF16)
                return dlg + _colsum(dv * vn), dlb + _colsum(dv)
            zero = jnp.zeros((1, D), F32)
            dlg, dlb = _grouped_loop(CHUNK // 16, 4, ln_bwd, (zero, zero))
            dlg_ref[...] += dlg
            dlb_ref[...] += dlb
            return carry
        lax.fori_loop(0, ts // CHUNK, chunk, 0)

        @pl.when(i == nt - 1)
        def _():
            for g in range(GROUPS):
                cols = slice(g * GD, (g + 1) * GD)
                dws_ref[g] = dws_ref[g] * mask_ref[...]
                dbs_ref[g:g + 1, :] = _colsum(dsv_acc[:, cols].T)

    c2 = lambda shape: pl.BlockSpec(shape, lambda i: (0, 0))
    c3 = lambda: pl.BlockSpec((GROUPS, CHUNK, CHUNK), lambda i: (0, 0, 0))
    return pl.pallas_call(
        body, name="branch_a_bwd", grid=(nt,),
        in_specs=[_zcol(ts, D, 0), _zcol(ts, D, 1), _zcol(ts, D, 2), _plane(ts, D, 0), _row1(D), _row1(D),
                  c3(), c3(), c2((CHUNK, D)), c2((CHUNK, CHUNK))],
        out_specs=[_tok(ts, 3 * D), c3(), c2((GROUPS, CHUNK)), _row1(D), _row1(D)],
        out_shape=[jax.ShapeDtypeStruct((S, 3 * D), BF16), jax.ShapeDtypeStruct((GROUPS, CHUNK, CHUNK), F32),
                   jax.ShapeDtypeStruct((GROUPS, CHUNK), F32), jax.ShapeDtypeStruct((1, D), F32),
                   jax.ShapeDtypeStruct((1, D), F32)],
        scratch_shapes=[pltpu.VMEM((CHUNK, D), BF16), pltpu.VMEM((CHUNK, D), F32), pltpu.VMEM((CHUNK, 1), F32),
                        pltpu.VMEM((CHUNK, D), F32), pltpu.VMEM((CHUNK, D), F32)],
        compiler_params=_params("arbitrary"),
    )(z, z, z, dbr, lng, lnb, t_mix, t_mix_t, bias_full, mask)


def _branch_b_bwd(c0, z, dbr, g, b, comm=None):
    S, D = c0.shape
    ts = _tile(S, 512)

    def body(c0_ref, zg_ref, d_ref, g_ref, b_ref, dc0_ref, dz_ref, dg_ref, db_ref, dcb_ref):
        @pl.when(pl.program_id(0) == 0)
        def _():
            dg_ref[...] = jnp.zeros_like(dg_ref)
            db_ref[...] = jnp.zeros_like(db_ref)
            dcb_ref[...] = jnp.zeros_like(dcb_ref)

        def slab(si, c):
            dg, db, dcb = c
            r = _rows(si, 16)
            v = c0_ref[r, :]
            xc = v - _mean(v)
            rstd = lax.rsqrt(_mean(xc * xc) + LN_EPS)
            cn = xc * rstd
            c1 = cn * g_ref[...] + b_ref[...]
            c2, dc2_dc1 = _silu_and_grad(c1)
            sg, dsg = _silu_and_grad(zg_ref[r, :])
            d = d_ref[r, :]
            dz_ref[r, :] = ((d * c2) * dsg).astype(BF16)
            dc1 = (d * sg) * dc2_dc1
            dcn = dc1 * g_ref[...]
            dc0 = rstd * ((dcn - _mean(dcn)) - cn * _mean(dcn * cn))
            dc0_ref[r, :] = dc0
            return dg + _colsum(dc1 * cn), db + _colsum(dc1), dcb + _colsum(dc0)
        zero = jnp.zeros((1, D), F32)
        dg, db, dcb = _grouped_loop(ts // 16, 4, slab, (zero, zero, zero))
        dg_ref[...] += dg
        db_ref[...] += db
        dcb_ref[...] += dcb

    grid = (S // ts,)
    body, c_in, c_out, c_shapes, c_scr, _ = _host(comm, 5, 5, 0, grid, body)
    outs = pl.pallas_call(
        body, name="branch_b_bwd", grid=grid,
        in_specs=[_tok(ts, D), _zcol(ts, D, 5), _plane(ts, D, 1), _row1(D), _row1(D)] + c_in,
        out_specs=[_tok(ts, D), _zcol(ts, D, 2), _row1(D), _row1(D), _row1(D)] + c_out,
        out_shape=[jax.ShapeDtypeStruct((S, D), F32), jax.ShapeDtypeStruct((S, 3 * D), BF16),
                   jax.ShapeDtypeStruct((1, D), F32), jax.ShapeDtypeStruct((1, D), F32),
                   jax.ShapeDtypeStruct((1, D), F32)] + c_shapes,
        scratch_shapes=c_scr,
        compiler_params=_params("arbitrary"),
    )(c0, z, dbr, g, b, *(comm.ins if comm else []))
    return (*outs[:5], outs[5:])


def _conv_bwd(dc0, z, w_all, dz_b, l, comm=None):
    S, D = dc0.shape
    ts = _tile(S, 256)
    nt = S // ts
    CB = 256
    hb = ts // HALO
    nh = S // HALO

    def body(dc_ref, dcn_ref, za_ref, zb_ref, w_ref, dzin_ref, dz_ref, dw_ref, dc_scr, dw_acc, dsh_scr):
        del dzin_ref
        i = pl.program_id(0)

        @pl.when(i == 0)
        def _():
            dw_acc[...] = jnp.zeros_like(dw_acc)

        dc_scr[ts:ts + HALO, :] = jnp.where(i == nt - 1, 0.0, dcn_ref[...])

        def fill(si, c):
            r = _rows(si, 32)
            dc_scr[r, :] = dc_ref[r, :]
            return c
        _grouped_loop(ts // 32, 2, fill, 0)

        def colblock(ci, c):
            cols = _rows(ci, CB)
            cols_b = pl.ds(pl.multiple_of(D + ci * CB, CB), CB)
            _fill_shifts(dsh_scr, dc_scr, cols, ts)
            for s in range(ts // 32):
                rows = pl.ds(s * 32, 32)
                za = za_ref[rows, cols]
                sg = _sigmoid(zb_ref[rows, cols])
                glu = za * sg
                acc = jnp.zeros((32, CB), F32)
                for k in range(CONV_K):
                    dcs = _shifted(dsh_scr, dc_scr, cols, s * 32 + CONV_K - 1 - k, 32)
                    prod = glu * dcs
                    dw_acc[k, :, cols] += (prod[0:8] + prod[8:16]) + (prod[16:24] + prod[24:32])
                    acc = acc + dcs * w_ref[k:k + 1, cols]
                dz_ref[rows, cols] = (acc * sg).astype(BF16)
                dz_ref[rows, cols_b] = ((acc * za) * (sg * (1.0 - sg))).astype(BF16)
            return c
        lax.fori_loop(0, D // CB, colblock, 0)

        @pl.when(i == nt - 1)
        def _():
            for k in range(CONV_K):
                dw_ref[k:k + 1, :] = _colsum(dw_acc[k])

    grid = (nt,)
    body, c_in, c_out, c_shapes, c_scr, _ = _host(comm, 6, 2, 3, grid, body)
    outs = pl.pallas_call(
        body, name="conv_bwd", grid=grid,
        in_specs=[_tok(ts, D), pl.BlockSpec((HALO, D), lambda i: (jnp.minimum((i + 1) * hb, nh - 1), 0)),
                  _zcol(ts, D, 3), _zcol(ts, D, 4),
                  pl.BlockSpec((None, CONV_K, D), lambda i: (l, 0, 0)), ANY] + c_in,
        out_specs=[_tok(ts, 2 * D), pl.BlockSpec((CONV_K, D), lambda i: (0, 0))] + c_out,
        out_shape=[jax.ShapeDtypeStruct((S, 3 * D), BF16), jax.ShapeDtypeStruct((CONV_K, D), F32)] + c_shapes,
        scratch_shapes=[pltpu.VMEM((ts + HALO, D), F32), pltpu.VMEM((CONV_K, 8, D), F32),
                        pltpu.VMEM((7, ts + HALO, CB), F32)] + c_scr,
        input_output_aliases={5: 0},
        compiler_params=_params("arbitrary"),
    )(dc0, dc0, z, z, w_all, dz_b, *(comm.ins if comm else []))
    return outs[0], outs[1], outs[2:]


def _branch_c_bwd(z, dbr, kv, kvt):
    S = z.shape[0]
    M, D2 = kv.shape
    D = D2 // 2
    HD = D // HEADS
    scale = 1.0 / math.sqrt(HD)
    ts = _tile(S, 256)

    def body(q_ref, zg_ref, d_ref, kt_ref, vt_ref, k_ref, v_ref, dz_ref, dk_ref, dv_ref):
        @pl.when(pl.program_id(0) == 0)
        def _():
            dk_ref[...] = jnp.zeros_like(dk_ref)
            dv_ref[...] = jnp.zeros_like(dv_ref)

        for h in range(HEADS):
            cols = slice(h * HD, (h + 1) * HD)
            qb = q_ref[:, cols].astype(BF16)
            p = _softmax_rows(_dot(qb, kt_ref[cols, :]) * scale)
            pb = p.astype(BF16)
            att = _dot(pb, v_ref[:, cols])
            sg, dsg = _silu_and_grad(zg_ref[:, cols])
            d = d_ref[:, cols]
            dz_ref[:, D + h * HD:D + (h + 1) * HD] = ((d * att) * dsg).astype(BF16)
            datt = (d * sg).astype(BF16)
            dp = _dot(datt, vt_ref[cols, :])
            dv_ref[:, cols] += _dot_ta(pb, datt)
            ds = ((p * (dp - jnp.sum(dp * p, axis=-1, keepdims=True))) * scale).astype(BF16)
            dz_ref[:, cols] = _dot(ds, k_ref[:, cols]).astype(BF16)
            dk_ref[:, cols] += _dot_ta(ds, qb)

    return pl.pallas_call(
        body, name="branch_c_bwd", grid=(S // ts,),
        in_specs=[_zcol(ts, D, 6), _zcol(ts, D, 7), _plane(ts, D, 1),
                  pl.BlockSpec((D, M), lambda i: (0, 0)), pl.BlockSpec((D, M), lambda i: (1, 0)),
                  pl.BlockSpec((M, D), lambda i: (0, 0)), pl.BlockSpec((M, D), lambda i: (0, 1))],
        out_specs=[_tok(ts, 2 * D), pl.BlockSpec((M, D), lambda i: (0, 0)), pl.BlockSpec((M, D), lambda i: (0, 0))],
        out_shape=[jax.ShapeDtypeStruct((S, 2 * D), BF16), jax.ShapeDtypeStruct((M, D), F32),
                   jax.ShapeDtypeStruct((M, D), F32)],
        compiler_params=_params("arbitrary"),
    )(z, z, dbr, kvt, kvt, kv, kv)


def _kv_bwd(dk, dv, mn, mem, wkv_all, l, prev):
    M, D = mem.shape

    def body(*refs):
        dk_ref, dv_ref, mn_ref, mem_ref, w_ref = refs[:5]
        dw_ref, dg_ref = refs[-2:]
        dkb = dk_ref[...].astype(BF16)
        dvb = dv_ref[...].astype(BF16)
        dw_ref[:, 0:D] = _dot_ta(mn_ref[...], dkb)
        dw_ref[:, D:2 * D] = _dot_ta(mn_ref[...], dvb)
        dmn = _dot_tb(dkb, w_ref[:, 0:D]) + _dot_tb(dvb, w_ref[:, D:2 * D])
        m = mem_ref[...]
        dg_ref[...] = _colsum(dmn * (m * lax.rsqrt(_mean(m * m) + RMS_EPS)))

    full = lambda shape: pl.BlockSpec(shape, lambda i: (0, 0))
    in_specs = [full((M, D)), full((M, D)), full((M, D)), full((M, D)),
                pl.BlockSpec((None, D, 2 * D), lambda i: (l, 0, 0))]
    args = [dk, dv, mn, mem, wkv_all]
    aliases = {}
    if prev is not None:
        in_specs.append(ANY)
        args.append(prev)
        aliases = {5: 0}
    return pl.pallas_call(
        body, name="kv_bwd", grid=(1,),
        in_specs=in_specs,
        out_specs=[pl.BlockSpec((None, D, 2 * D), lambda i: (l, 0, 0)), _row1(D)],
        out_shape=[jax.ShapeDtypeStruct((wkv_all.shape[0], D, 2 * D), F32), jax.ShapeDtypeStruct((1, D), F32)],
        input_output_aliases=aliases,
        compiler_params=_params("arbitrary"),
    )(*args)


def _in_proj_bwd(pieces, w_all, x, g, dxp, l, comm=None, part=(0, 1), prev=None):
    S, D = x.shape
    tm = _tile(S, 1024)
    n_p = len(pieces)
    n_i = (S // tm) // part[1]
    i0 = part[0] * n_i
    n_in = n_p + 4 + (prev is not None)

    def body(*refs):
        dz_refs = refs[:n_p]
        w_ref, x_hbm, g_ref, dxp_hbm = refs[n_p:n_p + 4]
        dx_ref, dg_ref, acc, x_ref, dxp_ref, sem = refs[n_in:]
        i = pl.program_id(0)
        k = pl.program_id(1)
        rows = pl.ds(pl.multiple_of((i + i0) * tm, tm), tm)
        fetch = [pltpu.make_async_copy(x_hbm.at[rows, :], x_ref, sem.at[0]),
                 pltpu.make_async_copy(dxp_hbm.at[rows, :], dxp_ref, sem.at[1])]

        @pl.when(k == N_PLANES - 2)
        def _():
            for cp in fetch:
                cp.start()

        @pl.when(k == 0)
        def _():
            acc[...] = jnp.zeros_like(acc)

        @pl.when((i == 0) & (k == 0))
        def _():
            dg_ref[...] = jnp.zeros_like(dg_ref)

        for (_, lo, n), r in zip(pieces, dz_refs):
            @pl.when((k >= lo) & (k < lo + n))
            def _(r=r):
                acc[...] += _dot_tb(r[...], w_ref[...])

        @pl.when(k == N_PLANES - 1)
        def _():
            for cp in fetch:
                cp.wait()

            def slab(si, c):
                rr = _rows(si, 16)
                xs = x_ref[rr, :]
                rs = lax.rsqrt(_mean(xs * xs) + RMS_EPS)
                n = xs * rs
                dh = acc[rr, :]
                dn = dh * g_ref[...]
                dx_ref[rr, :] = rs * (dn - n * _mean(dn * n)) + dxp_ref[rr, :]
                return c + _colsum(dh * n)
            dg_ref[...] += _grouped_loop(tm // 16, 4, slab, jnp.zeros((1, D), F32))

    def piece_spec(lo, n):
        return pl.BlockSpec((tm, D), lambda i, k: (i + i0, jnp.clip(k - lo, 0, n - 1)))

    tokk = lambda: pl.BlockSpec((tm, D), lambda i, k: (i + i0, 0))
    rowk = lambda: pl.BlockSpec((1, D), lambda i, k: (0, 0))
    grid = (n_i, N_PLANES)
    body, c_in, c_out, c_shapes, c_scr, _ = _host(comm, n_in, 2, 4, grid, body)
    more_in, more_args, aliases = ([ANY], [prev], {n_in - 1: 0}) if prev is not None else ([], [], {})
    outs = pl.pallas_call(
        body, name="in_proj_bwd", grid=grid,
        in_specs=[piece_spec(lo, n) for _, lo, n in pieces]
        + [pl.BlockSpec((None, D, D), lambda i, k: (l, 0, k)), ANY, rowk(), ANY] + more_in + c_in,
        out_specs=[tokk(), rowk()] + c_out,
        out_shape=[jax.ShapeDtypeStruct((S, D), F32), jax.ShapeDtypeStruct((1, D), F32)] + c_shapes,
        scratch_shapes=[pltpu.VMEM((tm, D), F32), pltpu.VMEM((tm, D), F32), pltpu.VMEM((tm, D), F32),
                        pltpu.SemaphoreType.DMA((2,))] + c_scr,
        input_output_aliases=aliases,
        compiler_params=_params("arbitrary", "arbitrary"),
    )(*[p for p, _, _ in pieces], w_all, x, g, dxp, *more_args, *(comm.ins if comm else []))
    return outs[0], outs[1], outs[2:]


def _wgrad(a, b, out_shape, plane, col_off, prev, a_plane=None, b_plane=None):
    S = a.shape[-2]
    K1 = a.shape[-1]
    nb_cols = b.shape[-1]
    ts = _tile(S, 2048)
    tn = _tile(nb_cols, 1024)
    col_off = col_off // tn

    def body(*refs):
        a_ref, b_ref = refs[:2]
        o_ref = refs[-1]
        k = pl.program_id(1)
        prod = _dot_ta(a_ref[...].astype(BF16), b_ref[...].astype(BF16))

        @pl.when(k == 0)
        def _():
            o_ref[...] = prod

        @pl.when(k > 0)
        def _():
            o_ref[...] += prod

    if a_plane is None:
        a_spec = pl.BlockSpec((ts, K1), lambda j, k: (k, 0))
    else:
        a_spec = pl.BlockSpec((None, ts, K1), lambda j, k: (a_plane, k, 0))
    if b_plane is None:
        b_spec = pl.BlockSpec((ts, tn), lambda j, k: (k, j))
    else:
        b_spec = pl.BlockSpec((None, ts, tn), lambda j, k: (b_plane, k, j))
    in_specs, args, aliases = [a_spec, b_spec], [a, b], {}
    if prev is not None:
        in_specs.append(ANY)
        args.append(prev)
        aliases = {2: 0}
    return pl.pallas_call(
        body, name="wgrad", grid=(nb_cols // tn, S // ts),
        in_specs=in_specs,
        out_specs=pl.BlockSpec((None, K1, tn), lambda j, k: (plane, 0, col_off + j)),
        out_shape=jax.ShapeDtypeStruct(out_shape, F32),
        input_output_aliases=aliases,
        compiler_params=_params("parallel", "arbitrary"),
    )(*args)


def _position():
    x, y, c = lax.axis_index("x"), lax.axis_index("y"), lax.axis_index("c")
    chips = [(1 - x, y), (x, 1 - y), (1 - x, 1 - y)]
    return x, y, c, chips


def _sub(ref, axis, start, size):
    idx = [slice(None)] * len(ref.shape)
    idx[axis] = pl.ds(start, size)
    return ref.at[tuple(idx)]


def _dma(src, dst, ssem, rsem, to):
    return pltpu.make_async_remote_copy(src_ref=src, dst_ref=dst, send_sem=ssem, recv_sem=rsem,
                                        device_id=to, device_id_type=MESH)


def _dma_sems(*counts):
    return [pltpu.SemaphoreType.DMA((n,)) for n in counts]


def _both(a, b):
    na, oa, sa = len(a.ins), len(a.out_shapes), len(a.sem_shapes)
    phases = [(w, lambda i, o, s, fn=fn: fn(i[:na], o[:oa], s[:sa])) for w, fn in a.phases]
    phases += [(w, lambda i, o, s, fn=fn: fn(i[na:], o[oa:], s[sa:])) for w, fn in b.phases]
    return _Hosted(a.ins + b.ins, list(a.out_shapes) + list(b.out_shapes), list(a.sem_shapes) + list(b.sem_shapes), phases)


def _run_alone(comm, name):
    ci, co = len(comm.ins), len(comm.out_shapes)

    def body(*refs):
        for _, fn in comm.phases:
            fn(refs[:ci], refs[ci:ci + co], refs[ci + co:])

    return pl.pallas_call(
        body, name=name, in_specs=[ANY] * ci, out_specs=[ANY] * co, out_shape=list(comm.out_shapes),
        scratch_shapes=list(comm.sem_shapes),
    )(*comm.ins)


def _gather_comm(shards, layers, axes, start_at="first"):
    n = len(shards)
    out_shapes = []
    for s, (ba, _) in zip(shards, axes):
        shp = list(s.shape[1:])
        shp[ba] *= N_CHIPS
        out_shapes.append(jax.ShapeDtypeStruct(tuple(shp), s.dtype))

    def half(ref, t, h):
        ha = axes[t][1]
        if ha is None:
            return ref
        hs = shards[t].shape[1 + ha] // 2
        return _sub(ref, ha, h * hs, hs)

    def block(outs, t, blk):
        ba = axes[t][0]
        bs = shards[t].shape[1 + ba]
        return _sub(outs[t], ba, blk * bs, bs)

    def start(ins, outs, sems):
        s_own, r_own, s_ici, r_ici, _, _ = sems
        x, y, c, chips = _position()
        j = 2 * x + y
        for t in range(n):
            _dma(ins[t].at[layers[t]], block(outs, t, j), s_own.at[t], r_own.at[t], (x, y, 1 - c)).start()
            for k, (px, py) in enumerate(chips):
                _dma(half(ins[t].at[layers[t]], t, c), half(block(outs, t, j), t, c),
                     s_ici.at[3 * t + k], r_ici.at[3 * t + k], (px, py, c)).start()

    def forward(ins, outs, sems):
        _, _, s_ici, r_ici, s_fwd, r_fwd = sems
        x, y, c, chips = _position()
        for t in range(n):
            if axes[t][1] is None:
                continue
            for k, (px, py) in enumerate(chips):
                got = half(block(outs, t, 2 * px + py), t, c)
                _dma(got, got, s_ici.at[3 * t + k], r_ici.at[3 * t + k], (px, py, c)).wait_recv()
                _dma(got, got, s_fwd.at[3 * t + k], r_fwd.at[3 * t + k], (x, y, 1 - c)).start()

    def finish(ins, outs, sems):
        s_own, r_own, s_ici, r_ici, s_fwd, r_fwd = sems
        x, y, c, chips = _position()
        j = 2 * x + y
        sib = (x, y, 1 - c)
        for t in range(n):
            for k, (px, py) in enumerate(chips):
                i = 3 * t + k
                sent = half(block(outs, t, j), t, c)
                _dma(sent, sent, s_ici.at[i], r_ici.at[i], (px, py, c)).wait_send()
                theirs = block(outs, t, 2 * px + py)
                if axes[t][1] is None:
                    _dma(theirs, theirs, s_ici.at[i], r_ici.at[i], (px, py, c)).wait_recv()
                else:
                    got, other = half(theirs, t, c), half(theirs, t, 1 - c)
                    _dma(got, got, s_fwd.at[i], r_fwd.at[i], sib).wait_send()
                    _dma(other, other, s_fwd.at[i], r_fwd.at[i], sib).wait_recv()
            _dma(ins[t].at[layers[t]], block(outs, t, j), s_own.at[t], r_own.at[t], sib).wait()

    return _Hosted(list(shards), out_shapes, _dma_sems(n, n, 3 * n, 3 * n, 3 * n, 3 * n),
                   [(start_at, start), ("late", forward), ("last", finish)])


def _halved(shape, axis):
    shp = list(shape)
    shp[axis] //= 2
    return tuple(shp)


def _pair_comm(grads, axes):
    n = len(grads)

    def copies(ins, outs, sems):
        x, y, c, _ = _position()
        cps = []
        for t in range(n):
            ha = axes[t][1]
            hs = grads[t].shape[ha] // 2
            cps.append(_dma(_sub(ins[t], ha, (1 - c) * hs, hs), outs[t], sems[0].at[t], sems[1].at[t], (x, y, 1 - c)))
        return cps

    def start(ins, outs, sems):
        for cp in copies(ins, outs, sems):
            cp.start()

    def finish(ins, outs, sems):
        for cp in copies(ins, outs, sems):
            cp.wait()

    return _Hosted(list(grads), [jax.ShapeDtypeStruct(_halved(g.shape, ax[1]), g.dtype) for g, ax in zip(grads, axes)],
                   _dma_sems(n, n), [("first", start), ("last", finish)])


def _block_comm(sums, axes):
    n = len(sums)
    out_shapes = []
    for s, (ba, _) in zip(sums, axes):
        shp = list(s.shape)
        shp[ba] //= N_CHIPS
        out_shapes.append(jax.ShapeDtypeStruct((3,) + tuple(shp), s.dtype))

    def copies(ins, outs, sems):
        x, y, c, chips = _position()
        cps = []
        for t in range(n):
            ba = axes[t][0]
            bs = sums[t].shape[ba] // N_CHIPS
            for k, (px, py) in enumerate(chips):
                cps.append(_dma(_sub(ins[t], ba, (2 * px + py) * bs, bs), outs[t].at[k],
                                sems[0].at[3 * t + k], sems[1].at[3 * t + k], (px, py, c)))
        return cps

    def start(ins, outs, sems):
        for cp in copies(ins, outs, sems):
            cp.start()

    def finish(ins, outs, sems):
        for cp in copies(ins, outs, sems):
            cp.wait()

    return _Hosted(list(sums), out_shapes, _dma_sems(3 * n, 3 * n), [("first", start), ("last", finish)])


def _sibling_share(shards, axes):
    n = len(shards)

    def body(*refs):
        outs = refs[n:2 * n]
        ssem, rsem = refs[2 * n:]
        x, y, c, _ = _position()
        cps = []
        for t in range(n):
            ha = axes[t][1]
            hs = shards[t].shape[ha] // 2
            mine = _sub(outs[t], ha, c * hs, hs)
            cps.append(_dma(mine, mine, ssem.at[t], rsem.at[t], (x, y, 1 - c)))
        for cp in cps:
            cp.start()
        for cp in cps:
            cp.wait()

    return pl.pallas_call(
        body, name="sibling_share",
        in_specs=[ANY] * n, out_specs=[ANY] * n,
        out_shape=[jax.ShapeDtypeStruct(s.shape, s.dtype) for s in shards],
        input_output_aliases={t: t for t in range(n)},
        scratch_shapes=_dma_sems(n, n),
    )(*shards)


def _small_allreduce(p):
    R, C = p.shape

    def body(p_ref, o_ref, sib_buf, chip_sums, ssem, rsem):
        x, y, c, chips = _position()
        j = 2 * x + y
        sib = pltpu.make_async_remote_copy(
            src_ref=p_ref, dst_ref=sib_buf, send_sem=ssem.at[0], recv_sem=rsem.at[0],
            device_id=(x, y, 1 - c), device_id_type=MESH)
        sib.start()
        sib.wait()
        chip_sums[j] = p_ref[...] + sib_buf[...]
        cps = [pltpu.make_async_remote_copy(
            src_ref=chip_sums.at[j], dst_ref=chip_sums.at[j], send_sem=ssem.at[1 + k], recv_sem=rsem.at[1 + k],
            device_id=(px, py, c), device_id_type=MESH) for k, (px, py) in enumerate(chips)]
        for cp in cps:
            cp.start()
        for k, (px, py) in enumerate(chips):
            pltpu.make_async_remote_copy(
                src_ref=chip_sums.at[j], dst_ref=chip_sums.at[2 * px + py],
                send_sem=ssem.at[1 + k], recv_sem=rsem.at[1 + k],
                device_id=(px, py, c), device_id_type=MESH).wait()
        o_ref[...] = ((chip_sums[0] + chip_sums[1]) + chip_sums[2]) + chip_sums[3]

    vm = pl.BlockSpec(memory_space=pltpu.VMEM)
    return pl.pallas_call(
        body, name="small_allreduce",
        in_specs=[vm], out_specs=vm, out_shape=jax.ShapeDtypeStruct((R, C), F32),
        scratch_shapes=[pltpu.VMEM((R, C), F32), pltpu.VMEM((N_CHIPS, R, C), F32),
                        pltpu.SemaphoreType.DMA((4,)), pltpu.SemaphoreType.DMA((4,))],
        compiler_params=pltpu.CompilerParams(vmem_limit_bytes=VMEM_LIMIT_BYTES),
    )(p)


def _add_half(g, recv, half_axis, pos):
    B, R, C = recv.shape
    tr = _tile(R, 256)
    tc = _tile(C, 2816)
    nr, nc = R // tr, C // tc
    if half_axis == 1:
        g_map = lambda b, i, j, pos_ref: (b, pos_ref[0] * nr + i, j)
    else:
        g_map = lambda b, i, j, pos_ref: (b, i, pos_ref[0] * nc + j)

    def body(pos_ref, g_ref, r_ref, o_ref, ob_ref):
        del pos_ref
        s = g_ref[...] + r_ref[...]
        o_ref[...] = s
        ob_ref[...] = s.astype(BF16)

    spec = lambda: pl.BlockSpec((None, tr, tc), lambda b, i, j, pos_ref: (b, i, j))
    return pl.pallas_call(
        body, name="add_half",
        grid_spec=pltpu.PrefetchScalarGridSpec(
            num_scalar_prefetch=1, grid=(B, nr, nc),
            in_specs=[pl.BlockSpec((None, tr, tc), g_map), spec()],
            out_specs=[spec(), spec()]),
        out_shape=[jax.ShapeDtypeStruct((B, R, C), F32), jax.ShapeDtypeStruct((B, R, C), BF16)],
        compiler_params=_params("parallel", "parallel", "parallel"),
    )(pos, g, recv)


def _sum_half(own, recv, axes, pos):
    _, B, rs, cs = recv.shape
    ba, ha = axes
    tr = _tile(rs, 128)
    nr = rs // tr

    def body(pos_ref, o_ref, r0_ref, r1_ref, r2_ref, out_ref):
        del pos_ref
        out_ref[...] = (((o_ref[...] + r0_ref[...].astype(F32)) + r1_ref[...].astype(F32))
                        + r2_ref[...].astype(F32))

    if ba == 2:
        own_spec = pl.BlockSpec((None, tr, cs), lambda b, i, pos_ref: (b, i, pos_ref[1]))
    else:
        own_spec = pl.BlockSpec((None, tr, cs), lambda b, i, pos_ref: (b, pos_ref[1] * nr + i, 0))
    if ha == 1:
        out_shape, out_map = (B, 2 * rs, cs), (lambda b, i, pos_ref: (b, pos_ref[0] * nr + i, 0))
    else:
        out_shape, out_map = (B, rs, 2 * cs), (lambda b, i, pos_ref: (b, i, pos_ref[0]))
    rspec = lambda k: pl.BlockSpec((None, None, tr, cs), lambda b, i, pos_ref: (k, b, i, 0))
    return pl.pallas_call(
        body, name="sum_half",
        grid_spec=pltpu.PrefetchScalarGridSpec(
            num_scalar_prefetch=1, grid=(B, nr),
            in_specs=[own_spec, rspec(0), rspec(1), rspec(2)],
            out_specs=pl.BlockSpec((None, tr, cs), out_map)),
        out_shape=jax.ShapeDtypeStruct(out_shape, F32),
        compiler_params=_params("parallel", "parallel"),
    )(pos, own, recv, recv, recv)


def _adamw(w, g, m, v):
    shape = w.shape
    C = shape[-1]
    R = w.size // C
    tr = R
    for cand in (512, 256, 128, 64, 32, 16, 8):
        if R % cand == 0 and cand * C * 4 <= (1 << 20):
            tr = cand
            break
    bc1 = 1.0 - ADAM_B1 ** ADAM_STEP
    bc2 = 1.0 - ADAM_B2 ** ADAM_STEP

    def body(w_ref, g_ref, m_ref, v_ref, d_ref, mo_ref, vo_ref):
        gg = g_ref[...]
        mn = ADAM_B1 * m_ref[...] + (1.0 - ADAM_B1) * gg
        vn = ADAM_B2 * v_ref[...] + (1.0 - ADAM_B2) * (gg * gg)
        mo_ref[...] = mn
        vo_ref[...] = vn
        d_ref[...] = -ADAM_LR * ((mn / bc1) / (jnp.sqrt(vn / bc2) + ADAM_EPS) + ADAM_WD * w_ref[...])

    spec = lambda: pl.BlockSpec((tr, C), lambda i: (i, 0))
    outs = pl.pallas_call(
        body, name="adamw", grid=(R // tr,),
        in_specs=[spec()] * 4, out_specs=[spec()] * 3,
        out_shape=[jax.ShapeDtypeStruct((R, C), F32)] * 3,
        compiler_params=_params("parallel"),
    )(*[a.reshape(R, C) for a in (w, g, m, v)])
    return tuple(o.reshape(shape) for o in outs)


def kernel(x, mem, norm_g, mem_norm_g, w_in, gmlp_ln_g, gmlp_ln_b, w_s, b_s, conv_w, conv_b, conv_ln_g, conv_ln_b, w_kv, w_branch, w_out, final_norm_g, loss_target, m_norm_g, m_mem_norm_g, m_w_in, m_gmlp_ln_g, m_gmlp_ln_b, m_w_s, m_b_s, m_conv_w, m_conv_b, m_conv_ln_g, m_conv_ln_b, m_w_kv, m_w_branch, m_w_out, m_final_norm_g, v_norm_g, v_mem_norm_g, v_w_in, v_gmlp_ln_g, v_gmlp_ln_b, v_w_s, v_b_s, v_conv_w, v_conv_b, v_conv_ln_g, v_conv_ln_b, v_w_kv, v_w_branch, v_w_out, v_final_norm_g):
    xs, mems, tgt = x[0], mem[0], loss_target[0]
    S, D = xs.shape
    assert D // GROUPS == LANES and S % CHUNK == 0 and w_s.shape[-1] == CHUNK
    L = N_LAYERS

    assert L == 2
    shards = [w_in.astype(BF16)[:, None], w_kv.astype(BF16)[:, None], w_branch.astype(BF16), w_out.astype(BF16)[:, None],
              conv_w[:, None]]
    gather_axes = [(2, 1), (2, 1), (1, 2), (1, 2), (2, None)]
    reduce_axes = gather_axes[:4]
    win0 = _run_alone(_gather_comm(shards[:1], [0], gather_axes[:1]), "gather_w_in0")[0]
    gather_early = _gather_comm(shards[1:] + shards[:1], [0] * 4 + [1], gather_axes[1:] + gather_axes[:1])
    gather_late = _gather_comm(shards[1:], [1] * 4, gather_axes[1:])

    tri = jnp.tril(jnp.ones((CHUNK, CHUNK), F32))
    t_mix = (w_s * tri).astype(BF16)
    t_mix_t = jnp.swapaxes(t_mix, -1, -2)
    bias_full = jnp.repeat(jnp.swapaxes(b_s, -1, -2), D // GROUPS, axis=-1)
    row = lambda a, l: a[l][None, :]
    xi, yi, ci = lax.axis_index("x"), lax.axis_index("y"), lax.axis_index("c")
    pos = jnp.stack([ci, 2 * xi + yi]).astype(jnp.int32)

    saved = []
    h = xs
    weights = [None] * L
    for l in range(L):
        if l == 0:
            z, hb, fetched = _in_proj(h, row(norm_g, l), win0, 0, gather_early)
            weights[0] = [win0] + list(fetched[:4])
        else:
            z, hb, _ = _in_proj(h, row(norm_g, l), weights[l][0], 0)
        win, wkv, wb, wo, cw = weights[l]
        bra = _branch_a_fwd(z, row(gmlp_ln_g, l), row(gmlp_ln_b, l), t_mix[l], bias_full[l])
        c0, late = _conv_fwd(z, cw, row(conv_b, l), 0, gather_late if l == 0 else None)
        if l == 0:
            weights[1] = [fetched[4]] + list(late)
        brb = _branch_b_fwd(c0, z, row(conv_ln_g, l), row(conv_ln_b, l))
        mn, kv, kvt = _kv_fwd(mems, row(mem_norm_g, l), wkv, 0)
        brc = _branch_c_fwd(z, kv, kvt)
        merged, h_next = _merge_fwd(bra, brb, brc, z, h, wb[None], wo, 0)
        saved.append((h, z, hb, bra, c0, brb, mn, kv, kvt, brc, merged))
        h = h_next

    dx, loss_cols, d_final_g = _loss_bwd(h, tgt, final_norm_g[None, :])
    loss = lax.psum(0.5 * jnp.sum(loss_cols) / D, ("x", "y", "c"))

    def add_halves(grads, recv, axes):
        return [_add_half(g, r, ax[1], pos) for g, r, ax in zip(grads, recv, axes)]

    def finish_reduction(sums, blocks, axes):
        halves = [_sum_half(s, b, ax, pos) for (s, _), b, ax in zip(sums, blocks, axes)]
        return list(_sibling_share(halves, axes))

    small = {k: [None] * L for k in ("norm_g", "mem_norm_g", "gmlp_ln_g", "gmlp_ln_b", "w_s", "b_s", "conv_w",
                                     "conv_b", "conv_ln_g", "conv_ln_b")}
    early_axes = reduce_axes[1:]
    pending, reduced, early = None, [None] * L, None
    for l in reversed(range(L)):
        h_in, z, hb, bra, c0, brb, mn, kv, kvt, brc, merged = saved[l]
        win, wkv, wb, wo, cw = weights[l]
        dz_m, dproj, dbr, dc0, dz_b, dcg, dcbeta, dcb, recv = _merge_bwd(
            dx, bra, brb, brc, z, c0, row(conv_ln_g, l), row(conv_ln_b, l), wb[None], wo, 0,
            _pair_comm(pending, reduce_axes) if pending else None)
        sums = add_halves(pending, recv, reduce_axes) if pending else []
        g_o = _wgrad(merged, dx, (1, D, D), 0, 0, None)
        g_b = None
        for n, br in enumerate((bra, brb, brc)):
            g_b = _wgrad(br, dproj, (N_BRANCHES, D, D), n, 0, g_b, b_plane=n)
        dz_a, dws, dbs, dlg, dlb = _branch_a_bwd(z, dbr, row(gmlp_ln_g, l), row(gmlp_ln_b, l),
                                                t_mix[l], t_mix_t[l], bias_full[l], tri)
        dz_c, dk, dv = _branch_c_bwd(z, dbr, kv, kvt)
        g_kv, dmg = _kv_bwd(dk, dv, mn, mems, wkv, 0, None)
        ready = [g_kv, g_b, g_o]
        riding = _both(_block_comm([sb for _, sb in sums], reduce_axes), _pair_comm(ready, early_axes)) if pending else None
        dz_b, dcw, got = _conv_bwd(dc0, z, cw, dz_b, 0, riding)
        sums_early = add_halves(ready, got[4:], early_axes) if pending else []
        if pending:
            reduced[l + 1] = finish_reduction(sums, got[:4], reduce_axes)
        pieces =[(dz_a, 0, 3), (dz_b, 3, 3), (dz_c, 6, 2), (dz_m, 8, 3)]
        g_in = None
        for dzp, lo, _ in pieces:
            g_in = _wgrad(hb, dzp, (1, D, N_PLANES * D), 0, lo * D, g_in)
        if pending:
            riding = _block_comm([sb for _, sb in sums_early], early_axes)
            dx_half, dng_a, got = _in_proj_bwd(pieces, win, h_in, row(norm_g, l), dx, 0,
                                               _both(riding, _pair_comm([g_in], reduce_axes[:1])), (0, 2))
            early = finish_reduction(sums_early, got[:3], early_axes)
            sums_in = add_halves([g_in], got[3:], reduce_axes[:1])
            dx, dng_b, blocks_in = _in_proj_bwd(pieces, win, h_in, row(norm_g, l), dx, 0,
                                                _block_comm([sb for _, sb in sums_in], reduce_axes[:1]), (1, 2), dx_half)
            dng = dng_a + dng_b
            reduced[l] = finish_reduction(sums_in, blocks_in, reduce_axes[:1]) + early
        else:
            dx, dng, _ = _in_proj_bwd(pieces, win, h_in, row(norm_g, l), dx, 0)
        pending = [g_in, g_kv, g_b, g_o]
        for k, val in (("norm_g", dng), ("mem_norm_g", dmg), ("gmlp_ln_g", dlg), ("gmlp_ln_b", dlb), ("w_s", dws),
                       ("b_s", dbs), ("conv_w", dcw), ("conv_b", dcb), ("conv_ln_g", dcg), ("conv_ln_b", dcbeta)):
            small[k][l] = val
    grad_x = dx[None]

    grad_w_in, grad_w_kv, grad_w_branch, grad_w_out = [
        jnp.stack([reduced[0][t], reduced[1][t]]).reshape(w.shape) for t, w in enumerate((w_in, w_kv, w_branch, w_out))]

    order = [("norm_g", norm_g.shape), ("mem_norm_g", mem_norm_g.shape), ("gmlp_ln_g", gmlp_ln_g.shape),
             ("gmlp_ln_b", gmlp_ln_b.shape), ("w_s", w_s.shape), ("b_s", b_s.shape),
             ("conv_w", (L, CONV_K, D)), ("conv_b", conv_b.shape), ("conv_ln_g", conv_ln_g.shape),
             ("conv_ln_b", conv_ln_b.shape)]
    parts = [jnp.stack([v.reshape(shp[1:]) for v in small[k]]).reshape(-1, LANES) for k, shp in order]
    parts.append(d_final_g.reshape(-1, LANES))
    sizes = [p.shape[0] for p in parts]
    assert all(s % 8 == 0 for s in sizes)
    total = _small_allreduce(jnp.concatenate(parts, axis=0))
    red, off = {}, 0
    for (k, shp), n in zip(order + [("final_norm_g", final_norm_g.shape)], sizes):
        red[k] = total[off:off + n].reshape(shp)
        off += n
    cs = conv_w.shape[-1]
    red["conv_w"] = lax.dynamic_slice_in_dim(red["conv_w"], (2 * xi + yi) * cs, cs, axis=2)
    red.update(w_in=grad_w_in, w_kv=grad_w_kv, w_branch=grad_w_branch, w_out=grad_w_out)

    names = ["norm_g", "mem_norm_g", "w_in", "gmlp_ln_g", "gmlp_ln_b", "w_s", "b_s", "conv_w", "conv_b",
             "conv_ln_g", "conv_ln_b", "w_kv", "w_branch", "w_out", "final_norm_g"]
    weights = dict(norm_g=norm_g, mem_norm_g=mem_norm_g, w_in=w_in, gmlp_ln_g=gmlp_ln_g, gmlp_ln_b=gmlp_ln_b,
                   w_s=w_s, b_s=b_s, conv_w=conv_w, conv_b=conv_b, conv_ln_g=conv_ln_g, conv_ln_b=conv_ln_b,
                   w_kv=w_kv, w_branch=w_branch, w_out=w_out, final_norm_g=final_norm_g)
    ms = dict(norm_g=m_norm_g, mem_norm_g=m_mem_norm_g, w_in=m_w_in, gmlp_ln_g=m_gmlp_ln_g, gmlp_ln_b=m_gmlp_ln_b,
              w_s=m_w_s, b_s=m_b_s, conv_w=m_conv_w, conv_b=m_conv_b, conv_ln_g=m_conv_ln_g, conv_ln_b=m_conv_ln_b,
              w_kv=m_w_kv, w_branch=m_w_branch, w_out=m_w_out, final_norm_g=m_final_norm_g)
    vs = dict(norm_g=v_norm_g, mem_norm_g=v_mem_norm_g, w_in=v_w_in, gmlp_ln_g=v_gmlp_ln_g, gmlp_ln_b=v_gmlp_ln_b,
              w_s=v_w_s, b_s=v_b_s, conv_w=v_conv_w, conv_b=v_conv_b, conv_ln_g=v_conv_ln_g, conv_ln_b=v_conv_ln_b,
              w_kv=v_w_kv, w_branch=v_w_branch, w_out=v_w_out, final_norm_g=v_final_norm_g)
    deltas, new_m, new_v = [], [], []
    for k in names:
        w2 = weights[k] if weights[k].ndim > 1 else weights[k][None, :]
        d, mo, vo = _adamw(w2, red[k].reshape(w2.shape), ms[k].reshape(w2.shape), vs[k].reshape(w2.shape))
        deltas.append(d.reshape(weights[k].shape))
        new_m.append(mo.reshape(weights[k].shape))
        new_v.append(vo.reshape(weights[k].shape))
    grads = [red[k].reshape(weights[k].shape) for k in names]
    return (loss, grad_x, *grads, *deltas, *new_m, *new_v)
```

```python
import math

import jax
import jax.numpy as jnp
from jax import lax
from jax.experimental import pallas as pl
from jax.experimental.pallas import tpu as pltpu

F32 = jnp.float32
BF16 = jnp.bfloat16

N_LAYERS = 2
N_BRANCHES = 3
N_PLANES = 11
N_CHIPS = 4
CHUNK = 128
GROUPS = 8
HEADS = 4
CONV_K = 31
HALO = 32
LANES = 128
RMS_EPS = 1e-6
LN_EPS = 1e-5
ADAM_LR, ADAM_B1, ADAM_B2, ADAM_EPS, ADAM_WD, ADAM_STEP = 0.001, 0.9, 0.999, 1e-08, 0.01, 10
VMEM_LIMIT_BYTES = 48 * 1024 * 1024
MESH = pl.DeviceIdType.MESH
ANY = pl.BlockSpec(memory_space=pl.ANY)


def _params(*sem):
    return pltpu.CompilerParams(dimension_semantics=sem, vmem_limit_bytes=VMEM_LIMIT_BYTES)


def _tile(n, pref):
    t = min(n, pref)
    assert n % t == 0, (n, t)
    return t


def _dot(a, b):
    return jnp.dot(a, b, preferred_element_type=F32)


def _dot_tb(a, b):
    return lax.dot_general(a, b, (((1,), (1,)), ((), ())), preferred_element_type=F32)


def _dot_ta(a, b):
    return lax.dot_general(a, b, (((0,), (0,)), ((), ())), preferred_element_type=F32)


def _sigmoid(x):
    return 1.0 / (1.0 + jnp.exp(-x))


def _silu(x):
    return x * _sigmoid(x)


def _silu_and_grad(x):
    s = _sigmoid(x)
    return x * s, s * (1.0 + x * (1.0 - s))


_GELU_C = math.sqrt(2.0 / math.pi)
_GELU_A = 0.044715


def _gelu(x):
    return x * (0.5 * (1.0 + jnp.tanh(_GELU_C * (x + _GELU_A * (x * x * x)))))


def _gelu_and_grad(x):
    x2 = x * x
    t = jnp.tanh(_GELU_C * (x + _GELU_A * (x2 * x)))
    cdf = 0.5 * (1.0 + t)
    dcdf = 0.5 * (1.0 - t * t) * (_GELU_C * (1.0 + 3.0 * _GELU_A * x2))
    return x * cdf, cdf + x * dcdf


def _rows(i, n):
    return pl.ds(pl.multiple_of(i * n, n), n)


def _grouped_loop(n, group, body, init):
    assert n % group == 0

    def trip(i, c):
        for u in range(group):
            c = body(i * group + u, c)
        return c
    return lax.fori_loop(0, n // group, trip, init)


def _mean(x):
    return jnp.mean(x, axis=-1, keepdims=True)


def _colsum(x):
    return jnp.sum(x, axis=0, keepdims=True)


def _tok(ts, d):
    return pl.BlockSpec((ts, d), lambda i: (i, 0))


def _zcol(ts, d, n):
    return pl.BlockSpec((ts, d), lambda i: (i, n))


def _row1(d):
    return pl.BlockSpec((1, d), lambda i: (0, 0))


def _plane(ts, d, n):
    return pl.BlockSpec((None, ts, d), lambda i: (n, i, 0))


class _Hosted:
    def __init__(self, ins, out_shapes, sem_shapes, phases):
        self.ins, self.out_shapes, self.sem_shapes, self.phases = ins, out_shapes, sem_shapes, phases


def _host(comm, n_in, n_out, n_scr, grid, body, n_prefetch=0, wants_results=False):
    if comm is None:
        return body, [], [], [], [], None
    ci, co = len(comm.ins), len(comm.out_shapes)
    total = math.prod(grid)
    at = {"first": 0, "late": (7 * total) // 8, "last": total - 1}

    def wrapped(*refs):
        pre, refs = refs[:n_prefetch], refs[n_prefetch:]
        ins, cin = refs[:n_in], refs[n_in:n_in + ci]
        o0 = n_in + ci
        outs, cout = refs[o0:o0 + n_out], refs[o0 + n_out:o0 + n_out + co]
        s0 = o0 + n_out + co
        scr, csem = refs[s0:s0 + n_scr], refs[s0 + n_scr:]
        step = pl.program_id(0)
        for a in range(1, len(grid)):
            step = step * grid[a] + pl.program_id(a)

        def run(last):
            for w, fn in comm.phases:
                if (w == "last") == last:
                    @pl.when(step == at.get(w, w))
                    def _(fn=fn):
                        fn(cin, cout, csem)
        run(False)
        body(*pre, *ins, *outs, *scr, **({"cout": cout} if wants_results else {}))
        run(True)

    return wrapped, [ANY] * ci, [ANY] * co, list(comm.out_shapes), list(comm.sem_shapes), ("arbitrary",) * len(grid)


def _in_proj(x, g, w_all, l, comm=None):
    S, D = x.shape
    N = w_all.shape[2]
    tm, tn = _tile(S, 1024), _tile(N, 1408)
    grid = (S // tm, N // tn)

    def body(x_ref, g_ref, w_ref, z_ref, hb_ref, h_scr):
        @pl.when(pl.program_id(1) == 0)
        def _():
            def slab(i, c):
                r = _rows(i, 16)
                xs = x_ref[r, :]
                h = (xs * lax.rsqrt(_mean(xs * xs) + RMS_EPS)) * g_ref[...]
                hb = h.astype(BF16)
                h_scr[r, :] = hb
                hb_ref[r, :] = hb
                return c
            _grouped_loop(tm // 16, 4, slab, 0)
        z_ref[...] = _dot(h_scr[...], w_ref[...])

    body, c_in, c_out, c_shapes, c_scr, sem = _host(comm, 3, 2, 1, grid, body)
    outs = pl.pallas_call(
        body, name="in_proj", grid=grid,
        in_specs=[pl.BlockSpec((tm, D), lambda i, j: (i, 0)),
                  pl.BlockSpec((1, D), lambda i, j: (0, 0)),
                  pl.BlockSpec((None, D, tn), lambda i, j: (l, 0, j))] + c_in,
        out_specs=[pl.BlockSpec((tm, tn), lambda i, j: (i, j)),
                   pl.BlockSpec((tm, D), lambda i, j: (i, 0))] + c_out,
        out_shape=[jax.ShapeDtypeStruct((S, N), F32), jax.ShapeDtypeStruct((S, D), BF16)] + c_shapes,
        scratch_shapes=[pltpu.VMEM((tm, D), BF16)] + c_scr,
        compiler_params=_params(*(sem or ("parallel", "arbitrary"))),
    )(x, g, w_all, *(comm.ins if comm else []))
    return outs[0], outs[1], outs[2:]


def _branch_a_fwd(z, lng, lnb, t_mix, bias_full):
    S = z.shape[0]
    D = lng.shape[1]
    GD = D // GROUPS
    ts = _tile(S, 256)

    def body(zu_ref, zv_ref, zg_ref, lng_ref, lnb_ref, t_ref, bias_ref, a_ref, v_scr):
        def chunk(ci, carry):
            c0 = pl.multiple_of(ci * CHUNK, CHUNK)

            def slab(si, c):
                r = pl.multiple_of(si * 16, 16)
                vg = _gelu(zv_ref[pl.ds(pl.multiple_of(c0 + r, 16), 16), :])
                xc = vg - _mean(vg)
                y = xc * lax.rsqrt(_mean(xc * xc) + LN_EPS)
                v_scr[pl.ds(r, 16), :] = (y * lng_ref[...] + lnb_ref[...]).astype(BF16)
                return c
            _grouped_loop(CHUNK // 16, 4, slab, 0)
            for g in range(GROUPS):
                cols = slice(g * GD, (g + 1) * GD)
                sv = _dot(t_ref[g], v_scr[:, cols]) + bias_ref[:, cols]
                u = _gelu(zu_ref[pl.ds(c0, CHUNK), cols])
                a_ref[pl.ds(c0, CHUNK), cols] = ((u * sv) * _silu(zg_ref[pl.ds(c0, CHUNK), cols])).astype(BF16)
            return carry
        lax.fori_loop(0, ts // CHUNK, chunk, 0)

    return pl.pallas_call(
        body, name="branch_a_fwd", grid=(S // ts,),
        in_specs=[_zcol(ts, D, 0), _zcol(ts, D, 1), _zcol(ts, D, 2), _row1(D), _row1(D),
                  pl.BlockSpec((GROUPS, CHUNK, CHUNK), lambda i: (0, 0, 0)),
                  pl.BlockSpec((CHUNK, D), lambda i: (0, 0))],
        out_specs=_tok(ts, D),
        out_shape=jax.ShapeDtypeStruct((S, D), BF16),
        scratch_shapes=[pltpu.VMEM((CHUNK, D), BF16)],
        compiler_params=_params("parallel"),
    )(z, z, z, lng, lnb, t_mix, bias_full)


SH = HALO - CONV_K + 1


def _fill_shifts(sh_scr, src_scr, cols, ts):
    for j in range(1, 8):
        for g in range(ts // 32):
            sh_scr[j - 1, pl.ds(g * 32, 32), :] = src_scr[pl.ds(g * 32 + j, 32), cols]
        sh_scr[j - 1, pl.ds(ts, HALO - 8), :] = src_scr[pl.ds(ts + j, HALO - 8), cols]


def _shifted(sh_scr, src_scr, cols, row0, n):
    q, j = divmod(row0, 8)
    if j == 0:
        return src_scr[pl.ds(row0, n), cols]
    return sh_scr[j - 1, pl.ds(8 * q, n), :]


def _conv_fwd(z, w_all, cb, l, comm=None):
    S = z.shape[0]
    D = cb.shape[1]
    ts = _tile(S, 256)
    CB = 256
    hb = ts // HALO

    def body(za_ref, zb_ref, zap_ref, zbp_ref, w_ref, b_ref, c0_ref, glu_scr, sh_scr):
        i = pl.program_id(0)
        prev = zap_ref[...] * _sigmoid(zbp_ref[...])
        glu_scr[0:HALO, :] = jnp.where(i == 0, 0.0, prev)

        def fill(si, c):
            r = _rows(si, 32)
            glu_scr[pl.ds(pl.multiple_of(HALO + si * 32, 32), 32), :] = za_ref[r, :] * _sigmoid(zb_ref[r, :])
            return c
        _grouped_loop(ts // 32, 2, fill, 0)

        def colblock(ci, c):
            cols = _rows(ci, CB)
            _fill_shifts(sh_scr, glu_scr, cols, ts)
            for s in range(ts // 32):
                acc = jnp.zeros((32, CB), F32)
                for k in range(CONV_K):
                    acc = acc + _shifted(sh_scr, glu_scr, cols, s * 32 + k + SH, 32) * w_ref[k:k + 1, cols]
                c0_ref[pl.ds(s * 32, 32), cols] = acc + b_ref[:, cols]
            return c
        lax.fori_loop(0, D // CB, colblock, 0)

    prev_map = lambda n: (lambda i: (jnp.maximum(i * hb - 1, 0), n))
    grid = (S // ts,)
    body, c_in, c_out, c_shapes, c_scr, sem = _host(comm, 6, 1, 2, grid, body)
    outs = pl.pallas_call(
        body, name="conv_fwd", grid=grid,
        in_specs=[_zcol(ts, D, 3), _zcol(ts, D, 4),
                  pl.BlockSpec((HALO, D), prev_map(3)), pl.BlockSpec((HALO, D), prev_map(4)),
                  pl.BlockSpec((None, CONV_K, D), lambda i: (l, 0, 0)), _row1(D)] + c_in,
        out_specs=[_tok(ts, D)] + c_out,
        out_shape=[jax.ShapeDtypeStruct((S, D), F32)] + c_shapes,
        scratch_shapes=[pltpu.VMEM((ts + HALO, D), F32), pltpu.VMEM((7, ts + HALO, CB), F32)] + c_scr,
        compiler_params=_params(*(sem or ("parallel",))),
    )(z, z, z, z, w_all, cb, *(comm.ins if comm else []))
    return outs[0], outs[1:]


def _branch_b_fwd(c0, z, g, b):
    S, D = c0.shape
    ts = _tile(S, 512)

    def body(c0_ref, zg_ref, g_ref, b_ref, o_ref):
        def slab(si, c):
            r = _rows(si, 16)
            v = c0_ref[r, :]
            xc = v - _mean(v)
            c1 = (xc * lax.rsqrt(_mean(xc * xc) + LN_EPS)) * g_ref[...] + b_ref[...]
            o_ref[r, :] = (_silu(c1) * _silu(zg_ref[r, :])).astype(BF16)
            return c
        _grouped_loop(ts // 16, 4, slab, 0)

    return pl.pallas_call(
        body, name="branch_b_fwd", grid=(S // ts,),
        in_specs=[_tok(ts, D), _zcol(ts, D, 5), _row1(D), _row1(D)],
        out_specs=_tok(ts, D),
        out_shape=jax.ShapeDtypeStruct((S, D), BF16),
        compiler_params=_params("parallel"),
    )(c0, z, g, b)


def _kv_fwd(mem, mg, wkv_all, l):
    M, D = mem.shape

    def body(mem_ref, g_ref, w_ref, mn_ref, kv_ref, kvt_ref):
        m = mem_ref[...]
        mn = ((m * lax.rsqrt(_mean(m * m) + RMS_EPS)) * g_ref[...]).astype(BF16)
        mn_ref[...] = mn
        kv = _dot(mn, w_ref[...])
        kv_ref[...] = kv.astype(BF16)
        kvt_ref[...] = kv.T.astype(BF16)

    return pl.pallas_call(
        body, name="kv_fwd", grid=(1,),
        in_specs=[pl.BlockSpec((M, D), lambda i: (0, 0)), _row1(D),
                  pl.BlockSpec((None, D, 2 * D), lambda i: (l, 0, 0))],
        out_specs=[pl.BlockSpec((M, D), lambda i: (0, 0)), pl.BlockSpec((M, 2 * D), lambda i: (0, 0)),
                   pl.BlockSpec((2 * D, M), lambda i: (0, 0))],
        out_shape=[jax.ShapeDtypeStruct((M, D), BF16), jax.ShapeDtypeStruct((M, 2 * D), BF16),
                   jax.ShapeDtypeStruct((2 * D, M), BF16)],
        compiler_params=_params("arbitrary"),
    )(mem, mg, wkv_all)


def _softmax_rows(s):
    e = jnp.exp(s - jnp.max(s, axis=-1, keepdims=True))
    return e / jnp.sum(e, axis=-1, keepdims=True)


def _branch_c_fwd(z, kv, kvt):
    S = z.shape[0]
    M, D2 = kv.shape
    D = D2 // 2
    HD = D // HEADS
    scale = 1.0 / math.sqrt(HD)
    ts = _tile(S, 256)

    def body(q_ref, zg_ref, kt_ref, v_ref, o_ref):
        for h in range(HEADS):
            cols = slice(h * HD, (h + 1) * HD)
            p = _softmax_rows(_dot(q_ref[:, cols].astype(BF16), kt_ref[cols, :]) * scale)
            att = _dot(p.astype(BF16), v_ref[:, cols])
            o_ref[:, cols] = (att * _silu(zg_ref[:, cols])).astype(BF16)

    return pl.pallas_call(
        body, name="branch_c_fwd", grid=(S // ts,),
        in_specs=[_zcol(ts, D, 6), _zcol(ts, D, 7),
                  pl.BlockSpec((D, M), lambda i: (0, 0)), pl.BlockSpec((M, D), lambda i: (0, 1))],
        out_specs=_tok(ts, D),
        out_shape=jax.ShapeDtypeStruct((S, D), BF16),
        compiler_params=_params("parallel"),
    )(z, z, kvt, kv)


def _merge_fwd(brc, z, x, c0, cg, cb, lng, lnb, t_mix, bias_full, wb_all, wo_all, l):
    S, D = x.shape
    GD = D // GROUPS
    ts = _tile(S, 256)
    CB = 256

    def body(c_ref, zm0_ref, zm1_ref, zm2_ref, x_ref, c0_ref, zbg_ref, cg_ref, cb_ref, zu_ref, zv_ref, zag_ref,
             lng_ref, lnb_ref, t_ref, bias_ref, wb_ref, wo_ref, a_ref, b_ref, m_ref, xo_ref, v_scr):
        for s in range(ts // 16):
            r = pl.ds(s * 16, 16)
            v = c0_ref[r, :]
            xc = v - _mean(v)
            c1 = (xc * lax.rsqrt(_mean(xc * xc) + LN_EPS)) * cg_ref[...] + cb_ref[...]
            b_ref[r, :] = (_silu(c1) * _silu(zbg_ref[r, :])).astype(BF16)
        for ch in range(ts // CHUNK):
            for s in range(CHUNK // 16):
                vg = _gelu(zv_ref[pl.ds(ch * CHUNK + s * 16, 16), :])
                xc = vg - _mean(vg)
                y = xc * lax.rsqrt(_mean(xc * xc) + LN_EPS)
                v_scr[pl.ds(ch * CHUNK + s * 16, 16), :] = (y * lng_ref[...] + lnb_ref[...]).astype(BF16)
            rows = pl.ds(ch * CHUNK, CHUNK)
            for g in range(GROUPS):
                cols = slice(g * GD, (g + 1) * GD)
                sv = _dot(t_ref[g], v_scr[rows, cols]) + bias_ref[:, cols]
                u = _gelu(zu_ref[rows, cols])
                a_ref[rows, cols] = ((u * sv) * _silu(zag_ref[rows, cols])).astype(BF16)

        brs = (a_ref, b_ref, c_ref)
        zms = (zm0_ref, zm1_ref, zm2_ref)
        for cb in range(D // CB):
            cols = slice(cb * CB, (cb + 1) * CB)
            acc = None
            for n in range(N_BRANCHES):
                t = _sigmoid(zms[n][:, cols]) * _dot(brs[n][...], wb_ref[n, :, cols])
                acc = t if acc is None else acc + t
            m_ref[:, cols] = acc.astype(BF16)
        for cb in range(D // CB):
            cols = slice(cb * CB, (cb + 1) * CB)
            xo_ref[:, cols] = x_ref[:, cols] + _dot(m_ref[...], wo_ref[:, cols])

    return pl.pallas_call(
        body, name="merge_fwd", grid=(S // ts,),
        in_specs=[_tok(ts, D), _zcol(ts, D, 8), _zcol(ts, D, 9), _zcol(ts, D, 10), _tok(ts, D),
                  _tok(ts, D), _zcol(ts, D, 5), _row1(D), _row1(D),
                  _zcol(ts, D, 0), _zcol(ts, D, 1), _zcol(ts, D, 2), _row1(D), _row1(D),
                  pl.BlockSpec((GROUPS, CHUNK, CHUNK), lambda i: (0, 0, 0)),
                  pl.BlockSpec((CHUNK, D), lambda i: (0, 0)),
                  pl.BlockSpec((None, N_BRANCHES, D, D), lambda i: (l, 0, 0, 0)),
                  pl.BlockSpec((None, D, D), lambda i: (l, 0, 0))],
        out_specs=[_tok(ts, D), _tok(ts, D), _tok(ts, D), _tok(ts, D)],
        out_shape=[jax.ShapeDtypeStruct((S, D), BF16), jax.ShapeDtypeStruct((S, D), BF16),
                   jax.ShapeDtypeStruct((S, D), BF16), jax.ShapeDtypeStruct((S, D), F32)],
        scratch_shapes=[pltpu.VMEM((ts, D), BF16)],
        compiler_params=_params("parallel"),
    )(brc, z, z, z, x, c0, z, cg, cb, z, z, z, lng, lnb, t_mix, bias_full, wb_all, wo_all)


def _loss_bwd(x, target, fg):
    S, D = x.shape
    ts = _tile(S, 512)

    def body(x_ref, t_ref, g_ref, dx_ref, ls_ref, dg_ref):
        @pl.when(pl.program_id(0) == 0)
        def _():
            ls_ref[...] = jnp.zeros_like(ls_ref)
            dg_ref[...] = jnp.zeros_like(dg_ref)

        def slab(si, c):
            ls, dg = c
            r = _rows(si, 16)
            xs = x_ref[r, :]
            rs = lax.rsqrt(_mean(xs * xs) + RMS_EPS)
            n = xs * rs
            e = n * g_ref[...] - t_ref[r, :]
            dy = e * (1.0 / D)
            dn = dy * g_ref[...]
            dx_ref[r, :] = rs * (dn - n * _mean(dn * n))
            return ls + _colsum(e * e), dg + _colsum(dy * n)
        zero = jnp.zeros((1, D), F32)
        ls, dg = _grouped_loop(ts // 16, 4, slab, (zero, zero))
        ls_ref[...] += ls
        dg_ref[...] += dg

    return pl.pallas_call(
        body, name="loss_bwd", grid=(S // ts,),
        in_specs=[_tok(ts, D), _tok(ts, D), _row1(D)],
        out_specs=[_tok(ts, D), _row1(D), _row1(D)],
        out_shape=[jax.ShapeDtypeStruct((S, D), F32), jax.ShapeDtypeStruct((1, D), F32),
                   jax.ShapeDtypeStruct((1, D), F32)],
        compiler_params=_params("arbitrary"),
    )(x, target, fg)


def _merge_bwd(dxp, bra, brb, brc, z, c0, lng, lnb, wb_all, wo_all, l, comm=None):
    S, D = dxp.shape
    ts = _tile(S, 256)
    CB = 256

    def body(d_ref, a_ref, b_ref, c_ref, zm0_ref, zm1_ref, zm2_ref, c0_ref, zg_ref, g_ref, be_ref, wb_ref, wo_ref,
             dzm_ref, dproj_ref, dbr_ref, dc0_ref, dzb_ref, dg_ref, db_ref, dcb_ref, dxb_scr, dbb_scr):
        @pl.when(pl.program_id(0) == 0)
        def _():
            dg_ref[...] = jnp.zeros_like(dg_ref)
            db_ref[...] = jnp.zeros_like(db_ref)
            dcb_ref[...] = jnp.zeros_like(dcb_ref)

        brs = (a_ref, b_ref, c_ref)
        zms = (zm0_ref, zm1_ref, zm2_ref)
        dxb_scr[...] = d_ref[...].astype(BF16)
        for cb in range(D // CB):
            cols = slice(cb * CB, (cb + 1) * CB)
            dm = _dot_tb(dxb_scr[...], wo_ref[cols, :])
            for n in range(N_BRANCHES):
                proj = _dot(brs[n][...], wb_ref[n, :, cols])
                g = _sigmoid(zms[n][:, cols])
                dzm_ref[:, n * D + cb * CB:n * D + (cb + 1) * CB] = ((dm * proj) * (g * (1.0 - g))).astype(BF16)
                dproj_ref[n, :, cols] = (dm * g).astype(BF16)
        for cb in range(D // CB):
            cols = slice(cb * CB, (cb + 1) * CB)
            dbb_scr[:, cols] = _dot_tb(dproj_ref[1], wb_ref[1, cols, :])

        dg = db = dcb = jnp.zeros((1, D), F32)
        for s in range(ts // 16):
            r = pl.ds(s * 16, 16)
            v = c0_ref[r, :]
            xc = v - _mean(v)
            rstd = lax.rsqrt(_mean(xc * xc) + LN_EPS)
            cn = xc * rstd
            c1 = cn * g_ref[...] + be_ref[...]
            c2, dc2_dc1 = _silu_and_grad(c1)
            sg, dsg = _silu_and_grad(zg_ref[r, :])
            d = dbb_scr[r, :]
            dzb_ref[r, :] = ((d * c2) * dsg).astype(BF16)
            dc1 = (d * sg) * dc2_dc1
            dcn = dc1 * g_ref[...]
            dc0 = rstd * ((dcn - _mean(dcn)) - cn * _mean(dcn * cn))
            dc0_ref[r, :] = dc0
            dg, db, dcb = dg + _colsum(dc1 * cn), db + _colsum(dc1), dcb + _colsum(dc0)
        dg_ref[...] += dg
        db_ref[...] += db
        dcb_ref[...] += dcb

        for plane, n in enumerate((0, 2)):
            for cb in range(D // CB):
                cols = slice(cb * CB, (cb + 1) * CB)
                dbr_ref[plane, :, cols] = _dot_tb(dproj_ref[n], wb_ref[n, cols, :])

    tokn = lambda n: pl.BlockSpec((n, ts, D), lambda i: (0, i, 0))
    grid = (S // ts,)
    body, c_in, c_out, c_shapes, c_scr, _ = _host(comm, 13, 8, 2, grid, body)
    outs = pl.pallas_call(
        body, name="merge_bwd", grid=grid,
        in_specs=[_tok(ts, D), _tok(ts, D), _tok(ts, D), _tok(ts, D),
                  _zcol(ts, D, 8), _zcol(ts, D, 9), _zcol(ts, D, 10), _tok(ts, D), _zcol(ts, D, 5), _row1(D), _row1(D),
                  pl.BlockSpec((None, N_BRANCHES, D, D), lambda i: (l, 0, 0, 0)),
                  pl.BlockSpec((None, D, D), lambda i: (l, 0, 0))] + c_in,
        out_specs=[_tok(ts, 3 * D), tokn(N_BRANCHES), tokn(2), _tok(ts, D), _zcol(ts, D, 2),
                   _row1(D), _row1(D), _row1(D)] + c_out,
        out_shape=[jax.ShapeDtypeStruct((S, 3 * D), BF16), jax.ShapeDtypeStruct((N_BRANCHES, S, D), BF16),
                   jax.ShapeDtypeStruct((2, S, D), F32), jax.ShapeDtypeStruct((S, D), F32),
                   jax.ShapeDtypeStruct((S, 3 * D), BF16), jax.ShapeDtypeStruct((1, D), F32),
                   jax.ShapeDtypeStruct((1, D), F32), jax.ShapeDtypeStruct((1, D), F32)] + c_shapes,
        scratch_shapes=[pltpu.VMEM((ts, D), BF16), pltpu.VMEM((ts, D), F32)] + c_scr,
        compiler_params=_params("arbitrary"),
    )(dxp, bra, brb, brc, z, z, z, c0, z, lng, lnb, wb_all, wo_all, *(comm.ins if comm else []))
    return (*outs[:8], outs[8:])


def _branch_a_bwd(z, dbr, lng, lnb, t_mix, t_mix_t, bias_full, mask):
    S = z.shape[0]
    D = lng.shape[1]
    GD = D // GROUPS
    ts = _tile(S, 256)
    nt = S // ts

    def body(zu_ref, zv_ref, zg_ref, d_ref, lng_ref, lnb_ref, t_ref, tt_ref, bias_ref, mask_ref,
             dz_ref, dws_ref, dbs_ref, dlg_ref, dlb_ref,
             v_scr, vn_scr, rstd_scr, dv_scr, dsv_acc):
        i = pl.program_id(0)

        @pl.when(i == 0)
        def _():
            dws_ref[...] = jnp.zeros_like(dws_ref)
            dlg_ref[...] = jnp.zeros_like(dlg_ref)
            dlb_ref[...] = jnp.zeros_like(dlb_ref)
            dsv_acc[...] = jnp.zeros_like(dsv_acc)

        def chunk(ci, carry):
            c0 = pl.multiple_of(ci * CHUNK, CHUNK)

            def ln_fwd(si, c):
                r = pl.multiple_of(si * 16, 16)
                vg = _gelu(zv_ref[pl.ds(pl.multiple_of(c0 + r, 16), 16), :])
                xc = vg - _mean(vg)
                rstd = lax.rsqrt(_mean(xc * xc) + LN_EPS)
                vn = xc * rstd
                vn_scr[pl.ds(r, 16), :] = vn
                rstd_scr[pl.ds(r, 16), :] = rstd
                v_scr[pl.ds(r, 16), :] = (vn * lng_ref[...] + lnb_ref[...]).astype(BF16)
                return c
            _grouped_loop(CHUNK // 16, 4, ln_fwd, 0)

            for g in range(GROUPS):
                cols = slice(g * GD, (g + 1) * GD)
                rows = pl.ds(c0, CHUNK)
                vb = v_scr[:, cols]
                sv = _dot(t_ref[g], vb) + bias_ref[:, cols]
                u, du = _gelu_and_grad(zu_ref[rows, cols])
                sg, dsg = _silu_and_grad(zg_ref[rows, cols])
                d = d_ref[rows, cols]
                dsv = (d * u) * sg
                dz_ref[rows, g * GD:(g + 1) * GD] = (((d * sv) * sg) * du).astype(BF16)
                dz_ref[rows, 2 * D + g * GD:2 * D + (g + 1) * GD] = (((d * u) * sv) * dsg).astype(BF16)
                dsvb = dsv.astype(BF16)
                dws_ref[g] += _dot_tb(dsvb, vb)
                dv_scr[:, cols] = _dot(tt_ref[g], dsvb)
                dsv_acc[:, cols] += dsv

            def ln_bwd(si, c):
                dlg, dlb = c
                r = pl.multiple_of(si * 16, 16)
                rr = pl.ds(r, 16)
                zrows = pl.ds(pl.multiple_of(c0 + r, 16), 16)
                dv = dv_scr[rr, :]
                vn = vn_scr[rr, :]
                dvn = dv * lng_ref[...]
                dvg = rstd_scr[rr, :] * ((dvn - _mean(dvn)) - vn * _mean(dvn * vn))
                _, gg = _gelu_and_grad(zv_ref[zrows, :])
                dz_ref[zrows, D:2 * D] = (dvg * gg).astype(BF16)
                return dlg + _colsum(dv * vn), dlb + _colsum(dv)
            zero = jnp.zeros((1, D), F32)
            dlg, dlb = _grouped_loop(CHUNK // 16, 4, ln_bwd, (zero, zero))
            dlg_ref[...] += dlg
            dlb_ref[...] += dlb
            return carry
        lax.fori_loop(0, ts // CHUNK, chunk, 0)

        @pl.when(i == nt - 1)
        def _():
            for g in range(GROUPS):
                cols = slice(g * GD, (g + 1) * GD)
                dws_ref[g] = dws_ref[g] * mask_ref[...]
                dbs_ref[g:g + 1, :] = _colsum(dsv_acc[:, cols].T)

    c2 = lambda shape: pl.BlockSpec(shape, lambda i: (0, 0))
    c3 = lambda: pl.BlockSpec((GROUPS, CHUNK, CHUNK), lambda i: (0, 0, 0))
    return pl.pallas_call(
        body, name="branch_a_bwd", grid=(nt,),
        in_specs=[_zcol(ts, D, 0), _zcol(ts, D, 1), _zcol(ts, D, 2), _plane(ts, D, 0), _row1(D), _row1(D),
                  c3(), c3(), c2((CHUNK, D)), c2((CHUNK, CHUNK))],
        out_specs=[_tok(ts, 3 * D), c3(), c2((GROUPS, CHUNK)), _row1(D), _row1(D)],
        out_shape=[jax.ShapeDtypeStruct((S, 3 * D), BF16), jax.ShapeDtypeStruct((GROUPS, CHUNK, CHUNK), F32),
                   jax.ShapeDtypeStruct((GROUPS, CHUNK), F32), jax.ShapeDtypeStruct((1, D), F32),
                   jax.ShapeDtypeStruct((1, D), F32)],
        scratch_shapes=[pltpu.VMEM((CHUNK, D), BF16), pltpu.VMEM((CHUNK, D), F32), pltpu.VMEM((CHUNK, 1), F32),
                        pltpu.VMEM((CHUNK, D), F32), pltpu.VMEM((CHUNK, D), F32)],
        compiler_params=_params("arbitrary"),
    )(z, z, z, dbr, lng, lnb, t_mix, t_mix_t, bias_full, mask)


def _branch_b_bwd(c0, z, dbr, g, b, comm=None):
    S, D = c0.shape
    ts = _tile(S, 512)

    def body(c0_ref, zg_ref, d_ref, g_ref, b_ref, dc0_ref, dz_ref, dg_ref, db_ref, dcb_ref):
        @pl.when(pl.program_id(0) == 0)
        def _():
            dg_ref[...] = jnp.zeros_like(dg_ref)
            db_ref[...] = jnp.zeros_like(db_ref)
            dcb_ref[...] = jnp.zeros_like(dcb_ref)

        def slab(si, c):
            dg, db, dcb = c
            r = _rows(si, 16)
            v = c0_ref[r, :]
            xc = v - _mean(v)
            rstd = lax.rsqrt(_mean(xc * xc) + LN_EPS)
            cn = xc * rstd
            c1 = cn * g_ref[...] + b_ref[...]
            c2, dc2_dc1 = _silu_and_grad(c1)
            sg, dsg = _silu_and_grad(zg_ref[r, :])
            d = d_ref[r, :]
            dz_ref[r, :] = ((d * c2) * dsg).astype(BF16)
            dc1 = (d * sg) * dc2_dc1
            dcn = dc1 * g_ref[...]
            dc0 = rstd * ((dcn - _mean(dcn)) - cn * _mean(dcn * cn))
            dc0_ref[r, :] = dc0
            return dg + _colsum(dc1 * cn), db + _colsum(dc1), dcb + _colsum(dc0)
        zero = jnp.zeros((1, D), F32)
        dg, db, dcb = _grouped_loop(ts // 16, 4, slab, (zero, zero, zero))
        dg_ref[...] += dg
        db_ref[...] += db
        dcb_ref[...] += dcb

    grid = (S // ts,)
    body, c_in, c_out, c_shapes, c_scr, _ = _host(comm, 5, 5, 0, grid, body)
    outs = pl.pallas_call(
        body, name="branch_b_bwd", grid=grid,
        in_specs=[_tok(ts, D), _zcol(ts, D, 5), _plane(ts, D, 1), _row1(D), _row1(D)] + c_in,
        out_specs=[_tok(ts, D), _zcol(ts, D, 2), _row1(D), _row1(D), _row1(D)] + c_out,
        out_shape=[jax.ShapeDtypeStruct((S, D), F32), jax.ShapeDtypeStruct((S, 3 * D), BF16),
                   jax.ShapeDtypeStruct((1, D), F32), jax.ShapeDtypeStruct((1, D), F32),
                   jax.ShapeDtypeStruct((1, D), F32)] + c_shapes,
        scratch_shapes=c_scr,
        compiler_params=_params("arbitrary"),
    )(c0, z, dbr, g, b, *(comm.ins if comm else []))
    return (*outs[:5], outs[5:])


def _conv_bwd(dc0, z, w_all, dz_b, l, comm=None):
    S, D = dc0.shape
    ts = _tile(S, 256)
    nt = S // ts
    CB = 256
    hb = ts // HALO
    nh = S // HALO

    def body(dc_ref, dcn_ref, za_ref, zb_ref, w_ref, dzin_ref, dz_ref, dw_ref, dc_scr, dw_acc, dsh_scr):
        del dzin_ref
        i = pl.program_id(0)

        @pl.when(i == 0)
        def _():
            dw_acc[...] = jnp.zeros_like(dw_acc)

        dc_scr[ts:ts + HALO, :] = jnp.where(i == nt - 1, 0.0, dcn_ref[...])

        def fill(si, c):
            r = _rows(si, 32)
            dc_scr[r, :] = dc_ref[r, :]
            return c
        _grouped_loop(ts // 32, 2, fill, 0)

        def colblock(ci, c):
            cols = _rows(ci, CB)
            cols_b = pl.ds(pl.multiple_of(D + ci * CB, CB), CB)
            _fill_shifts(dsh_scr, dc_scr, cols, ts)
            for s in range(ts // 32):
                rows = pl.ds(s * 32, 32)
                za = za_ref[rows, cols]
                sg = _sigmoid(zb_ref[rows, cols])
                glu = za * sg
                acc = jnp.zeros((32, CB), F32)
                for k in range(CONV_K):
                    dcs = _shifted(dsh_scr, dc_scr, cols, s * 32 + CONV_K - 1 - k, 32)
                    prod = glu * dcs
                    dw_acc[k, :, cols] += (prod[0:8] + prod[8:16]) + (prod[16:24] + prod[24:32])
                    acc = acc + dcs * w_ref[k:k + 1, cols]
                dz_ref[rows, cols] = (acc * sg).astype(BF16)
                dz_ref[rows, cols_b] = ((acc * za) * (sg * (1.0 - sg))).astype(BF16)
            return c
        lax.fori_loop(0, D // CB, colblock, 0)

        @pl.when(i == nt - 1)
        def _():
            for k in range(CONV_K):
                dw_ref[k:k + 1, :] = _colsum(dw_acc[k])

    grid = (nt,)
    body, c_in, c_out, c_shapes, c_scr, _ = _host(comm, 6, 2, 3, grid, body)
    outs = pl.pallas_call(
        body, name="conv_bwd", grid=grid,
        in_specs=[_tok(ts, D), pl.BlockSpec((HALO, D), lambda i: (jnp.minimum((i + 1) * hb, nh - 1), 0)),
                  _zcol(ts, D, 3), _zcol(ts, D, 4),
                  pl.BlockSpec((None, CONV_K, D), lambda i: (l, 0, 0)), ANY] + c_in,
        out_specs=[_tok(ts, 2 * D), pl.BlockSpec((CONV_K, D), lambda i: (0, 0))] + c_out,
        out_shape=[jax.ShapeDtypeStruct((S, 3 * D), BF16), jax.ShapeDtypeStruct((CONV_K, D), F32)] + c_shapes,
        scratch_shapes=[pltpu.VMEM((ts + HALO, D), F32), pltpu.VMEM((CONV_K, 8, D), F32),
                        pltpu.VMEM((7, ts + HALO, CB), F32)] + c_scr,
        input_output_aliases={5: 0},
        compiler_params=_params("arbitrary"),
    )(dc0, dc0, z, z, w_all, dz_b, *(comm.ins if comm else []))
    return outs[0], outs[1], outs[2:]


def _branch_c_bwd(z, dbr, kv, kvt):
    S = z.shape[0]
    M, D2 = kv.shape
    D = D2 // 2
    HD = D // HEADS
    scale = 1.0 / math.sqrt(HD)
    ts = _tile(S, 256)

    def body(q_ref, zg_ref, d_ref, kt_ref, vt_ref, k_ref, v_ref, dz_ref, dk_ref, dv_ref):
        @pl.when(pl.program_id(0) == 0)
        def _():
            dk_ref[...] = jnp.zeros_like(dk_ref)
            dv_ref[...] = jnp.zeros_like(dv_ref)

        for h in range(HEADS):
            cols = slice(h * HD, (h + 1) * HD)
            qb = q_ref[:, cols].astype(BF16)
            p = _softmax_rows(_dot(qb, kt_ref[cols, :]) * scale)
            pb = p.astype(BF16)
            att = _dot(pb, v_ref[:, cols])
            sg, dsg = _silu_and_grad(zg_ref[:, cols])
            d = d_ref[:, cols]
            dz_ref[:, D + h * HD:D + (h + 1) * HD] = ((d * att) * dsg).astype(BF16)
            datt = (d * sg).astype(BF16)
            dp = _dot(datt, vt_ref[cols, :])
            dv_ref[:, cols] += _dot_ta(pb, datt)
            ds = ((p * (dp - jnp.sum(dp * p, axis=-1, keepdims=True))) * scale).astype(BF16)
            dz_ref[:, cols] = _dot(ds, k_ref[:, cols]).astype(BF16)
            dk_ref[:, cols] += _dot_ta(ds, qb)

    return pl.pallas_call(
        body, name="branch_c_bwd", grid=(S // ts,),
        in_specs=[_zcol(ts, D, 6), _zcol(ts, D, 7), _plane(ts, D, 1),
                  pl.BlockSpec((D, M), lambda i: (0, 0)), pl.BlockSpec((D, M), lambda i: (1, 0)),
                  pl.BlockSpec((M, D), lambda i: (0, 0)), pl.BlockSpec((M, D), lambda i: (0, 1))],
        out_specs=[_tok(ts, 2 * D), pl.BlockSpec((M, D), lambda i: (0, 0)), pl.BlockSpec((M, D), lambda i: (0, 0))],
        out_shape=[jax.ShapeDtypeStruct((S, 2 * D), BF16), jax.ShapeDtypeStruct((M, D), F32),
                   jax.ShapeDtypeStruct((M, D), F32)],
        compiler_params=_params("arbitrary"),
    )(z, z, dbr, kvt, kvt, kv, kv)


def _kv_bwd(dk, dv, mn, mem, wkv_all, l, prev):
    M, D = mem.shape

    def body(*refs):
        dk_ref, dv_ref, mn_ref, mem_ref, w_ref = refs[:5]
        dw_ref, dg_ref = refs[-2:]
        dkb = dk_ref[...].astype(BF16)
        dvb = dv_ref[...].astype(BF16)
        dw_ref[:, 0:D] = _dot_ta(mn_ref[...], dkb)
        dw_ref[:, D:2 * D] = _dot_ta(mn_ref[...], dvb)
        dmn = _dot_tb(dkb, w_ref[:, 0:D]) + _dot_tb(dvb, w_ref[:, D:2 * D])
        m = mem_ref[...]
        dg_ref[...] = _colsum(dmn * (m * lax.rsqrt(_mean(m * m) + RMS_EPS)))

    full = lambda shape: pl.BlockSpec(shape, lambda i: (0, 0))
    in_specs = [full((M, D)), full((M, D)), full((M, D)), full((M, D)),
                pl.BlockSpec((None, D, 2 * D), lambda i: (l, 0, 0))]
    args = [dk, dv, mn, mem, wkv_all]
    aliases = {}
    if prev is not None:
        in_specs.append(ANY)
        args.append(prev)
        aliases = {5: 0}
    return pl.pallas_call(
        body, name="kv_bwd", grid=(1,),
        in_specs=in_specs,
        out_specs=[pl.BlockSpec((None, D, 2 * D), lambda i: (l, 0, 0)), _row1(D)],
        out_shape=[jax.ShapeDtypeStruct((wkv_all.shape[0], D, 2 * D), F32), jax.ShapeDtypeStruct((1, D), F32)],
        input_output_aliases=aliases,
        compiler_params=_params("arbitrary"),
    )(*args)


def _in_proj_bwd(pieces, w_all, x, g, dxp, l, comm=None, part=(0, 1), prev=None):
    S, D = x.shape
    tm = _tile(S, 1024)
    n_p = len(pieces)
    n_i = (S // tm) // part[1]
    i0 = part[0] * n_i
    n_in = n_p + 4 + (prev is not None)

    def body(*refs):
        dz_refs = refs[:n_p]
        w_ref, x_hbm, g_ref, dxp_hbm = refs[n_p:n_p + 4]
        dx_ref, dg_ref, acc, x_ref, dxp_ref, sem = refs[n_in:]
        i = pl.program_id(0)
        k = pl.program_id(1)
        rows = pl.ds(pl.multiple_of((i + i0) * tm, tm), tm)
        fetch = [pltpu.make_async_copy(x_hbm.at[rows, :], x_ref, sem.at[0]),
                 pltpu.make_async_copy(dxp_hbm.at[rows, :], dxp_ref, sem.at[1])]

        @pl.when(k == N_PLANES - 2)
        def _():
            for cp in fetch:
                cp.start()

        @pl.when(k == 0)
        def _():
            acc[...] = jnp.zeros_like(acc)

        @pl.when((i == 0) & (k == 0))
        def _():
            dg_ref[...] = jnp.zeros_like(dg_ref)

        for (_, lo, n), r in zip(pieces, dz_refs):
            @pl.when((k >= lo) & (k < lo + n))
            def _(r=r):
                acc[...] += _dot_tb(r[...], w_ref[...])

        @pl.when(k == N_PLANES - 1)
        def _():
            for cp in fetch:
                cp.wait()

            def slab(si, c):
                rr = _rows(si, 16)
                xs = x_ref[rr, :]
                rs = lax.rsqrt(_mean(xs * xs) + RMS_EPS)
                n = xs * rs
                dh = acc[rr, :]
                dn = dh * g_ref[...]
                dx_ref[rr, :] = rs * (dn - n * _mean(dn * n)) + dxp_ref[rr, :]
                return c + _colsum(dh * n)
            dg_ref[...] += _grouped_loop(tm // 16, 4, slab, jnp.zeros((1, D), F32))

    def piece_spec(lo, n):
        return pl.BlockSpec((tm, D), lambda i, k: (i + i0, jnp.clip(k - lo, 0, n - 1)))

    tokk = lambda: pl.BlockSpec((tm, D), lambda i, k: (i + i0, 0))
    rowk = lambda: pl.BlockSpec((1, D), lambda i, k: (0, 0))
    grid = (n_i, N_PLANES)
    body, c_in, c_out, c_shapes, c_scr, _ = _host(comm, n_in, 2, 4, grid, body)
    more_in, more_args, aliases = ([ANY], [prev], {n_in - 1: 0}) if prev is not None else ([], [], {})
    outs = pl.pallas_call(
        body, name="in_proj_bwd", grid=grid,
        in_specs=[piece_spec(lo, n) for _, lo, n in pieces]
        + [pl.BlockSpec((None, D, D), lambda i, k: (l, 0, k)), ANY, rowk(), ANY] + more_in + c_in,
        out_specs=[tokk(), rowk()] + c_out,
        out_shape=[jax.ShapeDtypeStruct((S, D), F32), jax.ShapeDtypeStruct((1, D), F32)] + c_shapes,
        scratch_shapes=[pltpu.VMEM((tm, D), F32), pltpu.VMEM((tm, D), F32), pltpu.VMEM((tm, D), F32),
                        pltpu.SemaphoreType.DMA((2,))] + c_scr,
        input_output_aliases=aliases,
        compiler_params=_params("arbitrary", "arbitrary"),
    )(*[p for p, _, _ in pieces], w_all, x, g, dxp, *more_args, *(comm.ins if comm else []))
    return outs[0], outs[1], outs[2:]


def _wgrad(a, b, out_shape, plane, col_off, prev, a_plane=None, b_plane=None):
    S = a.shape[-2]
    K1 = a.shape[-1]
    nb_cols = b.shape[-1]
    ts = _tile(S, 2048)
    tn = _tile(nb_cols, 1024)
    col_off = col_off // tn

    def body(*refs):
        a_ref, b_ref = refs[:2]
        o_ref = refs[-1]
        k = pl.program_id(1)
        prod = _dot_ta(a_ref[...].astype(BF16), b_ref[...].astype(BF16))

        @pl.when(k == 0)
        def _():
            o_ref[...] = prod

        @pl.when(k > 0)
        def _():
            o_ref[...] += prod

    if a_plane is None:
        a_spec = pl.BlockSpec((ts, K1), lambda j, k: (k, 0))
    else:
        a_spec = pl.BlockSpec((None, ts, K1), lambda j, k: (a_plane, k, 0))
    if b_plane is None:
        b_spec = pl.BlockSpec((ts, tn), lambda j, k: (k, j))
    else:
        b_spec = pl.BlockSpec((None, ts, tn), lambda j, k: (b_plane, k, j))
    in_specs, args, aliases = [a_spec, b_spec], [a, b], {}
    if prev is not None:
        in_specs.append(ANY)
        args.append(prev)
        aliases = {2: 0}
    return pl.pallas_call(
        body, name="wgrad", grid=(nb_cols // tn, S // ts),
        in_specs=in_specs,
        out_specs=pl.BlockSpec((None, K1, tn), lambda j, k: (plane, 0, col_off + j)),
        out_shape=jax.ShapeDtypeStruct(out_shape, F32),
        input_output_aliases=aliases,
        compiler_params=_params("parallel", "arbitrary"),
    )(*args)


def _position():
    x, y, c = lax.axis_index("x"), lax.axis_index("y"), lax.axis_index("c")
    chips = [(1 - x, y), (x, 1 - y), (1 - x, 1 - y)]
    return x, y, c, chips


def _sub(ref, axis, start, size):
    idx = [slice(None)] * len(ref.shape)
    idx[axis] = pl.ds(start, size)
    return ref.at[tuple(idx)]


def _dma(src, dst, ssem, rsem, to):
    return pltpu.make_async_remote_copy(src_ref=src, dst_ref=dst, send_sem=ssem, recv_sem=rsem,
                                        device_id=to, device_id_type=MESH)


def _dma_sems(*counts):
    return [pltpu.SemaphoreType.DMA((n,)) for n in counts]


def _both(a, b):
    na, oa, sa = len(a.ins), len(a.out_shapes), len(a.sem_shapes)
    phases = [(w, lambda i, o, s, fn=fn: fn(i[:na], o[:oa], s[:sa])) for w, fn in a.phases]
    phases += [(w, lambda i, o, s, fn=fn: fn(i[na:], o[oa:], s[sa:])) for w, fn in b.phases]
    return _Hosted(a.ins + b.ins, list(a.out_shapes) + list(b.out_shapes), list(a.sem_shapes) + list(b.sem_shapes), phases)


def _run_alone(comm, name):
    ci, co = len(comm.ins), len(comm.out_shapes)

    def body(*refs):
        for _, fn in comm.phases:
            fn(refs[:ci], refs[ci:ci + co], refs[ci + co:])

    return pl.pallas_call(
        body, name=name, in_specs=[ANY] * ci, out_specs=[ANY] * co, out_shape=list(comm.out_shapes),
        scratch_shapes=list(comm.sem_shapes),
    )(*comm.ins)


def _gather_comm(shards, layers, axes, start_at="first"):
    n = len(shards)
    out_shapes = []
    for s, (ba, _) in zip(shards, axes):
        shp = list(s.shape[1:])
        shp[ba] *= N_CHIPS
        out_shapes.append(jax.ShapeDtypeStruct(tuple(shp), s.dtype))

    def half(ref, t, h):
        ha = axes[t][1]
        if ha is None:
            return ref
        hs = shards[t].shape[1 + ha] // 2
        return _sub(ref, ha, h * hs, hs)

    def block(outs, t, blk):
        ba = axes[t][0]
        bs = shards[t].shape[1 + ba]
        return _sub(outs[t], ba, blk * bs, bs)

    def start(ins, outs, sems):
        s_own, r_own, s_ici, r_ici, _, _ = sems
        x, y, c, chips = _position()
        j = 2 * x + y
        for t in range(n):
            _dma(ins[t].at[layers[t]], block(outs, t, j), s_own.at[t], r_own.at[t], (x, y, 1 - c)).start()
            for k, (px, py) in enumerate(chips):
                _dma(half(ins[t].at[layers[t]], t, c), half(block(outs, t, j), t, c),
                     s_ici.at[3 * t + k], r_ici.at[3 * t + k], (px, py, c)).start()

    def forward(ins, outs, sems):
        _, _, s_ici, r_ici, s_fwd, r_fwd = sems
        x, y, c, chips = _position()
        for t in range(n):
            if axes[t][1] is None:
                continue
            for k, (px, py) in enumerate(chips):
                got = half(block(outs, t, 2 * px + py), t, c)
                _dma(got, got, s_ici.at[3 * t + k], r_ici.at[3 * t + k], (px, py, c)).wait_recv()
                _dma(got, got, s_fwd.at[3 * t + k], r_fwd.at[3 * t + k], (x, y, 1 - c)).start()

    def finish(ins, outs, sems):
        s_own, r_own, s_ici, r_ici, s_fwd, r_fwd = sems
        x, y, c, chips = _position()
        j = 2 * x + y
        sib = (x, y, 1 - c)
        for t in range(n):
            for k, (px, py) in enumerate(chips):
                i = 3 * t + k
                sent = half(block(outs, t, j), t, c)
                _dma(sent, sent, s_ici.at[i], r_ici.at[i], (px, py, c)).wait_send()
                theirs = block(outs, t, 2 * px + py)
                if axes[t][1] is None:
                    _dma(theirs, theirs, s_ici.at[i], r_ici.at[i], (px, py, c)).wait_recv()
                else:
                    got, other = half(theirs, t, c), half(theirs, t, 1 - c)
                    _dma(got, got, s_fwd.at[i], r_fwd.at[i], sib).wait_send()
                    _dma(other, other, s_fwd.at[i], r_fwd.at[i], sib).wait_recv()
            _dma(ins[t].at[layers[t]], block(outs, t, j), s_own.at[t], r_own.at[t], sib).wait()

    return _Hosted(list(shards), out_shapes, _dma_sems(n, n, 3 * n, 3 * n, 3 * n, 3 * n),
                   [(start_at, start), ("late", forward), ("last", finish)])


def _halved(shape, axis):
    shp = list(shape)
    shp[axis] //= 2
    return tuple(shp)


def _pair_comm(grads, axes):
    n = len(grads)

    def copies(ins, outs, sems):
        x, y, c, _ = _position()
        cps = []
        for t in range(n):
            ha = axes[t][1]
            hs = grads[t].shape[ha] // 2
            cps.append(_dma(_sub(ins[t], ha, (1 - c) * hs, hs), outs[t], sems[0].at[t], sems[1].at[t], (x, y, 1 - c)))
        return cps

    def start(ins, outs, sems):
        for cp in copies(ins, outs, sems):
            cp.start()

    def finish(ins, outs, sems):
        for cp in copies(ins, outs, sems):
            cp.wait()

    return _Hosted(list(grads), [jax.ShapeDtypeStruct(_halved(g.shape, ax[1]), g.dtype) for g, ax in zip(grads, axes)],
                   _dma_sems(n, n), [("first", start), ("last", finish)])


def _block_comm(sums, axes):
    n = len(sums)
    out_shapes = []
    for s, (ba, _) in zip(sums, axes):
        shp = list(s.shape)
        shp[ba] //= N_CHIPS
        out_shapes.append(jax.ShapeDtypeStruct((3,) + tuple(shp), s.dtype))

    def copies(ins, outs, sems):
        x, y, c, chips = _position()
        cps = []
        for t in range(n):
            ba = axes[t][0]
            bs = sums[t].shape[ba] // N_CHIPS
            for k, (px, py) in enumerate(chips):
                cps.append(_dma(_sub(ins[t], ba, (2 * px + py) * bs, bs), outs[t].at[k],
                                sems[0].at[3 * t + k], sems[1].at[3 * t + k], (px, py, c)))
        return cps

    def start(ins, outs, sems):
        for cp in copies(ins, outs, sems):
            cp.start()

    def finish(ins, outs, sems):
        for cp in copies(ins, outs, sems):
            cp.wait()

    return _Hosted(list(sums), out_shapes, _dma_sems(3 * n, 3 * n), [("first", start), ("last", finish)])


def _sibling_share(shards, axes):
    n = len(shards)

    def body(*refs):
        outs = refs[n:2 * n]
        ssem, rsem = refs[2 * n:]
        x, y, c, _ = _position()
        cps = []
        for t in range(n):
            ha = axes[t][1]
            hs = shards[t].shape[ha] // 2
            mine = _sub(outs[t], ha, c * hs, hs)
            cps.append(_dma(mine, mine, ssem.at[t], rsem.at[t], (x, y, 1 - c)))
        for cp in cps:
            cp.start()
        for cp in cps:
            cp.wait()

    return pl.pallas_call(
        body, name="sibling_share",
        in_specs=[ANY] * n, out_specs=[ANY] * n,
        out_shape=[jax.ShapeDtypeStruct(s.shape, s.dtype) for s in shards],
        input_output_aliases={t: t for t in range(n)},
        scratch_shapes=_dma_sems(n, n),
    )(*shards)


def _small_allreduce(p):
    R, C = p.shape

    def body(p_ref, o_ref, sib_buf, chip_sums, ssem, rsem):
        x, y, c, chips = _position()
        j = 2 * x + y
        sib = pltpu.make_async_remote_copy(
            src_ref=p_ref, dst_ref=sib_buf, send_sem=ssem.at[0], recv_sem=rsem.at[0],
            device_id=(x, y, 1 - c), device_id_type=MESH)
        sib.start()
        sib.wait()
        chip_sums[j] = p_ref[...] + sib_buf[...]
        cps = [pltpu.make_async_remote_copy(
            src_ref=chip_sums.at[j], dst_ref=chip_sums.at[j], send_sem=ssem.at[1 + k], recv_sem=rsem.at[1 + k],
            device_id=(px, py, c), device_id_type=MESH) for k, (px, py) in enumerate(chips)]
        for cp in cps:
            cp.start()
        for k, (px, py) in enumerate(chips):
            pltpu.make_async_remote_copy(
                src_ref=chip_sums.at[j], dst_ref=chip_sums.at[2 * px + py],
                send_sem=ssem.at[1 + k], recv_sem=rsem.at[1 + k],
                device_id=(px, py, c), device_id_type=MESH).wait()
        o_ref[...] = ((chip_sums[0] + chip_sums[1]) + chip_sums[2]) + chip_sums[3]

    vm = pl.BlockSpec(memory_space=pltpu.VMEM)
    return pl.pallas_call(
        body, name="small_allreduce",
        in_specs=[vm], out_specs=vm, out_shape=jax.ShapeDtypeStruct((R, C), F32),
        scratch_shapes=[pltpu.VMEM((R, C), F32), pltpu.VMEM((N_CHIPS, R, C), F32),
                        pltpu.SemaphoreType.DMA((4,)), pltpu.SemaphoreType.DMA((4,))],
        compiler_params=pltpu.CompilerParams(vmem_limit_bytes=VMEM_LIMIT_BYTES),
    )(p)


def _add_half(g, recv, half_axis, pos):
    B, R, C = recv.shape
    tr = _tile(R, 256)
    tc = _tile(C, 2816)
    nr, nc = R // tr, C // tc
    if half_axis == 1:
        g_map = lambda b, i, j, pos_ref: (b, pos_ref[0] * nr + i, j)
    else:
        g_map = lambda b, i, j, pos_ref: (b, i, pos_ref[0] * nc + j)

    def body(pos_ref, g_ref, r_ref, o_ref, ob_ref):
        del pos_ref
        s = g_ref[...] + r_ref[...]
        o_ref[...] = s
        ob_ref[...] = s.astype(BF16)

    spec = lambda: pl.BlockSpec((None, tr, tc), lambda b, i, j, pos_ref: (b, i, j))
    return pl.pallas_call(
        body, name="add_half",
        grid_spec=pltpu.PrefetchScalarGridSpec(
            num_scalar_prefetch=1, grid=(B, nr, nc),
            in_specs=[pl.BlockSpec((None, tr, tc), g_map), spec()],
            out_specs=[spec(), spec()]),
        out_shape=[jax.ShapeDtypeStruct((B, R, C), F32), jax.ShapeDtypeStruct((B, R, C), BF16)],
        compiler_params=_params("parallel", "parallel", "parallel"),
    )(pos, g, recv)


def _sum_half(own, recv, axes, pos):
    _, B, rs, cs = recv.shape
    ba, ha = axes
    tr = _tile(rs, 128)
    nr = rs // tr

    def body(pos_ref, o_ref, r0_ref, r1_ref, r2_ref, out_ref):
        del pos_ref
        out_ref[...] = (((o_ref[...] + r0_ref[...].astype(F32)) + r1_ref[...].astype(F32))
                        + r2_ref[...].astype(F32))

    if ba == 2:
        own_spec = pl.BlockSpec((None, tr, cs), lambda b, i, pos_ref: (b, i, pos_ref[1]))
    else:
        own_spec = pl.BlockSpec((None, tr, cs), lambda b, i, pos_ref: (b, pos_ref[1] * nr + i, 0))
    if ha == 1:
        out_shape, out_map = (B, 2 * rs, cs), (lambda b, i, pos_ref: (b, pos_ref[0] * nr + i, 0))
    else:
        out_shape, out_map = (B, rs, 2 * cs), (lambda b, i, pos_ref: (b, i, pos_ref[0]))
    rspec = lambda k: pl.BlockSpec((None, None, tr, cs), lambda b, i, pos_ref: (k, b, i, 0))
    return pl.pallas_call(
        body, name="sum_half",
        grid_spec=pltpu.PrefetchScalarGridSpec(
            num_scalar_prefetch=1, grid=(B, nr),
            in_specs=[own_spec, rspec(0), rspec(1), rspec(2)],
            out_specs=pl.BlockSpec((None, tr, cs), out_map)),
        out_shape=jax.ShapeDtypeStruct(out_shape, F32),
        compiler_params=_params("parallel", "parallel"),
    )(pos, own, recv, recv, recv)


def _adamw(w, g, m, v):
    shape = w.shape
    C = shape[-1]
    R = w.size // C
    tr = R
    for cand in (512, 256, 128, 64, 32, 16, 8):
        if R % cand == 0 and cand * C * 4 <= (1 << 20):
            tr = cand
            break
    bc1 = 1.0 - ADAM_B1 ** ADAM_STEP
    bc2 = 1.0 - ADAM_B2 ** ADAM_STEP

    def body(w_ref, g_ref, m_ref, v_ref, d_ref, mo_ref, vo_ref):
        gg = g_ref[...]
        mn = ADAM_B1 * m_ref[...] + (1.0 - ADAM_B1) * gg
        vn = ADAM_B2 * v_ref[...] + (1.0 - ADAM_B2) * (gg * gg)
        mo_ref[...] = mn
        vo_ref[...] = vn
        d_ref[...] = -ADAM_LR * ((mn / bc1) / (jnp.sqrt(vn / bc2) + ADAM_EPS) + ADAM_WD * w_ref[...])

    spec = lambda: pl.BlockSpec((tr, C), lambda i: (i, 0))
    outs = pl.pallas_call(
        body, name="adamw", grid=(R // tr,),
        in_specs=[spec()] * 4, out_specs=[spec()] * 3,
        out_shape=[jax.ShapeDtypeStruct((R, C), F32)] * 3,
        compiler_params=_params("parallel"),
    )(*[a.reshape(R, C) for a in (w, g, m, v)])
    return tuple(o.reshape(shape) for o in outs)


def kernel(x, mem, norm_g, mem_norm_g, w_in, gmlp_ln_g, gmlp_ln_b, w_s, b_s, conv_w, conv_b, conv_ln_g, conv_ln_b, w_kv, w_branch, w_out, final_norm_g, loss_target, m_norm_g, m_mem_norm_g, m_w_in, m_gmlp_ln_g, m_gmlp_ln_b, m_w_s, m_b_s, m_conv_w, m_conv_b, m_conv_ln_g, m_conv_ln_b, m_w_kv, m_w_branch, m_w_out, m_final_norm_g, v_norm_g, v_mem_norm_g, v_w_in, v_gmlp_ln_g, v_gmlp_ln_b, v_w_s, v_b_s, v_conv_w, v_conv_b, v_conv_ln_g, v_conv_ln_b, v_w_kv, v_w_branch, v_w_out, v_final_norm_g):
    xs, mems, tgt = x[0], mem[0], loss_target[0]
    S, D = xs.shape
    assert D // GROUPS == LANES and S % CHUNK == 0 and w_s.shape[-1] == CHUNK
    L = N_LAYERS

    assert L == 2
    shards = [w_in.astype(BF16)[:, None], w_kv.astype(BF16)[:, None], w_branch.astype(BF16), w_out.astype(BF16)[:, None],
              conv_w[:, None]]
    gather_axes = [(2, 1), (2, 1), (1, 2), (1, 2), (2, None)]
    reduce_axes = gather_axes[:4]
    win0 = _run_alone(_gather_comm(shards[:1], [0], gather_axes[:1]), "gather_w_in0")[0]
    gather_early = _gather_comm(shards[1:] + shards[:1], [0] * 4 + [1], gather_axes[1:] + gather_axes[:1])
    gather_late = _gather_comm(shards[1:], [1] * 4, gather_axes[1:])

    tri = jnp.tril(jnp.ones((CHUNK, CHUNK), F32))
    t_mix = (w_s * tri).astype(BF16)
    t_mix_t = jnp.swapaxes(t_mix, -1, -2)
    bias_full = jnp.repeat(jnp.swapaxes(b_s, -1, -2), D // GROUPS, axis=-1)
    row = lambda a, l: a[l][None, :]
    xi, yi, ci = lax.axis_index("x"), lax.axis_index("y"), lax.axis_index("c")
    pos = jnp.stack([ci, 2 * xi + yi]).astype(jnp.int32)

    saved = []
    h = xs
    weights = [None] * L
    for l in range(L):
        if l == 0:
            z, hb, fetched = _in_proj(h, row(norm_g, l), win0, 0, gather_early)
            weights[0] = [win0] + list(fetched[:4])
        else:
            z, hb, _ = _in_proj(h, row(norm_g, l), weights[l][0], 0)
        win, wkv, wb, wo, cw = weights[l]
        c0, late = _conv_fwd(z, cw, row(conv_b, l), 0, gather_late if l == 0 else None)
        if l == 0:
            weights[1] = [fetched[4]] + list(late)
        mn, kv, kvt = _kv_fwd(mems, row(mem_norm_g, l), wkv, 0)
        brc = _branch_c_fwd(z, kv, kvt)
        bra, brb, merged, h_next = _merge_fwd(brc, z, h, c0, row(conv_ln_g, l), row(conv_ln_b, l), row(gmlp_ln_g, l),
                                              row(gmlp_ln_b, l), t_mix[l], bias_full[l], wb[None], wo, 0)
        saved.append((h, z, hb, bra, c0, brb, mn, kv, kvt, brc, merged))
        h = h_next

    dx, loss_cols, d_final_g = _loss_bwd(h, tgt, final_norm_g[None, :])
    loss = lax.psum(0.5 * jnp.sum(loss_cols) / D, ("x", "y", "c"))

    def add_halves(grads, recv, axes):
        return [_add_half(g, r, ax[1], pos) for g, r, ax in zip(grads, recv, axes)]

    def finish_reduction(sums, blocks, axes):
        halves = [_sum_half(s, b, ax, pos) for (s, _), b, ax in zip(sums, blocks, axes)]
        return list(_sibling_share(halves, axes))

    small = {k: [None] * L for k in ("norm_g", "mem_norm_g", "gmlp_ln_g", "gmlp_ln_b", "w_s", "b_s", "conv_w",
                                     "conv_b", "conv_ln_g", "conv_ln_b")}
    early_axes = reduce_axes[1:]
    pending, reduced, early = None, [None] * L, None
    for l in reversed(range(L)):
        h_in, z, hb, bra, c0, brb, mn, kv, kvt, brc, merged = saved[l]
        win, wkv, wb, wo, cw = weights[l]
        dz_m, dproj, dbr, dc0, dz_b, dcg, dcbeta, dcb, recv = _merge_bwd(
            dx, bra, brb, brc, z, c0, row(conv_ln_g, l), row(conv_ln_b, l), wb[None], wo, 0,
            _pair_comm(pending, reduce_axes) if pending else None)
        sums = add_halves(pending, recv, reduce_axes) if pending else []
        g_o = _wgrad(merged, dx, (1, D, D), 0, 0, None)
        g_b = None
        for n, br in enumerate((bra, brb, brc)):
            g_b = _wgrad(br, dproj, (N_BRANCHES, D, D), n, 0, g_b, b_plane=n)
        dz_a, dws, dbs, dlg, dlb = _branch_a_bwd(z, dbr, row(gmlp_ln_g, l), row(gmlp_ln_b, l),
                                                t_mix[l], t_mix_t[l], bias_full[l], tri)
        dz_c, dk, dv = _branch_c_bwd(z, dbr, kv, kvt)
        g_kv, dmg = _kv_bwd(dk, dv, mn, mems, wkv, 0, None)
        ready = [g_kv, g_b, g_o]
        riding = _both(_block_comm([sb for _, sb in sums], reduce_axes), _pair_comm(ready, early_axes)) if pending else None
        dz_b, dcw, got = _conv_bwd(dc0, z, cw, dz_b, 0, riding)
        sums_early = add_halves(ready, got[4:], early_axes) if pending else []
        if pending:
            reduced[l + 1] = finish_reduction(sums, got[:4], reduce_axes)
        pieces =[(dz_a, 0, 3), (dz_b, 3, 3), (dz_c, 6, 2), (dz_m, 8, 3)]
        g_in = None
        for dzp, lo, _ in pieces:
            g_in = _wgrad(hb, dzp, (1, D, N_PLANES * D), 0, lo * D, g_in)
        if pending:
            riding = _block_comm([sb for _, sb in sums_early], early_axes)
            dx_half, dng_a, got = _in_proj_bwd(pieces, win, h_in, row(norm_g, l), dx, 0,
                                               _both(riding, _pair_comm([g_in], reduce_axes[:1])), (0, 2))
            early = finish_reduction(sums_early, got[:3], early_axes)
            sums_in = add_halves([g_in], got[3:], reduce_axes[:1])
            dx, dng_b, blocks_in = _in_proj_bwd(pieces, win, h_in, row(norm_g, l), dx, 0,
                                                _block_comm([sb for _, sb in sums_in], reduce_axes[:1]), (1, 2), dx_half)
            dng = dng_a + dng_b
            reduced[l] = finish_reduction(sums_in, blocks_in, reduce_axes[:1]) + early
        else:
            dx, dng, _ = _in_proj_bwd(pieces, win, h_in, row(norm_g, l), dx, 0)
        pending = [g_in, g_kv, g_b, g_o]
        for k, val in (("norm_g", dng), ("mem_norm_g", dmg), ("gmlp_ln_g", dlg), ("gmlp_ln_b", dlb), ("w_s", dws),
                       ("b_s", dbs), ("conv_w", dcw), ("conv_b", dcb), ("conv_ln_g", dcg), ("conv_ln_b", dcbeta)):
            small[k][l] = val
    grad_x = dx[None]

    grad_w_in, grad_w_kv, grad_w_branch, grad_w_out = [
        jnp.stack([reduced[0][t], reduced[1][t]]).reshape(w.shape) for t, w in enumerate((w_in, w_kv, w_branch, w_out))]

    order = [("norm_g", norm_g.shape), ("mem_norm_g", mem_norm_g.shape), ("gmlp_ln_g", gmlp_ln_g.shape),
             ("gmlp_ln_b", gmlp_ln_b.shape), ("w_s", w_s.shape), ("b_s", b_s.shape),
             ("conv_w", (L, CONV_K, D)), ("conv_b", conv_b.shape), ("conv_ln_g", conv_ln_g.shape),
             ("conv_ln_b", conv_ln_b.shape)]
    parts = [jnp.stack([v.reshape(shp[1:]) for v in small[k]]).reshape(-1, LANES) for k, shp in order]
    parts.append(d_final_g.reshape(-1, LANES))
    sizes = [p.shape[0] for p in parts]
    assert all(s % 8 == 0 for s in sizes)
    total = _small_allreduce(jnp.concatenate(parts, axis=0))
    red, off = {}, 0
    for (k, shp), n in zip(order + [("final_norm_g", final_norm_g.shape)], sizes):
        red[k] = total[off:off + n].reshape(shp)
        off += n
    cs = conv_w.shape[-1]
    red["conv_w"] = lax.dynamic_slice_in_dim(red["conv_w"], (2 * xi + yi) * cs, cs, axis=2)
    red.update(w_in=grad_w_in, w_kv=grad_w_kv, w_branch=grad_w_branch, w_out=grad_w_out)

    names = ["norm_g", "mem_norm_g", "w_in", "gmlp_ln_g", "gmlp_ln_b", "w_s", "b_s", "conv_w", "conv_b",
             "conv_ln_g", "conv_ln_b", "w_kv", "w_branch", "w_out", "final_norm_g"]
    weights = dict(norm_g=norm_g, mem_norm_g=mem_norm_g, w_in=w_in, gmlp_ln_g=gmlp_ln_g, gmlp_ln_b=gmlp_ln_b,
                   w_s=w_s, b_s=b_s, conv_w=conv_w, conv_b=conv_b, conv_ln_g=conv_ln_g, conv_ln_b=conv_ln_b,
                   w_kv=w_kv, w_branch=w_branch, w_out=w_out, final_norm_g=final_norm_g)
    ms = dict(norm_g=m_norm_g, mem_norm_g=m_mem_norm_g, w_in=m_w_in, gmlp_ln_g=m_gmlp_ln_g, gmlp_ln_b=m_gmlp_ln_b,
              w_s=m_w_s, b_s=m_b_s, conv_w=m_conv_w, conv_b=m_conv_b, conv_ln_g=m_conv_ln_g, conv_ln_b=m_conv_ln_b,
              w_kv=m_w_kv, w_branch=m_w_branch, w_out=m_w_out, final_norm_g=m_final_norm_g)
    vs = dict(norm_g=v_norm_g, mem_norm_g=v_mem_norm_g, w_in=v_w_in, gmlp_ln_g=v_gmlp_ln_g, gmlp_ln_b=v_gmlp_ln_b,
              w_s=v_w_s, b_s=v_b_s, conv_w=v_conv_w, conv_b=v_conv_b, conv_ln_g=v_conv_ln_g, conv_ln_b=v_conv_ln_b,
              w_kv=v_w_kv, w_branch=v_w_branch, w_out=v_w_out, final_norm_g=v_final_norm_g)
    deltas, new_m, new_v = [], [], []
    for k in names:
        w2 = weights[k] if weights[k].ndim > 1 else weights[k][None, :]
        d, mo, vo = _adamw(w2, red[k].reshape(w2.shape), ms[k].reshape(w2.shape), vs[k].reshape(w2.shape))
        deltas.append(d.reshape(weights[k].shape))
        new_m.append(mo.reshape(weights[k].shape))
        new_v.append(vo.reshape(weights[k].shape))
    grads = [red[k].reshape(weights[k].shape) for k in names]
    return (loss, grad_x, *grads, *deltas, *new_m, *new_v)
```

```python
import math

import jax
import jax.numpy as jnp
from jax import lax
from jax.experimental import pallas as pl
from jax.experimental.pallas import tpu as pltpu

F32 = jnp.float32
BF16 = jnp.bfloat16

N_LAYERS = 2
N_BRANCHES = 3
N_PLANES = 11
N_CHIPS = 4
CHUNK = 128
GROUPS = 8
HEADS = 4
CONV_K = 31
HALO = 32
LANES = 128
RMS_EPS = 1e-6
LN_EPS = 1e-5
ADAM_LR, ADAM_B1, ADAM_B2, ADAM_EPS, ADAM_WD, ADAM_STEP = 0.001, 0.9, 0.999, 1e-08, 0.01, 10
VMEM_LIMIT_BYTES = 48 * 1024 * 1024
MESH = pl.DeviceIdType.MESH
ANY = pl.BlockSpec(memory_space=pl.ANY)


def _params(*sem):
    return pltpu.CompilerParams(dimension_semantics=sem, vmem_limit_bytes=VMEM_LIMIT_BYTES)


def _tile(n, pref):
    t = min(n, pref)
    assert n % t == 0, (n, t)
    return t


def _dot(a, b):
    return jnp.dot(a, b, preferred_element_type=F32)


def _dot_tb(a, b):
    return lax.dot_general(a, b, (((1,), (1,)), ((), ())), preferred_element_type=F32)


def _dot_ta(a, b):
    return lax.dot_general(a, b, (((0,), (0,)), ((), ())), preferred_element_type=F32)


def _sigmoid(x):
    return 1.0 / (1.0 + jnp.exp(-x))


def _silu(x):
    return x * _sigmoid(x)


def _silu_and_grad(x):
    s = _sigmoid(x)
    return x * s, s * (1.0 + x * (1.0 - s))


_GELU_C = math.sqrt(2.0 / math.pi)
_GELU_A = 0.044715


def _gelu(x):
    return x * (0.5 * (1.0 + jnp.tanh(_GELU_C * (x + _GELU_A * (x * x * x)))))


def _gelu_and_grad(x):
    x2 = x * x
    t = jnp.tanh(_GELU_C * (x + _GELU_A * (x2 * x)))
    cdf = 0.5 * (1.0 + t)
    dcdf = 0.5 * (1.0 - t * t) * (_GELU_C * (1.0 + 3.0 * _GELU_A * x2))
    return x * cdf, cdf + x * dcdf


def _rows(i, n):
    return pl.ds(pl.multiple_of(i * n, n), n)


def _grouped_loop(n, group, body, init):
    assert n % group == 0

    def trip(i, c):
        for u in range(group):
            c = body(i * group + u, c)
        return c
    return lax.fori_loop(0, n // group, trip, init)


def _mean(x):
    return jnp.mean(x, axis=-1, keepdims=True)


def _colsum(x):
    return jnp.sum(x, axis=0, keepdims=True)


def _tok(ts, d):
    return pl.BlockSpec((ts, d), lambda i: (i, 0))


def _zcol(ts, d, n):
    return pl.BlockSpec((ts, d), lambda i: (i, n))


def _row1(d):
    return pl.BlockSpec((1, d), lambda i: (0, 0))


def _plane(ts, d, n):
    return pl.BlockSpec((None, ts, d), lambda i: (n, i, 0))


class _Hosted:
    def __init__(self, ins, out_shapes, sem_shapes, phases):
        self.ins, self.out_shapes, self.sem_shapes, self.phases = ins, out_shapes, sem_shapes, phases


def _host(comm, n_in, n_out, n_scr, grid, body, n_prefetch=0, wants_results=False):
    if comm is None:
        return body, [], [], [], [], None
    ci, co = len(comm.ins), len(comm.out_shapes)
    total = math.prod(grid)
    at = {"first": 0, "late": (7 * total) // 8, "last": total - 1}

    def wrapped(*refs):
        pre, refs = refs[:n_prefetch], refs[n_prefetch:]
        ins, cin = refs[:n_in], refs[n_in:n_in + ci]
        o0 = n_in + ci
        outs, cout = refs[o0:o0 + n_out], refs[o0 + n_out:o0 + n_out + co]
        s0 = o0 + n_out + co
        scr, csem = refs[s0:s0 + n_scr], refs[s0 + n_scr:]
        step = pl.program_id(0)
        for a in range(1, len(grid)):
            step = step * grid[a] + pl.program_id(a)

        def run(last):
            for w, fn in comm.phases:
                if (w == "last") == last:
                    @pl.when(step == at.get(w, w))
                    def _(fn=fn):
                        fn(cin, cout, csem)
        run(False)
        body(*pre, *ins, *outs, *scr, **({"cout": cout} if wants_results else {}))
        run(True)

    return wrapped, [ANY] * ci, [ANY] * co, list(comm.out_shapes), list(comm.sem_shapes), ("arbitrary",) * len(grid)


def _in_proj(x, g, w_all, l, comm=None):
    S, D = x.shape
    N = w_all.shape[2]
    tm, tn = _tile(S, 1024), _tile(N, 1408)
    grid = (S // tm, N // tn)

    def body(x_ref, g_ref, w_ref, z_ref, hb_ref, h_scr):
        @pl.when(pl.program_id(1) == 0)
        def _():
            def slab(i, c):
                r = _rows(i, 16)
                xs = x_ref[r, :]
                h = (xs * lax.rsqrt(_mean(xs * xs) + RMS_EPS)) * g_ref[...]
                hb = h.astype(BF16)
                h_scr[r, :] = hb
                hb_ref[r, :] = hb
                return c
            _grouped_loop(tm // 16, 4, slab, 0)
        z_ref[...] = _dot(h_scr[...], w_ref[...])

    body, c_in, c_out, c_shapes, c_scr, sem = _host(comm, 3, 2, 1, grid, body)
    outs = pl.pallas_call(
        body, name="in_proj", grid=grid,
        in_specs=[pl.BlockSpec((tm, D), lambda i, j: (i, 0)),
                  pl.BlockSpec((1, D), lambda i, j: (0, 0)),
                  pl.BlockSpec((None, D, tn), lambda i, j: (l, 0, j))] + c_in,
        out_specs=[pl.BlockSpec((tm, tn), lambda i, j: (i, j)),
                   pl.BlockSpec((tm, D), lambda i, j: (i, 0))] + c_out,
        out_shape=[jax.ShapeDtypeStruct((S, N), F32), jax.ShapeDtypeStruct((S, D), BF16)] + c_shapes,
        scratch_shapes=[pltpu.VMEM((tm, D), BF16)] + c_scr,
        compiler_params=_params(*(sem or ("parallel", "arbitrary"))),
    )(x, g, w_all, *(comm.ins if comm else []))
    return outs[0], outs[1], outs[2:]


def _branch_a_fwd(z, lng, lnb, t_mix, bias_full):
    S = z.shape[0]
    D = lng.shape[1]
    GD = D // GROUPS
    ts = _tile(S, 256)

    def body(zu_ref, zv_ref, zg_ref, lng_ref, lnb_ref, t_ref, bias_ref, a_ref, v_scr):
        def chunk(ci, carry):
            c0 = pl.multiple_of(ci * CHUNK, CHUNK)

            def slab(si, c):
                r = pl.multiple_of(si * 16, 16)
                vg = _gelu(zv_ref[pl.ds(pl.multiple_of(c0 + r, 16), 16), :])
                xc = vg - _mean(vg)
                y = xc * lax.rsqrt(_mean(xc * xc) + LN_EPS)
                v_scr[pl.ds(r, 16), :] = (y * lng_ref[...] + lnb_ref[...]).astype(BF16)
                return c
            _grouped_loop(CHUNK // 16, 4, slab, 0)
            for g in range(GROUPS):
                cols = slice(g * GD, (g + 1) * GD)
                sv = _dot(t_ref[g], v_scr[:, cols]) + bias_ref[:, cols]
                u = _gelu(zu_ref[pl.ds(c0, CHUNK), cols])
                a_ref[pl.ds(c0, CHUNK), cols] = ((u * sv) * _silu(zg_ref[pl.ds(c0, CHUNK), cols])).astype(BF16)
            return carry
        lax.fori_loop(0, ts // CHUNK, chunk, 0)

    return pl.pallas_call(
        body, name="branch_a_fwd", grid=(S // ts,),
        in_specs=[_zcol(ts, D, 0), _zcol(ts, D, 1), _zcol(ts, D, 2), _row1(D), _row1(D),
                  pl.BlockSpec((GROUPS, CHUNK, CHUNK), lambda i: (0, 0, 0)),
                  pl.BlockSpec((CHUNK, D), lambda i: (0, 0))],
        out_specs=_tok(ts, D),
        out_shape=jax.ShapeDtypeStruct((S, D), BF16),
        scratch_shapes=[pltpu.VMEM((CHUNK, D), BF16)],
        compiler_params=_params("parallel"),
    )(z, z, z, lng, lnb, t_mix, bias_full)


SH = HALO - CONV_K + 1


def _fill_shifts(sh_scr, src_scr, cols, ts, base=0):
    for j in range(1, 8):
        for g in range(ts // 32):
            sh_scr[j - 1, pl.ds(g * 32, 32), :] = src_scr[pl.ds(base + g * 32 + j, 32), cols]
        sh_scr[j - 1, pl.ds(ts, HALO - 8), :] = src_scr[pl.ds(base + ts + j, HALO - 8), cols]


def _shifted(sh_scr, src_scr, cols, row0, n, base=0):
    q, j = divmod(row0, 8)
    if j == 0:
        return src_scr[pl.ds(base + row0, n), cols]
    return sh_scr[j - 1, pl.ds(8 * q, n), :]


def _in_proj_conv(x, g, w_all, cw_all, cb, l, comm=None):
    S, D = x.shape
    N = w_all.shape[2]
    tm = _tile(S, 1024)
    QR, CB = 256, 256
    nq = tm // QR
    PA, PB = 3, 4
    assert tm % QR == 0 and PB + 1 + nq <= N // D
    grid = (S // tm, N // D)

    def body(x_ref, g_ref, w_ref, cw_ref, cb_ref, z_ref, hb_ref, c0_ref, h_scr, za_scr, glu_scr, sh_scr):
        i, j = pl.program_id(0), pl.program_id(1)

        @pl.when(j == 0)
        def _():
            def slab(si, c):
                r = _rows(si, 16)
                xs = x_ref[r, :]
                hb = ((xs * lax.rsqrt(_mean(xs * xs) + RMS_EPS)) * g_ref[...]).astype(BF16)
                h_scr[r, :] = hb
                hb_ref[r, :] = hb
                return c
            _grouped_loop(tm // 16, 4, slab, 0)

        def matmul_cols(cols):
            zc = _dot(h_scr[...], w_ref[:, cols])
            z_ref[:, cols] = zc
            return zc

        @pl.when(j == PA)
        def _():
            def colblock(ci, c):
                cols = _rows(ci, CB)
                za_scr[:, cols] = matmul_cols(cols)
                return c
            lax.fori_loop(0, D // CB, colblock, 0)

        @pl.when((j == PB) & (i == 0))
        def _():
            glu_scr[0:HALO, :] = jnp.zeros((HALO, D), F32)

        @pl.when((j == PB) & (i > 0))
        def _():
            glu_scr[0:HALO, :] = glu_scr[tm:tm + HALO, :]

        @pl.when(j == PB)
        def _():
            def colblock(ci, c):
                cols = _rows(ci, CB)
                glu_scr[HALO:HALO + tm, cols] = za_scr[:, cols] * _sigmoid(matmul_cols(cols))
                return c
            lax.fori_loop(0, D // CB, colblock, 0)

        for q in range(nq):
            @pl.when(j == PB + 1 + q)
            def _(q=q):
                def colblock(ci, c):
                    cols = _rows(ci, CB)
                    matmul_cols(cols)
                    _fill_shifts(sh_scr, glu_scr, cols, QR, q * QR)
                    for s in range(QR // 32):
                        acc = jnp.zeros((32, CB), F32)
                        for k in range(CONV_K):
                            acc = acc + _shifted(sh_scr, glu_scr, cols, s * 32 + k + SH, 32, q * QR) * cw_ref[k:k + 1, cols]
                        c0_ref[pl.ds(q * QR + s * 32, 32), cols] = acc + cb_ref[:, cols]
                    return c
                lax.fori_loop(0, D // CB, colblock, 0)

        @pl.when((j < PA) | (j > PB + nq))
        def _():
            z_ref[...] = _dot(h_scr[...], w_ref[...])

    tile = lambda: pl.BlockSpec((tm, D), lambda i, j: (i, 0))
    body, c_in, c_out, c_shapes, c_scr, _ = _host(comm, 5, 3, 4, grid, body)
    outs = pl.pallas_call(
        body, name="in_proj_conv", grid=grid,
        in_specs=[tile(), pl.BlockSpec((1, D), lambda i, j: (0, 0)),
                  pl.BlockSpec((None, D, D), lambda i, j: (l, 0, j)),
                  pl.BlockSpec((None, CONV_K, D), lambda i, j: (l, 0, 0)),
                  pl.BlockSpec((1, D), lambda i, j: (0, 0))] + c_in,
        out_specs=[pl.BlockSpec((tm, D), lambda i, j: (i, j)), tile(), tile()] + c_out,
        out_shape=[jax.ShapeDtypeStruct((S, N), F32), jax.ShapeDtypeStruct((S, D), BF16),
                   jax.ShapeDtypeStruct((S, D), F32)] + c_shapes,
        scratch_shapes=[pltpu.VMEM((tm, D), BF16), pltpu.VMEM((tm, D), F32), pltpu.VMEM((tm + HALO, D), F32),
                        pltpu.VMEM((7, QR + HALO, CB), F32)] + c_scr,
        compiler_params=_params("arbitrary", "arbitrary"),
    )(x, g, w_all, cw_all, cb, *(comm.ins if comm else []))
    return outs[0], outs[1], outs[2], outs[3:]


def _branch_b_fwd(c0, z, g, b):
    S, D = c0.shape
    ts = _tile(S, 512)

    def body(c0_ref, zg_ref, g_ref, b_ref, o_ref):
        def slab(si, c):
            r = _rows(si, 16)
            v = c0_ref[r, :]
            xc = v - _mean(v)
            c1 = (xc * lax.rsqrt(_mean(xc * xc) + LN_EPS)) * g_ref[...] + b_ref[...]
            o_ref[r, :] = (_silu(c1) * _silu(zg_ref[r, :])).astype(BF16)
            return c
        _grouped_loop(ts // 16, 4, slab, 0)

    return pl.pallas_call(
        body, name="branch_b_fwd", grid=(S // ts,),
        in_specs=[_tok(ts, D), _zcol(ts, D, 5), _row1(D), _row1(D)],
        out_specs=_tok(ts, D),
        out_shape=jax.ShapeDtypeStruct((S, D), BF16),
        compiler_params=_params("parallel"),
    )(c0, z, g, b)


def _kv_fwd(mem, mg, wkv_all, l):
    M, D = mem.shape

    def body(mem_ref, g_ref, w_ref, mn_ref, kv_ref, kvt_ref):
        m = mem_ref[...]
        mn = ((m * lax.rsqrt(_mean(m * m) + RMS_EPS)) * g_ref[...]).astype(BF16)
        mn_ref[...] = mn
        kv = _dot(mn, w_ref[...])
        kv_ref[...] = kv.astype(BF16)
        kvt_ref[...] = kv.T.astype(BF16)

    return pl.pallas_call(
        body, name="kv_fwd", grid=(1,),
        in_specs=[pl.BlockSpec((M, D), lambda i: (0, 0)), _row1(D),
                  pl.BlockSpec((None, D, 2 * D), lambda i: (l, 0, 0))],
        out_specs=[pl.BlockSpec((M, D), lambda i: (0, 0)), pl.BlockSpec((M, 2 * D), lambda i: (0, 0)),
                   pl.BlockSpec((2 * D, M), lambda i: (0, 0))],
        out_shape=[jax.ShapeDtypeStruct((M, D), BF16), jax.ShapeDtypeStruct((M, 2 * D), BF16),
                   jax.ShapeDtypeStruct((2 * D, M), BF16)],
        compiler_params=_params("arbitrary"),
    )(mem, mg, wkv_all)


def _softmax_rows(s):
    e = jnp.exp(s - jnp.max(s, axis=-1, keepdims=True))
    return e / jnp.sum(e, axis=-1, keepdims=True)


def _branch_c_fwd(z, kv, kvt):
    S = z.shape[0]
    M, D2 = kv.shape
    D = D2 // 2
    HD = D // HEADS
    scale = 1.0 / math.sqrt(HD)
    ts = _tile(S, 256)

    def body(q_ref, zg_ref, kt_ref, v_ref, o_ref):
        for h in range(HEADS):
            cols = slice(h * HD, (h + 1) * HD)
            p = _softmax_rows(_dot(q_ref[:, cols].astype(BF16), kt_ref[cols, :]) * scale)
            att = _dot(p.astype(BF16), v_ref[:, cols])
            o_ref[:, cols] = (att * _silu(zg_ref[:, cols])).astype(BF16)

    return pl.pallas_call(
        body, name="branch_c_fwd", grid=(S // ts,),
        in_specs=[_zcol(ts, D, 6), _zcol(ts, D, 7),
                  pl.BlockSpec((D, M), lambda i: (0, 0)), pl.BlockSpec((M, D), lambda i: (0, 1))],
        out_specs=_tok(ts, D),
        out_shape=jax.ShapeDtypeStruct((S, D), BF16),
        compiler_params=_params("parallel"),
    )(z, z, kvt, kv)


def _merge_fwd(brc, z, x, c0, cg, cb, lng, lnb, t_mix, bias_full, wb_all, wo_all, l):
    S, D = x.shape
    GD = D // GROUPS
    ts = _tile(S, 256)
    CB = 256

    def body(c_ref, zm0_ref, zm1_ref, zm2_ref, x_ref, c0_ref, zbg_ref, cg_ref, cb_ref, zu_ref, zv_ref, zag_ref,
             lng_ref, lnb_ref, t_ref, bias_ref, wb_ref, wo_ref, a_ref, b_ref, m_ref, xo_ref, v_scr):
        for s in range(ts // 16):
            r = pl.ds(s * 16, 16)
            v = c0_ref[r, :]
            xc = v - _mean(v)
            c1 = (xc * lax.rsqrt(_mean(xc * xc) + LN_EPS)) * cg_ref[...] + cb_ref[...]
            b_ref[r, :] = (_silu(c1) * _silu(zbg_ref[r, :])).astype(BF16)
        for ch in range(ts // CHUNK):
            for s in range(CHUNK // 16):
                vg = _gelu(zv_ref[pl.ds(ch * CHUNK + s * 16, 16), :])
                xc = vg - _mean(vg)
                y = xc * lax.rsqrt(_mean(xc * xc) + LN_EPS)
                v_scr[pl.ds(ch * CHUNK + s * 16, 16), :] = (y * lng_ref[...] + lnb_ref[...]).astype(BF16)
            rows = pl.ds(ch * CHUNK, CHUNK)
            for g in range(GROUPS):
                cols = slice(g * GD, (g + 1) * GD)
                sv = _dot(t_ref[g], v_scr[rows, cols]) + bias_ref[:, cols]
                u = _gelu(zu_ref[rows, cols])
                a_ref[rows, cols] = ((u * sv) * _silu(zag_ref[rows, cols])).astype(BF16)

        brs = (a_ref, b_ref, c_ref)
        zms = (zm0_ref, zm1_ref, zm2_ref)
        for cb in range(D // CB):
            cols = slice(cb * CB, (cb + 1) * CB)
            acc = None
            for n in range(N_BRANCHES):
                t = _sigmoid(zms[n][:, cols]) * _dot(brs[n][...], wb_ref[n, :, cols])
                acc = t if acc is None else acc + t
            m_ref[:, cols] = acc.astype(BF16)
        for cb in range(D // CB):
            cols = slice(cb * CB, (cb + 1) * CB)
            xo_ref[:, cols] = x_ref[:, cols] + _dot(m_ref[...], wo_ref[:, cols])

    return pl.pallas_call(
        body, name="merge_fwd", grid=(S // ts,),
        in_specs=[_tok(ts, D), _zcol(ts, D, 8), _zcol(ts, D, 9), _zcol(ts, D, 10), _tok(ts, D),
                  _tok(ts, D), _zcol(ts, D, 5), _row1(D), _row1(D),
                  _zcol(ts, D, 0), _zcol(ts, D, 1), _zcol(ts, D, 2), _row1(D), _row1(D),
                  pl.BlockSpec((GROUPS, CHUNK, CHUNK), lambda i: (0, 0, 0)),
                  pl.BlockSpec((CHUNK, D), lambda i: (0, 0)),
                  pl.BlockSpec((None, N_BRANCHES, D, D), lambda i: (l, 0, 0, 0)),
                  pl.BlockSpec((None, D, D), lambda i: (l, 0, 0))],
        out_specs=[_tok(ts, D), _tok(ts, D), _tok(ts, D), _tok(ts, D)],
        out_shape=[jax.ShapeDtypeStruct((S, D), BF16), jax.ShapeDtypeStruct((S, D), BF16),
                   jax.ShapeDtypeStruct((S, D), BF16), jax.ShapeDtypeStruct((S, D), F32)],
        scratch_shapes=[pltpu.VMEM((ts, D), BF16)],
        compiler_params=_params("parallel"),
    )(brc, z, z, z, x, c0, z, cg, cb, z, z, z, lng, lnb, t_mix, bias_full, wb_all, wo_all)


def _loss_bwd(x, target, fg):
    S, D = x.shape
    ts = _tile(S, 512)

    def body(x_ref, t_ref, g_ref, dx_ref, ls_ref, dg_ref):
        @pl.when(pl.program_id(0) == 0)
        def _():
            ls_ref[...] = jnp.zeros_like(ls_ref)
            dg_ref[...] = jnp.zeros_like(dg_ref)

        def slab(si, c):
            ls, dg = c
            r = _rows(si, 16)
            xs = x_ref[r, :]
            rs = lax.rsqrt(_mean(xs * xs) + RMS_EPS)
            n = xs * rs
            e = n * g_ref[...] - t_ref[r, :]
            dy = e * (1.0 / D)
            dn = dy * g_ref[...]
            dx_ref[r, :] = rs * (dn - n * _mean(dn * n))
            return ls + _colsum(e * e), dg + _colsum(dy * n)
        zero = jnp.zeros((1, D), F32)
        ls, dg = _grouped_loop(ts // 16, 4, slab, (zero, zero))
        ls_ref[...] += ls
        dg_ref[...] += dg

    return pl.pallas_call(
        body, name="loss_bwd", grid=(S // ts,),
        in_specs=[_tok(ts, D), _tok(ts, D), _row1(D)],
        out_specs=[_tok(ts, D), _row1(D), _row1(D)],
        out_shape=[jax.ShapeDtypeStruct((S, D), F32), jax.ShapeDtypeStruct((1, D), F32),
                   jax.ShapeDtypeStruct((1, D), F32)],
        compiler_params=_params("arbitrary"),
    )(x, target, fg)


def _merge_bwd(dxp, bra, brb, brc, z, c0, lng, lnb, wb_all, wo_all, l, comm=None):
    S, D = dxp.shape
    ts = _tile(S, 256)
    CB = 256

    def body(d_ref, a_ref, b_ref, c_ref, zm0_ref, zm1_ref, zm2_ref, c0_ref, zg_ref, g_ref, be_ref, wb_ref, wo_ref,
             dzm_ref, dproj_ref, dbr_ref, dc0_ref, dzb_ref, dg_ref, db_ref, dcb_ref, dxb_scr, dbb_scr):
        @pl.when(pl.program_id(0) == 0)
        def _():
            dg_ref[...] = jnp.zeros_like(dg_ref)
            db_ref[...] = jnp.zeros_like(db_ref)
            dcb_ref[...] = jnp.zeros_like(dcb_ref)

        brs = (a_ref, b_ref, c_ref)
        zms = (zm0_ref, zm1_ref, zm2_ref)
        dxb_scr[...] = d_ref[...].astype(BF16)
        for cb in range(D // CB):
            cols = slice(cb * CB, (cb + 1) * CB)
            dm = _dot_tb(dxb_scr[...], wo_ref[cols, :])
            for n in range(N_BRANCHES):
                proj = _dot(brs[n][...], wb_ref[n, :, cols])
                g = _sigmoid(zms[n][:, cols])
                dzm_ref[:, n * D + cb * CB:n * D + (cb + 1) * CB] = ((dm * proj) * (g * (1.0 - g))).astype(BF16)
                dproj_ref[n, :, cols] = (dm * g).astype(BF16)
        for cb in range(D // CB):
            cols = slice(cb * CB, (cb + 1) * CB)
            dbb_scr[:, cols] = _dot_tb(dproj_ref[1], wb_ref[1, cols, :])

        dg = db = dcb = jnp.zeros((1, D), F32)
        for s in range(ts // 16):
            r = pl.ds(s * 16, 16)
            v = c0_ref[r, :]
            xc = v - _mean(v)
            rstd = lax.rsqrt(_mean(xc * xc) + LN_EPS)
            cn = xc * rstd
            c1 = cn * g_ref[...] + be_ref[...]
            c2, dc2_dc1 = _silu_and_grad(c1)
            sg, dsg = _silu_and_grad(zg_ref[r, :])
            d = dbb_scr[r, :]
            dzb_ref[r, :] = ((d * c2) * dsg).astype(BF16)
            dc1 = (d * sg) * dc2_dc1
            dcn = dc1 * g_ref[...]
            dc0 = rstd * ((dcn - _mean(dcn)) - cn * _mean(dcn * cn))
            dc0_ref[r, :] = dc0
            dg, db, dcb = dg + _colsum(dc1 * cn), db + _colsum(dc1), dcb + _colsum(dc0)
        dg_ref[...] += dg
        db_ref[...] += db
        dcb_ref[...] += dcb

        for plane, n in enumerate((0, 2)):
            for cb in range(D // CB):
                cols = slice(cb * CB, (cb + 1) * CB)
                dbr_ref[plane, :, cols] = _dot_tb(dproj_ref[n], wb_ref[n, cols, :])

    tokn = lambda n: pl.BlockSpec((n, ts, D), lambda i: (0, i, 0))
    grid = (S // ts,)
    body, c_in, c_out, c_shapes, c_scr, _ = _host(comm, 13, 8, 2, grid, body)
    outs = pl.pallas_call(
        body, name="merge_bwd", grid=grid,
        in_specs=[_tok(ts, D), _tok(ts, D), _tok(ts, D), _tok(ts, D),
                  _zcol(ts, D, 8), _zcol(ts, D, 9), _zcol(ts, D, 10), _tok(ts, D), _zcol(ts, D, 5), _row1(D), _row1(D),
                  pl.BlockSpec((None, N_BRANCHES, D, D), lambda i: (l, 0, 0, 0)),
                  pl.BlockSpec((None, D, D), lambda i: (l, 0, 0))] + c_in,
        out_specs=[_tok(ts, 3 * D), tokn(N_BRANCHES), tokn(2), _tok(ts, D), _zcol(ts, D, 2),
                   _row1(D), _row1(D), _row1(D)] + c_out,
        out_shape=[jax.ShapeDtypeStruct((S, 3 * D), BF16), jax.ShapeDtypeStruct((N_BRANCHES, S, D), BF16),
                   jax.ShapeDtypeStruct((2, S, D), F32), jax.ShapeDtypeStruct((S, D), F32),
                   jax.ShapeDtypeStruct((S, 3 * D), BF16), jax.ShapeDtypeStruct((1, D), F32),
                   jax.ShapeDtypeStruct((1, D), F32), jax.ShapeDtypeStruct((1, D), F32)] + c_shapes,
        scratch_shapes=[pltpu.VMEM((ts, D), BF16), pltpu.VMEM((ts, D), F32)] + c_scr,
        compiler_params=_params("arbitrary"),
    )(dxp, bra, brb, brc, z, z, z, c0, z, lng, lnb, wb_all, wo_all, *(comm.ins if comm else []))
    return (*outs[:8], outs[8:])


def _branch_a_bwd(z, dbr, lng, lnb, t_mix, t_mix_t, bias_full, mask):
    S = z.shape[0]
    D = lng.shape[1]
    GD = D // GROUPS
    ts = _tile(S, 256)
    nt = S // ts

    def body(zu_ref, zv_ref, zg_ref, d_ref, lng_ref, lnb_ref, t_ref, tt_ref, bias_ref, mask_ref,
             dz_ref, dws_ref, dbs_ref, dlg_ref, dlb_ref,
             v_scr, vn_scr, rstd_scr, dv_scr, dsv_acc):
        i = pl.program_id(0)

        @pl.when(i == 0)
        def _():
            dws_ref[...] = jnp.zeros_like(dws_ref)
            dlg_ref[...] = jnp.zeros_like(dlg_ref)
            dlb_ref[...] = jnp.zeros_like(dlb_ref)
            dsv_acc[...] = jnp.zeros_like(dsv_acc)

        def chunk(ci, carry):
            c0 = pl.multiple_of(ci * CHUNK, CHUNK)

            def ln_fwd(si, c):
                r = pl.multiple_of(si * 16, 16)
                vg = _gelu(zv_ref[pl.ds(pl.multiple_of(c0 + r, 16), 16), :])
                xc = vg - _mean(vg)
                rstd = lax.rsqrt(_mean(xc * xc) + LN_EPS)
                vn = xc * rstd
                vn_scr[pl.ds(r, 16), :] = vn
                rstd_scr[pl.ds(r, 16), :] = rstd
                v_scr[pl.ds(r, 16), :] = (vn * lng_ref[...] + lnb_ref[...]).astype(BF16)
                return c
            _grouped_loop(CHUNK // 16, 4, ln_fwd, 0)

            for g in range(GROUPS):
                cols = slice(g * GD, (g + 1) * GD)
                rows = pl.ds(c0, CHUNK)
                vb = v_scr[:, cols]
                sv = _dot(t_ref[g], vb) + bias_ref[:, cols]
                u, du = _gelu_and_grad(zu_ref[rows, cols])
                sg, dsg = _silu_and_grad(zg_ref[rows, cols])
                d = d_ref[rows, cols]
                dsv = (d * u) * sg
                dz_ref[rows, g * GD:(g + 1) * GD] = (((d * sv) * sg) * du).astype(BF16)
                dz_ref[rows, 2 * D + g * GD:2 * D + (g + 1) * GD] = (((d * u) * sv) * dsg).astype(BF16)
                dsvb = dsv.astype(BF16)
                dws_ref[g] += _dot_tb(dsvb, vb)
                dv_scr[:, cols] = _dot(tt_ref[g], dsvb)
                dsv_acc[:, cols] += dsv

            def ln_bwd(si, c):
                dlg, dlb = c
                r = pl.multiple_of(si * 16, 16)
                rr = pl.ds(r, 16)
                zrows = pl.ds(pl.multiple_of(c0 + r, 16), 16)
                dv = dv_scr[rr, :]
                vn = vn_scr[rr, :]
                dvn = dv * lng_ref[...]
                dvg = rstd_scr[rr, :] * ((dvn - _mean(dvn)) - vn * _mean(dvn * vn))
                _, gg = _gelu_and_grad(zv_ref[zrows, :])
                dz_ref[zrows, D:2 * D] = (dvg * gg).astype(BF16)
                return dlg + _colsum(dv * vn), dlb + _colsum(dv)
            zero = jnp.zeros((1, D), F32)
            dlg, dlb = _grouped_loop(CHUNK // 16, 4, ln_bwd, (zero, zero))
            dlg_ref[...] += dlg
            dlb_ref[...] += dlb
            return carry
        lax.fori_loop(0, ts // CHUNK, chunk, 0)

        @pl.when(i == nt - 1)
        def _():
            for g in range(GROUPS):
                cols = slice(g * GD, (g + 1) * GD)
                dws_ref[g] = dws_ref[g] * mask_ref[...]
                dbs_ref[g:g + 1, :] = _colsum(dsv_acc[:, cols].T)

    c2 = lambda shape: pl.BlockSpec(shape, lambda i: (0, 0))
    c3 = lambda: pl.BlockSpec((GROUPS, CHUNK, CHUNK), lambda i: (0, 0, 0))
    return pl.pallas_call(
        body, name="branch_a_bwd", grid=(nt,),
        in_specs=[_zcol(ts, D, 0), _zcol(ts, D, 1), _zcol(ts, D, 2), _plane(ts, D, 0), _row1(D), _row1(D),
                  c3(), c3(), c2((CHUNK, D)), c2((CHUNK, CHUNK))],
        out_specs=[_tok(ts, 3 * D), c3(), c2((GROUPS, CHUNK)), _row1(D), _row1(D)],
        out_shape=[jax.ShapeDtypeStruct((S, 3 * D), BF16), jax.ShapeDtypeStruct((GROUPS, CHUNK, CHUNK), F32),
                   jax.ShapeDtypeStruct((GROUPS, CHUNK), F32), jax.ShapeDtypeStruct((1, D), F32),
                   jax.ShapeDtypeStruct((1, D), F32)],
        scratch_shapes=[pltpu.VMEM((CHUNK, D), BF16), pltpu.VMEM((CHUNK, D), F32), pltpu.VMEM((CHUNK, 1), F32),
                        pltpu.VMEM((CHUNK, D), F32), pltpu.VMEM((CHUNK, D), F32)],
        compiler_params=_params("arbitrary"),
    )(z, z, z, dbr, lng, lnb, t_mix, t_mix_t, bias_full, mask)


def _branch_b_bwd(c0, z, dbr, g, b, comm=None):
    S, D = c0.shape
    ts = _tile(S, 512)

    def body(c0_ref, zg_ref, d_ref, g_ref, b_ref, dc0_ref, dz_ref, dg_ref, db_ref, dcb_ref):
        @pl.when(pl.program_id(0) == 0)
        def _():
            dg_ref[...] = jnp.zeros_like(dg_ref)
            db_ref[...] = jnp.zeros_like(db_ref)
            dcb_ref[...] = jnp.zeros_like(dcb_ref)

        def slab(si, c):
            dg, db, dcb = c
            r = _rows(si, 16)
            v = c0_ref[r, :]
            xc = v - _mean(v)
            rstd = lax.rsqrt(_mean(xc * xc) + LN_EPS)
            cn = xc * rstd
            c1 = cn * g_ref[...] + b_ref[...]
            c2, dc2_dc1 = _silu_and_grad(c1)
            sg, dsg = _silu_and_grad(zg_ref[r, :])
            d = d_ref[r, :]
            dz_ref[r, :] = ((d * c2) * dsg).astype(BF16)
            dc1 = (d * sg) * dc2_dc1
            dcn = dc1 * g_ref[...]
            dc0 = rstd * ((dcn - _mean(dcn)) - cn * _mean(dcn * cn))
            dc0_ref[r, :] = dc0
            return dg + _colsum(dc1 * cn), db + _colsum(dc1), dcb + _colsum(dc0)
        zero = jnp.zeros((1, D), F32)
        dg, db, dcb = _grouped_loop(ts // 16, 4, slab, (zero, zero, zero))
        dg_ref[...] += dg
        db_ref[...] += db
        dcb_ref[...] += dcb

    grid = (S // ts,)
    body, c_in, c_out, c_shapes, c_scr, _ = _host(comm, 5, 5, 0, grid, body)
    outs = pl.pallas_call(
        body, name="branch_b_bwd", grid=grid,
        in_specs=[_tok(ts, D), _zcol(ts, D, 5), _plane(ts, D, 1), _row1(D), _row1(D)] + c_in,
        out_specs=[_tok(ts, D), _zcol(ts, D, 2), _row1(D), _row1(D), _row1(D)] + c_out,
        out_shape=[jax.ShapeDtypeStruct((S, D), F32), jax.ShapeDtypeStruct((S, 3 * D), BF16),
                   jax.ShapeDtypeStruct((1, D), F32), jax.ShapeDtypeStruct((1, D), F32),
                   jax.ShapeDtypeStruct((1, D), F32)] + c_shapes,
        scratch_shapes=c_scr,
        compiler_params=_params("arbitrary"),
    )(c0, z, dbr, g, b, *(comm.ins if comm else []))
    return (*outs[:5], outs[5:])


def _conv_bwd(dc0, z, w_all, dz_b, l, comm=None):
    S, D = dc0.shape
    ts = _tile(S, 256)
    nt = S // ts
    CB = 256
    hb = ts // HALO
    nh = S // HALO

    def body(dc_ref, dcn_ref, za_ref, zb_ref, w_ref, dzin_ref, dz_ref, dw_ref, dc_scr, dw_acc, dsh_scr):
        del dzin_ref
        i = pl.program_id(0)

        @pl.when(i == 0)
        def _():
            dw_acc[...] = jnp.zeros_like(dw_acc)

        dc_scr[ts:ts + HALO, :] = jnp.where(i == nt - 1, 0.0, dcn_ref[...])

        def fill(si, c):
            r = _rows(si, 32)
            dc_scr[r, :] = dc_ref[r, :]
            return c
        _grouped_loop(ts // 32, 2, fill, 0)

        def colblock(ci, c):
            cols = _rows(ci, CB)
            cols_b = pl.ds(pl.multiple_of(D + ci * CB, CB), CB)
            _fill_shifts(dsh_scr, dc_scr, cols, ts)
            for s in range(ts // 32):
                rows = pl.ds(s * 32, 32)
                za = za_ref[rows, cols]
                sg = _sigmoid(zb_ref[rows, cols])
                glu = za * sg
                acc = jnp.zeros((32, CB), F32)
                for k in range(CONV_K):
                    dcs = _shifted(dsh_scr, dc_scr, cols, s * 32 + CONV_K - 1 - k, 32)
                    prod = glu * dcs
                    dw_acc[k, :, cols] += (prod[0:8] + prod[8:16]) + (prod[16:24] + prod[24:32])
                    acc = acc + dcs * w_ref[k:k + 1, cols]
                dz_ref[rows, cols] = (acc * sg).astype(BF16)
                dz_ref[rows, cols_b] = ((acc * za) * (sg * (1.0 - sg))).astype(BF16)
            return c
        lax.fori_loop(0, D // CB, colblock, 0)

        @pl.when(i == nt - 1)
        def _():
            for k in range(CONV_K):
                dw_ref[k:k + 1, :] = _colsum(dw_acc[k])

    grid = (nt,)
    body, c_in, c_out, c_shapes, c_scr, _ = _host(comm, 6, 2, 3, grid, body)
    outs = pl.pallas_call(
        body, name="conv_bwd", grid=grid,
        in_specs=[_tok(ts, D), pl.BlockSpec((HALO, D), lambda i: (jnp.minimum((i + 1) * hb, nh - 1), 0)),
                  _zcol(ts, D, 3), _zcol(ts, D, 4),
                  pl.BlockSpec((None, CONV_K, D), lambda i: (l, 0, 0)), ANY] + c_in,
        out_specs=[_tok(ts, 2 * D), pl.BlockSpec((CONV_K, D), lambda i: (0, 0))] + c_out,
        out_shape=[jax.ShapeDtypeStruct((S, 3 * D), BF16), jax.ShapeDtypeStruct((CONV_K, D), F32)] + c_shapes,
        scratch_shapes=[pltpu.VMEM((ts + HALO, D), F32), pltpu.VMEM((CONV_K, 8, D), F32),
                        pltpu.VMEM((7, ts + HALO, CB), F32)] + c_scr,
        input_output_aliases={5: 0},
        compiler_params=_params("arbitrary"),
    )(dc0, dc0, z, z, w_all, dz_b, *(comm.ins if comm else []))
    return outs[0], outs[1], outs[2:]


def _branch_c_bwd(z, dbr, kv, kvt):
    S = z.shape[0]
    M, D2 = kv.shape
    D = D2 // 2
    HD = D // HEADS
    scale = 1.0 / math.sqrt(HD)
    ts = _tile(S, 256)

    def body(q_ref, zg_ref, d_ref, kt_ref, vt_ref, k_ref, v_ref, dz_ref, dk_ref, dv_ref):
        @pl.when(pl.program_id(0) == 0)
        def _():
            dk_ref[...] = jnp.zeros_like(dk_ref)
            dv_ref[...] = jnp.zeros_like(dv_ref)

        for h in range(HEADS):
            cols = slice(h * HD, (h + 1) * HD)
            qb = q_ref[:, cols].astype(BF16)
            p = _softmax_rows(_dot(qb, kt_ref[cols, :]) * scale)
            pb = p.astype(BF16)
            att = _dot(pb, v_ref[:, cols])
            sg, dsg = _silu_and_grad(zg_ref[:, cols])
            d = d_ref[:, cols]
            dz_ref[:, D + h * HD:D + (h + 1) * HD] = ((d * att) * dsg).astype(BF16)
            datt = (d * sg).astype(BF16)
            dp = _dot(datt, vt_ref[cols, :])
            dv_ref[:, cols] += _dot_ta(pb, datt)
            ds = ((p * (dp - jnp.sum(dp * p, axis=-1, keepdims=True))) * scale).astype(BF16)
            dz_ref[:, cols] = _dot(ds, k_ref[:, cols]).astype(BF16)
            dk_ref[:, cols] += _dot_ta(ds, qb)

    return pl.pallas_call(
        body, name="branch_c_bwd", grid=(S // ts,),
        in_specs=[_zcol(ts, D, 6), _zcol(ts, D, 7), _plane(ts, D, 1),
                  pl.BlockSpec((D, M), lambda i: (0, 0)), pl.BlockSpec((D, M), lambda i: (1, 0)),
                  pl.BlockSpec((M, D), lambda i: (0, 0)), pl.BlockSpec((M, D), lambda i: (0, 1))],
        out_specs=[_tok(ts, 2 * D), pl.BlockSpec((M, D), lambda i: (0, 0)), pl.BlockSpec((M, D), lambda i: (0, 0))],
        out_shape=[jax.ShapeDtypeStruct((S, 2 * D), BF16), jax.ShapeDtypeStruct((M, D), F32),
                   jax.ShapeDtypeStruct((M, D), F32)],
        compiler_params=_params("arbitrary"),
    )(z, z, dbr, kvt, kvt, kv, kv)


def _kv_bwd(dk, dv, mn, mem, wkv_all, l, prev):
    M, D = mem.shape

    def body(*refs):
        dk_ref, dv_ref, mn_ref, mem_ref, w_ref = refs[:5]
        dw_ref, dg_ref = refs[-2:]
        dkb = dk_ref[...].astype(BF16)
        dvb = dv_ref[...].astype(BF16)
        dw_ref[:, 0:D] = _dot_ta(mn_ref[...], dkb)
        dw_ref[:, D:2 * D] = _dot_ta(mn_ref[...], dvb)
        dmn = _dot_tb(dkb, w_ref[:, 0:D]) + _dot_tb(dvb, w_ref[:, D:2 * D])
        m = mem_ref[...]
        dg_ref[...] = _colsum(dmn * (m * lax.rsqrt(_mean(m * m) + RMS_EPS)))

    full = lambda shape: pl.BlockSpec(shape, lambda i: (0, 0))
    in_specs = [full((M, D)), full((M, D)), full((M, D)), full((M, D)),
                pl.BlockSpec((None, D, 2 * D), lambda i: (l, 0, 0))]
    args = [dk, dv, mn, mem, wkv_all]
    aliases = {}
    if prev is not None:
        in_specs.append(ANY)
        args.append(prev)
        aliases = {5: 0}
    return pl.pallas_call(
        body, name="kv_bwd", grid=(1,),
        in_specs=in_specs,
        out_specs=[pl.BlockSpec((None, D, 2 * D), lambda i: (l, 0, 0)), _row1(D)],
        out_shape=[jax.ShapeDtypeStruct((wkv_all.shape[0], D, 2 * D), F32), jax.ShapeDtypeStruct((1, D), F32)],
        input_output_aliases=aliases,
        compiler_params=_params("arbitrary"),
    )(*args)


def _in_proj_bwd(pieces, w_all, x, g, dxp, l, comm=None, part=(0, 1), prev=None):
    S, D = x.shape
    tm = _tile(S, 1024)
    n_p = len(pieces)
    n_i = (S // tm) // part[1]
    i0 = part[0] * n_i
    n_in = n_p + 4 + (prev is not None)

    def body(*refs):
        dz_refs = refs[:n_p]
        w_ref, x_hbm, g_ref, dxp_hbm = refs[n_p:n_p + 4]
        dx_ref, dg_ref, acc, x_ref, dxp_ref, sem = refs[n_in:]
        i = pl.program_id(0)
        k = pl.program_id(1)
        rows = pl.ds(pl.multiple_of((i + i0) * tm, tm), tm)
        fetch = [pltpu.make_async_copy(x_hbm.at[rows, :], x_ref, sem.at[0]),
                 pltpu.make_async_copy(dxp_hbm.at[rows, :], dxp_ref, sem.at[1])]

        @pl.when(k == N_PLANES - 2)
        def _():
            for cp in fetch:
                cp.start()

        @pl.when(k == 0)
        def _():
            acc[...] = jnp.zeros_like(acc)

        @pl.when((i == 0) & (k == 0))
        def _():
            dg_ref[...] = jnp.zeros_like(dg_ref)

        for (_, lo, n), r in zip(pieces, dz_refs):
            @pl.when((k >= lo) & (k < lo + n))
            def _(r=r):
                acc[...] += _dot_tb(r[...], w_ref[...])

        @pl.when(k == N_PLANES - 1)
        def _():
            for cp in fetch:
                cp.wait()

            def slab(si, c):
                rr = _rows(si, 16)
                xs = x_ref[rr, :]
                rs = lax.rsqrt(_mean(xs * xs) + RMS_EPS)
                n = xs * rs
                dh = acc[rr, :]
                dn = dh * g_ref[...]
                dx_ref[rr, :] = rs * (dn - n * _mean(dn * n)) + dxp_ref[rr, :]
                return c + _colsum(dh * n)
            dg_ref[...] += _grouped_loop(tm // 16, 4, slab, jnp.zeros((1, D), F32))

    def piece_spec(lo, n):
        return pl.BlockSpec((tm, D), lambda i, k: (i + i0, jnp.clip(k - lo, 0, n - 1)))

    tokk = lambda: pl.BlockSpec((tm, D), lambda i, k: (i + i0, 0))
    rowk = lambda: pl.BlockSpec((1, D), lambda i, k: (0, 0))
    grid = (n_i, N_PLANES)
    body, c_in, c_out, c_shapes, c_scr, _ = _host(comm, n_in, 2, 4, grid, body)
    more_in, more_args, aliases = ([ANY], [prev], {n_in - 1: 0}) if prev is not None else ([], [], {})
    outs = pl.pallas_call(
        body, name="in_proj_bwd", grid=grid,
        in_specs=[piece_spec(lo, n) for _, lo, n in pieces]
        + [pl.BlockSpec((None, D, D), lambda i, k: (l, 0, k)), ANY, rowk(), ANY] + more_in + c_in,
        out_specs=[tokk(), rowk()] + c_out,
        out_shape=[jax.ShapeDtypeStruct((S, D), F32), jax.ShapeDtypeStruct((1, D), F32)] + c_shapes,
        scratch_shapes=[pltpu.VMEM((tm, D), F32), pltpu.VMEM((tm, D), F32), pltpu.VMEM((tm, D), F32),
                        pltpu.SemaphoreType.DMA((2,))] + c_scr,
        input_output_aliases=aliases,
        compiler_params=_params("arbitrary", "arbitrary"),
    )(*[p for p, _, _ in pieces], w_all, x, g, dxp, *more_args, *(comm.ins if comm else []))
    return outs[0], outs[1], outs[2:]


def _wgrad(a, b, out_shape, plane, col_off, prev, a_plane=None, b_plane=None):
    S = a.shape[-2]
    K1 = a.shape[-1]
    nb_cols = b.shape[-1]
    ts = _tile(S, 2048)
    tn = _tile(nb_cols, 1024)
    col_off = col_off // tn

    def body(*refs):
        a_ref, b_ref = refs[:2]
        o_ref = refs[-1]
        k = pl.program_id(1)
        prod = _dot_ta(a_ref[...].astype(BF16), b_ref[...].astype(BF16))

        @pl.when(k == 0)
        def _():
            o_ref[...] = prod

        @pl.when(k > 0)
        def _():
            o_ref[...] += prod

    if a_plane is None:
        a_spec = pl.BlockSpec((ts, K1), lambda j, k: (k, 0))
    else:
        a_spec = pl.BlockSpec((None, ts, K1), lambda j, k: (a_plane, k, 0))
    if b_plane is None:
        b_spec = pl.BlockSpec((ts, tn), lambda j, k: (k, j))
    else:
        b_spec = pl.BlockSpec((None, ts, tn), lambda j, k: (b_plane, k, j))
    in_specs, args, aliases = [a_spec, b_spec], [a, b], {}
    if prev is not None:
        in_specs.append(ANY)
        args.append(prev)
        aliases = {2: 0}
    return pl.pallas_call(
        body, name="wgrad", grid=(nb_cols // tn, S // ts),
        in_specs=in_specs,
        out_specs=pl.BlockSpec((None, K1, tn), lambda j, k: (plane, 0, col_off + j)),
        out_shape=jax.ShapeDtypeStruct(out_shape, F32),
        input_output_aliases=aliases,
        compiler_params=_params("parallel", "arbitrary"),
    )(*args)


def _position():
    x, y, c = lax.axis_index("x"), lax.axis_index("y"), lax.axis_index("c")
    chips = [(1 - x, y), (x, 1 - y), (1 - x, 1 - y)]
    return x, y, c, chips


def _sub(ref, axis, start, size):
    idx = [slice(None)] * len(ref.shape)
    idx[axis] = pl.ds(start, size)
    return ref.at[tuple(idx)]


def _dma(src, dst, ssem, rsem, to):
    return pltpu.make_async_remote_copy(src_ref=src, dst_ref=dst, send_sem=ssem, recv_sem=rsem,
                                        device_id=to, device_id_type=MESH)


def _dma_sems(*counts):
    return [pltpu.SemaphoreType.DMA((n,)) for n in counts]


def _both(a, b):
    na, oa, sa = len(a.ins), len(a.out_shapes), len(a.sem_shapes)
    phases = [(w, lambda i, o, s, fn=fn: fn(i[:na], o[:oa], s[:sa])) for w, fn in a.phases]
    phases += [(w, lambda i, o, s, fn=fn: fn(i[na:], o[oa:], s[sa:])) for w, fn in b.phases]
    return _Hosted(a.ins + b.ins, list(a.out_shapes) + list(b.out_shapes), list(a.sem_shapes) + list(b.sem_shapes), phases)


def _run_alone(comm, name):
    ci, co = len(comm.ins), len(comm.out_shapes)

    def body(*refs):
        for _, fn in comm.phases:
            fn(refs[:ci], refs[ci:ci + co], refs[ci + co:])

    return pl.pallas_call(
        body, name=name, in_specs=[ANY] * ci, out_specs=[ANY] * co, out_shape=list(comm.out_shapes),
        scratch_shapes=list(comm.sem_shapes),
    )(*comm.ins)


def _gather_comm(shards, layers, axes, start_at="first"):
    n = len(shards)
    out_shapes = []
    for s, (ba, _) in zip(shards, axes):
        shp = list(s.shape[1:])
        shp[ba] *= N_CHIPS
        out_shapes.append(jax.ShapeDtypeStruct(tuple(shp), s.dtype))

    def half(ref, t, h):
        ha = axes[t][1]
        if ha is None:
            return ref
        hs = shards[t].shape[1 + ha] // 2
        return _sub(ref, ha, h * hs, hs)

    def block(outs, t, blk):
        ba = axes[t][0]
        bs = shards[t].shape[1 + ba]
        return _sub(outs[t], ba, blk * bs, bs)

    def start(ins, outs, sems):
        s_own, r_own, s_ici, r_ici, _, _ = sems
        x, y, c, chips = _position()
        j = 2 * x + y
        for t in range(n):
            _dma(ins[t].at[layers[t]], block(outs, t, j), s_own.at[t], r_own.at[t], (x, y, 1 - c)).start()
            for k, (px, py) in enumerate(chips):
                _dma(half(ins[t].at[layers[t]], t, c), half(block(outs, t, j), t, c),
                     s_ici.at[3 * t + k], r_ici.at[3 * t + k], (px, py, c)).start()

    def forward(ins, outs, sems):
        _, _, s_ici, r_ici, s_fwd, r_fwd = sems
        x, y, c, chips = _position()
        for t in range(n):
            if axes[t][1] is None:
                continue
            for k, (px, py) in enumerate(chips):
                got = half(block(outs, t, 2 * px + py), t, c)
                _dma(got, got, s_ici.at[3 * t + k], r_ici.at[3 * t + k], (px, py, c)).wait_recv()
                _dma(got, got, s_fwd.at[3 * t + k], r_fwd.at[3 * t + k], (x, y, 1 - c)).start()

    def finish(ins, outs, sems):
        s_own, r_own, s_ici, r_ici, s_fwd, r_fwd = sems
        x, y, c, chips = _position()
        j = 2 * x + y
        sib = (x, y, 1 - c)
        for t in range(n):
            for k, (px, py) in enumerate(chips):
                i = 3 * t + k
                sent = half(block(outs, t, j), t, c)
                _dma(sent, sent, s_ici.at[i], r_ici.at[i], (px, py, c)).wait_send()
                theirs = block(outs, t, 2 * px + py)
                if axes[t][1] is None:
                    _dma(theirs, theirs, s_ici.at[i], r_ici.at[i], (px, py, c)).wait_recv()
                else:
                    got, other = half(theirs, t, c), half(theirs, t, 1 - c)
                    _dma(got, got, s_fwd.at[i], r_fwd.at[i], sib).wait_send()
                    _dma(other, other, s_fwd.at[i], r_fwd.at[i], sib).wait_recv()
            _dma(ins[t].at[layers[t]], block(outs, t, j), s_own.at[t], r_own.at[t], sib).wait()

    return _Hosted(list(shards), out_shapes, _dma_sems(n, n, 3 * n, 3 * n, 3 * n, 3 * n),
                   [(start_at, start), ("late", forward), ("last", finish)])


def _halved(shape, axis):
    shp = list(shape)
    shp[axis] //= 2
    return tuple(shp)


def _pair_comm(grads, axes):
    n = len(grads)

    def copies(ins, outs, sems):
        x, y, c, _ = _position()
        cps = []
        for t in range(n):
            ha = axes[t][1]
            hs = grads[t].shape[ha] // 2
            cps.append(_dma(_sub(ins[t], ha, (1 - c) * hs, hs), outs[t], sems[0].at[t], sems[1].at[t], (x, y, 1 - c)))
        return cps

    def start(ins, outs, sems):
        for cp in copies(ins, outs, sems):
            cp.start()

    def finish(ins, outs, sems):
        for cp in copies(ins, outs, sems):
            cp.wait()

    return _Hosted(list(grads), [jax.ShapeDtypeStruct(_halved(g.shape, ax[1]), g.dtype) for g, ax in zip(grads, axes)],
                   _dma_sems(n, n), [("first", start), ("last", finish)])


def _block_comm(sums, axes):
    n = len(sums)
    out_shapes = []
    for s, (ba, _) in zip(sums, axes):
        shp = list(s.shape)
        shp[ba] //= N_CHIPS
        out_shapes.append(jax.ShapeDtypeStruct((3,) + tuple(shp), s.dtype))

    def copies(ins, outs, sems):
        x, y, c, chips = _position()
        cps = []
        for t in range(n):
            ba = axes[t][0]
            bs = sums[t].shape[ba] // N_CHIPS
            for k, (px, py) in enumerate(chips):
                cps.append(_dma(_sub(ins[t], ba, (2 * px + py) * bs, bs), outs[t].at[k],
                                sems[0].at[3 * t + k], sems[1].at[3 * t + k], (px, py, c)))
        return cps

    def start(ins, outs, sems):
        for cp in copies(ins, outs, sems):
            cp.start()

    def finish(ins, outs, sems):
        for cp in copies(ins, outs, sems):
            cp.wait()

    return _Hosted(list(sums), out_shapes, _dma_sems(3 * n, 3 * n), [("first", start), ("last", finish)])


def _sibling_share(shards, axes):
    n = len(shards)

    def body(*refs):
        outs = refs[n:2 * n]
        ssem, rsem = refs[2 * n:]
        x, y, c, _ = _position()
        cps = []
        for t in range(n):
            ha = axes[t][1]
            hs = shards[t].shape[ha] // 2
            mine = _sub(outs[t], ha, c * hs, hs)
            cps.append(_dma(mine, mine, ssem.at[t], rsem.at[t], (x, y, 1 - c)))
        for cp in cps:
            cp.start()
        for cp in cps:
            cp.wait()

    return pl.pallas_call(
        body, name="sibling_share",
        in_specs=[ANY] * n, out_specs=[ANY] * n,
        out_shape=[jax.ShapeDtypeStruct(s.shape, s.dtype) for s in shards],
        input_output_aliases={t: t for t in range(n)},
        scratch_shapes=_dma_sems(n, n),
    )(*shards)


def _small_allreduce(p):
    R, C = p.shape

    def body(p_ref, o_ref, sib_buf, chip_sums, ssem, rsem):
        x, y, c, chips = _position()
        j = 2 * x + y
        sib = pltpu.make_async_remote_copy(
            src_ref=p_ref, dst_ref=sib_buf, send_sem=ssem.at[0], recv_sem=rsem.at[0],
            device_id=(x, y, 1 - c), device_id_type=MESH)
        sib.start()
        sib.wait()
        chip_sums[j] = p_ref[...] + sib_buf[...]
        cps = [pltpu.make_async_remote_copy(
            src_ref=chip_sums.at[j], dst_ref=chip_sums.at[j], send_sem=ssem.at[1 + k], recv_sem=rsem.at[1 + k],
            device_id=(px, py, c), device_id_type=MESH) for k, (px, py) in enumerate(chips)]
        for cp in cps:
            cp.start()
        for k, (px, py) in enumerate(chips):
            pltpu.make_async_remote_copy(
                src_ref=chip_sums.at[j], dst_ref=chip_sums.at[2 * px + py],
                send_sem=ssem.at[1 + k], recv_sem=rsem.at[1 + k],
                device_id=(px, py, c), device_id_type=MESH).wait()
        o_ref[...] = ((chip_sums[0] + chip_sums[1]) + chip_sums[2]) + chip_sums[3]

    vm = pl.BlockSpec(memory_space=pltpu.VMEM)
    return pl.pallas_call(
        body, name="small_allreduce",
        in_specs=[vm], out_specs=vm, out_shape=jax.ShapeDtypeStruct((R, C), F32),
        scratch_shapes=[pltpu.VMEM((R, C), F32), pltpu.VMEM((N_CHIPS, R, C), F32),
                        pltpu.SemaphoreType.DMA((4,)), pltpu.SemaphoreType.DMA((4,))],
        compiler_params=pltpu.CompilerParams(vmem_limit_bytes=VMEM_LIMIT_BYTES),
    )(p)


def _add_half(g, recv, half_axis, pos):
    B, R, C = recv.shape
    tr = _tile(R, 256)
    tc = _tile(C, 2816)
    nr, nc = R // tr, C // tc
    if half_axis == 1:
        g_map = lambda b, i, j, pos_ref: (b, pos_ref[0] * nr + i, j)
    else:
        g_map = lambda b, i, j, pos_ref: (b, i, pos_ref[0] * nc + j)

    def body(pos_ref, g_ref, r_ref, o_ref, ob_ref):
        del pos_ref
        s = g_ref[...] + r_ref[...]
        o_ref[...] = s
        ob_ref[...] = s.astype(BF16)

    spec = lambda: pl.BlockSpec((None, tr, tc), lambda b, i, j, pos_ref: (b, i, j))
    return pl.pallas_call(
        body, name="add_half",
        grid_spec=pltpu.PrefetchScalarGridSpec(
            num_scalar_prefetch=1, grid=(B, nr, nc),
            in_specs=[pl.BlockSpec((None, tr, tc), g_map), spec()],
            out_specs=[spec(), spec()]),
        out_shape=[jax.ShapeDtypeStruct((B, R, C), F32), jax.ShapeDtypeStruct((B, R, C), BF16)],
        compiler_params=_params("parallel", "parallel", "parallel"),
    )(pos, g, recv)


def _sum_half(own, recv, axes, pos):
    _, B, rs, cs = recv.shape
    ba, ha = axes
    tr = _tile(rs, 128)
    nr = rs // tr

    def body(pos_ref, o_ref, r0_ref, r1_ref, r2_ref, out_ref):
        del pos_ref
        out_ref[...] = (((o_ref[...] + r0_ref[...].astype(F32)) + r1_ref[...].astype(F32))
                        + r2_ref[...].astype(F32))

    if ba == 2:
        own_spec = pl.BlockSpec((None, tr, cs), lambda b, i, pos_ref: (b, i, pos_ref[1]))
    else:
        own_spec = pl.BlockSpec((None, tr, cs), lambda b, i, pos_ref: (b, pos_ref[1] * nr + i, 0))
    if ha == 1:
        out_shape, out_map = (B, 2 * rs, cs), (lambda b, i, pos_ref: (b, pos_ref[0] * nr + i, 0))
    else:
        out_shape, out_map = (B, rs, 2 * cs), (lambda b, i, pos_ref: (b, i, pos_ref[0]))
    rspec = lambda k: pl.BlockSpec((None, None, tr, cs), lambda b, i, pos_ref: (k, b, i, 0))
    return pl.pallas_call(
        body, name="sum_half",
        grid_spec=pltpu.PrefetchScalarGridSpec(
            num_scalar_prefetch=1, grid=(B, nr),
            in_specs=[own_spec, rspec(0), rspec(1), rspec(2)],
            out_specs=pl.BlockSpec((None, tr, cs), out_map)),
        out_shape=jax.ShapeDtypeStruct(out_shape, F32),
        compiler_params=_params("parallel", "parallel"),
    )(pos, own, recv, recv, recv)


def _adamw(w, g, m, v):
    shape = w.shape
    C = shape[-1]
    R = w.size // C
    tr = R
    for cand in (512, 256, 128, 64, 32, 16, 8):
        if R % cand == 0 and cand * C * 4 <= (1 << 20):
            tr = cand
            break
    bc1 = 1.0 - ADAM_B1 ** ADAM_STEP
    bc2 = 1.0 - ADAM_B2 ** ADAM_STEP

    def body(w_ref, g_ref, m_ref, v_ref, d_ref, mo_ref, vo_ref):
        gg = g_ref[...]
        mn = ADAM_B1 * m_ref[...] + (1.0 - ADAM_B1) * gg
        vn = ADAM_B2 * v_ref[...] + (1.0 - ADAM_B2) * (gg * gg)
        mo_ref[...] = mn
        vo_ref[...] = vn
        d_ref[...] = -ADAM_LR * ((mn / bc1) / (jnp.sqrt(vn / bc2) + ADAM_EPS) + ADAM_WD * w_ref[...])

    spec = lambda: pl.BlockSpec((tr, C), lambda i: (i, 0))
    outs = pl.pallas_call(
        body, name="adamw", grid=(R // tr,),
        in_specs=[spec()] * 4, out_specs=[spec()] * 3,
        out_shape=[jax.ShapeDtypeStruct((R, C), F32)] * 3,
        compiler_params=_params("parallel"),
    )(*[a.reshape(R, C) for a in (w, g, m, v)])
    return tuple(o.reshape(shape) for o in outs)


def kernel(x, mem, norm_g, mem_norm_g, w_in, gmlp_ln_g, gmlp_ln_b, w_s, b_s, conv_w, conv_b, conv_ln_g, conv_ln_b, w_kv, w_branch, w_out, final_norm_g, loss_target, m_norm_g, m_mem_norm_g, m_w_in, m_gmlp_ln_g, m_gmlp_ln_b, m_w_s, m_b_s, m_conv_w, m_conv_b, m_conv_ln_g, m_conv_ln_b, m_w_kv, m_w_branch, m_w_out, m_final_norm_g, v_norm_g, v_mem_norm_g, v_w_in, v_gmlp_ln_g, v_gmlp_ln_b, v_w_s, v_b_s, v_conv_w, v_conv_b, v_conv_ln_g, v_conv_ln_b, v_w_kv, v_w_branch, v_w_out, v_final_norm_g):
    xs, mems, tgt = x[0], mem[0], loss_target[0]
    S, D = xs.shape
    assert D // GROUPS == LANES and S % CHUNK == 0 and w_s.shape[-1] == CHUNK
    L = N_LAYERS

    assert L == 2
    shards = [w_in.astype(BF16)[:, None], w_kv.astype(BF16)[:, None], w_branch.astype(BF16), w_out.astype(BF16)[:, None],
              conv_w[:, None]]
    gather_axes = [(2, 1), (2, 1), (1, 2), (1, 2), (2, None)]
    reduce_axes = gather_axes[:4]
    win0, cw0 = _run_alone(_gather_comm([shards[0], shards[4]], [0, 0], [gather_axes[0], gather_axes[4]]), "gather_first")
    gather_rest = _gather_comm(shards[1:4] + shards, [0] * 3 + [1] * 5, gather_axes[1:4] + gather_axes)

    tri = jnp.tril(jnp.ones((CHUNK, CHUNK), F32))
    t_mix = (w_s * tri).astype(BF16)
    t_mix_t = jnp.swapaxes(t_mix, -1, -2)
    bias_full = jnp.repeat(jnp.swapaxes(b_s, -1, -2), D // GROUPS, axis=-1)
    row = lambda a, l: a[l][None, :]
    xi, yi, ci = lax.axis_index("x"), lax.axis_index("y"), lax.axis_index("c")
    pos = jnp.stack([ci, 2 * xi + yi]).astype(jnp.int32)

    saved = []
    h = xs
    weights = [None] * L
    for l in range(L):
        if l == 0:
            z, hb, c0, fetched = _in_proj_conv(h, row(norm_g, l), win0, cw0, row(conv_b, l), 0, gather_rest)
            weights = [[win0] + list(fetched[:3]) + [cw0], list(fetched[3:])]
        else:
            z, hb, c0, _ = _in_proj_conv(h, row(norm_g, l), weights[l][0], weights[l][4], row(conv_b, l), 0)
        win, wkv, wb, wo, cw = weights[l]
        mn, kv, kvt = _kv_fwd(mems, row(mem_norm_g, l), wkv, 0)
        brc = _branch_c_fwd(z, kv, kvt)
        bra, brb, merged, h_next = _merge_fwd(brc, z, h, c0, row(conv_ln_g, l), row(conv_ln_b, l), row(gmlp_ln_g, l),
                                              row(gmlp_ln_b, l), t_mix[l], bias_full[l], wb[None], wo, 0)
        saved.append((h, z, hb, bra, c0, brb, mn, kv, kvt, brc, merged))
        h = h_next

    dx, loss_cols, d_final_g = _loss_bwd(h, tgt, final_norm_g[None, :])
    loss = lax.psum(0.5 * jnp.sum(loss_cols) / D, ("x", "y", "c"))

    def add_halves(grads, recv, axes):
        return [_add_half(g, r, ax[1], pos) for g, r, ax in zip(grads, recv, axes)]

    def finish_reduction(sums, blocks, axes):
        halves = [_sum_half(s, b, ax, pos) for (s, _), b, ax in zip(sums, blocks, axes)]
        return list(_sibling_share(halves, axes))

    small = {k: [None] * L for k in ("norm_g", "mem_norm_g", "gmlp_ln_g", "gmlp_ln_b", "w_s", "b_s", "conv_w",
                                     "conv_b", "conv_ln_g", "conv_ln_b")}
    early_axes = reduce_axes[1:]
    pending, reduced, early = None, [None] * L, None
    for l in reversed(range(L)):
        h_in, z, hb, bra, c0, brb, mn, kv, kvt, brc, merged = saved[l]
        win, wkv, wb, wo, cw = weights[l]
        dz_m, dproj, dbr, dc0, dz_b, dcg, dcbeta, dcb, recv = _merge_bwd(
            dx, bra, brb, brc, z, c0, row(conv_ln_g, l), row(conv_ln_b, l), wb[None], wo, 0,
            _pair_comm(pending, reduce_axes) if pending else None)
        sums = add_halves(pending, recv, reduce_axes) if pending else []
        g_o = _wgrad(merged, dx, (1, D, D), 0, 0, None)
        g_b = None
        for n, br in enumerate((bra, brb, brc)):
            g_b = _wgrad(br, dproj, (N_BRANCHES, D, D), n, 0, g_b, b_plane=n)
        dz_a, dws, dbs, dlg, dlb = _branch_a_bwd(z, dbr, row(gmlp_ln_g, l), row(gmlp_ln_b, l),
                                                t_mix[l], t_mix_t[l], bias_full[l], tri)
        dz_c, dk, dv = _branch_c_bwd(z, dbr, kv, kvt)
        g_kv, dmg = _kv_bwd(dk, dv, mn, mems, wkv, 0, None)
        ready = [g_kv, g_b, g_o]
        riding = _both(_block_comm([sb for _, sb in sums], reduce_axes), _pair_comm(ready, early_axes)) if pending else None
        dz_b, dcw, got = _conv_bwd(dc0, z, cw, dz_b, 0, riding)
        sums_early = add_halves(ready, got[4:], early_axes) if pending else []
        if pending:
            reduced[l + 1] = finish_reduction(sums, got[:4], reduce_axes)
        pieces =[(dz_a, 0, 3), (dz_b, 3, 3), (dz_c, 6, 2), (dz_m, 8, 3)]
        g_in = None
        for dzp, lo, _ in pieces:
            g_in = _wgrad(hb, dzp, (1, D, N_PLANES * D), 0, lo * D, g_in)
        if pending:
            riding = _block_comm([sb for _, sb in sums_early], early_axes)
            dx_half, dng_a, got = _in_proj_bwd(pieces, win, h_in, row(norm_g, l), dx, 0,
                                               _both(riding, _pair_comm([g_in], reduce_axes[:1])), (0, 2))
            early = finish_reduction(sums_early, got[:3], early_axes)
            sums_in = add_halves([g_in], got[3:], reduce_axes[:1])
            dx, dng_b, blocks_in = _in_proj_bwd(pieces, win, h_in, row(norm_g, l), dx, 0,
                                                _block_comm([sb for _, sb in sums_in], reduce_axes[:1]), (1, 2), dx_half)
            dng = dng_a + dng_b
            reduced[l] = finish_reduction(sums_in, blocks_in, reduce_axes[:1]) + early
        else:
            dx, dng, _ = _in_proj_bwd(pieces, win, h_in, row(norm_g, l), dx, 0)
        pending = [g_in, g_kv, g_b, g_o]
        for k, val in (("norm_g", dng), ("mem_norm_g", dmg), ("gmlp_ln_g", dlg), ("gmlp_ln_b", dlb), ("w_s", dws),
                       ("b_s", dbs), ("conv_w", dcw), ("conv_b", dcb), ("conv_ln_g", dcg), ("conv_ln_b", dcbeta)):
            small[k][l] = val
    grad_x = dx[None]

    grad_w_in, grad_w_kv, grad_w_branch, grad_w_out = [
        jnp.stack([reduced[0][t], reduced[1][t]]).reshape(w.shape) for t, w in enumerate((w_in, w_kv, w_branch, w_out))]

    order = [("norm_g", norm_g.shape), ("mem_norm_g", mem_norm_g.shape), ("gmlp_ln_g", gmlp_ln_g.shape),
             ("gmlp_ln_b", gmlp_ln_b.shape), ("w_s", w_s.shape), ("b_s", b_s.shape),
             ("conv_w", (L, CONV_K, D)), ("conv_b", conv_b.shape), ("conv_ln_g", conv_ln_g.shape),
             ("conv_ln_b", conv_ln_b.shape)]
    parts = [jnp.stack([v.reshape(shp[1:]) for v in small[k]]).reshape(-1, LANES) for k, shp in order]
    parts.append(d_final_g.reshape(-1, LANES))
    sizes = [p.shape[0] for p in parts]
    assert all(s % 8 == 0 for s in sizes)
    total = _small_allreduce(jnp.concatenate(parts, axis=0))
    red, off = {}, 0
    for (k, shp), n in zip(order + [("final_norm_g", final_norm_g.shape)], sizes):
        red[k] = total[off:off + n].reshape(shp)
        off += n
    cs = conv_w.shape[-1]
    red["conv_w"] = lax.dynamic_slice_in_dim(red["conv_w"], (2 * xi + yi) * cs, cs, axis=2)
    red.update(w_in=grad_w_in, w_kv=grad_w_kv, w_branch=grad_w_branch, w_out=grad_w_out)

    names = ["norm_g", "mem_norm_g", "w_in", "gmlp_ln_g", "gmlp_ln_b", "w_s", "b_s", "conv_w", "conv_b",
             "conv_ln_g", "conv_ln_b", "w_kv", "w_branch", "w_out", "final_norm_g"]
    weights = dict(norm_g=norm_g, mem_norm_g=mem_norm_g, w_in=w_in, gmlp_ln_g=gmlp_ln_g, gmlp_ln_b=gmlp_ln_b,
                   w_s=w_s, b_s=b_s, conv_w=conv_w, conv_b=conv_b, conv_ln_g=conv_ln_g, conv_ln_b=conv_ln_b,
                   w_kv=w_kv, w_branch=w_branch, w_out=w_out, final_norm_g=final_norm_g)
    ms = dict(norm_g=m_norm_g, mem_norm_g=m_mem_norm_g, w_in=m_w_in, gmlp_ln_g=m_gmlp_ln_g, gmlp_ln_b=m_gmlp_ln_b,
              w_s=m_w_s, b_s=m_b_s, conv_w=m_conv_w, conv_b=m_conv_b, conv_ln_g=m_conv_ln_g, conv_ln_b=m_conv_ln_b,
              w_kv=m_w_kv, w_branch=m_w_branch, w_out=m_w_out, final_norm_g=m_final_norm_g)
    vs = dict(norm_g=v_norm_g, mem_norm_g=v_mem_norm_g, w_in=v_w_in, gmlp_ln_g=v_gmlp_ln_g, gmlp_ln_b=v_gmlp_ln_b,
              w_s=v_w_s, b_s=v_b_s, conv_w=v_conv_w, conv_b=v_conv_b, conv_ln_g=v_conv_ln_g, conv_ln_b=v_conv_ln_b,
              w_kv=v_w_kv, w_branch=v_w_branch, w_out=v_w_out, final_norm_g=v_final_norm_g)
    deltas, new_m, new_v = [], [], []
    for k in names:
        w2 = weights[k] if weights[k].ndim > 1 else weights[k][None, :]
        d, mo, vo = _adamw(w2, red[k].reshape(w2.shape), ms[k].reshape(w2.shape), vs[k].reshape(w2.shape))
        deltas.append(d.reshape(weights[k].shape))
        new_m.append(mo.reshape(weights[k].shape))
        new_v.append(vo.reshape(weights[k].shape))
    grads = [red[k].reshape(weights[k].shape) for k in names]
    return (loss, grad_x, *grads, *deltas, *new_m, *new_v)
```

```python
import math

import jax
import jax.numpy as jnp
from jax import lax
from jax.experimental import pallas as pl
from jax.experimental.pallas import tpu as pltpu

F32 = jnp.float32
BF16 = jnp.bfloat16

N_LAYERS = 2
N_BRANCHES = 3
N_PLANES = 11
N_CHIPS = 4
CHUNK = 128
GROUPS = 8
HEADS = 4
CONV_K = 31
HALO = 32
LANES = 128
RMS_EPS = 1e-6
LN_EPS = 1e-5
ADAM_LR, ADAM_B1, ADAM_B2, ADAM_EPS, ADAM_WD, ADAM_STEP = 0.001, 0.9, 0.999, 1e-08, 0.01, 10
VMEM_LIMIT_BYTES = 48 * 1024 * 1024
MESH = pl.DeviceIdType.MESH
ANY = pl.BlockSpec(memory_space=pl.ANY)


def _params(*sem):
    return pltpu.CompilerParams(dimension_semantics=sem, vmem_limit_bytes=VMEM_LIMIT_BYTES)


def _tile(n, pref):
    t = min(n, pref)
    assert n % t == 0, (n, t)
    return t


def _dot(a, b):
    return jnp.dot(a, b, preferred_element_type=F32)


def _dot_tb(a, b):
    return lax.dot_general(a, b, (((1,), (1,)), ((), ())), preferred_element_type=F32)


def _dot_ta(a, b):
    return lax.dot_general(a, b, (((0,), (0,)), ((), ())), preferred_element_type=F32)


def _sigmoid(x):
    return 1.0 / (1.0 + jnp.exp(-x))


def _silu(x):
    return x * _sigmoid(x)


def _silu_and_grad(x):
    s = _sigmoid(x)
    return x * s, s * (1.0 + x * (1.0 - s))


_GELU_C = math.sqrt(2.0 / math.pi)
_GELU_A = 0.044715


def _gelu(x):
    return x * (0.5 * (1.0 + jnp.tanh(_GELU_C * (x + _GELU_A * (x * x * x)))))


def _gelu_and_grad(x):
    x2 = x * x
    t = jnp.tanh(_GELU_C * (x + _GELU_A * (x2 * x)))
    cdf = 0.5 * (1.0 + t)
    dcdf = 0.5 * (1.0 - t * t) * (_GELU_C * (1.0 + 3.0 * _GELU_A * x2))
    return x * cdf, cdf + x * dcdf


def _rows(i, n):
    return pl.ds(pl.multiple_of(i * n, n), n)


def _grouped_loop(n, group, body, init):
    assert n % group == 0

    def trip(i, c):
        for u in range(group):
            c = body(i * group + u, c)
        return c
    return lax.fori_loop(0, n // group, trip, init)


def _mean(x):
    return jnp.mean(x, axis=-1, keepdims=True)


def _colsum(x):
    return jnp.sum(x, axis=0, keepdims=True)


def _tok(ts, d):
    return pl.BlockSpec((ts, d), lambda i: (i, 0))


def _zcol(ts, d, n):
    return pl.BlockSpec((ts, d), lambda i: (i, n))


def _row1(d):
    return pl.BlockSpec((1, d), lambda i: (0, 0))


def _plane(ts, d, n):
    return pl.BlockSpec((None, ts, d), lambda i: (n, i, 0))


class _Hosted:
    def __init__(self, ins, out_shapes, sem_shapes, phases):
        self.ins, self.out_shapes, self.sem_shapes, self.phases = ins, out_shapes, sem_shapes, phases


def _host(comm, n_in, n_out, n_scr, grid, body):
    if comm is None:
        return body, [], [], [], [], None
    ci, co = len(comm.ins), len(comm.out_shapes)
    total = math.prod(grid)
    at = {"first": 0, "late": (7 * total) // 8, "last": total - 1}

    def wrapped(*refs):
        ins, cin = refs[:n_in], refs[n_in:n_in + ci]
        o0 = n_in + ci
        outs, cout = refs[o0:o0 + n_out], refs[o0 + n_out:o0 + n_out + co]
        s0 = o0 + n_out + co
        scr, csem = refs[s0:s0 + n_scr], refs[s0 + n_scr:]
        step = pl.program_id(0)
        for a in range(1, len(grid)):
            step = step * grid[a] + pl.program_id(a)

        def run(last):
            for w, fn in comm.phases:
                if (w == "last") == last:
                    @pl.when(step == at.get(w, w))
                    def _(fn=fn):
                        fn(cin, cout, csem)
        run(False)
        body(*ins, *outs, *scr)
        run(True)

    return wrapped, [ANY] * ci, [ANY] * co, list(comm.out_shapes), list(comm.sem_shapes), ("arbitrary",) * len(grid)


SH = HALO - CONV_K + 1


def _fill_shifts(sh_scr, src_scr, cols, ts, base=0):
    for j in range(1, 8):
        for g in range(ts // 32):
            sh_scr[j - 1, pl.ds(g * 32, 32), :] = src_scr[pl.ds(base + g * 32 + j, 32), cols]
        sh_scr[j - 1, pl.ds(ts, HALO - 8), :] = src_scr[pl.ds(base + ts + j, HALO - 8), cols]


def _shifted(sh_scr, src_scr, cols, row0, n, base=0):
    q, j = divmod(row0, 8)
    if j == 0:
        return src_scr[pl.ds(base + row0, n), cols]
    return sh_scr[j - 1, pl.ds(8 * q, n), :]


def _in_proj_conv(x, g, w_all, cw_all, cb, l, comm=None):
    S, D = x.shape
    N = w_all.shape[2]
    tm = _tile(S, 1024)
    QR, CB = 256, 256
    nq = tm // QR
    PA, PB = 3, 4
    assert tm % QR == 0 and PB + 1 + nq <= N // D
    grid = (S // tm, N // D)

    def body(x_ref, g_ref, w_ref, cw_ref, cb_ref, z_ref, hb_ref, c0_ref, h_scr, za_scr, glu_scr, sh_scr):
        i, j = pl.program_id(0), pl.program_id(1)

        @pl.when(j == 0)
        def _():
            def slab(si, c):
                r = _rows(si, 16)
                xs = x_ref[r, :]
                hb = ((xs * lax.rsqrt(_mean(xs * xs) + RMS_EPS)) * g_ref[...]).astype(BF16)
                h_scr[r, :] = hb
                hb_ref[r, :] = hb
                return c
            _grouped_loop(tm // 16, 4, slab, 0)

        def matmul_cols(cols):
            zc = _dot(h_scr[...], w_ref[:, cols])
            z_ref[:, cols] = zc
            return zc

        @pl.when(j == PA)
        def _():
            def colblock(ci, c):
                cols = _rows(ci, CB)
                za_scr[:, cols] = matmul_cols(cols)
                return c
            lax.fori_loop(0, D // CB, colblock, 0)

        @pl.when((j == PB) & (i == 0))
        def _():
            glu_scr[0:HALO, :] = jnp.zeros((HALO, D), F32)

        @pl.when((j == PB) & (i > 0))
        def _():
            glu_scr[0:HALO, :] = glu_scr[tm:tm + HALO, :]

        @pl.when(j == PB)
        def _():
            def colblock(ci, c):
                cols = _rows(ci, CB)
                glu_scr[HALO:HALO + tm, cols] = za_scr[:, cols] * _sigmoid(matmul_cols(cols))
                return c
            lax.fori_loop(0, D // CB, colblock, 0)

        for q in range(nq):
            @pl.when(j == PB + 1 + q)
            def _(q=q):
                def colblock(ci, c):
                    cols = _rows(ci, CB)
                    matmul_cols(cols)
                    _fill_shifts(sh_scr, glu_scr, cols, QR, q * QR)
                    for s in range(QR // 32):
                        acc = jnp.zeros((32, CB), F32)
                        for k in range(CONV_K):
                            acc = acc + _shifted(sh_scr, glu_scr, cols, s * 32 + k + SH, 32, q * QR) * cw_ref[k:k + 1, cols]
                        c0_ref[pl.ds(q * QR + s * 32, 32), cols] = acc + cb_ref[:, cols]
                    return c
                lax.fori_loop(0, D // CB, colblock, 0)

        @pl.when((j < PA) | (j > PB + nq))
        def _():
            z_ref[...] = _dot(h_scr[...], w_ref[...])

    tile = lambda: pl.BlockSpec((tm, D), lambda i, j: (i, 0))
    body, c_in, c_out, c_shapes, c_scr, _ = _host(comm, 5, 3, 4, grid, body)
    outs = pl.pallas_call(
        body, name="in_proj_conv", grid=grid,
        in_specs=[tile(), pl.BlockSpec((1, D), lambda i, j: (0, 0)),
                  pl.BlockSpec((None, D, D), lambda i, j: (l, 0, j)),
                  pl.BlockSpec((None, CONV_K, D), lambda i, j: (l, 0, 0)),
                  pl.BlockSpec((1, D), lambda i, j: (0, 0))] + c_in,
        out_specs=[pl.BlockSpec((tm, D), lambda i, j: (i, j)), tile(), tile()] + c_out,
        out_shape=[jax.ShapeDtypeStruct((S, N), F32), jax.ShapeDtypeStruct((S, D), BF16),
                   jax.ShapeDtypeStruct((S, D), F32)] + c_shapes,
        scratch_shapes=[pltpu.VMEM((tm, D), BF16), pltpu.VMEM((tm, D), F32), pltpu.VMEM((tm + HALO, D), F32),
                        pltpu.VMEM((7, QR + HALO, CB), F32)] + c_scr,
        compiler_params=_params("arbitrary", "arbitrary"),
    )(x, g, w_all, cw_all, cb, *(comm.ins if comm else []))
    return outs[0], outs[1], outs[2], outs[3:]


def _kv_fwd(mem, mg, wkv_all, l):
    M, D = mem.shape

    def body(mem_ref, g_ref, w_ref, mn_ref, kv_ref, kvt_ref):
        m = mem_ref[...]
        mn = ((m * lax.rsqrt(_mean(m * m) + RMS_EPS)) * g_ref[...]).astype(BF16)
        mn_ref[...] = mn
        kv = _dot(mn, w_ref[...])
        kv_ref[...] = kv.astype(BF16)
        kvt_ref[...] = kv.T.astype(BF16)

    return pl.pallas_call(
        body, name="kv_fwd", grid=(1,),
        in_specs=[pl.BlockSpec((M, D), lambda i: (0, 0)), _row1(D),
                  pl.BlockSpec((None, D, 2 * D), lambda i: (l, 0, 0))],
        out_specs=[pl.BlockSpec((M, D), lambda i: (0, 0)), pl.BlockSpec((M, 2 * D), lambda i: (0, 0)),
                   pl.BlockSpec((2 * D, M), lambda i: (0, 0))],
        out_shape=[jax.ShapeDtypeStruct((M, D), BF16), jax.ShapeDtypeStruct((M, 2 * D), BF16),
                   jax.ShapeDtypeStruct((2 * D, M), BF16)],
        compiler_params=_params("arbitrary"),
    )(mem, mg, wkv_all)


def _softmax_rows(s):
    e = jnp.exp(s - jnp.max(s, axis=-1, keepdims=True))
    return e / jnp.sum(e, axis=-1, keepdims=True)


def _branch_c_fwd(z, kv, kvt):
    S = z.shape[0]
    M, D2 = kv.shape
    D = D2 // 2
    HD = D // HEADS
    scale = 1.0 / math.sqrt(HD)
    ts = _tile(S, 256)

    def body(q_ref, zg_ref, kt_ref, v_ref, o_ref):
        for h in range(HEADS):
            cols = slice(h * HD, (h + 1) * HD)
            p = _softmax_rows(_dot(q_ref[:, cols].astype(BF16), kt_ref[cols, :]) * scale)
            att = _dot(p.astype(BF16), v_ref[:, cols])
            o_ref[:, cols] = (att * _silu(zg_ref[:, cols])).astype(BF16)

    return pl.pallas_call(
        body, name="branch_c_fwd", grid=(S // ts,),
        in_specs=[_zcol(ts, D, 6), _zcol(ts, D, 7),
                  pl.BlockSpec((D, M), lambda i: (0, 0)), pl.BlockSpec((M, D), lambda i: (0, 1))],
        out_specs=_tok(ts, D),
        out_shape=jax.ShapeDtypeStruct((S, D), BF16),
        compiler_params=_params("parallel"),
    )(z, z, kvt, kv)


def _merge_fwd(brc, z, x, c0, cg, cb, lng, lnb, t_mix, bias_full, wb_all, wo_all, l):
    S, D = x.shape
    GD = D // GROUPS
    ts = _tile(S, 256)
    CB = 256

    def body(c_ref, zm0_ref, zm1_ref, zm2_ref, x_ref, c0_ref, zbg_ref, cg_ref, cb_ref, zu_ref, zv_ref, zag_ref,
             lng_ref, lnb_ref, t_ref, bias_ref, wb_ref, wo_ref, a_ref, b_ref, m_ref, xo_ref, v_scr):
        for s in range(ts // 16):
            r = pl.ds(s * 16, 16)
            v = c0_ref[r, :]
            xc = v - _mean(v)
            c1 = (xc * lax.rsqrt(_mean(xc * xc) + LN_EPS)) * cg_ref[...] + cb_ref[...]
            b_ref[r, :] = (_silu(c1) * _silu(zbg_ref[r, :])).astype(BF16)
        for ch in range(ts // CHUNK):
            for s in range(CHUNK // 16):
                vg = _gelu(zv_ref[pl.ds(ch * CHUNK + s * 16, 16), :])
                xc = vg - _mean(vg)
                y = xc * lax.rsqrt(_mean(xc * xc) + LN_EPS)
                v_scr[pl.ds(ch * CHUNK + s * 16, 16), :] = (y * lng_ref[...] + lnb_ref[...]).astype(BF16)
            rows = pl.ds(ch * CHUNK, CHUNK)
            for g in range(GROUPS):
                cols = slice(g * GD, (g + 1) * GD)
                sv = _dot(t_ref[g], v_scr[rows, cols]) + bias_ref[:, cols]
                u = _gelu(zu_ref[rows, cols])
                a_ref[rows, cols] = ((u * sv) * _silu(zag_ref[rows, cols])).astype(BF16)

        brs = (a_ref, b_ref, c_ref)
        zms = (zm0_ref, zm1_ref, zm2_ref)
        for cb in range(D // CB):
            cols = slice(cb * CB, (cb + 1) * CB)
            acc = None
            for n in range(N_BRANCHES):
                t = _sigmoid(zms[n][:, cols]) * _dot(brs[n][...], wb_ref[n, :, cols])
                acc = t if acc is None else acc + t
            m_ref[:, cols] = acc.astype(BF16)
        for cb in range(D // CB):
            cols = slice(cb * CB, (cb + 1) * CB)
            xo_ref[:, cols] = x_ref[:, cols] + _dot(m_ref[...], wo_ref[:, cols])

    return pl.pallas_call(
        body, name="merge_fwd", grid=(S // ts,),
        in_specs=[_tok(ts, D), _zcol(ts, D, 8), _zcol(ts, D, 9), _zcol(ts, D, 10), _tok(ts, D),
                  _tok(ts, D), _zcol(ts, D, 5), _row1(D), _row1(D),
                  _zcol(ts, D, 0), _zcol(ts, D, 1), _zcol(ts, D, 2), _row1(D), _row1(D),
                  pl.BlockSpec((GROUPS, CHUNK, CHUNK), lambda i: (0, 0, 0)),
                  pl.BlockSpec((CHUNK, D), lambda i: (0, 0)),
                  pl.BlockSpec((None, N_BRANCHES, D, D), lambda i: (l, 0, 0, 0)),
                  pl.BlockSpec((None, D, D), lambda i: (l, 0, 0))],
        out_specs=[_tok(ts, D), _tok(ts, D), _tok(ts, D), _tok(ts, D)],
        out_shape=[jax.ShapeDtypeStruct((S, D), BF16), jax.ShapeDtypeStruct((S, D), BF16),
                   jax.ShapeDtypeStruct((S, D), BF16), jax.ShapeDtypeStruct((S, D), F32)],
        scratch_shapes=[pltpu.VMEM((ts, D), BF16)],
        compiler_params=_params("parallel"),
    )(brc, z, z, z, x, c0, z, cg, cb, z, z, z, lng, lnb, t_mix, bias_full, wb_all, wo_all)


def _loss_bwd(x, target, fg):
    S, D = x.shape
    ts = _tile(S, 512)

    def body(x_ref, t_ref, g_ref, dx_ref, ls_ref, dg_ref):
        @pl.when(pl.program_id(0) == 0)
        def _():
            ls_ref[...] = jnp.zeros_like(ls_ref)
            dg_ref[...] = jnp.zeros_like(dg_ref)

        def slab(si, c):
            ls, dg = c
            r = _rows(si, 16)
            xs = x_ref[r, :]
            rs = lax.rsqrt(_mean(xs * xs) + RMS_EPS)
            n = xs * rs
            e = n * g_ref[...] - t_ref[r, :]
            dy = e * (1.0 / D)
            dn = dy * g_ref[...]
            dx_ref[r, :] = rs * (dn - n * _mean(dn * n))
            return ls + _colsum(e * e), dg + _colsum(dy * n)
        zero = jnp.zeros((1, D), F32)
        ls, dg = _grouped_loop(ts // 16, 4, slab, (zero, zero))
        ls_ref[...] += ls
        dg_ref[...] += dg

    return pl.pallas_call(
        body, name="loss_bwd", grid=(S // ts,),
        in_specs=[_tok(ts, D), _tok(ts, D), _row1(D)],
        out_specs=[_tok(ts, D), _row1(D), _row1(D)],
        out_shape=[jax.ShapeDtypeStruct((S, D), F32), jax.ShapeDtypeStruct((1, D), F32),
                   jax.ShapeDtypeStruct((1, D), F32)],
        compiler_params=_params("arbitrary"),
    )(x, target, fg)


def _merge_bwd(dxp, bra, brb, brc, z, c0, lng, lnb, wb_all, wo_all, l, comm=None):
    S, D = dxp.shape
    ts = _tile(S, 256)
    CB = 256

    def body(d_ref, a_ref, b_ref, c_ref, zm0_ref, zm1_ref, zm2_ref, c0_ref, zg_ref, g_ref, be_ref, wb_ref, wo_ref,
             dzm_ref, dproj_ref, dbr_ref, dc0_ref, dzb_ref, dg_ref, db_ref, dcb_ref, dxb_scr, dbb_scr):
        @pl.when(pl.program_id(0) == 0)
        def _():
            dg_ref[...] = jnp.zeros_like(dg_ref)
            db_ref[...] = jnp.zeros_like(db_ref)
            dcb_ref[...] = jnp.zeros_like(dcb_ref)

        brs = (a_ref, b_ref, c_ref)
        zms = (zm0_ref, zm1_ref, zm2_ref)
        dxb_scr[...] = d_ref[...].astype(BF16)
        for cb in range(D // CB):
            cols = slice(cb * CB, (cb + 1) * CB)
            dm = _dot_tb(dxb_scr[...], wo_ref[cols, :])
            for n in range(N_BRANCHES):
                proj = _dot(brs[n][...], wb_ref[n, :, cols])
                g = _sigmoid(zms[n][:, cols])
                dzm_ref[:, n * D + cb * CB:n * D + (cb + 1) * CB] = ((dm * proj) * (g * (1.0 - g))).astype(BF16)
                dproj_ref[n, :, cols] = (dm * g).astype(BF16)
        for cb in range(D // CB):
            cols = slice(cb * CB, (cb + 1) * CB)
            dbb_scr[:, cols] = _dot_tb(dproj_ref[1], wb_ref[1, cols, :])

        dg = db = dcb = jnp.zeros((1, D), F32)
        for s in range(ts // 16):
            r = pl.ds(s * 16, 16)
            v = c0_ref[r, :]
            xc = v - _mean(v)
            rstd = lax.rsqrt(_mean(xc * xc) + LN_EPS)
            cn = xc * rstd
            c1 = cn * g_ref[...] + be_ref[...]
            c2, dc2_dc1 = _silu_and_grad(c1)
            sg, dsg = _silu_and_grad(zg_ref[r, :])
            d = dbb_scr[r, :]
            dzb_ref[r, :] = ((d * c2) * dsg).astype(BF16)
            dc1 = (d * sg) * dc2_dc1
            dcn = dc1 * g_ref[...]
            dc0 = rstd * ((dcn - _mean(dcn)) - cn * _mean(dcn * cn))
            dc0_ref[r, :] = dc0
            dg, db, dcb = dg + _colsum(dc1 * cn), db + _colsum(dc1), dcb + _colsum(dc0)
        dg_ref[...] += dg
        db_ref[...] += db
        dcb_ref[...] += dcb

        for plane, n in enumerate((0, 2)):
            for cb in range(D // CB):
                cols = slice(cb * CB, (cb + 1) * CB)
                dbr_ref[plane, :, cols] = _dot_tb(dproj_ref[n], wb_ref[n, cols, :])

    tokn = lambda n: pl.BlockSpec((n, ts, D), lambda i: (0, i, 0))
    grid = (S // ts,)
    body, c_in, c_out, c_shapes, c_scr, _ = _host(comm, 13, 8, 2, grid, body)
    outs = pl.pallas_call(
        body, name="merge_bwd", grid=grid,
        in_specs=[_tok(ts, D), _tok(ts, D), _tok(ts, D), _tok(ts, D),
                  _zcol(ts, D, 8), _zcol(ts, D, 9), _zcol(ts, D, 10), _tok(ts, D), _zcol(ts, D, 5), _row1(D), _row1(D),
                  pl.BlockSpec((None, N_BRANCHES, D, D), lambda i: (l, 0, 0, 0)),
                  pl.BlockSpec((None, D, D), lambda i: (l, 0, 0))] + c_in,
        out_specs=[_tok(ts, 3 * D), tokn(N_BRANCHES), tokn(2), _tok(ts, D), _zcol(ts, D, 2),
                   _row1(D), _row1(D), _row1(D)] + c_out,
        out_shape=[jax.ShapeDtypeStruct((S, 3 * D), BF16), jax.ShapeDtypeStruct((N_BRANCHES, S, D), BF16),
                   jax.ShapeDtypeStruct((2, S, D), F32), jax.ShapeDtypeStruct((S, D), F32),
                   jax.ShapeDtypeStruct((S, 3 * D), BF16), jax.ShapeDtypeStruct((1, D), F32),
                   jax.ShapeDtypeStruct((1, D), F32), jax.ShapeDtypeStruct((1, D), F32)] + c_shapes,
        scratch_shapes=[pltpu.VMEM((ts, D), BF16), pltpu.VMEM((ts, D), F32)] + c_scr,
        compiler_params=_params("arbitrary"),
    )(dxp, bra, brb, brc, z, z, z, c0, z, lng, lnb, wb_all, wo_all, *(comm.ins if comm else []))
    return (*outs[:8], outs[8:])


def _branch_a_bwd(z, dbr, lng, lnb, t_mix, t_mix_t, bias_full, mask):
    S = z.shape[0]
    D = lng.shape[1]
    GD = D // GROUPS
    ts = _tile(S, 256)
    nt = S // ts

    def body(zu_ref, zv_ref, zg_ref, d_ref, lng_ref, lnb_ref, t_ref, tt_ref, bias_ref, mask_ref,
             dz_ref, dws_ref, dbs_ref, dlg_ref, dlb_ref,
             v_scr, vn_scr, rstd_scr, dv_scr, dsv_acc):
        i = pl.program_id(0)

        @pl.when(i == 0)
        def _():
            dws_ref[...] = jnp.zeros_like(dws_ref)
            dlg_ref[...] = jnp.zeros_like(dlg_ref)
            dlb_ref[...] = jnp.zeros_like(dlb_ref)
            dsv_acc[...] = jnp.zeros_like(dsv_acc)

        def chunk(ci, carry):
            c0 = pl.multiple_of(ci * CHUNK, CHUNK)

            def ln_fwd(si, c):
                r = pl.multiple_of(si * 16, 16)
                vg = _gelu(zv_ref[pl.ds(pl.multiple_of(c0 + r, 16), 16), :])
                xc = vg - _mean(vg)
                rstd = lax.rsqrt(_mean(xc * xc) + LN_EPS)
                vn = xc * rstd
                vn_scr[pl.ds(r, 16), :] = vn
                rstd_scr[pl.ds(r, 16), :] = rstd
                v_scr[pl.ds(r, 16), :] = (vn * lng_ref[...] + lnb_ref[...]).astype(BF16)
                return c
            _grouped_loop(CHUNK // 16, 4, ln_fwd, 0)

            for g in range(GROUPS):
                cols = slice(g * GD, (g + 1) * GD)
                rows = pl.ds(c0, CHUNK)
                vb = v_scr[:, cols]
                sv = _dot(t_ref[g], vb) + bias_ref[:, cols]
                u, du = _gelu_and_grad(zu_ref[rows, cols])
                sg, dsg = _silu_and_grad(zg_ref[rows, cols])
                d = d_ref[rows, cols]
                dsv = (d * u) * sg
                dz_ref[rows, g * GD:(g + 1) * GD] = (((d * sv) * sg) * du).astype(BF16)
                dz_ref[rows, 2 * D + g * GD:2 * D + (g + 1) * GD] = (((d * u) * sv) * dsg).astype(BF16)
                dsvb = dsv.astype(BF16)
                dws_ref[g] += _dot_tb(dsvb, vb)
                dv_scr[:, cols] = _dot(tt_ref[g], dsvb)
                dsv_acc[:, cols] += dsv

            def ln_bwd(si, c):
                dlg, dlb = c
                r = pl.multiple_of(si * 16, 16)
                rr = pl.ds(r, 16)
                zrows = pl.ds(pl.multiple_of(c0 + r, 16), 16)
                dv = dv_scr[rr, :]
                vn = vn_scr[rr, :]
                dvn = dv * lng_ref[...]
                dvg = rstd_scr[rr, :] * ((dvn - _mean(dvn)) - vn * _mean(dvn * vn))
                _, gg = _gelu_and_grad(zv_ref[zrows, :])
                dz_ref[zrows, D:2 * D] = (dvg * gg).astype(BF16)
                return dlg + _colsum(dv * vn), dlb + _colsum(dv)
            zero = jnp.zeros((1, D), F32)
            dlg, dlb = _grouped_loop(CHUNK // 16, 4, ln_bwd, (zero, zero))
            dlg_ref[...] += dlg
            dlb_ref[...] += dlb
            return carry
        lax.fori_loop(0, ts // CHUNK, chunk, 0)

        @pl.when(i == nt - 1)
        def _():
            for g in range(GROUPS):
                cols = slice(g * GD, (g + 1) * GD)
                dws_ref[g] = dws_ref[g] * mask_ref[...]
                dbs_ref[g:g + 1, :] = _colsum(dsv_acc[:, cols].T)

    c2 = lambda shape: pl.BlockSpec(shape, lambda i: (0, 0))
    c3 = lambda: pl.BlockSpec((GROUPS, CHUNK, CHUNK), lambda i: (0, 0, 0))
    return pl.pallas_call(
        body, name="branch_a_bwd", grid=(nt,),
        in_specs=[_zcol(ts, D, 0), _zcol(ts, D, 1), _zcol(ts, D, 2), _plane(ts, D, 0), _row1(D), _row1(D),
                  c3(), c3(), c2((CHUNK, D)), c2((CHUNK, CHUNK))],
        out_specs=[_tok(ts, 3 * D), c3(), c2((GROUPS, CHUNK)), _row1(D), _row1(D)],
        out_shape=[jax.ShapeDtypeStruct((S, 3 * D), BF16), jax.ShapeDtypeStruct((GROUPS, CHUNK, CHUNK), F32),
                   jax.ShapeDtypeStruct((GROUPS, CHUNK), F32), jax.ShapeDtypeStruct((1, D), F32),
                   jax.ShapeDtypeStruct((1, D), F32)],
        scratch_shapes=[pltpu.VMEM((CHUNK, D), BF16), pltpu.VMEM((CHUNK, D), F32), pltpu.VMEM((CHUNK, 1), F32),
                        pltpu.VMEM((CHUNK, D), F32), pltpu.VMEM((CHUNK, D), F32)],
        compiler_params=_params("arbitrary"),
    )(z, z, z, dbr, lng, lnb, t_mix, t_mix_t, bias_full, mask)


def _conv_bwd(dc0, z, w_all, dz_b, l, comm=None):
    S, D = dc0.shape
    ts = _tile(S, 256)
    nt = S // ts
    CB = 256
    hb = ts // HALO
    nh = S // HALO

    def body(dc_ref, dcn_ref, za_ref, zb_ref, w_ref, dzin_ref, dz_ref, dw_ref, dc_scr, dw_acc, dsh_scr):
        del dzin_ref
        i = pl.program_id(0)

        @pl.when(i == 0)
        def _():
            dw_acc[...] = jnp.zeros_like(dw_acc)

        dc_scr[ts:ts + HALO, :] = jnp.where(i == nt - 1, 0.0, dcn_ref[...])

        def fill(si, c):
            r = _rows(si, 32)
            dc_scr[r, :] = dc_ref[r, :]
            return c
        _grouped_loop(ts // 32, 2, fill, 0)

        def colblock(ci, c):
            cols = _rows(ci, CB)
            cols_b = pl.ds(pl.multiple_of(D + ci * CB, CB), CB)
            _fill_shifts(dsh_scr, dc_scr, cols, ts)
            for s in range(ts // 32):
                rows = pl.ds(s * 32, 32)
                za = za_ref[rows, cols]
                sg = _sigmoid(zb_ref[rows, cols])
                glu = za * sg
                acc = jnp.zeros((32, CB), F32)
                for k in range(CONV_K):
                    dcs = _shifted(dsh_scr, dc_scr, cols, s * 32 + CONV_K - 1 - k, 32)
                    prod = glu * dcs
                    dw_acc[k, :, cols] += (prod[0:8] + prod[8:16]) + (prod[16:24] + prod[24:32])
                    acc = acc + dcs * w_ref[k:k + 1, cols]
                dz_ref[rows, cols] = (acc * sg).astype(BF16)
                dz_ref[rows, cols_b] = ((acc * za) * (sg * (1.0 - sg))).astype(BF16)
            return c
        lax.fori_loop(0, D // CB, colblock, 0)

        @pl.when(i == nt - 1)
        def _():
            for k in range(CONV_K):
                dw_ref[k:k + 1, :] = _colsum(dw_acc[k])

    grid = (nt,)
    body, c_in, c_out, c_shapes, c_scr, _ = _host(comm, 6, 2, 3, grid, body)
    outs = pl.pallas_call(
        body, name="conv_bwd", grid=grid,
        in_specs=[_tok(ts, D), pl.BlockSpec((HALO, D), lambda i: (jnp.minimum((i + 1) * hb, nh - 1), 0)),
                  _zcol(ts, D, 3), _zcol(ts, D, 4),
                  pl.BlockSpec((None, CONV_K, D), lambda i: (l, 0, 0)), ANY] + c_in,
        out_specs=[_tok(ts, 2 * D), pl.BlockSpec((CONV_K, D), lambda i: (0, 0))] + c_out,
        out_shape=[jax.ShapeDtypeStruct((S, 3 * D), BF16), jax.ShapeDtypeStruct((CONV_K, D), F32)] + c_shapes,
        scratch_shapes=[pltpu.VMEM((ts + HALO, D), F32), pltpu.VMEM((CONV_K, 8, D), F32),
                        pltpu.VMEM((7, ts + HALO, CB), F32)] + c_scr,
        input_output_aliases={5: 0},
        compiler_params=_params("arbitrary"),
    )(dc0, dc0, z, z, w_all, dz_b, *(comm.ins if comm else []))
    return outs[0], outs[1], outs[2:]


def _branch_c_bwd(z, dbr, kv, kvt):
    S = z.shape[0]
    M, D2 = kv.shape
    D = D2 // 2
    HD = D // HEADS
    scale = 1.0 / math.sqrt(HD)
    ts = _tile(S, 256)

    def body(q_ref, zg_ref, d_ref, kt_ref, vt_ref, k_ref, v_ref, dz_ref, dk_ref, dv_ref):
        @pl.when(pl.program_id(0) == 0)
        def _():
            dk_ref[...] = jnp.zeros_like(dk_ref)
            dv_ref[...] = jnp.zeros_like(dv_ref)

        for h in range(HEADS):
            cols = slice(h * HD, (h + 1) * HD)
            qb = q_ref[:, cols].astype(BF16)
            p = _softmax_rows(_dot(qb, kt_ref[cols, :]) * scale)
            pb = p.astype(BF16)
            att = _dot(pb, v_ref[:, cols])
            sg, dsg = _silu_and_grad(zg_ref[:, cols])
            d = d_ref[:, cols]
            dz_ref[:, D + h * HD:D + (h + 1) * HD] = ((d * att) * dsg).astype(BF16)
            datt = (d * sg).astype(BF16)
            dp = _dot(datt, vt_ref[cols, :])
            dv_ref[:, cols] += _dot_ta(pb, datt)
            ds = ((p * (dp - jnp.sum(dp * p, axis=-1, keepdims=True))) * scale).astype(BF16)
            dz_ref[:, cols] = _dot(ds, k_ref[:, cols]).astype(BF16)
            dk_ref[:, cols] += _dot_ta(ds, qb)

    return pl.pallas_call(
        body, name="branch_c_bwd", grid=(S // ts,),
        in_specs=[_zcol(ts, D, 6), _zcol(ts, D, 7), _plane(ts, D, 1),
                  pl.BlockSpec((D, M), lambda i: (0, 0)), pl.BlockSpec((D, M), lambda i: (1, 0)),
                  pl.BlockSpec((M, D), lambda i: (0, 0)), pl.BlockSpec((M, D), lambda i: (0, 1))],
        out_specs=[_tok(ts, 2 * D), pl.BlockSpec((M, D), lambda i: (0, 0)), pl.BlockSpec((M, D), lambda i: (0, 0))],
        out_shape=[jax.ShapeDtypeStruct((S, 2 * D), BF16), jax.ShapeDtypeStruct((M, D), F32),
                   jax.ShapeDtypeStruct((M, D), F32)],
        compiler_params=_params("arbitrary"),
    )(z, z, dbr, kvt, kvt, kv, kv)


def _kv_bwd(dk, dv, mn, mem, wkv_all, l, prev):
    M, D = mem.shape

    def body(*refs):
        dk_ref, dv_ref, mn_ref, mem_ref, w_ref = refs[:5]
        dw_ref, dg_ref = refs[-2:]
        dkb = dk_ref[...].astype(BF16)
        dvb = dv_ref[...].astype(BF16)
        dw_ref[:, 0:D] = _dot_ta(mn_ref[...], dkb)
        dw_ref[:, D:2 * D] = _dot_ta(mn_ref[...], dvb)
        dmn = _dot_tb(dkb, w_ref[:, 0:D]) + _dot_tb(dvb, w_ref[:, D:2 * D])
        m = mem_ref[...]
        dg_ref[...] = _colsum(dmn * (m * lax.rsqrt(_mean(m * m) + RMS_EPS)))

    full = lambda shape: pl.BlockSpec(shape, lambda i: (0, 0))
    in_specs = [full((M, D)), full((M, D)), full((M, D)), full((M, D)),
                pl.BlockSpec((None, D, 2 * D), lambda i: (l, 0, 0))]
    args = [dk, dv, mn, mem, wkv_all]
    aliases = {}
    if prev is not None:
        in_specs.append(ANY)
        args.append(prev)
        aliases = {5: 0}
    return pl.pallas_call(
        body, name="kv_bwd", grid=(1,),
        in_specs=in_specs,
        out_specs=[pl.BlockSpec((None, D, 2 * D), lambda i: (l, 0, 0)), _row1(D)],
        out_shape=[jax.ShapeDtypeStruct((wkv_all.shape[0], D, 2 * D), F32), jax.ShapeDtypeStruct((1, D), F32)],
        input_output_aliases=aliases,
        compiler_params=_params("arbitrary"),
    )(*args)


def _in_proj_bwd(pieces, w_all, x, g, dxp, l, comm=None, part=(0, 1), prev=None):
    S, D = x.shape
    tm = _tile(S, 1024)
    n_p = len(pieces)
    n_i = (S // tm) // part[1]
    i0 = part[0] * n_i
    n_in = n_p + 4 + (prev is not None)

    def body(*refs):
        dz_refs = refs[:n_p]
        w_ref, x_hbm, g_ref, dxp_hbm = refs[n_p:n_p + 4]
        dx_ref, dg_ref, acc, x_ref, dxp_ref, sem = refs[n_in:]
        i = pl.program_id(0)
        k = pl.program_id(1)
        rows = pl.ds(pl.multiple_of((i + i0) * tm, tm), tm)
        fetch = [pltpu.make_async_copy(x_hbm.at[rows, :], x_ref, sem.at[0]),
                 pltpu.make_async_copy(dxp_hbm.at[rows, :], dxp_ref, sem.at[1])]

        @pl.when(k == N_PLANES - 2)
        def _():
            for cp in fetch:
                cp.start()

        @pl.when(k == 0)
        def _():
            acc[...] = jnp.zeros_like(acc)

        @pl.when((i == 0) & (k == 0))
        def _():
            dg_ref[...] = jnp.zeros_like(dg_ref)

        for (_, lo, n), r in zip(pieces, dz_refs):
            @pl.when((k >= lo) & (k < lo + n))
            def _(r=r):
                acc[...] += _dot_tb(r[...], w_ref[...])

        @pl.when(k == N_PLANES - 1)
        def _():
            for cp in fetch:
                cp.wait()

            def slab(si, c):
                rr = _rows(si, 16)
                xs = x_ref[rr, :]
                rs = lax.rsqrt(_mean(xs * xs) + RMS_EPS)
                n = xs * rs
                dh = acc[rr, :]
                dn = dh * g_ref[...]
                dx_ref[rr, :] = rs * (dn - n * _mean(dn * n)) + dxp_ref[rr, :]
                return c + _colsum(dh * n)
            dg_ref[...] += _grouped_loop(tm // 16, 4, slab, jnp.zeros((1, D), F32))

    def piece_spec(lo, n):
        return pl.BlockSpec((tm, D), lambda i, k: (i + i0, jnp.clip(k - lo, 0, n - 1)))

    tokk = lambda: pl.BlockSpec((tm, D), lambda i, k: (i + i0, 0))
    rowk = lambda: pl.BlockSpec((1, D), lambda i, k: (0, 0))
    grid = (n_i, N_PLANES)
    body, c_in, c_out, c_shapes, c_scr, _ = _host(comm, n_in, 2, 4, grid, body)
    more_in, more_args, aliases = ([ANY], [prev], {n_in - 1: 0}) if prev is not None else ([], [], {})
    outs = pl.pallas_call(
        body, name="in_proj_bwd", grid=grid,
        in_specs=[piece_spec(lo, n) for _, lo, n in pieces]
        + [pl.BlockSpec((None, D, D), lambda i, k: (l, 0, k)), ANY, rowk(), ANY] + more_in + c_in,
        out_specs=[tokk(), rowk()] + c_out,
        out_shape=[jax.ShapeDtypeStruct((S, D), F32), jax.ShapeDtypeStruct((1, D), F32)] + c_shapes,
        scratch_shapes=[pltpu.VMEM((tm, D), F32), pltpu.VMEM((tm, D), F32), pltpu.VMEM((tm, D), F32),
                        pltpu.SemaphoreType.DMA((2,))] + c_scr,
        input_output_aliases=aliases,
        compiler_params=_params("arbitrary", "arbitrary"),
    )(*[p for p, _, _ in pieces], w_all, x, g, dxp, *more_args, *(comm.ins if comm else []))
    return outs[0], outs[1], outs[2:]


def _wgrad(a, b, out_shape, plane, col_off, prev, a_plane=None, b_plane=None):
    S = a.shape[-2]
    K1 = a.shape[-1]
    nb_cols = b.shape[-1]
    ts = _tile(S, 2048)
    tn = _tile(nb_cols, 1024)
    col_off = col_off // tn

    def body(*refs):
        a_ref, b_ref = refs[:2]
        o_ref = refs[-1]
        k = pl.program_id(1)
        prod = _dot_ta(a_ref[...].astype(BF16), b_ref[...].astype(BF16))

        @pl.when(k == 0)
        def _():
            o_ref[...] = prod

        @pl.when(k > 0)
        def _():
            o_ref[...] += prod

    if a_plane is None:
        a_spec = pl.BlockSpec((ts, K1), lambda j, k: (k, 0))
    else:
        a_spec = pl.BlockSpec((None, ts, K1), lambda j, k: (a_plane, k, 0))
    if b_plane is None:
        b_spec = pl.BlockSpec((ts, tn), lambda j, k: (k, j))
    else:
        b_spec = pl.BlockSpec((None, ts, tn), lambda j, k: (b_plane, k, j))
    in_specs, args, aliases = [a_spec, b_spec], [a, b], {}
    if prev is not None:
        in_specs.append(ANY)
        args.append(prev)
        aliases = {2: 0}
    return pl.pallas_call(
        body, name="wgrad", grid=(nb_cols // tn, S // ts),
        in_specs=in_specs,
        out_specs=pl.BlockSpec((None, K1, tn), lambda j, k: (plane, 0, col_off + j)),
        out_shape=jax.ShapeDtypeStruct(out_shape, F32),
        input_output_aliases=aliases,
        compiler_params=_params("parallel", "arbitrary"),
    )(*args)


def _position():
    x, y, c = lax.axis_index("x"), lax.axis_index("y"), lax.axis_index("c")
    chips = [(1 - x, y), (x, 1 - y), (1 - x, 1 - y)]
    return x, y, c, chips


def _sub(ref, axis, start, size):
    idx = [slice(None)] * len(ref.shape)
    idx[axis] = pl.ds(start, size)
    return ref.at[tuple(idx)]


def _dma(src, dst, ssem, rsem, to):
    return pltpu.make_async_remote_copy(src_ref=src, dst_ref=dst, send_sem=ssem, recv_sem=rsem,
                                        device_id=to, device_id_type=MESH)


def _dma_sems(*counts):
    return [pltpu.SemaphoreType.DMA((n,)) for n in counts]


def _both(a, b):
    na, oa, sa = len(a.ins), len(a.out_shapes), len(a.sem_shapes)
    phases = [(w, lambda i, o, s, fn=fn: fn(i[:na], o[:oa], s[:sa])) for w, fn in a.phases]
    phases += [(w, lambda i, o, s, fn=fn: fn(i[na:], o[oa:], s[sa:])) for w, fn in b.phases]
    return _Hosted(a.ins + b.ins, list(a.out_shapes) + list(b.out_shapes), list(a.sem_shapes) + list(b.sem_shapes), phases)


def _run_alone(comm, name):
    ci, co = len(comm.ins), len(comm.out_shapes)

    def body(*refs):
        for _, fn in comm.phases:
            fn(refs[:ci], refs[ci:ci + co], refs[ci + co:])

    return pl.pallas_call(
        body, name=name, in_specs=[ANY] * ci, out_specs=[ANY] * co, out_shape=list(comm.out_shapes),
        scratch_shapes=list(comm.sem_shapes),
    )(*comm.ins)


def _gather_comm(shards, layers, axes):
    n = len(shards)
    out_shapes = []
    for s, (ba, _) in zip(shards, axes):
        shp = list(s.shape[1:])
        shp[ba] *= N_CHIPS
        out_shapes.append(jax.ShapeDtypeStruct(tuple(shp), s.dtype))

    def half(ref, t, h):
        ha = axes[t][1]
        if ha is None:
            return ref
        hs = shards[t].shape[1 + ha] // 2
        return _sub(ref, ha, h * hs, hs)

    def block(outs, t, blk):
        ba = axes[t][0]
        bs = shards[t].shape[1 + ba]
        return _sub(outs[t], ba, blk * bs, bs)

    def start(ins, outs, sems):
        s_own, r_own, s_ici, r_ici, _, _ = sems
        x, y, c, chips = _position()
        j = 2 * x + y
        for t in range(n):
            _dma(ins[t].at[layers[t]], block(outs, t, j), s_own.at[t], r_own.at[t], (x, y, 1 - c)).start()
            for k, (px, py) in enumerate(chips):
                _dma(half(ins[t].at[layers[t]], t, c), half(block(outs, t, j), t, c),
                     s_ici.at[3 * t + k], r_ici.at[3 * t + k], (px, py, c)).start()

    def forward(ins, outs, sems):
        _, _, s_ici, r_ici, s_fwd, r_fwd = sems
        x, y, c, chips = _position()
        for t in range(n):
            if axes[t][1] is None:
                continue
            for k, (px, py) in enumerate(chips):
                got = half(block(outs, t, 2 * px + py), t, c)
                _dma(got, got, s_ici.at[3 * t + k], r_ici.at[3 * t + k], (px, py, c)).wait_recv()
                _dma(got, got, s_fwd.at[3 * t + k], r_fwd.at[3 * t + k], (x, y, 1 - c)).start()

    def finish(ins, outs, sems):
        s_own, r_own, s_ici, r_ici, s_fwd, r_fwd = sems
        x, y, c, chips = _position()
        j = 2 * x + y
        sib = (x, y, 1 - c)
        for t in range(n):
            for k, (px, py) in enumerate(chips):
                i = 3 * t + k
                sent = half(block(outs, t, j), t, c)
                _dma(sent, sent, s_ici.at[i], r_ici.at[i], (px, py, c)).wait_send()
                theirs = block(outs, t, 2 * px + py)
                if axes[t][1] is None:
                    _dma(theirs, theirs, s_ici.at[i], r_ici.at[i], (px, py, c)).wait_recv()
                else:
                    got, other = half(theirs, t, c), half(theirs, t, 1 - c)
                    _dma(got, got, s_fwd.at[i], r_fwd.at[i], sib).wait_send()
                    _dma(other, other, s_fwd.at[i], r_fwd.at[i], sib).wait_recv()
            _dma(ins[t].at[layers[t]], block(outs, t, j), s_own.at[t], r_own.at[t], sib).wait()

    return _Hosted(list(shards), out_shapes, _dma_sems(n, n, 3 * n, 3 * n, 3 * n, 3 * n),
                   [("first", start), ("late", forward), ("last", finish)])


def _halved(shape, axis):
    shp = list(shape)
    shp[axis] //= 2
    return tuple(shp)


def _pair_comm(grads, axes):
    n = len(grads)

    def copies(ins, outs, sems):
        x, y, c, _ = _position()
        cps = []
        for t in range(n):
            ha = axes[t][1]
            hs = grads[t].shape[ha] // 2
            cps.append(_dma(_sub(ins[t], ha, (1 - c) * hs, hs), outs[t], sems[0].at[t], sems[1].at[t], (x, y, 1 - c)))
        return cps

    def start(ins, outs, sems):
        for cp in copies(ins, outs, sems):
            cp.start()

    def finish(ins, outs, sems):
        for cp in copies(ins, outs, sems):
            cp.wait()

    return _Hosted(list(grads), [jax.ShapeDtypeStruct(_halved(g.shape, ax[1]), g.dtype) for g, ax in zip(grads, axes)],
                   _dma_sems(n, n), [("first", start), ("last", finish)])


def _block_comm(sums, axes):
    n = len(sums)
    out_shapes = []
    for s, (ba, _) in zip(sums, axes):
        shp = list(s.shape)
        shp[ba] //= N_CHIPS
        out_shapes.append(jax.ShapeDtypeStruct((3,) + tuple(shp), s.dtype))

    def copies(ins, outs, sems):
        x, y, c, chips = _position()
        cps = []
        for t in range(n):
            ba = axes[t][0]
            bs = sums[t].shape[ba] // N_CHIPS
            for k, (px, py) in enumerate(chips):
                cps.append(_dma(_sub(ins[t], ba, (2 * px + py) * bs, bs), outs[t].at[k],
                                sems[0].at[3 * t + k], sems[1].at[3 * t + k], (px, py, c)))
        return cps

    def start(ins, outs, sems):
        for cp in copies(ins, outs, sems):
            cp.start()

    def finish(ins, outs, sems):
        for cp in copies(ins, outs, sems):
            cp.wait()

    return _Hosted(list(sums), out_shapes, _dma_sems(3 * n, 3 * n), [("first", start), ("last", finish)])


def _sibling_share(shards, axes, layer):
    n = len(shards)

    def body(*refs):
        outs = refs[n:2 * n]
        ssem, rsem = refs[2 * n:]
        x, y, c, _ = _position()
        cps = []
        for t in range(n):
            ha = axes[t][1]
            hs = shards[t].shape[1 + ha] // 2
            mine = _sub(outs[t].at[layer], ha, c * hs, hs)
            cps.append(_dma(mine, mine, ssem.at[t], rsem.at[t], (x, y, 1 - c)))
        for cp in cps:
            cp.start()
        for cp in cps:
            cp.wait()

    return pl.pallas_call(
        body, name="sibling_share",
        in_specs=[ANY] * n, out_specs=[ANY] * n,
        out_shape=[jax.ShapeDtypeStruct(s.shape, s.dtype) for s in shards],
        input_output_aliases={t: t for t in range(n)},
        scratch_shapes=_dma_sems(n, n),
    )(*shards)


def _small_allreduce(p):
    R, C = p.shape

    def body(p_ref, o_ref, sib_buf, chip_sums, ssem, rsem):
        x, y, c, chips = _position()
        j = 2 * x + y
        sib = pltpu.make_async_remote_copy(
            src_ref=p_ref, dst_ref=sib_buf, send_sem=ssem.at[0], recv_sem=rsem.at[0],
            device_id=(x, y, 1 - c), device_id_type=MESH)
        sib.start()
        sib.wait()
        chip_sums[j] = p_ref[...] + sib_buf[...]
        cps = [pltpu.make_async_remote_copy(
            src_ref=chip_sums.at[j], dst_ref=chip_sums.at[j], send_sem=ssem.at[1 + k], recv_sem=rsem.at[1 + k],
            device_id=(px, py, c), device_id_type=MESH) for k, (px, py) in enumerate(chips)]
        for cp in cps:
            cp.start()
        for k, (px, py) in enumerate(chips):
            pltpu.make_async_remote_copy(
                src_ref=chip_sums.at[j], dst_ref=chip_sums.at[2 * px + py],
                send_sem=ssem.at[1 + k], recv_sem=rsem.at[1 + k],
                device_id=(px, py, c), device_id_type=MESH).wait()
        o_ref[...] = ((chip_sums[0] + chip_sums[1]) + chip_sums[2]) + chip_sums[3]

    vm = pl.BlockSpec(memory_space=pltpu.VMEM)
    return pl.pallas_call(
        body, name="small_allreduce",
        in_specs=[vm], out_specs=vm, out_shape=jax.ShapeDtypeStruct((R, C), F32),
        scratch_shapes=[pltpu.VMEM((R, C), F32), pltpu.VMEM((N_CHIPS, R, C), F32),
                        pltpu.SemaphoreType.DMA((4,)), pltpu.SemaphoreType.DMA((4,))],
        compiler_params=pltpu.CompilerParams(vmem_limit_bytes=VMEM_LIMIT_BYTES),
    )(p)


def _add_half(g, recv, half_axis, pos):
    B, R, C = recv.shape
    tr = _tile(R, 256)
    tc = _tile(C, 2816)
    nr, nc = R // tr, C // tc
    if half_axis == 1:
        g_map = lambda b, i, j, pos_ref: (b, pos_ref[0] * nr + i, j)
    else:
        g_map = lambda b, i, j, pos_ref: (b, i, pos_ref[0] * nc + j)

    def body(pos_ref, g_ref, r_ref, o_ref, ob_ref):
        del pos_ref
        s = g_ref[...] + r_ref[...]
        o_ref[...] = s
        ob_ref[...] = s.astype(BF16)

    spec = lambda: pl.BlockSpec((None, tr, tc), lambda b, i, j, pos_ref: (b, i, j))
    return pl.pallas_call(
        body, name="add_half",
        grid_spec=pltpu.PrefetchScalarGridSpec(
            num_scalar_prefetch=1, grid=(B, nr, nc),
            in_specs=[pl.BlockSpec((None, tr, tc), g_map), spec()],
            out_specs=[spec(), spec()]),
        out_shape=[jax.ShapeDtypeStruct((B, R, C), F32), jax.ShapeDtypeStruct((B, R, C), BF16)],
        compiler_params=_params("parallel", "parallel", "parallel"),
    )(pos, g, recv)


def _sum_half(own, recv, axes, pos, layer, prev):
    _, B, rs, cs = recv.shape
    ba, ha = axes
    tr = _tile(rs, 128)
    nr = rs // tr

    def body(pos_ref, o_ref, r0_ref, r1_ref, r2_ref, *rest):
        out_ref = rest[-1]
        out_ref[...] = (((o_ref[...] + r0_ref[...].astype(F32)) + r1_ref[...].astype(F32))
                        + r2_ref[...].astype(F32))

    if ba == 2:
        own_spec = pl.BlockSpec((None, tr, cs), lambda b, i, pos_ref: (b, i, pos_ref[1]))
    else:
        own_spec = pl.BlockSpec((None, tr, cs), lambda b, i, pos_ref: (b, pos_ref[1] * nr + i, 0))
    if ha == 1:
        out_shape, out_map = (B, 2 * rs, cs), (lambda b, i, pos_ref: (layer, b, pos_ref[0] * nr + i, 0))
    else:
        out_shape, out_map = (B, rs, 2 * cs), (lambda b, i, pos_ref: (layer, b, i, pos_ref[0]))
    rspec = lambda k: pl.BlockSpec((None, None, tr, cs), lambda b, i, pos_ref: (k, b, i, 0))
    more_in, more_args, aliases = ([ANY], [prev], {5: 0}) if prev is not None else ([], [], {})
    return pl.pallas_call(
        body, name="sum_half",
        grid_spec=pltpu.PrefetchScalarGridSpec(
            num_scalar_prefetch=1, grid=(B, nr),
            in_specs=[own_spec, rspec(0), rspec(1), rspec(2)] + more_in,
            out_specs=pl.BlockSpec((None, None, tr, cs), out_map)),
        out_shape=jax.ShapeDtypeStruct((N_LAYERS,) + out_shape, F32),
        input_output_aliases=aliases,
        compiler_params=_params("parallel", "parallel"),
    )(pos, own, recv, recv, recv, *more_args)


def _adamw(w, g, m, v):
    shape = w.shape
    C = shape[-1]
    R = w.size // C
    tr = R
    for cand in (512, 256, 128, 64, 32, 16, 8):
        if R % cand == 0 and cand * C * 4 <= (1 << 20):
            tr = cand
            break
    bc1 = 1.0 - ADAM_B1 ** ADAM_STEP
    bc2 = 1.0 - ADAM_B2 ** ADAM_STEP

    def body(w_ref, g_ref, m_ref, v_ref, d_ref, mo_ref, vo_ref):
        gg = g_ref[...]
        mn = ADAM_B1 * m_ref[...] + (1.0 - ADAM_B1) * gg
        vn = ADAM_B2 * v_ref[...] + (1.0 - ADAM_B2) * (gg * gg)
        mo_ref[...] = mn
        vo_ref[...] = vn
        d_ref[...] = -ADAM_LR * ((mn / bc1) / (jnp.sqrt(vn / bc2) + ADAM_EPS) + ADAM_WD * w_ref[...])

    spec = lambda: pl.BlockSpec((tr, C), lambda i: (i, 0))
    outs = pl.pallas_call(
        body, name="adamw", grid=(R // tr,),
        in_specs=[spec()] * 4, out_specs=[spec()] * 3,
        out_shape=[jax.ShapeDtypeStruct((R, C), F32)] * 3,
        compiler_params=_params("parallel"),
    )(*[a.reshape(R, C) for a in (w, g, m, v)])
    return tuple(o.reshape(shape) for o in outs)


def kernel(x, mem, norm_g, mem_norm_g, w_in, gmlp_ln_g, gmlp_ln_b, w_s, b_s, conv_w, conv_b, conv_ln_g, conv_ln_b, w_kv, w_branch, w_out, final_norm_g, loss_target, m_norm_g, m_mem_norm_g, m_w_in, m_gmlp_ln_g, m_gmlp_ln_b, m_w_s, m_b_s, m_conv_w, m_conv_b, m_conv_ln_g, m_conv_ln_b, m_w_kv, m_w_branch, m_w_out, m_final_norm_g, v_norm_g, v_mem_norm_g, v_w_in, v_gmlp_ln_g, v_gmlp_ln_b, v_w_s, v_b_s, v_conv_w, v_conv_b, v_conv_ln_g, v_conv_ln_b, v_w_kv, v_w_branch, v_w_out, v_final_norm_g):
    xs, mems, tgt = x[0], mem[0], loss_target[0]
    S, D = xs.shape
    assert D // GROUPS == LANES and S % CHUNK == 0 and w_s.shape[-1] == CHUNK
    L = N_LAYERS

    assert L == 2
    shards = [w_in.astype(BF16)[:, None], w_kv.astype(BF16)[:, None], w_branch.astype(BF16), w_out.astype(BF16)[:, None],
              conv_w[:, None]]
    gather_axes = [(2, 1), (2, 1), (1, 2), (1, 2), (2, None)]
    reduce_axes = gather_axes[:4]
    win0, cw0 = _run_alone(_gather_comm([shards[0], shards[4]], [0, 0], [gather_axes[0], gather_axes[4]]), "gather_first")
    gather_rest = _gather_comm(shards[1:4] + shards, [0] * 3 + [1] * 5, gather_axes[1:4] + gather_axes)

    tri = jnp.tril(jnp.ones((CHUNK, CHUNK), F32))
    t_mix = (w_s * tri).astype(BF16)
    t_mix_t = jnp.swapaxes(t_mix, -1, -2)
    bias_full = jnp.repeat(jnp.swapaxes(b_s, -1, -2), D // GROUPS, axis=-1)
    row = lambda a, l: a[l][None, :]
    xi, yi, ci = lax.axis_index("x"), lax.axis_index("y"), lax.axis_index("c")
    pos = jnp.stack([ci, 2 * xi + yi]).astype(jnp.int32)

    saved = []
    h = xs
    weights = [None] * L
    for l in range(L):
        if l == 0:
            z, hb, c0, fetched = _in_proj_conv(h, row(norm_g, l), win0, cw0, row(conv_b, l), 0, gather_rest)
            weights = [[win0] + list(fetched[:3]) + [cw0], list(fetched[3:])]
        else:
            z, hb, c0, _ = _in_proj_conv(h, row(norm_g, l), weights[l][0], weights[l][4], row(conv_b, l), 0)
        win, wkv, wb, wo, cw = weights[l]
        mn, kv, kvt = _kv_fwd(mems, row(mem_norm_g, l), wkv, 0)
        brc = _branch_c_fwd(z, kv, kvt)
        bra, brb, merged, h_next = _merge_fwd(brc, z, h, c0, row(conv_ln_g, l), row(conv_ln_b, l), row(gmlp_ln_g, l),
                                              row(gmlp_ln_b, l), t_mix[l], bias_full[l], wb[None], wo, 0)
        saved.append((h, z, hb, bra, c0, brb, mn, kv, kvt, brc, merged))
        h = h_next

    dx, loss_cols, d_final_g = _loss_bwd(h, tgt, final_norm_g[None, :])
    loss = lax.psum(0.5 * jnp.sum(loss_cols) / D, ("x", "y", "c"))

    def add_halves(grads, recv, axes):
        return [_add_half(g, r, ax[1], pos) for g, r, ax in zip(grads, recv, axes)]

    def finish_reduction(sums, blocks, axes, layer, prevs):
        halves = [_sum_half(s, b, ax, pos, layer, p) for (s, _), b, ax, p in zip(sums, blocks, axes, prevs)]
        return list(_sibling_share(halves, axes, layer))

    small = {k: [None] * L for k in ("norm_g", "mem_norm_g", "gmlp_ln_g", "gmlp_ln_b", "w_s", "b_s", "conv_w",
                                     "conv_b", "conv_ln_g", "conv_ln_b")}
    early_axes = reduce_axes[1:]
    pending, stacked = None, None
    for l in reversed(range(L)):
        h_in, z, hb, bra, c0, brb, mn, kv, kvt, brc, merged = saved[l]
        win, wkv, wb, wo, cw = weights[l]
        dz_m, dproj, dbr, dc0, dz_b, dcg, dcbeta, dcb, recv = _merge_bwd(
            dx, bra, brb, brc, z, c0, row(conv_ln_g, l), row(conv_ln_b, l), wb[None], wo, 0,
            _pair_comm(pending, reduce_axes) if pending else None)
        sums = add_halves(pending, recv, reduce_axes) if pending else []
        g_o = _wgrad(merged, dx, (1, D, D), 0, 0, None)
        g_b = None
        for n, br in enumerate((bra, brb, brc)):
            g_b = _wgrad(br, dproj, (N_BRANCHES, D, D), n, 0, g_b, b_plane=n)
        dz_a, dws, dbs, dlg, dlb = _branch_a_bwd(z, dbr, row(gmlp_ln_g, l), row(gmlp_ln_b, l),
                                                t_mix[l], t_mix_t[l], bias_full[l], tri)
        dz_c, dk, dv = _branch_c_bwd(z, dbr, kv, kvt)
        g_kv, dmg = _kv_bwd(dk, dv, mn, mems, wkv, 0, None)
        ready = [g_kv, g_b, g_o]
        riding = _both(_block_comm([sb for _, sb in sums], reduce_axes), _pair_comm(ready, early_axes)) if pending else None
        dz_b, dcw, got = _conv_bwd(dc0, z, cw, dz_b, 0, riding)
        sums_early = add_halves(ready, got[4:], early_axes) if pending else []
        if pending:
            stacked = finish_reduction(sums, got[:4], reduce_axes, l + 1, [None] * 4)
        pieces =[(dz_a, 0, 3), (dz_b, 3, 3), (dz_c, 6, 2), (dz_m, 8, 3)]
        g_in = None
        for dzp, lo, _ in pieces:
            g_in = _wgrad(hb, dzp, (1, D, N_PLANES * D), 0, lo * D, g_in)
        if pending:
            riding = _block_comm([sb for _, sb in sums_early], early_axes)
            dx_half, dng_a, got = _in_proj_bwd(pieces, win, h_in, row(norm_g, l), dx, 0,
                                               _both(riding, _pair_comm([g_in], reduce_axes[:1])), (0, 2))
            stacked[1:] = finish_reduction(sums_early, got[:3], early_axes, l, stacked[1:])
            sums_in = add_halves([g_in], got[3:], reduce_axes[:1])
            dx, dng_b, blocks_in = _in_proj_bwd(pieces, win, h_in, row(norm_g, l), dx, 0,
                                                _block_comm([sb for _, sb in sums_in], reduce_axes[:1]), (1, 2), dx_half)
            dng = dng_a + dng_b
            stacked[:1] = finish_reduction(sums_in, blocks_in, reduce_axes[:1], l, stacked[:1])
        else:
            dx, dng, _ = _in_proj_bwd(pieces, win, h_in, row(norm_g, l), dx, 0)
        pending = [g_in, g_kv, g_b, g_o]
        for k, val in (("norm_g", dng), ("mem_norm_g", dmg), ("gmlp_ln_g", dlg), ("gmlp_ln_b", dlb), ("w_s", dws),
                       ("b_s", dbs), ("conv_w", dcw), ("conv_b", dcb), ("conv_ln_g", dcg), ("conv_ln_b", dcbeta)):
            small[k][l] = val
    grad_x = dx[None]

    grad_w_in, grad_w_kv, grad_w_branch, grad_w_out = [
        g.reshape(w.shape) for g, w in zip(stacked, (w_in, w_kv, w_branch, w_out))]

    order = [("norm_g", norm_g.shape), ("mem_norm_g", mem_norm_g.shape), ("gmlp_ln_g", gmlp_ln_g.shape),
             ("gmlp_ln_b", gmlp_ln_b.shape), ("w_s", w_s.shape), ("b_s", b_s.shape),
             ("conv_w", (L, CONV_K, D)), ("conv_b", conv_b.shape), ("conv_ln_g", conv_ln_g.shape),
             ("conv_ln_b", conv_ln_b.shape)]
    parts = [jnp.stack([v.reshape(shp[1:]) for v in small[k]]).reshape(-1, LANES) for k, shp in order]
    parts.append(d_final_g.reshape(-1, LANES))
    sizes = [p.shape[0] for p in parts]
    assert all(s % 8 == 0 for s in sizes)
    total = _small_allreduce(jnp.concatenate(parts, axis=0))
    red, off = {}, 0
    for (k, shp), n in zip(order + [("final_norm_g", final_norm_g.shape)], sizes):
        red[k] = total[off:off + n].reshape(shp)
        off += n
    cs = conv_w.shape[-1]
    red["conv_w"] = lax.dynamic_slice_in_dim(red["conv_w"], (2 * xi + yi) * cs, cs, axis=2)
    red.update(w_in=grad_w_in, w_kv=grad_w_kv, w_branch=grad_w_branch, w_out=grad_w_out)

    names = ["norm_g", "mem_norm_g", "w_in", "gmlp_ln_g", "gmlp_ln_b", "w_s", "b_s", "conv_w", "conv_b",
             "conv_ln_g", "conv_ln_b", "w_kv", "w_branch", "w_out", "final_norm_g"]
    weights = dict(norm_g=norm_g, mem_norm_g=mem_norm_g, w_in=w_in, gmlp_ln_g=gmlp_ln_g, gmlp_ln_b=gmlp_ln_b,
                   w_s=w_s, b_s=b_s, conv_w=conv_w, conv_b=conv_b, conv_ln_g=conv_ln_g, conv_ln_b=conv_ln_b,
                   w_kv=w_kv, w_branch=w_branch, w_out=w_out, final_norm_g=final_norm_g)
    ms = dict(norm_g=m_norm_g, mem_norm_g=m_mem_norm_g, w_in=m_w_in, gmlp_ln_g=m_gmlp_ln_g, gmlp_ln_b=m_gmlp_ln_b,
              w_s=m_w_s, b_s=m_b_s, conv_w=m_conv_w, conv_b=m_conv_b, conv_ln_g=m_conv_ln_g, conv_ln_b=m_conv_ln_b,
              w_kv=m_w_kv, w_branch=m_w_branch, w_out=m_w_out, final_norm_g=m_final_norm_g)
    vs = dict(norm_g=v_norm_g, mem_norm_g=v_mem_norm_g, w_in=v_w_in, gmlp_ln_g=v_gmlp_ln_g, gmlp_ln_b=v_gmlp_ln_b,
              w_s=v_w_s, b_s=v_b_s, conv_w=v_conv_w, conv_b=v_conv_b, conv_ln_g=v_conv_ln_g, conv_ln_b=v_conv_ln_b,
              w_kv=v_w_kv, w_branch=v_w_branch, w_out=v_w_out, final_norm_g=v_final_norm_g)
    deltas, new_m, new_v = [], [], []
    for k in names:
        w2 = weights[k] if weights[k].ndim > 1 else weights[k][None, :]
        d, mo, vo = _adamw(w2, red[k].reshape(w2.shape), ms[k].reshape(w2.shape), vs[k].reshape(w2.shape))
        deltas.append(d.reshape(weights[k].shape))
        new_m.append(mo.reshape(weights[k].shape))
        new_v.append(vo.reshape(weights[k].shape))
    grads = [red[k].reshape(weights[k].shape) for k in names]
    return (loss, grad_x, *grads, *deltas, *new_m, *new_v)
```

```python
import math

import jax
import jax.numpy as jnp
from jax import lax
from jax.experimental import pallas as pl
from jax.experimental.pallas import tpu as pltpu

F32 = jnp.float32
BF16 = jnp.bfloat16

N_LAYERS = 2
N_BRANCHES = 3
N_PLANES = 11
N_CHIPS = 4
CHUNK = 128
GROUPS = 8
HEADS = 4
CONV_K = 31
HALO = 32
LANES = 128
RMS_EPS = 1e-6
LN_EPS = 1e-5
ADAM_LR, ADAM_B1, ADAM_B2, ADAM_EPS, ADAM_WD, ADAM_STEP = 0.001, 0.9, 0.999, 1e-08, 0.01, 10
VMEM_LIMIT_BYTES = 48 * 1024 * 1024
MESH = pl.DeviceIdType.MESH
ANY = pl.BlockSpec(memory_space=pl.ANY)


def _params(*sem):
    return pltpu.CompilerParams(dimension_semantics=sem, vmem_limit_bytes=VMEM_LIMIT_BYTES)


def _tile(n, pref):
    t = min(n, pref)
    assert n % t == 0, (n, t)
    return t


def _dot(a, b):
    return jnp.dot(a, b, preferred_element_type=F32)


def _dot_tb(a, b):
    return lax.dot_general(a, b, (((1,), (1,)), ((), ())), preferred_element_type=F32)


def _dot_ta(a, b):
    return lax.dot_general(a, b, (((0,), (0,)), ((), ())), preferred_element_type=F32)


def _sigmoid(x):
    return 1.0 / (1.0 + jnp.exp(-x))


def _silu(x):
    return x * _sigmoid(x)


def _silu_and_grad(x):
    s = _sigmoid(x)
    return x * s, s * (1.0 + x * (1.0 - s))


_GELU_C = math.sqrt(2.0 / math.pi)
_GELU_A = 0.044715


def _gelu(x):
    return x * (0.5 * (1.0 + jnp.tanh(_GELU_C * (x + _GELU_A * (x * x * x)))))


def _gelu_and_grad(x):
    x2 = x * x
    t = jnp.tanh(_GELU_C * (x + _GELU_A * (x2 * x)))
    cdf = 0.5 * (1.0 + t)
    dcdf = 0.5 * (1.0 - t * t) * (_GELU_C * (1.0 + 3.0 * _GELU_A * x2))
    return x * cdf, cdf + x * dcdf


def _rows(i, n):
    return pl.ds(pl.multiple_of(i * n, n), n)


def _grouped_loop(n, group, body, init):
    assert n % group == 0

    def trip(i, c):
        for u in range(group):
            c = body(i * group + u, c)
        return c
    return lax.fori_loop(0, n // group, trip, init)


def _mean(x):
    return jnp.mean(x, axis=-1, keepdims=True)


def _colsum(x):
    return jnp.sum(x, axis=0, keepdims=True)


def _tok(ts, d):
    return pl.BlockSpec((ts, d), lambda i: (i, 0))


def _zcol(ts, d, n):
    return pl.BlockSpec((ts, d), lambda i: (i, n))


def _row1(d):
    return pl.BlockSpec((1, d), lambda i: (0, 0))


def _plane(ts, d, n):
    return pl.BlockSpec((None, ts, d), lambda i: (n, i, 0))


class _Hosted:
    def __init__(self, ins, out_shapes, sem_shapes, phases):
        self.ins, self.out_shapes, self.sem_shapes, self.phases = ins, out_shapes, sem_shapes, phases


def _host(comm, n_in, n_out, n_scr, grid, body):
    if comm is None:
        return body, [], [], [], [], None
    ci, co = len(comm.ins), len(comm.out_shapes)
    total = math.prod(grid)
    at = {"first": 0, "late": (7 * total) // 8, "last": total - 1}

    def wrapped(*refs):
        ins, cin = refs[:n_in], refs[n_in:n_in + ci]
        o0 = n_in + ci
        outs, cout = refs[o0:o0 + n_out], refs[o0 + n_out:o0 + n_out + co]
        s0 = o0 + n_out + co
        scr, csem = refs[s0:s0 + n_scr], refs[s0 + n_scr:]
        step = pl.program_id(0)
        for a in range(1, len(grid)):
            step = step * grid[a] + pl.program_id(a)

        def run(last):
            for w, fn in comm.phases:
                if (w == "last") == last:
                    @pl.when(step == at.get(w, w))
                    def _(fn=fn):
                        fn(cin, cout, csem)
        run(False)
        body(*ins, *outs, *scr)
        run(True)

    return wrapped, [ANY] * ci, [ANY] * co, list(comm.out_shapes), list(comm.sem_shapes), ("arbitrary",) * len(grid)


SH = HALO - CONV_K + 1


def _fill_shifts(sh_scr, src_scr, cols, ts, base=0):
    for j in range(1, 8):
        for g in range(ts // 32):
            sh_scr[j - 1, pl.ds(g * 32, 32), :] = src_scr[pl.ds(base + g * 32 + j, 32), cols]
        sh_scr[j - 1, pl.ds(ts, HALO - 8), :] = src_scr[pl.ds(base + ts + j, HALO - 8), cols]


def _shifted(sh_scr, src_scr, cols, row0, n, base=0):
    q, j = divmod(row0, 8)
    if j == 0:
        return src_scr[pl.ds(base + row0, n), cols]
    return sh_scr[j - 1, pl.ds(8 * q, n), :]


def _in_proj_conv(x, g, w_all, cw_all, cb, l, comm=None):
    S, D = x.shape
    N = w_all.shape[2]
    tm = _tile(S, 1024)
    QR, CB = 256, 256
    nq = tm // QR
    PA, PB = 3, 4
    assert tm % QR == 0 and PB + 1 + nq <= N // D
    grid = (S // tm, N // D)

    def body(x_ref, g_ref, w_ref, cw_ref, cb_ref, z_ref, hb_ref, c0_ref, h_scr, za_scr, glu_scr, sh_scr):
        i, j = pl.program_id(0), pl.program_id(1)

        @pl.when(j == 0)
        def _():
            def slab(si, c):
                r = _rows(si, 16)
                xs = x_ref[r, :]
                hb = ((xs * lax.rsqrt(_mean(xs * xs) + RMS_EPS)) * g_ref[...]).astype(BF16)
                h_scr[r, :] = hb
                hb_ref[r, :] = hb
                return c
            _grouped_loop(tm // 16, 4, slab, 0)

        def matmul_cols(cols):
            zc = _dot(h_scr[...], w_ref[:, cols])
            z_ref[:, cols] = zc
            return zc

        @pl.when(j == PA)
        def _():
            def colblock(ci, c):
                cols = _rows(ci, CB)
                za_scr[:, cols] = matmul_cols(cols)
                return c
            lax.fori_loop(0, D // CB, colblock, 0)

        @pl.when((j == PB) & (i == 0))
        def _():
            glu_scr[0:HALO, :] = jnp.zeros((HALO, D), F32)

        @pl.when((j == PB) & (i > 0))
        def _():
            glu_scr[0:HALO, :] = glu_scr[tm:tm + HALO, :]

        @pl.when(j == PB)
        def _():
            def colblock(ci, c):
                cols = _rows(ci, CB)
                glu_scr[HALO:HALO + tm, cols] = za_scr[:, cols] * _sigmoid(matmul_cols(cols))
                return c
            lax.fori_loop(0, D // CB, colblock, 0)

        for q in range(nq):
            @pl.when(j == PB + 1 + q)
            def _(q=q):
                def colblock(ci, c):
                    cols = _rows(ci, CB)
                    matmul_cols(cols)
                    _fill_shifts(sh_scr, glu_scr, cols, QR, q * QR)
                    for s in range(QR // 32):
                        acc = jnp.zeros((32, CB), F32)
                        for k in range(CONV_K):
                            acc = acc + _shifted(sh_scr, glu_scr, cols, s * 32 + k + SH, 32, q * QR) * cw_ref[k:k + 1, cols]
                        c0_ref[pl.ds(q * QR + s * 32, 32), cols] = acc + cb_ref[:, cols]
                    return c
                lax.fori_loop(0, D // CB, colblock, 0)

        @pl.when((j < PA) | (j > PB + nq))
        def _():
            z_ref[...] = _dot(h_scr[...], w_ref[...])

    tile = lambda: pl.BlockSpec((tm, D), lambda i, j: (i, 0))
    body, c_in, c_out, c_shapes, c_scr, _ = _host(comm, 5, 3, 4, grid, body)
    outs = pl.pallas_call(
        body, name="in_proj_conv", grid=grid,
        in_specs=[tile(), pl.BlockSpec((1, D), lambda i, j: (0, 0)),
                  pl.BlockSpec((None, D, D), lambda i, j: (l, 0, j)),
                  pl.BlockSpec((None, CONV_K, D), lambda i, j: (l, 0, 0)),
                  pl.BlockSpec((1, D), lambda i, j: (0, 0))] + c_in,
        out_specs=[pl.BlockSpec((tm, D), lambda i, j: (i, j)), tile(), tile()] + c_out,
        out_shape=[jax.ShapeDtypeStruct((S, N), F32), jax.ShapeDtypeStruct((S, D), BF16),
                   jax.ShapeDtypeStruct((S, D), F32)] + c_shapes,
        scratch_shapes=[pltpu.VMEM((tm, D), BF16), pltpu.VMEM((tm, D), F32), pltpu.VMEM((tm + HALO, D), F32),
                        pltpu.VMEM((7, QR + HALO, CB), F32)] + c_scr,
        compiler_params=_params("arbitrary", "arbitrary"),
    )(x, g, w_all, cw_all, cb, *(comm.ins if comm else []))
    return outs[0], outs[1], outs[2], outs[3:]


def _kv_fwd(mem, mg, wkv_all, l):
    M, D = mem.shape

    def body(mem_ref, g_ref, w_ref, mn_ref, kv_ref, kvt_ref):
        m = mem_ref[...]
        mn = ((m * lax.rsqrt(_mean(m * m) + RMS_EPS)) * g_ref[...]).astype(BF16)
        mn_ref[...] = mn
        kv = _dot(mn, w_ref[...])
        kv_ref[...] = kv.astype(BF16)
        kvt_ref[...] = kv.T.astype(BF16)

    return pl.pallas_call(
        body, name="kv_fwd", grid=(1,),
        in_specs=[pl.BlockSpec((M, D), lambda i: (0, 0)), _row1(D),
                  pl.BlockSpec((None, D, 2 * D), lambda i: (l, 0, 0))],
        out_specs=[pl.BlockSpec((M, D), lambda i: (0, 0)), pl.BlockSpec((M, 2 * D), lambda i: (0, 0)),
                   pl.BlockSpec((2 * D, M), lambda i: (0, 0))],
        out_shape=[jax.ShapeDtypeStruct((M, D), BF16), jax.ShapeDtypeStruct((M, 2 * D), BF16),
                   jax.ShapeDtypeStruct((2 * D, M), BF16)],
        compiler_params=_params("arbitrary"),
    )(mem, mg, wkv_all)


def _softmax_rows(s):
    e = jnp.exp(s - jnp.max(s, axis=-1, keepdims=True))
    return e / jnp.sum(e, axis=-1, keepdims=True)


def _branch_c_fwd(z, kv, kvt):
    S = z.shape[0]
    M, D2 = kv.shape
    D = D2 // 2
    HD = D // HEADS
    scale = 1.0 / math.sqrt(HD)
    ts = _tile(S, 256)

    def body(q_ref, zg_ref, kt_ref, v_ref, o_ref):
        for h in range(HEADS):
            cols = slice(h * HD, (h + 1) * HD)
            p = _softmax_rows(_dot(q_ref[:, cols].astype(BF16), kt_ref[cols, :]) * scale)
            att = _dot(p.astype(BF16), v_ref[:, cols])
            o_ref[:, cols] = (att * _silu(zg_ref[:, cols])).astype(BF16)

    return pl.pallas_call(
        body, name="branch_c_fwd", grid=(S // ts,),
        in_specs=[_zcol(ts, D, 6), _zcol(ts, D, 7),
                  pl.BlockSpec((D, M), lambda i: (0, 0)), pl.BlockSpec((M, D), lambda i: (0, 1))],
        out_specs=_tok(ts, D),
        out_shape=jax.ShapeDtypeStruct((S, D), BF16),
        compiler_params=_params("parallel"),
    )(z, z, kvt, kv)


def _merge_fwd(brc, z, x, c0, cg, cb, lng, lnb, t_mix, bias_full, wb_all, wo_all, l):
    S, D = x.shape
    GD = D // GROUPS
    ts = _tile(S, 256)
    CB = 256

    def body(c_ref, zm0_ref, zm1_ref, zm2_ref, x_ref, c0_ref, zbg_ref, cg_ref, cb_ref, zu_ref, zv_ref, zag_ref,
             lng_ref, lnb_ref, t_ref, bias_ref, wb_ref, wo_ref, a_ref, b_ref, m_ref, xo_ref, v_scr):
        for s in range(ts // 16):
            r = pl.ds(s * 16, 16)
            v = c0_ref[r, :]
            xc = v - _mean(v)
            c1 = (xc * lax.rsqrt(_mean(xc * xc) + LN_EPS)) * cg_ref[...] + cb_ref[...]
            b_ref[r, :] = (_silu(c1) * _silu(zbg_ref[r, :])).astype(BF16)
        for ch in range(ts // CHUNK):
            for s in range(CHUNK // 16):
                vg = _gelu(zv_ref[pl.ds(ch * CHUNK + s * 16, 16), :])
                xc = vg - _mean(vg)
                y = xc * lax.rsqrt(_mean(xc * xc) + LN_EPS)
                v_scr[pl.ds(ch * CHUNK + s * 16, 16), :] = (y * lng_ref[...] + lnb_ref[...]).astype(BF16)
            rows = pl.ds(ch * CHUNK, CHUNK)
            for g in range(GROUPS):
                cols = slice(g * GD, (g + 1) * GD)
                sv = _dot(t_ref[g], v_scr[rows, cols]) + bias_ref[:, cols]
                u = _gelu(zu_ref[rows, cols])
                a_ref[rows, cols] = ((u * sv) * _silu(zag_ref[rows, cols])).astype(BF16)

        brs = (a_ref, b_ref, c_ref)
        zms = (zm0_ref, zm1_ref, zm2_ref)
        for cb in range(D // CB):
            cols = slice(cb * CB, (cb + 1) * CB)
            acc = None
            for n in range(N_BRANCHES):
                t = _sigmoid(zms[n][:, cols]) * _dot(brs[n][...], wb_ref[n, :, cols])
                acc = t if acc is None else acc + t
            m_ref[:, cols] = acc.astype(BF16)
        for cb in range(D // CB):
            cols = slice(cb * CB, (cb + 1) * CB)
            xo_ref[:, cols] = x_ref[:, cols] + _dot(m_ref[...], wo_ref[:, cols])

    return pl.pallas_call(
        body, name="merge_fwd", grid=(S // ts,),
        in_specs=[_tok(ts, D), _zcol(ts, D, 8), _zcol(ts, D, 9), _zcol(ts, D, 10), _tok(ts, D),
                  _tok(ts, D), _zcol(ts, D, 5), _row1(D), _row1(D),
                  _zcol(ts, D, 0), _zcol(ts, D, 1), _zcol(ts, D, 2), _row1(D), _row1(D),
                  pl.BlockSpec((GROUPS, CHUNK, CHUNK), lambda i: (0, 0, 0)),
                  pl.BlockSpec((CHUNK, D), lambda i: (0, 0)),
                  pl.BlockSpec((None, N_BRANCHES, D, D), lambda i: (l, 0, 0, 0)),
                  pl.BlockSpec((None, D, D), lambda i: (l, 0, 0))],
        out_specs=[_tok(ts, D), _tok(ts, D), _tok(ts, D), _tok(ts, D)],
        out_shape=[jax.ShapeDtypeStruct((S, D), BF16), jax.ShapeDtypeStruct((S, D), BF16),
                   jax.ShapeDtypeStruct((S, D), BF16), jax.ShapeDtypeStruct((S, D), F32)],
        scratch_shapes=[pltpu.VMEM((ts, D), BF16)],
        compiler_params=_params("parallel"),
    )(brc, z, z, z, x, c0, z, cg, cb, z, z, z, lng, lnb, t_mix, bias_full, wb_all, wo_all)


def _loss_bwd(x, target, fg):
    S, D = x.shape
    ts = _tile(S, 512)

    def body(x_ref, t_ref, g_ref, dx_ref, ls_ref, dg_ref):
        @pl.when(pl.program_id(0) == 0)
        def _():
            ls_ref[...] = jnp.zeros_like(ls_ref)
            dg_ref[...] = jnp.zeros_like(dg_ref)

        def slab(si, c):
            ls, dg = c
            r = _rows(si, 16)
            xs = x_ref[r, :]
            rs = lax.rsqrt(_mean(xs * xs) + RMS_EPS)
            n = xs * rs
            e = n * g_ref[...] - t_ref[r, :]
            dy = e * (1.0 / D)
            dn = dy * g_ref[...]
            dx_ref[r, :] = rs * (dn - n * _mean(dn * n))
            return ls + _colsum(e * e), dg + _colsum(dy * n)
        zero = jnp.zeros((1, D), F32)
        ls, dg = _grouped_loop(ts // 16, 4, slab, (zero, zero))
        ls_ref[...] += ls
        dg_ref[...] += dg

    return pl.pallas_call(
        body, name="loss_bwd", grid=(S // ts,),
        in_specs=[_tok(ts, D), _tok(ts, D), _row1(D)],
        out_specs=[_tok(ts, D), _row1(D), _row1(D)],
        out_shape=[jax.ShapeDtypeStruct((S, D), F32), jax.ShapeDtypeStruct((1, D), F32),
                   jax.ShapeDtypeStruct((1, D), F32)],
        compiler_params=_params("arbitrary"),
    )(x, target, fg)


def _merge_bwd(dxp, bra, brb, brc, z, c0, lng, lnb, wb_all, wo_all, l, comm=None):
    S, D = dxp.shape
    ts = _tile(S, 256)
    CB = 256

    def body(d_ref, a_ref, b_ref, c_ref, zm0_ref, zm1_ref, zm2_ref, c0_ref, zg_ref, g_ref, be_ref, wb_ref, wo_ref,
             dzm_ref, dproj_ref, dbr_ref, dc0_ref, dzb_ref, dg_ref, db_ref, dcb_ref, dxb_scr, dbb_scr):
        @pl.when(pl.program_id(0) == 0)
        def _():
            dg_ref[...] = jnp.zeros_like(dg_ref)
            db_ref[...] = jnp.zeros_like(db_ref)
            dcb_ref[...] = jnp.zeros_like(dcb_ref)

        brs = (a_ref, b_ref, c_ref)
        zms = (zm0_ref, zm1_ref, zm2_ref)
        dxb_scr[...] = d_ref[...].astype(BF16)
        for cb in range(D // CB):
            cols = slice(cb * CB, (cb + 1) * CB)
            dm = _dot_tb(dxb_scr[...], wo_ref[cols, :])
            for n in range(N_BRANCHES):
                proj = _dot(brs[n][...], wb_ref[n, :, cols])
                g = _sigmoid(zms[n][:, cols])
                dzm_ref[:, n * D + cb * CB:n * D + (cb + 1) * CB] = ((dm * proj) * (g * (1.0 - g))).astype(BF16)
                dproj_ref[n, :, cols] = (dm * g).astype(BF16)
        for cb in range(D // CB):
            cols = slice(cb * CB, (cb + 1) * CB)
            dbb_scr[:, cols] = _dot_tb(dproj_ref[1], wb_ref[1, cols, :])

        dg = db = dcb = jnp.zeros((1, D), F32)
        for s in range(ts // 16):
            r = pl.ds(s * 16, 16)
            v = c0_ref[r, :]
            xc = v - _mean(v)
            rstd = lax.rsqrt(_mean(xc * xc) + LN_EPS)
            cn = xc * rstd
            c1 = cn * g_ref[...] + be_ref[...]
            c2, dc2_dc1 = _silu_and_grad(c1)
            sg, dsg = _silu_and_grad(zg_ref[r, :])
            d = dbb_scr[r, :]
            dzb_ref[r, :] = ((d * c2) * dsg).astype(BF16)
            dc1 = (d * sg) * dc2_dc1
            dcn = dc1 * g_ref[...]
            dc0 = rstd * ((dcn - _mean(dcn)) - cn * _mean(dcn * cn))
            dc0_ref[r, :] = dc0
            dg, db, dcb = dg + _colsum(dc1 * cn), db + _colsum(dc1), dcb + _colsum(dc0)
        dg_ref[...] += dg
        db_ref[...] += db
        dcb_ref[...] += dcb

        for plane, n in enumerate((0, 2)):
            for cb in range(D // CB):
                cols = slice(cb * CB, (cb + 1) * CB)
                dbr_ref[plane, :, cols] = _dot_tb(dproj_ref[n], wb_ref[n, cols, :])

    tokn = lambda n: pl.BlockSpec((n, ts, D), lambda i: (0, i, 0))
    grid = (S // ts,)
    body, c_in, c_out, c_shapes, c_scr, _ = _host(comm, 13, 8, 2, grid, body)
    outs = pl.pallas_call(
        body, name="merge_bwd", grid=grid,
        in_specs=[_tok(ts, D), _tok(ts, D), _tok(ts, D), _tok(ts, D),
                  _zcol(ts, D, 8), _zcol(ts, D, 9), _zcol(ts, D, 10), _tok(ts, D), _zcol(ts, D, 5), _row1(D), _row1(D),
                  pl.BlockSpec((None, N_BRANCHES, D, D), lambda i: (l, 0, 0, 0)),
                  pl.BlockSpec((None, D, D), lambda i: (l, 0, 0))] + c_in,
        out_specs=[_tok(ts, 3 * D), tokn(N_BRANCHES), tokn(2), _tok(ts, D), _zcol(ts, D, 2),
                   _row1(D), _row1(D), _row1(D)] + c_out,
        out_shape=[jax.ShapeDtypeStruct((S, 3 * D), BF16), jax.ShapeDtypeStruct((N_BRANCHES, S, D), BF16),
                   jax.ShapeDtypeStruct((2, S, D), F32), jax.ShapeDtypeStruct((S, D), F32),
                   jax.ShapeDtypeStruct((S, 3 * D), BF16), jax.ShapeDtypeStruct((1, D), F32),
                   jax.ShapeDtypeStruct((1, D), F32), jax.ShapeDtypeStruct((1, D), F32)] + c_shapes,
        scratch_shapes=[pltpu.VMEM((ts, D), BF16), pltpu.VMEM((ts, D), F32)] + c_scr,
        compiler_params=_params("arbitrary"),
    )(dxp, bra, brb, brc, z, z, z, c0, z, lng, lnb, wb_all, wo_all, *(comm.ins if comm else []))
    return (*outs[:8], outs[8:])


def _branch_a_bwd(z, dbr, lng, lnb, t_mix, t_mix_t, bias_full, mask):
    S = z.shape[0]
    D = lng.shape[1]
    GD = D // GROUPS
    ts = _tile(S, 256)
    nt = S // ts

    def body(zu_ref, zv_ref, zg_ref, d_ref, lng_ref, lnb_ref, t_ref, tt_ref, bias_ref, mask_ref,
             dz_ref, dws_ref, dbs_ref, dlg_ref, dlb_ref,
             v_scr, vn_scr, rstd_scr, dv_scr, dsv_acc):
        i = pl.program_id(0)

        @pl.when(i == 0)
        def _():
            dws_ref[...] = jnp.zeros_like(dws_ref)
            dlg_ref[...] = jnp.zeros_like(dlg_ref)
            dlb_ref[...] = jnp.zeros_like(dlb_ref)
            dsv_acc[...] = jnp.zeros_like(dsv_acc)

        def chunk(ci, carry):
            c0 = pl.multiple_of(ci * CHUNK, CHUNK)

            def ln_fwd(si, c):
                r = pl.multiple_of(si * 16, 16)
                vg = _gelu(zv_ref[pl.ds(pl.multiple_of(c0 + r, 16), 16), :])
                xc = vg - _mean(vg)
                rstd = lax.rsqrt(_mean(xc * xc) + LN_EPS)
                vn = xc * rstd
                vn_scr[pl.ds(r, 16), :] = vn
                rstd_scr[pl.ds(r, 16), :] = rstd
                v_scr[pl.ds(r, 16), :] = (vn * lng_ref[...] + lnb_ref[...]).astype(BF16)
                return c
            _grouped_loop(CHUNK // 16, 4, ln_fwd, 0)

            for g in range(GROUPS):
                cols = slice(g * GD, (g + 1) * GD)
                rows = pl.ds(c0, CHUNK)
                vb = v_scr[:, cols]
                sv = _dot(t_ref[g], vb) + bias_ref[:, cols]
                u, du = _gelu_and_grad(zu_ref[rows, cols])
                sg, dsg = _silu_and_grad(zg_ref[rows, cols])
                d = d_ref[rows, cols]
                dsv = (d * u) * sg
                dz_ref[rows, g * GD:(g + 1) * GD] = (((d * sv) * sg) * du).astype(BF16)
                dz_ref[rows, 2 * D + g * GD:2 * D + (g + 1) * GD] = (((d * u) * sv) * dsg).astype(BF16)
                dsvb = dsv.astype(BF16)
                dws_ref[g] += _dot_tb(dsvb, vb)
                dv_scr[:, cols] = _dot(tt_ref[g], dsvb)
                dsv_acc[:, cols] += dsv

            def ln_bwd(si, c):
                dlg, dlb = c
                r = pl.multiple_of(si * 16, 16)
                rr = pl.ds(r, 16)
                zrows = pl.ds(pl.multiple_of(c0 + r, 16), 16)
                dv = dv_scr[rr, :]
                vn = vn_scr[rr, :]
                dvn = dv * lng_ref[...]
                dvg = rstd_scr[rr, :] * ((dvn - _mean(dvn)) - vn * _mean(dvn * vn))
                _, gg = _gelu_and_grad(zv_ref[zrows, :])
                dz_ref[zrows, D:2 * D] = (dvg * gg).astype(BF16)
                return dlg + _colsum(dv * vn), dlb + _colsum(dv)
            zero = jnp.zeros((1, D), F32)
            dlg, dlb = _grouped_loop(CHUNK // 16, 4, ln_bwd, (zero, zero))
            dlg_ref[...] += dlg
            dlb_ref[...] += dlb
            return carry
        lax.fori_loop(0, ts // CHUNK, chunk, 0)

        @pl.when(i == nt - 1)
        def _():
            for g in range(GROUPS):
                cols = slice(g * GD, (g + 1) * GD)
                dws_ref[g] = dws_ref[g] * mask_ref[...]
                dbs_ref[g:g + 1, :] = _colsum(dsv_acc[:, cols].T)

    c2 = lambda shape: pl.BlockSpec(shape, lambda i: (0, 0))
    c3 = lambda: pl.BlockSpec((GROUPS, CHUNK, CHUNK), lambda i: (0, 0, 0))
    return pl.pallas_call(
        body, name="branch_a_bwd", grid=(nt,),
        in_specs=[_zcol(ts, D, 0), _zcol(ts, D, 1), _zcol(ts, D, 2), _plane(ts, D, 0), _row1(D), _row1(D),
                  c3(), c3(), c2((CHUNK, D)), c2((CHUNK, CHUNK))],
        out_specs=[_tok(ts, 3 * D), c3(), c2((GROUPS, CHUNK)), _row1(D), _row1(D)],
        out_shape=[jax.ShapeDtypeStruct((S, 3 * D), BF16), jax.ShapeDtypeStruct((GROUPS, CHUNK, CHUNK), F32),
                   jax.ShapeDtypeStruct((GROUPS, CHUNK), F32), jax.ShapeDtypeStruct((1, D), F32),
                   jax.ShapeDtypeStruct((1, D), F32)],
        scratch_shapes=[pltpu.VMEM((CHUNK, D), BF16), pltpu.VMEM((CHUNK, D), F32), pltpu.VMEM((CHUNK, 1), F32),
                        pltpu.VMEM((CHUNK, D), F32), pltpu.VMEM((CHUNK, D), F32)],
        compiler_params=_params("arbitrary"),
    )(z, z, z, dbr, lng, lnb, t_mix, t_mix_t, bias_full, mask)


def _conv_bwd(dc0, z, w_all, dz_b, l, comm=None):
    S, D = dc0.shape
    ts = _tile(S, 256)
    nt = S // ts
    CB = 256
    hb = ts // HALO
    nh = S // HALO

    def body(dc_ref, dcn_ref, za_ref, zb_ref, w_ref, dzin_ref, dz_ref, dw_ref, dc_scr, dw_acc, dsh_scr):
        del dzin_ref
        i = pl.program_id(0)

        @pl.when(i == 0)
        def _():
            dw_acc[...] = jnp.zeros_like(dw_acc)

        dc_scr[ts:ts + HALO, :] = jnp.where(i == nt - 1, 0.0, dcn_ref[...])

        def fill(si, c):
            r = _rows(si, 32)
            dc_scr[r, :] = dc_ref[r, :]
            return c
        _grouped_loop(ts // 32, 2, fill, 0)

        def colblock(ci, c):
            cols = _rows(ci, CB)
            cols_b = pl.ds(pl.multiple_of(D + ci * CB, CB), CB)
            _fill_shifts(dsh_scr, dc_scr, cols, ts)
            for s in range(ts // 32):
                rows = pl.ds(s * 32, 32)
                za = za_ref[rows, cols]
                sg = _sigmoid(zb_ref[rows, cols])
                glu = za * sg
                acc = jnp.zeros((32, CB), F32)
                for k in range(CONV_K):
                    dcs = _shifted(dsh_scr, dc_scr, cols, s * 32 + CONV_K - 1 - k, 32)
                    prod = glu * dcs
                    dw_acc[k, :, cols] += (prod[0:8] + prod[8:16]) + (prod[16:24] + prod[24:32])
                    acc = acc + dcs * w_ref[k:k + 1, cols]
                dz_ref[rows, cols] = (acc * sg).astype(BF16)
                dz_ref[rows, cols_b] = ((acc * za) * (sg * (1.0 - sg))).astype(BF16)
            return c
        lax.fori_loop(0, D // CB, colblock, 0)

        @pl.when(i == nt - 1)
        def _():
            for k in range(CONV_K):
                dw_ref[k:k + 1, :] = _colsum(dw_acc[k])

    grid = (nt,)
    body, c_in, c_out, c_shapes, c_scr, _ = _host(comm, 6, 2, 3, grid, body)
    outs = pl.pallas_call(
        body, name="conv_bwd", grid=grid,
        in_specs=[_tok(ts, D), pl.BlockSpec((HALO, D), lambda i: (jnp.minimum((i + 1) * hb, nh - 1), 0)),
                  _zcol(ts, D, 3), _zcol(ts, D, 4),
                  pl.BlockSpec((None, CONV_K, D), lambda i: (l, 0, 0)), ANY] + c_in,
        out_specs=[_tok(ts, 2 * D), pl.BlockSpec((CONV_K, D), lambda i: (0, 0))] + c_out,
        out_shape=[jax.ShapeDtypeStruct((S, 3 * D), BF16), jax.ShapeDtypeStruct((CONV_K, D), F32)] + c_shapes,
        scratch_shapes=[pltpu.VMEM((ts + HALO, D), F32), pltpu.VMEM((CONV_K, 8, D), F32),
                        pltpu.VMEM((7, ts + HALO, CB), F32)] + c_scr,
        input_output_aliases={5: 0},
        compiler_params=_params("arbitrary"),
    )(dc0, dc0, z, z, w_all, dz_b, *(comm.ins if comm else []))
    return outs[0], outs[1], outs[2:]


def _branch_c_bwd(z, dbr, kv, kvt):
    S = z.shape[0]
    M, D2 = kv.shape
    D = D2 // 2
    HD = D // HEADS
    scale = 1.0 / math.sqrt(HD)
    ts = _tile(S, 256)

    def body(q_ref, zg_ref, d_ref, kt_ref, vt_ref, k_ref, v_ref, dz_ref, dk_ref, dv_ref):
        @pl.when(pl.program_id(0) == 0)
        def _():
            dk_ref[...] = jnp.zeros_like(dk_ref)
            dv_ref[...] = jnp.zeros_like(dv_ref)

        for h in range(HEADS):
            cols = slice(h * HD, (h + 1) * HD)
            qb = q_ref[:, cols].astype(BF16)
            p = _softmax_rows(_dot(qb, kt_ref[cols, :]) * scale)
            pb = p.astype(BF16)
            att = _dot(pb, v_ref[:, cols])
            sg, dsg = _silu_and_grad(zg_ref[:, cols])
            d = d_ref[:, cols]
            dz_ref[:, D + h * HD:D + (h + 1) * HD] = ((d * att) * dsg).astype(BF16)
            datt = (d * sg).astype(BF16)
            dp = _dot(datt, vt_ref[cols, :])
            dv_ref[:, cols] += _dot_ta(pb, datt)
            ds = ((p * (dp - jnp.sum(dp * p, axis=-1, keepdims=True))) * scale).astype(BF16)
            dz_ref[:, cols] = _dot(ds, k_ref[:, cols]).astype(BF16)
            dk_ref[:, cols] += _dot_ta(ds, qb)

    return pl.pallas_call(
        body, name="branch_c_bwd", grid=(S // ts,),
        in_specs=[_zcol(ts, D, 6), _zcol(ts, D, 7), _plane(ts, D, 1),
                  pl.BlockSpec((D, M), lambda i: (0, 0)), pl.BlockSpec((D, M), lambda i: (1, 0)),
                  pl.BlockSpec((M, D), lambda i: (0, 0)), pl.BlockSpec((M, D), lambda i: (0, 1))],
        out_specs=[_tok(ts, 2 * D), pl.BlockSpec((M, D), lambda i: (0, 0)), pl.BlockSpec((M, D), lambda i: (0, 0))],
        out_shape=[jax.ShapeDtypeStruct((S, 2 * D), BF16), jax.ShapeDtypeStruct((M, D), F32),
                   jax.ShapeDtypeStruct((M, D), F32)],
        compiler_params=_params("arbitrary"),
    )(z, z, dbr, kvt, kvt, kv, kv)


def _kv_bwd(dk, dv, mn, mem, wkv_all, l, prev):
    M, D = mem.shape

    def body(*refs):
        dk_ref, dv_ref, mn_ref, mem_ref, w_ref = refs[:5]
        dw_ref, dg_ref = refs[-2:]
        dkb = dk_ref[...].astype(BF16)
        dvb = dv_ref[...].astype(BF16)
        dw_ref[:, 0:D] = _dot_ta(mn_ref[...], dkb)
        dw_ref[:, D:2 * D] = _dot_ta(mn_ref[...], dvb)
        dmn = _dot_tb(dkb, w_ref[:, 0:D]) + _dot_tb(dvb, w_ref[:, D:2 * D])
        m = mem_ref[...]
        dg_ref[...] = _colsum(dmn * (m * lax.rsqrt(_mean(m * m) + RMS_EPS)))

    full = lambda shape: pl.BlockSpec(shape, lambda i: (0, 0))
    in_specs = [full((M, D)), full((M, D)), full((M, D)), full((M, D)),
                pl.BlockSpec((None, D, 2 * D), lambda i: (l, 0, 0))]
    args = [dk, dv, mn, mem, wkv_all]
    aliases = {}
    if prev is not None:
        in_specs.append(ANY)
        args.append(prev)
        aliases = {5: 0}
    return pl.pallas_call(
        body, name="kv_bwd", grid=(1,),
        in_specs=in_specs,
        out_specs=[pl.BlockSpec((None, D, 2 * D), lambda i: (l, 0, 0)), _row1(D)],
        out_shape=[jax.ShapeDtypeStruct((wkv_all.shape[0], D, 2 * D), F32), jax.ShapeDtypeStruct((1, D), F32)],
        input_output_aliases=aliases,
        compiler_params=_params("arbitrary"),
    )(*args)


def _in_proj_bwd(pieces, w_all, x, g, dxp, l, comm=None, part=(0, 1), prev=None):
    S, D = x.shape
    tm = _tile(S, 1024)
    n_p = len(pieces)
    n_i = (S // tm) // part[1]
    i0 = part[0] * n_i
    n_in = n_p + 4 + (prev is not None)

    def body(*refs):
        dz_refs = refs[:n_p]
        w_ref, x_hbm, g_ref, dxp_hbm = refs[n_p:n_p + 4]
        dx_ref, dg_ref, acc, x_ref, dxp_ref, sem = refs[n_in:]
        i = pl.program_id(0)
        k = pl.program_id(1)
        rows = pl.ds(pl.multiple_of((i + i0) * tm, tm), tm)
        fetch = [pltpu.make_async_copy(x_hbm.at[rows, :], x_ref, sem.at[0]),
                 pltpu.make_async_copy(dxp_hbm.at[rows, :], dxp_ref, sem.at[1])]

        @pl.when(k == N_PLANES - 2)
        def _():
            for cp in fetch:
                cp.start()

        @pl.when(k == 0)
        def _():
            acc[...] = jnp.zeros_like(acc)

        @pl.when((i == 0) & (k == 0))
        def _():
            dg_ref[...] = jnp.zeros_like(dg_ref)

        for (_, lo, n), r in zip(pieces, dz_refs):
            @pl.when((k >= lo) & (k < lo + n))
            def _(r=r):
                acc[...] += _dot_tb(r[...], w_ref[...])

        @pl.when(k == N_PLANES - 1)
        def _():
            for cp in fetch:
                cp.wait()

            def slab(si, c):
                rr = _rows(si, 16)
                xs = x_ref[rr, :]
                rs = lax.rsqrt(_mean(xs * xs) + RMS_EPS)
                n = xs * rs
                dh = acc[rr, :]
                dn = dh * g_ref[...]
                dx_ref[rr, :] = rs * (dn - n * _mean(dn * n)) + dxp_ref[rr, :]
                return c + _colsum(dh * n)
            dg_ref[...] += _grouped_loop(tm // 16, 4, slab, jnp.zeros((1, D), F32))

    def piece_spec(lo, n):
        return pl.BlockSpec((tm, D), lambda i, k: (i + i0, jnp.clip(k - lo, 0, n - 1)))

    tokk = lambda: pl.BlockSpec((tm, D), lambda i, k: (i + i0, 0))
    rowk = lambda: pl.BlockSpec((1, D), lambda i, k: (0, 0))
    grid = (n_i, N_PLANES)
    body, c_in, c_out, c_shapes, c_scr, _ = _host(comm, n_in, 2, 4, grid, body)
    more_in, more_args, aliases = ([ANY], [prev], {n_in - 1: 0}) if prev is not None else ([], [], {})
    outs = pl.pallas_call(
        body, name="in_proj_bwd", grid=grid,
        in_specs=[piece_spec(lo, n) for _, lo, n in pieces]
        + [pl.BlockSpec((None, D, D), lambda i, k: (l, 0, k)), ANY, rowk(), ANY] + more_in + c_in,
        out_specs=[tokk(), rowk()] + c_out,
        out_shape=[jax.ShapeDtypeStruct((S, D), F32), jax.ShapeDtypeStruct((1, D), F32)] + c_shapes,
        scratch_shapes=[pltpu.VMEM((tm, D), F32), pltpu.VMEM((tm, D), F32), pltpu.VMEM((tm, D), F32),
                        pltpu.SemaphoreType.DMA((2,))] + c_scr,
        input_output_aliases=aliases,
        compiler_params=_params("arbitrary", "arbitrary"),
    )(*[p for p, _, _ in pieces], w_all, x, g, dxp, *more_args, *(comm.ins if comm else []))
    return outs[0], outs[1], outs[2:]


def _wgrad(a, b, out_shape, plane, col_off, prev, a_plane=None, b_plane=None):
    S = a.shape[-2]
    K1 = a.shape[-1]
    nb_cols = b.shape[-1]
    ts = _tile(S, 2048)
    tn = _tile(nb_cols, 1024)
    col_off = col_off // tn

    def body(*refs):
        a_ref, b_ref = refs[:2]
        o_ref = refs[-1]
        k = pl.program_id(1)
        prod = _dot_ta(a_ref[...].astype(BF16), b_ref[...].astype(BF16))

        @pl.when(k == 0)
        def _():
            o_ref[...] = prod

        @pl.when(k > 0)
        def _():
            o_ref[...] += prod

    if a_plane is None:
        a_spec = pl.BlockSpec((ts, K1), lambda j, k: (k, 0))
    else:
        a_spec = pl.BlockSpec((None, ts, K1), lambda j, k: (a_plane, k, 0))
    if b_plane is None:
        b_spec = pl.BlockSpec((ts, tn), lambda j, k: (k, j))
    else:
        b_spec = pl.BlockSpec((None, ts, tn), lambda j, k: (b_plane, k, j))
    in_specs, args, aliases = [a_spec, b_spec], [a, b], {}
    if prev is not None:
        in_specs.append(ANY)
        args.append(prev)
        aliases = {2: 0}
    return pl.pallas_call(
        body, name="wgrad", grid=(nb_cols // tn, S // ts),
        in_specs=in_specs,
        out_specs=pl.BlockSpec((None, K1, tn), lambda j, k: (plane, 0, col_off + j)),
        out_shape=jax.ShapeDtypeStruct(out_shape, F32),
        input_output_aliases=aliases,
        compiler_params=_params("parallel", "arbitrary"),
    )(*args)


def _position():
    x, y, c = lax.axis_index("x"), lax.axis_index("y"), lax.axis_index("c")
    chips = [(1 - x, y), (x, 1 - y), (1 - x, 1 - y)]
    return x, y, c, chips


def _sub(ref, axis, start, size):
    idx = [slice(None)] * len(ref.shape)
    idx[axis] = pl.ds(start, size)
    return ref.at[tuple(idx)]


def _dma(src, dst, ssem, rsem, to):
    return pltpu.make_async_remote_copy(src_ref=src, dst_ref=dst, send_sem=ssem, recv_sem=rsem,
                                        device_id=to, device_id_type=MESH)


def _dma_sems(*counts):
    return [pltpu.SemaphoreType.DMA((n,)) for n in counts]


def _both(a, b):
    na, oa, sa = len(a.ins), len(a.out_shapes), len(a.sem_shapes)
    phases = [(w, lambda i, o, s, fn=fn: fn(i[:na], o[:oa], s[:sa])) for w, fn in a.phases]
    phases += [(w, lambda i, o, s, fn=fn: fn(i[na:], o[oa:], s[sa:])) for w, fn in b.phases]
    return _Hosted(a.ins + b.ins, list(a.out_shapes) + list(b.out_shapes), list(a.sem_shapes) + list(b.sem_shapes), phases)


def _run_alone(comm, name):
    ci, co = len(comm.ins), len(comm.out_shapes)

    def body(*refs):
        for _, fn in comm.phases:
            fn(refs[:ci], refs[ci:ci + co], refs[ci + co:])

    return pl.pallas_call(
        body, name=name, in_specs=[ANY] * ci, out_specs=[ANY] * co, out_shape=list(comm.out_shapes),
        scratch_shapes=list(comm.sem_shapes),
    )(*comm.ins)


def _gather_comm(shards, layers, axes):
    n = len(shards)
    out_shapes = []
    for s, (ba, _) in zip(shards, axes):
        shp = list(s.shape[1:])
        shp[ba] *= N_CHIPS
        out_shapes.append(jax.ShapeDtypeStruct(tuple(shp), s.dtype))

    def half(ref, t, h):
        ha = axes[t][1]
        if ha is None:
            return ref
        hs = shards[t].shape[1 + ha] // 2
        return _sub(ref, ha, h * hs, hs)

    def block(outs, t, blk):
        ba = axes[t][0]
        bs = shards[t].shape[1 + ba]
        return _sub(outs[t], ba, blk * bs, bs)

    def start(ins, outs, sems):
        s_own, r_own, s_ici, r_ici, _, _ = sems
        x, y, c, chips = _position()
        j = 2 * x + y
        for t in range(n):
            _dma(ins[t].at[layers[t]], block(outs, t, j), s_own.at[t], r_own.at[t], (x, y, 1 - c)).start()
            for k, (px, py) in enumerate(chips):
                _dma(half(ins[t].at[layers[t]], t, c), half(block(outs, t, j), t, c),
                     s_ici.at[3 * t + k], r_ici.at[3 * t + k], (px, py, c)).start()

    def forward(ins, outs, sems):
        _, _, s_ici, r_ici, s_fwd, r_fwd = sems
        x, y, c, chips = _position()
        for t in range(n):
            if axes[t][1] is None:
                continue
            for k, (px, py) in enumerate(chips):
                got = half(block(outs, t, 2 * px + py), t, c)
                _dma(got, got, s_ici.at[3 * t + k], r_ici.at[3 * t + k], (px, py, c)).wait_recv()
                _dma(got, got, s_fwd.at[3 * t + k], r_fwd.at[3 * t + k], (x, y, 1 - c)).start()

    def finish(ins, outs, sems):
        s_own, r_own, s_ici, r_ici, s_fwd, r_fwd = sems
        x, y, c, chips = _position()
        j = 2 * x + y
        sib = (x, y, 1 - c)
        for t in range(n):
            for k, (px, py) in enumerate(chips):
                i = 3 * t + k
                sent = half(block(outs, t, j), t, c)
                _dma(sent, sent, s_ici.at[i], r_ici.at[i], (px, py, c)).wait_send()
                theirs = block(outs, t, 2 * px + py)
                if axes[t][1] is None:
                    _dma(theirs, theirs, s_ici.at[i], r_ici.at[i], (px, py, c)).wait_recv()
                else:
                    got, other = half(theirs, t, c), half(theirs, t, 1 - c)
                    _dma(got, got, s_fwd.at[i], r_fwd.at[i], sib).wait_send()
                    _dma(other, other, s_fwd.at[i], r_fwd.at[i], sib).wait_recv()
            _dma(ins[t].at[layers[t]], block(outs, t, j), s_own.at[t], r_own.at[t], sib).wait()

    return _Hosted(list(shards), out_shapes, _dma_sems(n, n, 3 * n, 3 * n, 3 * n, 3 * n),
                   [("first", start), ("late", forward), ("last", finish)])


def _halved(shape, axis):
    shp = list(shape)
    shp[axis] //= 2
    return tuple(shp)


def _pair_comm(grads, axes):
    n = len(grads)

    def copies(ins, outs, sems):
        x, y, c, _ = _position()
        cps = []
        for t in range(n):
            ha = axes[t][1]
            hs = grads[t].shape[ha] // 2
            cps.append(_dma(_sub(ins[t], ha, (1 - c) * hs, hs), outs[t], sems[0].at[t], sems[1].at[t], (x, y, 1 - c)))
        return cps

    def start(ins, outs, sems):
        for cp in copies(ins, outs, sems):
            cp.start()

    def finish(ins, outs, sems):
        for cp in copies(ins, outs, sems):
            cp.wait()

    return _Hosted(list(grads), [jax.ShapeDtypeStruct(_halved(g.shape, ax[1]), g.dtype) for g, ax in zip(grads, axes)],
                   _dma_sems(n, n), [("first", start), ("last", finish)])


def _block_comm(sums, axes):
    n = len(sums)
    out_shapes = []
    for s, (ba, _) in zip(sums, axes):
        shp = list(s.shape)
        shp[ba] //= N_CHIPS
        out_shapes.append(jax.ShapeDtypeStruct((3,) + tuple(shp), s.dtype))

    def copies(ins, outs, sems):
        x, y, c, chips = _position()
        cps = []
        for t in range(n):
            ba = axes[t][0]
            bs = sums[t].shape[ba] // N_CHIPS
            for k, (px, py) in enumerate(chips):
                cps.append(_dma(_sub(ins[t], ba, (2 * px + py) * bs, bs), outs[t].at[k],
                                sems[0].at[3 * t + k], sems[1].at[3 * t + k], (px, py, c)))
        return cps

    def start(ins, outs, sems):
        for cp in copies(ins, outs, sems):
            cp.start()

    def finish(ins, outs, sems):
        for cp in copies(ins, outs, sems):
            cp.wait()

    return _Hosted(list(sums), out_shapes, _dma_sems(3 * n, 3 * n), [("first", start), ("last", finish)])


def _sibling_share(shards, axes, layer):
    n = len(shards)

    def body(*refs):
        outs = refs[n:2 * n]
        ssem, rsem = refs[2 * n:]
        x, y, c, _ = _position()
        cps = []
        for t in range(n):
            ha = axes[t][1]
            hs = shards[t].shape[1 + ha] // 2
            mine = _sub(outs[t].at[layer], ha, c * hs, hs)
            cps.append(_dma(mine, mine, ssem.at[t], rsem.at[t], (x, y, 1 - c)))
        for cp in cps:
            cp.start()
        for cp in cps:
            cp.wait()

    return pl.pallas_call(
        body, name="sibling_share",
        in_specs=[ANY] * n, out_specs=[ANY] * n,
        out_shape=[jax.ShapeDtypeStruct(s.shape, s.dtype) for s in shards],
        input_output_aliases={t: t for t in range(n)},
        scratch_shapes=_dma_sems(n, n),
    )(*shards)


def _small_allreduce(p):
    R, C = p.shape
    Rh = R // 2
    assert Rh % 8 == 0

    def body(p_ref, o_ref, sib_buf, chip_sums, ssem, rsem):
        x, y, c, chips = _position()
        j = 2 * x + y
        sib_id = (x, y, 1 - c)
        sib = _dma(p_ref, sib_buf, ssem.at[0], rsem.at[0], sib_id)
        sib.start()
        sib.wait()
        chip_sums[j] = p_ref[...] + sib_buf[...]

        def half(blk, h):
            return chip_sums.at[blk, pl.ds(pl.multiple_of(h * Rh, 8), Rh), :]

        sent = [_dma(half(j, c), half(j, c), ssem.at[1 + k], rsem.at[1 + k], (px, py, c))
                for k, (px, py) in enumerate(chips)]
        for cp in sent:
            cp.start()
        passed = []
        for k, (px, py) in enumerate(chips):
            got = half(2 * px + py, c)
            _dma(got, got, ssem.at[1 + k], rsem.at[1 + k], (px, py, c)).wait_recv()
            cp = _dma(got, got, ssem.at[4 + k], rsem.at[4 + k], sib_id)
            cp.start()
            passed.append(cp)
        for k, (px, py) in enumerate(chips):
            other = half(2 * px + py, 1 - c)
            _dma(other, other, ssem.at[4 + k], rsem.at[4 + k], sib_id).wait_recv()
        for cp in sent + passed:
            cp.wait_send()
        o_ref[...] = ((chip_sums[0] + chip_sums[1]) + chip_sums[2]) + chip_sums[3]

    vm = pl.BlockSpec(memory_space=pltpu.VMEM)
    return pl.pallas_call(
        body, name="small_allreduce",
        in_specs=[vm], out_specs=vm, out_shape=jax.ShapeDtypeStruct((R, C), F32),
        scratch_shapes=[pltpu.VMEM((R, C), F32), pltpu.VMEM((N_CHIPS, R, C), F32)] + _dma_sems(7, 7),
        compiler_params=pltpu.CompilerParams(vmem_limit_bytes=VMEM_LIMIT_BYTES),
    )(p)


def _add_half(g, recv, half_axis, pos):
    B, R, C = recv.shape
    tr = _tile(R, 256)
    tc = _tile(C, 2816)
    nr, nc = R // tr, C // tc
    if half_axis == 1:
        g_map = lambda b, i, j, pos_ref: (b, pos_ref[0] * nr + i, j)
    else:
        g_map = lambda b, i, j, pos_ref: (b, i, pos_ref[0] * nc + j)

    def body(pos_ref, g_ref, r_ref, o_ref, ob_ref):
        del pos_ref
        s = g_ref[...] + r_ref[...]
        o_ref[...] = s
        ob_ref[...] = s.astype(BF16)

    spec = lambda: pl.BlockSpec((None, tr, tc), lambda b, i, j, pos_ref: (b, i, j))
    return pl.pallas_call(
        body, name="add_half",
        grid_spec=pltpu.PrefetchScalarGridSpec(
            num_scalar_prefetch=1, grid=(B, nr, nc),
            in_specs=[pl.BlockSpec((None, tr, tc), g_map), spec()],
            out_specs=[spec(), spec()]),
        out_shape=[jax.ShapeDtypeStruct((B, R, C), F32), jax.ShapeDtypeStruct((B, R, C), BF16)],
        compiler_params=_params("parallel", "parallel", "parallel"),
    )(pos, g, recv)


def _sum_half(own, recv, axes, pos, layer, prev):
    _, B, rs, cs = recv.shape
    ba, ha = axes
    tr = _tile(rs, 128)
    nr = rs // tr

    def body(pos_ref, o_ref, r0_ref, r1_ref, r2_ref, *rest):
        out_ref = rest[-1]
        out_ref[...] = (((o_ref[...] + r0_ref[...].astype(F32)) + r1_ref[...].astype(F32))
                        + r2_ref[...].astype(F32))

    if ba == 2:
        own_spec = pl.BlockSpec((None, tr, cs), lambda b, i, pos_ref: (b, i, pos_ref[1]))
    else:
        own_spec = pl.BlockSpec((None, tr, cs), lambda b, i, pos_ref: (b, pos_ref[1] * nr + i, 0))
    if ha == 1:
        out_shape, out_map = (B, 2 * rs, cs), (lambda b, i, pos_ref: (layer, b, pos_ref[0] * nr + i, 0))
    else:
        out_shape, out_map = (B, rs, 2 * cs), (lambda b, i, pos_ref: (layer, b, i, pos_ref[0]))
    rspec = lambda k: pl.BlockSpec((None, None, tr, cs), lambda b, i, pos_ref: (k, b, i, 0))
    more_in, more_args, aliases = ([ANY], [prev], {5: 0}) if prev is not None else ([], [], {})
    return pl.pallas_call(
        body, name="sum_half",
        grid_spec=pltpu.PrefetchScalarGridSpec(
            num_scalar_prefetch=1, grid=(B, nr),
            in_specs=[own_spec, rspec(0), rspec(1), rspec(2)] + more_in,
            out_specs=pl.BlockSpec((None, None, tr, cs), out_map)),
        out_shape=jax.ShapeDtypeStruct((N_LAYERS,) + out_shape, F32),
        input_output_aliases=aliases,
        compiler_params=_params("parallel", "parallel"),
    )(pos, own, recv, recv, recv, *more_args)


def _adamw(w, g, m, v):
    shape = w.shape
    C = shape[-1]
    R = w.size // C
    tr = R
    for cand in (512, 256, 128, 64, 32, 16, 8):
        if R % cand == 0 and cand * C * 4 <= (1 << 20):
            tr = cand
            break
    bc1 = 1.0 - ADAM_B1 ** ADAM_STEP
    bc2 = 1.0 - ADAM_B2 ** ADAM_STEP

    def body(w_ref, g_ref, m_ref, v_ref, d_ref, mo_ref, vo_ref):
        gg = g_ref[...]
        mn = ADAM_B1 * m_ref[...] + (1.0 - ADAM_B1) * gg
        vn = ADAM_B2 * v_ref[...] + (1.0 - ADAM_B2) * (gg * gg)
        mo_ref[...] = mn
        vo_ref[...] = vn
        d_ref[...] = -ADAM_LR * ((mn / bc1) / (jnp.sqrt(vn / bc2) + ADAM_EPS) + ADAM_WD * w_ref[...])

    spec = lambda: pl.BlockSpec((tr, C), lambda i: (i, 0))
    outs = pl.pallas_call(
        body, name="adamw", grid=(R // tr,),
        in_specs=[spec()] * 4, out_specs=[spec()] * 3,
        out_shape=[jax.ShapeDtypeStruct((R, C), F32)] * 3,
        compiler_params=_params("parallel"),
    )(*[a.reshape(R, C) for a in (w, g, m, v)])
    return tuple(o.reshape(shape) for o in outs)


def kernel(x, mem, norm_g, mem_norm_g, w_in, gmlp_ln_g, gmlp_ln_b, w_s, b_s, conv_w, conv_b, conv_ln_g, conv_ln_b, w_kv, w_branch, w_out, final_norm_g, loss_target, m_norm_g, m_mem_norm_g, m_w_in, m_gmlp_ln_g, m_gmlp_ln_b, m_w_s, m_b_s, m_conv_w, m_conv_b, m_conv_ln_g, m_conv_ln_b, m_w_kv, m_w_branch, m_w_out, m_final_norm_g, v_norm_g, v_mem_norm_g, v_w_in, v_gmlp_ln_g, v_gmlp_ln_b, v_w_s, v_b_s, v_conv_w, v_conv_b, v_conv_ln_g, v_conv_ln_b, v_w_kv, v_w_branch, v_w_out, v_final_norm_g):
    xs, mems, tgt = x[0], mem[0], loss_target[0]
    S, D = xs.shape
    assert D // GROUPS == LANES and S % CHUNK == 0 and w_s.shape[-1] == CHUNK
    L = N_LAYERS

    assert L == 2
    shards = [w_in.astype(BF16)[:, None], w_kv.astype(BF16)[:, None], w_branch.astype(BF16), w_out.astype(BF16)[:, None],
              conv_w[:, None]]
    gather_axes = [(2, 1), (2, 1), (1, 2), (1, 2), (2, None)]
    reduce_axes = gather_axes[:4]
    win0, cw0 = _run_alone(_gather_comm([shards[0], shards[4]], [0, 0], [gather_axes[0], gather_axes[4]]), "gather_first")
    gather_rest = _gather_comm(shards[1:4] + shards, [0] * 3 + [1] * 5, gather_axes[1:4] + gather_axes)

    tri = jnp.tril(jnp.ones((CHUNK, CHUNK), F32))
    t_mix = (w_s * tri).astype(BF16)
    t_mix_t = jnp.swapaxes(t_mix, -1, -2)
    bias_full = jnp.repeat(jnp.swapaxes(b_s, -1, -2), D // GROUPS, axis=-1)
    row = lambda a, l: a[l][None, :]
    xi, yi, ci = lax.axis_index("x"), lax.axis_index("y"), lax.axis_index("c")
    pos = jnp.stack([ci, 2 * xi + yi]).astype(jnp.int32)

    saved = []
    h = xs
    weights = [None] * L
    for l in range(L):
        if l == 0:
            z, hb, c0, fetched = _in_proj_conv(h, row(norm_g, l), win0, cw0, row(conv_b, l), 0, gather_rest)
            weights = [[win0] + list(fetched[:3]) + [cw0], list(fetched[3:])]
        else:
            z, hb, c0, _ = _in_proj_conv(h, row(norm_g, l), weights[l][0], weights[l][4], row(conv_b, l), 0)
        win, wkv, wb, wo, cw = weights[l]
        mn, kv, kvt = _kv_fwd(mems, row(mem_norm_g, l), wkv, 0)
        brc = _branch_c_fwd(z, kv, kvt)
        bra, brb, merged, h_next = _merge_fwd(brc, z, h, c0, row(conv_ln_g, l), row(conv_ln_b, l), row(gmlp_ln_g, l),
                                              row(gmlp_ln_b, l), t_mix[l], bias_full[l], wb[None], wo, 0)
        saved.append((h, z, hb, bra, c0, brb, mn, kv, kvt, brc, merged))
        h = h_next

    dx, loss_cols, d_final_g = _loss_bwd(h, tgt, final_norm_g[None, :])
    loss = lax.psum(0.5 * jnp.sum(loss_cols) / D, ("x", "y", "c"))

    def add_halves(grads, recv, axes):
        return [_add_half(g, r, ax[1], pos) for g, r, ax in zip(grads, recv, axes)]

    def finish_reduction(sums, blocks, axes, layer, prevs):
        halves = [_sum_half(s, b, ax, pos, layer, p) for (s, _), b, ax, p in zip(sums, blocks, axes, prevs)]
        return list(_sibling_share(halves, axes, layer))

    small = {k: [None] * L for k in ("norm_g", "mem_norm_g", "gmlp_ln_g", "gmlp_ln_b", "w_s", "b_s", "conv_w",
                                     "conv_b", "conv_ln_g", "conv_ln_b")}
    early_axes = reduce_axes[1:]
    pending, stacked = None, None
    for l in reversed(range(L)):
        h_in, z, hb, bra, c0, brb, mn, kv, kvt, brc, merged = saved[l]
        win, wkv, wb, wo, cw = weights[l]
        dz_m, dproj, dbr, dc0, dz_b, dcg, dcbeta, dcb, recv = _merge_bwd(
            dx, bra, brb, brc, z, c0, row(conv_ln_g, l), row(conv_ln_b, l), wb[None], wo, 0,
            _pair_comm(pending, reduce_axes) if pending else None)
        sums = add_halves(pending, recv, reduce_axes) if pending else []
        g_o = _wgrad(merged, dx, (1, D, D), 0, 0, None)
        g_b = None
        for n, br in enumerate((bra, brb, brc)):
            g_b = _wgrad(br, dproj, (N_BRANCHES, D, D), n, 0, g_b, b_plane=n)
        dz_a, dws, dbs, dlg, dlb = _branch_a_bwd(z, dbr, row(gmlp_ln_g, l), row(gmlp_ln_b, l),
                                                t_mix[l], t_mix_t[l], bias_full[l], tri)
        dz_c, dk, dv = _branch_c_bwd(z, dbr, kv, kvt)
        g_kv, dmg = _kv_bwd(dk, dv, mn, mems, wkv, 0, None)
        ready = [g_kv, g_b, g_o]
        riding = _both(_block_comm([sb for _, sb in sums], reduce_axes), _pair_comm(ready, early_axes)) if pending else None
        dz_b, dcw, got = _conv_bwd(dc0, z, cw, dz_b, 0, riding)
        sums_early = add_halves(ready, got[4:], early_axes) if pending else []
        if pending:
            stacked = finish_reduction(sums, got[:4], reduce_axes, l + 1, [None] * 4)
        pieces =[(dz_a, 0, 3), (dz_b, 3, 3), (dz_c, 6, 2), (dz_m, 8, 3)]
        g_in = None
        for dzp, lo, _ in pieces:
            g_in = _wgrad(hb, dzp, (1, D, N_PLANES * D), 0, lo * D, g_in)
        if pending:
            riding = _block_comm([sb for _, sb in sums_early], early_axes)
            dx_half, dng_a, got = _in_proj_bwd(pieces, win, h_in, row(norm_g, l), dx, 0,
                                               _both(riding, _pair_comm([g_in], reduce_axes[:1])), (0, 2))
            stacked[1:] = finish_reduction(sums_early, got[:3], early_axes, l, stacked[1:])
            sums_in = add_halves([g_in], got[3:], reduce_axes[:1])
            dx, dng_b, blocks_in = _in_proj_bwd(pieces, win, h_in, row(norm_g, l), dx, 0,
                                                _block_comm([sb for _, sb in sums_in], reduce_axes[:1]), (1, 2), dx_half)
            dng = dng_a + dng_b
            stacked[:1] = finish_reduction(sums_in, blocks_in, reduce_axes[:1], l, stacked[:1])
        else:
            dx, dng, _ = _in_proj_bwd(pieces, win, h_in, row(norm_g, l), dx, 0)
        pending = [g_in, g_kv, g_b, g_o]
        for k, val in (("norm_g", dng), ("mem_norm_g", dmg), ("gmlp_ln_g", dlg), ("gmlp_ln_b", dlb), ("w_s", dws),
                       ("b_s", dbs), ("conv_w", dcw), ("conv_b", dcb), ("conv_ln_g", dcg), ("conv_ln_b", dcbeta)):
            small[k][l] = val
    grad_x = dx[None]

    grad_w_in, grad_w_kv, grad_w_branch, grad_w_out = [
        g.reshape(w.shape) for g, w in zip(stacked, (w_in, w_kv, w_branch, w_out))]

    order = [("norm_g", norm_g.shape), ("mem_norm_g", mem_norm_g.shape), ("gmlp_ln_g", gmlp_ln_g.shape),
             ("gmlp_ln_b", gmlp_ln_b.shape), ("w_s", w_s.shape), ("b_s", b_s.shape),
             ("conv_w", (L, CONV_K, D)), ("conv_b", conv_b.shape), ("conv_ln_g", conv_ln_g.shape),
             ("conv_ln_b", conv_ln_b.shape)]
    parts = [jnp.stack([v.reshape(shp[1:]) for v in small[k]]).reshape(-1, LANES) for k, shp in order]
    parts.append(d_final_g.reshape(-1, LANES))
    sizes = [p.shape[0] for p in parts]
    assert all(s % 8 == 0 for s in sizes)
    if sum(sizes) % 16:
        parts.append(jnp.zeros((16 - sum(sizes) % 16, LANES), F32))
    total = _small_allreduce(jnp.concatenate(parts, axis=0))
    red, off = {}, 0
    for (k, shp), n in zip(order + [("final_norm_g", final_norm_g.shape)], sizes):
        red[k] = total[off:off + n].reshape(shp)
        off += n
    cs = conv_w.shape[-1]
    red["conv_w"] = lax.dynamic_slice_in_dim(red["conv_w"], (2 * xi + yi) * cs, cs, axis=2)
    red.update(w_in=grad_w_in, w_kv=grad_w_kv, w_branch=grad_w_branch, w_out=grad_w_out)

    names = ["norm_g", "mem_norm_g", "w_in", "gmlp_ln_g", "gmlp_ln_b", "w_s", "b_s", "conv_w", "conv_b",
             "conv_ln_g", "conv_ln_b", "w_kv", "w_branch", "w_out", "final_norm_g"]
    weights = dict(norm_g=norm_g, mem_norm_g=mem_norm_g, w_in=w_in, gmlp_ln_g=gmlp_ln_g, gmlp_ln_b=gmlp_ln_b,
                   w_s=w_s, b_s=b_s, conv_w=conv_w, conv_b=conv_b, conv_ln_g=conv_ln_g, conv_ln_b=conv_ln_b,
                   w_kv=w_kv, w_branch=w_branch, w_out=w_out, final_norm_g=final_norm_g)
    ms = dict(norm_g=m_norm_g, mem_norm_g=m_mem_norm_g, w_in=m_w_in, gmlp_ln_g=m_gmlp_ln_g, gmlp_ln_b=m_gmlp_ln_b,
              w_s=m_w_s, b_s=m_b_s, conv_w=m_conv_w, conv_b=m_conv_b, conv_ln_g=m_conv_ln_g, conv_ln_b=m_conv_ln_b,
              w_kv=m_w_kv, w_branch=m_w_branch, w_out=m_w_out, final_norm_g=m_final_norm_g)
    vs = dict(norm_g=v_norm_g, mem_norm_g=v_mem_norm_g, w_in=v_w_in, gmlp_ln_g=v_gmlp_ln_g, gmlp_ln_b=v_gmlp_ln_b,
              w_s=v_w_s, b_s=v_b_s, conv_w=v_conv_w, conv_b=v_conv_b, conv_ln_g=v_conv_ln_g, conv_ln_b=v_conv_ln_b,
              w_kv=v_w_kv, w_branch=v_w_branch, w_out=v_w_out, final_norm_g=v_final_norm_g)
    deltas, new_m, new_v = [], [], []
    for k in names:
        w2 = weights[k] if weights[k].ndim > 1 else weights[k][None, :]
        d, mo, vo = _adamw(w2, red[k].reshape(w2.shape), ms[k].reshape(w2.shape), vs[k].reshape(w2.shape))
        deltas.append(d.reshape(weights[k].shape))
        new_m.append(mo.reshape(weights[k].shape))
        new_v.append(vo.reshape(weights[k].shape))
    grads = [red[k].reshape(weights[k].shape) for k in names]
    return (loss, grad_x, *grads, *deltas, *new_m, *new_v)
```

```python
import math

import jax
import jax.numpy as jnp
from jax import lax
from jax.experimental import pallas as pl
from jax.experimental.pallas import tpu as pltpu

F32 = jnp.float32
BF16 = jnp.bfloat16

N_LAYERS = 2
N_BRANCHES = 3
N_PLANES = 11
N_CHIPS = 4
CHUNK = 128
GROUPS = 8
HEADS = 4
CONV_K = 31
HALO = 32
LANES = 128
RMS_EPS = 1e-6
LN_EPS = 1e-5
ADAM_LR, ADAM_B1, ADAM_B2, ADAM_EPS, ADAM_WD, ADAM_STEP = 0.001, 0.9, 0.999, 1e-08, 0.01, 10
VMEM_LIMIT_BYTES = 48 * 1024 * 1024
MESH = pl.DeviceIdType.MESH
ANY = pl.BlockSpec(memory_space=pl.ANY)


def _params(*sem):
    return pltpu.CompilerParams(dimension_semantics=sem, vmem_limit_bytes=VMEM_LIMIT_BYTES)


def _tile(n, pref):
    t = min(n, pref)
    assert n % t == 0, (n, t)
    return t


def _dot(a, b):
    return jnp.dot(a, b, preferred_element_type=F32)


def _dot_tb(a, b):
    return lax.dot_general(a, b, (((1,), (1,)), ((), ())), preferred_element_type=F32)


def _dot_ta(a, b):
    return lax.dot_general(a, b, (((0,), (0,)), ((), ())), preferred_element_type=F32)


def _sigmoid(x):
    return 1.0 / (1.0 + jnp.exp(-x))


def _silu(x):
    return x * _sigmoid(x)


def _silu_and_grad(x):
    s = _sigmoid(x)
    return x * s, s * (1.0 + x * (1.0 - s))


_GELU_C = math.sqrt(2.0 / math.pi)
_GELU_A = 0.044715


def _gelu(x):
    return x * (0.5 * (1.0 + jnp.tanh(_GELU_C * (x + _GELU_A * (x * x * x)))))


def _gelu_and_grad(x):
    x2 = x * x
    t = jnp.tanh(_GELU_C * (x + _GELU_A * (x2 * x)))
    cdf = 0.5 * (1.0 + t)
    dcdf = 0.5 * (1.0 - t * t) * (_GELU_C * (1.0 + 3.0 * _GELU_A * x2))
    return x * cdf, cdf + x * dcdf


def _rows(i, n):
    return pl.ds(pl.multiple_of(i * n, n), n)


def _grouped_loop(n, group, body, init):
    assert n % group == 0

    def trip(i, c):
        for u in range(group):
            c = body(i * group + u, c)
        return c
    return lax.fori_loop(0, n // group, trip, init)


def _mean(x):
    return jnp.mean(x, axis=-1, keepdims=True)


def _colsum(x):
    return jnp.sum(x, axis=0, keepdims=True)


def _tok(ts, d):
    return pl.BlockSpec((ts, d), lambda i: (i, 0))


def _zcol(ts, d, n):
    return pl.BlockSpec((ts, d), lambda i: (i, n))


def _row1(d):
    return pl.BlockSpec((1, d), lambda i: (0, 0))


def _plane(ts, d, n):
    return pl.BlockSpec((None, ts, d), lambda i: (n, i, 0))


class _Hosted:
    def __init__(self, ins, out_shapes, sem_shapes, phases):
        self.ins, self.out_shapes, self.sem_shapes, self.phases = ins, out_shapes, sem_shapes, phases


def _host(comm, n_in, n_out, n_scr, grid, body):
    if comm is None:
        return body, [], [], [], [], None
    ci, co = len(comm.ins), len(comm.out_shapes)
    total = math.prod(grid)
    at = {"first": 0, "late": (7 * total) // 8, "last": total - 1}

    def wrapped(*refs):
        ins, cin = refs[:n_in], refs[n_in:n_in + ci]
        o0 = n_in + ci
        outs, cout = refs[o0:o0 + n_out], refs[o0 + n_out:o0 + n_out + co]
        s0 = o0 + n_out + co
        scr, csem = refs[s0:s0 + n_scr], refs[s0 + n_scr:]
        step = pl.program_id(0)
        for a in range(1, len(grid)):
            step = step * grid[a] + pl.program_id(a)

        def run(last):
            for w, fn in comm.phases:
                if (w == "last") == last:
                    @pl.when(step == at.get(w, w))
                    def _(fn=fn):
                        fn(cin, cout, csem)
        run(False)
        body(*ins, *outs, *scr)
        run(True)

    return wrapped, [ANY] * ci, [ANY] * co, list(comm.out_shapes), list(comm.sem_shapes), ("arbitrary",) * len(grid)


SH = HALO - CONV_K + 1


def _fill_shifts(sh_scr, src_scr, cols, ts, base=0):
    for j in range(1, 8):
        for g in range(ts // 32):
            sh_scr[j - 1, pl.ds(g * 32, 32), :] = src_scr[pl.ds(base + g * 32 + j, 32), cols]
        sh_scr[j - 1, pl.ds(ts, HALO - 8), :] = src_scr[pl.ds(base + ts + j, HALO - 8), cols]


def _shifted(sh_scr, src_scr, cols, row0, n, base=0):
    q, j = divmod(row0, 8)
    if j == 0:
        return src_scr[pl.ds(base + row0, n), cols]
    return sh_scr[j - 1, pl.ds(8 * q, n), :]


def _in_proj_conv(x, g, w_all, cw_all, cb, l, comm=None):
    S, D = x.shape
    N = w_all.shape[2]
    tm = _tile(S, 1024)
    QR, CB = 256, 256
    nq = tm // QR
    PA, PB = 3, 4
    assert tm % QR == 0 and PB + 1 + nq <= N // D
    grid = (S // tm, N // D)

    def body(x_ref, g_ref, w_ref, cw_ref, cb_ref, z_ref, hb_ref, c0_ref, h_scr, za_scr, glu_scr, sh_scr):
        i, j = pl.program_id(0), pl.program_id(1)

        @pl.when(j == 0)
        def _():
            def slab(si, c):
                r = _rows(si, 16)
                xs = x_ref[r, :]
                hb = ((xs * lax.rsqrt(_mean(xs * xs) + RMS_EPS)) * g_ref[...]).astype(BF16)
                h_scr[r, :] = hb
                hb_ref[r, :] = hb
                return c
            _grouped_loop(tm // 16, 4, slab, 0)

        def matmul_cols(cols):
            zc = _dot(h_scr[...], w_ref[:, cols])
            z_ref[:, cols] = zc
            return zc

        @pl.when(j == PA)
        def _():
            def colblock(ci, c):
                cols = _rows(ci, CB)
                za_scr[:, cols] = matmul_cols(cols)
                return c
            lax.fori_loop(0, D // CB, colblock, 0)

        @pl.when((j == PB) & (i == 0))
        def _():
            glu_scr[0:HALO, :] = jnp.zeros((HALO, D), F32)

        @pl.when((j == PB) & (i > 0))
        def _():
            glu_scr[0:HALO, :] = glu_scr[tm:tm + HALO, :]

        @pl.when(j == PB)
        def _():
            def colblock(ci, c):
                cols = _rows(ci, CB)
                glu_scr[HALO:HALO + tm, cols] = za_scr[:, cols] * _sigmoid(matmul_cols(cols))
                return c
            lax.fori_loop(0, D // CB, colblock, 0)

        for q in range(nq):
            @pl.when(j == PB + 1 + q)
            def _(q=q):
                def colblock(ci, c):
                    cols = _rows(ci, CB)
                    matmul_cols(cols)
                    _fill_shifts(sh_scr, glu_scr, cols, QR, q * QR)
                    for s in range(QR // 32):
                        acc = jnp.zeros((32, CB), F32)
                        for k in range(CONV_K):
                            acc = acc + _shifted(sh_scr, glu_scr, cols, s * 32 + k + SH, 32, q * QR) * cw_ref[k:k + 1, cols]
                        c0_ref[pl.ds(q * QR + s * 32, 32), cols] = acc + cb_ref[:, cols]
                    return c
                lax.fori_loop(0, D // CB, colblock, 0)

        @pl.when((j < PA) | (j > PB + nq))
        def _():
            z_ref[...] = _dot(h_scr[...], w_ref[...])

    tile = lambda: pl.BlockSpec((tm, D), lambda i, j: (i, 0))
    body, c_in, c_out, c_shapes, c_scr, _ = _host(comm, 5, 3, 4, grid, body)
    outs = pl.pallas_call(
        body, name="in_proj_conv", grid=grid,
        in_specs=[tile(), pl.BlockSpec((1, D), lambda i, j: (0, 0)),
                  pl.BlockSpec((None, D, D), lambda i, j: (l, 0, j)),
                  pl.BlockSpec((None, CONV_K, D), lambda i, j: (l, 0, 0)),
                  pl.BlockSpec((1, D), lambda i, j: (0, 0))] + c_in,
        out_specs=[pl.BlockSpec((tm, D), lambda i, j: (i, j)), tile(), tile()] + c_out,
        out_shape=[jax.ShapeDtypeStruct((S, N), F32), jax.ShapeDtypeStruct((S, D), BF16),
                   jax.ShapeDtypeStruct((S, D), F32)] + c_shapes,
        scratch_shapes=[pltpu.VMEM((tm, D), BF16), pltpu.VMEM((tm, D), F32), pltpu.VMEM((tm + HALO, D), F32),
                        pltpu.VMEM((7, QR + HALO, CB), F32)] + c_scr,
        compiler_params=_params("arbitrary", "arbitrary"),
    )(x, g, w_all, cw_all, cb, *(comm.ins if comm else []))
    return outs[0], outs[1], outs[2], outs[3:]


def _kv_fwd(mem, mg, wkv_all, l):
    M, D = mem.shape

    def body(mem_ref, g_ref, w_ref, mn_ref, kv_ref, kvt_ref):
        m = mem_ref[...]
        mn = ((m * lax.rsqrt(_mean(m * m) + RMS_EPS)) * g_ref[...]).astype(BF16)
        mn_ref[...] = mn
        kv = _dot(mn, w_ref[...])
        kv_ref[...] = kv.astype(BF16)
        kvt_ref[...] = kv.T.astype(BF16)

    return pl.pallas_call(
        body, name="kv_fwd", grid=(1,),
        in_specs=[pl.BlockSpec((M, D), lambda i: (0, 0)), _row1(D),
                  pl.BlockSpec((None, D, 2 * D), lambda i: (l, 0, 0))],
        out_specs=[pl.BlockSpec((M, D), lambda i: (0, 0)), pl.BlockSpec((M, 2 * D), lambda i: (0, 0)),
                   pl.BlockSpec((2 * D, M), lambda i: (0, 0))],
        out_shape=[jax.ShapeDtypeStruct((M, D), BF16), jax.ShapeDtypeStruct((M, 2 * D), BF16),
                   jax.ShapeDtypeStruct((2 * D, M), BF16)],
        compiler_params=_params("arbitrary"),
    )(mem, mg, wkv_all)


def _softmax_rows(s):
    e = jnp.exp(s - jnp.max(s, axis=-1, keepdims=True))
    return e / jnp.sum(e, axis=-1, keepdims=True)


def _branch_c_fwd(z, kv, kvt):
    S = z.shape[0]
    M, D2 = kv.shape
    D = D2 // 2
    HD = D // HEADS
    scale = 1.0 / math.sqrt(HD)
    ts = _tile(S, 256)

    def body(q_ref, zg_ref, kt_ref, v_ref, o_ref):
        for h in range(HEADS):
            cols = slice(h * HD, (h + 1) * HD)
            p = _softmax_rows(_dot(q_ref[:, cols].astype(BF16), kt_ref[cols, :]) * scale)
            att = _dot(p.astype(BF16), v_ref[:, cols])
            o_ref[:, cols] = (att * _silu(zg_ref[:, cols])).astype(BF16)

    return pl.pallas_call(
        body, name="branch_c_fwd", grid=(S // ts,),
        in_specs=[_zcol(ts, D, 6), _zcol(ts, D, 7),
                  pl.BlockSpec((D, M), lambda i: (0, 0)), pl.BlockSpec((M, D), lambda i: (0, 1))],
        out_specs=_tok(ts, D),
        out_shape=jax.ShapeDtypeStruct((S, D), BF16),
        compiler_params=_params("parallel"),
    )(z, z, kvt, kv)


def _merge_fwd(brc, z, x, c0, cg, cb, lng, lnb, t_mix, bias_full, wb_all, wo_all, l):
    S, D = x.shape
    GD = D // GROUPS
    ts = _tile(S, 256)
    CB = 256

    def body(c_ref, zm0_ref, zm1_ref, zm2_ref, x_ref, c0_ref, zbg_ref, cg_ref, cb_ref, zu_ref, zv_ref, zag_ref,
             lng_ref, lnb_ref, t_ref, bias_ref, wb_ref, wo_ref, a_ref, b_ref, m_ref, xo_ref, v_scr):
        for s in range(ts // 16):
            r = pl.ds(s * 16, 16)
            v = c0_ref[r, :]
            xc = v - _mean(v)
            c1 = (xc * lax.rsqrt(_mean(xc * xc) + LN_EPS)) * cg_ref[...] + cb_ref[...]
            b_ref[r, :] = (_silu(c1) * _silu(zbg_ref[r, :])).astype(BF16)
        for ch in range(ts // CHUNK):
            for s in range(CHUNK // 16):
                vg = _gelu(zv_ref[pl.ds(ch * CHUNK + s * 16, 16), :])
                xc = vg - _mean(vg)
                y = xc * lax.rsqrt(_mean(xc * xc) + LN_EPS)
                v_scr[pl.ds(ch * CHUNK + s * 16, 16), :] = (y * lng_ref[...] + lnb_ref[...]).astype(BF16)
            rows = pl.ds(ch * CHUNK, CHUNK)
            for g in range(GROUPS):
                cols = slice(g * GD, (g + 1) * GD)
                sv = _dot(t_ref[g], v_scr[rows, cols]) + bias_ref[:, cols]
                u = _gelu(zu_ref[rows, cols])
                a_ref[rows, cols] = ((u * sv) * _silu(zag_ref[rows, cols])).astype(BF16)

        brs = (a_ref, b_ref, c_ref)
        zms = (zm0_ref, zm1_ref, zm2_ref)
        for cb in range(D // CB):
            cols = slice(cb * CB, (cb + 1) * CB)
            acc = None
            for n in range(N_BRANCHES):
                t = _sigmoid(zms[n][:, cols]) * _dot(brs[n][...], wb_ref[n, :, cols])
                acc = t if acc is None else acc + t
            m_ref[:, cols] = acc.astype(BF16)
        for cb in range(D // CB):
            cols = slice(cb * CB, (cb + 1) * CB)
            xo_ref[:, cols] = x_ref[:, cols] + _dot(m_ref[...], wo_ref[:, cols])

    return pl.pallas_call(
        body, name="merge_fwd", grid=(S // ts,),
        in_specs=[_tok(ts, D), _zcol(ts, D, 8), _zcol(ts, D, 9), _zcol(ts, D, 10), _tok(ts, D),
                  _tok(ts, D), _zcol(ts, D, 5), _row1(D), _row1(D),
                  _zcol(ts, D, 0), _zcol(ts, D, 1), _zcol(ts, D, 2), _row1(D), _row1(D),
                  pl.BlockSpec((GROUPS, CHUNK, CHUNK), lambda i: (0, 0, 0)),
                  pl.BlockSpec((CHUNK, D), lambda i: (0, 0)),
                  pl.BlockSpec((None, N_BRANCHES, D, D), lambda i: (l, 0, 0, 0)),
                  pl.BlockSpec((None, D, D), lambda i: (l, 0, 0))],
        out_specs=[_tok(ts, D), _tok(ts, D), _tok(ts, D), _tok(ts, D)],
        out_shape=[jax.ShapeDtypeStruct((S, D), BF16), jax.ShapeDtypeStruct((S, D), BF16),
                   jax.ShapeDtypeStruct((S, D), BF16), jax.ShapeDtypeStruct((S, D), F32)],
        scratch_shapes=[pltpu.VMEM((ts, D), BF16)],
        compiler_params=_params("parallel"),
    )(brc, z, z, z, x, c0, z, cg, cb, z, z, z, lng, lnb, t_mix, bias_full, wb_all, wo_all)


def _loss_bwd(x, target, fg):
    S, D = x.shape
    ts = _tile(S, 512)

    def body(x_ref, t_ref, g_ref, dx_ref, ls_ref, dg_ref):
        @pl.when(pl.program_id(0) == 0)
        def _():
            ls_ref[...] = jnp.zeros_like(ls_ref)
            dg_ref[...] = jnp.zeros_like(dg_ref)

        def slab(si, c):
            ls, dg = c
            r = _rows(si, 16)
            xs = x_ref[r, :]
            rs = lax.rsqrt(_mean(xs * xs) + RMS_EPS)
            n = xs * rs
            e = n * g_ref[...] - t_ref[r, :]
            dy = e * (1.0 / D)
            dn = dy * g_ref[...]
            dx_ref[r, :] = rs * (dn - n * _mean(dn * n))
            return ls + _colsum(e * e), dg + _colsum(dy * n)
        zero = jnp.zeros((1, D), F32)
        ls, dg = _grouped_loop(ts // 16, 4, slab, (zero, zero))
        ls_ref[...] += ls
        dg_ref[...] += dg

    return pl.pallas_call(
        body, name="loss_bwd", grid=(S // ts,),
        in_specs=[_tok(ts, D), _tok(ts, D), _row1(D)],
        out_specs=[_tok(ts, D), _row1(D), _row1(D)],
        out_shape=[jax.ShapeDtypeStruct((S, D), F32), jax.ShapeDtypeStruct((1, D), F32),
                   jax.ShapeDtypeStruct((1, D), F32)],
        compiler_params=_params("arbitrary"),
    )(x, target, fg)


def _merge_bwd(dxp, bra, brb, brc, z, c0, lng, lnb, wb_all, wo_all, l, comm=None):
    S, D = dxp.shape
    ts = _tile(S, 256)
    CB = 256

    def body(d_ref, a_ref, b_ref, c_ref, zm0_ref, zm1_ref, zm2_ref, c0_ref, zg_ref, g_ref, be_ref, wb_ref, wo_ref,
             dzm_ref, dproj_ref, dbr_ref, dc0_ref, dzb_ref, dg_ref, db_ref, dcb_ref, dxb_scr, dbb_scr):
        @pl.when(pl.program_id(0) == 0)
        def _():
            dg_ref[...] = jnp.zeros_like(dg_ref)
            db_ref[...] = jnp.zeros_like(db_ref)
            dcb_ref[...] = jnp.zeros_like(dcb_ref)

        brs = (a_ref, b_ref, c_ref)
        zms = (zm0_ref, zm1_ref, zm2_ref)
        dxb_scr[...] = d_ref[...].astype(BF16)
        for cb in range(D // CB):
            cols = slice(cb * CB, (cb + 1) * CB)
            dm = _dot_tb(dxb_scr[...], wo_ref[cols, :])
            for n in range(N_BRANCHES):
                proj = _dot(brs[n][...], wb_ref[n, :, cols])
                g = _sigmoid(zms[n][:, cols])
                dzm_ref[:, n * D + cb * CB:n * D + (cb + 1) * CB] = ((dm * proj) * (g * (1.0 - g))).astype(BF16)
                dproj_ref[n, :, cols] = (dm * g).astype(BF16)
        for cb in range(D // CB):
            cols = slice(cb * CB, (cb + 1) * CB)
            dbb_scr[:, cols] = _dot_tb(dproj_ref[1], wb_ref[1, cols, :])

        dg = db = dcb = jnp.zeros((1, D), F32)
        for s in range(ts // 16):
            r = pl.ds(s * 16, 16)
            v = c0_ref[r, :]
            xc = v - _mean(v)
            rstd = lax.rsqrt(_mean(xc * xc) + LN_EPS)
            cn = xc * rstd
            c1 = cn * g_ref[...] + be_ref[...]
            c2, dc2_dc1 = _silu_and_grad(c1)
            sg, dsg = _silu_and_grad(zg_ref[r, :])
            d = dbb_scr[r, :]
            dzb_ref[r, :] = ((d * c2) * dsg).astype(BF16)
            dc1 = (d * sg) * dc2_dc1
            dcn = dc1 * g_ref[...]
            dc0 = rstd * ((dcn - _mean(dcn)) - cn * _mean(dcn * cn))
            dc0_ref[r, :] = dc0
            dg, db, dcb = dg + _colsum(dc1 * cn), db + _colsum(dc1), dcb + _colsum(dc0)
        dg_ref[...] += dg
        db_ref[...] += db
        dcb_ref[...] += dcb

        for plane, n in enumerate((0, 2)):
            for cb in range(D // CB):
                cols = slice(cb * CB, (cb + 1) * CB)
                dbr_ref[plane, :, cols] = _dot_tb(dproj_ref[n], wb_ref[n, cols, :])

    tokn = lambda n: pl.BlockSpec((n, ts, D), lambda i: (0, i, 0))
    grid = (S // ts,)
    body, c_in, c_out, c_shapes, c_scr, _ = _host(comm, 13, 8, 2, grid, body)
    outs = pl.pallas_call(
        body, name="merge_bwd", grid=grid,
        in_specs=[_tok(ts, D), _tok(ts, D), _tok(ts, D), _tok(ts, D),
                  _zcol(ts, D, 8), _zcol(ts, D, 9), _zcol(ts, D, 10), _tok(ts, D), _zcol(ts, D, 5), _row1(D), _row1(D),
                  pl.BlockSpec((None, N_BRANCHES, D, D), lambda i: (l, 0, 0, 0)),
                  pl.BlockSpec((None, D, D), lambda i: (l, 0, 0))] + c_in,
        out_specs=[_tok(ts, 3 * D), tokn(N_BRANCHES), tokn(2), _tok(ts, D), _zcol(ts, D, 2),
                   _row1(D), _row1(D), _row1(D)] + c_out,
        out_shape=[jax.ShapeDtypeStruct((S, 3 * D), BF16), jax.ShapeDtypeStruct((N_BRANCHES, S, D), BF16),
                   jax.ShapeDtypeStruct((2, S, D), F32), jax.ShapeDtypeStruct((S, D), F32),
                   jax.ShapeDtypeStruct((S, 3 * D), BF16), jax.ShapeDtypeStruct((1, D), F32),
                   jax.ShapeDtypeStruct((1, D), F32), jax.ShapeDtypeStruct((1, D), F32)] + c_shapes,
        scratch_shapes=[pltpu.VMEM((ts, D), BF16), pltpu.VMEM((ts, D), F32)] + c_scr,
        compiler_params=_params("arbitrary"),
    )(dxp, bra, brb, brc, z, z, z, c0, z, lng, lnb, wb_all, wo_all, *(comm.ins if comm else []))
    return (*outs[:8], outs[8:])


def _branch_a_bwd(z, dbr, lng, lnb, t_mix, t_mix_t, bias_full, mask):
    S = z.shape[0]
    D = lng.shape[1]
    GD = D // GROUPS
    ts = _tile(S, 256)
    nt = S // ts

    def body(zu_ref, zv_ref, zg_ref, d_ref, lng_ref, lnb_ref, t_ref, tt_ref, bias_ref, mask_ref,
             dz_ref, dws_ref, dbs_ref, dlg_ref, dlb_ref,
             v_scr, vn_scr, rstd_scr, dv_scr, dsv_acc):
        i = pl.program_id(0)

        @pl.when(i == 0)
        def _():
            dws_ref[...] = jnp.zeros_like(dws_ref)
            dlg_ref[...] = jnp.zeros_like(dlg_ref)
            dlb_ref[...] = jnp.zeros_like(dlb_ref)
            dsv_acc[...] = jnp.zeros_like(dsv_acc)

        def chunk(ci, carry):
            c0 = pl.multiple_of(ci * CHUNK, CHUNK)

            def ln_fwd(si, c):
                r = pl.multiple_of(si * 16, 16)
                vg = _gelu(zv_ref[pl.ds(pl.multiple_of(c0 + r, 16), 16), :])
                xc = vg - _mean(vg)
                rstd = lax.rsqrt(_mean(xc * xc) + LN_EPS)
                vn = xc * rstd
                vn_scr[pl.ds(r, 16), :] = vn
                rstd_scr[pl.ds(r, 16), :] = rstd
                v_scr[pl.ds(r, 16), :] = (vn * lng_ref[...] + lnb_ref[...]).astype(BF16)
                return c
            _grouped_loop(CHUNK // 16, 4, ln_fwd, 0)

            for g in range(GROUPS):
                cols = slice(g * GD, (g + 1) * GD)
                rows = pl.ds(c0, CHUNK)
                vb = v_scr[:, cols]
                sv = _dot(t_ref[g], vb) + bias_ref[:, cols]
                u, du = _gelu_and_grad(zu_ref[rows, cols])
                sg, dsg = _silu_and_grad(zg_ref[rows, cols])
                d = d_ref[rows, cols]
                dsv = (d * u) * sg
                dz_ref[rows, g * GD:(g + 1) * GD] = (((d * sv) * sg) * du).astype(BF16)
                dz_ref[rows, 2 * D + g * GD:2 * D + (g + 1) * GD] = (((d * u) * sv) * dsg).astype(BF16)
                dsvb = dsv.astype(BF16)
                dws_ref[g] += _dot_tb(dsvb, vb)
                dv_scr[:, cols] = _dot(tt_ref[g], dsvb)
                dsv_acc[:, cols] += dsv

            def ln_bwd(si, c):
                dlg, dlb = c
                r = pl.multiple_of(si * 16, 16)
                rr = pl.ds(r, 16)
                zrows = pl.ds(pl.multiple_of(c0 + r, 16), 16)
                dv = dv_scr[rr, :]
                vn = vn_scr[rr, :]
                dvn = dv * lng_ref[...]
                dvg = rstd_scr[rr, :] * ((dvn - _mean(dvn)) - vn * _mean(dvn * vn))
                _, gg = _gelu_and_grad(zv_ref[zrows, :])
                dz_ref[zrows, D:2 * D] = (dvg * gg).astype(BF16)
                return dlg + _colsum(dv * vn), dlb + _colsum(dv)
            zero = jnp.zeros((1, D), F32)
            dlg, dlb = _grouped_loop(CHUNK // 16, 4, ln_bwd, (zero, zero))
            dlg_ref[...] += dlg
            dlb_ref[...] += dlb
            return carry
        lax.fori_loop(0, ts // CHUNK, chunk, 0)

        @pl.when(i == nt - 1)
        def _():
            for g in range(GROUPS):
                cols = slice(g * GD, (g + 1) * GD)
                dws_ref[g] = dws_ref[g] * mask_ref[...]
                dbs_ref[g:g + 1, :] = _colsum(dsv_acc[:, cols].T)

    c2 = lambda shape: pl.BlockSpec(shape, lambda i: (0, 0))
    c3 = lambda: pl.BlockSpec((GROUPS, CHUNK, CHUNK), lambda i: (0, 0, 0))
    return pl.pallas_call(
        body, name="branch_a_bwd", grid=(nt,),
        in_specs=[_zcol(ts, D, 0), _zcol(ts, D, 1), _zcol(ts, D, 2), _plane(ts, D, 0), _row1(D), _row1(D),
                  c3(), c3(), c2((CHUNK, D)), c2((CHUNK, CHUNK))],
        out_specs=[_tok(ts, 3 * D), c3(), c2((GROUPS, CHUNK)), _row1(D), _row1(D)],
        out_shape=[jax.ShapeDtypeStruct((S, 3 * D), BF16), jax.ShapeDtypeStruct((GROUPS, CHUNK, CHUNK), F32),
                   jax.ShapeDtypeStruct((GROUPS, CHUNK), F32), jax.ShapeDtypeStruct((1, D), F32),
                   jax.ShapeDtypeStruct((1, D), F32)],
        scratch_shapes=[pltpu.VMEM((CHUNK, D), BF16), pltpu.VMEM((CHUNK, D), F32), pltpu.VMEM((CHUNK, 1), F32),
                        pltpu.VMEM((CHUNK, D), F32), pltpu.VMEM((CHUNK, D), F32)],
        compiler_params=_params("arbitrary"),
    )(z, z, z, dbr, lng, lnb, t_mix, t_mix_t, bias_full, mask)


def _conv_bwd(dc0, z, w_all, dz_b, l, comm=None):
    S, D = dc0.shape
    ts = _tile(S, 256)
    nt = S // ts
    CB = 256
    hb = ts // HALO
    nh = S // HALO

    def body(dc_ref, dcn_ref, za_ref, zb_ref, w_ref, dzin_ref, dz_ref, dw_ref, dc_scr, dw_acc, dsh_scr):
        del dzin_ref
        i = pl.program_id(0)

        @pl.when(i == 0)
        def _():
            dw_acc[...] = jnp.zeros_like(dw_acc)

        dc_scr[ts:ts + HALO, :] = jnp.where(i == nt - 1, 0.0, dcn_ref[...])

        def fill(si, c):
            r = _rows(si, 32)
            dc_scr[r, :] = dc_ref[r, :]
            return c
        _grouped_loop(ts // 32, 2, fill, 0)

        def colblock(ci, c):
            cols = _rows(ci, CB)
            cols_b = pl.ds(pl.multiple_of(D + ci * CB, CB), CB)
            _fill_shifts(dsh_scr, dc_scr, cols, ts)
            for s in range(ts // 32):
                rows = pl.ds(s * 32, 32)
                za = za_ref[rows, cols]
                sg = _sigmoid(zb_ref[rows, cols])
                glu = za * sg
                acc = jnp.zeros((32, CB), F32)
                for k in range(CONV_K):
                    dcs = _shifted(dsh_scr, dc_scr, cols, s * 32 + CONV_K - 1 - k, 32)
                    prod = glu * dcs
                    dw_acc[k, :, cols] += (prod[0:8] + prod[8:16]) + (prod[16:24] + prod[24:32])
                    acc = acc + dcs * w_ref[k:k + 1, cols]
                dz_ref[rows, cols] = (acc * sg).astype(BF16)
                dz_ref[rows, cols_b] = ((acc * za) * (sg * (1.0 - sg))).astype(BF16)
            return c
        lax.fori_loop(0, D // CB, colblock, 0)

        @pl.when(i == nt - 1)
        def _():
            for k in range(CONV_K):
                dw_ref[k:k + 1, :] = _colsum(dw_acc[k])

    grid = (nt,)
    body, c_in, c_out, c_shapes, c_scr, _ = _host(comm, 6, 2, 3, grid, body)
    outs = pl.pallas_call(
        body, name="conv_bwd", grid=grid,
        in_specs=[_tok(ts, D), pl.BlockSpec((HALO, D), lambda i: (jnp.minimum((i + 1) * hb, nh - 1), 0)),
                  _zcol(ts, D, 3), _zcol(ts, D, 4),
                  pl.BlockSpec((None, CONV_K, D), lambda i: (l, 0, 0)), ANY] + c_in,
        out_specs=[_tok(ts, 2 * D), pl.BlockSpec((CONV_K, D), lambda i: (0, 0))] + c_out,
        out_shape=[jax.ShapeDtypeStruct((S, 3 * D), BF16), jax.ShapeDtypeStruct((CONV_K, D), F32)] + c_shapes,
        scratch_shapes=[pltpu.VMEM((ts + HALO, D), F32), pltpu.VMEM((CONV_K, 8, D), F32),
                        pltpu.VMEM((7, ts + HALO, CB), F32)] + c_scr,
        input_output_aliases={5: 0},
        compiler_params=_params("arbitrary"),
    )(dc0, dc0, z, z, w_all, dz_b, *(comm.ins if comm else []))
    return outs[0], outs[1], outs[2:]


def _branch_c_bwd(z, dbr, kv, kvt):
    S = z.shape[0]
    M, D2 = kv.shape
    D = D2 // 2
    HD = D // HEADS
    scale = 1.0 / math.sqrt(HD)
    ts = _tile(S, 256)

    def body(q_ref, zg_ref, d_ref, kt_ref, vt_ref, k_ref, v_ref, dz_ref, dk_ref, dv_ref):
        @pl.when(pl.program_id(0) == 0)
        def _():
            dk_ref[...] = jnp.zeros_like(dk_ref)
            dv_ref[...] = jnp.zeros_like(dv_ref)

        for h in range(HEADS):
            cols = slice(h * HD, (h + 1) * HD)
            qb = q_ref[:, cols].astype(BF16)
            p = _softmax_rows(_dot(qb, kt_ref[cols, :]) * scale)
            pb = p.astype(BF16)
            att = _dot(pb, v_ref[:, cols])
            sg, dsg = _silu_and_grad(zg_ref[:, cols])
            d = d_ref[:, cols]
            dz_ref[:, D + h * HD:D + (h + 1) * HD] = ((d * att) * dsg).astype(BF16)
            datt = (d * sg).astype(BF16)
            dp = _dot(datt, vt_ref[cols, :])
            dv_ref[:, cols] += _dot_ta(pb, datt)
            ds = ((p * (dp - jnp.sum(dp * p, axis=-1, keepdims=True))) * scale).astype(BF16)
            dz_ref[:, cols] = _dot(ds, k_ref[:, cols]).astype(BF16)
            dk_ref[:, cols] += _dot_ta(ds, qb)

    return pl.pallas_call(
        body, name="branch_c_bwd", grid=(S // ts,),
        in_specs=[_zcol(ts, D, 6), _zcol(ts, D, 7), _plane(ts, D, 1),
                  pl.BlockSpec((D, M), lambda i: (0, 0)), pl.BlockSpec((D, M), lambda i: (1, 0)),
                  pl.BlockSpec((M, D), lambda i: (0, 0)), pl.BlockSpec((M, D), lambda i: (0, 1))],
        out_specs=[_tok(ts, 2 * D), pl.BlockSpec((M, D), lambda i: (0, 0)), pl.BlockSpec((M, D), lambda i: (0, 0))],
        out_shape=[jax.ShapeDtypeStruct((S, 2 * D), BF16), jax.ShapeDtypeStruct((M, D), F32),
                   jax.ShapeDtypeStruct((M, D), F32)],
        compiler_params=_params("arbitrary"),
    )(z, z, dbr, kvt, kvt, kv, kv)


def _kv_bwd(dk, dv, mn, mem, wkv_all, l, prev):
    M, D = mem.shape

    def body(*refs):
        dk_ref, dv_ref, mn_ref, mem_ref, w_ref = refs[:5]
        dw_ref, dg_ref = refs[-2:]
        dkb = dk_ref[...].astype(BF16)
        dvb = dv_ref[...].astype(BF16)
        dw_ref[:, 0:D] = _dot_ta(mn_ref[...], dkb)
        dw_ref[:, D:2 * D] = _dot_ta(mn_ref[...], dvb)
        dmn = _dot_tb(dkb, w_ref[:, 0:D]) + _dot_tb(dvb, w_ref[:, D:2 * D])
        m = mem_ref[...]
        dg_ref[...] = _colsum(dmn * (m * lax.rsqrt(_mean(m * m) + RMS_EPS)))

    full = lambda shape: pl.BlockSpec(shape, lambda i: (0, 0))
    in_specs = [full((M, D)), full((M, D)), full((M, D)), full((M, D)),
                pl.BlockSpec((None, D, 2 * D), lambda i: (l, 0, 0))]
    args = [dk, dv, mn, mem, wkv_all]
    aliases = {}
    if prev is not None:
        in_specs.append(ANY)
        args.append(prev)
        aliases = {5: 0}
    return pl.pallas_call(
        body, name="kv_bwd", grid=(1,),
        in_specs=in_specs,
        out_specs=[pl.BlockSpec((None, D, 2 * D), lambda i: (l, 0, 0)), _row1(D)],
        out_shape=[jax.ShapeDtypeStruct((wkv_all.shape[0], D, 2 * D), F32), jax.ShapeDtypeStruct((1, D), F32)],
        input_output_aliases=aliases,
        compiler_params=_params("arbitrary"),
    )(*args)


def _in_proj_bwd(pieces, w_all, x, g, dxp, l, comm=None, tiles=None, prev=None):
    S, D = x.shape
    tm = _tile(S, 1024)
    n_p = len(pieces)
    i0, n_i = tiles or (0, S // tm)
    n_in = n_p + 4 + (prev is not None)

    def body(*refs):
        dz_refs = refs[:n_p]
        w_ref, x_hbm, g_ref, dxp_hbm = refs[n_p:n_p + 4]
        dx_ref, dg_ref, acc, x_ref, dxp_ref, sem = refs[n_in:]
        i = pl.program_id(0)
        k = pl.program_id(1)
        rows = pl.ds(pl.multiple_of((i + i0) * tm, tm), tm)
        fetch = [pltpu.make_async_copy(x_hbm.at[rows, :], x_ref, sem.at[0]),
                 pltpu.make_async_copy(dxp_hbm.at[rows, :], dxp_ref, sem.at[1])]

        @pl.when(k == N_PLANES - 2)
        def _():
            for cp in fetch:
                cp.start()

        @pl.when(k == 0)
        def _():
            acc[...] = jnp.zeros_like(acc)

        @pl.when((i == 0) & (k == 0))
        def _():
            dg_ref[...] = jnp.zeros_like(dg_ref)

        for (_, lo, n), r in zip(pieces, dz_refs):
            @pl.when((k >= lo) & (k < lo + n))
            def _(r=r):
                acc[...] += _dot_tb(r[...], w_ref[...])

        @pl.when(k == N_PLANES - 1)
        def _():
            for cp in fetch:
                cp.wait()

            def slab(si, c):
                rr = _rows(si, 16)
                xs = x_ref[rr, :]
                rs = lax.rsqrt(_mean(xs * xs) + RMS_EPS)
                n = xs * rs
                dh = acc[rr, :]
                dn = dh * g_ref[...]
                dx_ref[rr, :] = rs * (dn - n * _mean(dn * n)) + dxp_ref[rr, :]
                return c + _colsum(dh * n)
            dg_ref[...] += _grouped_loop(tm // 16, 4, slab, jnp.zeros((1, D), F32))

    def piece_spec(lo, n):
        return pl.BlockSpec((tm, D), lambda i, k: (i + i0, jnp.clip(k - lo, 0, n - 1)))

    tokk = lambda: pl.BlockSpec((tm, D), lambda i, k: (i + i0, 0))
    rowk = lambda: pl.BlockSpec((1, D), lambda i, k: (0, 0))
    grid = (n_i, N_PLANES)
    body, c_in, c_out, c_shapes, c_scr, _ = _host(comm, n_in, 2, 4, grid, body)
    more_in, more_args, aliases = ([ANY], [prev], {n_in - 1: 0}) if prev is not None else ([], [], {})
    outs = pl.pallas_call(
        body, name="in_proj_bwd", grid=grid,
        in_specs=[piece_spec(lo, n) for _, lo, n in pieces]
        + [pl.BlockSpec((None, D, D), lambda i, k: (l, 0, k)), ANY, rowk(), ANY] + more_in + c_in,
        out_specs=[tokk(), rowk()] + c_out,
        out_shape=[jax.ShapeDtypeStruct((S, D), F32), jax.ShapeDtypeStruct((1, D), F32)] + c_shapes,
        scratch_shapes=[pltpu.VMEM((tm, D), F32), pltpu.VMEM((tm, D), F32), pltpu.VMEM((tm, D), F32),
                        pltpu.SemaphoreType.DMA((2,))] + c_scr,
        input_output_aliases=aliases,
        compiler_params=_params("arbitrary", "arbitrary"),
    )(*[p for p, _, _ in pieces], w_all, x, g, dxp, *more_args, *(comm.ins if comm else []))
    return outs[0], outs[1], outs[2:]


def _wgrad(a, b, out_shape, plane, col_off, prev, a_plane=None, b_plane=None):
    S = a.shape[-2]
    K1 = a.shape[-1]
    nb_cols = b.shape[-1]
    ts = _tile(S, 2048)
    tn = _tile(nb_cols, 1024)
    col_off = col_off // tn

    def body(*refs):
        a_ref, b_ref = refs[:2]
        o_ref = refs[-1]
        k = pl.program_id(1)
        prod = _dot_ta(a_ref[...].astype(BF16), b_ref[...].astype(BF16))

        @pl.when(k == 0)
        def _():
            o_ref[...] = prod

        @pl.when(k > 0)
        def _():
            o_ref[...] += prod

    if a_plane is None:
        a_spec = pl.BlockSpec((ts, K1), lambda j, k: (k, 0))
    else:
        a_spec = pl.BlockSpec((None, ts, K1), lambda j, k: (a_plane, k, 0))
    if b_plane is None:
        b_spec = pl.BlockSpec((ts, tn), lambda j, k: (k, j))
    else:
        b_spec = pl.BlockSpec((None, ts, tn), lambda j, k: (b_plane, k, j))
    in_specs, args, aliases = [a_spec, b_spec], [a, b], {}
    if prev is not None:
        in_specs.append(ANY)
        args.append(prev)
        aliases = {2: 0}
    return pl.pallas_call(
        body, name="wgrad", grid=(nb_cols // tn, S // ts),
        in_specs=in_specs,
        out_specs=pl.BlockSpec((None, K1, tn), lambda j, k: (plane, 0, col_off + j)),
        out_shape=jax.ShapeDtypeStruct(out_shape, F32),
        input_output_aliases=aliases,
        compiler_params=_params("parallel", "arbitrary"),
    )(*args)


def _position():
    x, y, c = lax.axis_index("x"), lax.axis_index("y"), lax.axis_index("c")
    chips = [(1 - x, y), (x, 1 - y), (1 - x, 1 - y)]
    return x, y, c, chips


def _sub(ref, axis, start, size):
    idx = [slice(None)] * len(ref.shape)
    idx[axis] = pl.ds(start, size)
    return ref.at[tuple(idx)]


def _dma(src, dst, ssem, rsem, to):
    return pltpu.make_async_remote_copy(src_ref=src, dst_ref=dst, send_sem=ssem, recv_sem=rsem,
                                        device_id=to, device_id_type=MESH)


def _dma_sems(*counts):
    return [pltpu.SemaphoreType.DMA((n,)) for n in counts]


def _both(a, b):
    na, oa, sa = len(a.ins), len(a.out_shapes), len(a.sem_shapes)
    phases = [(w, lambda i, o, s, fn=fn: fn(i[:na], o[:oa], s[:sa])) for w, fn in a.phases]
    phases += [(w, lambda i, o, s, fn=fn: fn(i[na:], o[oa:], s[sa:])) for w, fn in b.phases]
    return _Hosted(a.ins + b.ins, list(a.out_shapes) + list(b.out_shapes), list(a.sem_shapes) + list(b.sem_shapes), phases)


def _run_alone(comm, name):
    ci, co = len(comm.ins), len(comm.out_shapes)

    def body(*refs):
        for _, fn in comm.phases:
            fn(refs[:ci], refs[ci:ci + co], refs[ci + co:])

    return pl.pallas_call(
        body, name=name, in_specs=[ANY] * ci, out_specs=[ANY] * co, out_shape=list(comm.out_shapes),
        scratch_shapes=list(comm.sem_shapes),
    )(*comm.ins)


def _gather_comm(shards, layers, axes):
    n = len(shards)
    out_shapes = []
    for s, (ba, _) in zip(shards, axes):
        shp = list(s.shape[1:])
        shp[ba] *= N_CHIPS
        out_shapes.append(jax.ShapeDtypeStruct(tuple(shp), s.dtype))

    def half(ref, t, h):
        ha = axes[t][1]
        if ha is None:
            return ref
        hs = shards[t].shape[1 + ha] // 2
        return _sub(ref, ha, h * hs, hs)

    def block(outs, t, blk):
        ba = axes[t][0]
        bs = shards[t].shape[1 + ba]
        return _sub(outs[t], ba, blk * bs, bs)

    def start(ins, outs, sems):
        s_own, r_own, s_ici, r_ici, _, _ = sems
        x, y, c, chips = _position()
        j = 2 * x + y
        for t in range(n):
            _dma(ins[t].at[layers[t]], block(outs, t, j), s_own.at[t], r_own.at[t], (x, y, 1 - c)).start()
            for k, (px, py) in enumerate(chips):
                _dma(half(ins[t].at[layers[t]], t, c), half(block(outs, t, j), t, c),
                     s_ici.at[3 * t + k], r_ici.at[3 * t + k], (px, py, c)).start()

    def forward(ins, outs, sems):
        _, _, s_ici, r_ici, s_fwd, r_fwd = sems
        x, y, c, chips = _position()
        for t in range(n):
            if axes[t][1] is None:
                continue
            for k, (px, py) in enumerate(chips):
                got = half(block(outs, t, 2 * px + py), t, c)
                _dma(got, got, s_ici.at[3 * t + k], r_ici.at[3 * t + k], (px, py, c)).wait_recv()
                _dma(got, got, s_fwd.at[3 * t + k], r_fwd.at[3 * t + k], (x, y, 1 - c)).start()

    def finish(ins, outs, sems):
        s_own, r_own, s_ici, r_ici, s_fwd, r_fwd = sems
        x, y, c, chips = _position()
        j = 2 * x + y
        sib = (x, y, 1 - c)
        for t in range(n):
            for k, (px, py) in enumerate(chips):
                i = 3 * t + k
                sent = half(block(outs, t, j), t, c)
                _dma(sent, sent, s_ici.at[i], r_ici.at[i], (px, py, c)).wait_send()
                theirs = block(outs, t, 2 * px + py)
                if axes[t][1] is None:
                    _dma(theirs, theirs, s_ici.at[i], r_ici.at[i], (px, py, c)).wait_recv()
                else:
                    got, other = half(theirs, t, c), half(theirs, t, 1 - c)
                    _dma(got, got, s_fwd.at[i], r_fwd.at[i], sib).wait_send()
                    _dma(other, other, s_fwd.at[i], r_fwd.at[i], sib).wait_recv()
            _dma(ins[t].at[layers[t]], block(outs, t, j), s_own.at[t], r_own.at[t], sib).wait()

    return _Hosted(list(shards), out_shapes, _dma_sems(n, n, 3 * n, 3 * n, 3 * n, 3 * n),
                   [("first", start), ("late", forward), ("last", finish)])


def _halved(shape, axis):
    shp = list(shape)
    shp[axis] //= 2
    return tuple(shp)


def _pair_comm(grads, axes):
    n = len(grads)

    def copies(ins, outs, sems):
        x, y, c, _ = _position()
        cps = []
        for t in range(n):
            ha = axes[t][1]
            hs = grads[t].shape[ha] // 2
            cps.append(_dma(_sub(ins[t], ha, (1 - c) * hs, hs), outs[t], sems[0].at[t], sems[1].at[t], (x, y, 1 - c)))
        return cps

    def start(ins, outs, sems):
        for cp in copies(ins, outs, sems):
            cp.start()

    def finish(ins, outs, sems):
        for cp in copies(ins, outs, sems):
            cp.wait()

    return _Hosted(list(grads), [jax.ShapeDtypeStruct(_halved(g.shape, ax[1]), g.dtype) for g, ax in zip(grads, axes)],
                   _dma_sems(n, n), [("first", start), ("last", finish)])


def _block_comm(sums, axes):
    n = len(sums)
    out_shapes = []
    for s, (ba, _) in zip(sums, axes):
        shp = list(s.shape)
        shp[ba] //= N_CHIPS
        out_shapes.append(jax.ShapeDtypeStruct((3,) + tuple(shp), s.dtype))

    def copies(ins, outs, sems):
        x, y, c, chips = _position()
        cps = []
        for t in range(n):
            ba = axes[t][0]
            bs = sums[t].shape[ba] // N_CHIPS
            for k, (px, py) in enumerate(chips):
                cps.append(_dma(_sub(ins[t], ba, (2 * px + py) * bs, bs), outs[t].at[k],
                                sems[0].at[3 * t + k], sems[1].at[3 * t + k], (px, py, c)))
        return cps

    def start(ins, outs, sems):
        for cp in copies(ins, outs, sems):
            cp.start()

    def finish(ins, outs, sems):
        for cp in copies(ins, outs, sems):
            cp.wait()

    return _Hosted(list(sums), out_shapes, _dma_sems(3 * n, 3 * n), [("first", start), ("last", finish)])


def _sibling_share(shards, axes, layer):
    n = len(shards)

    def body(*refs):
        outs = refs[n:2 * n]
        ssem, rsem = refs[2 * n:]
        x, y, c, _ = _position()
        cps = []
        for t in range(n):
            ha = axes[t][1]
            hs = shards[t].shape[1 + ha] // 2
            mine = _sub(outs[t].at[layer], ha, c * hs, hs)
            cps.append(_dma(mine, mine, ssem.at[t], rsem.at[t], (x, y, 1 - c)))
        for cp in cps:
            cp.start()
        for cp in cps:
            cp.wait()

    return pl.pallas_call(
        body, name="sibling_share",
        in_specs=[ANY] * n, out_specs=[ANY] * n,
        out_shape=[jax.ShapeDtypeStruct(s.shape, s.dtype) for s in shards],
        input_output_aliases={t: t for t in range(n)},
        scratch_shapes=_dma_sems(n, n),
    )(*shards)


def _small_allreduce(p):
    R, C = p.shape
    Rh = R // 2
    assert Rh % 8 == 0

    def body(p_ref, o_ref, sib_buf, chip_sums, ssem, rsem):
        x, y, c, chips = _position()
        j = 2 * x + y
        sib_id = (x, y, 1 - c)
        sib = _dma(p_ref, sib_buf, ssem.at[0], rsem.at[0], sib_id)
        sib.start()
        sib.wait()
        chip_sums[j] = p_ref[...] + sib_buf[...]

        def half(blk, h):
            return chip_sums.at[blk, pl.ds(pl.multiple_of(h * Rh, 8), Rh), :]

        sent = [_dma(half(j, c), half(j, c), ssem.at[1 + k], rsem.at[1 + k], (px, py, c))
                for k, (px, py) in enumerate(chips)]
        for cp in sent:
            cp.start()
        passed = []
        for k, (px, py) in enumerate(chips):
            got = half(2 * px + py, c)
            _dma(got, got, ssem.at[1 + k], rsem.at[1 + k], (px, py, c)).wait_recv()
            cp = _dma(got, got, ssem.at[4 + k], rsem.at[4 + k], sib_id)
            cp.start()
            passed.append(cp)
        for k, (px, py) in enumerate(chips):
            other = half(2 * px + py, 1 - c)
            _dma(other, other, ssem.at[4 + k], rsem.at[4 + k], sib_id).wait_recv()
        for cp in sent + passed:
            cp.wait_send()
        o_ref[...] = ((chip_sums[0] + chip_sums[1]) + chip_sums[2]) + chip_sums[3]

    vm = pl.BlockSpec(memory_space=pltpu.VMEM)
    return pl.pallas_call(
        body, name="small_allreduce",
        in_specs=[vm], out_specs=vm, out_shape=jax.ShapeDtypeStruct((R, C), F32),
        scratch_shapes=[pltpu.VMEM((R, C), F32), pltpu.VMEM((N_CHIPS, R, C), F32)] + _dma_sems(7, 7),
        compiler_params=pltpu.CompilerParams(vmem_limit_bytes=VMEM_LIMIT_BYTES),
    )(p)


def _add_half(g, recv, half_axis, pos):
    B, R, C = recv.shape
    tr = _tile(R, 256)
    tc = _tile(C, 2816)
    nr, nc = R // tr, C // tc
    if half_axis == 1:
        g_map = lambda b, i, j, pos_ref: (b, pos_ref[0] * nr + i, j)
    else:
        g_map = lambda b, i, j, pos_ref: (b, i, pos_ref[0] * nc + j)

    def body(pos_ref, g_ref, r_ref, o_ref, ob_ref):
        del pos_ref
        s = g_ref[...] + r_ref[...]
        o_ref[...] = s
        ob_ref[...] = s.astype(BF16)

    spec = lambda: pl.BlockSpec((None, tr, tc), lambda b, i, j, pos_ref: (b, i, j))
    return pl.pallas_call(
        body, name="add_half",
        grid_spec=pltpu.PrefetchScalarGridSpec(
            num_scalar_prefetch=1, grid=(B, nr, nc),
            in_specs=[pl.BlockSpec((None, tr, tc), g_map), spec()],
            out_specs=[spec(), spec()]),
        out_shape=[jax.ShapeDtypeStruct((B, R, C), F32), jax.ShapeDtypeStruct((B, R, C), BF16)],
        compiler_params=_params("parallel", "parallel", "parallel"),
    )(pos, g, recv)


def _sum_half(own, recv, axes, pos, layer, prev):
    _, B, rs, cs = recv.shape
    ba, ha = axes
    tr = _tile(rs, 128)
    nr = rs // tr

    def body(pos_ref, o_ref, r0_ref, r1_ref, r2_ref, *rest):
        out_ref = rest[-1]
        out_ref[...] = (((o_ref[...] + r0_ref[...].astype(F32)) + r1_ref[...].astype(F32))
                        + r2_ref[...].astype(F32))

    if ba == 2:
        own_spec = pl.BlockSpec((None, tr, cs), lambda b, i, pos_ref: (b, i, pos_ref[1]))
    else:
        own_spec = pl.BlockSpec((None, tr, cs), lambda b, i, pos_ref: (b, pos_ref[1] * nr + i, 0))
    if ha == 1:
        out_shape, out_map = (B, 2 * rs, cs), (lambda b, i, pos_ref: (layer, b, pos_ref[0] * nr + i, 0))
    else:
        out_shape, out_map = (B, rs, 2 * cs), (lambda b, i, pos_ref: (layer, b, i, pos_ref[0]))
    rspec = lambda k: pl.BlockSpec((None, None, tr, cs), lambda b, i, pos_ref: (k, b, i, 0))
    more_in, more_args, aliases = ([ANY], [prev], {5: 0}) if prev is not None else ([], [], {})
    return pl.pallas_call(
        body, name="sum_half",
        grid_spec=pltpu.PrefetchScalarGridSpec(
            num_scalar_prefetch=1, grid=(B, nr),
            in_specs=[own_spec, rspec(0), rspec(1), rspec(2)] + more_in,
            out_specs=pl.BlockSpec((None, None, tr, cs), out_map)),
        out_shape=jax.ShapeDtypeStruct((N_LAYERS,) + out_shape, F32),
        input_output_aliases=aliases,
        compiler_params=_params("parallel", "parallel"),
    )(pos, own, recv, recv, recv, *more_args)


def _adamw(w, g, m, v):
    shape = w.shape
    C = shape[-1]
    R = w.size // C
    tr = R
    for cand in (512, 256, 128, 64, 32, 16, 8):
        if R % cand == 0 and cand * C * 4 <= (1 << 20):
            tr = cand
            break
    bc1 = 1.0 - ADAM_B1 ** ADAM_STEP
    bc2 = 1.0 - ADAM_B2 ** ADAM_STEP

    def body(w_ref, g_ref, m_ref, v_ref, d_ref, mo_ref, vo_ref, go_ref):
        gg = g_ref[...]
        go_ref[...] = gg
        mn = ADAM_B1 * m_ref[...] + (1.0 - ADAM_B1) * gg
        vn = ADAM_B2 * v_ref[...] + (1.0 - ADAM_B2) * (gg * gg)
        mo_ref[...] = mn
        vo_ref[...] = vn
        d_ref[...] = -ADAM_LR * ((mn / bc1) / (jnp.sqrt(vn / bc2) + ADAM_EPS) + ADAM_WD * w_ref[...])

    spec = lambda: pl.BlockSpec((tr, C), lambda i: (i, 0))
    outs = pl.pallas_call(
        body, name="adamw", grid=(R // tr,),
        in_specs=[spec()] * 4, out_specs=[spec()] * 4,
        out_shape=[jax.ShapeDtypeStruct((R, C), F32)] * 4,
        compiler_params=_params("parallel"),
    )(*[a.reshape(R, C) for a in (w, g, m, v)])
    return tuple(o.reshape(shape) for o in outs)


def kernel(x, mem, norm_g, mem_norm_g, w_in, gmlp_ln_g, gmlp_ln_b, w_s, b_s, conv_w, conv_b, conv_ln_g, conv_ln_b, w_kv, w_branch, w_out, final_norm_g, loss_target, m_norm_g, m_mem_norm_g, m_w_in, m_gmlp_ln_g, m_gmlp_ln_b, m_w_s, m_b_s, m_conv_w, m_conv_b, m_conv_ln_g, m_conv_ln_b, m_w_kv, m_w_branch, m_w_out, m_final_norm_g, v_norm_g, v_mem_norm_g, v_w_in, v_gmlp_ln_g, v_gmlp_ln_b, v_w_s, v_b_s, v_conv_w, v_conv_b, v_conv_ln_g, v_conv_ln_b, v_w_kv, v_w_branch, v_w_out, v_final_norm_g):
    xs, mems, tgt = x[0], mem[0], loss_target[0]
    S, D = xs.shape
    assert D // GROUPS == LANES and S % CHUNK == 0 and w_s.shape[-1] == CHUNK
    L = N_LAYERS

    assert L == 2
    shards = [w_in.astype(BF16)[:, None], w_kv.astype(BF16)[:, None], w_branch.astype(BF16), w_out.astype(BF16)[:, None],
              conv_w[:, None]]
    gather_axes = [(2, 1), (2, 1), (1, 2), (1, 2), (2, None)]
    reduce_axes = gather_axes[:4]
    win0, cw0 = _run_alone(_gather_comm([shards[0], shards[4]], [0, 0], [gather_axes[0], gather_axes[4]]), "gather_first")
    gather_rest = _gather_comm(shards[1:4] + shards, [0] * 3 + [1] * 5, gather_axes[1:4] + gather_axes)

    tri = jnp.tril(jnp.ones((CHUNK, CHUNK), F32))
    t_mix = (w_s * tri).astype(BF16)
    t_mix_t = jnp.swapaxes(t_mix, -1, -2)
    bias_full = jnp.repeat(jnp.swapaxes(b_s, -1, -2), D // GROUPS, axis=-1)
    row = lambda a, l: a[l][None, :]
    xi, yi, ci = lax.axis_index("x"), lax.axis_index("y"), lax.axis_index("c")
    pos = jnp.stack([ci, 2 * xi + yi]).astype(jnp.int32)

    saved = []
    h = xs
    weights = [None] * L
    for l in range(L):
        if l == 0:
            z, hb, c0, fetched = _in_proj_conv(h, row(norm_g, l), win0, cw0, row(conv_b, l), 0, gather_rest)
            weights = [[win0] + list(fetched[:3]) + [cw0], list(fetched[3:])]
        else:
            z, hb, c0, _ = _in_proj_conv(h, row(norm_g, l), weights[l][0], weights[l][4], row(conv_b, l), 0)
        win, wkv, wb, wo, cw = weights[l]
        mn, kv, kvt = _kv_fwd(mems, row(mem_norm_g, l), wkv, 0)
        brc = _branch_c_fwd(z, kv, kvt)
        bra, brb, merged, h_next = _merge_fwd(brc, z, h, c0, row(conv_ln_g, l), row(conv_ln_b, l), row(gmlp_ln_g, l),
                                              row(gmlp_ln_b, l), t_mix[l], bias_full[l], wb[None], wo, 0)
        saved.append((h, z, hb, bra, c0, brb, mn, kv, kvt, brc, merged))
        h = h_next

    dx, loss_cols, d_final_g = _loss_bwd(h, tgt, final_norm_g[None, :])
    loss = lax.psum(0.5 * jnp.sum(loss_cols) / D, ("x", "y", "c"))

    def add_halves(grads, recv, axes):
        return [_add_half(g, r, ax[1], pos) for g, r, ax in zip(grads, recv, axes)]

    def finish_reduction(sums, blocks, axes, layer, prevs):
        halves = [_sum_half(s, b, ax, pos, layer, p) for (s, _), b, ax, p in zip(sums, blocks, axes, prevs)]
        return list(_sibling_share(halves, axes, layer))

    small = {k: [None] * L for k in ("norm_g", "mem_norm_g", "gmlp_ln_g", "gmlp_ln_b", "w_s", "b_s", "conv_w",
                                     "conv_b", "conv_ln_g", "conv_ln_b")}
    early_axes = reduce_axes[1:]
    pending, stacked = None, None
    for l in reversed(range(L)):
        h_in, z, hb, bra, c0, brb, mn, kv, kvt, brc, merged = saved[l]
        win, wkv, wb, wo, cw = weights[l]
        dz_m, dproj, dbr, dc0, dz_b, dcg, dcbeta, dcb, recv = _merge_bwd(
            dx, bra, brb, brc, z, c0, row(conv_ln_g, l), row(conv_ln_b, l), wb[None], wo, 0,
            _pair_comm(pending, reduce_axes) if pending else None)
        sums = add_halves(pending, recv, reduce_axes) if pending else []
        g_o = _wgrad(merged, dx, (1, D, D), 0, 0, None)
        g_b = None
        for n, br in enumerate((bra, brb, brc)):
            g_b = _wgrad(br, dproj, (N_BRANCHES, D, D), n, 0, g_b, b_plane=n)
        dz_a, dws, dbs, dlg, dlb = _branch_a_bwd(z, dbr, row(gmlp_ln_g, l), row(gmlp_ln_b, l),
                                                t_mix[l], t_mix_t[l], bias_full[l], tri)
        dz_c, dk, dv = _branch_c_bwd(z, dbr, kv, kvt)
        g_kv, dmg = _kv_bwd(dk, dv, mn, mems, wkv, 0, None)
        ready = [g_kv, g_b, g_o]
        riding = _both(_block_comm([sb for _, sb in sums], reduce_axes), _pair_comm(ready, early_axes)) if pending else None
        dz_b, dcw, got = _conv_bwd(dc0, z, cw, dz_b, 0, riding)
        sums_early = add_halves(ready, got[4:], early_axes) if pending else []
        if pending:
            stacked = finish_reduction(sums, got[:4], reduce_axes, l + 1, [None] * 4)
        pieces =[(dz_a, 0, 3), (dz_b, 3, 3), (dz_c, 6, 2), (dz_m, 8, 3)]
        g_in = None
        for dzp, lo, _ in pieces:
            g_in = _wgrad(hb, dzp, (1, D, N_PLANES * D), 0, lo * D, g_in)
        if pending:
            n_t = S // _tile(S, 1024)
            n_a, n_c = max(1, (3 * n_t) // 8), (1 if n_t >= 3 else 0)
            riding = _block_comm([sb for _, sb in sums_early], early_axes)
            dx_a, dng_a, got = _in_proj_bwd(pieces, win, h_in, row(norm_g, l), dx, 0,
                                            _both(riding, _pair_comm([g_in], reduce_axes[:1])), (0, n_a))
            stacked[1:] = finish_reduction(sums_early, got[:3], early_axes, l, stacked[1:])
            sums_in = add_halves([g_in], got[3:], reduce_axes[:1])
            dx_b, dng_b, blocks_in = _in_proj_bwd(pieces, win, h_in, row(norm_g, l), dx, 0,
                                                  _block_comm([sb for _, sb in sums_in], reduce_axes[:1]),
                                                  (n_a, n_t - n_a - n_c), dx_a)
            dng = dng_a + dng_b
            if n_c:
                dx_b, dng_c, _ = _in_proj_bwd(pieces, win, h_in, row(norm_g, l), dx, 0, None, (n_t - n_c, n_c), dx_b)
                dng = dng + dng_c
            dx = dx_b
            stacked[:1] = finish_reduction(sums_in, blocks_in, reduce_axes[:1], l, stacked[:1])
        else:
            dx, dng, _ = _in_proj_bwd(pieces, win, h_in, row(norm_g, l), dx, 0)
        pending = [g_in, g_kv, g_b, g_o]
        for k, val in (("norm_g", dng), ("mem_norm_g", dmg), ("gmlp_ln_g", dlg), ("gmlp_ln_b", dlb), ("w_s", dws),
                       ("b_s", dbs), ("conv_w", dcw), ("conv_b", dcb), ("conv_ln_g", dcg), ("conv_ln_b", dcbeta)):
            small[k][l] = val
    grad_x = dx[None]

    grad_w_in, grad_w_kv, grad_w_branch, grad_w_out = [
        g.reshape(w.shape) for g, w in zip(stacked, (w_in, w_kv, w_branch, w_out))]

    order = [("norm_g", norm_g.shape), ("mem_norm_g", mem_norm_g.shape), ("gmlp_ln_g", gmlp_ln_g.shape),
             ("gmlp_ln_b", gmlp_ln_b.shape), ("w_s", w_s.shape), ("b_s", b_s.shape),
             ("conv_w", (L, CONV_K, D)), ("conv_b", conv_b.shape), ("conv_ln_g", conv_ln_g.shape),
             ("conv_ln_b", conv_ln_b.shape)]
    parts = [jnp.stack([v.reshape(shp[1:]) for v in small[k]]).reshape(-1, LANES) for k, shp in order]
    parts.append(d_final_g.reshape(-1, LANES))
    sizes = [p.shape[0] for p in parts]
    assert all(s % 8 == 0 for s in sizes)
    if sum(sizes) % 16:
        parts.append(jnp.zeros((16 - sum(sizes) % 16, LANES), F32))
    total = _small_allreduce(jnp.concatenate(parts, axis=0))
    red, off = {}, 0
    for (k, shp), n in zip(order + [("final_norm_g", final_norm_g.shape)], sizes):
        red[k] = total[off:off + n].reshape(shp)
        off += n
    cs = conv_w.shape[-1]
    red["conv_w"] = lax.dynamic_slice_in_dim(red["conv_w"], (2 * xi + yi) * cs, cs, axis=2)
    red.update(w_in=grad_w_in, w_kv=grad_w_kv, w_branch=grad_w_branch, w_out=grad_w_out)

    names = ["norm_g", "mem_norm_g", "w_in", "gmlp_ln_g", "gmlp_ln_b", "w_s", "b_s", "conv_w", "conv_b",
             "conv_ln_g", "conv_ln_b", "w_kv", "w_branch", "w_out", "final_norm_g"]
    weights = dict(norm_g=norm_g, mem_norm_g=mem_norm_g, w_in=w_in, gmlp_ln_g=gmlp_ln_g, gmlp_ln_b=gmlp_ln_b,
                   w_s=w_s, b_s=b_s, conv_w=conv_w, conv_b=conv_b, conv_ln_g=conv_ln_g, conv_ln_b=conv_ln_b,
                   w_kv=w_kv, w_branch=w_branch, w_out=w_out, final_norm_g=final_norm_g)
    ms = dict(norm_g=m_norm_g, mem_norm_g=m_mem_norm_g, w_in=m_w_in, gmlp_ln_g=m_gmlp_ln_g, gmlp_ln_b=m_gmlp_ln_b,
              w_s=m_w_s, b_s=m_b_s, conv_w=m_conv_w, conv_b=m_conv_b, conv_ln_g=m_conv_ln_g, conv_ln_b=m_conv_ln_b,
              w_kv=m_w_kv, w_branch=m_w_branch, w_out=m_w_out, final_norm_g=m_final_norm_g)
    vs = dict(norm_g=v_norm_g, mem_norm_g=v_mem_norm_g, w_in=v_w_in, gmlp_ln_g=v_gmlp_ln_g, gmlp_ln_b=v_gmlp_ln_b,
              w_s=v_w_s, b_s=v_b_s, conv_w=v_conv_w, conv_b=v_conv_b, conv_ln_g=v_conv_ln_g, conv_ln_b=v_conv_ln_b,
              w_kv=v_w_kv, w_branch=v_w_branch, w_out=v_w_out, final_norm_g=v_final_norm_g)
    grads, deltas, new_m, new_v = [], [], [], []
    for k in names:
        w2 = weights[k] if weights[k].ndim > 1 else weights[k][None, :]
        d, mo, vo, go = _adamw(w2, red[k].reshape(w2.shape), ms[k].reshape(w2.shape), vs[k].reshape(w2.shape))
        grads.append(go.reshape(weights[k].shape))
        deltas.append(d.reshape(weights[k].shape))
        new_m.append(mo.reshape(weights[k].shape))
        new_v.append(vo.reshape(weights[k].shape))
    return (loss, grad_x, *grads, *deltas, *new_m, *new_v)
```

```python
import math

import jax
import jax.numpy as jnp
from jax import lax
from jax.experimental import pallas as pl
from jax.experimental.pallas import tpu as pltpu

F32 = jnp.float32
BF16 = jnp.bfloat16

N_LAYERS = 2
N_BRANCHES = 3
N_PLANES = 11
N_CHIPS = 4
CHUNK = 128
GROUPS = 8
HEADS = 4
CONV_K = 31
HALO = 32
LANES = 128
RMS_EPS = 1e-6
LN_EPS = 1e-5
ADAM_LR, ADAM_B1, ADAM_B2, ADAM_EPS, ADAM_WD, ADAM_STEP = 0.001, 0.9, 0.999, 1e-08, 0.01, 10
VMEM_LIMIT_BYTES = 48 * 1024 * 1024
MESH = pl.DeviceIdType.MESH
ANY = pl.BlockSpec(memory_space=pl.ANY)


def _params(*sem):
    return pltpu.CompilerParams(dimension_semantics=sem, vmem_limit_bytes=VMEM_LIMIT_BYTES)


def _tile(n, pref):
    t = min(n, pref)
    assert n % t == 0, (n, t)
    return t


def _dot(a, b):
    return jnp.dot(a, b, preferred_element_type=F32)


def _dot_tb(a, b):
    return lax.dot_general(a, b, (((1,), (1,)), ((), ())), preferred_element_type=F32)


def _dot_ta(a, b):
    return lax.dot_general(a, b, (((0,), (0,)), ((), ())), preferred_element_type=F32)


def _sigmoid(x):
    return 1.0 / (1.0 + jnp.exp(-x))


def _silu(x):
    return x * _sigmoid(x)


def _silu_and_grad(x):
    s = _sigmoid(x)
    return x * s, s * (1.0 + x * (1.0 - s))


_GELU_C = math.sqrt(2.0 / math.pi)
_GELU_A = 0.044715


def _gelu(x):
    return x * (0.5 * (1.0 + jnp.tanh(_GELU_C * (x + _GELU_A * (x * x * x)))))


def _gelu_and_grad(x):
    x2 = x * x
    t = jnp.tanh(_GELU_C * (x + _GELU_A * (x2 * x)))
    cdf = 0.5 * (1.0 + t)
    dcdf = 0.5 * (1.0 - t * t) * (_GELU_C * (1.0 + 3.0 * _GELU_A * x2))
    return x * cdf, cdf + x * dcdf


def _rows(i, n):
    return pl.ds(pl.multiple_of(i * n, n), n)


def _grouped_loop(n, group, body, init):
    assert n % group == 0

    def trip(i, c):
        for u in range(group):
            c = body(i * group + u, c)
        return c
    return lax.fori_loop(0, n // group, trip, init)


def _mean(x):
    return jnp.mean(x, axis=-1, keepdims=True)


def _colsum(x):
    return jnp.sum(x, axis=0, keepdims=True)


def _tok(ts, d):
    return pl.BlockSpec((ts, d), lambda i: (i, 0))


def _zcol(ts, d, n):
    return pl.BlockSpec((ts, d), lambda i: (i, n))


def _row1(d):
    return pl.BlockSpec((1, d), lambda i: (0, 0))


def _plane(ts, d, n):
    return pl.BlockSpec((None, ts, d), lambda i: (n, i, 0))


class _Hosted:
    def __init__(self, ins, out_shapes, sem_shapes, phases):
        self.ins, self.out_shapes, self.sem_shapes, self.phases = ins, out_shapes, sem_shapes, phases


def _host(comm, n_in, n_out, n_scr, grid, body):
    if comm is None:
        return body, [], [], [], [], None
    ci, co = len(comm.ins), len(comm.out_shapes)
    total = math.prod(grid)
    at = {"first": 0, "late": (7 * total) // 8, "last": total - 1}

    def wrapped(*refs):
        ins, cin = refs[:n_in], refs[n_in:n_in + ci]
        o0 = n_in + ci
        outs, cout = refs[o0:o0 + n_out], refs[o0 + n_out:o0 + n_out + co]
        s0 = o0 + n_out + co
        scr, csem = refs[s0:s0 + n_scr], refs[s0 + n_scr:]
        step = pl.program_id(0)
        for a in range(1, len(grid)):
            step = step * grid[a] + pl.program_id(a)

        def run(last):
            for w, fn in comm.phases:
                if (w == "last") == last:
                    @pl.when(step == at.get(w, w))
                    def _(fn=fn):
                        fn(cin, cout, csem)
        run(False)
        body(*ins, *outs, *scr)
        run(True)

    return wrapped, [ANY] * ci, [ANY] * co, list(comm.out_shapes), list(comm.sem_shapes), ("arbitrary",) * len(grid)


SH = HALO - CONV_K + 1


def _fill_shifts(sh_scr, src_scr, cols, ts, base=0):
    for j in range(1, 8):
        for g in range(ts // 32):
            sh_scr[j - 1, pl.ds(g * 32, 32), :] = src_scr[pl.ds(base + g * 32 + j, 32), cols]
        sh_scr[j - 1, pl.ds(ts, HALO - 8), :] = src_scr[pl.ds(base + ts + j, HALO - 8), cols]


def _shifted(sh_scr, src_scr, cols, row0, n, base=0):
    q, j = divmod(row0, 8)
    if j == 0:
        return src_scr[pl.ds(base + row0, n), cols]
    return sh_scr[j - 1, pl.ds(8 * q, n), :]


def _in_proj_conv(x, g, w_all, cw_all, cb, l, comm=None):
    S, D = x.shape
    N = w_all.shape[2]
    tm = _tile(S, 1024)
    QR, CB = 256, 256
    nq = tm // QR
    PA, PB = 3, 4
    assert tm % QR == 0 and PB + 1 + nq <= N // D
    grid = (S // tm, N // D)

    def body(x_ref, g_ref, w_ref, cw_ref, cb_ref, z_ref, hb_ref, c0_ref, h_scr, za_scr, glu_scr, sh_scr):
        i, j = pl.program_id(0), pl.program_id(1)

        @pl.when(j == 0)
        def _():
            def slab(si, c):
                r = _rows(si, 16)
                xs = x_ref[r, :]
                hb = ((xs * lax.rsqrt(_mean(xs * xs) + RMS_EPS)) * g_ref[...]).astype(BF16)
                h_scr[r, :] = hb
                hb_ref[r, :] = hb
                return c
            _grouped_loop(tm // 16, 4, slab, 0)

        def matmul_cols(cols):
            zc = _dot(h_scr[...], w_ref[:, cols])
            z_ref[:, cols] = zc
            return zc

        @pl.when(j == PA)
        def _():
            def colblock(ci, c):
                cols = _rows(ci, CB)
                za_scr[:, cols] = matmul_cols(cols)
                return c
            lax.fori_loop(0, D // CB, colblock, 0)

        @pl.when((j == PB) & (i == 0))
        def _():
            glu_scr[0:HALO, :] = jnp.zeros((HALO, D), F32)

        @pl.when((j == PB) & (i > 0))
        def _():
            glu_scr[0:HALO, :] = glu_scr[tm:tm + HALO, :]

        @pl.when(j == PB)
        def _():
            def colblock(ci, c):
                cols = _rows(ci, CB)
                glu_scr[HALO:HALO + tm, cols] = za_scr[:, cols] * _sigmoid(matmul_cols(cols))
                return c
            lax.fori_loop(0, D // CB, colblock, 0)

        for q in range(nq):
            @pl.when(j == PB + 1 + q)
            def _(q=q):
                def colblock(ci, c):
                    cols = _rows(ci, CB)
                    matmul_cols(cols)
                    _fill_shifts(sh_scr, glu_scr, cols, QR, q * QR)
                    for s in range(QR // 32):
                        acc = jnp.zeros((32, CB), F32)
                        for k in range(CONV_K):
                            acc = acc + _shifted(sh_scr, glu_scr, cols, s * 32 + k + SH, 32, q * QR) * cw_ref[k:k + 1, cols]
                        c0_ref[pl.ds(q * QR + s * 32, 32), cols] = acc + cb_ref[:, cols]
                    return c
                lax.fori_loop(0, D // CB, colblock, 0)

        @pl.when((j < PA) | (j > PB + nq))
        def _():
            z_ref[...] = _dot(h_scr[...], w_ref[...])

    tile = lambda: pl.BlockSpec((tm, D), lambda i, j: (i, 0))
    body, c_in, c_out, c_shapes, c_scr, _ = _host(comm, 5, 3, 4, grid, body)
    outs = pl.pallas_call(
        body, name="in_proj_conv", grid=grid,
        in_specs=[tile(), pl.BlockSpec((1, D), lambda i, j: (0, 0)),
                  pl.BlockSpec((None, D, D), lambda i, j: (l, 0, j)),
                  pl.BlockSpec((None, CONV_K, D), lambda i, j: (l, 0, 0)),
                  pl.BlockSpec((1, D), lambda i, j: (0, 0))] + c_in,
        out_specs=[pl.BlockSpec((tm, D), lambda i, j: (i, j)), tile(), tile()] + c_out,
        out_shape=[jax.ShapeDtypeStruct((S, N), F32), jax.ShapeDtypeStruct((S, D), BF16),
                   jax.ShapeDtypeStruct((S, D), F32)] + c_shapes,
        scratch_shapes=[pltpu.VMEM((tm, D), BF16), pltpu.VMEM((tm, D), F32), pltpu.VMEM((tm + HALO, D), F32),
                        pltpu.VMEM((7, QR + HALO, CB), F32)] + c_scr,
        compiler_params=_params("arbitrary", "arbitrary"),
    )(x, g, w_all, cw_all, cb, *(comm.ins if comm else []))
    return outs[0], outs[1], outs[2], outs[3:]


def _kv_fwd(mem, mg, wkv_all, l):
    M, D = mem.shape

    def body(mem_ref, g_ref, w_ref, mn_ref, kv_ref, kvt_ref):
        m = mem_ref[...]
        mn = ((m * lax.rsqrt(_mean(m * m) + RMS_EPS)) * g_ref[...]).astype(BF16)
        mn_ref[...] = mn
        kv = _dot(mn, w_ref[...])
        kv_ref[...] = kv.astype(BF16)
        kvt_ref[...] = kv.T.astype(BF16)

    return pl.pallas_call(
        body, name="kv_fwd", grid=(1,),
        in_specs=[pl.BlockSpec((M, D), lambda i: (0, 0)), _row1(D),
                  pl.BlockSpec((None, D, 2 * D), lambda i: (l, 0, 0))],
        out_specs=[pl.BlockSpec((M, D), lambda i: (0, 0)), pl.BlockSpec((M, 2 * D), lambda i: (0, 0)),
                   pl.BlockSpec((2 * D, M), lambda i: (0, 0))],
        out_shape=[jax.ShapeDtypeStruct((M, D), BF16), jax.ShapeDtypeStruct((M, 2 * D), BF16),
                   jax.ShapeDtypeStruct((2 * D, M), BF16)],
        compiler_params=_params("arbitrary"),
    )(mem, mg, wkv_all)


def _softmax_rows(s):
    e = jnp.exp(s - jnp.max(s, axis=-1, keepdims=True))
    return e / jnp.sum(e, axis=-1, keepdims=True)


def _branch_c_fwd(z, kv, kvt):
    S = z.shape[0]
    M, D2 = kv.shape
    D = D2 // 2
    HD = D // HEADS
    scale = 1.0 / math.sqrt(HD)
    ts = _tile(S, 256)

    def body(q_ref, zg_ref, kt_ref, v_ref, o_ref):
        for h in range(HEADS):
            cols = slice(h * HD, (h + 1) * HD)
            p = _softmax_rows(_dot(q_ref[:, cols].astype(BF16), kt_ref[cols, :]) * scale)
            att = _dot(p.astype(BF16), v_ref[:, cols])
            o_ref[:, cols] = (att * _silu(zg_ref[:, cols])).astype(BF16)

    return pl.pallas_call(
        body, name="branch_c_fwd", grid=(S // ts,),
        in_specs=[_zcol(ts, D, 6), _zcol(ts, D, 7),
                  pl.BlockSpec((D, M), lambda i: (0, 0)), pl.BlockSpec((M, D), lambda i: (0, 1))],
        out_specs=_tok(ts, D),
        out_shape=jax.ShapeDtypeStruct((S, D), BF16),
        compiler_params=_params("parallel"),
    )(z, z, kvt, kv)


def _merge_fwd(brc, z, x, c0, cg, cb, lng, lnb, t_mix, bias_full, wb_all, wo_all, l):
    S, D = x.shape
    GD = D // GROUPS
    ts = _tile(S, 256)
    CB = 256

    def body(c_ref, zm0_ref, zm1_ref, zm2_ref, x_ref, c0_ref, zbg_ref, cg_ref, cb_ref, zu_ref, zv_ref, zag_ref,
             lng_ref, lnb_ref, t_ref, bias_ref, wb_ref, wo_ref, a_ref, b_ref, m_ref, xo_ref, v_scr):
        for s in range(ts // 16):
            r = pl.ds(s * 16, 16)
            v = c0_ref[r, :]
            xc = v - _mean(v)
            c1 = (xc * lax.rsqrt(_mean(xc * xc) + LN_EPS)) * cg_ref[...] + cb_ref[...]
            b_ref[r, :] = (_silu(c1) * _silu(zbg_ref[r, :])).astype(BF16)
        for ch in range(ts // CHUNK):
            for s in range(CHUNK // 16):
                vg = _gelu(zv_ref[pl.ds(ch * CHUNK + s * 16, 16), :])
                xc = vg - _mean(vg)
                y = xc * lax.rsqrt(_mean(xc * xc) + LN_EPS)
                v_scr[pl.ds(ch * CHUNK + s * 16, 16), :] = (y * lng_ref[...] + lnb_ref[...]).astype(BF16)
            rows = pl.ds(ch * CHUNK, CHUNK)
            for g in range(GROUPS):
                cols = slice(g * GD, (g + 1) * GD)
                sv = _dot(t_ref[g], v_scr[rows, cols]) + bias_ref[:, cols]
                u = _gelu(zu_ref[rows, cols])
                a_ref[rows, cols] = ((u * sv) * _silu(zag_ref[rows, cols])).astype(BF16)

        brs = (a_ref, b_ref, c_ref)
        zms = (zm0_ref, zm1_ref, zm2_ref)
        for cb in range(D // CB):
            cols = slice(cb * CB, (cb + 1) * CB)
            acc = None
            for n in range(N_BRANCHES):
                t = _sigmoid(zms[n][:, cols]) * _dot(brs[n][...], wb_ref[n, :, cols])
                acc = t if acc is None else acc + t
            m_ref[:, cols] = acc.astype(BF16)
        for cb in range(D // CB):
            cols = slice(cb * CB, (cb + 1) * CB)
            xo_ref[:, cols] = x_ref[:, cols] + _dot(m_ref[...], wo_ref[:, cols])

    return pl.pallas_call(
        body, name="merge_fwd", grid=(S // ts,),
        in_specs=[_tok(ts, D), _zcol(ts, D, 8), _zcol(ts, D, 9), _zcol(ts, D, 10), _tok(ts, D),
                  _tok(ts, D), _zcol(ts, D, 5), _row1(D), _row1(D),
                  _zcol(ts, D, 0), _zcol(ts, D, 1), _zcol(ts, D, 2), _row1(D), _row1(D),
                  pl.BlockSpec((GROUPS, CHUNK, CHUNK), lambda i: (0, 0, 0)),
                  pl.BlockSpec((CHUNK, D), lambda i: (0, 0)),
                  pl.BlockSpec((None, N_BRANCHES, D, D), lambda i: (l, 0, 0, 0)),
                  pl.BlockSpec((None, D, D), lambda i: (l, 0, 0))],
        out_specs=[_tok(ts, D), _tok(ts, D), _tok(ts, D), _tok(ts, D)],
        out_shape=[jax.ShapeDtypeStruct((S, D), BF16), jax.ShapeDtypeStruct((S, D), BF16),
                   jax.ShapeDtypeStruct((S, D), BF16), jax.ShapeDtypeStruct((S, D), F32)],
        scratch_shapes=[pltpu.VMEM((ts, D), BF16)],
        compiler_params=_params("parallel"),
    )(brc, z, z, z, x, c0, z, cg, cb, z, z, z, lng, lnb, t_mix, bias_full, wb_all, wo_all)


def _loss_bwd(x, target, fg):
    S, D = x.shape
    ts = _tile(S, 512)

    def body(x_ref, t_ref, g_ref, dx_ref, ls_ref, dg_ref):
        @pl.when(pl.program_id(0) == 0)
        def _():
            ls_ref[...] = jnp.zeros_like(ls_ref)
            dg_ref[...] = jnp.zeros_like(dg_ref)

        def slab(si, c):
            ls, dg = c
            r = _rows(si, 16)
            xs = x_ref[r, :]
            rs = lax.rsqrt(_mean(xs * xs) + RMS_EPS)
            n = xs * rs
            e = n * g_ref[...] - t_ref[r, :]
            dy = e * (1.0 / D)
            dn = dy * g_ref[...]
            dx_ref[r, :] = rs * (dn - n * _mean(dn * n))
            return ls + _colsum(e * e), dg + _colsum(dy * n)
        zero = jnp.zeros((1, D), F32)
        ls, dg = _grouped_loop(ts // 16, 4, slab, (zero, zero))
        ls_ref[...] += ls
        dg_ref[...] += dg

    return pl.pallas_call(
        body, name="loss_bwd", grid=(S // ts,),
        in_specs=[_tok(ts, D), _tok(ts, D), _row1(D)],
        out_specs=[_tok(ts, D), _row1(D), _row1(D)],
        out_shape=[jax.ShapeDtypeStruct((S, D), F32), jax.ShapeDtypeStruct((1, D), F32),
                   jax.ShapeDtypeStruct((1, D), F32)],
        compiler_params=_params("arbitrary"),
    )(x, target, fg)


def _merge_bwd(dxp, bra, brb, brc, z, c0, lng, lnb, wb_all, wo_all, l, comm=None):
    S, D = dxp.shape
    ts = _tile(S, 256)
    CB = 256

    def body(d_ref, a_ref, b_ref, c_ref, zm0_ref, zm1_ref, zm2_ref, c0_ref, zg_ref, g_ref, be_ref, wb_ref, wo_ref,
             dzm_ref, dproj_ref, dbr_ref, dc0_ref, dzb_ref, dg_ref, db_ref, dcb_ref, dxb_scr, dbb_scr):
        @pl.when(pl.program_id(0) == 0)
        def _():
            dg_ref[...] = jnp.zeros_like(dg_ref)
            db_ref[...] = jnp.zeros_like(db_ref)
            dcb_ref[...] = jnp.zeros_like(dcb_ref)

        brs = (a_ref, b_ref, c_ref)
        zms = (zm0_ref, zm1_ref, zm2_ref)
        dxb_scr[...] = d_ref[...].astype(BF16)
        for cb in range(D // CB):
            cols = slice(cb * CB, (cb + 1) * CB)
            dm = _dot_tb(dxb_scr[...], wo_ref[cols, :])
            for n in range(N_BRANCHES):
                proj = _dot(brs[n][...], wb_ref[n, :, cols])
                g = _sigmoid(zms[n][:, cols])
                dzm_ref[:, n * D + cb * CB:n * D + (cb + 1) * CB] = ((dm * proj) * (g * (1.0 - g))).astype(BF16)
                dproj_ref[n, :, cols] = (dm * g).astype(BF16)
        for cb in range(D // CB):
            cols = slice(cb * CB, (cb + 1) * CB)
            dbb_scr[:, cols] = _dot_tb(dproj_ref[1], wb_ref[1, cols, :])

        dg = db = dcb = jnp.zeros((1, D), F32)
        for s in range(ts // 16):
            r = pl.ds(s * 16, 16)
            v = c0_ref[r, :]
            xc = v - _mean(v)
            rstd = lax.rsqrt(_mean(xc * xc) + LN_EPS)
            cn = xc * rstd
            c1 = cn * g_ref[...] + be_ref[...]
            c2, dc2_dc1 = _silu_and_grad(c1)
            sg, dsg = _silu_and_grad(zg_ref[r, :])
            d = dbb_scr[r, :]
            dzb_ref[r, :] = ((d * c2) * dsg).astype(BF16)
            dc1 = (d * sg) * dc2_dc1
            dcn = dc1 * g_ref[...]
            dc0 = rstd * ((dcn - _mean(dcn)) - cn * _mean(dcn * cn))
            dc0_ref[r, :] = dc0
            dg, db, dcb = dg + _colsum(dc1 * cn), db + _colsum(dc1), dcb + _colsum(dc0)
        dg_ref[...] += dg
        db_ref[...] += db
        dcb_ref[...] += dcb

        for plane, n in enumerate((0, 2)):
            for cb in range(D // CB):
                cols = slice(cb * CB, (cb + 1) * CB)
                dbr_ref[plane, :, cols] = _dot_tb(dproj_ref[n], wb_ref[n, cols, :])

    tokn = lambda n: pl.BlockSpec((n, ts, D), lambda i: (0, i, 0))
    grid = (S // ts,)
    body, c_in, c_out, c_shapes, c_scr, _ = _host(comm, 13, 8, 2, grid, body)
    outs = pl.pallas_call(
        body, name="merge_bwd", grid=grid,
        in_specs=[_tok(ts, D), _tok(ts, D), _tok(ts, D), _tok(ts, D),
                  _zcol(ts, D, 8), _zcol(ts, D, 9), _zcol(ts, D, 10), _tok(ts, D), _zcol(ts, D, 5), _row1(D), _row1(D),
                  pl.BlockSpec((None, N_BRANCHES, D, D), lambda i: (l, 0, 0, 0)),
                  pl.BlockSpec((None, D, D), lambda i: (l, 0, 0))] + c_in,
        out_specs=[_tok(ts, 3 * D), tokn(N_BRANCHES), tokn(2), _tok(ts, D), _zcol(ts, D, 2),
                   _row1(D), _row1(D), _row1(D)] + c_out,
        out_shape=[jax.ShapeDtypeStruct((S, 3 * D), BF16), jax.ShapeDtypeStruct((N_BRANCHES, S, D), BF16),
                   jax.ShapeDtypeStruct((2, S, D), F32), jax.ShapeDtypeStruct((S, D), F32),
                   jax.ShapeDtypeStruct((S, 3 * D), BF16), jax.ShapeDtypeStruct((1, D), F32),
                   jax.ShapeDtypeStruct((1, D), F32), jax.ShapeDtypeStruct((1, D), F32)] + c_shapes,
        scratch_shapes=[pltpu.VMEM((ts, D), BF16), pltpu.VMEM((ts, D), F32)] + c_scr,
        compiler_params=_params("arbitrary"),
    )(dxp, bra, brb, brc, z, z, z, c0, z, lng, lnb, wb_all, wo_all, *(comm.ins if comm else []))
    return (*outs[:8], outs[8:])


def _branch_a_bwd(z, dbr, lng, lnb, t_mix, t_mix_t, bias_full, mask):
    S = z.shape[0]
    D = lng.shape[1]
    GD = D // GROUPS
    ts = _tile(S, 256)
    nt = S // ts

    def body(zu_ref, zv_ref, zg_ref, d_ref, lng_ref, lnb_ref, t_ref, tt_ref, bias_ref, mask_ref,
             dz_ref, dws_ref, dbs_ref, dlg_ref, dlb_ref,
             v_scr, vn_scr, rstd_scr, dv_scr, dsv_acc, gg_scr):
        i = pl.program_id(0)

        @pl.when(i == 0)
        def _():
            dws_ref[...] = jnp.zeros_like(dws_ref)
            dlg_ref[...] = jnp.zeros_like(dlg_ref)
            dlb_ref[...] = jnp.zeros_like(dlb_ref)
            dsv_acc[...] = jnp.zeros_like(dsv_acc)

        def chunk(ci, carry):
            c0 = pl.multiple_of(ci * CHUNK, CHUNK)

            def ln_fwd(si, c):
                r = pl.multiple_of(si * 16, 16)
                vg, gg = _gelu_and_grad(zv_ref[pl.ds(pl.multiple_of(c0 + r, 16), 16), :])
                gg_scr[pl.ds(r, 16), :] = gg
                xc = vg - _mean(vg)
                rstd = lax.rsqrt(_mean(xc * xc) + LN_EPS)
                vn = xc * rstd
                vn_scr[pl.ds(r, 16), :] = vn
                rstd_scr[pl.ds(r, 16), :] = rstd
                v_scr[pl.ds(r, 16), :] = (vn * lng_ref[...] + lnb_ref[...]).astype(BF16)
                return c
            _grouped_loop(CHUNK // 16, 4, ln_fwd, 0)

            for g in range(GROUPS):
                cols = slice(g * GD, (g + 1) * GD)
                rows = pl.ds(c0, CHUNK)
                vb = v_scr[:, cols]
                sv = _dot(t_ref[g], vb) + bias_ref[:, cols]
                u, du = _gelu_and_grad(zu_ref[rows, cols])
                sg, dsg = _silu_and_grad(zg_ref[rows, cols])
                d = d_ref[rows, cols]
                dsv = (d * u) * sg
                dz_ref[rows, g * GD:(g + 1) * GD] = (((d * sv) * sg) * du).astype(BF16)
                dz_ref[rows, 2 * D + g * GD:2 * D + (g + 1) * GD] = (((d * u) * sv) * dsg).astype(BF16)
                dsvb = dsv.astype(BF16)
                dws_ref[g] += _dot_tb(dsvb, vb)
                dv_scr[:, cols] = _dot(tt_ref[g], dsvb)
                dsv_acc[:, cols] += dsv

            def ln_bwd(si, c):
                dlg, dlb = c
                r = pl.multiple_of(si * 16, 16)
                rr = pl.ds(r, 16)
                zrows = pl.ds(pl.multiple_of(c0 + r, 16), 16)
                dv = dv_scr[rr, :]
                vn = vn_scr[rr, :]
                dvn = dv * lng_ref[...]
                dvg = rstd_scr[rr, :] * ((dvn - _mean(dvn)) - vn * _mean(dvn * vn))
                dz_ref[zrows, D:2 * D] = (dvg * gg_scr[rr, :]).astype(BF16)
                return dlg + _colsum(dv * vn), dlb + _colsum(dv)
            zero = jnp.zeros((1, D), F32)
            dlg, dlb = _grouped_loop(CHUNK // 16, 4, ln_bwd, (zero, zero))
            dlg_ref[...] += dlg
            dlb_ref[...] += dlb
            return carry
        lax.fori_loop(0, ts // CHUNK, chunk, 0)

        @pl.when(i == nt - 1)
        def _():
            for g in range(GROUPS):
                cols = slice(g * GD, (g + 1) * GD)
                dws_ref[g] = dws_ref[g] * mask_ref[...]
                dbs_ref[g:g + 1, :] = _colsum(dsv_acc[:, cols].T)

    c2 = lambda shape: pl.BlockSpec(shape, lambda i: (0, 0))
    c3 = lambda: pl.BlockSpec((GROUPS, CHUNK, CHUNK), lambda i: (0, 0, 0))
    return pl.pallas_call(
        body, name="branch_a_bwd", grid=(nt,),
        in_specs=[_zcol(ts, D, 0), _zcol(ts, D, 1), _zcol(ts, D, 2), _plane(ts, D, 0), _row1(D), _row1(D),
                  c3(), c3(), c2((CHUNK, D)), c2((CHUNK, CHUNK))],
        out_specs=[_tok(ts, 3 * D), c3(), c2((GROUPS, CHUNK)), _row1(D), _row1(D)],
        out_shape=[jax.ShapeDtypeStruct((S, 3 * D), BF16), jax.ShapeDtypeStruct((GROUPS, CHUNK, CHUNK), F32),
                   jax.ShapeDtypeStruct((GROUPS, CHUNK), F32), jax.ShapeDtypeStruct((1, D), F32),
                   jax.ShapeDtypeStruct((1, D), F32)],
        scratch_shapes=[pltpu.VMEM((CHUNK, D), BF16), pltpu.VMEM((CHUNK, D), F32), pltpu.VMEM((CHUNK, 1), F32),
                        pltpu.VMEM((CHUNK, D), F32), pltpu.VMEM((CHUNK, D), F32), pltpu.VMEM((CHUNK, D), F32)],
        compiler_params=_params("arbitrary"),
    )(z, z, z, dbr, lng, lnb, t_mix, t_mix_t, bias_full, mask)


def _conv_bwd(dc0, z, w_all, dz_b, l, comm=None):
    S, D = dc0.shape
    ts = _tile(S, 256)
    nt = S // ts
    CB = 256
    hb = ts // HALO
    nh = S // HALO

    def body(dc_ref, dcn_ref, za_ref, zb_ref, w_ref, dzin_ref, dz_ref, dw_ref, dc_scr, dw_acc, dsh_scr):
        del dzin_ref
        i = pl.program_id(0)

        @pl.when(i == 0)
        def _():
            dw_acc[...] = jnp.zeros_like(dw_acc)

        dc_scr[ts:ts + HALO, :] = jnp.where(i == nt - 1, 0.0, dcn_ref[...])

        def fill(si, c):
            r = _rows(si, 32)
            dc_scr[r, :] = dc_ref[r, :]
            return c
        _grouped_loop(ts // 32, 2, fill, 0)

        def colblock(ci, c):
            cols = _rows(ci, CB)
            cols_b = pl.ds(pl.multiple_of(D + ci * CB, CB), CB)
            _fill_shifts(dsh_scr, dc_scr, cols, ts)
            for s in range(ts // 32):
                rows = pl.ds(s * 32, 32)
                za = za_ref[rows, cols]
                sg = _sigmoid(zb_ref[rows, cols])
                glu = za * sg
                acc = jnp.zeros((32, CB), F32)
                for k in range(CONV_K):
                    dcs = _shifted(dsh_scr, dc_scr, cols, s * 32 + CONV_K - 1 - k, 32)
                    prod = glu * dcs
                    dw_acc[k, :, cols] += (prod[0:8] + prod[8:16]) + (prod[16:24] + prod[24:32])
                    acc = acc + dcs * w_ref[k:k + 1, cols]
                dz_ref[rows, cols] = (acc * sg).astype(BF16)
                dz_ref[rows, cols_b] = ((acc * za) * (sg * (1.0 - sg))).astype(BF16)
            return c
        lax.fori_loop(0, D // CB, colblock, 0)

        @pl.when(i == nt - 1)
        def _():
            for k in range(CONV_K):
                dw_ref[k:k + 1, :] = _colsum(dw_acc[k])

    grid = (nt,)
    body, c_in, c_out, c_shapes, c_scr, _ = _host(comm, 6, 2, 3, grid, body)
    outs = pl.pallas_call(
        body, name="conv_bwd", grid=grid,
        in_specs=[_tok(ts, D), pl.BlockSpec((HALO, D), lambda i: (jnp.minimum((i + 1) * hb, nh - 1), 0)),
                  _zcol(ts, D, 3), _zcol(ts, D, 4),
                  pl.BlockSpec((None, CONV_K, D), lambda i: (l, 0, 0)), ANY] + c_in,
        out_specs=[_tok(ts, 2 * D), pl.BlockSpec((CONV_K, D), lambda i: (0, 0))] + c_out,
        out_shape=[jax.ShapeDtypeStruct((S, 3 * D), BF16), jax.ShapeDtypeStruct((CONV_K, D), F32)] + c_shapes,
        scratch_shapes=[pltpu.VMEM((ts + HALO, D), F32), pltpu.VMEM((CONV_K, 8, D), F32),
                        pltpu.VMEM((7, ts + HALO, CB), F32)] + c_scr,
        input_output_aliases={5: 0},
        compiler_params=_params("arbitrary"),
    )(dc0, dc0, z, z, w_all, dz_b, *(comm.ins if comm else []))
    return outs[0], outs[1], outs[2:]


def _branch_c_bwd(z, dbr, kv, kvt):
    S = z.shape[0]
    M, D2 = kv.shape
    D = D2 // 2
    HD = D // HEADS
    scale = 1.0 / math.sqrt(HD)
    ts = _tile(S, 256)

    def body(q_ref, zg_ref, d_ref, kt_ref, vt_ref, k_ref, v_ref, dz_ref, dk_ref, dv_ref):
        @pl.when(pl.program_id(0) == 0)
        def _():
            dk_ref[...] = jnp.zeros_like(dk_ref)
            dv_ref[...] = jnp.zeros_like(dv_ref)

        for h in range(HEADS):
            cols = slice(h * HD, (h + 1) * HD)
            qb = q_ref[:, cols].astype(BF16)
            p = _softmax_rows(_dot(qb, kt_ref[cols, :]) * scale)
            pb = p.astype(BF16)
            att = _dot(pb, v_ref[:, cols])
            sg, dsg = _silu_and_grad(zg_ref[:, cols])
            d = d_ref[:, cols]
            dz_ref[:, D + h * HD:D + (h + 1) * HD] = ((d * att) * dsg).astype(BF16)
            datt = (d * sg).astype(BF16)
            dp = _dot(datt, vt_ref[cols, :])
            dv_ref[:, cols] += _dot_ta(pb, datt)
            ds = ((p * (dp - jnp.sum(dp * p, axis=-1, keepdims=True))) * scale).astype(BF16)
            dz_ref[:, cols] = _dot(ds, k_ref[:, cols]).astype(BF16)
            dk_ref[:, cols] += _dot_ta(ds, qb)

    return pl.pallas_call(
        body, name="branch_c_bwd", grid=(S // ts,),
        in_specs=[_zcol(ts, D, 6), _zcol(ts, D, 7), _plane(ts, D, 1),
                  pl.BlockSpec((D, M), lambda i: (0, 0)), pl.BlockSpec((D, M), lambda i: (1, 0)),
                  pl.BlockSpec((M, D), lambda i: (0, 0)), pl.BlockSpec((M, D), lambda i: (0, 1))],
        out_specs=[_tok(ts, 2 * D), pl.BlockSpec((M, D), lambda i: (0, 0)), pl.BlockSpec((M, D), lambda i: (0, 0))],
        out_shape=[jax.ShapeDtypeStruct((S, 2 * D), BF16), jax.ShapeDtypeStruct((M, D), F32),
                   jax.ShapeDtypeStruct((M, D), F32)],
        compiler_params=_params("arbitrary"),
    )(z, z, dbr, kvt, kvt, kv, kv)


def _kv_bwd(dk, dv, mn, mem, wkv_all, l, prev):
    M, D = mem.shape

    def body(*refs):
        dk_ref, dv_ref, mn_ref, mem_ref, w_ref = refs[:5]
        dw_ref, dg_ref = refs[-2:]
        dkb = dk_ref[...].astype(BF16)
        dvb = dv_ref[...].astype(BF16)
        dw_ref[:, 0:D] = _dot_ta(mn_ref[...], dkb)
        dw_ref[:, D:2 * D] = _dot_ta(mn_ref[...], dvb)
        dmn = _dot_tb(dkb, w_ref[:, 0:D]) + _dot_tb(dvb, w_ref[:, D:2 * D])
        m = mem_ref[...]
        dg_ref[...] = _colsum(dmn * (m * lax.rsqrt(_mean(m * m) + RMS_EPS)))

    full = lambda shape: pl.BlockSpec(shape, lambda i: (0, 0))
    in_specs = [full((M, D)), full((M, D)), full((M, D)), full((M, D)),
                pl.BlockSpec((None, D, 2 * D), lambda i: (l, 0, 0))]
    args = [dk, dv, mn, mem, wkv_all]
    aliases = {}
    if prev is not None:
        in_specs.append(ANY)
        args.append(prev)
        aliases = {5: 0}
    return pl.pallas_call(
        body, name="kv_bwd", grid=(1,),
        in_specs=in_specs,
        out_specs=[pl.BlockSpec((None, D, 2 * D), lambda i: (l, 0, 0)), _row1(D)],
        out_shape=[jax.ShapeDtypeStruct((wkv_all.shape[0], D, 2 * D), F32), jax.ShapeDtypeStruct((1, D), F32)],
        input_output_aliases=aliases,
        compiler_params=_params("arbitrary"),
    )(*args)


def _in_proj_bwd(pieces, w_all, x, g, dxp, l, comm=None, tiles=None, prev=None):
    S, D = x.shape
    tm = _tile(S, 1024)
    n_p = len(pieces)
    i0, n_i = tiles or (0, S // tm)
    n_in = n_p + 4 + (prev is not None)

    def body(*refs):
        dz_refs = refs[:n_p]
        w_ref, x_hbm, g_ref, dxp_hbm = refs[n_p:n_p + 4]
        dx_ref, dg_ref, acc, x_ref, dxp_ref, sem = refs[n_in:]
        i = pl.program_id(0)
        k = pl.program_id(1)
        rows = pl.ds(pl.multiple_of((i + i0) * tm, tm), tm)
        fetch = [pltpu.make_async_copy(x_hbm.at[rows, :], x_ref, sem.at[0]),
                 pltpu.make_async_copy(dxp_hbm.at[rows, :], dxp_ref, sem.at[1])]

        @pl.when(k == N_PLANES - 2)
        def _():
            for cp in fetch:
                cp.start()

        @pl.when(k == 0)
        def _():
            acc[...] = jnp.zeros_like(acc)

        @pl.when((i == 0) & (k == 0))
        def _():
            dg_ref[...] = jnp.zeros_like(dg_ref)

        for (_, lo, n), r in zip(pieces, dz_refs):
            @pl.when((k >= lo) & (k < lo + n))
            def _(r=r):
                acc[...] += _dot_tb(r[...], w_ref[...])

        @pl.when(k == N_PLANES - 1)
        def _():
            for cp in fetch:
                cp.wait()

            def slab(si, c):
                rr = _rows(si, 16)
                xs = x_ref[rr, :]
                rs = lax.rsqrt(_mean(xs * xs) + RMS_EPS)
                n = xs * rs
                dh = acc[rr, :]
                dn = dh * g_ref[...]
                dx_ref[rr, :] = rs * (dn - n * _mean(dn * n)) + dxp_ref[rr, :]
                return c + _colsum(dh * n)
            dg_ref[...] += _grouped_loop(tm // 16, 4, slab, jnp.zeros((1, D), F32))

    def piece_spec(lo, n):
        return pl.BlockSpec((tm, D), lambda i, k: (i + i0, jnp.clip(k - lo, 0, n - 1)))

    tokk = lambda: pl.BlockSpec((tm, D), lambda i, k: (i + i0, 0))
    rowk = lambda: pl.BlockSpec((1, D), lambda i, k: (0, 0))
    grid = (n_i, N_PLANES)
    body, c_in, c_out, c_shapes, c_scr, _ = _host(comm, n_in, 2, 4, grid, body)
    more_in, more_args, aliases = ([ANY], [prev], {n_in - 1: 0}) if prev is not None else ([], [], {})
    outs = pl.pallas_call(
        body, name="in_proj_bwd", grid=grid,
        in_specs=[piece_spec(lo, n) for _, lo, n in pieces]
        + [pl.BlockSpec((None, D, D), lambda i, k: (l, 0, k)), ANY, rowk(), ANY] + more_in + c_in,
        out_specs=[tokk(), rowk()] + c_out,
        out_shape=[jax.ShapeDtypeStruct((S, D), F32), jax.ShapeDtypeStruct((1, D), F32)] + c_shapes,
        scratch_shapes=[pltpu.VMEM((tm, D), F32), pltpu.VMEM((tm, D), F32), pltpu.VMEM((tm, D), F32),
                        pltpu.SemaphoreType.DMA((2,))] + c_scr,
        input_output_aliases=aliases,
        compiler_params=_params("arbitrary", "arbitrary"),
    )(*[p for p, _, _ in pieces], w_all, x, g, dxp, *more_args, *(comm.ins if comm else []))
    return outs[0], outs[1], outs[2:]


def _wgrad(a, b, out_shape, plane, col_off, prev, a_plane=None, b_plane=None):
    S = a.shape[-2]
    K1 = a.shape[-1]
    nb_cols = b.shape[-1]
    ts = _tile(S, 2048)
    tn = _tile(nb_cols, 1024)
    col_off = col_off // tn

    def body(*refs):
        a_ref, b_ref = refs[:2]
        o_ref = refs[-1]
        k = pl.program_id(1)
        prod = _dot_ta(a_ref[...].astype(BF16), b_ref[...].astype(BF16))

        @pl.when(k == 0)
        def _():
            o_ref[...] = prod

        @pl.when(k > 0)
        def _():
            o_ref[...] += prod

    if a_plane is None:
        a_spec = pl.BlockSpec((ts, K1), lambda j, k: (k, 0))
    else:
        a_spec = pl.BlockSpec((None, ts, K1), lambda j, k: (a_plane, k, 0))
    if b_plane is None:
        b_spec = pl.BlockSpec((ts, tn), lambda j, k: (k, j))
    else:
        b_spec = pl.BlockSpec((None, ts, tn), lambda j, k: (b_plane, k, j))
    in_specs, args, aliases = [a_spec, b_spec], [a, b], {}
    if prev is not None:
        in_specs.append(ANY)
        args.append(prev)
        aliases = {2: 0}
    return pl.pallas_call(
        body, name="wgrad", grid=(nb_cols // tn, S // ts),
        in_specs=in_specs,
        out_specs=pl.BlockSpec((None, K1, tn), lambda j, k: (plane, 0, col_off + j)),
        out_shape=jax.ShapeDtypeStruct(out_shape, F32),
        input_output_aliases=aliases,
        compiler_params=_params("parallel", "arbitrary"),
    )(*args)


def _position():
    x, y, c = lax.axis_index("x"), lax.axis_index("y"), lax.axis_index("c")
    chips = [(1 - x, y), (x, 1 - y), (1 - x, 1 - y)]
    return x, y, c, chips


def _sub(ref, axis, start, size):
    idx = [slice(None)] * len(ref.shape)
    idx[axis] = pl.ds(start, size)
    return ref.at[tuple(idx)]


def _dma(src, dst, ssem, rsem, to):
    return pltpu.make_async_remote_copy(src_ref=src, dst_ref=dst, send_sem=ssem, recv_sem=rsem,
                                        device_id=to, device_id_type=MESH)


def _dma_sems(*counts):
    return [pltpu.SemaphoreType.DMA((n,)) for n in counts]


def _both(a, b):
    na, oa, sa = len(a.ins), len(a.out_shapes), len(a.sem_shapes)
    phases = [(w, lambda i, o, s, fn=fn: fn(i[:na], o[:oa], s[:sa])) for w, fn in a.phases]
    phases += [(w, lambda i, o, s, fn=fn: fn(i[na:], o[oa:], s[sa:])) for w, fn in b.phases]
    return _Hosted(a.ins + b.ins, list(a.out_shapes) + list(b.out_shapes), list(a.sem_shapes) + list(b.sem_shapes), phases)


def _run_alone(comm, name):
    ci, co = len(comm.ins), len(comm.out_shapes)

    def body(*refs):
        for _, fn in comm.phases:
            fn(refs[:ci], refs[ci:ci + co], refs[ci + co:])

    return pl.pallas_call(
        body, name=name, in_specs=[ANY] * ci, out_specs=[ANY] * co, out_shape=list(comm.out_shapes),
        scratch_shapes=list(comm.sem_shapes),
    )(*comm.ins)


def _gather_comm(shards, layers, axes):
    n = len(shards)
    out_shapes = []
    for s, (ba, _) in zip(shards, axes):
        shp = list(s.shape[1:])
        shp[ba] *= N_CHIPS
        out_shapes.append(jax.ShapeDtypeStruct(tuple(shp), s.dtype))

    def half(ref, t, h):
        ha = axes[t][1]
        if ha is None:
            return ref
        hs = shards[t].shape[1 + ha] // 2
        return _sub(ref, ha, h * hs, hs)

    def block(outs, t, blk):
        ba = axes[t][0]
        bs = shards[t].shape[1 + ba]
        return _sub(outs[t], ba, blk * bs, bs)

    def start(ins, outs, sems):
        s_own, r_own, s_ici, r_ici, _, _ = sems
        x, y, c, chips = _position()
        j = 2 * x + y
        for t in range(n):
            _dma(ins[t].at[layers[t]], block(outs, t, j), s_own.at[t], r_own.at[t], (x, y, 1 - c)).start()
            for k, (px, py) in enumerate(chips):
                _dma(half(ins[t].at[layers[t]], t, c), half(block(outs, t, j), t, c),
                     s_ici.at[3 * t + k], r_ici.at[3 * t + k], (px, py, c)).start()

    def forward(ins, outs, sems):
        _, _, s_ici, r_ici, s_fwd, r_fwd = sems
        x, y, c, chips = _position()
        for t in range(n):
            if axes[t][1] is None:
                continue
            for k, (px, py) in enumerate(chips):
                got = half(block(outs, t, 2 * px + py), t, c)
                _dma(got, got, s_ici.at[3 * t + k], r_ici.at[3 * t + k], (px, py, c)).wait_recv()
                _dma(got, got, s_fwd.at[3 * t + k], r_fwd.at[3 * t + k], (x, y, 1 - c)).start()

    def finish(ins, outs, sems):
        s_own, r_own, s_ici, r_ici, s_fwd, r_fwd = sems
        x, y, c, chips = _position()
        j = 2 * x + y
        sib = (x, y, 1 - c)
        for t in range(n):
            for k, (px, py) in enumerate(chips):
                i = 3 * t + k
                sent = half(block(outs, t, j), t, c)
                _dma(sent, sent, s_ici.at[i], r_ici.at[i], (px, py, c)).wait_send()
                theirs = block(outs, t, 2 * px + py)
                if axes[t][1] is None:
                    _dma(theirs, theirs, s_ici.at[i], r_ici.at[i], (px, py, c)).wait_recv()
                else:
                    got, other = half(theirs, t, c), half(theirs, t, 1 - c)
                    _dma(got, got, s_fwd.at[i], r_fwd.at[i], sib).wait_send()
                    _dma(other, other, s_fwd.at[i], r_fwd.at[i], sib).wait_recv()
            _dma(ins[t].at[layers[t]], block(outs, t, j), s_own.at[t], r_own.at[t], sib).wait()

    return _Hosted(list(shards), out_shapes, _dma_sems(n, n, 3 * n, 3 * n, 3 * n, 3 * n),
                   [("first", start), ("late", forward), ("last", finish)])


def _halved(shape, axis):
    shp = list(shape)
    shp[axis] //= 2
    return tuple(shp)


def _pair_comm(grads, axes):
    n = len(grads)

    def copies(ins, outs, sems):
        x, y, c, _ = _position()
        cps = []
        for t in range(n):
            ha = axes[t][1]
            hs = grads[t].shape[ha] // 2
            cps.append(_dma(_sub(ins[t], ha, (1 - c) * hs, hs), outs[t], sems[0].at[t], sems[1].at[t], (x, y, 1 - c)))
        return cps

    def start(ins, outs, sems):
        for cp in copies(ins, outs, sems):
            cp.start()

    def finish(ins, outs, sems):
        for cp in copies(ins, outs, sems):
            cp.wait()

    return _Hosted(list(grads), [jax.ShapeDtypeStruct(_halved(g.shape, ax[1]), g.dtype) for g, ax in zip(grads, axes)],
                   _dma_sems(n, n), [("first", start), ("last", finish)])


def _block_comm(sums, axes):
    n = len(sums)
    out_shapes = []
    for s, (ba, _) in zip(sums, axes):
        shp = list(s.shape)
        shp[ba] //= N_CHIPS
        out_shapes.append(jax.ShapeDtypeStruct((3,) + tuple(shp), s.dtype))

    def copies(ins, outs, sems):
        x, y, c, chips = _position()
        cps = []
        for t in range(n):
            ba = axes[t][0]
            bs = sums[t].shape[ba] // N_CHIPS
            for k, (px, py) in enumerate(chips):
                cps.append(_dma(_sub(ins[t], ba, (2 * px + py) * bs, bs), outs[t].at[k],
                                sems[0].at[3 * t + k], sems[1].at[3 * t + k], (px, py, c)))
        return cps

    def start(ins, outs, sems):
        for cp in copies(ins, outs, sems):
            cp.start()

    def finish(ins, outs, sems):
        for cp in copies(ins, outs, sems):
            cp.wait()

    return _Hosted(list(sums), out_shapes, _dma_sems(3 * n, 3 * n), [("first", start), ("last", finish)])


def _sibling_share(shards, axes, layer):
    n = len(shards)

    def body(*refs):
        outs = refs[n:2 * n]
        ssem, rsem = refs[2 * n:]
        x, y, c, _ = _position()
        cps = []
        for t in range(n):
            ha = axes[t][1]
            hs = shards[t].shape[1 + ha] // 2
            mine = _sub(outs[t].at[layer], ha, c * hs, hs)
            cps.append(_dma(mine, mine, ssem.at[t], rsem.at[t], (x, y, 1 - c)))
        for cp in cps:
            cp.start()
        for cp in cps:
            cp.wait()

    return pl.pallas_call(
        body, name="sibling_share",
        in_specs=[ANY] * n, out_specs=[ANY] * n,
        out_shape=[jax.ShapeDtypeStruct(s.shape, s.dtype) for s in shards],
        input_output_aliases={t: t for t in range(n)},
        scratch_shapes=_dma_sems(n, n),
    )(*shards)


def _small_allreduce(p):
    R, C = p.shape
    Rh = R // 2
    assert Rh % 8 == 0

    def body(p_ref, o_ref, sib_buf, chip_sums, ssem, rsem):
        x, y, c, chips = _position()
        j = 2 * x + y
        sib_id = (x, y, 1 - c)
        sib = _dma(p_ref, sib_buf, ssem.at[0], rsem.at[0], sib_id)
        sib.start()
        sib.wait()
        chip_sums[j] = p_ref[...] + sib_buf[...]

        def half(blk, h):
            return chip_sums.at[blk, pl.ds(pl.multiple_of(h * Rh, 8), Rh), :]

        sent = [_dma(half(j, c), half(j, c), ssem.at[1 + k], rsem.at[1 + k], (px, py, c))
                for k, (px, py) in enumerate(chips)]
        for cp in sent:
            cp.start()
        passed = []
        for k, (px, py) in enumerate(chips):
            got = half(2 * px + py, c)
            _dma(got, got, ssem.at[1 + k], rsem.at[1 + k], (px, py, c)).wait_recv()
            cp = _dma(got, got, ssem.at[4 + k], rsem.at[4 + k], sib_id)
            cp.start()
            passed.append(cp)
        for k, (px, py) in enumerate(chips):
            other = half(2 * px + py, 1 - c)
            _dma(other, other, ssem.at[4 + k], rsem.at[4 + k], sib_id).wait_recv()
        for cp in sent + passed:
            cp.wait_send()
        o_ref[...] = ((chip_sums[0] + chip_sums[1]) + chip_sums[2]) + chip_sums[3]

    vm = pl.BlockSpec(memory_space=pltpu.VMEM)
    return pl.pallas_call(
        body, name="small_allreduce",
        in_specs=[vm], out_specs=vm, out_shape=jax.ShapeDtypeStruct((R, C), F32),
        scratch_shapes=[pltpu.VMEM((R, C), F32), pltpu.VMEM((N_CHIPS, R, C), F32)] + _dma_sems(7, 7),
        compiler_params=pltpu.CompilerParams(vmem_limit_bytes=VMEM_LIMIT_BYTES),
    )(p)


def _add_half(g, recv, half_axis, pos):
    B, R, C = recv.shape
    tr = _tile(R, 256)
    tc = _tile(C, 2816)
    nr, nc = R // tr, C // tc
    if half_axis == 1:
        g_map = lambda b, i, j, pos_ref: (b, pos_ref[0] * nr + i, j)
    else:
        g_map = lambda b, i, j, pos_ref: (b, i, pos_ref[0] * nc + j)

    def body(pos_ref, g_ref, r_ref, o_ref, ob_ref):
        del pos_ref
        s = g_ref[...] + r_ref[...]
        o_ref[...] = s
        ob_ref[...] = s.astype(BF16)

    spec = lambda: pl.BlockSpec((None, tr, tc), lambda b, i, j, pos_ref: (b, i, j))
    return pl.pallas_call(
        body, name="add_half",
        grid_spec=pltpu.PrefetchScalarGridSpec(
            num_scalar_prefetch=1, grid=(B, nr, nc),
            in_specs=[pl.BlockSpec((None, tr, tc), g_map), spec()],
            out_specs=[spec(), spec()]),
        out_shape=[jax.ShapeDtypeStruct((B, R, C), F32), jax.ShapeDtypeStruct((B, R, C), BF16)],
        compiler_params=_params("parallel", "parallel", "parallel"),
    )(pos, g, recv)


def _sum_half(own, recv, axes, pos, layer, prev):
    _, B, rs, cs = recv.shape
    ba, ha = axes
    tr = _tile(rs, 128)
    nr = rs // tr

    def body(pos_ref, o_ref, r0_ref, r1_ref, r2_ref, *rest):
        out_ref = rest[-1]
        out_ref[...] = (((o_ref[...] + r0_ref[...].astype(F32)) + r1_ref[...].astype(F32))
                        + r2_ref[...].astype(F32))

    if ba == 2:
        own_spec = pl.BlockSpec((None, tr, cs), lambda b, i, pos_ref: (b, i, pos_ref[1]))
    else:
        own_spec = pl.BlockSpec((None, tr, cs), lambda b, i, pos_ref: (b, pos_ref[1] * nr + i, 0))
    if ha == 1:
        out_shape, out_map = (B, 2 * rs, cs), (lambda b, i, pos_ref: (layer, b, pos_ref[0] * nr + i, 0))
    else:
        out_shape, out_map = (B, rs, 2 * cs), (lambda b, i, pos_ref: (layer, b, i, pos_ref[0]))
    rspec = lambda k: pl.BlockSpec((None, None, tr, cs), lambda b, i, pos_ref: (k, b, i, 0))
    more_in, more_args, aliases = ([ANY], [prev], {5: 0}) if prev is not None else ([], [], {})
    return pl.pallas_call(
        body, name="sum_half",
        grid_spec=pltpu.PrefetchScalarGridSpec(
            num_scalar_prefetch=1, grid=(B, nr),
            in_specs=[own_spec, rspec(0), rspec(1), rspec(2)] + more_in,
            out_specs=pl.BlockSpec((None, None, tr, cs), out_map)),
        out_shape=jax.ShapeDtypeStruct((N_LAYERS,) + out_shape, F32),
        input_output_aliases=aliases,
        compiler_params=_params("parallel", "parallel"),
    )(pos, own, recv, recv, recv, *more_args)


def _adamw(w, g, m, v):
    shape = w.shape
    C = shape[-1]
    R = w.size // C
    tr = R
    for cand in (512, 256, 128, 64, 32, 16, 8):
        if R % cand == 0 and cand * C * 4 <= (1 << 20):
            tr = cand
            break
    bc1 = 1.0 - ADAM_B1 ** ADAM_STEP
    bc2 = 1.0 - ADAM_B2 ** ADAM_STEP

    def body(w_ref, g_ref, m_ref, v_ref, d_ref, mo_ref, vo_ref, go_ref):
        gg = g_ref[...]
        go_ref[...] = gg
        mn = ADAM_B1 * m_ref[...] + (1.0 - ADAM_B1) * gg
        vn = ADAM_B2 * v_ref[...] + (1.0 - ADAM_B2) * (gg * gg)
        mo_ref[...] = mn
        vo_ref[...] = vn
        d_ref[...] = -ADAM_LR * ((mn / bc1) / (jnp.sqrt(vn / bc2) + ADAM_EPS) + ADAM_WD * w_ref[...])

    spec = lambda: pl.BlockSpec((tr, C), lambda i: (i, 0))
    outs = pl.pallas_call(
        body, name="adamw", grid=(R // tr,),
        in_specs=[spec()] * 4, out_specs=[spec()] * 4,
        out_shape=[jax.ShapeDtypeStruct((R, C), F32)] * 4,
        compiler_params=_params("parallel"),
    )(*[a.reshape(R, C) for a in (w, g, m, v)])
    return tuple(o.reshape(shape) for o in outs)


def kernel(x, mem, norm_g, mem_norm_g, w_in, gmlp_ln_g, gmlp_ln_b, w_s, b_s, conv_w, conv_b, conv_ln_g, conv_ln_b, w_kv, w_branch, w_out, final_norm_g, loss_target, m_norm_g, m_mem_norm_g, m_w_in, m_gmlp_ln_g, m_gmlp_ln_b, m_w_s, m_b_s, m_conv_w, m_conv_b, m_conv_ln_g, m_conv_ln_b, m_w_kv, m_w_branch, m_w_out, m_final_norm_g, v_norm_g, v_mem_norm_g, v_w_in, v_gmlp_ln_g, v_gmlp_ln_b, v_w_s, v_b_s, v_conv_w, v_conv_b, v_conv_ln_g, v_conv_ln_b, v_w_kv, v_w_branch, v_w_out, v_final_norm_g):
    xs, mems, tgt = x[0], mem[0], loss_target[0]
    S, D = xs.shape
    assert D // GROUPS == LANES and S % CHUNK == 0 and w_s.shape[-1] == CHUNK
    L = N_LAYERS

    assert L == 2
    shards = [w_in.astype(BF16)[:, None], w_kv.astype(BF16)[:, None], w_branch.astype(BF16), w_out.astype(BF16)[:, None],
              conv_w[:, None]]
    gather_axes = [(2, 1), (2, 1), (1, 2), (1, 2), (2, None)]
    reduce_axes = gather_axes[:4]
    win0, cw0 = _run_alone(_gather_comm([shards[0], shards[4]], [0, 0], [gather_axes[0], gather_axes[4]]), "gather_first")
    gather_rest = _gather_comm(shards[1:4] + shards, [0] * 3 + [1] * 5, gather_axes[1:4] + gather_axes)

    tri = jnp.tril(jnp.ones((CHUNK, CHUNK), F32))
    t_mix = (w_s * tri).astype(BF16)
    t_mix_t = jnp.swapaxes(t_mix, -1, -2)
    bias_full = jnp.repeat(jnp.swapaxes(b_s, -1, -2), D // GROUPS, axis=-1)
    row = lambda a, l: a[l][None, :]
    xi, yi, ci = lax.axis_index("x"), lax.axis_index("y"), lax.axis_index("c")
    pos = jnp.stack([ci, 2 * xi + yi]).astype(jnp.int32)

    saved = []
    h = xs
    weights = [None] * L
    for l in range(L):
        if l == 0:
            z, hb, c0, fetched = _in_proj_conv(h, row(norm_g, l), win0, cw0, row(conv_b, l), 0, gather_rest)
            weights = [[win0] + list(fetched[:3]) + [cw0], list(fetched[3:])]
        else:
            z, hb, c0, _ = _in_proj_conv(h, row(norm_g, l), weights[l][0], weights[l][4], row(conv_b, l), 0)
        win, wkv, wb, wo, cw = weights[l]
        mn, kv, kvt = _kv_fwd(mems, row(mem_norm_g, l), wkv, 0)
        brc = _branch_c_fwd(z, kv, kvt)
        bra, brb, merged, h_next = _merge_fwd(brc, z, h, c0, row(conv_ln_g, l), row(conv_ln_b, l), row(gmlp_ln_g, l),
                                              row(gmlp_ln_b, l), t_mix[l], bias_full[l], wb[None], wo, 0)
        saved.append((h, z, hb, bra, c0, brb, mn, kv, kvt, brc, merged))
        h = h_next

    dx, loss_cols, d_final_g = _loss_bwd(h, tgt, final_norm_g[None, :])
    loss = lax.psum(0.5 * jnp.sum(loss_cols) / D, ("x", "y", "c"))

    def add_halves(grads, recv, axes):
        return [_add_half(g, r, ax[1], pos) for g, r, ax in zip(grads, recv, axes)]

    def finish_reduction(sums, blocks, axes, layer, prevs):
        halves = [_sum_half(s, b, ax, pos, layer, p) for (s, _), b, ax, p in zip(sums, blocks, axes, prevs)]
        return list(_sibling_share(halves, axes, layer))

    small = {k: [None] * L for k in ("norm_g", "mem_norm_g", "gmlp_ln_g", "gmlp_ln_b", "w_s", "b_s", "conv_w",
                                     "conv_b", "conv_ln_g", "conv_ln_b")}
    early_axes = reduce_axes[1:]
    pending, stacked = None, None
    for l in reversed(range(L)):
        h_in, z, hb, bra, c0, brb, mn, kv, kvt, brc, merged = saved[l]
        win, wkv, wb, wo, cw = weights[l]
        dz_m, dproj, dbr, dc0, dz_b, dcg, dcbeta, dcb, recv = _merge_bwd(
            dx, bra, brb, brc, z, c0, row(conv_ln_g, l), row(conv_ln_b, l), wb[None], wo, 0,
            _pair_comm(pending, reduce_axes) if pending else None)
        sums = add_halves(pending, recv, reduce_axes) if pending else []
        g_o = _wgrad(merged, dx, (1, D, D), 0, 0, None)
        g_b = None
        for n, br in enumerate((bra, brb, brc)):
            g_b = _wgrad(br, dproj, (N_BRANCHES, D, D), n, 0, g_b, b_plane=n)
        dz_a, dws, dbs, dlg, dlb = _branch_a_bwd(z, dbr, row(gmlp_ln_g, l), row(gmlp_ln_b, l),
                                                t_mix[l], t_mix_t[l], bias_full[l], tri)
        dz_c, dk, dv = _branch_c_bwd(z, dbr, kv, kvt)
        g_kv, dmg = _kv_bwd(dk, dv, mn, mems, wkv, 0, None)
        ready = [g_kv, g_b, g_o]
        riding = _both(_block_comm([sb for _, sb in sums], reduce_axes), _pair_comm(ready, early_axes)) if pending else None
        dz_b, dcw, got = _conv_bwd(dc0, z, cw, dz_b, 0, riding)
        sums_early = add_halves(ready, got[4:], early_axes) if pending else []
        if pending:
            stacked = finish_reduction(sums, got[:4], reduce_axes, l + 1, [None] * 4)
        pieces =[(dz_a, 0, 3), (dz_b, 3, 3), (dz_c, 6, 2), (dz_m, 8, 3)]
        g_in = None
        for dzp, lo, _ in pieces:
            g_in = _wgrad(hb, dzp, (1, D, N_PLANES * D), 0, lo * D, g_in)
        if pending:
            n_t = S // _tile(S, 1024)
            n_a, n_c = max(1, (3 * n_t) // 8), (1 if n_t >= 3 else 0)
            riding = _block_comm([sb for _, sb in sums_early], early_axes)
            dx_a, dng_a, got = _in_proj_bwd(pieces, win, h_in, row(norm_g, l), dx, 0,
                                            _both(riding, _pair_comm([g_in], reduce_axes[:1])), (0, n_a))
            stacked[1:] = finish_reduction(sums_early, got[:3], early_axes, l, stacked[1:])
            sums_in = add_halves([g_in], got[3:], reduce_axes[:1])
            dx_b, dng_b, blocks_in = _in_proj_bwd(pieces, win, h_in, row(norm_g, l), dx, 0,
                                                  _block_comm([sb for _, sb in sums_in], reduce_axes[:1]),
                                                  (n_a, n_t - n_a - n_c), dx_a)
            dng = dng_a + dng_b
            if n_c:
                dx_b, dng_c, _ = _in_proj_bwd(pieces, win, h_in, row(norm_g, l), dx, 0, None, (n_t - n_c, n_c), dx_b)
                dng = dng + dng_c
            dx = dx_b
            stacked[:1] = finish_reduction(sums_in, blocks_in, reduce_axes[:1], l, stacked[:1])
        else:
            dx, dng, _ = _in_proj_bwd(pieces, win, h_in, row(norm_g, l), dx, 0)
        pending = [g_in, g_kv, g_b, g_o]
        for k, val in (("norm_g", dng), ("mem_norm_g", dmg), ("gmlp_ln_g", dlg), ("gmlp_ln_b", dlb), ("w_s", dws),
                       ("b_s", dbs), ("conv_w", dcw), ("conv_b", dcb), ("conv_ln_g", dcg), ("conv_ln_b", dcbeta)):
            small[k][l] = val
    grad_x = dx[None]

    grad_w_in, grad_w_kv, grad_w_branch, grad_w_out = [
        g.reshape(w.shape) for g, w in zip(stacked, (w_in, w_kv, w_branch, w_out))]

    order = [("norm_g", norm_g.shape), ("mem_norm_g", mem_norm_g.shape), ("gmlp_ln_g", gmlp_ln_g.shape),
             ("gmlp_ln_b", gmlp_ln_b.shape), ("w_s", w_s.shape), ("b_s", b_s.shape),
             ("conv_w", (L, CONV_K, D)), ("conv_b", conv_b.shape), ("conv_ln_g", conv_ln_g.shape),
             ("conv_ln_b", conv_ln_b.shape)]
    parts = [jnp.stack([v.reshape(shp[1:]) for v in small[k]]).reshape(-1, LANES) for k, shp in order]
    parts.append(d_final_g.reshape(-1, LANES))
    sizes = [p.shape[0] for p in parts]
    assert all(s % 8 == 0 for s in sizes)
    if sum(sizes) % 16:
        parts.append(jnp.zeros((16 - sum(sizes) % 16, LANES), F32))
    total = _small_allreduce(jnp.concatenate(parts, axis=0))
    red, off = {}, 0
    for (k, shp), n in zip(order + [("final_norm_g", final_norm_g.shape)], sizes):
        red[k] = total[off:off + n].reshape(shp)
        off += n
    cs = conv_w.shape[-1]
    red["conv_w"] = lax.dynamic_slice_in_dim(red["conv_w"], (2 * xi + yi) * cs, cs, axis=2)
    red.update(w_in=grad_w_in, w_kv=grad_w_kv, w_branch=grad_w_branch, w_out=grad_w_out)

    names = ["norm_g", "mem_norm_g", "w_in", "gmlp_ln_g", "gmlp_ln_b", "w_s", "b_s", "conv_w", "conv_b",
             "conv_ln_g", "conv_ln_b", "w_kv", "w_branch", "w_out", "final_norm_g"]
    weights = dict(norm_g=norm_g, mem_norm_g=mem_norm_g, w_in=w_in, gmlp_ln_g=gmlp_ln_g, gmlp_ln_b=gmlp_ln_b,
                   w_s=w_s, b_s=b_s, conv_w=conv_w, conv_b=conv_b, conv_ln_g=conv_ln_g, conv_ln_b=conv_ln_b,
                   w_kv=w_kv, w_branch=w_branch, w_out=w_out, final_norm_g=final_norm_g)
    ms = dict(norm_g=m_norm_g, mem_norm_g=m_mem_norm_g, w_in=m_w_in, gmlp_ln_g=m_gmlp_ln_g, gmlp_ln_b=m_gmlp_ln_b,
              w_s=m_w_s, b_s=m_b_s, conv_w=m_conv_w, conv_b=m_conv_b, conv_ln_g=m_conv_ln_g, conv_ln_b=m_conv_ln_b,
              w_kv=m_w_kv, w_branch=m_w_branch, w_out=m_w_out, final_norm_g=m_final_norm_g)
    vs = dict(norm_g=v_norm_g, mem_norm_g=v_mem_norm_g, w_in=v_w_in, gmlp_ln_g=v_gmlp_ln_g, gmlp_ln_b=v_gmlp_ln_b,
              w_s=v_w_s, b_s=v_b_s, conv_w=v_conv_w, conv_b=v_conv_b, conv_ln_g=v_conv_ln_g, conv_ln_b=v_conv_ln_b,
              w_kv=v_w_kv, w_branch=v_w_branch, w_out=v_w_out, final_norm_g=v_final_norm_g)
    grads, deltas, new_m, new_v = [], [], [], []
    for k in names:
        w2 = weights[k] if weights[k].ndim > 1 else weights[k][None, :]
        d, mo, vo, go = _adamw(w2, red[k].reshape(w2.shape), ms[k].reshape(w2.shape), vs[k].reshape(w2.shape))
        grads.append(go.reshape(weights[k].shape))
        deltas.append(d.reshape(weights[k].shape))
        new_m.append(mo.reshape(weights[k].shape))
        new_v.append(vo.reshape(weights[k].shape))
    return (loss, grad_x, *grads, *deltas, *new_m, *new_v)
```
